```python
import jax
import jax.numpy as jnp
from jax import lax
import numpy as np

D_MODEL = 1024
BATCH = 4
SEQ = 8192
DEPTH = 1

GRID_W = 64
CTX_LEN = 256
HEAD_DIM = 64
A_Q_HEADS = D_MODEL // (2 * HEAD_DIM)
A_KV_HEADS = A_Q_HEADS // 4
A_REP = A_Q_HEADS // A_KV_HEADS
A_WINDOW = 128
A_BLOCK = 128
A_SPAN = A_BLOCK + 2 * A_WINDOW
B_HEADS = D_MODEL // (2 * HEAD_DIM)
NA_KH = 8
NA_KW = 16
NA_QBW = 16
NA_CSPAN = NA_QBW + NA_KW
A_WIDTH = A_Q_HEADS * HEAD_DIM
A_KV_WIDTH = A_KV_HEADS * HEAD_DIM
B_WIDTH = B_HEADS * HEAD_DIM
MIX_WIDTH = A_WIDTH + B_WIDTH
IN_COLS = A_WIDTH + 2 * A_KV_WIDTH + 3 * B_WIDTH
ROPE_BASE = 10000.0
N_EXPERTS = 256
TOP_K = 8
N_GROUPS = 8
TOPK_GROUPS = 4
EXPERT_DIM = D_MODEL // 4
SHARED_DIM = D_MODEL // 4
ROUTED_SCALE = 2.5
DISPATCH_BLOCK = 128
EPS = 1e-6
NEG_INF = -1e30

kernel_name = 'hybrid_swa_natten_moe_flow_block'


def _rms_norm(x, gain):
    xf = x.astype(jnp.float32)
    y = xf * lax.rsqrt(jnp.mean(xf * xf, axis=-1, keepdims=True) + EPS)
    return (y * gain.astype(jnp.float32)).astype(x.dtype)


def _modulate(h, shift, scale):
    return h * (1 + scale) + shift


def _swiglu(x, w_gate, w_up, w_down):
    return (jax.nn.silu(x @ w_gate) * (x @ w_up)) @ w_down


def _axial_rope(x, row_pos, col_pos):
    quarter = HEAD_DIM // 4
    half = HEAD_DIM // 2
    freqs = ROPE_BASE ** (-jnp.arange(quarter, dtype=jnp.float32) / quarter)

    def rot(xs, pos):
        ang = pos[:, None] * freqs[None, :]
        cos = jnp.cos(ang)[:, None, :]
        sin = jnp.sin(ang)[:, None, :]
        x1 = xs[..., :quarter].astype(jnp.float32)
        x2 = xs[..., quarter:].astype(jnp.float32)
        return jnp.concatenate([x1 * cos - x2 * sin, x1 * sin + x2 * cos], axis=-1)

    out = jnp.concatenate([rot(x[..., :half], row_pos), rot(x[..., half:], col_pos)], axis=-1)
    return out.astype(x.dtype)


def _sink_softmax(s, sink):
    m = jnp.maximum(jnp.max(s, axis=-1, keepdims=True), sink)
    p = jnp.exp(s - m)
    return p / (jnp.sum(p, axis=-1, keepdims=True) + jnp.exp(sink - m))


def _project(h, w_in, q_norm_a, k_norm_a, q_norm_b, k_norm_b):
    z = h @ w_in
    cuts = [A_WIDTH, A_WIDTH + A_KV_WIDTH, A_WIDTH + 2 * A_KV_WIDTH,
            A_WIDTH + 2 * A_KV_WIDTH + B_WIDTH, A_WIDTH + 2 * A_KV_WIDTH + 2 * B_WIDTH]
    qa, ka, va, qb, kb, vb = jnp.split(z, cuts, axis=-1)

    def heads(u):
        return u.reshape(u.shape[:-1] + (-1, HEAD_DIM))

    return (_rms_norm(heads(qa), q_norm_a), _rms_norm(heads(ka), k_norm_a), heads(va),
            _rms_norm(heads(qb), q_norm_b), _rms_norm(heads(kb), k_norm_b), heads(vb))


def _windowed_gqa(q, k, v, k_ctx, v_ctx, sink):
    b, s = q.shape[0], q.shape[1]
    nb = s // A_BLOCK
    scale = HEAD_DIM ** -0.5
    qb = q.reshape(b, nb, A_BLOCK, A_KV_HEADS, A_REP, HEAD_DIM).transpose(1, 0, 2, 3, 4, 5)
    pad = ((0, 0), (A_WINDOW, A_WINDOW), (0, 0), (0, 0))
    kp = jnp.pad(k, pad)
    vp = jnp.pad(v, pad)
    sink_b = sink.astype(jnp.float32).reshape(1, A_KV_HEADS, A_REP, 1, 1)

    def block(args):
        bi, q_blk = args
        start = bi * A_BLOCK
        k_blk = lax.dynamic_slice_in_dim(kp, start, A_SPAN, axis=1)
        v_blk = lax.dynamic_slice_in_dim(vp, start, A_SPAN, axis=1)
        q_pos = start + jnp.arange(A_BLOCK)
        k_pos = start - A_WINDOW + jnp.arange(A_SPAN)
        mask = ((jnp.abs(q_pos[:, None] - k_pos[None, :]) <= A_WINDOW)
                & (k_pos >= 0)[None, :] & (k_pos < s)[None, :])
        s_loc = jnp.einsum('bqgrd,bkgd->bgrqk', q_blk, k_blk).astype(jnp.float32) * scale
        s_loc = jnp.where(mask, s_loc, NEG_INF)
        s_ctx = jnp.einsum('bqgrd,bkgd->bgrqk', q_blk, k_ctx).astype(jnp.float32) * scale
        p = _sink_softmax(jnp.concatenate([s_loc, s_ctx], axis=-1), sink_b)
        o = (jnp.einsum('bgrqk,bkgd->bqgrd', p[..., :A_SPAN], v_blk.astype(jnp.float32))
             + jnp.einsum('bgrqk,bkgd->bqgrd', p[..., A_SPAN:], v_ctx.astype(jnp.float32)))
        return o.reshape(b, A_BLOCK, A_WIDTH).astype(q.dtype)

    out = lax.map(block, (jnp.arange(nb), qb))
    return out.transpose(1, 0, 2, 3).reshape(b, s, A_WIDTH)


def _na_column_layout():
    ncb = GRID_W // NA_QBW
    n = np.arange(ncb)
    cs = np.clip(n * NA_QBW - NA_KW // 2, 0, GRID_W - NA_CSPAN)
    col_idx = cs[:, None] + np.arange(NA_CSPAN)[None, :]
    q_col = n[:, None] * NA_QBW + np.arange(NA_QBW)[None, :]
    ws = np.clip(q_col - NA_KW // 2, 0, GRID_W - NA_KW)[..., None]
    k_col = col_idx[:, None, :]
    col_mask = (k_col >= ws) & (k_col < ws + NA_KW)
    dc_idx = np.clip(k_col - q_col[..., None], -(NA_KW - 1), NA_KW - 1) + (NA_KW - 1)
    return col_idx, col_mask, dc_idx


def _neighbourhood_attention(q, k, v, k_ctx, v_ctx, rpb):
    b, s, h, d = q.shape
    rows = s // GRID_W
    kh = min(NA_KH, rows)
    ncb = GRID_W // NA_QBW
    col_idx, col_mask, dc_idx = _na_column_layout()
    scale = HEAD_DIM ** -0.5
    qg = q.reshape(b, rows, ncb, NA_QBW, h, d).transpose(1, 0, 2, 3, 4, 5)
    kg = k.reshape(b, rows, GRID_W, h, d)
    vg = v.reshape(b, rows, GRID_W, h, d)
    rpb = rpb.astype(jnp.float32)
    n_loc = kh * NA_CSPAN

    def row_block(args):
        r, q_row = args
        rs = jnp.clip(r - kh // 2, 0, rows - kh)
        k_reg = lax.dynamic_slice_in_dim(kg, rs, kh, axis=1)[:, :, col_idx]
        v_reg = lax.dynamic_slice_in_dim(vg, rs, kh, axis=1)[:, :, col_idx]
        s_loc = jnp.einsum('bnqhd,bknchd->bhnqkc', q_row, k_reg).astype(jnp.float32) * scale
        dr = rs + jnp.arange(kh) - r + (NA_KH - 1)
        bias = rpb[:, dr[:, None, None, None], dc_idx[None]].transpose(0, 2, 3, 1, 4)
        s_loc = jnp.where(col_mask[:, :, None, :], s_loc + bias[None], NEG_INF)
        s_loc = s_loc.reshape(b, h, ncb, NA_QBW, n_loc)
        s_ctx = jnp.einsum('bnqhd,blhd->bhnql', q_row, k_ctx).astype(jnp.float32) * scale
        p = jax.nn.softmax(jnp.concatenate([s_loc, s_ctx], axis=-1), axis=-1)
        v_loc = v_reg.transpose(0, 2, 1, 3, 4, 5).reshape(b, ncb, n_loc, h, d)
        o = (jnp.einsum('bhnqj,bnjhd->bnqhd', p[..., :n_loc], v_loc.astype(jnp.float32))
             + jnp.einsum('bhnql,blhd->bnqhd', p[..., n_loc:], v_ctx.astype(jnp.float32)))
        return o.reshape(b, GRID_W, h * d).astype(q.dtype)

    out = lax.map(row_block, (jnp.arange(rows), qg))
    return out.transpose(1, 0, 2, 3).reshape(b, s, h * d)


def _context_attention(q, k, v, sink):
    b, l, h, d = q.shape
    g = k.shape[2]
    r = h // g
    qg = q.reshape(b, l, g, r, d)
    s = jnp.einsum('bqgrd,bkgd->bgrqk', qg, k).astype(jnp.float32) * (HEAD_DIM ** -0.5)
    if sink is None:
        p = jax.nn.softmax(s, axis=-1)
    else:
        p = _sink_softmax(s, sink.astype(jnp.float32).reshape(1, g, r, 1, 1))
    o = jnp.einsum('bgrqk,bkgd->bqgrd', p, v.astype(jnp.float32))
    return o.reshape(b, l, h * d).astype(q.dtype)


def _merge(o_a, o_b, out_norm_a, out_norm_b, w_out):
    return jnp.concatenate([_rms_norm(o_a, out_norm_a), _rms_norm(o_b, out_norm_b)], axis=-1) @ w_out


def _moe(h, w_router, router_bias, we_gate, we_up, we_down, ws_gate, ws_up, ws_down):
    shape = h.shape
    hf = h.reshape(-1, shape[-1])
    n = hf.shape[0]
    scores = jax.nn.sigmoid((hf @ w_router).astype(jnp.float32))
    sel = scores + router_bias.astype(jnp.float32)
    grouped = sel.reshape(n, N_GROUPS, N_EXPERTS // N_GROUPS)
    group_score = jnp.sum(lax.top_k(grouped, 2)[0], axis=-1)
    _, top_groups = lax.top_k(group_score, TOPK_GROUPS)
    group_mask = jnp.sum(jax.nn.one_hot(top_groups, N_GROUPS, dtype=jnp.float32), axis=1) > 0
    expert_mask = jnp.repeat(group_mask, N_EXPERTS // N_GROUPS, axis=1)
    _, top_idx = lax.top_k(jnp.where(expert_mask, sel, NEG_INF), TOP_K)
    top_s = jnp.take_along_axis(scores, top_idx, axis=1)
    gates = top_s / jnp.sum(top_s, axis=-1, keepdims=True) * ROUTED_SCALE
    t = DISPATCH_BLOCK
    n_assign = n * TOP_K
    n_blocks = -(-n_assign // t) + N_EXPERTS
    e_flat = top_idx.reshape(-1)
    t_flat = jnp.repeat(jnp.arange(n, dtype=jnp.int32), TOP_K)
    g_flat = gates.reshape(-1)
    order = jnp.argsort(e_flat)
    e_sorted = e_flat[order]
    counts = jnp.bincount(e_flat, length=N_EXPERTS)
    starts = jnp.cumsum(counts) - counts
    padded = (counts + t - 1) // t * t
    pends = jnp.cumsum(padded)
    pstarts = pends - padded
    dest = pstarts[e_sorted] + jnp.arange(n_assign) - starts[e_sorted]
    tok_buf = jnp.zeros((n_blocks * t,), jnp.int32).at[dest].set(t_flat[order])
    gate_buf = jnp.zeros((n_blocks * t,), jnp.float32).at[dest].set(g_flat[order])
    block_expert = jnp.minimum(jnp.searchsorted(pends, jnp.arange(n_blocks) * t, side='right'),
                               N_EXPERTS - 1)

    def body(bi, acc):
        tok = lax.dynamic_slice_in_dim(tok_buf, bi * t, t)
        g = lax.dynamic_slice_in_dim(gate_buf, bi * t, t)
        e = block_expert[bi]
        y = _swiglu(hf[tok], we_gate[e], we_up[e], we_down[e])
        return acc.at[tok].add((y.astype(jnp.float32) * g[:, None]).astype(acc.dtype))

    routed = lax.fori_loop(0, n_blocks, body, jnp.zeros_like(hf))
    out = routed + _swiglu(hf, ws_gate, ws_up, ws_down)
    return out.reshape(shape)


def setup_inputs(seed: int = 0) -> dict:
    key = jax.random.key(seed)
    ks = jax.random.split(key, 26)

    def nrm(k, shape, scale):
        return jax.random.normal(k, shape, jnp.float32) * scale

    d = D_MODEL
    return {
        'x': nrm(ks[0], (BATCH, SEQ, d), 1.0),
        'c': nrm(ks[1], (BATCH, d), 1.0),
        'ctx': nrm(ks[2], (BATCH, CTX_LEN, d), 1.0),
        'c_ctx': nrm(ks[3], (d,), 1.0),
        'w_ada': nrm(ks[4], (DEPTH, d, 6 * d), 0.5 * d ** -0.5),
        'b_ada': nrm(ks[5], (DEPTH, 6 * d), 0.02),
        'norm1': 1.0 + nrm(ks[6], (DEPTH, d), 0.02),
        'norm2': 1.0 + nrm(ks[7], (DEPTH, d), 0.02),
        'w_in': nrm(ks[8], (DEPTH, d, IN_COLS), d ** -0.5),
        'q_norm_a': 1.0 + nrm(ks[9], (DEPTH, HEAD_DIM), 0.02),
        'k_norm_a': 1.0 + nrm(ks[10], (DEPTH, HEAD_DIM), 0.02),
        'q_norm_b': 1.0 + nrm(ks[11], (DEPTH, HEAD_DIM), 0.02),
        'k_norm_b': 1.0 + nrm(ks[12], (DEPTH, HEAD_DIM), 0.02),
        'sink_a': nrm(ks[13], (DEPTH, A_Q_HEADS), 0.5),
        'rpb_b': nrm(ks[14], (DEPTH, B_HEADS, 2 * NA_KH - 1, 2 * NA_KW - 1), 0.1),
        'out_norm_a': 1.0 + nrm(ks[15], (DEPTH, A_WIDTH), 0.02),
        'out_norm_b': 1.0 + nrm(ks[16], (DEPTH, B_WIDTH), 0.02),
        'w_out': nrm(ks[17], (DEPTH, MIX_WIDTH, d), MIX_WIDTH ** -0.5),
        'w_router': nrm(ks[18], (DEPTH, d, N_EXPERTS), d ** -0.5),
        'router_bias': nrm(ks[19], (DEPTH, N_EXPERTS), 0.01),
        'we_gate': nrm(ks[20], (DEPTH, N_EXPERTS, d, EXPERT_DIM), d ** -0.5),
        'we_up': nrm(ks[21], (DEPTH, N_EXPERTS, d, EXPERT_DIM), d ** -0.5),
        'we_down': nrm(ks[22], (DEPTH, N_EXPERTS, EXPERT_DIM, d), EXPERT_DIM ** -0.5),
        'ws_gate': nrm(ks[23], (DEPTH, d, SHARED_DIM), d ** -0.5),
        'ws_up': nrm(ks[24], (DEPTH, d, SHARED_DIM), d ** -0.5),
        'ws_down': nrm(ks[25], (DEPTH, SHARED_DIM, d), SHARED_DIM ** -0.5),
    }


def reference(x, c, ctx, c_ctx, w_ada, b_ada, norm1, norm2, w_in, q_norm_a, k_norm_a, q_norm_b,
              k_norm_b, sink_a, rpb_b, out_norm_a, out_norm_b, w_out, w_router, router_bias,
              we_gate, we_up, we_down, ws_gate, ws_up, ws_down):
    s = x.shape[1]
    t = jnp.arange(s)
    row_pos = (t // GRID_W).astype(jnp.float32)
    col_pos = (t % GRID_W).astype(jnp.float32)
    for i in range(DEPTH):
        mod = jax.nn.silu(c) @ w_ada[i] + b_ada[i]
        sh1, sc1, g1, sh2, sc2, g2 = [m[:, None, :] for m in jnp.split(mod, 6, axis=-1)]
        mod_c = jax.nn.silu(c_ctx) @ w_ada[i] + b_ada[i]
        sh1c, sc1c, g1c, sh2c, sc2c, g2c = jnp.split(mod_c, 6, axis=-1)

        h = _modulate(_rms_norm(x, norm1[i]), sh1, sc1)
        hc = _modulate(_rms_norm(ctx, norm1[i]), sh1c, sc1c)
        qa, ka, va, qb, kb, vb = _project(h, w_in[i], q_norm_a[i], k_norm_a[i], q_norm_b[i], k_norm_b[i])
        qa_c, ka_c, va_c, qb_c, kb_c, vb_c = _project(hc, w_in[i], q_norm_a[i], k_norm_a[i],
                                                      q_norm_b[i], k_norm_b[i])
        qa = _axial_rope(qa, row_pos, col_pos)
        ka = _axial_rope(ka, row_pos, col_pos)
        o_a = _windowed_gqa(qa, ka, va, ka_c, va_c, sink_a[i])
        o_b = _neighbourhood_attention(qb, kb, vb, kb_c, vb_c, rpb_b[i])
        x = x + g1 * _merge(o_a, o_b, out_norm_a[i], out_norm_b[i], w_out[i])

        h2 = _modulate(_rms_norm(x, norm2[i]), sh2, sc2)
        x = x + g2 * _moe(h2, w_router[i], router_bias[i], we_gate[i], we_up[i], we_down[i],
                          ws_gate[i], ws_up[i], ws_down[i])

        if i < DEPTH - 1:
            oc_a = _context_attention(qa_c, ka_c, va_c, sink_a[i])
            oc_b = _context_attention(qb_c, kb_c, vb_c, None)
            ctx = ctx + g1c * _merge(oc_a, oc_b, out_norm_a[i], out_norm_b[i], w_out[i])
            hc2 = _modulate(_rms_norm(ctx, norm2[i]), sh2c, sc2c)
            ctx = ctx + g2c * _moe(hc2, w_router[i], router_bias[i], we_gate[i], we_up[i], we_down[i],
                                   ws_gate[i], ws_up[i], ws_down[i])
    return x
```

```python
import functools

import numpy as np
import jax
import jax.numpy as jnp
from jax import lax
from jax.experimental import pallas as pl
from jax.experimental.pallas import tpu as pltpu

F32 = jnp.float32
BF16 = jnp.bfloat16
I32 = jnp.int32
U32 = jnp.uint32

LANES = 128
HEAD_DIM = 64
HEAD_PAIR = 2 * HEAD_DIM
GRID_W = 64
A_Q_HEADS = 8
A_KV_HEADS = 2
A_WINDOW = 128
B_HEADS = 8
NA_KH = 8
NA_KW = 16
NA_QROWS = 4
NA_KROWS = NA_QROWS + NA_KH
ROPE_BASE = 10000.0
N_EXPERTS = 256
TOP_K = 8
N_GROUPS = 8
TOPK_GROUPS = 4
ROUTED_SCALE = 2.5
EPS = 1e-6
NEG_INF = -1e30
EXPERT_ROWS = 256
VMEM_LIMIT = 56 * 1024 * 1024

_NT = (((1,), (1,)), ((), ()))


def _params(*sem):
    return pltpu.CompilerParams(dimension_semantics=sem, vmem_limit_bytes=VMEM_LIMIT)


def _silu(v):
    return v * jax.nn.sigmoid(v)


def _rms(v, gain):
    return v * lax.rsqrt(jnp.mean(v * v, axis=-1, keepdims=True) + EPS) * gain


def _pack_bf16_pairs(v):
    n = v.shape[1] // 2
    lo = lax.bitcast_convert_type(v[:, :n].astype(BF16).astype(F32), U32) >> 16
    hi = lax.bitcast_convert_type(v[:, n:].astype(BF16).astype(F32), U32) & jnp.uint32(0xFFFF0000)
    return hi | lo


def _unpack_bf16_pairs(w):
    lo = lax.bitcast_convert_type(w << 16, F32)
    hi = lax.bitcast_convert_type(w & jnp.uint32(0xFFFF0000), F32)
    return jnp.concatenate([lo, hi], axis=1)


def _ada_kernel(c_ref, w_ref, b_ref, o_ref):
    a = _silu(c_ref[...])
    o_ref[...] = jnp.dot(a, w_ref[...], preferred_element_type=F32,
                         precision=lax.Precision.HIGHEST) + b_ref[...]


def _ada(c8, w, b):
    d, n = w.shape
    bn = n // 4
    return pl.pallas_call(
        _ada_kernel,
        out_shape=jax.ShapeDtypeStruct((8, n), F32),
        grid=(n // bn,),
        in_specs=[pl.BlockSpec((8, d), lambda j: (0, 0)),
                  pl.BlockSpec((d, bn), lambda j: (0, j)),
                  pl.BlockSpec((1, bn), lambda j: (0, j))],
        out_specs=pl.BlockSpec((8, bn), lambda j: (0, j)),
        compiler_params=_params("arbitrary"),
        name="ada",
    )(c8, w, b)


_QA, _QB, _KB, _VB, _KA, _VA, _QA_SW, _KA_SW = 0, 4, 8, 12, 16, 17, 18, 22
_OUT_BLOCKS = 18
_EXT_BLOCKS = 23


def _inproj_kernel(x_ref, mod_ref, n1_ref, w_ref, g_ref, cos_ref, sin_ref, bd_ref, o_ref, *, mod_row, d):
    b = pl.program_id(0) if mod_row is None else mod_row
    xn = _rms(x_ref[0], n1_ref[...])
    sh = mod_ref[pl.ds(b, 1), 0:d]
    sc = mod_ref[pl.ds(b, 1), d:2 * d]
    h = (xn * (1.0 + sc) + sh).astype(BF16)
    z = jnp.dot(h, w_ref[...], preferred_element_type=F32)
    bd = bd_ref[...]
    cos = cos_ref[...]
    sin = sin_ref[...]

    def blk(j):
        return z[:, j * LANES:(j + 1) * LANES]

    def head_rinv(zb):
        ms = jnp.dot((zb * zb).astype(BF16), bd, preferred_element_type=F32)
        return lax.rsqrt(ms + EPS)

    def put(j, v):
        o_ref[0, :, j * LANES:(j + 1) * LANES] = v.astype(BF16)

    def roped(j, j_sw, g_row, g_sw_row):
        zb = blk(j)
        r = head_rinv(zb)
        put(j, (zb * r * g_ref[g_row:g_row + 1, :]) * cos + (blk(j_sw) * r * g_ref[g_sw_row:g_sw_row + 1, :]) * sin)

    def normed(j, g_row):
        zb = blk(j)
        put(j, zb * head_rinv(zb) * g_ref[g_row:g_row + 1, :])

    for j in range(4):
        roped(_QA + j, _QA_SW + j, 0, 4)
        normed(_QB + j, 1)
        normed(_KB + j, 2)
        put(_VB + j, blk(_VB + j))
    roped(_KA, _KA_SW, 3, 5)
    put(_VA, blk(_VA))


def _inproj(x, mod, n1, w_ext, gains, cos_t, sin_t, bd, *, mod_row, tm):
    bsz, s, d = x.shape
    kern = functools.partial(_inproj_kernel, mod_row=mod_row, d=d)
    return pl.pallas_call(
        kern,
        out_shape=jax.ShapeDtypeStruct((bsz, s, _OUT_BLOCKS * LANES), BF16),
        grid=(bsz, s // tm),
        in_specs=[pl.BlockSpec((1, tm, d), lambda b, i: (b, i, 0)),
                  pl.BlockSpec(mod.shape, lambda b, i: (0, 0)),
                  pl.BlockSpec((1, d), lambda b, i: (0, 0)),
                  pl.BlockSpec(w_ext.shape, lambda b, i: (0, 0)),
                  pl.BlockSpec(gains.shape, lambda b, i: (0, 0)),
                  pl.BlockSpec((tm, LANES), lambda b, i: (i, 0)),
                  pl.BlockSpec((tm, LANES), lambda b, i: (i, 0)),
                  pl.BlockSpec((LANES, LANES), lambda b, i: (0, 0))],
        out_specs=pl.BlockSpec((1, tm, _OUT_BLOCKS * LANES), lambda b, i: (b, i, 0)),
        compiler_params=_params("arbitrary", "arbitrary"),
        name="inproj",
    )(x, mod, n1, w_ext, gains, cos_t, sin_t, bd)


def _split_pair(qp, lo):
    zero = jnp.zeros_like(qp)
    return jnp.concatenate([jnp.where(lo, qp, zero), jnp.where(lo, zero, qp)], axis=0)


def _softmax_pv(s_loc, s_ctx, v_loc, v_ctx, sink):
    m = jnp.maximum(jnp.max(s_loc, axis=-1, keepdims=True), jnp.max(s_ctx, axis=-1, keepdims=True))
    if sink is not None:
        m = jnp.maximum(m, sink)
    p_loc = jnp.exp(s_loc - m)
    p_ctx = jnp.exp(s_ctx - m)
    l = jnp.sum(p_loc, axis=-1, keepdims=True) + jnp.sum(p_ctx, axis=-1, keepdims=True)
    if sink is not None:
        l = l + jnp.exp(sink - m)
    o = (jnp.dot(p_loc.astype(BF16), v_loc, preferred_element_type=F32)
         + jnp.dot(p_ctx.astype(BF16), v_ctx, preferred_element_type=F32))
    return o * (1.0 / l)


def _attn_a_kernel(q_ref, kp_ref, kc_ref, kn_ref, vp_ref, vc_ref, vn_ref, kx_ref, vx_ref, sink_ref, mask_ref,
                   o_ref, *, nblk):
    j = pl.program_id(1)
    tq = q_ref.shape[1]
    lo = lax.broadcasted_iota(I32, (tq, LANES), 1) < HEAD_DIM
    q = q_ref[0]
    qs = jnp.concatenate([_split_pair(q[:, p * LANES:(p + 1) * LANES], lo) for p in range(4)], axis=0)
    k = jnp.concatenate([kp_ref[0], kc_ref[0], kn_ref[0]], axis=0)
    v = jnp.concatenate([vp_ref[0], vc_ref[0], vn_ref[0]], axis=0)
    col = lax.broadcasted_iota(I32, (1, 3 * tq), 1)
    edge_ok = ((col >= tq) | (j > 0)) & ((col < 2 * tq) | (j < nblk - 1))
    edge = jnp.where(edge_ok, 0.0, NEG_INF)
    s_loc = lax.dot_general(qs, k, _NT, preferred_element_type=F32) + mask_ref[...] + edge
    s_ctx = lax.dot_general(qs, kx_ref[0], _NT, preferred_element_type=F32)
    o = _softmax_pv(s_loc, s_ctx, v, vx_ref[0], sink_ref[...])
    for p in range(4):
        o_lo = o[(2 * p) * tq:(2 * p + 1) * tq]
        o_hi = o[(2 * p + 1) * tq:(2 * p + 2) * tq]
        o_ref[0, :, p * LANES:(p + 1) * LANES] = jnp.where(lo, o_lo, o_hi).astype(BF16)


def _attn_a(zq, zc, sink_col, mask):
    bsz, s, _ = zq.shape
    lc = zc.shape[1]
    tq = A_WINDOW
    nblk = s // tq
    ka, va = _KA, _VA

    def kv_spec(col, shift):
        return pl.BlockSpec((1, tq, LANES), lambda b, j: (b, jnp.clip(j + shift, 0, nblk - 1), col))

    return pl.pallas_call(
        functools.partial(_attn_a_kernel, nblk=nblk),
        out_shape=jax.ShapeDtypeStruct((bsz, s, 4 * LANES), BF16),
        grid=(bsz, nblk),
        in_specs=[pl.BlockSpec((1, tq, 4 * LANES), lambda b, j: (b, j, 0)),
                  kv_spec(ka, -1), kv_spec(ka, 0), kv_spec(ka, 1),
                  kv_spec(va, -1), kv_spec(va, 0), kv_spec(va, 1),
                  pl.BlockSpec((1, lc, LANES), lambda b, j: (b, 0, ka)),
                  pl.BlockSpec((1, lc, LANES), lambda b, j: (b, 0, va)),
                  pl.BlockSpec(sink_col.shape, lambda b, j: (0, 0)),
                  pl.BlockSpec(mask.shape, lambda b, j: (0, 0))],
        out_specs=pl.BlockSpec((1, tq, 4 * LANES), lambda b, j: (b, j, 0)),
        compiler_params=_params("arbitrary", "arbitrary"),
        name="attn_a",
    )(zq, zq, zq, zq, zq, zq, zq, zc, zc, sink_col, mask)


def _attn_a_mask(tq):
    qi = np.arange(tq)[:, None]
    kj = np.arange(3 * tq)[None, :]
    ok = (kj >= qi) & (kj <= qi + 2 * tq)
    m = np.where(ok, 0.0, NEG_INF).astype(np.float32)
    return jnp.asarray(np.tile(m, (A_Q_HEADS, 1)))


def _attn_b_kernel(q_ref, k0_ref, k1_ref, k2_ref, v0_ref, v1_ref, v2_ref, kx_ref, vx_ref, tab_ref, o_ref):
    tq = q_ref.shape[1]
    lo = lax.broadcasted_iota(I32, (tq, LANES), 1) < HEAD_DIM
    for p in range(4):
        sl = slice(p * LANES, (p + 1) * LANES)
        qs = _split_pair(q_ref[0, :, sl], lo)
        k = jnp.concatenate([k0_ref[0, :, sl], k1_ref[0, :, sl], k2_ref[0, :, sl]], axis=0)
        v = jnp.concatenate([v0_ref[0, :, sl], v1_ref[0, :, sl], v2_ref[0, :, sl]], axis=0)
        s_loc = lax.dot_general(qs, k, _NT, preferred_element_type=F32) + tab_ref[0, p]
        s_ctx = lax.dot_general(qs, kx_ref[0, :, sl], _NT, preferred_element_type=F32)
        o = _softmax_pv(s_loc, s_ctx, v, vx_ref[0, :, sl], None)
        o_ref[0, :, sl] = jnp.where(lo, o[:tq], o[tq:]).astype(BF16)


def _attn_b(zq, zc, table):
    bsz, s, _ = zq.shape
    lc = zc.shape[1]
    tq = NA_QROWS * GRID_W
    ng = s // tq
    qb, kb, vb = _QB // 4, _KB // 4, _VB // 4

    def kv_spec(col, off):
        return pl.BlockSpec((1, tq, 4 * LANES), lambda i, b: (b, jnp.clip(i - 1, 0, ng - 3) + off, col))

    def variant(i):
        return jnp.where(i == 0, 0, jnp.where(i == ng - 1, 2, 1))

    return pl.pallas_call(
        _attn_b_kernel,
        out_shape=jax.ShapeDtypeStruct((bsz, s, 4 * LANES), BF16),
        grid=(ng, bsz),
        in_specs=[pl.BlockSpec((1, tq, 4 * LANES), lambda i, b: (b, i, qb)),
                  kv_spec(kb, 0), kv_spec(kb, 1), kv_spec(kb, 2),
                  kv_spec(vb, 0), kv_spec(vb, 1), kv_spec(vb, 2),
                  pl.BlockSpec((1, lc, 4 * LANES), lambda i, b: (b, 0, kb)),
                  pl.BlockSpec((1, lc, 4 * LANES), lambda i, b: (b, 0, vb)),
                  pl.BlockSpec((1,) + table.shape[1:], lambda i, b: (variant(i), 0, 0, 0))],
        out_specs=pl.BlockSpec((1, tq, 4 * LANES), lambda i, b: (b, i, 0)),
        compiler_params=_params("arbitrary", "arbitrary"),
        name="attn_b",
    )(zq, zq, zq, zq, zq, zq, zq, zc, zc, table)


def _na_table(rpb, rows):
    ng = rows // NA_QROWS
    qc = np.arange(GRID_W)
    kc = np.arange(GRID_W)
    ws = np.clip(qc - NA_KW // 2, 0, GRID_W - NA_KW)
    valid_c = (kc[None, :] >= ws[:, None]) & (kc[None, :] < ws[:, None] + NA_KW)
    dc = np.clip(kc[None, :] - qc[:, None], -(NA_KW - 1), NA_KW - 1) + (NA_KW - 1)
    tabs = []
    for i in (0, 1, ng - 1):
        start = int(np.clip(NA_QROWS * i - NA_KH // 2, 0, rows - NA_KROWS))
        r = NA_QROWS * i + np.arange(NA_QROWS)
        rs = np.clip(r - NA_KH // 2, 0, rows - NA_KH)
        krow = start + np.arange(NA_KROWS)
        valid_r = (krow[None, :] >= rs[:, None]) & (krow[None, :] < rs[:, None] + NA_KH)
        dr = np.clip(krow[None, :] - r[:, None] + (NA_KH - 1), 0, 2 * NA_KH - 2)
        valid = valid_r[:, None, :, None] & valid_c[None, :, None, :]
        shape = valid.shape
        dr_f = np.broadcast_to(dr[:, None, :, None], shape).reshape(NA_QROWS * GRID_W, NA_KROWS * GRID_W)
        dc_f = np.broadcast_to(dc[None, :, None, :], shape).reshape(NA_QROWS * GRID_W, NA_KROWS * GRID_W)
        valid_f = valid.reshape(NA_QROWS * GRID_W, NA_KROWS * GRID_W)
        bias = rpb.astype(F32)[:, dr_f, dc_f]
        tabs.append(jnp.where(jnp.asarray(valid_f)[None], bias, NEG_INF))
    tab = jnp.stack(tabs)
    nq, nk = NA_QROWS * GRID_W, NA_KROWS * GRID_W
    return tab.reshape(3, B_HEADS // 2, 2 * nq, nk)


def _merge_kernel(oa_ref, ob_ref, x_ref, mod_ref, ga_ref, gb_ref, wo_ref, n2_ref, wr_ref, rb_ref,
                  wsg_ref, wsu_ref, wsd_ref, xres_ref, hp_ref, idx_ref, gate_ref, *, d):
    b = pl.program_id(0)
    tm = x_ref.shape[1]

    def mod(k):
        return mod_ref[pl.ds(b, 1), k * d:(k + 1) * d]

    na = _rms(oa_ref[0].astype(F32), ga_ref[...])
    nb = _rms(ob_ref[0].astype(F32), gb_ref[...])
    cat = jnp.concatenate([na, nb], axis=1).astype(BF16)
    y = jnp.dot(cat, wo_ref[...], preferred_element_type=F32)
    x1 = x_ref[0] + mod(2) * y
    h2 = _rms(x1, n2_ref[...]) * (1.0 + mod(4)) + mod(3)

    hb = h2.astype(BF16)
    act = _silu(jnp.dot(hb, wsg_ref[...], preferred_element_type=F32)) * jnp.dot(hb, wsu_ref[...],
                                                                                 preferred_element_type=F32)
    shared = jnp.dot(act.astype(BF16), wsd_ref[...], preferred_element_type=F32)
    xres_ref[0] = x1 + mod(5) * shared
    hp_ref[...] = _pack_bf16_pairs(h2)

    logits = lax.dot_general(wr_ref[...], h2, _NT, preferred_element_type=F32, precision=lax.Precision.HIGHEST)
    scores = jax.nn.sigmoid(logits)
    sel = scores + rb_ref[...]
    per = N_EXPERTS // N_GROUPS
    g3 = sel.reshape(N_GROUPS, per, tm)
    it3 = lax.broadcasted_iota(I32, (N_GROUPS, per, tm), 1)
    m1 = jnp.max(g3, axis=1, keepdims=True)
    first = jnp.min(jnp.where(g3 == m1, it3, per), axis=1, keepdims=True)
    m2 = jnp.max(jnp.where(it3 == first, -jnp.inf, g3), axis=1, keepdims=True)
    gscore = (m1 + m2).reshape(N_GROUPS, tm)

    itg = lax.broadcasted_iota(I32, (N_GROUPS, tm), 0)
    gsel = jnp.zeros((N_GROUPS, tm), F32)
    cur = gscore
    for _ in range(TOPK_GROUPS):
        mx = jnp.max(cur, axis=0, keepdims=True)
        fi = jnp.min(jnp.where(cur == mx, itg, N_GROUPS), axis=0, keepdims=True)
        pick = itg == fi
        gsel = jnp.where(pick, 1.0, gsel)
        cur = jnp.where(pick, -jnp.inf, cur)
    emask = jnp.broadcast_to(gsel.reshape(N_GROUPS, 1, tm), (N_GROUPS, per, tm)).reshape(N_EXPERTS, tm) > 0.5

    ite = lax.broadcasted_iota(I32, (N_EXPERTS, tm), 0)
    cur = jnp.where(emask, sel, NEG_INF)
    idx_rows, s_rows = [], []
    for _ in range(TOP_K):
        mx = jnp.max(cur, axis=0, keepdims=True)
        fi = jnp.min(jnp.where(cur == mx, ite, N_EXPERTS), axis=0, keepdims=True)
        pick = ite == fi
        idx_rows.append(fi)
        s_rows.append(jnp.sum(jnp.where(pick, scores, 0.0), axis=0, keepdims=True))
        cur = jnp.where(pick, -jnp.inf, cur)
    top_s = jnp.concatenate(s_rows, axis=0)
    idx_ref[...] = jnp.concatenate(idx_rows, axis=0)
    gate_ref[...] = top_s / jnp.sum(top_s, axis=0, keepdims=True) * ROUTED_SCALE


def _merge(o_a, o_b, x, mod, ga, gb, wo, n2, wr_t, rb_col, wsg, wsu, wsd, *, tm):
    bsz, s, d = x.shape
    nt = s // tm
    n = bsz * s
    full = lambda a: pl.BlockSpec(a.shape, lambda b, i: (0,) * a.ndim)
    return pl.pallas_call(
        functools.partial(_merge_kernel, d=d),
        out_shape=(jax.ShapeDtypeStruct((bsz, s, d), F32),
                   jax.ShapeDtypeStruct((n, d // 2), U32),
                   jax.ShapeDtypeStruct((TOP_K, n), I32),
                   jax.ShapeDtypeStruct((TOP_K, n), F32)),
        grid=(bsz, nt),
        in_specs=[pl.BlockSpec((1, tm, d // 2), lambda b, i: (b, i, 0)),
                  pl.BlockSpec((1, tm, d // 2), lambda b, i: (b, i, 0)),
                  pl.BlockSpec((1, tm, d), lambda b, i: (b, i, 0)),
                  full(mod), full(ga), full(gb), full(wo), full(n2), full(wr_t), full(rb_col),
                  full(wsg), full(wsu), full(wsd)],
        out_specs=(pl.BlockSpec((1, tm, d), lambda b, i: (b, i, 0)),
                   pl.BlockSpec((tm, d // 2), lambda b, i: (b * nt + i, 0)),
                   pl.BlockSpec((TOP_K, tm), lambda b, i: (0, b * nt + i)),
                   pl.BlockSpec((TOP_K, tm), lambda b, i: (0, b * nt + i))),
        compiler_params=_params("arbitrary", "arbitrary"),
        name="merge",
    )(o_a, o_b, x, mod, ga, gb, wo, n2, wr_t, rb_col, wsg, wsu, wsd)


def _rank_kernel(idx_ref, rank_ref, cnt_ref, carry_ref):
    tm = idx_ref.shape[1]

    @pl.when(pl.program_id(0) == 0)
    def _():
        carry_ref[...] = jnp.zeros_like(carry_ref)

    idx = idx_ref[...]
    ite = lax.broadcasted_iota(I32, (N_EXPERTS, tm), 0)
    before = (lax.broadcasted_iota(I32, (tm, tm), 0) < lax.broadcasted_iota(I32, (tm, tm), 1)).astype(BF16)
    base = carry_ref[...]
    rows = []
    for k in range(TOP_K):
        oh = ite == idx[k:k + 1, :]
        ohf = jnp.where(oh, 1.0, 0.0)
        cum = jnp.dot(ohf.astype(BF16), before, preferred_element_type=F32)
        rows.append(jnp.sum(jnp.where(oh, cum + base, 0.0), axis=0, keepdims=True))
        base = base + jnp.sum(ohf, axis=1, keepdims=True)
    rank_ref[...] = jnp.concatenate(rows, axis=0).astype(I32)
    carry_ref[...] = base
    cnt_ref[...] = base


def _rank(idx, *, tm):
    n = idx.shape[1]
    return pl.pallas_call(
        _rank_kernel,
        out_shape=(jax.ShapeDtypeStruct((TOP_K, n), I32), jax.ShapeDtypeStruct((N_EXPERTS, 1), F32)),
        grid=(n // tm,),
        in_specs=[pl.BlockSpec((TOP_K, tm), lambda i: (0, i))],
        out_specs=(pl.BlockSpec((TOP_K, tm), lambda i: (0, i)), pl.BlockSpec((N_EXPERTS, 1), lambda i: (0, 0))),
        scratch_shapes=[pltpu.VMEM((N_EXPERTS, 1), F32)],
        compiler_params=_params("arbitrary"),
        name="rank",
    )(idx)


def _pos_kernel(idx_ref, rank_ref, pstart_ref, pos_ref):
    tm = idx_ref.shape[1]
    idx = idx_ref[...]
    ite = lax.broadcasted_iota(I32, (N_EXPERTS, tm), 0)
    pstart = pstart_ref[...]
    rows = [jnp.sum(jnp.where(ite == idx[k:k + 1, :], pstart, 0.0), axis=0, keepdims=True) for k in range(TOP_K)]
    pos_ref[...] = jnp.concatenate(rows, axis=0).astype(I32) + rank_ref[...]


def _pos(idx, rank, pstart_col, *, tm):
    n = idx.shape[1]
    return pl.pallas_call(
        _pos_kernel,
        out_shape=jax.ShapeDtypeStruct((TOP_K, n), I32),
        grid=(n // tm,),
        in_specs=[pl.BlockSpec((TOP_K, tm), lambda i: (0, i)),
                  pl.BlockSpec((TOP_K, tm), lambda i: (0, i)),
                  pl.BlockSpec((N_EXPERTS, 1), lambda i: (0, 0))],
        out_specs=pl.BlockSpec((TOP_K, tm), lambda i: (0, i)),
        compiler_params=_params("arbitrary"),
        name="pos",
    )(idx, rank, pstart_col)


def _dispatch_kernel(pos_ref, h_ref, xs_in_ref, xs_ref, sem):
    del xs_in_ref
    tm = h_ref.shape[0]

    def row_copy(t, p):
        return pltpu.make_async_copy(h_ref.at[pl.ds(t, 1)], xs_ref.at[pl.ds(p, 1)], sem)

    def issue(t, c):
        for k in range(TOP_K):
            row_copy(t, pos_ref[k, t]).start()
        return c

    def drain(t, c):
        for k in range(TOP_K):
            row_copy(t, pos_ref[k, t]).wait()
        return c

    lax.fori_loop(0, tm, issue, 0)
    lax.fori_loop(0, tm, drain, 0)


def _dispatch(pos, hp, xs_init, *, tm):
    n, w = hp.shape
    return pl.pallas_call(
        _dispatch_kernel,
        out_shape=jax.ShapeDtypeStruct(xs_init.shape, U32),
        grid=(n // tm,),
        in_specs=[pl.BlockSpec((TOP_K, tm), lambda i: (0, i), memory_space=pltpu.SMEM),
                  pl.BlockSpec((tm, w), lambda i: (i, 0)),
                  pl.BlockSpec(memory_space=pl.ANY)],
        out_specs=pl.BlockSpec(memory_space=pl.ANY),
        scratch_shapes=[pltpu.SemaphoreType.DMA(())],
        input_output_aliases={2: 0},
        compiler_params=_params("arbitrary"),
        name="dispatch",
    )(pos, hp, xs_init)


def _expert_kernel(be_ref, nu_ref, xs_ref, wg_ref, wu_ref, wd_ref, ys_ref):
    del be_ref
    i = pl.program_id(0)

    @pl.when(i < nu_ref[0])
    def _():
        x = _unpack_bf16_pairs(xs_ref[...]).astype(BF16)
        g = jnp.dot(x, wg_ref[0].astype(BF16), preferred_element_type=F32)
        u = jnp.dot(x, wu_ref[0].astype(BF16), preferred_element_type=F32)
        y = jnp.dot((_silu(g) * u).astype(BF16), wd_ref[0].astype(BF16), preferred_element_type=F32)
        ys_ref[...] = _pack_bf16_pairs(y)

    @pl.when(i >= nu_ref[0])
    def _():
        ys_ref[...] = jnp.zeros_like(ys_ref)


def _experts(block_expert, n_used, xs, wg, wu, wd):
    rows, w = xs.shape
    t = EXPERT_ROWS
    d, f = wg.shape[1], wg.shape[2]
    grid_spec = pltpu.PrefetchScalarGridSpec(
        num_scalar_prefetch=2,
        grid=(rows // t,),
        in_specs=[pl.BlockSpec((t, w), lambda i, be, nu: (i, 0)),
                  pl.BlockSpec((1, d, f), lambda i, be, nu: (be[i], 0, 0)),
                  pl.BlockSpec((1, d, f), lambda i, be, nu: (be[i], 0, 0)),
                  pl.BlockSpec((1, f, d), lambda i, be, nu: (be[i], 0, 0))],
        out_specs=pl.BlockSpec((t, w), lambda i, be, nu: (i, 0)),
    )
    return pl.pallas_call(
        _expert_kernel,
        out_shape=jax.ShapeDtypeStruct((rows, w), U32),
        grid_spec=grid_spec,
        compiler_params=_params("arbitrary"),
        name="experts",
    )(block_expert, n_used, xs, wg, wu, wd)


def _combine_kernel(pos_ref, gate_ref, xres_ref, mod_ref, ys_ref, o_ref, buf_ref, sem, *, d, tiles_per_batch):
    tm = gate_ref.shape[1]
    b = pl.program_id(0) // tiles_per_batch

    def row_copy(k, t, p):
        return pltpu.make_async_copy(ys_ref.at[pl.ds(p, 1)], buf_ref.at[k, pl.ds(t, 1)], sem)

    def issue(t, c):
        for k in range(TOP_K):
            row_copy(k, t, pos_ref[k, t]).start()
        return c

    def drain(t, c):
        for k in range(TOP_K):
            row_copy(k, t, pos_ref[k, t]).wait()
        return c

    lax.fori_loop(0, tm, issue, 0)
    lax.fori_loop(0, tm, drain, 0)

    gates = gate_ref[...].T
    acc = jnp.zeros((tm, d), F32)
    for k in range(TOP_K):
        acc = acc + _unpack_bf16_pairs(buf_ref[k]) * gates[:, k:k + 1]
    o_ref[...] = xres_ref[...] + mod_ref[pl.ds(b, 1), 5 * d:6 * d] * acc


def _combine(pos, gates, xres, mod, ys, *, tm, tiles_per_batch):
    n, d = xres.shape
    return pl.pallas_call(
        functools.partial(_combine_kernel, d=d, tiles_per_batch=tiles_per_batch),
        out_shape=jax.ShapeDtypeStruct((n, d), F32),
        grid=(n // tm,),
        in_specs=[pl.BlockSpec((TOP_K, tm), lambda i: (0, i), memory_space=pltpu.SMEM),
                  pl.BlockSpec((TOP_K, tm), lambda i: (0, i)),
                  pl.BlockSpec((tm, d), lambda i: (i, 0)),
                  pl.BlockSpec(mod.shape, lambda i: (0, 0)),
                  pl.BlockSpec(memory_space=pl.ANY)],
        out_specs=pl.BlockSpec((tm, d), lambda i: (i, 0)),
        scratch_shapes=[pltpu.VMEM((TOP_K, tm, d // 2), U32), pltpu.SemaphoreType.DMA(())],
        compiler_params=_params("arbitrary"),
        name="combine",
    )(pos, gates, xres, mod, ys)


_QA_HEAD_ORDER = (0, 4, 1, 5, 2, 6, 3, 7)


def _rope_swap_index():
    quarter = HEAD_DIM // 4
    dd = np.arange(HEAD_DIM)
    return np.where((dd % (2 * quarter)) < quarter, dd + quarter, dd - quarter)


def _rope_tables(s):
    quarter = HEAD_DIM // 4
    t = jnp.arange(s)
    row = (t // GRID_W).astype(F32)
    col = (t % GRID_W).astype(F32)
    freqs = ROPE_BASE ** (-jnp.arange(quarter, dtype=F32) / quarter)
    ar = row[:, None] * freqs[None, :]
    ac = col[:, None] * freqs[None, :]
    cos = jnp.concatenate([jnp.cos(ar), jnp.cos(ar), jnp.cos(ac), jnp.cos(ac)], axis=1)
    sin = jnp.concatenate([-jnp.sin(ar), jnp.sin(ar), -jnp.sin(ac), jnp.sin(ac)], axis=1)
    return jnp.tile(cos, (1, 2)), jnp.tile(sin, (1, 2))


def kernel(x, c, ctx, c_ctx, w_ada, b_ada, norm1, norm2, w_in, q_norm_a, k_norm_a, q_norm_b, k_norm_b, sink_a,
           rpb_b, out_norm_a, out_norm_b, w_out, w_router, router_bias, we_gate, we_up, we_down, ws_gate, ws_up,
           ws_down):
    assert w_ada.shape[0] == 1, "single-layer block"
    bsz, s, d = x.shape
    lc = ctx.shape[1]
    n = bsz * s
    rows = s // GRID_W
    assert s % (NA_QROWS * GRID_W) == 0 and rows >= NA_KROWS and bsz <= 4 and d == 1024

    c8 = jnp.concatenate([c, c_ctx[None, :], jnp.zeros((8 - bsz - 1, d), F32)], axis=0)
    mod = _ada(c8, w_ada[0], b_ada[0][None, :])

    w = w_in[0]
    aw, akw, bw = A_Q_HEADS * HEAD_DIM, A_KV_HEADS * HEAD_DIM, B_HEADS * HEAD_DIM
    cuts = np.cumsum([0, aw, akw, akw, bw, bw, bw])
    qa_w, ka_w, va_w, qb_w, kb_w, vb_w = [w[:, cuts[i]:cuts[i + 1]] for i in range(6)]
    head_cols = (np.asarray(_QA_HEAD_ORDER)[:, None] * HEAD_DIM + np.arange(HEAD_DIM)[None, :]).reshape(-1)
    qa_w = qa_w[:, head_cols]
    swap = _rope_swap_index()

    def swap_cols(wb):
        nh = wb.shape[1] // HEAD_DIM
        return wb[:, (np.arange(nh)[:, None] * HEAD_DIM + swap[None, :]).reshape(-1)]

    w_ext = jnp.concatenate([qa_w, qb_w, kb_w, vb_w, ka_w, va_w, swap_cols(qa_w), swap_cols(ka_w)],
                            axis=1).astype(BF16)
    scale = HEAD_DIM ** -0.5
    pair = lambda g: jnp.tile(g, 2)
    gains = jnp.stack([pair(q_norm_a[0]) * scale, pair(q_norm_b[0]) * scale, pair(k_norm_b[0]), pair(k_norm_a[0]),
                       pair(q_norm_a[0][swap]) * scale, pair(k_norm_a[0][swap]),
                       jnp.zeros((LANES,), F32), jnp.zeros((LANES,), F32)])
    bd = jnp.asarray(np.kron(np.eye(2), np.full((HEAD_DIM, HEAD_DIM), 1.0 / HEAD_DIM)), BF16)
    cos_t, sin_t = _rope_tables(s)
    n1 = norm1[0][None, :]
    zq = _inproj(x, mod, n1, w_ext, gains, cos_t, sin_t, bd, mod_row=None, tm=512)
    zc = _inproj(ctx, mod, n1, w_ext, gains, jnp.ones((lc, LANES), F32), jnp.zeros((lc, LANES), F32), bd,
                 mod_row=bsz, tm=lc)

    order = np.asarray(_QA_HEAD_ORDER)
    sink_col = jnp.repeat(sink_a[0][order].astype(F32), A_WINDOW)[:, None]
    o_a = _attn_a(zq, zc, sink_col, _attn_a_mask(A_WINDOW))
    o_b = _attn_b(zq, zc, _na_table(rpb_b[0], rows))

    ga = out_norm_a[0][head_cols][None, :]
    gb = out_norm_b[0][None, :]
    wo = jnp.concatenate([w_out[0][:aw][head_cols], w_out[0][aw:]], axis=0).astype(BF16)
    xres, hp, idx, gates = _merge(o_a, o_b, x, mod, ga, gb, wo, norm2[0][None, :], w_router[0].T,
                                  router_bias[0][:, None], ws_gate[0].astype(BF16), ws_up[0].astype(BF16),
                                  ws_down[0].astype(BF16), tm=512)

    rank, counts = _rank(idx, tm=512)
    t = EXPERT_ROWS
    cnt = counts[:, 0].astype(I32)
    padded = (cnt + t - 1) // t * t
    pends = jnp.cumsum(padded)
    pstart = pends - padded
    n_blocks = n * TOP_K // t + N_EXPERTS
    block_expert = jnp.minimum(jnp.searchsorted(pends, jnp.arange(n_blocks, dtype=I32) * t, side='right'),
                               N_EXPERTS - 1).astype(I32)
    n_used = (pends[-1:] // t).astype(I32)
    pos = _pos(idx, rank, pstart.astype(F32)[:, None], tm=512)

    xs = _dispatch(pos, hp, jnp.zeros((n_blocks * t, d // 2), U32), tm=128)
    ys = _experts(block_expert, n_used, xs, we_gate[0], we_up[0], we_down[0])
    out = _combine(pos, gates, xres.reshape(n, d), mod, ys, tm=128, tiles_per_batch=s // 128)
    return out.reshape(bsz, s, d)
```

```python
import functools

import numpy as np
import jax
import jax.numpy as jnp
from jax import lax
from jax.experimental import pallas as pl
from jax.experimental.pallas import tpu as pltpu

F32 = jnp.float32
BF16 = jnp.bfloat16
I32 = jnp.int32
U32 = jnp.uint32

LANES = 128
HEAD_DIM = 64
HEAD_PAIR = 2 * HEAD_DIM
GRID_W = 64
A_Q_HEADS = 8
A_KV_HEADS = 2
A_WINDOW = 128
B_HEADS = 8
NA_KH = 8
NA_KW = 16
NA_QROWS = 4
NA_KROWS = NA_QROWS + NA_KH
ROPE_BASE = 10000.0
N_EXPERTS = 256
TOP_K = 8
N_GROUPS = 8
TOPK_GROUPS = 4
ROUTED_SCALE = 2.5
EPS = 1e-6
NEG_INF = -1e30
EXPERT_ROWS = 256
VMEM_LIMIT = 56 * 1024 * 1024

_NT = (((1,), (1,)), ((), ()))


def _params(*sem):
    return pltpu.CompilerParams(dimension_semantics=sem, vmem_limit_bytes=VMEM_LIMIT)


def _silu(v):
    return v * jax.nn.sigmoid(v)


def _rms(v, gain):
    return v * lax.rsqrt(jnp.mean(v * v, axis=-1, keepdims=True) + EPS) * gain


def _pack_bf16_pairs(v):
    n = v.shape[1] // 2
    lo = lax.bitcast_convert_type(v[:, :n].astype(BF16).astype(F32), U32) >> 16
    hi = lax.bitcast_convert_type(v[:, n:].astype(BF16).astype(F32), U32) & jnp.uint32(0xFFFF0000)
    return hi | lo


def _unpack_bf16_pairs(w):
    lo = lax.bitcast_convert_type(w << 16, F32)
    hi = lax.bitcast_convert_type(w & jnp.uint32(0xFFFF0000), F32)
    return jnp.concatenate([lo, hi], axis=1)


def _ada_kernel(c_ref, w_ref, b_ref, o_ref):
    a = _silu(c_ref[...])
    o_ref[...] = jnp.dot(a, w_ref[...], preferred_element_type=F32,
                         precision=lax.Precision.HIGHEST) + b_ref[...]


def _ada(c8, w, b):
    d, n = w.shape
    bn = n // 4
    return pl.pallas_call(
        _ada_kernel,
        out_shape=jax.ShapeDtypeStruct((8, n), F32),
        grid=(n // bn,),
        in_specs=[pl.BlockSpec((8, d), lambda j: (0, 0)),
                  pl.BlockSpec((d, bn), lambda j: (0, j)),
                  pl.BlockSpec((1, bn), lambda j: (0, j))],
        out_specs=pl.BlockSpec((8, bn), lambda j: (0, j)),
        compiler_params=_params("arbitrary"),
        name="ada",
    )(c8, w, b)


_QA, _QB, _KB, _VB, _KA, _VA, _QA_SW, _KA_SW = 0, 4, 8, 12, 16, 17, 18, 22
_OUT_BLOCKS = 18
_EXT_BLOCKS = 23


def _inproj_kernel(x_ref, mod_ref, n1_ref, w_ref, g_ref, cos_ref, sin_ref, bd_ref, o_ref, *, mod_row, d):
    b = pl.program_id(0) if mod_row is None else mod_row
    xn = _rms(x_ref[0], n1_ref[...])
    sh = mod_ref[pl.ds(b, 1), 0:d]
    sc = mod_ref[pl.ds(b, 1), d:2 * d]
    h = (xn * (1.0 + sc) + sh).astype(BF16)
    z = jnp.dot(h, w_ref[...], preferred_element_type=F32)
    bd = bd_ref[...]
    cos = cos_ref[...]
    sin = sin_ref[...]

    def blk(j):
        return z[:, j * LANES:(j + 1) * LANES]

    def head_rinv(zb):
        ms = jnp.dot((zb * zb).astype(BF16), bd, preferred_element_type=F32)
        return lax.rsqrt(ms + EPS)

    def put(j, v):
        o_ref[0, :, j * LANES:(j + 1) * LANES] = v.astype(BF16)

    def roped(j, j_sw, g_row, g_sw_row):
        zb = blk(j)
        r = head_rinv(zb)
        put(j, (zb * r * g_ref[g_row:g_row + 1, :]) * cos + (blk(j_sw) * r * g_ref[g_sw_row:g_sw_row + 1, :]) * sin)

    def normed(j, g_row):
        zb = blk(j)
        put(j, zb * head_rinv(zb) * g_ref[g_row:g_row + 1, :])

    for j in range(4):
        roped(_QA + j, _QA_SW + j, 0, 4)
        normed(_QB + j, 1)
        normed(_KB + j, 2)
        put(_VB + j, blk(_VB + j))
    roped(_KA, _KA_SW, 3, 5)
    put(_VA, blk(_VA))


def _inproj(x, mod, n1, w_ext, gains, cos_t, sin_t, bd, *, mod_row, tm):
    bsz, s, d = x.shape
    kern = functools.partial(_inproj_kernel, mod_row=mod_row, d=d)
    return pl.pallas_call(
        kern,
        out_shape=jax.ShapeDtypeStruct((bsz, s, _OUT_BLOCKS * LANES), BF16),
        grid=(bsz, s // tm),
        in_specs=[pl.BlockSpec((1, tm, d), lambda b, i: (b, i, 0)),
                  pl.BlockSpec(mod.shape, lambda b, i: (0, 0)),
                  pl.BlockSpec((1, d), lambda b, i: (0, 0)),
                  pl.BlockSpec(w_ext.shape, lambda b, i: (0, 0)),
                  pl.BlockSpec(gains.shape, lambda b, i: (0, 0)),
                  pl.BlockSpec((tm, LANES), lambda b, i: (i, 0)),
                  pl.BlockSpec((tm, LANES), lambda b, i: (i, 0)),
                  pl.BlockSpec((LANES, LANES), lambda b, i: (0, 0))],
        out_specs=pl.BlockSpec((1, tm, _OUT_BLOCKS * LANES), lambda b, i: (b, i, 0)),
        compiler_params=_params("arbitrary", "arbitrary"),
        name="inproj",
    )(x, mod, n1, w_ext, gains, cos_t, sin_t, bd)


def _split_pair(qp, lo):
    zero = jnp.zeros_like(qp)
    return jnp.concatenate([jnp.where(lo, qp, zero), jnp.where(lo, zero, qp)], axis=0)


def _softmax_pv(s_loc, s_ctx, v_loc, v_ctx, sink):
    m = jnp.maximum(jnp.max(s_loc, axis=-1, keepdims=True), jnp.max(s_ctx, axis=-1, keepdims=True))
    if sink is not None:
        m = jnp.maximum(m, sink)
    p_loc = jnp.exp(s_loc - m)
    p_ctx = jnp.exp(s_ctx - m)
    l = jnp.sum(p_loc, axis=-1, keepdims=True) + jnp.sum(p_ctx, axis=-1, keepdims=True)
    if sink is not None:
        l = l + jnp.exp(sink - m)
    o = (jnp.dot(p_loc.astype(BF16), v_loc, preferred_element_type=F32)
         + jnp.dot(p_ctx.astype(BF16), v_ctx, preferred_element_type=F32))
    return o * (1.0 / l)


def _attn_a_kernel(q_ref, kp_ref, kc_ref, kn_ref, vp_ref, vc_ref, vn_ref, kx_ref, vx_ref, sink_ref, mask_ref,
                   o_ref, *, nblk):
    j = pl.program_id(1)
    tq = q_ref.shape[1]
    lo = lax.broadcasted_iota(I32, (tq, LANES), 1) < HEAD_DIM
    q = q_ref[0]
    qs = jnp.concatenate([_split_pair(q[:, p * LANES:(p + 1) * LANES], lo) for p in range(4)], axis=0)
    k = jnp.concatenate([kp_ref[0], kc_ref[0], kn_ref[0]], axis=0)
    v = jnp.concatenate([vp_ref[0], vc_ref[0], vn_ref[0]], axis=0)
    col = lax.broadcasted_iota(I32, (1, 3 * tq), 1)
    edge_ok = ((col >= tq) | (j > 0)) & ((col < 2 * tq) | (j < nblk - 1))
    edge = jnp.where(edge_ok, 0.0, NEG_INF)
    s_loc = lax.dot_general(qs, k, _NT, preferred_element_type=F32) + mask_ref[...] + edge
    s_ctx = lax.dot_general(qs, kx_ref[0], _NT, preferred_element_type=F32)
    o = _softmax_pv(s_loc, s_ctx, v, vx_ref[0], sink_ref[...])
    for p in range(4):
        o_lo = o[(2 * p) * tq:(2 * p + 1) * tq]
        o_hi = o[(2 * p + 1) * tq:(2 * p + 2) * tq]
        o_ref[0, :, p * LANES:(p + 1) * LANES] = jnp.where(lo, o_lo, o_hi).astype(BF16)


def _attn_a(zq, zc, sink_col, mask):
    bsz, s, _ = zq.shape
    lc = zc.shape[1]
    tq = A_WINDOW
    nblk = s // tq
    ka, va = _KA, _VA

    def kv_spec(col, shift):
        return pl.BlockSpec((1, tq, LANES), lambda b, j: (b, jnp.clip(j + shift, 0, nblk - 1), col))

    return pl.pallas_call(
        functools.partial(_attn_a_kernel, nblk=nblk),
        out_shape=jax.ShapeDtypeStruct((bsz, s, 4 * LANES), BF16),
        grid=(bsz, nblk),
        in_specs=[pl.BlockSpec((1, tq, 4 * LANES), lambda b, j: (b, j, 0)),
                  kv_spec(ka, -1), kv_spec(ka, 0), kv_spec(ka, 1),
                  kv_spec(va, -1), kv_spec(va, 0), kv_spec(va, 1),
                  pl.BlockSpec((1, lc, LANES), lambda b, j: (b, 0, ka)),
                  pl.BlockSpec((1, lc, LANES), lambda b, j: (b, 0, va)),
                  pl.BlockSpec(sink_col.shape, lambda b, j: (0, 0)),
                  pl.BlockSpec(mask.shape, lambda b, j: (0, 0))],
        out_specs=pl.BlockSpec((1, tq, 4 * LANES), lambda b, j: (b, j, 0)),
        compiler_params=_params("arbitrary", "arbitrary"),
        name="attn_a",
    )(zq, zq, zq, zq, zq, zq, zq, zc, zc, sink_col, mask)


def _attn_a_mask(tq):
    qi = np.arange(tq)[:, None]
    kj = np.arange(3 * tq)[None, :]
    ok = (kj >= qi) & (kj <= qi + 2 * tq)
    m = np.where(ok, 0.0, NEG_INF).astype(np.float32)
    return jnp.asarray(np.tile(m, (A_Q_HEADS, 1)))


def _attn_b_kernel(q_ref, k0_ref, k1_ref, k2_ref, v0_ref, v1_ref, v2_ref, kx_ref, vx_ref, tab_ref, o_ref):
    tq = q_ref.shape[1]
    lo = lax.broadcasted_iota(I32, (tq, LANES), 1) < HEAD_DIM
    for p in range(4):
        sl = slice(p * LANES, (p + 1) * LANES)
        qs = _split_pair(q_ref[0, :, sl], lo)
        k = jnp.concatenate([k0_ref[0, :, sl], k1_ref[0, :, sl], k2_ref[0, :, sl]], axis=0)
        v = jnp.concatenate([v0_ref[0, :, sl], v1_ref[0, :, sl], v2_ref[0, :, sl]], axis=0)
        s_loc = lax.dot_general(qs, k, _NT, preferred_element_type=F32) + tab_ref[0, p]
        s_ctx = lax.dot_general(qs, kx_ref[0, :, sl], _NT, preferred_element_type=F32)
        o = _softmax_pv(s_loc, s_ctx, v, vx_ref[0, :, sl], None)
        o_ref[0, :, sl] = jnp.where(lo, o[:tq], o[tq:]).astype(BF16)


def _attn_b(zq, zc, table):
    bsz, s, _ = zq.shape
    lc = zc.shape[1]
    tq = NA_QROWS * GRID_W
    ng = s // tq
    qb, kb, vb = _QB // 4, _KB // 4, _VB // 4

    def kv_spec(col, off):
        return pl.BlockSpec((1, tq, 4 * LANES), lambda i, b: (b, jnp.clip(i - 1, 0, ng - 3) + off, col))

    def variant(i):
        return jnp.where(i == 0, 0, jnp.where(i == ng - 1, 2, 1))

    return pl.pallas_call(
        _attn_b_kernel,
        out_shape=jax.ShapeDtypeStruct((bsz, s, 4 * LANES), BF16),
        grid=(ng, bsz),
        in_specs=[pl.BlockSpec((1, tq, 4 * LANES), lambda i, b: (b, i, qb)),
                  kv_spec(kb, 0), kv_spec(kb, 1), kv_spec(kb, 2),
                  kv_spec(vb, 0), kv_spec(vb, 1), kv_spec(vb, 2),
                  pl.BlockSpec((1, lc, 4 * LANES), lambda i, b: (b, 0, kb)),
                  pl.BlockSpec((1, lc, 4 * LANES), lambda i, b: (b, 0, vb)),
                  pl.BlockSpec((1,) + table.shape[1:], lambda i, b: (variant(i), 0, 0, 0))],
        out_specs=pl.BlockSpec((1, tq, 4 * LANES), lambda i, b: (b, i, 0)),
        compiler_params=_params("arbitrary", "arbitrary"),
        name="attn_b",
    )(zq, zq, zq, zq, zq, zq, zq, zc, zc, table)


def _na_table(rpb, rows):
    ng = rows // NA_QROWS
    nq, nk = NA_QROWS * GRID_W, NA_KROWS * GRID_W
    qc = np.arange(GRID_W)
    kc = np.arange(GRID_W)
    ws = np.clip(qc - NA_KW // 2, 0, GRID_W - NA_KW)
    valid_c = (kc[None, :] >= ws[:, None]) & (kc[None, :] < ws[:, None] + NA_KW)
    dc = np.clip(kc[None, :] - qc[:, None], -(NA_KW - 1), NA_KW - 1) + (NA_KW - 1)
    c_sel = (dc[..., None] == np.arange(2 * NA_KW - 1)) & valid_c[..., None]
    r_sels, valids = [], []
    for i in (0, 1, ng - 1):
        start = int(np.clip(NA_QROWS * i - NA_KH // 2, 0, rows - NA_KROWS))
        r = NA_QROWS * i + np.arange(NA_QROWS)
        rs = np.clip(r - NA_KH // 2, 0, rows - NA_KH)
        krow = start + np.arange(NA_KROWS)
        valid_r = (krow[None, :] >= rs[:, None]) & (krow[None, :] < rs[:, None] + NA_KH)
        dr = krow[None, :] - r[:, None] + (NA_KH - 1)
        r_sels.append((dr[..., None] == np.arange(2 * NA_KH - 1)) & valid_r[..., None])
        valids.append((valid_r[:, None, :, None] & valid_c[None, :, None, :]).reshape(nq, nk))
    r_sel = jnp.asarray(np.stack(r_sels), F32)
    bias = jnp.einsum('wxya,hab,uvb->whxuyv', r_sel, rpb.astype(F32), jnp.asarray(c_sel, F32),
                      precision=lax.Precision.HIGHEST).reshape(3, B_HEADS, nq, nk)
    tab = jnp.where(jnp.asarray(np.stack(valids))[:, None], bias, NEG_INF)
    return tab.reshape(3, B_HEADS // 2, 2 * nq, nk)


def _merge_kernel(oa_ref, ob_ref, x_ref, mod_ref, ga_ref, gb_ref, wo_ref, n2_ref, wr_ref, rb_ref,
                  wsg_ref, wsu_ref, wsd_ref, xres_ref, hp_ref, idx_ref, gate_ref, *, d):
    b = pl.program_id(0)
    tm = x_ref.shape[1]

    def mod(k):
        return mod_ref[pl.ds(b, 1), k * d:(k + 1) * d]

    na = _rms(oa_ref[0].astype(F32), ga_ref[...])
    nb = _rms(ob_ref[0].astype(F32), gb_ref[...])
    cat = jnp.concatenate([na, nb], axis=1).astype(BF16)
    y = jnp.dot(cat, wo_ref[...], preferred_element_type=F32)
    x1 = x_ref[0] + mod(2) * y
    h2 = _rms(x1, n2_ref[...]) * (1.0 + mod(4)) + mod(3)

    hb = h2.astype(BF16)
    act = _silu(jnp.dot(hb, wsg_ref[...], preferred_element_type=F32)) * jnp.dot(hb, wsu_ref[...],
                                                                                 preferred_element_type=F32)
    shared = jnp.dot(act.astype(BF16), wsd_ref[...], preferred_element_type=F32)
    xres_ref[0] = x1 + mod(5) * shared
    hp_ref[...] = _pack_bf16_pairs(h2)

    logits = lax.dot_general(wr_ref[...], h2, _NT, preferred_element_type=F32, precision=lax.Precision.HIGHEST)
    scores = jax.nn.sigmoid(logits)
    sel = scores + rb_ref[...]
    per = N_EXPERTS // N_GROUPS
    g3 = sel.reshape(N_GROUPS, per, tm)
    it3 = lax.broadcasted_iota(I32, (N_GROUPS, per, tm), 1)
    m1 = jnp.max(g3, axis=1, keepdims=True)
    first = jnp.min(jnp.where(g3 == m1, it3, per), axis=1, keepdims=True)
    m2 = jnp.max(jnp.where(it3 == first, -jnp.inf, g3), axis=1, keepdims=True)
    gscore = (m1 + m2).reshape(N_GROUPS, tm)

    itg = lax.broadcasted_iota(I32, (N_GROUPS, tm), 0)
    gsel = jnp.zeros((N_GROUPS, tm), F32)
    cur = gscore
    for _ in range(TOPK_GROUPS):
        mx = jnp.max(cur, axis=0, keepdims=True)
        fi = jnp.min(jnp.where(cur == mx, itg, N_GROUPS), axis=0, keepdims=True)
        pick = itg == fi
        gsel = jnp.where(pick, 1.0, gsel)
        cur = jnp.where(pick, -jnp.inf, cur)
    emask = jnp.broadcast_to(gsel.reshape(N_GROUPS, 1, tm), (N_GROUPS, per, tm)).reshape(N_EXPERTS, tm) > 0.5

    ite = lax.broadcasted_iota(I32, (N_EXPERTS, tm), 0)
    cur = jnp.where(emask, sel, NEG_INF)
    idx_rows, s_rows = [], []
    for _ in range(TOP_K):
        mx = jnp.max(cur, axis=0, keepdims=True)
        fi = jnp.min(jnp.where(cur == mx, ite, N_EXPERTS), axis=0, keepdims=True)
        pick = ite == fi
        idx_rows.append(fi)
        s_rows.append(jnp.sum(jnp.where(pick, scores, 0.0), axis=0, keepdims=True))
        cur = jnp.where(pick, -jnp.inf, cur)
    top_s = jnp.concatenate(s_rows, axis=0)
    idx_ref[...] = jnp.concatenate(idx_rows, axis=0)
    gate_ref[...] = top_s / jnp.sum(top_s, axis=0, keepdims=True) * ROUTED_SCALE


def _merge(o_a, o_b, x, mod, ga, gb, wo, n2, wr_t, rb_col, wsg, wsu, wsd, *, tm):
    bsz, s, d = x.shape
    nt = s // tm
    n = bsz * s
    full = lambda a: pl.BlockSpec(a.shape, lambda b, i: (0,) * a.ndim)
    return pl.pallas_call(
        functools.partial(_merge_kernel, d=d),
        out_shape=(jax.ShapeDtypeStruct((bsz, s, d), F32),
                   jax.ShapeDtypeStruct((n, d // 2), U32),
                   jax.ShapeDtypeStruct((TOP_K, n), I32),
                   jax.ShapeDtypeStruct((TOP_K, n), F32)),
        grid=(bsz, nt),
        in_specs=[pl.BlockSpec((1, tm, d // 2), lambda b, i: (b, i, 0)),
                  pl.BlockSpec((1, tm, d // 2), lambda b, i: (b, i, 0)),
                  pl.BlockSpec((1, tm, d), lambda b, i: (b, i, 0)),
                  full(mod), full(ga), full(gb), full(wo), full(n2), full(wr_t), full(rb_col),
                  full(wsg), full(wsu), full(wsd)],
        out_specs=(pl.BlockSpec((1, tm, d), lambda b, i: (b, i, 0)),
                   pl.BlockSpec((tm, d // 2), lambda b, i: (b * nt + i, 0)),
                   pl.BlockSpec((TOP_K, tm), lambda b, i: (0, b * nt + i)),
                   pl.BlockSpec((TOP_K, tm), lambda b, i: (0, b * nt + i))),
        compiler_params=_params("arbitrary", "arbitrary"),
        name="merge",
    )(o_a, o_b, x, mod, ga, gb, wo, n2, wr_t, rb_col, wsg, wsu, wsd)


def _rank_kernel(idx_ref, rank_ref, cnt_ref, carry_ref):
    tm = idx_ref.shape[1]

    @pl.when(pl.program_id(0) == 0)
    def _():
        carry_ref[...] = jnp.zeros_like(carry_ref)

    idx = idx_ref[...]
    ite = lax.broadcasted_iota(I32, (N_EXPERTS, tm), 0)
    before = (lax.broadcasted_iota(I32, (tm, tm), 0) < lax.broadcasted_iota(I32, (tm, tm), 1)).astype(BF16)
    base = carry_ref[...]
    rows = []
    for k in range(TOP_K):
        oh = ite == idx[k:k + 1, :]
        ohf = jnp.where(oh, 1.0, 0.0)
        cum = jnp.dot(ohf.astype(BF16), before, preferred_element_type=F32)
        rows.append(jnp.sum(jnp.where(oh, cum + base, 0.0), axis=0, keepdims=True))
        base = base + jnp.sum(ohf, axis=1, keepdims=True)
    rank_ref[...] = jnp.concatenate(rows, axis=0).astype(I32)
    carry_ref[...] = base
    cnt_ref[...] = base


def _rank(idx, *, tm):
    n = idx.shape[1]
    return pl.pallas_call(
        _rank_kernel,
        out_shape=(jax.ShapeDtypeStruct((TOP_K, n), I32), jax.ShapeDtypeStruct((N_EXPERTS, 1), F32)),
        grid=(n // tm,),
        in_specs=[pl.BlockSpec((TOP_K, tm), lambda i: (0, i))],
        out_specs=(pl.BlockSpec((TOP_K, tm), lambda i: (0, i)), pl.BlockSpec((N_EXPERTS, 1), lambda i: (0, 0))),
        scratch_shapes=[pltpu.VMEM((N_EXPERTS, 1), F32)],
        compiler_params=_params("arbitrary"),
        name="rank",
    )(idx)


def _pos_kernel(idx_ref, rank_ref, pstart_ref, pos_ref):
    tm = idx_ref.shape[1]
    idx = idx_ref[...]
    ite = lax.broadcasted_iota(I32, (N_EXPERTS, tm), 0)
    pstart = pstart_ref[...]
    rows = [jnp.sum(jnp.where(ite == idx[k:k + 1, :], pstart, 0.0), axis=0, keepdims=True) for k in range(TOP_K)]
    pos_ref[...] = jnp.concatenate(rows, axis=0).astype(I32) + rank_ref[...]


def _pos(idx, rank, pstart_col, *, tm):
    n = idx.shape[1]
    return pl.pallas_call(
        _pos_kernel,
        out_shape=jax.ShapeDtypeStruct((TOP_K, n), I32),
        grid=(n // tm,),
        in_specs=[pl.BlockSpec((TOP_K, tm), lambda i: (0, i)),
                  pl.BlockSpec((TOP_K, tm), lambda i: (0, i)),
                  pl.BlockSpec((N_EXPERTS, 1), lambda i: (0, 0))],
        out_specs=pl.BlockSpec((TOP_K, tm), lambda i: (0, i)),
        compiler_params=_params("arbitrary"),
        name="pos",
    )(idx, rank, pstart_col)


def _dispatch_kernel(pos_ref, h_ref, xs_in_ref, xs_ref, sem):
    del xs_in_ref
    tm = h_ref.shape[0]

    def row_copy(t, p):
        return pltpu.make_async_copy(h_ref.at[pl.ds(t, 1)], xs_ref.at[pl.ds(p, 1)], sem)

    def issue(t, c):
        for k in range(TOP_K):
            row_copy(t, pos_ref[k, t]).start()
        return c

    def drain(t, c):
        for k in range(TOP_K):
            row_copy(t, pos_ref[k, t]).wait()
        return c

    lax.fori_loop(0, tm, issue, 0)
    lax.fori_loop(0, tm, drain, 0)


def _dispatch(pos, hp, xs_init, *, tm):
    n, w = hp.shape
    return pl.pallas_call(
        _dispatch_kernel,
        out_shape=jax.ShapeDtypeStruct(xs_init.shape, U32),
        grid=(n // tm,),
        in_specs=[pl.BlockSpec((TOP_K, tm), lambda i: (0, i), memory_space=pltpu.SMEM),
                  pl.BlockSpec((tm, w), lambda i: (i, 0)),
                  pl.BlockSpec(memory_space=pl.ANY)],
        out_specs=pl.BlockSpec(memory_space=pl.ANY),
        scratch_shapes=[pltpu.SemaphoreType.DMA(())],
        input_output_aliases={2: 0},
        compiler_params=_params("arbitrary"),
        name="dispatch",
    )(pos, hp, xs_init)


def _expert_kernel(be_ref, nu_ref, xs_ref, wg_ref, wu_ref, wd_ref, ys_ref):
    del be_ref
    i = pl.program_id(0)

    @pl.when(i < nu_ref[0])
    def _():
        x = _unpack_bf16_pairs(xs_ref[...]).astype(BF16)
        g = jnp.dot(x, wg_ref[0].astype(BF16), preferred_element_type=F32)
        u = jnp.dot(x, wu_ref[0].astype(BF16), preferred_element_type=F32)
        y = jnp.dot((_silu(g) * u).astype(BF16), wd_ref[0].astype(BF16), preferred_element_type=F32)
        ys_ref[...] = _pack_bf16_pairs(y)

    @pl.when(i >= nu_ref[0])
    def _():
        ys_ref[...] = jnp.zeros_like(ys_ref)


def _experts(block_expert, n_used, xs, wg, wu, wd):
    rows, w = xs.shape
    t = EXPERT_ROWS
    d, f = wg.shape[1], wg.shape[2]
    grid_spec = pltpu.PrefetchScalarGridSpec(
        num_scalar_prefetch=2,
        grid=(rows // t,),
        in_specs=[pl.BlockSpec((t, w), lambda i, be, nu: (i, 0)),
                  pl.BlockSpec((1, d, f), lambda i, be, nu: (be[i], 0, 0)),
                  pl.BlockSpec((1, d, f), lambda i, be, nu: (be[i], 0, 0)),
                  pl.BlockSpec((1, f, d), lambda i, be, nu: (be[i], 0, 0))],
        out_specs=pl.BlockSpec((t, w), lambda i, be, nu: (i, 0)),
    )
    return pl.pallas_call(
        _expert_kernel,
        out_shape=jax.ShapeDtypeStruct((rows, w), U32),
        grid_spec=grid_spec,
        compiler_params=_params("arbitrary"),
        name="experts",
    )(block_expert, n_used, xs, wg, wu, wd)


def _combine_kernel(pos_ref, gate_ref, xres_ref, mod_ref, ys_ref, o_ref, buf_ref, sem, *, d, tiles_per_batch):
    tm = gate_ref.shape[1]
    b = pl.program_id(0) // tiles_per_batch

    def row_copy(k, t, p):
        return pltpu.make_async_copy(ys_ref.at[pl.ds(p, 1)], buf_ref.at[k, pl.ds(t, 1)], sem)

    def issue(t, c):
        for k in range(TOP_K):
            row_copy(k, t, pos_ref[k, t]).start()
        return c

    def drain(t, c):
        for k in range(TOP_K):
            row_copy(k, t, pos_ref[k, t]).wait()
        return c

    lax.fori_loop(0, tm, issue, 0)
    lax.fori_loop(0, tm, drain, 0)

    gates = gate_ref[...].T
    acc = jnp.zeros((tm, d), F32)
    for k in range(TOP_K):
        acc = acc + _unpack_bf16_pairs(buf_ref[k]) * gates[:, k:k + 1]
    o_ref[...] = xres_ref[...] + mod_ref[pl.ds(b, 1), 5 * d:6 * d] * acc


def _combine(pos, gates, xres, mod, ys, *, tm, tiles_per_batch):
    n, d = xres.shape
    return pl.pallas_call(
        functools.partial(_combine_kernel, d=d, tiles_per_batch=tiles_per_batch),
        out_shape=jax.ShapeDtypeStruct((n, d), F32),
        grid=(n // tm,),
        in_specs=[pl.BlockSpec((TOP_K, tm), lambda i: (0, i), memory_space=pltpu.SMEM),
                  pl.BlockSpec((TOP_K, tm), lambda i: (0, i)),
                  pl.BlockSpec((tm, d), lambda i: (i, 0)),
                  pl.BlockSpec(mod.shape, lambda i: (0, 0)),
                  pl.BlockSpec(memory_space=pl.ANY)],
        out_specs=pl.BlockSpec((tm, d), lambda i: (i, 0)),
        scratch_shapes=[pltpu.VMEM((TOP_K, tm, d // 2), U32), pltpu.SemaphoreType.DMA(())],
        compiler_params=_params("arbitrary"),
        name="combine",
    )(pos, gates, xres, mod, ys)


_QA_HEAD_ORDER = (0, 4, 1, 5, 2, 6, 3, 7)


def _reorder_qa_heads(a, axis):
    return jnp.concatenate([lax.slice_in_dim(a, h * HEAD_DIM, (h + 1) * HEAD_DIM, axis=axis)
                            for h in _QA_HEAD_ORDER], axis=axis)


def _rope_swap(a):
    quarter = HEAD_DIM // 4
    lead = a.shape[:-1]
    return jnp.flip(a.reshape(lead + (-1, 2, quarter)), axis=-2).reshape(a.shape)


def _rope_tables(s):
    quarter = HEAD_DIM // 4
    t = jnp.arange(s)
    row = (t // GRID_W).astype(F32)
    col = (t % GRID_W).astype(F32)
    freqs = ROPE_BASE ** (-jnp.arange(quarter, dtype=F32) / quarter)
    ar = row[:, None] * freqs[None, :]
    ac = col[:, None] * freqs[None, :]
    cos = jnp.concatenate([jnp.cos(ar), jnp.cos(ar), jnp.cos(ac), jnp.cos(ac)], axis=1)
    sin = jnp.concatenate([-jnp.sin(ar), jnp.sin(ar), -jnp.sin(ac), jnp.sin(ac)], axis=1)
    return jnp.tile(cos, (1, 2)), jnp.tile(sin, (1, 2))


def kernel(x, c, ctx, c_ctx, w_ada, b_ada, norm1, norm2, w_in, q_norm_a, k_norm_a, q_norm_b, k_norm_b, sink_a,
           rpb_b, out_norm_a, out_norm_b, w_out, w_router, router_bias, we_gate, we_up, we_down, ws_gate, ws_up,
           ws_down):
    assert w_ada.shape[0] == 1, "single-layer block"
    bsz, s, d = x.shape
    lc = ctx.shape[1]
    n = bsz * s
    rows = s // GRID_W
    assert s % (NA_QROWS * GRID_W) == 0 and rows >= NA_KROWS and bsz <= 4 and d == 1024

    c8 = jnp.concatenate([c, c_ctx[None, :], jnp.zeros((8 - bsz - 1, d), F32)], axis=0)
    mod = _ada(c8, w_ada[0], b_ada[0][None, :])

    w = w_in[0]
    aw, akw, bw = A_Q_HEADS * HEAD_DIM, A_KV_HEADS * HEAD_DIM, B_HEADS * HEAD_DIM
    cuts = np.cumsum([0, aw, akw, akw, bw, bw, bw])
    qa_w, ka_w, va_w, qb_w, kb_w, vb_w = [w[:, cuts[i]:cuts[i + 1]] for i in range(6)]
    qa_w = _reorder_qa_heads(qa_w, 1)
    w_ext = jnp.concatenate([qa_w, qb_w, kb_w, vb_w, ka_w, va_w, _rope_swap(qa_w), _rope_swap(ka_w)],
                            axis=1).astype(BF16)
    scale = HEAD_DIM ** -0.5
    pair = lambda g: jnp.tile(g, 2)
    gains = jnp.stack([pair(q_norm_a[0]) * scale, pair(q_norm_b[0]) * scale, pair(k_norm_b[0]), pair(k_norm_a[0]),
                       pair(_rope_swap(q_norm_a[0])) * scale, pair(_rope_swap(k_norm_a[0])),
                       jnp.zeros((LANES,), F32), jnp.zeros((LANES,), F32)])
    bd = jnp.asarray(np.kron(np.eye(2), np.full((HEAD_DIM, HEAD_DIM), 1.0 / HEAD_DIM)), BF16)
    cos_t, sin_t = _rope_tables(s)
    n1 = norm1[0][None, :]
    zq = _inproj(x, mod, n1, w_ext, gains, cos_t, sin_t, bd, mod_row=None, tm=512)
    zc = _inproj(ctx, mod, n1, w_ext, gains, jnp.ones((lc, LANES), F32), jnp.zeros((lc, LANES), F32), bd,
                 mod_row=bsz, tm=lc)

    sink = sink_a[0].astype(F32)
    sink_col = jnp.concatenate([jnp.broadcast_to(sink[h], (A_WINDOW,)) for h in _QA_HEAD_ORDER])[:, None]
    o_a = _attn_a(zq, zc, sink_col, _attn_a_mask(A_WINDOW))
    o_b = _attn_b(zq, zc, _na_table(rpb_b[0], rows))

    ga = _reorder_qa_heads(out_norm_a[0], 0)[None, :]
    gb = out_norm_b[0][None, :]
    wo = jnp.concatenate([_reorder_qa_heads(w_out[0][:aw], 0), w_out[0][aw:]], axis=0).astype(BF16)
    xres, hp, idx, gates = _merge(o_a, o_b, x, mod, ga, gb, wo, norm2[0][None, :], w_router[0].T,
                                  router_bias[0][:, None], ws_gate[0].astype(BF16), ws_up[0].astype(BF16),
                                  ws_down[0].astype(BF16), tm=512)

    rank, counts = _rank(idx, tm=512)
    t = EXPERT_ROWS
    cnt = counts[:, 0].astype(I32)
    padded = (cnt + t - 1) // t * t
    pends = jnp.cumsum(padded)
    pstart = pends - padded
    n_blocks = n * TOP_K // t + N_EXPERTS
    block_start = jnp.arange(n_blocks, dtype=I32) * t
    block_expert = jnp.minimum(jnp.sum((pends[None, :] <= block_start[:, None]).astype(I32), axis=1), N_EXPERTS - 1)
    n_used = (pends[-1:] // t).astype(I32)
    pos = _pos(idx, rank, pstart.astype(F32)[:, None], tm=512)

    xs = _dispatch(pos, hp, jnp.zeros((n_blocks * t, d // 2), U32), tm=128)
    ys = _experts(block_expert, n_used, xs, we_gate[0], we_up[0], we_down[0])
    out = _combine(pos, gates, xres.reshape(n, d), mod, ys, tm=128, tiles_per_batch=s // 128)
    return out.reshape(bsz, s, d)
```

```python
import functools

import numpy as np
import jax
import jax.numpy as jnp
from jax import lax
from jax.experimental import pallas as pl
from jax.experimental.pallas import tpu as pltpu

F32 = jnp.float32
BF16 = jnp.bfloat16
I32 = jnp.int32
U32 = jnp.uint32

LANES = 128
HEAD_DIM = 64
HEAD_PAIR = 2 * HEAD_DIM
GRID_W = 64
A_Q_HEADS = 8
A_KV_HEADS = 2
A_WINDOW = 128
B_HEADS = 8
NA_KH = 8
NA_KW = 16
NA_QROWS = 4
NA_KROWS = NA_QROWS + NA_KH
ROPE_BASE = 10000.0
N_EXPERTS = 256
TOP_K = 8
N_GROUPS = 8
TOPK_GROUPS = 4
ROUTED_SCALE = 2.5
EPS = 1e-6
NEG_INF = -1e30
EXPERT_ROWS = 512
VMEM_LIMIT = 56 * 1024 * 1024

_NT = (((1,), (1,)), ((), ()))


def _params(*sem):
    return pltpu.CompilerParams(dimension_semantics=sem, vmem_limit_bytes=VMEM_LIMIT)


def _silu(v):
    return v * jax.nn.sigmoid(v)


def _rms(v, gain):
    return v * lax.rsqrt(jnp.mean(v * v, axis=-1, keepdims=True) + EPS) * gain


def _pack_bf16_pairs(v):
    n = v.shape[1] // 2
    lo = lax.bitcast_convert_type(v[:, :n].astype(BF16).astype(F32), U32) >> 16
    hi = lax.bitcast_convert_type(v[:, n:].astype(BF16).astype(F32), U32) & jnp.uint32(0xFFFF0000)
    return hi | lo


def _unpack_bf16_pairs(w):
    lo = lax.bitcast_convert_type(w << 16, F32)
    hi = lax.bitcast_convert_type(w & jnp.uint32(0xFFFF0000), F32)
    return jnp.concatenate([lo, hi], axis=1)


def _ada_kernel(c_ref, w_ref, b_ref, o_ref):
    a = _silu(c_ref[...])
    o_ref[...] = jnp.dot(a, w_ref[...], preferred_element_type=F32,
                         precision=lax.Precision.HIGHEST) + b_ref[...]


def _ada(c8, w, b):
    d, n = w.shape
    bn = n // 4
    return pl.pallas_call(
        _ada_kernel,
        out_shape=jax.ShapeDtypeStruct((8, n), F32),
        grid=(n // bn,),
        in_specs=[pl.BlockSpec((8, d), lambda j: (0, 0)),
                  pl.BlockSpec((d, bn), lambda j: (0, j)),
                  pl.BlockSpec((1, bn), lambda j: (0, j))],
        out_specs=pl.BlockSpec((8, bn), lambda j: (0, j)),
        compiler_params=_params("arbitrary"),
        name="ada",
    )(c8, w, b)


_QA, _QB, _KB, _VB, _KA, _VA, _QA_SW, _KA_SW = 0, 4, 8, 12, 16, 17, 18, 22
_OUT_BLOCKS = 18
_EXT_BLOCKS = 23


def _inproj_kernel(x_ref, mod_ref, n1_ref, w_ref, g_ref, cos_ref, sin_ref, bd_ref, o_ref, *, mod_row, d):
    b = pl.program_id(0) if mod_row is None else mod_row
    xn = _rms(x_ref[0], n1_ref[...])
    sh = mod_ref[pl.ds(b, 1), 0:d]
    sc = mod_ref[pl.ds(b, 1), d:2 * d]
    h = (xn * (1.0 + sc) + sh).astype(BF16)
    z = jnp.dot(h, w_ref[...], preferred_element_type=F32)
    bd = bd_ref[...]
    cos = cos_ref[...]
    sin = sin_ref[...]

    def blk(j):
        return z[:, j * LANES:(j + 1) * LANES]

    def head_rinv(zb):
        ms = jnp.dot((zb * zb).astype(BF16), bd, preferred_element_type=F32)
        return lax.rsqrt(ms + EPS)

    def put(j, v):
        o_ref[0, :, j * LANES:(j + 1) * LANES] = v.astype(BF16)

    def roped(j, j_sw, g_row, g_sw_row):
        zb = blk(j)
        r = head_rinv(zb)
        put(j, (zb * r * g_ref[g_row:g_row + 1, :]) * cos + (blk(j_sw) * r * g_ref[g_sw_row:g_sw_row + 1, :]) * sin)

    def normed(j, g_row):
        zb = blk(j)
        put(j, zb * head_rinv(zb) * g_ref[g_row:g_row + 1, :])

    for j in range(4):
        roped(_QA + j, _QA_SW + j, 0, 4)
        normed(_QB + j, 1)
        normed(_KB + j, 2)
        put(_VB + j, blk(_VB + j))
    roped(_KA, _KA_SW, 3, 5)
    put(_VA, blk(_VA))


def _inproj(x, mod, n1, w_ext, gains, cos_t, sin_t, bd, *, mod_row, tm):
    bsz, s, d = x.shape
    kern = functools.partial(_inproj_kernel, mod_row=mod_row, d=d)
    return pl.pallas_call(
        kern,
        out_shape=jax.ShapeDtypeStruct((bsz, s, _OUT_BLOCKS * LANES), BF16),
        grid=(bsz, s // tm),
        in_specs=[pl.BlockSpec((1, tm, d), lambda b, i: (b, i, 0)),
                  pl.BlockSpec(mod.shape, lambda b, i: (0, 0)),
                  pl.BlockSpec((1, d), lambda b, i: (0, 0)),
                  pl.BlockSpec(w_ext.shape, lambda b, i: (0, 0)),
                  pl.BlockSpec(gains.shape, lambda b, i: (0, 0)),
                  pl.BlockSpec((tm, LANES), lambda b, i: (i, 0)),
                  pl.BlockSpec((tm, LANES), lambda b, i: (i, 0)),
                  pl.BlockSpec((LANES, LANES), lambda b, i: (0, 0))],
        out_specs=pl.BlockSpec((1, tm, _OUT_BLOCKS * LANES), lambda b, i: (b, i, 0)),
        compiler_params=_params("arbitrary", "arbitrary"),
        name="inproj",
    )(x, mod, n1, w_ext, gains, cos_t, sin_t, bd)


def _split_pair(qp, lo):
    zero = jnp.zeros_like(qp)
    return jnp.concatenate([jnp.where(lo, qp, zero), jnp.where(lo, zero, qp)], axis=0)


def _softmax_pv(s_loc, s_ctx, v_loc, v_ctx, sink):
    m = jnp.maximum(jnp.max(s_loc, axis=-1, keepdims=True), jnp.max(s_ctx, axis=-1, keepdims=True))
    if sink is not None:
        m = jnp.maximum(m, sink)
    p_loc = jnp.exp(s_loc - m)
    p_ctx = jnp.exp(s_ctx - m)
    l = jnp.sum(p_loc, axis=-1, keepdims=True) + jnp.sum(p_ctx, axis=-1, keepdims=True)
    if sink is not None:
        l = l + jnp.exp(sink - m)
    o = (jnp.dot(p_loc.astype(BF16), v_loc, preferred_element_type=F32)
         + jnp.dot(p_ctx.astype(BF16), v_ctx, preferred_element_type=F32))
    return o * (1.0 / l)


def _attn_a_kernel(q_ref, kp_ref, kc_ref, kn_ref, vp_ref, vc_ref, vn_ref, kx_ref, vx_ref, sink_ref, mask_ref,
                   o_ref, *, nblk):
    j = pl.program_id(1)
    tq = q_ref.shape[1]
    lo = lax.broadcasted_iota(I32, (tq, LANES), 1) < HEAD_DIM
    q = q_ref[0]
    qs = jnp.concatenate([_split_pair(q[:, p * LANES:(p + 1) * LANES], lo) for p in range(4)], axis=0)
    k = jnp.concatenate([kp_ref[0], kc_ref[0], kn_ref[0]], axis=0)
    v = jnp.concatenate([vp_ref[0], vc_ref[0], vn_ref[0]], axis=0)
    col = lax.broadcasted_iota(I32, (1, 3 * tq), 1)
    edge_ok = ((col >= tq) | (j > 0)) & ((col < 2 * tq) | (j < nblk - 1))
    edge = jnp.where(edge_ok, 0.0, NEG_INF)
    s_loc = lax.dot_general(qs, k, _NT, preferred_element_type=F32) + mask_ref[...] + edge
    s_ctx = lax.dot_general(qs, kx_ref[0], _NT, preferred_element_type=F32)
    o = _softmax_pv(s_loc, s_ctx, v, vx_ref[0], sink_ref[...])
    for p in range(4):
        o_lo = o[(2 * p) * tq:(2 * p + 1) * tq]
        o_hi = o[(2 * p + 1) * tq:(2 * p + 2) * tq]
        o_ref[0, :, p * LANES:(p + 1) * LANES] = jnp.where(lo, o_lo, o_hi).astype(BF16)


def _attn_a(zq, zc, sink_col, mask):
    bsz, s, _ = zq.shape
    lc = zc.shape[1]
    tq = A_WINDOW
    nblk = s // tq
    ka, va = _KA, _VA

    def kv_spec(col, shift):
        return pl.BlockSpec((1, tq, LANES), lambda b, j: (b, jnp.clip(j + shift, 0, nblk - 1), col))

    return pl.pallas_call(
        functools.partial(_attn_a_kernel, nblk=nblk),
        out_shape=jax.ShapeDtypeStruct((bsz, s, 4 * LANES), BF16),
        grid=(bsz, nblk),
        in_specs=[pl.BlockSpec((1, tq, 4 * LANES), lambda b, j: (b, j, 0)),
                  kv_spec(ka, -1), kv_spec(ka, 0), kv_spec(ka, 1),
                  kv_spec(va, -1), kv_spec(va, 0), kv_spec(va, 1),
                  pl.BlockSpec((1, lc, LANES), lambda b, j: (b, 0, ka)),
                  pl.BlockSpec((1, lc, LANES), lambda b, j: (b, 0, va)),
                  pl.BlockSpec(sink_col.shape, lambda b, j: (0, 0)),
                  pl.BlockSpec(mask.shape, lambda b, j: (0, 0))],
        out_specs=pl.BlockSpec((1, tq, 4 * LANES), lambda b, j: (b, j, 0)),
        compiler_params=_params("arbitrary", "arbitrary"),
        name="attn_a",
    )(zq, zq, zq, zq, zq, zq, zq, zc, zc, sink_col, mask)


def _attn_a_mask(tq):
    qi = np.arange(tq)[:, None]
    kj = np.arange(3 * tq)[None, :]
    ok = (kj >= qi) & (kj <= qi + 2 * tq)
    m = np.where(ok, 0.0, NEG_INF).astype(np.float32)
    return jnp.asarray(np.tile(m, (A_Q_HEADS, 1)))


def _attn_b_kernel(q_ref, k0_ref, k1_ref, k2_ref, v0_ref, v1_ref, v2_ref, kx_ref, vx_ref, tab_ref, o_ref):
    tq = q_ref.shape[1]
    lo = lax.broadcasted_iota(I32, (tq, LANES), 1) < HEAD_DIM
    for p in range(4):
        sl = slice(p * LANES, (p + 1) * LANES)
        qs = _split_pair(q_ref[0, :, sl], lo)
        k = jnp.concatenate([k0_ref[0, :, sl], k1_ref[0, :, sl], k2_ref[0, :, sl]], axis=0)
        v = jnp.concatenate([v0_ref[0, :, sl], v1_ref[0, :, sl], v2_ref[0, :, sl]], axis=0)
        s_loc = lax.dot_general(qs, k, _NT, preferred_element_type=F32) + tab_ref[0, p]
        s_ctx = lax.dot_general(qs, kx_ref[0, :, sl], _NT, preferred_element_type=F32)
        o = _softmax_pv(s_loc, s_ctx, v, vx_ref[0, :, sl], None)
        o_ref[0, :, sl] = jnp.where(lo, o[:tq], o[tq:]).astype(BF16)


def _attn_b(zq, zc, table):
    bsz, s, _ = zq.shape
    lc = zc.shape[1]
    tq = NA_QROWS * GRID_W
    ng = s // tq
    qb, kb, vb = _QB // 4, _KB // 4, _VB // 4

    def kv_spec(col, off):
        return pl.BlockSpec((1, tq, 4 * LANES), lambda i, b: (b, jnp.clip(i - 1, 0, ng - 3) + off, col))

    def variant(i):
        return jnp.where(i == 0, 0, jnp.where(i == ng - 1, 2, 1))

    return pl.pallas_call(
        _attn_b_kernel,
        out_shape=jax.ShapeDtypeStruct((bsz, s, 4 * LANES), BF16),
        grid=(ng, bsz),
        in_specs=[pl.BlockSpec((1, tq, 4 * LANES), lambda i, b: (b, i, qb)),
                  kv_spec(kb, 0), kv_spec(kb, 1), kv_spec(kb, 2),
                  kv_spec(vb, 0), kv_spec(vb, 1), kv_spec(vb, 2),
                  pl.BlockSpec((1, lc, 4 * LANES), lambda i, b: (b, 0, kb)),
                  pl.BlockSpec((1, lc, 4 * LANES), lambda i, b: (b, 0, vb)),
                  pl.BlockSpec((1,) + table.shape[1:], lambda i, b: (variant(i), 0, 0, 0))],
        out_specs=pl.BlockSpec((1, tq, 4 * LANES), lambda i, b: (b, i, 0)),
        compiler_params=_params("arbitrary", "arbitrary"),
        name="attn_b",
    )(zq, zq, zq, zq, zq, zq, zq, zc, zc, table)


def _na_table(rpb, rows):
    ng = rows // NA_QROWS
    nq, nk = NA_QROWS * GRID_W, NA_KROWS * GRID_W
    qc = np.arange(GRID_W)
    kc = np.arange(GRID_W)
    ws = np.clip(qc - NA_KW // 2, 0, GRID_W - NA_KW)
    valid_c = (kc[None, :] >= ws[:, None]) & (kc[None, :] < ws[:, None] + NA_KW)
    dc = np.clip(kc[None, :] - qc[:, None], -(NA_KW - 1), NA_KW - 1) + (NA_KW - 1)
    c_sel = (dc[..., None] == np.arange(2 * NA_KW - 1)) & valid_c[..., None]
    r_sels, valids = [], []
    for i in (0, 1, ng - 1):
        start = int(np.clip(NA_QROWS * i - NA_KH // 2, 0, rows - NA_KROWS))
        r = NA_QROWS * i + np.arange(NA_QROWS)
        rs = np.clip(r - NA_KH // 2, 0, rows - NA_KH)
        krow = start + np.arange(NA_KROWS)
        valid_r = (krow[None, :] >= rs[:, None]) & (krow[None, :] < rs[:, None] + NA_KH)
        dr = krow[None, :] - r[:, None] + (NA_KH - 1)
        r_sels.append((dr[..., None] == np.arange(2 * NA_KH - 1)) & valid_r[..., None])
        valids.append((valid_r[:, None, :, None] & valid_c[None, :, None, :]).reshape(nq, nk))
    r_sel = jnp.asarray(np.stack(r_sels), F32)
    bias = jnp.einsum('wxya,hab,uvb->whxuyv', r_sel, rpb.astype(F32), jnp.asarray(c_sel, F32),
                      precision=lax.Precision.HIGHEST).reshape(3, B_HEADS, nq, nk)
    tab = jnp.where(jnp.asarray(np.stack(valids))[:, None], bias, NEG_INF)
    return tab.reshape(3, B_HEADS // 2, 2 * nq, nk)


def _merge_kernel(oa_ref, ob_ref, x_ref, mod_ref, ga_ref, gb_ref, wo_ref, n2_ref, wr_ref, rb_ref,
                  wsg_ref, wsu_ref, wsd_ref, xres_ref, hp_ref, idx_ref, gate_ref, *, d):
    b = pl.program_id(0)
    tm = x_ref.shape[1]

    def mod(k):
        return mod_ref[pl.ds(b, 1), k * d:(k + 1) * d]

    na = _rms(oa_ref[0].astype(F32), ga_ref[...])
    nb = _rms(ob_ref[0].astype(F32), gb_ref[...])
    cat = jnp.concatenate([na, nb], axis=1).astype(BF16)
    y = jnp.dot(cat, wo_ref[...], preferred_element_type=F32)
    x1 = x_ref[0] + mod(2) * y
    h2 = _rms(x1, n2_ref[...]) * (1.0 + mod(4)) + mod(3)

    hb = h2.astype(BF16)
    act = _silu(jnp.dot(hb, wsg_ref[...], preferred_element_type=F32)) * jnp.dot(hb, wsu_ref[...],
                                                                                 preferred_element_type=F32)
    shared = jnp.dot(act.astype(BF16), wsd_ref[...], preferred_element_type=F32)
    xres_ref[0] = x1 + mod(5) * shared
    hp_ref[...] = _pack_bf16_pairs(h2)

    logits = lax.dot_general(wr_ref[...], h2, _NT, preferred_element_type=F32, precision=lax.Precision.HIGHEST)
    scores = jax.nn.sigmoid(logits)
    sel = scores + rb_ref[...]
    per = N_EXPERTS // N_GROUPS
    g3 = sel.reshape(N_GROUPS, per, tm)
    it3 = lax.broadcasted_iota(I32, (N_GROUPS, per, tm), 1)
    m1 = jnp.max(g3, axis=1, keepdims=True)
    first = jnp.min(jnp.where(g3 == m1, it3, per), axis=1, keepdims=True)
    m2 = jnp.max(jnp.where(it3 == first, -jnp.inf, g3), axis=1, keepdims=True)
    gscore = (m1 + m2).reshape(N_GROUPS, tm)

    itg = lax.broadcasted_iota(I32, (N_GROUPS, tm), 0)
    gsel = jnp.zeros((N_GROUPS, tm), F32)
    cur = gscore
    for _ in range(TOPK_GROUPS):
        mx = jnp.max(cur, axis=0, keepdims=True)
        fi = jnp.min(jnp.where(cur == mx, itg, N_GROUPS), axis=0, keepdims=True)
        pick = itg == fi
        gsel = jnp.where(pick, 1.0, gsel)
        cur = jnp.where(pick, -jnp.inf, cur)
    emask = jnp.broadcast_to(gsel.reshape(N_GROUPS, 1, tm), (N_GROUPS, per, tm)).reshape(N_EXPERTS, tm) > 0.5

    ite = lax.broadcasted_iota(I32, (N_EXPERTS, tm), 0)
    cur = jnp.where(emask, sel, NEG_INF)
    idx_rows, s_rows = [], []
    for _ in range(TOP_K):
        mx = jnp.max(cur, axis=0, keepdims=True)
        fi = jnp.min(jnp.where(cur == mx, ite, N_EXPERTS), axis=0, keepdims=True)
        pick = ite == fi
        idx_rows.append(fi)
        s_rows.append(jnp.sum(jnp.where(pick, scores, 0.0), axis=0, keepdims=True))
        cur = jnp.where(pick, -jnp.inf, cur)
    top_s = jnp.concatenate(s_rows, axis=0)
    idx_ref[...] = jnp.concatenate(idx_rows, axis=0)
    gate_ref[...] = top_s / jnp.sum(top_s, axis=0, keepdims=True) * ROUTED_SCALE


def _merge(o_a, o_b, x, mod, ga, gb, wo, n2, wr_t, rb_col, wsg, wsu, wsd, *, tm):
    bsz, s, d = x.shape
    nt = s // tm
    n = bsz * s
    full = lambda a: pl.BlockSpec(a.shape, lambda b, i: (0,) * a.ndim)
    return pl.pallas_call(
        functools.partial(_merge_kernel, d=d),
        out_shape=(jax.ShapeDtypeStruct((bsz, s, d), F32),
                   jax.ShapeDtypeStruct((n, d // 2), U32),
                   jax.ShapeDtypeStruct((TOP_K, n), I32),
                   jax.ShapeDtypeStruct((TOP_K, n), F32)),
        grid=(bsz, nt),
        in_specs=[pl.BlockSpec((1, tm, d // 2), lambda b, i: (b, i, 0)),
                  pl.BlockSpec((1, tm, d // 2), lambda b, i: (b, i, 0)),
                  pl.BlockSpec((1, tm, d), lambda b, i: (b, i, 0)),
                  full(mod), full(ga), full(gb), full(wo), full(n2), full(wr_t), full(rb_col),
                  full(wsg), full(wsu), full(wsd)],
        out_specs=(pl.BlockSpec((1, tm, d), lambda b, i: (b, i, 0)),
                   pl.BlockSpec((tm, d // 2), lambda b, i: (b * nt + i, 0)),
                   pl.BlockSpec((TOP_K, tm), lambda b, i: (0, b * nt + i)),
                   pl.BlockSpec((TOP_K, tm), lambda b, i: (0, b * nt + i))),
        compiler_params=_params("arbitrary", "arbitrary"),
        name="merge",
    )(o_a, o_b, x, mod, ga, gb, wo, n2, wr_t, rb_col, wsg, wsu, wsd)


def _rank_kernel(idx_ref, rank_ref, cnt_ref, carry_ref):
    tm = idx_ref.shape[1]

    @pl.when(pl.program_id(0) == 0)
    def _():
        carry_ref[...] = jnp.zeros_like(carry_ref)

    idx = idx_ref[...]
    ite = lax.broadcasted_iota(I32, (N_EXPERTS, tm), 0)
    before = (lax.broadcasted_iota(I32, (tm, tm), 0) < lax.broadcasted_iota(I32, (tm, tm), 1)).astype(BF16)
    base = carry_ref[...]
    rows = []
    for k in range(TOP_K):
        oh = ite == idx[k:k + 1, :]
        ohf = jnp.where(oh, 1.0, 0.0)
        cum = jnp.dot(ohf.astype(BF16), before, preferred_element_type=F32)
        rows.append(jnp.sum(jnp.where(oh, cum + base, 0.0), axis=0, keepdims=True))
        base = base + jnp.sum(ohf, axis=1, keepdims=True)
    rank_ref[...] = jnp.concatenate(rows, axis=0).astype(I32)
    carry_ref[...] = base
    cnt_ref[...] = base


def _rank(idx, *, tm):
    n = idx.shape[1]
    return pl.pallas_call(
        _rank_kernel,
        out_shape=(jax.ShapeDtypeStruct((TOP_K, n), I32), jax.ShapeDtypeStruct((N_EXPERTS, 1), F32)),
        grid=(n // tm,),
        in_specs=[pl.BlockSpec((TOP_K, tm), lambda i: (0, i))],
        out_specs=(pl.BlockSpec((TOP_K, tm), lambda i: (0, i)), pl.BlockSpec((N_EXPERTS, 1), lambda i: (0, 0))),
        scratch_shapes=[pltpu.VMEM((N_EXPERTS, 1), F32)],
        compiler_params=_params("arbitrary"),
        name="rank",
    )(idx)


def _pos_kernel(idx_ref, rank_ref, pstart_ref, pos_ref):
    tm = idx_ref.shape[1]
    idx = idx_ref[...]
    ite = lax.broadcasted_iota(I32, (N_EXPERTS, tm), 0)
    pstart = pstart_ref[...]
    rows = [jnp.sum(jnp.where(ite == idx[k:k + 1, :], pstart, 0.0), axis=0, keepdims=True) for k in range(TOP_K)]
    pos_ref[...] = jnp.concatenate(rows, axis=0).astype(I32) + rank_ref[...]


def _pos(idx, rank, pstart_col, *, tm):
    n = idx.shape[1]
    return pl.pallas_call(
        _pos_kernel,
        out_shape=jax.ShapeDtypeStruct((TOP_K, n), I32),
        grid=(n // tm,),
        in_specs=[pl.BlockSpec((TOP_K, tm), lambda i: (0, i)),
                  pl.BlockSpec((TOP_K, tm), lambda i: (0, i)),
                  pl.BlockSpec((N_EXPERTS, 1), lambda i: (0, 0))],
        out_specs=pl.BlockSpec((TOP_K, tm), lambda i: (0, i)),
        compiler_params=_params("arbitrary"),
        name="pos",
    )(idx, rank, pstart_col)


def _dispatch_kernel(pos_ref, h_ref, xs_in_ref, xs_ref, sem):
    del xs_in_ref
    tm = h_ref.shape[0]

    def row_copy(t, p):
        return pltpu.make_async_copy(h_ref.at[pl.ds(t, 1)], xs_ref.at[pl.ds(p, 1)], sem)

    def issue(t, c):
        for k in range(TOP_K):
            row_copy(t, pos_ref[k, t]).start()
        return c

    def drain(t, c):
        for k in range(TOP_K):
            row_copy(t, pos_ref[k, t]).wait()
        return c

    lax.fori_loop(0, tm, issue, 0)
    lax.fori_loop(0, tm, drain, 0)


def _dispatch(pos, hp, xs_init, *, tm):
    n, w = hp.shape
    return pl.pallas_call(
        _dispatch_kernel,
        out_shape=jax.ShapeDtypeStruct(xs_init.shape, U32),
        grid=(n // tm,),
        in_specs=[pl.BlockSpec((TOP_K, tm), lambda i: (0, i), memory_space=pltpu.SMEM),
                  pl.BlockSpec((tm, w), lambda i: (i, 0)),
                  pl.BlockSpec(memory_space=pl.ANY)],
        out_specs=pl.BlockSpec(memory_space=pl.ANY),
        scratch_shapes=[pltpu.SemaphoreType.DMA(())],
        input_output_aliases={2: 0},
        compiler_params=_params("arbitrary"),
        name="dispatch",
    )(pos, hp, xs_init)


def _expert_kernel(be_ref, nu_ref, xs_ref, wg_ref, wu_ref, wd_ref, ys_ref, wg_s, wu_s, wd_s):
    i = pl.program_id(0)
    used = i < nu_ref[0]

    @pl.when(used & ((i == 0) | (be_ref[i] != be_ref[jnp.maximum(i - 1, 0)])))
    def _():
        wg_s[...] = wg_ref[0].astype(BF16)
        wu_s[...] = wu_ref[0].astype(BF16)
        wd_s[...] = wd_ref[0].astype(BF16)

    @pl.when(used)
    def _():
        x = _unpack_bf16_pairs(xs_ref[...]).astype(BF16)
        g = jnp.dot(x, wg_s[...], preferred_element_type=F32)
        u = jnp.dot(x, wu_s[...], preferred_element_type=F32)
        y = jnp.dot((_silu(g) * u).astype(BF16), wd_s[...], preferred_element_type=F32)
        ys_ref[...] = _pack_bf16_pairs(y)

    @pl.when(i >= nu_ref[0])
    def _():
        ys_ref[...] = jnp.zeros_like(ys_ref)


def _experts(block_expert, n_used, xs, wg, wu, wd):
    rows, w = xs.shape
    t = EXPERT_ROWS
    d, f = wg.shape[1], wg.shape[2]
    grid_spec = pltpu.PrefetchScalarGridSpec(
        num_scalar_prefetch=2,
        grid=(rows // t,),
        in_specs=[pl.BlockSpec((t, w), lambda i, be, nu: (i, 0)),
                  pl.BlockSpec((1, d, f), lambda i, be, nu: (be[i], 0, 0)),
                  pl.BlockSpec((1, d, f), lambda i, be, nu: (be[i], 0, 0)),
                  pl.BlockSpec((1, f, d), lambda i, be, nu: (be[i], 0, 0))],
        out_specs=pl.BlockSpec((t, w), lambda i, be, nu: (i, 0)),
        scratch_shapes=[pltpu.VMEM((d, f), BF16), pltpu.VMEM((d, f), BF16), pltpu.VMEM((f, d), BF16)],
    )
    return pl.pallas_call(
        _expert_kernel,
        out_shape=jax.ShapeDtypeStruct((rows, w), U32),
        grid_spec=grid_spec,
        compiler_params=_params("arbitrary"),
        name="experts",
    )(block_expert, n_used, xs, wg, wu, wd)


def _combine_kernel(pos_ref, gate_ref, xres_ref, mod_ref, ys_ref, o_ref, buf_ref, sem, *, d, tiles_per_batch):
    tm = gate_ref.shape[1]
    b = pl.program_id(0) // tiles_per_batch

    def row_copy(k, t, p):
        return pltpu.make_async_copy(ys_ref.at[pl.ds(p, 1)], buf_ref.at[k, pl.ds(t, 1)], sem)

    def issue(t, c):
        for k in range(TOP_K):
            row_copy(k, t, pos_ref[k, t]).start()
        return c

    def drain(t, c):
        for k in range(TOP_K):
            row_copy(k, t, pos_ref[k, t]).wait()
        return c

    lax.fori_loop(0, tm, issue, 0)
    lax.fori_loop(0, tm, drain, 0)

    gates = gate_ref[...].T
    acc = jnp.zeros((tm, d), F32)
    for k in range(TOP_K):
        acc = acc + _unpack_bf16_pairs(buf_ref[k]) * gates[:, k:k + 1]
    o_ref[...] = xres_ref[...] + mod_ref[pl.ds(b, 1), 5 * d:6 * d] * acc


def _combine(pos, gates, xres, mod, ys, *, tm, tiles_per_batch):
    n, d = xres.shape
    return pl.pallas_call(
        functools.partial(_combine_kernel, d=d, tiles_per_batch=tiles_per_batch),
        out_shape=jax.ShapeDtypeStruct((n, d), F32),
        grid=(n // tm,),
        in_specs=[pl.BlockSpec((TOP_K, tm), lambda i: (0, i), memory_space=pltpu.SMEM),
                  pl.BlockSpec((TOP_K, tm), lambda i: (0, i)),
                  pl.BlockSpec((tm, d), lambda i: (i, 0)),
                  pl.BlockSpec(mod.shape, lambda i: (0, 0)),
                  pl.BlockSpec(memory_space=pl.ANY)],
        out_specs=pl.BlockSpec((tm, d), lambda i: (i, 0)),
        scratch_shapes=[pltpu.VMEM((TOP_K, tm, d // 2), U32), pltpu.SemaphoreType.DMA(())],
        compiler_params=_params("arbitrary"),
        name="combine",
    )(pos, gates, xres, mod, ys)


_QA_HEAD_ORDER = (0, 4, 1, 5, 2, 6, 3, 7)


def _reorder_qa_heads(a, axis):
    return jnp.concatenate([lax.slice_in_dim(a, h * HEAD_DIM, (h + 1) * HEAD_DIM, axis=axis)
                            for h in _QA_HEAD_ORDER], axis=axis)


def _rope_swap(a):
    quarter = HEAD_DIM // 4
    lead = a.shape[:-1]
    return jnp.flip(a.reshape(lead + (-1, 2, quarter)), axis=-2).reshape(a.shape)


def _rope_tables(s):
    quarter = HEAD_DIM // 4
    t = jnp.arange(s)
    row = (t // GRID_W).astype(F32)
    col = (t % GRID_W).astype(F32)
    freqs = ROPE_BASE ** (-jnp.arange(quarter, dtype=F32) / quarter)
    ar = row[:, None] * freqs[None, :]
    ac = col[:, None] * freqs[None, :]
    cos = jnp.concatenate([jnp.cos(ar), jnp.cos(ar), jnp.cos(ac), jnp.cos(ac)], axis=1)
    sin = jnp.concatenate([-jnp.sin(ar), jnp.sin(ar), -jnp.sin(ac), jnp.sin(ac)], axis=1)
    return jnp.tile(cos, (1, 2)), jnp.tile(sin, (1, 2))


def kernel(x, c, ctx, c_ctx, w_ada, b_ada, norm1, norm2, w_in, q_norm_a, k_norm_a, q_norm_b, k_norm_b, sink_a,
           rpb_b, out_norm_a, out_norm_b, w_out, w_router, router_bias, we_gate, we_up, we_down, ws_gate, ws_up,
           ws_down):
    assert w_ada.shape[0] == 1, "single-layer block"
    bsz, s, d = x.shape
    lc = ctx.shape[1]
    n = bsz * s
    rows = s // GRID_W
    assert s % (NA_QROWS * GRID_W) == 0 and rows >= NA_KROWS and bsz <= 4 and d == 1024

    c8 = jnp.concatenate([c, c_ctx[None, :], jnp.zeros((8 - bsz - 1, d), F32)], axis=0)
    mod = _ada(c8, w_ada[0], b_ada[0][None, :])

    w = w_in[0]
    aw, akw, bw = A_Q_HEADS * HEAD_DIM, A_KV_HEADS * HEAD_DIM, B_HEADS * HEAD_DIM
    cuts = np.cumsum([0, aw, akw, akw, bw, bw, bw])
    qa_w, ka_w, va_w, qb_w, kb_w, vb_w = [w[:, cuts[i]:cuts[i + 1]] for i in range(6)]
    qa_w = _reorder_qa_heads(qa_w, 1)
    w_ext = jnp.concatenate([qa_w, qb_w, kb_w, vb_w, ka_w, va_w, _rope_swap(qa_w), _rope_swap(ka_w)],
                            axis=1).astype(BF16)
    scale = HEAD_DIM ** -0.5
    pair = lambda g: jnp.tile(g, 2)
    gains = jnp.stack([pair(q_norm_a[0]) * scale, pair(q_norm_b[0]) * scale, pair(k_norm_b[0]), pair(k_norm_a[0]),
                       pair(_rope_swap(q_norm_a[0])) * scale, pair(_rope_swap(k_norm_a[0])),
                       jnp.zeros((LANES,), F32), jnp.zeros((LANES,), F32)])
    bd = jnp.asarray(np.kron(np.eye(2), np.full((HEAD_DIM, HEAD_DIM), 1.0 / HEAD_DIM)), BF16)
    cos_t, sin_t = _rope_tables(s)
    n1 = norm1[0][None, :]
    zq = _inproj(x, mod, n1, w_ext, gains, cos_t, sin_t, bd, mod_row=None, tm=512)
    zc = _inproj(ctx, mod, n1, w_ext, gains, jnp.ones((lc, LANES), F32), jnp.zeros((lc, LANES), F32), bd,
                 mod_row=bsz, tm=lc)

    sink = sink_a[0].astype(F32)
    sink_col = jnp.concatenate([jnp.broadcast_to(sink[h], (A_WINDOW,)) for h in _QA_HEAD_ORDER])[:, None]
    o_a = _attn_a(zq, zc, sink_col, _attn_a_mask(A_WINDOW))
    o_b = _attn_b(zq, zc, _na_table(rpb_b[0], rows))

    ga = _reorder_qa_heads(out_norm_a[0], 0)[None, :]
    gb = out_norm_b[0][None, :]
    wo = jnp.concatenate([_reorder_qa_heads(w_out[0][:aw], 0), w_out[0][aw:]], axis=0).astype(BF16)
    xres, hp, idx, gates = _merge(o_a, o_b, x, mod, ga, gb, wo, norm2[0][None, :], w_router[0].T,
                                  router_bias[0][:, None], ws_gate[0].astype(BF16), ws_up[0].astype(BF16),
                                  ws_down[0].astype(BF16), tm=512)

    rank, counts = _rank(idx, tm=512)
    t = EXPERT_ROWS
    cnt = counts[:, 0].astype(I32)
    padded = (cnt + t - 1) // t * t
    pends = jnp.cumsum(padded)
    pstart = pends - padded
    n_blocks = n * TOP_K // t + N_EXPERTS
    block_start = jnp.arange(n_blocks, dtype=I32) * t
    block_expert = jnp.minimum(jnp.sum((pends[None, :] <= block_start[:, None]).astype(I32), axis=1), N_EXPERTS - 1)
    n_used = (pends[-1:] // t).astype(I32)
    pos = _pos(idx, rank, pstart.astype(F32)[:, None], tm=512)

    xs = _dispatch(pos, hp, jnp.zeros((n_blocks * t, d // 2), U32), tm=128)
    ys = _experts(block_expert, n_used, xs, we_gate[0], we_up[0], we_down[0])
    out = _combine(pos, gates, xres.reshape(n, d), mod, ys, tm=128, tiles_per_batch=s // 128)
    return out.reshape(bsz, s, d)
```

```python
import functools

import numpy as np
import jax
import jax.numpy as jnp
from jax import lax
from jax.experimental import pallas as pl
from jax.experimental.pallas import tpu as pltpu
from jax.experimental.pallas import tpu_sc as plsc

F32 = jnp.float32
BF16 = jnp.bfloat16
I32 = jnp.int32
U32 = jnp.uint32

LANES = 128
HEAD_DIM = 64
HEAD_PAIR = 2 * HEAD_DIM
GRID_W = 64
A_Q_HEADS = 8
A_KV_HEADS = 2
A_WINDOW = 128
B_HEADS = 8
NA_KH = 8
NA_KW = 16
NA_QROWS = 4
NA_KROWS = NA_QROWS + NA_KH
ROPE_BASE = 10000.0
N_EXPERTS = 256
TOP_K = 8
N_GROUPS = 8
TOPK_GROUPS = 4
ROUTED_SCALE = 2.5
EPS = 1e-6
NEG_INF = -1e30
EXPERT_ROWS = 512
SC_WINDOW = 128
VMEM_LIMIT = 56 * 1024 * 1024

_NT = (((1,), (1,)), ((), ()))


def _params(*sem):
    return pltpu.CompilerParams(dimension_semantics=sem, vmem_limit_bytes=VMEM_LIMIT)


def _silu(v):
    return v * jax.nn.sigmoid(v)


def _rms(v, gain):
    return v * lax.rsqrt(jnp.mean(v * v, axis=-1, keepdims=True) + EPS) * gain


def _pack_bf16_pairs(v):
    n = v.shape[1] // 2
    lo = lax.bitcast_convert_type(v[:, :n].astype(BF16).astype(F32), U32) >> 16
    hi = lax.bitcast_convert_type(v[:, n:].astype(BF16).astype(F32), U32) & jnp.uint32(0xFFFF0000)
    return hi | lo


def _unpack_bf16_pairs(w):
    lo = lax.bitcast_convert_type(w << 16, F32)
    hi = lax.bitcast_convert_type(w & jnp.uint32(0xFFFF0000), F32)
    return jnp.concatenate([lo, hi], axis=1)


def _ada_kernel(c_ref, w_ref, b_ref, o_ref):
    a = _silu(c_ref[...])
    o_ref[...] = jnp.dot(a, w_ref[...], preferred_element_type=F32,
                         precision=lax.Precision.HIGHEST) + b_ref[...]


def _ada(c8, w, b):
    d, n = w.shape
    bn = n // 4
    return pl.pallas_call(
        _ada_kernel,
        out_shape=jax.ShapeDtypeStruct((8, n), F32),
        grid=(n // bn,),
        in_specs=[pl.BlockSpec((8, d), lambda j: (0, 0)),
                  pl.BlockSpec((d, bn), lambda j: (0, j)),
                  pl.BlockSpec((1, bn), lambda j: (0, j))],
        out_specs=pl.BlockSpec((8, bn), lambda j: (0, j)),
        compiler_params=_params("arbitrary"),
        name="ada",
    )(c8, w, b)


_QA, _QB, _KB, _VB, _KA, _VA, _QA_SW, _KA_SW = 0, 4, 8, 12, 16, 17, 18, 22
_OUT_BLOCKS = 18
_EXT_BLOCKS = 23


def _inproj_kernel(x_ref, mod_ref, n1_ref, w_ref, g_ref, cos_ref, sin_ref, bd_ref, o_ref, *, mod_row, d):
    b = pl.program_id(0) if mod_row is None else mod_row
    xn = _rms(x_ref[0], n1_ref[...])
    sh = mod_ref[pl.ds(b, 1), 0:d]
    sc = mod_ref[pl.ds(b, 1), d:2 * d]
    h = (xn * (1.0 + sc) + sh).astype(BF16)
    z = jnp.dot(h, w_ref[...], preferred_element_type=F32)
    bd = bd_ref[...]
    cos = cos_ref[...]
    sin = sin_ref[...]

    def blk(j):
        return z[:, j * LANES:(j + 1) * LANES]

    def head_rinv(zb):
        ms = jnp.dot((zb * zb).astype(BF16), bd, preferred_element_type=F32)
        return lax.rsqrt(ms + EPS)

    def put(j, v):
        o_ref[0, :, j * LANES:(j + 1) * LANES] = v.astype(BF16)

    def roped(j, j_sw, g_row, g_sw_row):
        zb = blk(j)
        r = head_rinv(zb)
        put(j, (zb * r * g_ref[g_row:g_row + 1, :]) * cos + (blk(j_sw) * r * g_ref[g_sw_row:g_sw_row + 1, :]) * sin)

    def normed(j, g_row):
        zb = blk(j)
        put(j, zb * head_rinv(zb) * g_ref[g_row:g_row + 1, :])

    for j in range(4):
        roped(_QA + j, _QA_SW + j, 0, 4)
        normed(_QB + j, 1)
        normed(_KB + j, 2)
        put(_VB + j, blk(_VB + j))
    roped(_KA, _KA_SW, 3, 5)
    put(_VA, blk(_VA))


def _inproj(x, mod, n1, w_ext, gains, cos_t, sin_t, bd, *, mod_row, tm):
    bsz, s, d = x.shape
    kern = functools.partial(_inproj_kernel, mod_row=mod_row, d=d)
    return pl.pallas_call(
        kern,
        out_shape=jax.ShapeDtypeStruct((bsz, s, _OUT_BLOCKS * LANES), BF16),
        grid=(bsz, s // tm),
        in_specs=[pl.BlockSpec((1, tm, d), lambda b, i: (b, i, 0)),
                  pl.BlockSpec(mod.shape, lambda b, i: (0, 0)),
                  pl.BlockSpec((1, d), lambda b, i: (0, 0)),
                  pl.BlockSpec(w_ext.shape, lambda b, i: (0, 0)),
                  pl.BlockSpec(gains.shape, lambda b, i: (0, 0)),
                  pl.BlockSpec((tm, LANES), lambda b, i: (i, 0)),
                  pl.BlockSpec((tm, LANES), lambda b, i: (i, 0)),
                  pl.BlockSpec((LANES, LANES), lambda b, i: (0, 0))],
        out_specs=pl.BlockSpec((1, tm, _OUT_BLOCKS * LANES), lambda b, i: (b, i, 0)),
        compiler_params=_params("arbitrary", "arbitrary"),
        name="inproj",
    )(x, mod, n1, w_ext, gains, cos_t, sin_t, bd)


def _split_pair(qp, lo):
    zero = jnp.zeros_like(qp)
    return jnp.concatenate([jnp.where(lo, qp, zero), jnp.where(lo, zero, qp)], axis=0)


def _softmax_pv(s_loc, s_ctx, v_loc, v_ctx, sink):
    m = jnp.maximum(jnp.max(s_loc, axis=-1, keepdims=True), jnp.max(s_ctx, axis=-1, keepdims=True))
    if sink is not None:
        m = jnp.maximum(m, sink)
    p_loc = jnp.exp(s_loc - m)
    p_ctx = jnp.exp(s_ctx - m)
    l = jnp.sum(p_loc, axis=-1, keepdims=True) + jnp.sum(p_ctx, axis=-1, keepdims=True)
    if sink is not None:
        l = l + jnp.exp(sink - m)
    o = (jnp.dot(p_loc.astype(BF16), v_loc, preferred_element_type=F32)
         + jnp.dot(p_ctx.astype(BF16), v_ctx, preferred_element_type=F32))
    return o * (1.0 / l)


def _attn_a_kernel(q_ref, kp_ref, kc_ref, kn_ref, vp_ref, vc_ref, vn_ref, kx_ref, vx_ref, sink_ref, mask_ref,
                   o_ref, *, nblk):
    j = pl.program_id(1)
    tq = q_ref.shape[1]
    lo = lax.broadcasted_iota(I32, (tq, LANES), 1) < HEAD_DIM
    q = q_ref[0]
    qs = jnp.concatenate([_split_pair(q[:, p * LANES:(p + 1) * LANES], lo) for p in range(4)], axis=0)
    k = jnp.concatenate([kp_ref[0], kc_ref[0], kn_ref[0]], axis=0)
    v = jnp.concatenate([vp_ref[0], vc_ref[0], vn_ref[0]], axis=0)
    col = lax.broadcasted_iota(I32, (1, 3 * tq), 1)
    edge_ok = ((col >= tq) | (j > 0)) & ((col < 2 * tq) | (j < nblk - 1))
    edge = jnp.where(edge_ok, 0.0, NEG_INF)
    s_loc = lax.dot_general(qs, k, _NT, preferred_element_type=F32) + mask_ref[...] + edge
    s_ctx = lax.dot_general(qs, kx_ref[0], _NT, preferred_element_type=F32)
    o = _softmax_pv(s_loc, s_ctx, v, vx_ref[0], sink_ref[...])
    for p in range(4):
        o_lo = o[(2 * p) * tq:(2 * p + 1) * tq]
        o_hi = o[(2 * p + 1) * tq:(2 * p + 2) * tq]
        o_ref[0, :, p * LANES:(p + 1) * LANES] = jnp.where(lo, o_lo, o_hi).astype(BF16)


def _attn_a(zq, zc, sink_col, mask):
    bsz, s, _ = zq.shape
    lc = zc.shape[1]
    tq = A_WINDOW
    nblk = s // tq
    ka, va = _KA, _VA

    def kv_spec(col, shift):
        return pl.BlockSpec((1, tq, LANES), lambda b, j: (b, jnp.clip(j + shift, 0, nblk - 1), col))

    return pl.pallas_call(
        functools.partial(_attn_a_kernel, nblk=nblk),
        out_shape=jax.ShapeDtypeStruct((bsz, s, 4 * LANES), BF16),
        grid=(bsz, nblk),
        in_specs=[pl.BlockSpec((1, tq, 4 * LANES), lambda b, j: (b, j, 0)),
                  kv_spec(ka, -1), kv_spec(ka, 0), kv_spec(ka, 1),
                  kv_spec(va, -1), kv_spec(va, 0), kv_spec(va, 1),
                  pl.BlockSpec((1, lc, LANES), lambda b, j: (b, 0, ka)),
                  pl.BlockSpec((1, lc, LANES), lambda b, j: (b, 0, va)),
                  pl.BlockSpec(sink_col.shape, lambda b, j: (0, 0)),
                  pl.BlockSpec(mask.shape, lambda b, j: (0, 0))],
        out_specs=pl.BlockSpec((1, tq, 4 * LANES), lambda b, j: (b, j, 0)),
        compiler_params=_params("arbitrary", "arbitrary"),
        name="attn_a",
    )(zq, zq, zq, zq, zq, zq, zq, zc, zc, sink_col, mask)


def _attn_a_mask(tq):
    qi = np.arange(tq)[:, None]
    kj = np.arange(3 * tq)[None, :]
    ok = (kj >= qi) & (kj <= qi + 2 * tq)
    m = np.where(ok, 0.0, NEG_INF).astype(np.float32)
    return jnp.asarray(np.tile(m, (A_Q_HEADS, 1)))


def _attn_b_kernel(q_ref, k0_ref, k1_ref, k2_ref, v0_ref, v1_ref, v2_ref, kx_ref, vx_ref, tab_ref, o_ref):
    tq = q_ref.shape[1]
    lo = lax.broadcasted_iota(I32, (tq, LANES), 1) < HEAD_DIM
    for p in range(4):
        sl = slice(p * LANES, (p + 1) * LANES)
        qs = _split_pair(q_ref[0, :, sl], lo)
        k = jnp.concatenate([k0_ref[0, :, sl], k1_ref[0, :, sl], k2_ref[0, :, sl]], axis=0)
        v = jnp.concatenate([v0_ref[0, :, sl], v1_ref[0, :, sl], v2_ref[0, :, sl]], axis=0)
        s_loc = lax.dot_general(qs, k, _NT, preferred_element_type=F32) + tab_ref[0, p]
        s_ctx = lax.dot_general(qs, kx_ref[0, :, sl], _NT, preferred_element_type=F32)
        o = _softmax_pv(s_loc, s_ctx, v, vx_ref[0, :, sl], None)
        o_ref[0, :, sl] = jnp.where(lo, o[:tq], o[tq:]).astype(BF16)


def _attn_b(zq, zc, table):
    bsz, s, _ = zq.shape
    lc = zc.shape[1]
    tq = NA_QROWS * GRID_W
    ng = s // tq
    qb, kb, vb = _QB // 4, _KB // 4, _VB // 4

    def kv_spec(col, off):
        return pl.BlockSpec((1, tq, 4 * LANES), lambda i, b: (b, jnp.clip(i - 1, 0, ng - 3) + off, col))

    def variant(i):
        return jnp.where(i == 0, 0, jnp.where(i == ng - 1, 2, 1))

    return pl.pallas_call(
        _attn_b_kernel,
        out_shape=jax.ShapeDtypeStruct((bsz, s, 4 * LANES), BF16),
        grid=(ng, bsz),
        in_specs=[pl.BlockSpec((1, tq, 4 * LANES), lambda i, b: (b, i, qb)),
                  kv_spec(kb, 0), kv_spec(kb, 1), kv_spec(kb, 2),
                  kv_spec(vb, 0), kv_spec(vb, 1), kv_spec(vb, 2),
                  pl.BlockSpec((1, lc, 4 * LANES), lambda i, b: (b, 0, kb)),
                  pl.BlockSpec((1, lc, 4 * LANES), lambda i, b: (b, 0, vb)),
                  pl.BlockSpec((1,) + table.shape[1:], lambda i, b: (variant(i), 0, 0, 0))],
        out_specs=pl.BlockSpec((1, tq, 4 * LANES), lambda i, b: (b, i, 0)),
        compiler_params=_params("arbitrary", "arbitrary"),
        name="attn_b",
    )(zq, zq, zq, zq, zq, zq, zq, zc, zc, table)


def _na_table(rpb, rows):
    ng = rows // NA_QROWS
    nq, nk = NA_QROWS * GRID_W, NA_KROWS * GRID_W
    qc = np.arange(GRID_W)
    kc = np.arange(GRID_W)
    ws = np.clip(qc - NA_KW // 2, 0, GRID_W - NA_KW)
    valid_c = (kc[None, :] >= ws[:, None]) & (kc[None, :] < ws[:, None] + NA_KW)
    dc = np.clip(kc[None, :] - qc[:, None], -(NA_KW - 1), NA_KW - 1) + (NA_KW - 1)
    c_sel = (dc[..., None] == np.arange(2 * NA_KW - 1)) & valid_c[..., None]
    r_sels, valids = [], []
    for i in (0, 1, ng - 1):
        start = int(np.clip(NA_QROWS * i - NA_KH // 2, 0, rows - NA_KROWS))
        r = NA_QROWS * i + np.arange(NA_QROWS)
        rs = np.clip(r - NA_KH // 2, 0, rows - NA_KH)
        krow = start + np.arange(NA_KROWS)
        valid_r = (krow[None, :] >= rs[:, None]) & (krow[None, :] < rs[:, None] + NA_KH)
        dr = krow[None, :] - r[:, None] + (NA_KH - 1)
        r_sels.append((dr[..., None] == np.arange(2 * NA_KH - 1)) & valid_r[..., None])
        valids.append((valid_r[:, None, :, None] & valid_c[None, :, None, :]).reshape(nq, nk))
    r_sel = jnp.asarray(np.stack(r_sels), F32)
    bias = jnp.einsum('wxya,hab,uvb->whxuyv', r_sel, rpb.astype(F32), jnp.asarray(c_sel, F32),
                      precision=lax.Precision.HIGHEST).reshape(3, B_HEADS, nq, nk)
    tab = jnp.where(jnp.asarray(np.stack(valids))[:, None], bias, NEG_INF)
    return tab.reshape(3, B_HEADS // 2, 2 * nq, nk)


def _merge_kernel(oa_ref, ob_ref, x_ref, mod_ref, ga_ref, gb_ref, wo_ref, n2_ref, wr_ref, rb_ref,
                  wsg_ref, wsu_ref, wsd_ref, xres_ref, hp_ref, idx_ref, gate_ref, *, d):
    b = pl.program_id(0)
    tm = x_ref.shape[1]

    def mod(k):
        return mod_ref[pl.ds(b, 1), k * d:(k + 1) * d]

    na = _rms(oa_ref[0].astype(F32), ga_ref[...])
    nb = _rms(ob_ref[0].astype(F32), gb_ref[...])
    cat = jnp.concatenate([na, nb], axis=1).astype(BF16)
    y = jnp.dot(cat, wo_ref[...], preferred_element_type=F32)
    x1 = x_ref[0] + mod(2) * y
    h2 = _rms(x1, n2_ref[...]) * (1.0 + mod(4)) + mod(3)

    hb = h2.astype(BF16)
    act = _silu(jnp.dot(hb, wsg_ref[...], preferred_element_type=F32)) * jnp.dot(hb, wsu_ref[...],
                                                                                 preferred_element_type=F32)
    shared = jnp.dot(act.astype(BF16), wsd_ref[...], preferred_element_type=F32)
    xres_ref[0] = x1 + mod(5) * shared
    hp_ref[...] = _pack_bf16_pairs(h2)

    logits = lax.dot_general(wr_ref[...], h2, _NT, preferred_element_type=F32, precision=lax.Precision.HIGHEST)
    scores = jax.nn.sigmoid(logits)
    sel = scores + rb_ref[...]
    per = N_EXPERTS // N_GROUPS
    g3 = sel.reshape(N_GROUPS, per, tm)
    it3 = lax.broadcasted_iota(I32, (N_GROUPS, per, tm), 1)
    m1 = jnp.max(g3, axis=1, keepdims=True)
    first = jnp.min(jnp.where(g3 == m1, it3, per), axis=1, keepdims=True)
    m2 = jnp.max(jnp.where(it3 == first, -jnp.inf, g3), axis=1, keepdims=True)
    gscore = (m1 + m2).reshape(N_GROUPS, tm)

    itg = lax.broadcasted_iota(I32, (N_GROUPS, tm), 0)
    gsel = jnp.zeros((N_GROUPS, tm), F32)
    cur = gscore
    for _ in range(TOPK_GROUPS):
        mx = jnp.max(cur, axis=0, keepdims=True)
        fi = jnp.min(jnp.where(cur == mx, itg, N_GROUPS), axis=0, keepdims=True)
        pick = itg == fi
        gsel = jnp.where(pick, 1.0, gsel)
        cur = jnp.where(pick, -jnp.inf, cur)
    emask = jnp.broadcast_to(gsel.reshape(N_GROUPS, 1, tm), (N_GROUPS, per, tm)).reshape(N_EXPERTS, tm) > 0.5

    ite = lax.broadcasted_iota(I32, (N_EXPERTS, tm), 0)
    cur = jnp.where(emask, sel, NEG_INF)
    idx_rows, s_rows = [], []
    for _ in range(TOP_K):
        mx = jnp.max(cur, axis=0, keepdims=True)
        fi = jnp.min(jnp.where(cur == mx, ite, N_EXPERTS), axis=0, keepdims=True)
        pick = ite == fi
        idx_rows.append(fi)
        s_rows.append(jnp.sum(jnp.where(pick, scores, 0.0), axis=0, keepdims=True))
        cur = jnp.where(pick, -jnp.inf, cur)
    top_s = jnp.concatenate(s_rows, axis=0)
    idx_ref[...] = jnp.concatenate(idx_rows, axis=0)
    gate_ref[...] = top_s / jnp.sum(top_s, axis=0, keepdims=True) * ROUTED_SCALE


def _merge(o_a, o_b, x, mod, ga, gb, wo, n2, wr_t, rb_col, wsg, wsu, wsd, *, tm):
    bsz, s, d = x.shape
    nt = s // tm
    n = bsz * s
    full = lambda a: pl.BlockSpec(a.shape, lambda b, i: (0,) * a.ndim)
    return pl.pallas_call(
        functools.partial(_merge_kernel, d=d),
        out_shape=(jax.ShapeDtypeStruct((bsz, s, d), F32),
                   jax.ShapeDtypeStruct((n, d // 2), U32),
                   jax.ShapeDtypeStruct((TOP_K, n), I32),
                   jax.ShapeDtypeStruct((TOP_K, n), F32)),
        grid=(bsz, nt),
        in_specs=[pl.BlockSpec((1, tm, d // 2), lambda b, i: (b, i, 0)),
                  pl.BlockSpec((1, tm, d // 2), lambda b, i: (b, i, 0)),
                  pl.BlockSpec((1, tm, d), lambda b, i: (b, i, 0)),
                  full(mod), full(ga), full(gb), full(wo), full(n2), full(wr_t), full(rb_col),
                  full(wsg), full(wsu), full(wsd)],
        out_specs=(pl.BlockSpec((1, tm, d), lambda b, i: (b, i, 0)),
                   pl.BlockSpec((tm, d // 2), lambda b, i: (b * nt + i, 0)),
                   pl.BlockSpec((TOP_K, tm), lambda b, i: (0, b * nt + i)),
                   pl.BlockSpec((TOP_K, tm), lambda b, i: (0, b * nt + i))),
        compiler_params=_params("arbitrary", "arbitrary"),
        name="merge",
    )(o_a, o_b, x, mod, ga, gb, wo, n2, wr_t, rb_col, wsg, wsu, wsd)


def _rank_kernel(idx_ref, rank_ref, cnt_ref, carry_ref):
    tm = idx_ref.shape[1]

    @pl.when(pl.program_id(0) == 0)
    def _():
        carry_ref[...] = jnp.zeros_like(carry_ref)

    idx = idx_ref[...]
    ite = lax.broadcasted_iota(I32, (N_EXPERTS, tm), 0)
    before = (lax.broadcasted_iota(I32, (tm, tm), 0) < lax.broadcasted_iota(I32, (tm, tm), 1)).astype(BF16)
    base = carry_ref[...]
    rows = []
    for k in range(TOP_K):
        oh = ite == idx[k:k + 1, :]
        ohf = jnp.where(oh, 1.0, 0.0)
        cum = jnp.dot(ohf.astype(BF16), before, preferred_element_type=F32)
        rows.append(jnp.sum(jnp.where(oh, cum + base, 0.0), axis=0, keepdims=True))
        base = base + jnp.sum(ohf, axis=1, keepdims=True)
    rank_ref[...] = jnp.concatenate(rows, axis=0).astype(I32)
    carry_ref[...] = base
    cnt_ref[...] = base


def _rank(idx, *, tm):
    n = idx.shape[1]
    return pl.pallas_call(
        _rank_kernel,
        out_shape=(jax.ShapeDtypeStruct((TOP_K, n), I32), jax.ShapeDtypeStruct((N_EXPERTS, 1), F32)),
        grid=(n // tm,),
        in_specs=[pl.BlockSpec((TOP_K, tm), lambda i: (0, i))],
        out_specs=(pl.BlockSpec((TOP_K, tm), lambda i: (0, i)), pl.BlockSpec((N_EXPERTS, 1), lambda i: (0, 0))),
        scratch_shapes=[pltpu.VMEM((N_EXPERTS, 1), F32)],
        compiler_params=_params("arbitrary"),
        name="rank",
    )(idx)


def _pos_kernel(idx_ref, rank_ref, pstart_ref, pos_ref):
    tm = idx_ref.shape[1]
    idx = idx_ref[...]
    ite = lax.broadcasted_iota(I32, (N_EXPERTS, tm), 0)
    pstart = pstart_ref[...]
    rows = [jnp.sum(jnp.where(ite == idx[k:k + 1, :], pstart, 0.0), axis=0, keepdims=True) for k in range(TOP_K)]
    pos_ref[...] = jnp.concatenate(rows, axis=0).astype(I32) + rank_ref[...]


def _pos(idx, rank, pstart_col, *, tm):
    n = idx.shape[1]
    return pl.pallas_call(
        _pos_kernel,
        out_shape=jax.ShapeDtypeStruct((TOP_K, n), I32),
        grid=(n // tm,),
        in_specs=[pl.BlockSpec((TOP_K, tm), lambda i: (0, i)),
                  pl.BlockSpec((TOP_K, tm), lambda i: (0, i)),
                  pl.BlockSpec((N_EXPERTS, 1), lambda i: (0, 0))],
        out_specs=pl.BlockSpec((TOP_K, tm), lambda i: (0, i)),
        compiler_params=_params("arbitrary"),
        name="pos",
    )(idx, rank, pstart_col)


def _dispatch_kernel(pos_ref, h_ref, xs_in_ref, xs_ref, sem):
    del xs_in_ref
    tm = h_ref.shape[0]

    def row_copy(t, p):
        return pltpu.make_async_copy(h_ref.at[pl.ds(t, 1)], xs_ref.at[pl.ds(p, 1)], sem)

    def issue(t, c):
        for k in range(TOP_K):
            row_copy(t, pos_ref[k, t]).start()
        return c

    def drain(t, c):
        for k in range(TOP_K):
            row_copy(t, pos_ref[k, t]).wait()
        return c

    lax.fori_loop(0, tm, issue, 0)
    lax.fori_loop(0, tm, drain, 0)


def _dispatch(pos, hp, xs_init, *, tm):
    n, w = hp.shape
    return pl.pallas_call(
        _dispatch_kernel,
        out_shape=jax.ShapeDtypeStruct(xs_init.shape, U32),
        grid=(n // tm,),
        in_specs=[pl.BlockSpec((TOP_K, tm), lambda i: (0, i), memory_space=pltpu.SMEM),
                  pl.BlockSpec((tm, w), lambda i: (i, 0)),
                  pl.BlockSpec(memory_space=pl.ANY)],
        out_specs=pl.BlockSpec(memory_space=pl.ANY),
        scratch_shapes=[pltpu.SemaphoreType.DMA(())],
        input_output_aliases={2: 0},
        compiler_params=_params("arbitrary"),
        name="dispatch",
    )(pos, hp, xs_init)


def _expert_kernel(be_ref, nu_ref, xs_ref, wg_ref, wu_ref, wd_ref, ya_ref, yb_ref, wg_s, wu_s, wd_s):
    i = pl.program_id(0)
    used = i < nu_ref[0]

    @pl.when(used & ((i == 0) | (be_ref[i] != be_ref[jnp.maximum(i - 1, 0)])))
    def _():
        wg_s[...] = wg_ref[0].astype(BF16)
        wu_s[...] = wu_ref[0].astype(BF16)
        wd_s[...] = wd_ref[0].astype(BF16)

    @pl.when(used)
    def _():
        x = _unpack_bf16_pairs(xs_ref[...]).astype(BF16)
        g = jnp.dot(x, wg_s[...], preferred_element_type=F32)
        u = jnp.dot(x, wu_s[...], preferred_element_type=F32)
        y = jnp.dot((_silu(g) * u).astype(BF16), wd_s[...], preferred_element_type=F32)
        packed = _pack_bf16_pairs(y)
        half = packed.shape[1] // 2
        ya_ref[...] = packed[:, :half]
        yb_ref[...] = packed[:, half:]

    @pl.when(i >= nu_ref[0])
    def _():
        ya_ref[...] = jnp.zeros_like(ya_ref)
        yb_ref[...] = jnp.zeros_like(yb_ref)


def _experts(block_expert, n_used, xs, wg, wu, wd):
    rows, w = xs.shape
    t = EXPERT_ROWS
    d, f = wg.shape[1], wg.shape[2]
    grid_spec = pltpu.PrefetchScalarGridSpec(
        num_scalar_prefetch=2,
        grid=(rows // t,),
        in_specs=[pl.BlockSpec((t, w), lambda i, be, nu: (i, 0)),
                  pl.BlockSpec((1, d, f), lambda i, be, nu: (be[i], 0, 0)),
                  pl.BlockSpec((1, d, f), lambda i, be, nu: (be[i], 0, 0)),
                  pl.BlockSpec((1, f, d), lambda i, be, nu: (be[i], 0, 0))],
        out_specs=(pl.BlockSpec((t, w // 2), lambda i, be, nu: (i, 0)),
                   pl.BlockSpec((t, w // 2), lambda i, be, nu: (i, 0))),
        scratch_shapes=[pltpu.VMEM((d, f), BF16), pltpu.VMEM((d, f), BF16), pltpu.VMEM((f, d), BF16)],
    )
    return pl.pallas_call(
        _expert_kernel,
        out_shape=(jax.ShapeDtypeStruct((rows, w // 2), U32), jax.ShapeDtypeStruct((rows, w // 2), U32)),
        grid_spec=grid_spec,
        compiler_params=_params("arbitrary"),
        name="experts",
    )(block_expert, n_used, xs, wg, wu, wd)


def _combine_kernel(pos_ref, gate_ref, xres_ref, mod_ref, ys_ref, o_ref, buf_ref, sem, *, d, tiles_per_batch):
    tm = gate_ref.shape[1]
    b = pl.program_id(0) // tiles_per_batch

    def row_copy(k, t, p):
        return pltpu.make_async_copy(ys_ref.at[pl.ds(p, 1)], buf_ref.at[k, pl.ds(t, 1)], sem)

    def issue(t, c):
        for k in range(TOP_K):
            row_copy(k, t, pos_ref[k, t]).start()
        return c

    def drain(t, c):
        for k in range(TOP_K):
            row_copy(k, t, pos_ref[k, t]).wait()
        return c

    lax.fori_loop(0, tm, issue, 0)
    lax.fori_loop(0, tm, drain, 0)

    gates = gate_ref[...].T
    acc = jnp.zeros((tm, d), F32)
    for k in range(TOP_K):
        acc = acc + _unpack_bf16_pairs(buf_ref[k]) * gates[:, k:k + 1]
    o_ref[...] = xres_ref[...] + mod_ref[pl.ds(b, 1), 5 * d:6 * d] * acc


def _combine(pos, gates, xres, mod, ys, *, tm, tiles_per_batch):
    n, d = xres.shape
    return pl.pallas_call(
        functools.partial(_combine_kernel, d=d, tiles_per_batch=tiles_per_batch),
        out_shape=jax.ShapeDtypeStruct((n, d), F32),
        grid=(n // tm,),
        in_specs=[pl.BlockSpec((TOP_K, tm), lambda i: (0, i), memory_space=pltpu.SMEM),
                  pl.BlockSpec((TOP_K, tm), lambda i: (0, i)),
                  pl.BlockSpec((tm, d), lambda i: (i, 0)),
                  pl.BlockSpec(mod.shape, lambda i: (0, 0)),
                  pl.BlockSpec(memory_space=pl.ANY)],
        out_specs=pl.BlockSpec((tm, d), lambda i: (i, 0)),
        scratch_shapes=[pltpu.VMEM((TOP_K, tm, d // 2), U32), pltpu.SemaphoreType.DMA(())],
        compiler_params=_params("arbitrary"),
        name="combine",
    )(pos, gates, xres, mod, ys)


def _sc_gather_rows(src, idx_flat):
    m = idx_flat.shape[1]
    w = src.shape[1]
    mesh = plsc.VectorSubcoreMesh(core_axis_name="core", subcore_axis_name="subcore")

    @functools.partial(pl.kernel, out_type=jax.ShapeDtypeStruct((m, w), src.dtype), mesh=mesh)
    def gather_kernel(x_hbm, i_hbm, o_hbm):
        def body(i_vmem, o_vmem):
            pltpu.sync_copy(x_hbm.at[i_vmem.at[0]], o_vmem)

        pltpu.emit_pipeline(
            body,
            grid=(m // SC_WINDOW,),
            in_specs=[pl.BlockSpec((1, SC_WINDOW), lambda i: (0, i))],
            out_specs=[pl.BlockSpec((SC_WINDOW, w), lambda i: (i, 0))],
            core_axis_name=("core", "subcore"),
            dimension_semantics=(pltpu.PARALLEL,),
        )(i_hbm, o_hbm)

    return gather_kernel(src, idx_flat)


def _finish_kernel(gate_ref, xres_ref, mod_ref, ya_ref, yb_ref, o_ref, *, d, tiles_per_batch):
    tm = gate_ref.shape[1]
    b = pl.program_id(0) // tiles_per_batch
    gates = gate_ref[...].T
    acc = jnp.zeros((tm, d), F32)
    for k in range(TOP_K):
        packed = jnp.concatenate([ya_ref[k], yb_ref[k]], axis=1)
        acc = acc + _unpack_bf16_pairs(packed) * gates[:, k:k + 1]
    o_ref[...] = xres_ref[...] + mod_ref[pl.ds(b, 1), 5 * d:6 * d] * acc


def _finish(gates, xres, mod, yg_a, yg_b, *, tm, tiles_per_batch):
    n, d = xres.shape
    return pl.pallas_call(
        functools.partial(_finish_kernel, d=d, tiles_per_batch=tiles_per_batch),
        out_shape=jax.ShapeDtypeStruct((n, d), F32),
        grid=(n // tm,),
        in_specs=[pl.BlockSpec((TOP_K, tm), lambda i: (0, i)),
                  pl.BlockSpec((tm, d), lambda i: (i, 0)),
                  pl.BlockSpec(mod.shape, lambda i: (0, 0)),
                  pl.BlockSpec((TOP_K, tm, d // 4), lambda i: (0, i, 0)),
                  pl.BlockSpec((TOP_K, tm, d // 4), lambda i: (0, i, 0))],
        out_specs=pl.BlockSpec((tm, d), lambda i: (i, 0)),
        compiler_params=_params("arbitrary"),
        name="finish",
    )(gates, xres, mod, yg_a, yg_b)


_QA_HEAD_ORDER = (0, 4, 1, 5, 2, 6, 3, 7)


def _reorder_qa_heads(a, axis):
    return jnp.concatenate([lax.slice_in_dim(a, h * HEAD_DIM, (h + 1) * HEAD_DIM, axis=axis)
                            for h in _QA_HEAD_ORDER], axis=axis)


def _rope_swap(a):
    quarter = HEAD_DIM // 4
    lead = a.shape[:-1]
    return jnp.flip(a.reshape(lead + (-1, 2, quarter)), axis=-2).reshape(a.shape)


def _rope_tables(s):
    quarter = HEAD_DIM // 4
    t = jnp.arange(s)
    row = (t // GRID_W).astype(F32)
    col = (t % GRID_W).astype(F32)
    freqs = ROPE_BASE ** (-jnp.arange(quarter, dtype=F32) / quarter)
    ar = row[:, None] * freqs[None, :]
    ac = col[:, None] * freqs[None, :]
    cos = jnp.concatenate([jnp.cos(ar), jnp.cos(ar), jnp.cos(ac), jnp.cos(ac)], axis=1)
    sin = jnp.concatenate([-jnp.sin(ar), jnp.sin(ar), -jnp.sin(ac), jnp.sin(ac)], axis=1)
    return jnp.tile(cos, (1, 2)), jnp.tile(sin, (1, 2))


def kernel(x, c, ctx, c_ctx, w_ada, b_ada, norm1, norm2, w_in, q_norm_a, k_norm_a, q_norm_b, k_norm_b, sink_a,
           rpb_b, out_norm_a, out_norm_b, w_out, w_router, router_bias, we_gate, we_up, we_down, ws_gate, ws_up,
           ws_down):
    assert w_ada.shape[0] == 1, "single-layer block"
    bsz, s, d = x.shape
    lc = ctx.shape[1]
    n = bsz * s
    rows = s // GRID_W
    assert s % (NA_QROWS * GRID_W) == 0 and rows >= NA_KROWS and bsz <= 4 and d == 1024

    c8 = jnp.concatenate([c, c_ctx[None, :], jnp.zeros((8 - bsz - 1, d), F32)], axis=0)
    mod = _ada(c8, w_ada[0], b_ada[0][None, :])

    w = w_in[0]
    aw, akw, bw = A_Q_HEADS * HEAD_DIM, A_KV_HEADS * HEAD_DIM, B_HEADS * HEAD_DIM
    cuts = np.cumsum([0, aw, akw, akw, bw, bw, bw])
    qa_w, ka_w, va_w, qb_w, kb_w, vb_w = [w[:, cuts[i]:cuts[i + 1]] for i in range(6)]
    qa_w = _reorder_qa_heads(qa_w, 1)
    w_ext = jnp.concatenate([qa_w, qb_w, kb_w, vb_w, ka_w, va_w, _rope_swap(qa_w), _rope_swap(ka_w)],
                            axis=1).astype(BF16)
    scale = HEAD_DIM ** -0.5
    pair = lambda g: jnp.tile(g, 2)
    gains = jnp.stack([pair(q_norm_a[0]) * scale, pair(q_norm_b[0]) * scale, pair(k_norm_b[0]), pair(k_norm_a[0]),
                       pair(_rope_swap(q_norm_a[0])) * scale, pair(_rope_swap(k_norm_a[0])),
                       jnp.zeros((LANES,), F32), jnp.zeros((LANES,), F32)])
    bd = jnp.asarray(np.kron(np.eye(2), np.full((HEAD_DIM, HEAD_DIM), 1.0 / HEAD_DIM)), BF16)
    cos_t, sin_t = _rope_tables(s)
    n1 = norm1[0][None, :]
    zq = _inproj(x, mod, n1, w_ext, gains, cos_t, sin_t, bd, mod_row=None, tm=512)
    zc = _inproj(ctx, mod, n1, w_ext, gains, jnp.ones((lc, LANES), F32), jnp.zeros((lc, LANES), F32), bd,
                 mod_row=bsz, tm=lc)

    sink = sink_a[0].astype(F32)
    sink_col = jnp.concatenate([jnp.broadcast_to(sink[h], (A_WINDOW,)) for h in _QA_HEAD_ORDER])[:, None]
    o_a = _attn_a(zq, zc, sink_col, _attn_a_mask(A_WINDOW))
    o_b = _attn_b(zq, zc, _na_table(rpb_b[0], rows))

    ga = _reorder_qa_heads(out_norm_a[0], 0)[None, :]
    gb = out_norm_b[0][None, :]
    wo = jnp.concatenate([_reorder_qa_heads(w_out[0][:aw], 0), w_out[0][aw:]], axis=0).astype(BF16)
    xres, hp, idx, gates = _merge(o_a, o_b, x, mod, ga, gb, wo, norm2[0][None, :], w_router[0].T,
                                  router_bias[0][:, None], ws_gate[0].astype(BF16), ws_up[0].astype(BF16),
                                  ws_down[0].astype(BF16), tm=512)

    rank, counts = _rank(idx, tm=512)
    t = EXPERT_ROWS
    cnt = counts[:, 0].astype(I32)
    padded = (cnt + t - 1) // t * t
    pends = jnp.cumsum(padded)
    pstart = pends - padded
    n_blocks = n * TOP_K // t + N_EXPERTS
    block_start = jnp.arange(n_blocks, dtype=I32) * t
    block_expert = jnp.minimum(jnp.sum((pends[None, :] <= block_start[:, None]).astype(I32), axis=1), N_EXPERTS - 1)
    n_used = (pends[-1:] // t).astype(I32)
    pos = _pos(idx, rank, pstart.astype(F32)[:, None], tm=512)

    xs = _dispatch(pos, hp, jnp.zeros((n_blocks * t, d // 2), U32), tm=128)
    ys_a, ys_b = _experts(block_expert, n_used, xs, we_gate[0], we_up[0], we_down[0])
    pos_flat = pos.reshape(1, TOP_K * n)
    yg_a = _sc_gather_rows(ys_a, pos_flat).reshape(TOP_K, n, d // 4)
    yg_b = _sc_gather_rows(ys_b, pos_flat).reshape(TOP_K, n, d // 4)
    out = _finish(gates, xres.reshape(n, d), mod, yg_a, yg_b, tm=256, tiles_per_batch=s // 256)
    return out.reshape(bsz, s, d)
```

```python
import functools

import numpy as np
import jax
import jax.numpy as jnp
from jax import lax
from jax.experimental import pallas as pl
from jax.experimental.pallas import tpu as pltpu
from jax.experimental.pallas import tpu_sc as plsc

F32 = jnp.float32
BF16 = jnp.bfloat16
I32 = jnp.int32
U32 = jnp.uint32

LANES = 128
HEAD_DIM = 64
HEAD_PAIR = 2 * HEAD_DIM
GRID_W = 64
A_Q_HEADS = 8
A_KV_HEADS = 2
A_WINDOW = 128
B_HEADS = 8
NA_KH = 8
NA_KW = 16
NA_QROWS = 4
NA_KROWS = NA_QROWS + NA_KH
ROPE_BASE = 10000.0
N_EXPERTS = 256
TOP_K = 8
N_GROUPS = 8
TOPK_GROUPS = 4
ROUTED_SCALE = 2.5
EPS = 1e-6
NEG_INF = -1e30
EXPERT_ROWS = 512
SC_WINDOW = 128
VMEM_LIMIT = 56 * 1024 * 1024

_NT = (((1,), (1,)), ((), ()))


def _params(*sem):
    return pltpu.CompilerParams(dimension_semantics=sem, vmem_limit_bytes=VMEM_LIMIT)


def _silu(v):
    return v * jax.nn.sigmoid(v)


def _rms(v, gain):
    return v * lax.rsqrt(jnp.mean(v * v, axis=-1, keepdims=True) + EPS) * gain


def _pack_bf16_pairs(v):
    n = v.shape[1] // 2
    lo = lax.bitcast_convert_type(v[:, :n].astype(BF16).astype(F32), U32) >> 16
    hi = lax.bitcast_convert_type(v[:, n:].astype(BF16).astype(F32), U32) & jnp.uint32(0xFFFF0000)
    return hi | lo


def _unpack_bf16_pairs(w):
    lo = lax.bitcast_convert_type(w << 16, F32)
    hi = lax.bitcast_convert_type(w & jnp.uint32(0xFFFF0000), F32)
    return jnp.concatenate([lo, hi], axis=1)


def _ada_kernel(c_ref, w_ref, b_ref, o_ref):
    a = _silu(c_ref[...])
    o_ref[...] = jnp.dot(a, w_ref[...], preferred_element_type=F32,
                         precision=lax.Precision.HIGHEST) + b_ref[...]


def _ada(c8, w, b):
    d, n = w.shape
    bn = n // 4
    return pl.pallas_call(
        _ada_kernel,
        out_shape=jax.ShapeDtypeStruct((8, n), F32),
        grid=(n // bn,),
        in_specs=[pl.BlockSpec((8, d), lambda j: (0, 0)),
                  pl.BlockSpec((d, bn), lambda j: (0, j)),
                  pl.BlockSpec((1, bn), lambda j: (0, j))],
        out_specs=pl.BlockSpec((8, bn), lambda j: (0, j)),
        compiler_params=_params("arbitrary"),
        name="ada",
    )(c8, w, b)


_QA, _QB, _KB, _VB, _KA, _VA, _QA_SW, _KA_SW = 0, 4, 8, 12, 16, 17, 18, 22
_OUT_BLOCKS = 18
_EXT_BLOCKS = 23


def _inproj_kernel(x_ref, mod_ref, n1_ref, w_ref, g_ref, cos_ref, sin_ref, bd_ref, o_ref, *, mod_row, d):
    b = pl.program_id(0) if mod_row is None else mod_row
    xn = _rms(x_ref[0], n1_ref[...])
    sh = mod_ref[pl.ds(b, 1), 0:d]
    sc = mod_ref[pl.ds(b, 1), d:2 * d]
    h = (xn * (1.0 + sc) + sh).astype(BF16)
    z = jnp.dot(h, w_ref[...], preferred_element_type=F32)
    bd = bd_ref[...]
    cos = cos_ref[...]
    sin = sin_ref[...]

    def blk(j):
        return z[:, j * LANES:(j + 1) * LANES]

    def head_rinv(zb):
        ms = jnp.dot((zb * zb).astype(BF16), bd, preferred_element_type=F32)
        return lax.rsqrt(ms + EPS)

    def put(j, v):
        o_ref[0, :, j * LANES:(j + 1) * LANES] = v.astype(BF16)

    def roped(j, j_sw, g_row, g_sw_row):
        zb = blk(j)
        r = head_rinv(zb)
        put(j, (zb * r * g_ref[g_row:g_row + 1, :]) * cos + (blk(j_sw) * r * g_ref[g_sw_row:g_sw_row + 1, :]) * sin)

    def normed(j, g_row):
        zb = blk(j)
        put(j, zb * head_rinv(zb) * g_ref[g_row:g_row + 1, :])

    for j in range(4):
        roped(_QA + j, _QA_SW + j, 0, 4)
        normed(_QB + j, 1)
        normed(_KB + j, 2)
        put(_VB + j, blk(_VB + j))
    roped(_KA, _KA_SW, 3, 5)
    put(_VA, blk(_VA))


def _inproj(x, mod, n1, w_ext, gains, cos_t, sin_t, bd, *, mod_row, tm):
    bsz, s, d = x.shape
    kern = functools.partial(_inproj_kernel, mod_row=mod_row, d=d)
    return pl.pallas_call(
        kern,
        out_shape=jax.ShapeDtypeStruct((bsz, s, _OUT_BLOCKS * LANES), BF16),
        grid=(bsz, s // tm),
        in_specs=[pl.BlockSpec((1, tm, d), lambda b, i: (b, i, 0)),
                  pl.BlockSpec(mod.shape, lambda b, i: (0, 0)),
                  pl.BlockSpec((1, d), lambda b, i: (0, 0)),
                  pl.BlockSpec(w_ext.shape, lambda b, i: (0, 0)),
                  pl.BlockSpec(gains.shape, lambda b, i: (0, 0)),
                  pl.BlockSpec((tm, LANES), lambda b, i: (i, 0)),
                  pl.BlockSpec((tm, LANES), lambda b, i: (i, 0)),
                  pl.BlockSpec((LANES, LANES), lambda b, i: (0, 0))],
        out_specs=pl.BlockSpec((1, tm, _OUT_BLOCKS * LANES), lambda b, i: (b, i, 0)),
        compiler_params=_params("arbitrary", "arbitrary"),
        name="inproj",
    )(x, mod, n1, w_ext, gains, cos_t, sin_t, bd)


def _split_pair(qp, lo):
    zero = jnp.zeros_like(qp)
    return jnp.concatenate([jnp.where(lo, qp, zero), jnp.where(lo, zero, qp)], axis=0)


def _softmax_pv(s_loc, s_ctx, v_loc, v_ctx, sink):
    m = jnp.maximum(jnp.max(s_loc, axis=-1, keepdims=True), jnp.max(s_ctx, axis=-1, keepdims=True))
    if sink is not None:
        m = jnp.maximum(m, sink)
    p_loc = jnp.exp(s_loc - m)
    p_ctx = jnp.exp(s_ctx - m)
    l = jnp.sum(p_loc, axis=-1, keepdims=True) + jnp.sum(p_ctx, axis=-1, keepdims=True)
    if sink is not None:
        l = l + jnp.exp(sink - m)
    o = (jnp.dot(p_loc.astype(BF16), v_loc, preferred_element_type=F32)
         + jnp.dot(p_ctx.astype(BF16), v_ctx, preferred_element_type=F32))
    return o * (1.0 / l)


def _attn_a_kernel(q_ref, kp_ref, kc_ref, kn_ref, vp_ref, vc_ref, vn_ref, kx_ref, vx_ref, sink_ref, mask_ref,
                   o_ref, *, nblk):
    j = pl.program_id(1)
    tq = q_ref.shape[1]
    lo = lax.broadcasted_iota(I32, (tq, LANES), 1) < HEAD_DIM
    q = q_ref[0]
    qs = jnp.concatenate([_split_pair(q[:, p * LANES:(p + 1) * LANES], lo) for p in range(4)], axis=0)
    k = jnp.concatenate([kp_ref[0], kc_ref[0], kn_ref[0]], axis=0)
    v = jnp.concatenate([vp_ref[0], vc_ref[0], vn_ref[0]], axis=0)
    col = lax.broadcasted_iota(I32, (1, 3 * tq), 1)
    edge_ok = ((col >= tq) | (j > 0)) & ((col < 2 * tq) | (j < nblk - 1))
    edge = jnp.where(edge_ok, 0.0, NEG_INF)
    s_loc = lax.dot_general(qs, k, _NT, preferred_element_type=F32) + mask_ref[...] + edge
    s_ctx = lax.dot_general(qs, kx_ref[0], _NT, preferred_element_type=F32)
    o = _softmax_pv(s_loc, s_ctx, v, vx_ref[0], sink_ref[...])
    for p in range(4):
        o_lo = o[(2 * p) * tq:(2 * p + 1) * tq]
        o_hi = o[(2 * p + 1) * tq:(2 * p + 2) * tq]
        o_ref[0, :, p * LANES:(p + 1) * LANES] = jnp.where(lo, o_lo, o_hi).astype(BF16)


def _attn_a(zq, zc, sink_col, mask):
    bsz, s, _ = zq.shape
    lc = zc.shape[1]
    tq = A_WINDOW
    nblk = s // tq
    ka, va = _KA, _VA

    def kv_spec(col, shift):
        return pl.BlockSpec((1, tq, LANES), lambda b, j: (b, jnp.clip(j + shift, 0, nblk - 1), col))

    return pl.pallas_call(
        functools.partial(_attn_a_kernel, nblk=nblk),
        out_shape=jax.ShapeDtypeStruct((bsz, s, 4 * LANES), BF16),
        grid=(bsz, nblk),
        in_specs=[pl.BlockSpec((1, tq, 4 * LANES), lambda b, j: (b, j, 0)),
                  kv_spec(ka, -1), kv_spec(ka, 0), kv_spec(ka, 1),
                  kv_spec(va, -1), kv_spec(va, 0), kv_spec(va, 1),
                  pl.BlockSpec((1, lc, LANES), lambda b, j: (b, 0, ka)),
                  pl.BlockSpec((1, lc, LANES), lambda b, j: (b, 0, va)),
                  pl.BlockSpec(sink_col.shape, lambda b, j: (0, 0)),
                  pl.BlockSpec(mask.shape, lambda b, j: (0, 0))],
        out_specs=pl.BlockSpec((1, tq, 4 * LANES), lambda b, j: (b, j, 0)),
        compiler_params=_params("arbitrary", "arbitrary"),
        name="attn_a",
    )(zq, zq, zq, zq, zq, zq, zq, zc, zc, sink_col, mask)


def _attn_a_mask(tq):
    qi = np.arange(tq)[:, None]
    kj = np.arange(3 * tq)[None, :]
    ok = (kj >= qi) & (kj <= qi + 2 * tq)
    m = np.where(ok, 0.0, NEG_INF).astype(np.float32)
    return jnp.asarray(np.tile(m, (A_Q_HEADS, 1)))


def _attn_b_kernel(q_ref, k0_ref, k1_ref, k2_ref, v0_ref, v1_ref, v2_ref, kx_ref, vx_ref, tab_ref, o_ref):
    tq = q_ref.shape[1]
    lo = lax.broadcasted_iota(I32, (tq, LANES), 1) < HEAD_DIM
    for p in range(4):
        sl = slice(p * LANES, (p + 1) * LANES)
        qs = _split_pair(q_ref[0, :, sl], lo)
        k = jnp.concatenate([k0_ref[0, :, sl], k1_ref[0, :, sl], k2_ref[0, :, sl]], axis=0)
        v = jnp.concatenate([v0_ref[0, :, sl], v1_ref[0, :, sl], v2_ref[0, :, sl]], axis=0)
        s_loc = lax.dot_general(qs, k, _NT, preferred_element_type=F32) + tab_ref[0, p]
        s_ctx = lax.dot_general(qs, kx_ref[0, :, sl], _NT, preferred_element_type=F32)
        o = _softmax_pv(s_loc, s_ctx, v, vx_ref[0, :, sl], None)
        o_ref[0, :, sl] = jnp.where(lo, o[:tq], o[tq:]).astype(BF16)


def _attn_b(zq, zc, table):
    bsz, s, _ = zq.shape
    lc = zc.shape[1]
    tq = NA_QROWS * GRID_W
    ng = s // tq
    qb, kb, vb = _QB // 4, _KB // 4, _VB // 4

    def kv_spec(col, off):
        return pl.BlockSpec((1, tq, 4 * LANES), lambda i, b: (b, jnp.clip(i - 1, 0, ng - 3) + off, col))

    def variant(i):
        return jnp.where(i == 0, 0, jnp.where(i == ng - 1, 2, 1))

    return pl.pallas_call(
        _attn_b_kernel,
        out_shape=jax.ShapeDtypeStruct((bsz, s, 4 * LANES), BF16),
        grid=(ng, bsz),
        in_specs=[pl.BlockSpec((1, tq, 4 * LANES), lambda i, b: (b, i, qb)),
                  kv_spec(kb, 0), kv_spec(kb, 1), kv_spec(kb, 2),
                  kv_spec(vb, 0), kv_spec(vb, 1), kv_spec(vb, 2),
                  pl.BlockSpec((1, lc, 4 * LANES), lambda i, b: (b, 0, kb)),
                  pl.BlockSpec((1, lc, 4 * LANES), lambda i, b: (b, 0, vb)),
                  pl.BlockSpec((1,) + table.shape[1:], lambda i, b: (variant(i), 0, 0, 0))],
        out_specs=pl.BlockSpec((1, tq, 4 * LANES), lambda i, b: (b, i, 0)),
        compiler_params=_params("arbitrary", "arbitrary"),
        name="attn_b",
    )(zq, zq, zq, zq, zq, zq, zq, zc, zc, table)


def _na_table(rpb, rows):
    ng = rows // NA_QROWS
    nq, nk = NA_QROWS * GRID_W, NA_KROWS * GRID_W
    qc = np.arange(GRID_W)
    kc = np.arange(GRID_W)
    ws = np.clip(qc - NA_KW // 2, 0, GRID_W - NA_KW)
    valid_c = (kc[None, :] >= ws[:, None]) & (kc[None, :] < ws[:, None] + NA_KW)
    dc = np.clip(kc[None, :] - qc[:, None], -(NA_KW - 1), NA_KW - 1) + (NA_KW - 1)
    c_sel = (dc[..., None] == np.arange(2 * NA_KW - 1)) & valid_c[..., None]
    r_sels, valids = [], []
    for i in (0, 1, ng - 1):
        start = int(np.clip(NA_QROWS * i - NA_KH // 2, 0, rows - NA_KROWS))
        r = NA_QROWS * i + np.arange(NA_QROWS)
        rs = np.clip(r - NA_KH // 2, 0, rows - NA_KH)
        krow = start + np.arange(NA_KROWS)
        valid_r = (krow[None, :] >= rs[:, None]) & (krow[None, :] < rs[:, None] + NA_KH)
        dr = krow[None, :] - r[:, None] + (NA_KH - 1)
        r_sels.append((dr[..., None] == np.arange(2 * NA_KH - 1)) & valid_r[..., None])
        valids.append((valid_r[:, None, :, None] & valid_c[None, :, None, :]).reshape(nq, nk))
    r_sel = jnp.asarray(np.stack(r_sels), F32)
    bias = jnp.einsum('wxya,hab,uvb->whxuyv', r_sel, rpb.astype(F32), jnp.asarray(c_sel, F32),
                      precision=lax.Precision.HIGHEST).reshape(3, B_HEADS, nq, nk)
    tab = jnp.where(jnp.asarray(np.stack(valids))[:, None], bias, NEG_INF)
    return tab.reshape(3, B_HEADS // 2, 2 * nq, nk)


def _merge_kernel(oa_ref, ob_ref, x_ref, mod_ref, ga_ref, gb_ref, wo_ref, n2_ref, wr_ref, rb_ref,
                  wsg_ref, wsu_ref, wsd_ref, xres_ref, hpa_ref, hpb_ref, idx_ref, gate_ref, *, d):
    b = pl.program_id(0)
    tm = x_ref.shape[1]

    def mod(k):
        return mod_ref[pl.ds(b, 1), k * d:(k + 1) * d]

    na = _rms(oa_ref[0].astype(F32), ga_ref[...])
    nb = _rms(ob_ref[0].astype(F32), gb_ref[...])
    cat = jnp.concatenate([na, nb], axis=1).astype(BF16)
    y = jnp.dot(cat, wo_ref[...], preferred_element_type=F32)
    x1 = x_ref[0] + mod(2) * y
    h2 = _rms(x1, n2_ref[...]) * (1.0 + mod(4)) + mod(3)

    hb = h2.astype(BF16)
    act = _silu(jnp.dot(hb, wsg_ref[...], preferred_element_type=F32)) * jnp.dot(hb, wsu_ref[...],
                                                                                 preferred_element_type=F32)
    shared = jnp.dot(act.astype(BF16), wsd_ref[...], preferred_element_type=F32)
    xres_ref[0] = x1 + mod(5) * shared
    packed = _pack_bf16_pairs(h2)
    hpa_ref[...] = packed[:, :d // 4]
    hpb_ref[...] = packed[:, d // 4:]

    logits = lax.dot_general(wr_ref[...], h2, _NT, preferred_element_type=F32, precision=lax.Precision.HIGHEST)
    scores = jax.nn.sigmoid(logits)
    sel = scores + rb_ref[...]
    per = N_EXPERTS // N_GROUPS
    g3 = sel.reshape(N_GROUPS, per, tm)
    it3 = lax.broadcasted_iota(I32, (N_GROUPS, per, tm), 1)
    m1 = jnp.max(g3, axis=1, keepdims=True)
    first = jnp.min(jnp.where(g3 == m1, it3, per), axis=1, keepdims=True)
    m2 = jnp.max(jnp.where(it3 == first, -jnp.inf, g3), axis=1, keepdims=True)
    gscore = (m1 + m2).reshape(N_GROUPS, tm)

    itg = lax.broadcasted_iota(I32, (N_GROUPS, tm), 0)
    gsel = jnp.zeros((N_GROUPS, tm), F32)
    cur = gscore
    for _ in range(TOPK_GROUPS):
        mx = jnp.max(cur, axis=0, keepdims=True)
        fi = jnp.min(jnp.where(cur == mx, itg, N_GROUPS), axis=0, keepdims=True)
        pick = itg == fi
        gsel = jnp.where(pick, 1.0, gsel)
        cur = jnp.where(pick, -jnp.inf, cur)
    emask = jnp.broadcast_to(gsel.reshape(N_GROUPS, 1, tm), (N_GROUPS, per, tm)).reshape(N_EXPERTS, tm) > 0.5

    ite = lax.broadcasted_iota(I32, (N_EXPERTS, tm), 0)
    cur = jnp.where(emask, sel, NEG_INF)
    idx_rows, s_rows = [], []
    for _ in range(TOP_K):
        mx = jnp.max(cur, axis=0, keepdims=True)
        fi = jnp.min(jnp.where(cur == mx, ite, N_EXPERTS), axis=0, keepdims=True)
        pick = ite == fi
        idx_rows.append(fi)
        s_rows.append(jnp.sum(jnp.where(pick, scores, 0.0), axis=0, keepdims=True))
        cur = jnp.where(pick, -jnp.inf, cur)
    top_s = jnp.concatenate(s_rows, axis=0)
    idx_ref[...] = jnp.concatenate(idx_rows, axis=0)
    gate_ref[...] = top_s / jnp.sum(top_s, axis=0, keepdims=True) * ROUTED_SCALE


def _merge(o_a, o_b, x, mod, ga, gb, wo, n2, wr_t, rb_col, wsg, wsu, wsd, *, tm):
    bsz, s, d = x.shape
    nt = s // tm
    n = bsz * s
    full = lambda a: pl.BlockSpec(a.shape, lambda b, i: (0,) * a.ndim)
    return pl.pallas_call(
        functools.partial(_merge_kernel, d=d),
        out_shape=(jax.ShapeDtypeStruct((bsz, s, d), F32),
                   jax.ShapeDtypeStruct((n, d // 4), U32),
                   jax.ShapeDtypeStruct((n, d // 4), U32),
                   jax.ShapeDtypeStruct((TOP_K, n), I32),
                   jax.ShapeDtypeStruct((TOP_K, n), F32)),
        grid=(bsz, nt),
        in_specs=[pl.BlockSpec((1, tm, d // 2), lambda b, i: (b, i, 0)),
                  pl.BlockSpec((1, tm, d // 2), lambda b, i: (b, i, 0)),
                  pl.BlockSpec((1, tm, d), lambda b, i: (b, i, 0)),
                  full(mod), full(ga), full(gb), full(wo), full(n2), full(wr_t), full(rb_col),
                  full(wsg), full(wsu), full(wsd)],
        out_specs=(pl.BlockSpec((1, tm, d), lambda b, i: (b, i, 0)),
                   pl.BlockSpec((tm, d // 4), lambda b, i: (b * nt + i, 0)),
                   pl.BlockSpec((tm, d // 4), lambda b, i: (b * nt + i, 0)),
                   pl.BlockSpec((TOP_K, tm), lambda b, i: (0, b * nt + i)),
                   pl.BlockSpec((TOP_K, tm), lambda b, i: (0, b * nt + i))),
        compiler_params=_params("arbitrary", "arbitrary"),
        name="merge",
    )(o_a, o_b, x, mod, ga, gb, wo, n2, wr_t, rb_col, wsg, wsu, wsd)


def _rank_kernel(idx_ref, rank_ref, cnt_ref, carry_ref):
    tm = idx_ref.shape[1]

    @pl.when(pl.program_id(0) == 0)
    def _():
        carry_ref[...] = jnp.zeros_like(carry_ref)

    idx = idx_ref[...]
    ite = lax.broadcasted_iota(I32, (N_EXPERTS, tm), 0)
    before = (lax.broadcasted_iota(I32, (tm, tm), 0) < lax.broadcasted_iota(I32, (tm, tm), 1)).astype(BF16)
    base = carry_ref[...]
    rows = []
    for k in range(TOP_K):
        oh = ite == idx[k:k + 1, :]
        ohf = jnp.where(oh, 1.0, 0.0)
        cum = jnp.dot(ohf.astype(BF16), before, preferred_element_type=F32)
        rows.append(jnp.sum(jnp.where(oh, cum + base, 0.0), axis=0, keepdims=True))
        base = base + jnp.sum(ohf, axis=1, keepdims=True)
    rank_ref[...] = jnp.concatenate(rows, axis=0).astype(I32)
    carry_ref[...] = base
    cnt_ref[...] = base


def _rank(idx, *, tm):
    n = idx.shape[1]
    return pl.pallas_call(
        _rank_kernel,
        out_shape=(jax.ShapeDtypeStruct((TOP_K, n), I32), jax.ShapeDtypeStruct((N_EXPERTS, 1), F32)),
        grid=(n // tm,),
        in_specs=[pl.BlockSpec((TOP_K, tm), lambda i: (0, i))],
        out_specs=(pl.BlockSpec((TOP_K, tm), lambda i: (0, i)), pl.BlockSpec((N_EXPERTS, 1), lambda i: (0, 0))),
        scratch_shapes=[pltpu.VMEM((N_EXPERTS, 1), F32)],
        compiler_params=_params("arbitrary"),
        name="rank",
    )(idx)


def _pos_kernel(idx_ref, rank_ref, pstart_ref, pos_ref):
    tm = idx_ref.shape[1]
    idx = idx_ref[...]
    ite = lax.broadcasted_iota(I32, (N_EXPERTS, tm), 0)
    pstart = pstart_ref[...]
    rows = [jnp.sum(jnp.where(ite == idx[k:k + 1, :], pstart, 0.0), axis=0, keepdims=True) for k in range(TOP_K)]
    pos_ref[...] = jnp.concatenate(rows, axis=0).astype(I32) + rank_ref[...]


def _pos(idx, rank, pstart_col, *, tm):
    n = idx.shape[1]
    return pl.pallas_call(
        _pos_kernel,
        out_shape=jax.ShapeDtypeStruct((TOP_K, n), I32),
        grid=(n // tm,),
        in_specs=[pl.BlockSpec((TOP_K, tm), lambda i: (0, i)),
                  pl.BlockSpec((TOP_K, tm), lambda i: (0, i)),
                  pl.BlockSpec((N_EXPERTS, 1), lambda i: (0, 0))],
        out_specs=pl.BlockSpec((TOP_K, tm), lambda i: (0, i)),
        compiler_params=_params("arbitrary"),
        name="pos",
    )(idx, rank, pstart_col)


def _expert_kernel(be_ref, bv_ref, nu_ref, xa_ref, xb_ref, wg_ref, wu_ref, wd_ref, ya_ref, yb_ref, wg_s, wu_s, wd_s):
    i = pl.program_id(0)
    used = i < nu_ref[0]

    @pl.when(used & ((i == 0) | (be_ref[i] != be_ref[jnp.maximum(i - 1, 0)])))
    def _():
        wg_s[...] = wg_ref[0].astype(BF16)
        wu_s[...] = wu_ref[0].astype(BF16)
        wd_s[...] = wd_ref[0].astype(BF16)

    @pl.when(used)
    def _():
        packed = jnp.concatenate([xa_ref[...], xb_ref[...]], axis=1)
        row = lax.broadcasted_iota(I32, packed.shape, 0)
        packed = jnp.where(row < bv_ref[i], packed, jnp.zeros_like(packed))
        x = _unpack_bf16_pairs(packed).astype(BF16)
        g = jnp.dot(x, wg_s[...], preferred_element_type=F32)
        u = jnp.dot(x, wu_s[...], preferred_element_type=F32)
        y = jnp.dot((_silu(g) * u).astype(BF16), wd_s[...], preferred_element_type=F32)
        packed = _pack_bf16_pairs(y)
        half = packed.shape[1] // 2
        ya_ref[...] = packed[:, :half]
        yb_ref[...] = packed[:, half:]

    @pl.when(i >= nu_ref[0])
    def _():
        ya_ref[...] = jnp.zeros_like(ya_ref)
        yb_ref[...] = jnp.zeros_like(yb_ref)


def _experts(block_expert, block_valid, n_used, xs_a, xs_b, wg, wu, wd):
    rows, hw = xs_a.shape
    t = EXPERT_ROWS
    d, f = wg.shape[1], wg.shape[2]
    row_spec = pl.BlockSpec((t, hw), lambda i, be, bv, nu: (i, 0))
    grid_spec = pltpu.PrefetchScalarGridSpec(
        num_scalar_prefetch=3,
        grid=(rows // t,),
        in_specs=[row_spec, row_spec,
                  pl.BlockSpec((1, d, f), lambda i, be, bv, nu: (be[i], 0, 0)),
                  pl.BlockSpec((1, d, f), lambda i, be, bv, nu: (be[i], 0, 0)),
                  pl.BlockSpec((1, f, d), lambda i, be, bv, nu: (be[i], 0, 0))],
        out_specs=(row_spec, row_spec),
        scratch_shapes=[pltpu.VMEM((d, f), BF16), pltpu.VMEM((d, f), BF16), pltpu.VMEM((f, d), BF16)],
    )
    return pl.pallas_call(
        _expert_kernel,
        out_shape=(jax.ShapeDtypeStruct((rows, hw), U32), jax.ShapeDtypeStruct((rows, hw), U32)),
        grid_spec=grid_spec,
        compiler_params=_params("arbitrary"),
        name="experts",
    )(block_expert, block_valid, n_used, xs_a, xs_b, wg, wu, wd)


def _sc_scatter_rows(x, idx_flat, n_rows):
    n, w = x.shape
    m = idx_flat.shape[1]
    nwin = n // SC_WINDOW
    mesh = plsc.VectorSubcoreMesh(core_axis_name="core", subcore_axis_name="subcore")

    @functools.partial(pl.kernel, out_type=jax.ShapeDtypeStruct((n_rows, w), x.dtype), mesh=mesh,
                       scratch_types=[])
    def scatter_kernel(x_hbm, i_hbm, o_hbm):
        def body(x_vmem, i_vmem):
            pltpu.sync_copy(x_vmem, o_hbm.at[i_vmem.at[0]])

        pltpu.emit_pipeline(
            body,
            grid=(m // SC_WINDOW,),
            in_specs=[pl.BlockSpec((SC_WINDOW, w), lambda i: (i % nwin, 0)),
                      pl.BlockSpec((1, SC_WINDOW), lambda i: (0, i))],
            out_specs=[],
            core_axis_name=("core", "subcore"),
            dimension_semantics=(pltpu.PARALLEL,),
        )(x_hbm, i_hbm)

    return scatter_kernel(x, idx_flat)


def _sc_gather_rows(src, idx_flat):
    m = idx_flat.shape[1]
    w = src.shape[1]
    mesh = plsc.VectorSubcoreMesh(core_axis_name="core", subcore_axis_name="subcore")

    @functools.partial(pl.kernel, out_type=jax.ShapeDtypeStruct((m, w), src.dtype), mesh=mesh)
    def gather_kernel(x_hbm, i_hbm, o_hbm):
        def body(i_vmem, o_vmem):
            pltpu.sync_copy(x_hbm.at[i_vmem.at[0]], o_vmem)

        pltpu.emit_pipeline(
            body,
            grid=(m // SC_WINDOW,),
            in_specs=[pl.BlockSpec((1, SC_WINDOW), lambda i: (0, i))],
            out_specs=[pl.BlockSpec((SC_WINDOW, w), lambda i: (i, 0))],
            core_axis_name=("core", "subcore"),
            dimension_semantics=(pltpu.PARALLEL,),
        )(i_hbm, o_hbm)

    return gather_kernel(src, idx_flat)


def _finish_kernel(gate_ref, xres_ref, mod_ref, ya_ref, yb_ref, o_ref, *, d, tiles_per_batch):
    tm = gate_ref.shape[1]
    b = pl.program_id(0) // tiles_per_batch
    gates = gate_ref[...].T
    acc = jnp.zeros((tm, d), F32)
    for k in range(TOP_K):
        packed = jnp.concatenate([ya_ref[k], yb_ref[k]], axis=1)
        acc = acc + _unpack_bf16_pairs(packed) * gates[:, k:k + 1]
    o_ref[...] = xres_ref[...] + mod_ref[pl.ds(b, 1), 5 * d:6 * d] * acc


def _finish(gates, xres, mod, yg_a, yg_b, *, tm, tiles_per_batch):
    n, d = xres.shape
    return pl.pallas_call(
        functools.partial(_finish_kernel, d=d, tiles_per_batch=tiles_per_batch),
        out_shape=jax.ShapeDtypeStruct((n, d), F32),
        grid=(n // tm,),
        in_specs=[pl.BlockSpec((TOP_K, tm), lambda i: (0, i)),
                  pl.BlockSpec((tm, d), lambda i: (i, 0)),
                  pl.BlockSpec(mod.shape, lambda i: (0, 0)),
                  pl.BlockSpec((TOP_K, tm, d // 4), lambda i: (0, i, 0)),
                  pl.BlockSpec((TOP_K, tm, d // 4), lambda i: (0, i, 0))],
        out_specs=pl.BlockSpec((tm, d), lambda i: (i, 0)),
        compiler_params=_params("arbitrary"),
        name="finish",
    )(gates, xres, mod, yg_a, yg_b)


_QA_HEAD_ORDER = (0, 4, 1, 5, 2, 6, 3, 7)


def _reorder_qa_heads(a, axis):
    return jnp.concatenate([lax.slice_in_dim(a, h * HEAD_DIM, (h + 1) * HEAD_DIM, axis=axis)
                            for h in _QA_HEAD_ORDER], axis=axis)


def _rope_swap(a):
    quarter = HEAD_DIM // 4
    lead = a.shape[:-1]
    return jnp.flip(a.reshape(lead + (-1, 2, quarter)), axis=-2).reshape(a.shape)


def _rope_tables(s):
    quarter = HEAD_DIM // 4
    t = jnp.arange(s)
    row = (t // GRID_W).astype(F32)
    col = (t % GRID_W).astype(F32)
    freqs = ROPE_BASE ** (-jnp.arange(quarter, dtype=F32) / quarter)
    ar = row[:, None] * freqs[None, :]
    ac = col[:, None] * freqs[None, :]
    cos = jnp.concatenate([jnp.cos(ar), jnp.cos(ar), jnp.cos(ac), jnp.cos(ac)], axis=1)
    sin = jnp.concatenate([-jnp.sin(ar), jnp.sin(ar), -jnp.sin(ac), jnp.sin(ac)], axis=1)
    return jnp.tile(cos, (1, 2)), jnp.tile(sin, (1, 2))


def kernel(x, c, ctx, c_ctx, w_ada, b_ada, norm1, norm2, w_in, q_norm_a, k_norm_a, q_norm_b, k_norm_b, sink_a,
           rpb_b, out_norm_a, out_norm_b, w_out, w_router, router_bias, we_gate, we_up, we_down, ws_gate, ws_up,
           ws_down):
    assert w_ada.shape[0] == 1, "single-layer block"
    bsz, s, d = x.shape
    lc = ctx.shape[1]
    n = bsz * s
    rows = s // GRID_W
    assert s % (NA_QROWS * GRID_W) == 0 and rows >= NA_KROWS and bsz <= 4 and d == 1024

    c8 = jnp.concatenate([c, c_ctx[None, :], jnp.zeros((8 - bsz - 1, d), F32)], axis=0)
    mod = _ada(c8, w_ada[0], b_ada[0][None, :])

    w = w_in[0]
    aw, akw, bw = A_Q_HEADS * HEAD_DIM, A_KV_HEADS * HEAD_DIM, B_HEADS * HEAD_DIM
    cuts = np.cumsum([0, aw, akw, akw, bw, bw, bw])
    qa_w, ka_w, va_w, qb_w, kb_w, vb_w = [w[:, cuts[i]:cuts[i + 1]] for i in range(6)]
    qa_w = _reorder_qa_heads(qa_w, 1)
    w_ext = jnp.concatenate([qa_w, qb_w, kb_w, vb_w, ka_w, va_w, _rope_swap(qa_w), _rope_swap(ka_w)],
                            axis=1).astype(BF16)
    scale = HEAD_DIM ** -0.5
    pair = lambda g: jnp.tile(g, 2)
    gains = jnp.stack([pair(q_norm_a[0]) * scale, pair(q_norm_b[0]) * scale, pair(k_norm_b[0]), pair(k_norm_a[0]),
                       pair(_rope_swap(q_norm_a[0])) * scale, pair(_rope_swap(k_norm_a[0])),
                       jnp.zeros((LANES,), F32), jnp.zeros((LANES,), F32)])
    bd = jnp.asarray(np.kron(np.eye(2), np.full((HEAD_DIM, HEAD_DIM), 1.0 / HEAD_DIM)), BF16)
    cos_t, sin_t = _rope_tables(s)
    n1 = norm1[0][None, :]
    zq = _inproj(x, mod, n1, w_ext, gains, cos_t, sin_t, bd, mod_row=None, tm=512)
    zc = _inproj(ctx, mod, n1, w_ext, gains, jnp.ones((lc, LANES), F32), jnp.zeros((lc, LANES), F32), bd,
                 mod_row=bsz, tm=lc)

    sink = sink_a[0].astype(F32)
    sink_col = jnp.concatenate([jnp.broadcast_to(sink[h], (A_WINDOW,)) for h in _QA_HEAD_ORDER])[:, None]
    o_a = _attn_a(zq, zc, sink_col, _attn_a_mask(A_WINDOW))
    o_b = _attn_b(zq, zc, _na_table(rpb_b[0], rows))

    ga = _reorder_qa_heads(out_norm_a[0], 0)[None, :]
    gb = out_norm_b[0][None, :]
    wo = jnp.concatenate([_reorder_qa_heads(w_out[0][:aw], 0), w_out[0][aw:]], axis=0).astype(BF16)
    xres, hp_a, hp_b, idx, gates = _merge(o_a, o_b, x, mod, ga, gb, wo, norm2[0][None, :], w_router[0].T,
                                  router_bias[0][:, None], ws_gate[0].astype(BF16), ws_up[0].astype(BF16),
                                  ws_down[0].astype(BF16), tm=512)

    rank, counts = _rank(idx, tm=512)
    t = EXPERT_ROWS
    cnt = counts[:, 0].astype(I32)
    padded = (cnt + t - 1) // t * t
    pends = jnp.cumsum(padded)
    pstart = pends - padded
    n_blocks = n * TOP_K // t + N_EXPERTS
    block_start = jnp.arange(n_blocks, dtype=I32) * t
    block_expert = jnp.minimum(jnp.sum((pends[None, :] <= block_start[:, None]).astype(I32), axis=1), N_EXPERTS - 1)
    of_expert = (block_expert[:, None] == jnp.arange(N_EXPERTS, dtype=I32)[None, :]).astype(I32)
    block_valid = jnp.clip(jnp.sum(of_expert * (cnt + pstart)[None, :], axis=1) - block_start, 0, t)
    n_used = (pends[-1:] // t).astype(I32)
    pos = _pos(idx, rank, pstart.astype(F32)[:, None], tm=512)

    pos_flat = pos.reshape(1, TOP_K * n)
    xs_a = _sc_scatter_rows(hp_a, pos_flat, n_blocks * t)
    xs_b = _sc_scatter_rows(hp_b, pos_flat, n_blocks * t)
    ys_a, ys_b = _experts(block_expert, block_valid, n_used, xs_a, xs_b, we_gate[0], we_up[0], we_down[0])
    yg_a = _sc_gather_rows(ys_a, pos_flat).reshape(TOP_K, n, d // 4)
    yg_b = _sc_gather_rows(ys_b, pos_flat).reshape(TOP_K, n, d // 4)
    out = _finish(gates, xres.reshape(n, d), mod, yg_a, yg_b, tm=256, tiles_per_batch=s // 256)
    return out.reshape(bsz, s, d)
```

```python
import functools

import numpy as np
import jax
import jax.numpy as jnp
from jax import lax
from jax.experimental import pallas as pl
from jax.experimental.pallas import tpu as pltpu
from jax.experimental.pallas import tpu_sc as plsc

F32 = jnp.float32
BF16 = jnp.bfloat16
I32 = jnp.int32
U32 = jnp.uint32

LANES = 128
HEAD_DIM = 64
HEAD_PAIR = 2 * HEAD_DIM
GRID_W = 64
A_Q_HEADS = 8
A_KV_HEADS = 2
A_WINDOW = 128
B_HEADS = 8
NA_KH = 8
NA_KW = 16
NA_QROWS = 4
NA_KROWS = NA_QROWS + NA_KH
ROPE_BASE = 10000.0
N_EXPERTS = 256
TOP_K = 8
N_GROUPS = 8
TOPK_GROUPS = 4
ROUTED_SCALE = 2.5
LOG2E = 1.4426950408889634
EPS = 1e-6
NEG_INF = -1e30
EXPERT_ROWS = 512
SC_WINDOW = 128
VMEM_LIMIT = 56 * 1024 * 1024

_NT = (((1,), (1,)), ((), ()))


def _params(*sem):
    return pltpu.CompilerParams(dimension_semantics=sem, vmem_limit_bytes=VMEM_LIMIT)


def _silu(v):
    return v * jax.nn.sigmoid(v)


def _rms(v, gain):
    return v * lax.rsqrt(jnp.mean(v * v, axis=-1, keepdims=True) + EPS) * gain


def _pack_bf16_pairs(v):
    n = v.shape[1] // 2
    lo = lax.bitcast_convert_type(v[:, :n].astype(BF16).astype(F32), U32) >> 16
    hi = lax.bitcast_convert_type(v[:, n:].astype(BF16).astype(F32), U32) & jnp.uint32(0xFFFF0000)
    return hi | lo


def _unpack_bf16_pairs(w):
    lo = lax.bitcast_convert_type(w << 16, F32)
    hi = lax.bitcast_convert_type(w & jnp.uint32(0xFFFF0000), F32)
    return jnp.concatenate([lo, hi], axis=1)


def _ada_kernel(c_ref, w_ref, b_ref, o_ref):
    a = _silu(c_ref[...])
    o_ref[...] = jnp.dot(a, w_ref[...], preferred_element_type=F32,
                         precision=lax.Precision.HIGHEST) + b_ref[...]


def _ada(c8, w, b):
    d, n = w.shape
    bn = n // 4
    return pl.pallas_call(
        _ada_kernel,
        out_shape=jax.ShapeDtypeStruct((8, n), F32),
        grid=(n // bn,),
        in_specs=[pl.BlockSpec((8, d), lambda j: (0, 0)),
                  pl.BlockSpec((d, bn), lambda j: (0, j)),
                  pl.BlockSpec((1, bn), lambda j: (0, j))],
        out_specs=pl.BlockSpec((8, bn), lambda j: (0, j)),
        compiler_params=_params("arbitrary"),
        name="ada",
    )(c8, w, b)


_QA, _QB, _KB, _VB, _KA, _VA, _QA_SW, _KA_SW = 0, 4, 8, 12, 16, 17, 18, 22
_OUT_BLOCKS = 18
_EXT_BLOCKS = 23


def _inproj_kernel(x_ref, mod_ref, n1_ref, w_ref, g_ref, cos_ref, sin_ref, bd_ref, o_ref, *, mod_row, d):
    b = pl.program_id(0) if mod_row is None else mod_row
    xn = _rms(x_ref[0], n1_ref[...])
    sh = mod_ref[pl.ds(b, 1), 0:d]
    sc = mod_ref[pl.ds(b, 1), d:2 * d]
    h = (xn * (1.0 + sc) + sh).astype(BF16)
    z = jnp.dot(h, w_ref[...], preferred_element_type=F32)
    bd = bd_ref[...]
    cos = cos_ref[...]
    sin = sin_ref[...]

    def blk(j):
        return z[:, j * LANES:(j + 1) * LANES]

    def head_rinv(zb):
        ms = jnp.dot((zb * zb).astype(BF16), bd, preferred_element_type=F32)
        return lax.rsqrt(ms + EPS)

    def put(j, v):
        o_ref[0, :, j * LANES:(j + 1) * LANES] = v.astype(BF16)

    def roped(j, j_sw, g_row, g_sw_row):
        zb = blk(j)
        r = head_rinv(zb)
        put(j, (zb * r * g_ref[g_row:g_row + 1, :]) * cos + (blk(j_sw) * r * g_ref[g_sw_row:g_sw_row + 1, :]) * sin)

    def normed(j, g_row):
        zb = blk(j)
        put(j, zb * head_rinv(zb) * g_ref[g_row:g_row + 1, :])

    for j in range(4):
        roped(_QA + j, _QA_SW + j, 0, 4)
        normed(_QB + j, 1)
        normed(_KB + j, 2)
        put(_VB + j, blk(_VB + j))
    roped(_KA, _KA_SW, 3, 5)
    put(_VA, blk(_VA))


def _inproj(x, mod, n1, w_ext, gains, cos_t, sin_t, bd, *, mod_row, tm):
    bsz, s, d = x.shape
    kern = functools.partial(_inproj_kernel, mod_row=mod_row, d=d)
    return pl.pallas_call(
        kern,
        out_shape=jax.ShapeDtypeStruct((bsz, s, _OUT_BLOCKS * LANES), BF16),
        grid=(bsz, s // tm),
        in_specs=[pl.BlockSpec((1, tm, d), lambda b, i: (b, i, 0)),
                  pl.BlockSpec(mod.shape, lambda b, i: (0, 0)),
                  pl.BlockSpec((1, d), lambda b, i: (0, 0)),
                  pl.BlockSpec(w_ext.shape, lambda b, i: (0, 0)),
                  pl.BlockSpec(gains.shape, lambda b, i: (0, 0)),
                  pl.BlockSpec((tm, LANES), lambda b, i: (i, 0)),
                  pl.BlockSpec((tm, LANES), lambda b, i: (i, 0)),
                  pl.BlockSpec((LANES, LANES), lambda b, i: (0, 0))],
        out_specs=pl.BlockSpec((1, tm, _OUT_BLOCKS * LANES), lambda b, i: (b, i, 0)),
        compiler_params=_params("arbitrary", "arbitrary"),
        name="inproj",
    )(x, mod, n1, w_ext, gains, cos_t, sin_t, bd)


def _split_pair(qp, lo):
    zero = jnp.zeros_like(qp)
    return jnp.concatenate([jnp.where(lo, qp, zero), jnp.where(lo, zero, qp)], axis=0)


def _softmax_pv(s_parts, v_parts, sink_rep):
    chunks = [s[:, c * LANES:(c + 1) * LANES] for s in s_parts for c in range(s.shape[1] // LANES)]
    m = jnp.max(functools.reduce(jnp.maximum, chunks), axis=-1, keepdims=True)
    m_rep = jnp.broadcast_to(m, (m.shape[0], LANES))
    if sink_rep is not None:
        m_rep = jnp.maximum(m_rep, sink_rep)
    acc = None
    for s, v in zip(s_parts, v_parts):
        p = jnp.concatenate([jnp.exp2(s[:, c * LANES:(c + 1) * LANES] - m_rep)
                             for c in range(s.shape[1] // LANES)], axis=1).astype(BF16)
        v_ext = jnp.concatenate([v, jnp.ones_like(v)], axis=1)
        o = jnp.dot(p, v_ext, preferred_element_type=F32)
        acc = o if acc is None else acc + o
    l_rep = acc[:, LANES:]
    if sink_rep is not None:
        l_rep = l_rep + jnp.exp2(sink_rep - m_rep)
    return acc[:, :LANES] * (1.0 / l_rep)


def _attn_a_kernel(q_ref, kp_ref, kc_ref, kn_ref, vp_ref, vc_ref, vn_ref, kx_ref, vx_ref, sink_ref, mask_ref,
                   o_ref):
    tq = q_ref.shape[1]
    lo = lax.broadcasted_iota(I32, (tq, LANES), 1) < HEAD_DIM
    q = q_ref[0]
    qs = jnp.concatenate([_split_pair(q[:, p * LANES:(p + 1) * LANES], lo) for p in range(4)], axis=0)
    k = jnp.concatenate([kp_ref[0], kc_ref[0], kn_ref[0]], axis=0)
    v = jnp.concatenate([vp_ref[0], vc_ref[0], vn_ref[0]], axis=0)
    s_loc = lax.dot_general(qs, k, _NT, preferred_element_type=F32) + mask_ref[0]
    s_ctx = lax.dot_general(qs, kx_ref[0], _NT, preferred_element_type=F32)
    o = _softmax_pv([s_loc, s_ctx], [v, vx_ref[0]], sink_ref[...])
    for p in range(4):
        o_lo = o[(2 * p) * tq:(2 * p + 1) * tq]
        o_hi = o[(2 * p + 1) * tq:(2 * p + 2) * tq]
        o_ref[0, :, p * LANES:(p + 1) * LANES] = jnp.where(lo, o_lo, o_hi).astype(BF16)


def _attn_a(zq, zc, sink_rep, mask):
    bsz, s, _ = zq.shape
    lc = zc.shape[1]
    tq = A_WINDOW
    nblk = s // tq
    ka, va = _KA, _VA

    def kv_spec(col, shift):
        return pl.BlockSpec((1, tq, LANES), lambda b, j: (b, jnp.clip(j + shift, 0, nblk - 1), col))

    def variant(j):
        return jnp.where(j == 0, 0, jnp.where(j == nblk - 1, 2, 1))

    return pl.pallas_call(
        _attn_a_kernel,
        out_shape=jax.ShapeDtypeStruct((bsz, s, 4 * LANES), BF16),
        grid=(bsz, nblk),
        in_specs=[pl.BlockSpec((1, tq, 4 * LANES), lambda b, j: (b, j, 0)),
                  kv_spec(ka, -1), kv_spec(ka, 0), kv_spec(ka, 1),
                  kv_spec(va, -1), kv_spec(va, 0), kv_spec(va, 1),
                  pl.BlockSpec((1, lc, LANES), lambda b, j: (b, 0, ka)),
                  pl.BlockSpec((1, lc, LANES), lambda b, j: (b, 0, va)),
                  pl.BlockSpec(sink_rep.shape, lambda b, j: (0, 0)),
                  pl.BlockSpec((1,) + mask.shape[1:], lambda b, j: (variant(j), 0, 0))],
        out_specs=pl.BlockSpec((1, tq, 4 * LANES), lambda b, j: (b, j, 0)),
        compiler_params=_params("arbitrary", "arbitrary"),
        name="attn_a",
    )(zq, zq, zq, zq, zq, zq, zq, zc, zc, sink_rep, mask)


def _attn_a_mask(tq):
    qi = np.arange(tq)[:, None]
    kj = np.arange(3 * tq)[None, :]
    ok = (kj >= qi) & (kj <= qi + 2 * tq)
    variants = [ok & (kj >= tq), ok, ok & (kj < 2 * tq)]
    m = np.stack([np.tile(np.where(v, 0.0, NEG_INF).astype(np.float32), (A_Q_HEADS, 1)) for v in variants])
    return jnp.asarray(m)


def _attn_b_kernel(q_ref, k0_ref, k1_ref, k2_ref, v0_ref, v1_ref, v2_ref, kx_ref, vx_ref, tab_ref, o_ref):
    tq = q_ref.shape[1]
    lo = lax.broadcasted_iota(I32, (tq, LANES), 1) < HEAD_DIM
    for p in range(4):
        sl = slice(p * LANES, (p + 1) * LANES)
        qs = _split_pair(q_ref[0, :, sl], lo)
        k = jnp.concatenate([k0_ref[0, :, sl], k1_ref[0, :, sl], k2_ref[0, :, sl]], axis=0)
        v = jnp.concatenate([v0_ref[0, :, sl], v1_ref[0, :, sl], v2_ref[0, :, sl]], axis=0)
        s_loc = lax.dot_general(qs, k, _NT, preferred_element_type=F32) + tab_ref[0, p]
        s_ctx = lax.dot_general(qs, kx_ref[0, :, sl], _NT, preferred_element_type=F32)
        o = _softmax_pv([s_loc, s_ctx], [v, vx_ref[0, :, sl]], None)
        o_ref[0, :, sl] = jnp.where(lo, o[:tq], o[tq:]).astype(BF16)


def _attn_b(zq, zc, table):
    bsz, s, _ = zq.shape
    lc = zc.shape[1]
    tq = NA_QROWS * GRID_W
    ng = s // tq
    qb, kb, vb = _QB // 4, _KB // 4, _VB // 4

    def kv_spec(col, off):
        return pl.BlockSpec((1, tq, 4 * LANES), lambda i, b: (b, jnp.clip(i - 1, 0, ng - 3) + off, col))

    def variant(i):
        return jnp.where(i == 0, 0, jnp.where(i == ng - 1, 2, 1))

    return pl.pallas_call(
        _attn_b_kernel,
        out_shape=jax.ShapeDtypeStruct((bsz, s, 4 * LANES), BF16),
        grid=(ng, bsz),
        in_specs=[pl.BlockSpec((1, tq, 4 * LANES), lambda i, b: (b, i, qb)),
                  kv_spec(kb, 0), kv_spec(kb, 1), kv_spec(kb, 2),
                  kv_spec(vb, 0), kv_spec(vb, 1), kv_spec(vb, 2),
                  pl.BlockSpec((1, lc, 4 * LANES), lambda i, b: (b, 0, kb)),
                  pl.BlockSpec((1, lc, 4 * LANES), lambda i, b: (b, 0, vb)),
                  pl.BlockSpec((1,) + table.shape[1:], lambda i, b: (variant(i), 0, 0, 0))],
        out_specs=pl.BlockSpec((1, tq, 4 * LANES), lambda i, b: (b, i, 0)),
        compiler_params=_params("arbitrary", "arbitrary"),
        name="attn_b",
    )(zq, zq, zq, zq, zq, zq, zq, zc, zc, table)


def _na_table(rpb, rows):
    ng = rows // NA_QROWS
    nq, nk = NA_QROWS * GRID_W, NA_KROWS * GRID_W
    qc = np.arange(GRID_W)
    kc = np.arange(GRID_W)
    ws = np.clip(qc - NA_KW // 2, 0, GRID_W - NA_KW)
    valid_c = (kc[None, :] >= ws[:, None]) & (kc[None, :] < ws[:, None] + NA_KW)
    dc = np.clip(kc[None, :] - qc[:, None], -(NA_KW - 1), NA_KW - 1) + (NA_KW - 1)
    c_sel = (dc[..., None] == np.arange(2 * NA_KW - 1)) & valid_c[..., None]
    r_sels, valids = [], []
    for i in (0, 1, ng - 1):
        start = int(np.clip(NA_QROWS * i - NA_KH // 2, 0, rows - NA_KROWS))
        r = NA_QROWS * i + np.arange(NA_QROWS)
        rs = np.clip(r - NA_KH // 2, 0, rows - NA_KH)
        krow = start + np.arange(NA_KROWS)
        valid_r = (krow[None, :] >= rs[:, None]) & (krow[None, :] < rs[:, None] + NA_KH)
        dr = krow[None, :] - r[:, None] + (NA_KH - 1)
        r_sels.append((dr[..., None] == np.arange(2 * NA_KH - 1)) & valid_r[..., None])
        valids.append((valid_r[:, None, :, None] & valid_c[None, :, None, :]).reshape(nq, nk))
    r_sel = jnp.asarray(np.stack(r_sels), F32)
    bias = jnp.einsum('wxya,hab,uvb->whxuyv', r_sel, rpb.astype(F32), jnp.asarray(c_sel, F32),
                      precision=lax.Precision.HIGHEST).reshape(3, B_HEADS, nq, nk)
    tab = jnp.where(jnp.asarray(np.stack(valids))[:, None], bias * LOG2E, NEG_INF)
    return tab.reshape(3, B_HEADS // 2, 2 * nq, nk)


def _merge_kernel(oa_ref, ob_ref, x_ref, mod_ref, ga_ref, gb_ref, wo_ref, n2_ref, wr_ref, rb_ref,
                  wsg_ref, wsu_ref, wsd_ref, xres_ref, hpa_ref, hpb_ref, idx_ref, gate_ref, *, d):
    b = pl.program_id(0)
    tm = x_ref.shape[1]

    def mod(k):
        return mod_ref[pl.ds(b, 1), k * d:(k + 1) * d]

    na = _rms(oa_ref[0].astype(F32), ga_ref[...])
    nb = _rms(ob_ref[0].astype(F32), gb_ref[...])
    cat = jnp.concatenate([na, nb], axis=1).astype(BF16)
    y = jnp.dot(cat, wo_ref[...], preferred_element_type=F32)
    x1 = x_ref[0] + mod(2) * y
    h2 = _rms(x1, n2_ref[...]) * (1.0 + mod(4)) + mod(3)

    hb = h2.astype(BF16)
    act = _silu(jnp.dot(hb, wsg_ref[...], preferred_element_type=F32)) * jnp.dot(hb, wsu_ref[...],
                                                                                 preferred_element_type=F32)
    shared = jnp.dot(act.astype(BF16), wsd_ref[...], preferred_element_type=F32)
    xres_ref[0] = x1 + mod(5) * shared
    packed = _pack_bf16_pairs(h2)
    hpa_ref[...] = packed[:, :d // 4]
    hpb_ref[...] = packed[:, d // 4:]

    logits = lax.dot_general(wr_ref[...], h2, _NT, preferred_element_type=F32, precision=lax.Precision.HIGHEST)
    scores = jax.nn.sigmoid(logits)
    sel = scores + rb_ref[...]
    per = N_EXPERTS // N_GROUPS
    g3 = sel.reshape(N_GROUPS, per, tm)
    it3 = lax.broadcasted_iota(I32, (N_GROUPS, per, tm), 1)
    m1 = jnp.max(g3, axis=1, keepdims=True)
    first = jnp.min(jnp.where(g3 == m1, it3, per), axis=1, keepdims=True)
    m2 = jnp.max(jnp.where(it3 == first, -jnp.inf, g3), axis=1, keepdims=True)
    gscore = (m1 + m2).reshape(N_GROUPS, tm)

    itg = lax.broadcasted_iota(I32, (N_GROUPS, tm), 0)
    gsel = jnp.zeros((N_GROUPS, tm), F32)
    cur = gscore
    for _ in range(TOPK_GROUPS):
        mx = jnp.max(cur, axis=0, keepdims=True)
        fi = jnp.min(jnp.where(cur == mx, itg, N_GROUPS), axis=0, keepdims=True)
        pick = itg == fi
        gsel = jnp.where(pick, 1.0, gsel)
        cur = jnp.where(pick, -jnp.inf, cur)
    emask = jnp.broadcast_to(gsel.reshape(N_GROUPS, 1, tm), (N_GROUPS, per, tm)).reshape(N_EXPERTS, tm) > 0.5

    ite = lax.broadcasted_iota(I32, (N_EXPERTS, tm), 0)
    cur = jnp.where(emask, sel, NEG_INF)
    idx_rows, s_rows = [], []
    for _ in range(TOP_K):
        mx = jnp.max(cur, axis=0, keepdims=True)
        fi = jnp.min(jnp.where(cur == mx, ite, N_EXPERTS), axis=0, keepdims=True)
        pick = ite == fi
        idx_rows.append(fi)
        s_rows.append(jnp.sum(jnp.where(pick, scores, 0.0), axis=0, keepdims=True))
        cur = jnp.where(pick, -jnp.inf, cur)
    top_s = jnp.concatenate(s_rows, axis=0)
    idx_ref[...] = jnp.concatenate(idx_rows, axis=0)
    gate_ref[...] = top_s / jnp.sum(top_s, axis=0, keepdims=True) * ROUTED_SCALE


def _merge(o_a, o_b, x, mod, ga, gb, wo, n2, wr_t, rb_col, wsg, wsu, wsd, *, tm):
    bsz, s, d = x.shape
    nt = s // tm
    n = bsz * s
    full = lambda a: pl.BlockSpec(a.shape, lambda b, i: (0,) * a.ndim)
    return pl.pallas_call(
        functools.partial(_merge_kernel, d=d),
        out_shape=(jax.ShapeDtypeStruct((bsz, s, d), F32),
                   jax.ShapeDtypeStruct((n, d // 4), U32),
                   jax.ShapeDtypeStruct((n, d // 4), U32),
                   jax.ShapeDtypeStruct((TOP_K, n), I32),
                   jax.ShapeDtypeStruct((TOP_K, n), F32)),
        grid=(bsz, nt),
        in_specs=[pl.BlockSpec((1, tm, d // 2), lambda b, i: (b, i, 0)),
                  pl.BlockSpec((1, tm, d // 2), lambda b, i: (b, i, 0)),
                  pl.BlockSpec((1, tm, d), lambda b, i: (b, i, 0)),
                  full(mod), full(ga), full(gb), full(wo), full(n2), full(wr_t), full(rb_col),
                  full(wsg), full(wsu), full(wsd)],
        out_specs=(pl.BlockSpec((1, tm, d), lambda b, i: (b, i, 0)),
                   pl.BlockSpec((tm, d // 4), lambda b, i: (b * nt + i, 0)),
                   pl.BlockSpec((tm, d // 4), lambda b, i: (b * nt + i, 0)),
                   pl.BlockSpec((TOP_K, tm), lambda b, i: (0, b * nt + i)),
                   pl.BlockSpec((TOP_K, tm), lambda b, i: (0, b * nt + i))),
        compiler_params=_params("arbitrary", "arbitrary"),
        name="merge",
    )(o_a, o_b, x, mod, ga, gb, wo, n2, wr_t, rb_col, wsg, wsu, wsd)


def _rank_kernel(idx_ref, rank_ref, cnt_ref, carry_ref):
    tm = idx_ref.shape[1]

    @pl.when(pl.program_id(0) == 0)
    def _():
        carry_ref[...] = jnp.zeros_like(carry_ref)

    idx = idx_ref[...]
    ite = lax.broadcasted_iota(I32, (N_EXPERTS, tm), 0)
    before = (lax.broadcasted_iota(I32, (tm, tm), 0) < lax.broadcasted_iota(I32, (tm, tm), 1)).astype(BF16)
    base = carry_ref[...]
    rows = []
    for k in range(TOP_K):
        oh = ite == idx[k:k + 1, :]
        ohf = jnp.where(oh, 1.0, 0.0)
        cum = jnp.dot(ohf.astype(BF16), before, preferred_element_type=F32)
        rows.append(jnp.sum(jnp.where(oh, cum + base, 0.0), axis=0, keepdims=True))
        base = base + jnp.sum(ohf, axis=1, keepdims=True)
    rank_ref[...] = jnp.concatenate(rows, axis=0).astype(I32)
    carry_ref[...] = base
    cnt_ref[...] = base


def _rank(idx, *, tm):
    n = idx.shape[1]
    return pl.pallas_call(
        _rank_kernel,
        out_shape=(jax.ShapeDtypeStruct((TOP_K, n), I32), jax.ShapeDtypeStruct((N_EXPERTS, 1), F32)),
        grid=(n // tm,),
        in_specs=[pl.BlockSpec((TOP_K, tm), lambda i: (0, i))],
        out_specs=(pl.BlockSpec((TOP_K, tm), lambda i: (0, i)), pl.BlockSpec((N_EXPERTS, 1), lambda i: (0, 0))),
        scratch_shapes=[pltpu.VMEM((N_EXPERTS, 1), F32)],
        compiler_params=_params("arbitrary"),
        name="rank",
    )(idx)


def _pos_kernel(idx_ref, rank_ref, pstart_ref, pos_ref):
    tm = idx_ref.shape[1]
    idx = idx_ref[...]
    ite = lax.broadcasted_iota(I32, (N_EXPERTS, tm), 0)
    pstart = pstart_ref[...]
    rows = [jnp.sum(jnp.where(ite == idx[k:k + 1, :], pstart, 0.0), axis=0, keepdims=True) for k in range(TOP_K)]
    pos_ref[...] = jnp.concatenate(rows, axis=0).astype(I32) + rank_ref[...]


def _pos(idx, rank, pstart_col, *, tm):
    n = idx.shape[1]
    return pl.pallas_call(
        _pos_kernel,
        out_shape=jax.ShapeDtypeStruct((TOP_K, n), I32),
        grid=(n // tm,),
        in_specs=[pl.BlockSpec((TOP_K, tm), lambda i: (0, i)),
                  pl.BlockSpec((TOP_K, tm), lambda i: (0, i)),
                  pl.BlockSpec((N_EXPERTS, 1), lambda i: (0, 0))],
        out_specs=pl.BlockSpec((TOP_K, tm), lambda i: (0, i)),
        compiler_params=_params("arbitrary"),
        name="pos",
    )(idx, rank, pstart_col)


def _expert_kernel(be_ref, bv_ref, nu_ref, xa_ref, xb_ref, wg_ref, wu_ref, wd_ref, ya_ref, yb_ref, wg_s, wu_s, wd_s):
    i = pl.program_id(0)
    used = i < nu_ref[0]

    @pl.when(used & ((i == 0) | (be_ref[i] != be_ref[jnp.maximum(i - 1, 0)])))
    def _():
        wg_s[...] = wg_ref[0].astype(BF16)
        wu_s[...] = wu_ref[0].astype(BF16)
        wd_s[...] = wd_ref[0].astype(BF16)

    @pl.when(used)
    def _():
        packed = jnp.concatenate([xa_ref[...], xb_ref[...]], axis=1)
        row = lax.broadcasted_iota(I32, packed.shape, 0)
        packed = jnp.where(row < bv_ref[i], packed, jnp.zeros_like(packed))
        x = _unpack_bf16_pairs(packed).astype(BF16)
        g = jnp.dot(x, wg_s[...], preferred_element_type=F32)
        u = jnp.dot(x, wu_s[...], preferred_element_type=F32)
        y = jnp.dot((_silu(g) * u).astype(BF16), wd_s[...], preferred_element_type=F32)
        packed = _pack_bf16_pairs(y)
        half = packed.shape[1] // 2
        ya_ref[...] = packed[:, :half]
        yb_ref[...] = packed[:, half:]

    @pl.when(i >= nu_ref[0])
    def _():
        ya_ref[...] = jnp.zeros_like(ya_ref)
        yb_ref[...] = jnp.zeros_like(yb_ref)


def _experts(block_expert, block_valid, n_used, xs_a, xs_b, wg, wu, wd):
    rows, hw = xs_a.shape
    t = EXPERT_ROWS
    d, f = wg.shape[1], wg.shape[2]
    row_spec = pl.BlockSpec((t, hw), lambda i, be, bv, nu: (i, 0))
    grid_spec = pltpu.PrefetchScalarGridSpec(
        num_scalar_prefetch=3,
        grid=(rows // t,),
        in_specs=[row_spec, row_spec,
                  pl.BlockSpec((1, d, f), lambda i, be, bv, nu: (be[i], 0, 0)),
                  pl.BlockSpec((1, d, f), lambda i, be, bv, nu: (be[i], 0, 0)),
                  pl.BlockSpec((1, f, d), lambda i, be, bv, nu: (be[i], 0, 0))],
        out_specs=(row_spec, row_spec),
        scratch_shapes=[pltpu.VMEM((d, f), BF16), pltpu.VMEM((d, f), BF16), pltpu.VMEM((f, d), BF16)],
    )
    return pl.pallas_call(
        _expert_kernel,
        out_shape=(jax.ShapeDtypeStruct((rows, hw), U32), jax.ShapeDtypeStruct((rows, hw), U32)),
        grid_spec=grid_spec,
        compiler_params=_params("arbitrary"),
        name="experts",
    )(block_expert, block_valid, n_used, xs_a, xs_b, wg, wu, wd)


def _sc_scatter_rows(x, idx_flat, n_rows):
    n, w = x.shape
    m = idx_flat.shape[1]
    nwin = n // SC_WINDOW
    mesh = plsc.VectorSubcoreMesh(core_axis_name="core", subcore_axis_name="subcore")

    @functools.partial(pl.kernel, out_type=jax.ShapeDtypeStruct((n_rows, w), x.dtype), mesh=mesh,
                       scratch_types=[])
    def scatter_kernel(x_hbm, i_hbm, o_hbm):
        def body(x_vmem, i_vmem):
            pltpu.sync_copy(x_vmem, o_hbm.at[i_vmem.at[0]])

        pltpu.emit_pipeline(
            body,
            grid=(m // SC_WINDOW,),
            in_specs=[pl.BlockSpec((SC_WINDOW, w), lambda i: (i % nwin, 0)),
                      pl.BlockSpec((1, SC_WINDOW), lambda i: (0, i))],
            out_specs=[],
            core_axis_name=("core", "subcore"),
            dimension_semantics=(pltpu.PARALLEL,),
        )(x_hbm, i_hbm)

    return scatter_kernel(x, idx_flat)


def _sc_gather_rows(src, idx_flat):
    m = idx_flat.shape[1]
    w = src.shape[1]
    mesh = plsc.VectorSubcoreMesh(core_axis_name="core", subcore_axis_name="subcore")

    @functools.partial(pl.kernel, out_type=jax.ShapeDtypeStruct((m, w), src.dtype), mesh=mesh)
    def gather_kernel(x_hbm, i_hbm, o_hbm):
        def body(i_vmem, o_vmem):
            pltpu.sync_copy(x_hbm.at[i_vmem.at[0]], o_vmem)

        pltpu.emit_pipeline(
            body,
            grid=(m // SC_WINDOW,),
            in_specs=[pl.BlockSpec((1, SC_WINDOW), lambda i: (0, i))],
            out_specs=[pl.BlockSpec((SC_WINDOW, w), lambda i: (i, 0))],
            core_axis_name=("core", "subcore"),
            dimension_semantics=(pltpu.PARALLEL,),
        )(i_hbm, o_hbm)

    return gather_kernel(src, idx_flat)


def _finish_kernel(gate_ref, xres_ref, mod_ref, ya_ref, yb_ref, o_ref, *, d, tiles_per_batch):
    tm = gate_ref.shape[1]
    b = pl.program_id(0) // tiles_per_batch
    gates = gate_ref[...].T
    acc = jnp.zeros((tm, d), F32)
    for k in range(TOP_K):
        packed = jnp.concatenate([ya_ref[k], yb_ref[k]], axis=1)
        acc = acc + _unpack_bf16_pairs(packed) * gates[:, k:k + 1]
    o_ref[...] = xres_ref[...] + mod_ref[pl.ds(b, 1), 5 * d:6 * d] * acc


def _finish(gates, xres, mod, yg_a, yg_b, *, tm, tiles_per_batch):
    n, d = xres.shape
    return pl.pallas_call(
        functools.partial(_finish_kernel, d=d, tiles_per_batch=tiles_per_batch),
        out_shape=jax.ShapeDtypeStruct((n, d), F32),
        grid=(n // tm,),
        in_specs=[pl.BlockSpec((TOP_K, tm), lambda i: (0, i)),
                  pl.BlockSpec((tm, d), lambda i: (i, 0)),
                  pl.BlockSpec(mod.shape, lambda i: (0, 0)),
                  pl.BlockSpec((TOP_K, tm, d // 4), lambda i: (0, i, 0)),
                  pl.BlockSpec((TOP_K, tm, d // 4), lambda i: (0, i, 0))],
        out_specs=pl.BlockSpec((tm, d), lambda i: (i, 0)),
        compiler_params=_params("arbitrary"),
        name="finish",
    )(gates, xres, mod, yg_a, yg_b)


_QA_HEAD_ORDER = (0, 4, 1, 5, 2, 6, 3, 7)


def _reorder_qa_heads(a, axis):
    return jnp.concatenate([lax.slice_in_dim(a, h * HEAD_DIM, (h + 1) * HEAD_DIM, axis=axis)
                            for h in _QA_HEAD_ORDER], axis=axis)


def _rope_swap(a):
    quarter = HEAD_DIM // 4
    lead = a.shape[:-1]
    return jnp.flip(a.reshape(lead + (-1, 2, quarter)), axis=-2).reshape(a.shape)


def _rope_tables(s):
    quarter = HEAD_DIM // 4
    t = jnp.arange(s)
    row = (t // GRID_W).astype(F32)
    col = (t % GRID_W).astype(F32)
    freqs = ROPE_BASE ** (-jnp.arange(quarter, dtype=F32) / quarter)
    ar = row[:, None] * freqs[None, :]
    ac = col[:, None] * freqs[None, :]
    cos = jnp.concatenate([jnp.cos(ar), jnp.cos(ar), jnp.cos(ac), jnp.cos(ac)], axis=1)
    sin = jnp.concatenate([-jnp.sin(ar), jnp.sin(ar), -jnp.sin(ac), jnp.sin(ac)], axis=1)
    return jnp.tile(cos, (1, 2)), jnp.tile(sin, (1, 2))


def kernel(x, c, ctx, c_ctx, w_ada, b_ada, norm1, norm2, w_in, q_norm_a, k_norm_a, q_norm_b, k_norm_b, sink_a,
           rpb_b, out_norm_a, out_norm_b, w_out, w_router, router_bias, we_gate, we_up, we_down, ws_gate, ws_up,
           ws_down):
    assert w_ada.shape[0] == 1, "single-layer block"
    bsz, s, d = x.shape
    lc = ctx.shape[1]
    n = bsz * s
    rows = s // GRID_W
    assert s % (NA_QROWS * GRID_W) == 0 and rows >= NA_KROWS and bsz <= 4 and d == 1024

    c8 = jnp.concatenate([c, c_ctx[None, :], jnp.zeros((8 - bsz - 1, d), F32)], axis=0)
    mod = _ada(c8, w_ada[0], b_ada[0][None, :])

    w = w_in[0]
    aw, akw, bw = A_Q_HEADS * HEAD_DIM, A_KV_HEADS * HEAD_DIM, B_HEADS * HEAD_DIM
    cuts = np.cumsum([0, aw, akw, akw, bw, bw, bw])
    qa_w, ka_w, va_w, qb_w, kb_w, vb_w = [w[:, cuts[i]:cuts[i + 1]] for i in range(6)]
    qa_w = _reorder_qa_heads(qa_w, 1)
    w_ext = jnp.concatenate([qa_w, qb_w, kb_w, vb_w, ka_w, va_w, _rope_swap(qa_w), _rope_swap(ka_w)],
                            axis=1).astype(BF16)
    scale = HEAD_DIM ** -0.5 * LOG2E
    pair = lambda g: jnp.tile(g, 2)
    gains = jnp.stack([pair(q_norm_a[0]) * scale, pair(q_norm_b[0]) * scale, pair(k_norm_b[0]), pair(k_norm_a[0]),
                       pair(_rope_swap(q_norm_a[0])) * scale, pair(_rope_swap(k_norm_a[0])),
                       jnp.zeros((LANES,), F32), jnp.zeros((LANES,), F32)])
    bd = jnp.asarray(np.kron(np.eye(2), np.full((HEAD_DIM, HEAD_DIM), 1.0 / HEAD_DIM)), BF16)
    cos_t, sin_t = _rope_tables(s)
    n1 = norm1[0][None, :]
    zq = _inproj(x, mod, n1, w_ext, gains, cos_t, sin_t, bd, mod_row=None, tm=512)
    zc = _inproj(ctx, mod, n1, w_ext, gains, jnp.ones((lc, LANES), F32), jnp.zeros((lc, LANES), F32), bd,
                 mod_row=bsz, tm=lc)

    sink = sink_a[0].astype(F32)
    sink_rep = jnp.concatenate([jnp.broadcast_to(sink[h] * LOG2E, (A_WINDOW, LANES)) for h in _QA_HEAD_ORDER])
    o_a = _attn_a(zq, zc, sink_rep, _attn_a_mask(A_WINDOW))
    o_b = _attn_b(zq, zc, _na_table(rpb_b[0], rows))

    ga = _reorder_qa_heads(out_norm_a[0], 0)[None, :]
    gb = out_norm_b[0][None, :]
    wo = jnp.concatenate([_reorder_qa_heads(w_out[0][:aw], 0), w_out[0][aw:]], axis=0).astype(BF16)
    xres, hp_a, hp_b, idx, gates = _merge(o_a, o_b, x, mod, ga, gb, wo, norm2[0][None, :], w_router[0].T,
                                  router_bias[0][:, None], ws_gate[0].astype(BF16), ws_up[0].astype(BF16),
                                  ws_down[0].astype(BF16), tm=512)

    rank, counts = _rank(idx, tm=512)
    t = EXPERT_ROWS
    cnt = counts[:, 0].astype(I32)
    padded = (cnt + t - 1) // t * t
    pends = jnp.cumsum(padded)
    pstart = pends - padded
    n_blocks = n * TOP_K // t + N_EXPERTS
    block_start = jnp.arange(n_blocks, dtype=I32) * t
    block_expert = jnp.minimum(jnp.sum((pends[None, :] <= block_start[:, None]).astype(I32), axis=1), N_EXPERTS - 1)
    of_expert = (block_expert[:, None] == jnp.arange(N_EXPERTS, dtype=I32)[None, :]).astype(I32)
    block_valid = jnp.clip(jnp.sum(of_expert * (cnt + pstart)[None, :], axis=1) - block_start, 0, t)
    n_used = (pends[-1:] // t).astype(I32)
    pos = _pos(idx, rank, pstart.astype(F32)[:, None], tm=512)

    pos_flat = pos.reshape(1, TOP_K * n)
    xs_a = _sc_scatter_rows(hp_a, pos_flat, n_blocks * t)
    xs_b = _sc_scatter_rows(hp_b, pos_flat, n_blocks * t)
    ys_a, ys_b = _experts(block_expert, block_valid, n_used, xs_a, xs_b, we_gate[0], we_up[0], we_down[0])
    yg_a = _sc_gather_rows(ys_a, pos_flat).reshape(TOP_K, n, d // 4)
    yg_b = _sc_gather_rows(ys_b, pos_flat).reshape(TOP_K, n, d // 4)
    out = _finish(gates, xres.reshape(n, d), mod, yg_a, yg_b, tm=256, tiles_per_batch=s // 256)
    return out.reshape(bsz, s, d)
```

```python
import functools

import numpy as np
import jax
import jax.numpy as jnp
from jax import lax
from jax.experimental import pallas as pl
from jax.experimental.pallas import tpu as pltpu
from jax.experimental.pallas import tpu_sc as plsc

F32 = jnp.float32
BF16 = jnp.bfloat16
I32 = jnp.int32
U32 = jnp.uint32

LANES = 128
HEAD_DIM = 64
HEAD_PAIR = 2 * HEAD_DIM
GRID_W = 64
A_Q_HEADS = 8
A_KV_HEADS = 2
A_WINDOW = 128
B_HEADS = 8
NA_KH = 8
NA_KW = 16
NA_QROWS = 4
NA_KROWS = NA_QROWS + NA_KH
ROPE_BASE = 10000.0
N_EXPERTS = 256
TOP_K = 8
N_GROUPS = 8
TOPK_GROUPS = 4
ROUTED_SCALE = 2.5
LOG2E = 1.4426950408889634
EPS = 1e-6
NEG_INF = -1e30
EXPERT_ROWS = 256
SC_WINDOW = 128
VMEM_LIMIT = 56 * 1024 * 1024

_NT = (((1,), (1,)), ((), ()))


def _params(*sem):
    return pltpu.CompilerParams(dimension_semantics=sem, vmem_limit_bytes=VMEM_LIMIT)


def _silu(v):
    return v * jax.nn.sigmoid(v)


def _rms(v, gain):
    return v * lax.rsqrt(jnp.mean(v * v, axis=-1, keepdims=True) + EPS) * gain


def _pack_bf16_pairs(v):
    n = v.shape[1] // 2
    lo = lax.bitcast_convert_type(v[:, :n].astype(BF16).astype(F32), U32) >> 16
    hi = lax.bitcast_convert_type(v[:, n:].astype(BF16).astype(F32), U32) & jnp.uint32(0xFFFF0000)
    return hi | lo


def _unpack_bf16_pairs(w):
    lo = lax.bitcast_convert_type(w << 16, F32)
    hi = lax.bitcast_convert_type(w & jnp.uint32(0xFFFF0000), F32)
    return jnp.concatenate([lo, hi], axis=1)


def _ada_kernel(c_ref, w_ref, b_ref, o_ref):
    a = _silu(c_ref[...])
    o_ref[...] = jnp.dot(a, w_ref[...], preferred_element_type=F32,
                         precision=lax.Precision.HIGHEST) + b_ref[...]


def _ada(c8, w, b):
    d, n = w.shape
    bn = n // 4
    return pl.pallas_call(
        _ada_kernel,
        out_shape=jax.ShapeDtypeStruct((8, n), F32),
        grid=(n // bn,),
        in_specs=[pl.BlockSpec((8, d), lambda j: (0, 0)),
                  pl.BlockSpec((d, bn), lambda j: (0, j)),
                  pl.BlockSpec((1, bn), lambda j: (0, j))],
        out_specs=pl.BlockSpec((8, bn), lambda j: (0, j)),
        compiler_params=_params("arbitrary"),
        name="ada",
    )(c8, w, b)


_QA, _QB, _KB, _VB, _KA, _VA, _QA_SW, _KA_SW = 0, 4, 8, 12, 16, 17, 18, 22
_OUT_BLOCKS = 18
_EXT_BLOCKS = 23


def _inproj_kernel(x_ref, mod_ref, n1_ref, w_ref, g_ref, cos_ref, sin_ref, bd_ref, o_ref, *, mod_row, d):
    b = pl.program_id(0) if mod_row is None else mod_row
    xn = _rms(x_ref[0], n1_ref[...])
    sh = mod_ref[pl.ds(b, 1), 0:d]
    sc = mod_ref[pl.ds(b, 1), d:2 * d]
    h = (xn * (1.0 + sc) + sh).astype(BF16)
    z = jnp.dot(h, w_ref[...], preferred_element_type=F32)
    bd = bd_ref[...]
    cos = cos_ref[...]
    sin = sin_ref[...]

    def blk(j):
        return z[:, j * LANES:(j + 1) * LANES]

    def head_rinv(zb):
        ms = jnp.dot((zb * zb).astype(BF16), bd, preferred_element_type=F32)
        return lax.rsqrt(ms + EPS)

    def put(j, v):
        o_ref[0, :, j * LANES:(j + 1) * LANES] = v.astype(BF16)

    def roped(j, j_sw, g_row, g_sw_row):
        zb = blk(j)
        r = head_rinv(zb)
        put(j, (zb * r * g_ref[g_row:g_row + 1, :]) * cos + (blk(j_sw) * r * g_ref[g_sw_row:g_sw_row + 1, :]) * sin)

    def normed(j, g_row):
        zb = blk(j)
        put(j, zb * head_rinv(zb) * g_ref[g_row:g_row + 1, :])

    for j in range(4):
        roped(_QA + j, _QA_SW + j, 0, 4)
        normed(_QB + j, 1)
        normed(_KB + j, 2)
        put(_VB + j, blk(_VB + j))
    roped(_KA, _KA_SW, 3, 5)
    put(_VA, blk(_VA))


def _inproj(x, mod, n1, w_ext, gains, cos_t, sin_t, bd, *, mod_row, tm):
    bsz, s, d = x.shape
    kern = functools.partial(_inproj_kernel, mod_row=mod_row, d=d)
    return pl.pallas_call(
        kern,
        out_shape=jax.ShapeDtypeStruct((bsz, s, _OUT_BLOCKS * LANES), BF16),
        grid=(bsz, s // tm),
        in_specs=[pl.BlockSpec((1, tm, d), lambda b, i: (b, i, 0)),
                  pl.BlockSpec(mod.shape, lambda b, i: (0, 0)),
                  pl.BlockSpec((1, d), lambda b, i: (0, 0)),
                  pl.BlockSpec(w_ext.shape, lambda b, i: (0, 0)),
                  pl.BlockSpec(gains.shape, lambda b, i: (0, 0)),
                  pl.BlockSpec((tm, LANES), lambda b, i: (i, 0)),
                  pl.BlockSpec((tm, LANES), lambda b, i: (i, 0)),
                  pl.BlockSpec((LANES, LANES), lambda b, i: (0, 0))],
        out_specs=pl.BlockSpec((1, tm, _OUT_BLOCKS * LANES), lambda b, i: (b, i, 0)),
        compiler_params=_params("arbitrary", "arbitrary"),
        name="inproj",
    )(x, mod, n1, w_ext, gains, cos_t, sin_t, bd)


def _split_pair(qp, lo):
    zero = jnp.zeros_like(qp)
    return jnp.concatenate([jnp.where(lo, qp, zero), jnp.where(lo, zero, qp)], axis=0)


def _softmax_pv(s_parts, v_parts, sink_rep):
    chunks = [s[:, c * LANES:(c + 1) * LANES] for s in s_parts for c in range(s.shape[1] // LANES)]
    m = jnp.max(functools.reduce(jnp.maximum, chunks), axis=-1, keepdims=True)
    m_rep = jnp.broadcast_to(m, (m.shape[0], LANES))
    if sink_rep is not None:
        m_rep = jnp.maximum(m_rep, sink_rep)
    acc = None
    for s, v in zip(s_parts, v_parts):
        p = jnp.concatenate([jnp.exp2(s[:, c * LANES:(c + 1) * LANES] - m_rep)
                             for c in range(s.shape[1] // LANES)], axis=1).astype(BF16)
        v_ext = jnp.concatenate([v, jnp.ones_like(v)], axis=1)
        o = jnp.dot(p, v_ext, preferred_element_type=F32)
        acc = o if acc is None else acc + o
    l_rep = acc[:, LANES:]
    if sink_rep is not None:
        l_rep = l_rep + jnp.exp2(sink_rep - m_rep)
    return acc[:, :LANES] * (1.0 / l_rep)


def _attn_a_kernel(q_ref, kp_ref, kc_ref, kn_ref, vp_ref, vc_ref, vn_ref, kx_ref, vx_ref, sink_ref, mask_ref,
                   o_ref):
    tq = q_ref.shape[1]
    lo = lax.broadcasted_iota(I32, (tq, LANES), 1) < HEAD_DIM
    q = q_ref[0]
    qs = jnp.concatenate([_split_pair(q[:, p * LANES:(p + 1) * LANES], lo) for p in range(4)], axis=0)
    k = jnp.concatenate([kp_ref[0], kc_ref[0], kn_ref[0]], axis=0)
    v = jnp.concatenate([vp_ref[0], vc_ref[0], vn_ref[0]], axis=0)
    s_loc = lax.dot_general(qs, k, _NT, preferred_element_type=F32) + mask_ref[0]
    s_ctx = lax.dot_general(qs, kx_ref[0], _NT, preferred_element_type=F32)
    o = _softmax_pv([s_loc, s_ctx], [v, vx_ref[0]], sink_ref[...])
    for p in range(4):
        o_lo = o[(2 * p) * tq:(2 * p + 1) * tq]
        o_hi = o[(2 * p + 1) * tq:(2 * p + 2) * tq]
        o_ref[0, :, p * LANES:(p + 1) * LANES] = jnp.where(lo, o_lo, o_hi).astype(BF16)


def _attn_a(zq, zc, sink_rep, mask):
    bsz, s, _ = zq.shape
    lc = zc.shape[1]
    tq = A_WINDOW
    nblk = s // tq
    ka, va = _KA, _VA

    def kv_spec(col, shift):
        return pl.BlockSpec((1, tq, LANES), lambda b, j: (b, jnp.clip(j + shift, 0, nblk - 1), col))

    def variant(j):
        return jnp.where(j == 0, 0, jnp.where(j == nblk - 1, 2, 1))

    return pl.pallas_call(
        _attn_a_kernel,
        out_shape=jax.ShapeDtypeStruct((bsz, s, 4 * LANES), BF16),
        grid=(bsz, nblk),
        in_specs=[pl.BlockSpec((1, tq, 4 * LANES), lambda b, j: (b, j, 0)),
                  kv_spec(ka, -1), kv_spec(ka, 0), kv_spec(ka, 1),
                  kv_spec(va, -1), kv_spec(va, 0), kv_spec(va, 1),
                  pl.BlockSpec((1, lc, LANES), lambda b, j: (b, 0, ka)),
                  pl.BlockSpec((1, lc, LANES), lambda b, j: (b, 0, va)),
                  pl.BlockSpec(sink_rep.shape, lambda b, j: (0, 0)),
                  pl.BlockSpec((1,) + mask.shape[1:], lambda b, j: (variant(j), 0, 0))],
        out_specs=pl.BlockSpec((1, tq, 4 * LANES), lambda b, j: (b, j, 0)),
        compiler_params=_params("arbitrary", "arbitrary"),
        name="attn_a",
    )(zq, zq, zq, zq, zq, zq, zq, zc, zc, sink_rep, mask)


def _attn_a_mask(tq):
    qi = np.arange(tq)[:, None]
    kj = np.arange(3 * tq)[None, :]
    ok = (kj >= qi) & (kj <= qi + 2 * tq)
    variants = [ok & (kj >= tq), ok, ok & (kj < 2 * tq)]
    m = np.stack([np.tile(np.where(v, 0.0, NEG_INF).astype(np.float32), (A_Q_HEADS, 1)) for v in variants])
    return jnp.asarray(m)


def _attn_b_kernel(q_ref, k0_ref, k1_ref, k2_ref, v0_ref, v1_ref, v2_ref, kx_ref, vx_ref, tab_ref, o_ref):
    tq = q_ref.shape[1]
    lo = lax.broadcasted_iota(I32, (tq, LANES), 1) < HEAD_DIM
    for p in range(4):
        sl = slice(p * LANES, (p + 1) * LANES)
        qs = _split_pair(q_ref[0, :, sl], lo)
        k = jnp.concatenate([k0_ref[0, :, sl], k1_ref[0, :, sl], k2_ref[0, :, sl]], axis=0)
        v = jnp.concatenate([v0_ref[0, :, sl], v1_ref[0, :, sl], v2_ref[0, :, sl]], axis=0)
        s_loc = lax.dot_general(qs, k, _NT, preferred_element_type=F32) + tab_ref[0, p]
        s_ctx = lax.dot_general(qs, kx_ref[0, :, sl], _NT, preferred_element_type=F32)
        o = _softmax_pv([s_loc, s_ctx], [v, vx_ref[0, :, sl]], None)
        o_ref[0, :, sl] = jnp.where(lo, o[:tq], o[tq:]).astype(BF16)


def _attn_b(zq, zc, table):
    bsz, s, _ = zq.shape
    lc = zc.shape[1]
    tq = NA_QROWS * GRID_W
    ng = s // tq
    qb, kb, vb = _QB // 4, _KB // 4, _VB // 4

    def kv_spec(col, off):
        return pl.BlockSpec((1, tq, 4 * LANES), lambda i, b: (b, jnp.clip(i - 1, 0, ng - 3) + off, col))

    def variant(i):
        return jnp.where(i == 0, 0, jnp.where(i == ng - 1, 2, 1))

    return pl.pallas_call(
        _attn_b_kernel,
        out_shape=jax.ShapeDtypeStruct((bsz, s, 4 * LANES), BF16),
        grid=(ng, bsz),
        in_specs=[pl.BlockSpec((1, tq, 4 * LANES), lambda i, b: (b, i, qb)),
                  kv_spec(kb, 0), kv_spec(kb, 1), kv_spec(kb, 2),
                  kv_spec(vb, 0), kv_spec(vb, 1), kv_spec(vb, 2),
                  pl.BlockSpec((1, lc, 4 * LANES), lambda i, b: (b, 0, kb)),
                  pl.BlockSpec((1, lc, 4 * LANES), lambda i, b: (b, 0, vb)),
                  pl.BlockSpec((1,) + table.shape[1:], lambda i, b: (variant(i), 0, 0, 0))],
        out_specs=pl.BlockSpec((1, tq, 4 * LANES), lambda i, b: (b, i, 0)),
        compiler_params=_params("arbitrary", "arbitrary"),
        name="attn_b",
    )(zq, zq, zq, zq, zq, zq, zq, zc, zc, table)


def _na_table(rpb, rows):
    ng = rows // NA_QROWS
    nq, nk = NA_QROWS * GRID_W, NA_KROWS * GRID_W
    qc = np.arange(GRID_W)
    kc = np.arange(GRID_W)
    ws = np.clip(qc - NA_KW // 2, 0, GRID_W - NA_KW)
    valid_c = (kc[None, :] >= ws[:, None]) & (kc[None, :] < ws[:, None] + NA_KW)
    dc = np.clip(kc[None, :] - qc[:, None], -(NA_KW - 1), NA_KW - 1) + (NA_KW - 1)
    c_sel = (dc[..., None] == np.arange(2 * NA_KW - 1)) & valid_c[..., None]
    r_sels, valids = [], []
    for i in (0, 1, ng - 1):
        start = int(np.clip(NA_QROWS * i - NA_KH // 2, 0, rows - NA_KROWS))
        r = NA_QROWS * i + np.arange(NA_QROWS)
        rs = np.clip(r - NA_KH // 2, 0, rows - NA_KH)
        krow = start + np.arange(NA_KROWS)
        valid_r = (krow[None, :] >= rs[:, None]) & (krow[None, :] < rs[:, None] + NA_KH)
        dr = krow[None, :] - r[:, None] + (NA_KH - 1)
        r_sels.append((dr[..., None] == np.arange(2 * NA_KH - 1)) & valid_r[..., None])
        valids.append((valid_r[:, None, :, None] & valid_c[None, :, None, :]).reshape(nq, nk))
    r_sel = jnp.asarray(np.stack(r_sels), F32)
    bias = jnp.einsum('wxya,hab,uvb->whxuyv', r_sel, rpb.astype(F32), jnp.asarray(c_sel, F32),
                      precision=lax.Precision.HIGHEST).reshape(3, B_HEADS, nq, nk)
    tab = jnp.where(jnp.asarray(np.stack(valids))[:, None], bias * LOG2E, NEG_INF)
    return tab.reshape(3, B_HEADS // 2, 2 * nq, nk)


def _merge_kernel(oa_ref, ob_ref, x_ref, mod_ref, ga_ref, gb_ref, wo_ref, n2_ref, wr_ref, rb_ref,
                  wsg_ref, wsu_ref, wsd_ref, xres_ref, hpa_ref, hpb_ref, idx_ref, gate_ref, *, d):
    b = pl.program_id(0)
    tm = x_ref.shape[1]

    def mod(k):
        return mod_ref[pl.ds(b, 1), k * d:(k + 1) * d]

    na = _rms(oa_ref[0].astype(F32), ga_ref[...])
    nb = _rms(ob_ref[0].astype(F32), gb_ref[...])
    cat = jnp.concatenate([na, nb], axis=1).astype(BF16)
    y = jnp.dot(cat, wo_ref[...], preferred_element_type=F32)
    x1 = x_ref[0] + mod(2) * y
    h2 = _rms(x1, n2_ref[...]) * (1.0 + mod(4)) + mod(3)

    hb = h2.astype(BF16)
    act = _silu(jnp.dot(hb, wsg_ref[...], preferred_element_type=F32)) * jnp.dot(hb, wsu_ref[...],
                                                                                 preferred_element_type=F32)
    shared = jnp.dot(act.astype(BF16), wsd_ref[...], preferred_element_type=F32)
    xres_ref[0] = x1 + mod(5) * shared
    packed = _pack_bf16_pairs(h2)
    hpa_ref[...] = packed[:, :d // 4]
    hpb_ref[...] = packed[:, d // 4:]

    logits = lax.dot_general(wr_ref[...], h2, _NT, preferred_element_type=F32, precision=lax.Precision.HIGHEST)
    scores = jax.nn.sigmoid(logits)
    sel = scores + rb_ref[...]
    per = N_EXPERTS // N_GROUPS
    g3 = sel.reshape(N_GROUPS, per, tm)
    it3 = lax.broadcasted_iota(I32, (N_GROUPS, per, tm), 1)
    m1 = jnp.max(g3, axis=1, keepdims=True)
    first = jnp.min(jnp.where(g3 == m1, it3, per), axis=1, keepdims=True)
    m2 = jnp.max(jnp.where(it3 == first, -jnp.inf, g3), axis=1, keepdims=True)
    gscore = (m1 + m2).reshape(N_GROUPS, tm)

    itg = lax.broadcasted_iota(I32, (N_GROUPS, tm), 0)
    gsel = jnp.zeros((N_GROUPS, tm), F32)
    cur = gscore
    for _ in range(TOPK_GROUPS):
        mx = jnp.max(cur, axis=0, keepdims=True)
        fi = jnp.min(jnp.where(cur == mx, itg, N_GROUPS), axis=0, keepdims=True)
        pick = itg == fi
        gsel = jnp.where(pick, 1.0, gsel)
        cur = jnp.where(pick, -jnp.inf, cur)
    emask = jnp.broadcast_to(gsel.reshape(N_GROUPS, 1, tm), (N_GROUPS, per, tm)).reshape(N_EXPERTS, tm) > 0.5

    ite = lax.broadcasted_iota(I32, (N_EXPERTS, tm), 0)
    cur = jnp.where(emask, sel, NEG_INF)
    idx_rows, s_rows = [], []
    for _ in range(TOP_K):
        mx = jnp.max(cur, axis=0, keepdims=True)
        fi = jnp.min(jnp.where(cur == mx, ite, N_EXPERTS), axis=0, keepdims=True)
        pick = ite == fi
        idx_rows.append(fi)
        s_rows.append(jnp.sum(jnp.where(pick, scores, 0.0), axis=0, keepdims=True))
        cur = jnp.where(pick, -jnp.inf, cur)
    top_s = jnp.concatenate(s_rows, axis=0)
    idx_ref[...] = jnp.concatenate(idx_rows, axis=0)
    gate_ref[...] = top_s / jnp.sum(top_s, axis=0, keepdims=True) * ROUTED_SCALE


def _merge(o_a, o_b, x, mod, ga, gb, wo, n2, wr_t, rb_col, wsg, wsu, wsd, *, tm):
    bsz, s, d = x.shape
    nt = s // tm
    n = bsz * s
    full = lambda a: pl.BlockSpec(a.shape, lambda b, i: (0,) * a.ndim)
    return pl.pallas_call(
        functools.partial(_merge_kernel, d=d),
        out_shape=(jax.ShapeDtypeStruct((bsz, s, d), F32),
                   jax.ShapeDtypeStruct((n, d // 4), U32),
                   jax.ShapeDtypeStruct((n, d // 4), U32),
                   jax.ShapeDtypeStruct((TOP_K, n), I32),
                   jax.ShapeDtypeStruct((TOP_K, n), F32)),
        grid=(bsz, nt),
        in_specs=[pl.BlockSpec((1, tm, d // 2), lambda b, i: (b, i, 0)),
                  pl.BlockSpec((1, tm, d // 2), lambda b, i: (b, i, 0)),
                  pl.BlockSpec((1, tm, d), lambda b, i: (b, i, 0)),
                  full(mod), full(ga), full(gb), full(wo), full(n2), full(wr_t), full(rb_col),
                  full(wsg), full(wsu), full(wsd)],
        out_specs=(pl.BlockSpec((1, tm, d), lambda b, i: (b, i, 0)),
                   pl.BlockSpec((tm, d // 4), lambda b, i: (b * nt + i, 0)),
                   pl.BlockSpec((tm, d // 4), lambda b, i: (b * nt + i, 0)),
                   pl.BlockSpec((TOP_K, tm), lambda b, i: (0, b * nt + i)),
                   pl.BlockSpec((TOP_K, tm), lambda b, i: (0, b * nt + i))),
        compiler_params=_params("arbitrary", "arbitrary"),
        name="merge",
    )(o_a, o_b, x, mod, ga, gb, wo, n2, wr_t, rb_col, wsg, wsu, wsd)


def _rank_kernel(idx_ref, rank_ref, cnt_ref, carry_ref):
    tm = idx_ref.shape[1]

    @pl.when(pl.program_id(0) == 0)
    def _():
        carry_ref[...] = jnp.zeros_like(carry_ref)

    idx = idx_ref[...]
    ite = lax.broadcasted_iota(I32, (N_EXPERTS, tm), 0)
    before = (lax.broadcasted_iota(I32, (tm, tm), 0) < lax.broadcasted_iota(I32, (tm, tm), 1)).astype(BF16)
    base = carry_ref[...]
    rows = []
    for k in range(TOP_K):
        oh = ite == idx[k:k + 1, :]
        ohf = jnp.where(oh, 1.0, 0.0)
        cum = jnp.dot(ohf.astype(BF16), before, preferred_element_type=F32)
        rows.append(jnp.sum(jnp.where(oh, cum + base, 0.0), axis=0, keepdims=True))
        base = base + jnp.sum(ohf, axis=1, keepdims=True)
    rank_ref[...] = jnp.concatenate(rows, axis=0).astype(I32)
    carry_ref[...] = base
    cnt_ref[...] = base


def _rank(idx, *, tm):
    n = idx.shape[1]
    return pl.pallas_call(
        _rank_kernel,
        out_shape=(jax.ShapeDtypeStruct((TOP_K, n), I32), jax.ShapeDtypeStruct((N_EXPERTS, 1), F32)),
        grid=(n // tm,),
        in_specs=[pl.BlockSpec((TOP_K, tm), lambda i: (0, i))],
        out_specs=(pl.BlockSpec((TOP_K, tm), lambda i: (0, i)), pl.BlockSpec((N_EXPERTS, 1), lambda i: (0, 0))),
        scratch_shapes=[pltpu.VMEM((N_EXPERTS, 1), F32)],
        compiler_params=_params("arbitrary"),
        name="rank",
    )(idx)


def _pos_kernel(idx_ref, rank_ref, pstart_ref, pos_ref):
    tm = idx_ref.shape[1]
    idx = idx_ref[...]
    ite = lax.broadcasted_iota(I32, (N_EXPERTS, tm), 0)
    pstart = pstart_ref[...]
    rows = [jnp.sum(jnp.where(ite == idx[k:k + 1, :], pstart, 0.0), axis=0, keepdims=True) for k in range(TOP_K)]
    pos_ref[...] = jnp.concatenate(rows, axis=0).astype(I32) + rank_ref[...]


def _pos(idx, rank, pstart_col, *, tm):
    n = idx.shape[1]
    return pl.pallas_call(
        _pos_kernel,
        out_shape=jax.ShapeDtypeStruct((TOP_K, n), I32),
        grid=(n // tm,),
        in_specs=[pl.BlockSpec((TOP_K, tm), lambda i: (0, i)),
                  pl.BlockSpec((TOP_K, tm), lambda i: (0, i)),
                  pl.BlockSpec((N_EXPERTS, 1), lambda i: (0, 0))],
        out_specs=pl.BlockSpec((TOP_K, tm), lambda i: (0, i)),
        compiler_params=_params("arbitrary"),
        name="pos",
    )(idx, rank, pstart_col)


def _expert_kernel(ps_ref, cnt_ref, wg_ref, wu_ref, wd_ref, xa_hbm, xb_hbm, ya_hbm, yb_hbm,
                   wg_s, wu_s, wd_s, xa_buf, xb_buf, ya_buf, yb_buf, in_sem, out_sem):
    e = pl.program_id(0)
    r = EXPERT_ROWS
    base = ps_ref[e]
    cnt = cnt_ref[e]
    nch = (cnt + r - 1) // r

    def rows_of(c):
        return pl.ds(pl.multiple_of(base + c * r, r), r)

    def fetch(c, slot):
        return (pltpu.make_async_copy(xa_hbm.at[rows_of(c)], xa_buf.at[slot], in_sem.at[0, slot]),
                pltpu.make_async_copy(xb_hbm.at[rows_of(c)], xb_buf.at[slot], in_sem.at[1, slot]))

    def flush(c, slot):
        return (pltpu.make_async_copy(ya_buf.at[slot], ya_hbm.at[rows_of(c)], out_sem.at[0, slot]),
                pltpu.make_async_copy(yb_buf.at[slot], yb_hbm.at[rows_of(c)], out_sem.at[1, slot]))

    def start(copies):
        for cp in copies:
            cp.start()

    def wait(copies):
        for cp in copies:
            cp.wait()

    @pl.when(nch > 0)
    def _():
        start(fetch(0, 0))
        wg_s[...] = wg_ref[0].astype(BF16)
        wu_s[...] = wu_ref[0].astype(BF16)
        wd_s[...] = wd_ref[0].astype(BF16)

        def chunk(c, carry):
            slot = c % 2
            wait(fetch(c, slot))

            @pl.when(c + 1 < nch)
            def _():
                start(fetch(c + 1, 1 - slot))

            @pl.when(c >= 2)
            def _():
                wait(flush(c - 2, slot))

            packed = jnp.concatenate([xa_buf[slot], xb_buf[slot]], axis=1)
            row = lax.broadcasted_iota(I32, packed.shape, 0)
            packed = jnp.where(row < cnt - c * r, packed, jnp.zeros_like(packed))
            x = _unpack_bf16_pairs(packed).astype(BF16)
            g = jnp.dot(x, wg_s[...], preferred_element_type=F32)
            u = jnp.dot(x, wu_s[...], preferred_element_type=F32)
            y = jnp.dot((_silu(g) * u).astype(BF16), wd_s[...], preferred_element_type=F32)
            out = _pack_bf16_pairs(y)
            half = out.shape[1] // 2
            ya_buf[slot] = out[:, :half]
            yb_buf[slot] = out[:, half:]
            start(flush(c, slot))
            return carry

        lax.fori_loop(0, nch, chunk, 0)

        @pl.when(nch >= 2)
        def _():
            wait(flush(nch - 2, nch % 2))

        wait(flush(nch - 1, (nch - 1) % 2))


def _experts(pstart, cnt, xs_a, xs_b, wg, wu, wd):
    rows, hw = xs_a.shape
    r = EXPERT_ROWS
    n_exp, d, f = wg.shape
    hbm = pl.BlockSpec(memory_space=pl.ANY)
    grid_spec = pltpu.PrefetchScalarGridSpec(
        num_scalar_prefetch=2,
        grid=(n_exp,),
        in_specs=[pl.BlockSpec((1, d, f), lambda e, ps, cn: (e, 0, 0)),
                  pl.BlockSpec((1, d, f), lambda e, ps, cn: (e, 0, 0)),
                  pl.BlockSpec((1, f, d), lambda e, ps, cn: (e, 0, 0)),
                  hbm, hbm],
        out_specs=(hbm, hbm),
        scratch_shapes=[pltpu.VMEM((d, f), BF16), pltpu.VMEM((d, f), BF16), pltpu.VMEM((f, d), BF16),
                        pltpu.VMEM((2, r, hw), U32), pltpu.VMEM((2, r, hw), U32),
                        pltpu.VMEM((2, r, hw), U32), pltpu.VMEM((2, r, hw), U32),
                        pltpu.SemaphoreType.DMA((2, 2)), pltpu.SemaphoreType.DMA((2, 2))],
    )
    return pl.pallas_call(
        _expert_kernel,
        out_shape=(jax.ShapeDtypeStruct((rows, hw), U32), jax.ShapeDtypeStruct((rows, hw), U32)),
        grid_spec=grid_spec,
        compiler_params=_params("arbitrary"),
        name="experts",
    )(pstart, cnt, wg, wu, wd, xs_a, xs_b)


def _sc_scatter_rows(x, idx_flat, n_rows):
    n, w = x.shape
    m = idx_flat.shape[1]
    nwin = n // SC_WINDOW
    mesh = plsc.VectorSubcoreMesh(core_axis_name="core", subcore_axis_name="subcore")

    @functools.partial(pl.kernel, out_type=jax.ShapeDtypeStruct((n_rows, w), x.dtype), mesh=mesh,
                       scratch_types=[])
    def scatter_kernel(x_hbm, i_hbm, o_hbm):
        def body(x_vmem, i_vmem):
            pltpu.sync_copy(x_vmem, o_hbm.at[i_vmem.at[0]])

        pltpu.emit_pipeline(
            body,
            grid=(m // SC_WINDOW,),
            in_specs=[pl.BlockSpec((SC_WINDOW, w), lambda i: (i % nwin, 0)),
                      pl.BlockSpec((1, SC_WINDOW), lambda i: (0, i))],
            out_specs=[],
            core_axis_name=("core", "subcore"),
            dimension_semantics=(pltpu.PARALLEL,),
        )(x_hbm, i_hbm)

    return scatter_kernel(x, idx_flat)


def _sc_gather_rows(src, idx_flat):
    m = idx_flat.shape[1]
    w = src.shape[1]
    mesh = plsc.VectorSubcoreMesh(core_axis_name="core", subcore_axis_name="subcore")

    @functools.partial(pl.kernel, out_type=jax.ShapeDtypeStruct((m, w), src.dtype), mesh=mesh)
    def gather_kernel(x_hbm, i_hbm, o_hbm):
        def body(i_vmem, o_vmem):
            pltpu.sync_copy(x_hbm.at[i_vmem.at[0]], o_vmem)

        pltpu.emit_pipeline(
            body,
            grid=(m // SC_WINDOW,),
            in_specs=[pl.BlockSpec((1, SC_WINDOW), lambda i: (0, i))],
            out_specs=[pl.BlockSpec((SC_WINDOW, w), lambda i: (i, 0))],
            core_axis_name=("core", "subcore"),
            dimension_semantics=(pltpu.PARALLEL,),
        )(i_hbm, o_hbm)

    return gather_kernel(src, idx_flat)


def _finish_kernel(gate_ref, xres_ref, mod_ref, ya_ref, yb_ref, o_ref, *, d, tiles_per_batch):
    tm = gate_ref.shape[1]
    b = pl.program_id(0) // tiles_per_batch
    gates = gate_ref[...].T
    acc = jnp.zeros((tm, d), F32)
    for k in range(TOP_K):
        packed = jnp.concatenate([ya_ref[k], yb_ref[k]], axis=1)
        acc = acc + _unpack_bf16_pairs(packed) * gates[:, k:k + 1]
    o_ref[...] = xres_ref[...] + mod_ref[pl.ds(b, 1), 5 * d:6 * d] * acc


def _finish(gates, xres, mod, yg_a, yg_b, *, tm, tiles_per_batch):
    n, d = xres.shape
    return pl.pallas_call(
        functools.partial(_finish_kernel, d=d, tiles_per_batch=tiles_per_batch),
        out_shape=jax.ShapeDtypeStruct((n, d), F32),
        grid=(n // tm,),
        in_specs=[pl.BlockSpec((TOP_K, tm), lambda i: (0, i)),
                  pl.BlockSpec((tm, d), lambda i: (i, 0)),
                  pl.BlockSpec(mod.shape, lambda i: (0, 0)),
                  pl.BlockSpec((TOP_K, tm, d // 4), lambda i: (0, i, 0)),
                  pl.BlockSpec((TOP_K, tm, d // 4), lambda i: (0, i, 0))],
        out_specs=pl.BlockSpec((tm, d), lambda i: (i, 0)),
        compiler_params=_params("arbitrary"),
        name="finish",
    )(gates, xres, mod, yg_a, yg_b)


_QA_HEAD_ORDER = (0, 4, 1, 5, 2, 6, 3, 7)


def _reorder_qa_heads(a, axis):
    return jnp.concatenate([lax.slice_in_dim(a, h * HEAD_DIM, (h + 1) * HEAD_DIM, axis=axis)
                            for h in _QA_HEAD_ORDER], axis=axis)


def _rope_swap(a):
    quarter = HEAD_DIM // 4
    lead = a.shape[:-1]
    return jnp.flip(a.reshape(lead + (-1, 2, quarter)), axis=-2).reshape(a.shape)


def _rope_tables(s):
    quarter = HEAD_DIM // 4
    t = jnp.arange(s)
    row = (t // GRID_W).astype(F32)
    col = (t % GRID_W).astype(F32)
    freqs = ROPE_BASE ** (-jnp.arange(quarter, dtype=F32) / quarter)
    ar = row[:, None] * freqs[None, :]
    ac = col[:, None] * freqs[None, :]
    cos = jnp.concatenate([jnp.cos(ar), jnp.cos(ar), jnp.cos(ac), jnp.cos(ac)], axis=1)
    sin = jnp.concatenate([-jnp.sin(ar), jnp.sin(ar), -jnp.sin(ac), jnp.sin(ac)], axis=1)
    return jnp.tile(cos, (1, 2)), jnp.tile(sin, (1, 2))


def kernel(x, c, ctx, c_ctx, w_ada, b_ada, norm1, norm2, w_in, q_norm_a, k_norm_a, q_norm_b, k_norm_b, sink_a,
           rpb_b, out_norm_a, out_norm_b, w_out, w_router, router_bias, we_gate, we_up, we_down, ws_gate, ws_up,
           ws_down):
    assert w_ada.shape[0] == 1, "single-layer block"
    bsz, s, d = x.shape
    lc = ctx.shape[1]
    n = bsz * s
    rows = s // GRID_W
    assert s % (NA_QROWS * GRID_W) == 0 and rows >= NA_KROWS and bsz <= 4 and d == 1024

    c8 = jnp.concatenate([c, c_ctx[None, :], jnp.zeros((8 - bsz - 1, d), F32)], axis=0)
    mod = _ada(c8, w_ada[0], b_ada[0][None, :])

    w = w_in[0]
    aw, akw, bw = A_Q_HEADS * HEAD_DIM, A_KV_HEADS * HEAD_DIM, B_HEADS * HEAD_DIM
    cuts = np.cumsum([0, aw, akw, akw, bw, bw, bw])
    qa_w, ka_w, va_w, qb_w, kb_w, vb_w = [w[:, cuts[i]:cuts[i + 1]] for i in range(6)]
    qa_w = _reorder_qa_heads(qa_w, 1)
    w_ext = jnp.concatenate([qa_w, qb_w, kb_w, vb_w, ka_w, va_w, _rope_swap(qa_w), _rope_swap(ka_w)],
                            axis=1).astype(BF16)
    scale = HEAD_DIM ** -0.5 * LOG2E
    pair = lambda g: jnp.tile(g, 2)
    gains = jnp.stack([pair(q_norm_a[0]) * scale, pair(q_norm_b[0]) * scale, pair(k_norm_b[0]), pair(k_norm_a[0]),
                       pair(_rope_swap(q_norm_a[0])) * scale, pair(_rope_swap(k_norm_a[0])),
                       jnp.zeros((LANES,), F32), jnp.zeros((LANES,), F32)])
    bd = jnp.asarray(np.kron(np.eye(2), np.full((HEAD_DIM, HEAD_DIM), 1.0 / HEAD_DIM)), BF16)
    cos_t, sin_t = _rope_tables(s)
    n1 = norm1[0][None, :]
    zq = _inproj(x, mod, n1, w_ext, gains, cos_t, sin_t, bd, mod_row=None, tm=512)
    zc = _inproj(ctx, mod, n1, w_ext, gains, jnp.ones((lc, LANES), F32), jnp.zeros((lc, LANES), F32), bd,
                 mod_row=bsz, tm=lc)

    sink = sink_a[0].astype(F32)
    sink_rep = jnp.concatenate([jnp.broadcast_to(sink[h] * LOG2E, (A_WINDOW, LANES)) for h in _QA_HEAD_ORDER])
    o_a = _attn_a(zq, zc, sink_rep, _attn_a_mask(A_WINDOW))
    o_b = _attn_b(zq, zc, _na_table(rpb_b[0], rows))

    ga = _reorder_qa_heads(out_norm_a[0], 0)[None, :]
    gb = out_norm_b[0][None, :]
    wo = jnp.concatenate([_reorder_qa_heads(w_out[0][:aw], 0), w_out[0][aw:]], axis=0).astype(BF16)
    xres, hp_a, hp_b, idx, gates = _merge(o_a, o_b, x, mod, ga, gb, wo, norm2[0][None, :], w_router[0].T,
                                  router_bias[0][:, None], ws_gate[0].astype(BF16), ws_up[0].astype(BF16),
                                  ws_down[0].astype(BF16), tm=512)

    rank, counts = _rank(idx, tm=512)
    t = EXPERT_ROWS
    cnt = counts[:, 0].astype(I32)
    padded = (cnt + t - 1) // t * t
    pends = jnp.cumsum(padded)
    pstart = pends - padded
    n_rows = n * TOP_K + N_EXPERTS * t
    pos = _pos(idx, rank, pstart.astype(F32)[:, None], tm=512)

    pos_flat = pos.reshape(1, TOP_K * n)
    xs_a = _sc_scatter_rows(hp_a, pos_flat, n_rows)
    xs_b = _sc_scatter_rows(hp_b, pos_flat, n_rows)
    ys_a, ys_b = _experts(pstart, cnt, xs_a, xs_b, we_gate[0], we_up[0], we_down[0])
    yg_a = _sc_gather_rows(ys_a, pos_flat).reshape(TOP_K, n, d // 4)
    yg_b = _sc_gather_rows(ys_b, pos_flat).reshape(TOP_K, n, d // 4)
    out = _finish(gates, xres.reshape(n, d), mod, yg_a, yg_b, tm=256, tiles_per_batch=s // 256)
    return out.reshape(bsz, s, d)
```

```python
import functools

import numpy as np
import jax
import jax.numpy as jnp
from jax import lax
from jax.experimental import pallas as pl
from jax.experimental.pallas import tpu as pltpu
from jax.experimental.pallas import tpu_sc as plsc

F32 = jnp.float32
BF16 = jnp.bfloat16
I32 = jnp.int32
U32 = jnp.uint32

LANES = 128
HEAD_DIM = 64
HEAD_PAIR = 2 * HEAD_DIM
GRID_W = 64
A_Q_HEADS = 8
A_KV_HEADS = 2
A_WINDOW = 128
B_HEADS = 8
NA_KH = 8
NA_KW = 16
NA_QROWS = 4
NA_KROWS = NA_QROWS + NA_KH
ROPE_BASE = 10000.0
N_EXPERTS = 256
TOP_K = 8
N_GROUPS = 8
TOPK_GROUPS = 4
ROUTED_SCALE = 2.5
LOG2E = 1.4426950408889634
EPS = 1e-6
NEG_INF = -1e30
EXPERT_ROWS = 256
SC_WINDOW = 128
VMEM_LIMIT = 56 * 1024 * 1024

_NT = (((1,), (1,)), ((), ()))


def _params(*sem):
    return pltpu.CompilerParams(dimension_semantics=sem, vmem_limit_bytes=VMEM_LIMIT)


def _silu(v):
    return v * jax.nn.sigmoid(v)


def _rms(v, gain):
    return v * lax.rsqrt(jnp.mean(v * v, axis=-1, keepdims=True) + EPS) * gain


def _pack_bf16_pairs(v):
    n = v.shape[1] // 2
    lo = lax.bitcast_convert_type(v[:, :n].astype(BF16).astype(F32), U32) >> 16
    hi = lax.bitcast_convert_type(v[:, n:].astype(BF16).astype(F32), U32) & jnp.uint32(0xFFFF0000)
    return hi | lo


def _unpack_bf16_pairs(w):
    lo = lax.bitcast_convert_type(w << 16, F32)
    hi = lax.bitcast_convert_type(w & jnp.uint32(0xFFFF0000), F32)
    return jnp.concatenate([lo, hi], axis=1)


def _ada_kernel(c_ref, w_ref, b_ref, o_ref):
    a = _silu(c_ref[...])
    o_ref[...] = jnp.dot(a, w_ref[...], preferred_element_type=F32,
                         precision=lax.Precision.HIGHEST) + b_ref[...]


def _ada(c8, w, b):
    d, n = w.shape
    bn = n // 4
    return pl.pallas_call(
        _ada_kernel,
        out_shape=jax.ShapeDtypeStruct((8, n), F32),
        grid=(n // bn,),
        in_specs=[pl.BlockSpec((8, d), lambda j: (0, 0)),
                  pl.BlockSpec((d, bn), lambda j: (0, j)),
                  pl.BlockSpec((1, bn), lambda j: (0, j))],
        out_specs=pl.BlockSpec((8, bn), lambda j: (0, j)),
        compiler_params=_params("arbitrary"),
        name="ada",
    )(c8, w, b)


_QA, _QB, _KB, _VB, _KA, _VA, _QA_SW, _KA_SW = 0, 4, 8, 12, 16, 17, 18, 22
_OUT_BLOCKS = 18
_EXT_BLOCKS = 23


def _inproj_kernel(x_ref, mod_ref, n1_ref, w_ref, g_ref, cos_ref, sin_ref, bd_ref, o_ref, *, mod_row, d):
    b = pl.program_id(0) if mod_row is None else mod_row
    xn = _rms(x_ref[0], n1_ref[...])
    sh = mod_ref[pl.ds(b, 1), 0:d]
    sc = mod_ref[pl.ds(b, 1), d:2 * d]
    h = (xn * (1.0 + sc) + sh).astype(BF16)
    z = jnp.dot(h, w_ref[...], preferred_element_type=F32)
    bd = bd_ref[...]
    cos = cos_ref[...]
    sin = sin_ref[...]

    def blk(j):
        return z[:, j * LANES:(j + 1) * LANES]

    def head_rinv(zb):
        ms = jnp.dot((zb * zb).astype(BF16), bd, preferred_element_type=F32)
        return lax.rsqrt(ms + EPS)

    def put(j, v):
        o_ref[0, :, j * LANES:(j + 1) * LANES] = v.astype(BF16)

    def roped(j, j_sw, g_row, g_sw_row):
        zb = blk(j)
        r = head_rinv(zb)
        put(j, (zb * r * g_ref[g_row:g_row + 1, :]) * cos + (blk(j_sw) * r * g_ref[g_sw_row:g_sw_row + 1, :]) * sin)

    def normed(j, g_row):
        zb = blk(j)
        put(j, zb * head_rinv(zb) * g_ref[g_row:g_row + 1, :])

    for j in range(4):
        roped(_QA + j, _QA_SW + j, 0, 4)
        normed(_QB + j, 1)
        normed(_KB + j, 2)
        put(_VB + j, blk(_VB + j))
    roped(_KA, _KA_SW, 3, 5)
    put(_VA, blk(_VA))


def _inproj(x, mod, n1, w_ext, gains, cos_t, sin_t, bd, *, mod_row, tm):
    bsz, s, d = x.shape
    kern = functools.partial(_inproj_kernel, mod_row=mod_row, d=d)
    return pl.pallas_call(
        kern,
        out_shape=jax.ShapeDtypeStruct((bsz, s, _OUT_BLOCKS * LANES), BF16),
        grid=(bsz, s // tm),
        in_specs=[pl.BlockSpec((1, tm, d), lambda b, i: (b, i, 0)),
                  pl.BlockSpec(mod.shape, lambda b, i: (0, 0)),
                  pl.BlockSpec((1, d), lambda b, i: (0, 0)),
                  pl.BlockSpec(w_ext.shape, lambda b, i: (0, 0)),
                  pl.BlockSpec(gains.shape, lambda b, i: (0, 0)),
                  pl.BlockSpec((tm, LANES), lambda b, i: (i, 0)),
                  pl.BlockSpec((tm, LANES), lambda b, i: (i, 0)),
                  pl.BlockSpec((LANES, LANES), lambda b, i: (0, 0))],
        out_specs=pl.BlockSpec((1, tm, _OUT_BLOCKS * LANES), lambda b, i: (b, i, 0)),
        compiler_params=_params("arbitrary", "arbitrary"),
        name="inproj",
    )(x, mod, n1, w_ext, gains, cos_t, sin_t, bd)


def _split_pair(qp, lo):
    zero = jnp.zeros_like(qp)
    return jnp.concatenate([jnp.where(lo, qp, zero), jnp.where(lo, zero, qp)], axis=0)


def _softmax_pv(s_parts, v_parts, sink_rep):
    chunks = [s[:, c * LANES:(c + 1) * LANES] for s in s_parts for c in range(s.shape[1] // LANES)]
    m = jnp.max(functools.reduce(jnp.maximum, chunks), axis=-1, keepdims=True)
    m_rep = jnp.broadcast_to(m, (m.shape[0], LANES))
    if sink_rep is not None:
        m_rep = jnp.maximum(m_rep, sink_rep)
    acc = None
    for s, v in zip(s_parts, v_parts):
        p = jnp.concatenate([jnp.exp2(s[:, c * LANES:(c + 1) * LANES] - m_rep)
                             for c in range(s.shape[1] // LANES)], axis=1).astype(BF16)
        v_ext = jnp.concatenate([v, jnp.ones_like(v)], axis=1)
        o = jnp.dot(p, v_ext, preferred_element_type=F32)
        acc = o if acc is None else acc + o
    l_rep = acc[:, LANES:]
    if sink_rep is not None:
        l_rep = l_rep + jnp.exp2(sink_rep - m_rep)
    return acc[:, :LANES] * (1.0 / l_rep)


def _attn_a_kernel(q_ref, kp_ref, kc_ref, kn_ref, vp_ref, vc_ref, vn_ref, kx_ref, vx_ref, sink_ref, mask_ref,
                   o_ref):
    tq = q_ref.shape[1]
    lo = lax.broadcasted_iota(I32, (tq, LANES), 1) < HEAD_DIM
    q = q_ref[0]
    qs = jnp.concatenate([_split_pair(q[:, p * LANES:(p + 1) * LANES], lo) for p in range(4)], axis=0)
    k = jnp.concatenate([kp_ref[0], kc_ref[0], kn_ref[0]], axis=0)
    v = jnp.concatenate([vp_ref[0], vc_ref[0], vn_ref[0]], axis=0)
    s_loc = lax.dot_general(qs, k, _NT, preferred_element_type=F32) + mask_ref[0]
    s_ctx = lax.dot_general(qs, kx_ref[0], _NT, preferred_element_type=F32)
    o = _softmax_pv([s_loc, s_ctx], [v, vx_ref[0]], sink_ref[...])
    for p in range(4):
        o_lo = o[(2 * p) * tq:(2 * p + 1) * tq]
        o_hi = o[(2 * p + 1) * tq:(2 * p + 2) * tq]
        o_ref[0, :, p * LANES:(p + 1) * LANES] = jnp.where(lo, o_lo, o_hi).astype(BF16)


def _attn_a(zq, zc, sink_rep, mask):
    bsz, s, _ = zq.shape
    lc = zc.shape[1]
    tq = A_WINDOW
    nblk = s // tq
    ka, va = _KA, _VA

    def kv_spec(col, shift):
        return pl.BlockSpec((1, tq, LANES), lambda b, j: (b, jnp.clip(j + shift, 0, nblk - 1), col))

    def variant(j):
        return jnp.where(j == 0, 0, jnp.where(j == nblk - 1, 2, 1))

    return pl.pallas_call(
        _attn_a_kernel,
        out_shape=jax.ShapeDtypeStruct((bsz, s, 4 * LANES), BF16),
        grid=(bsz, nblk),
        in_specs=[pl.BlockSpec((1, tq, 4 * LANES), lambda b, j: (b, j, 0)),
                  kv_spec(ka, -1), kv_spec(ka, 0), kv_spec(ka, 1),
                  kv_spec(va, -1), kv_spec(va, 0), kv_spec(va, 1),
                  pl.BlockSpec((1, lc, LANES), lambda b, j: (b, 0, ka)),
                  pl.BlockSpec((1, lc, LANES), lambda b, j: (b, 0, va)),
                  pl.BlockSpec(sink_rep.shape, lambda b, j: (0, 0)),
                  pl.BlockSpec((1,) + mask.shape[1:], lambda b, j: (variant(j), 0, 0))],
        out_specs=pl.BlockSpec((1, tq, 4 * LANES), lambda b, j: (b, j, 0)),
        compiler_params=_params("arbitrary", "arbitrary"),
        name="attn_a",
    )(zq, zq, zq, zq, zq, zq, zq, zc, zc, sink_rep, mask)


def _attn_a_mask(tq):
    qi = np.arange(tq)[:, None]
    kj = np.arange(3 * tq)[None, :]
    ok = (kj >= qi) & (kj <= qi + 2 * tq)
    variants = [ok & (kj >= tq), ok, ok & (kj < 2 * tq)]
    m = np.stack([np.tile(np.where(v, 0.0, NEG_INF).astype(np.float32), (A_Q_HEADS, 1)) for v in variants])
    return jnp.asarray(m)


def _attn_b_kernel(q_ref, k0_ref, k1_ref, k2_ref, v0_ref, v1_ref, v2_ref, kx_ref, vx_ref, tab_ref, o_ref):
    tq = q_ref.shape[1]
    lo = lax.broadcasted_iota(I32, (tq, LANES), 1) < HEAD_DIM
    for p in range(4):
        sl = slice(p * LANES, (p + 1) * LANES)
        qs = _split_pair(q_ref[0, :, sl], lo)
        k = jnp.concatenate([k0_ref[0, :, sl], k1_ref[0, :, sl], k2_ref[0, :, sl]], axis=0)
        v = jnp.concatenate([v0_ref[0, :, sl], v1_ref[0, :, sl], v2_ref[0, :, sl]], axis=0)
        s_loc = lax.dot_general(qs, k, _NT, preferred_element_type=F32) + tab_ref[0, p]
        s_ctx = lax.dot_general(qs, kx_ref[0, :, sl], _NT, preferred_element_type=F32)
        o = _softmax_pv([s_loc, s_ctx], [v, vx_ref[0, :, sl]], None)
        o_ref[0, :, sl] = jnp.where(lo, o[:tq], o[tq:]).astype(BF16)


def _attn_b(zq, zc, table):
    bsz, s, _ = zq.shape
    lc = zc.shape[1]
    tq = NA_QROWS * GRID_W
    ng = s // tq
    qb, kb, vb = _QB // 4, _KB // 4, _VB // 4

    def kv_spec(col, off):
        return pl.BlockSpec((1, tq, 4 * LANES), lambda i, b: (b, jnp.clip(i - 1, 0, ng - 3) + off, col))

    def variant(i):
        return jnp.where(i == 0, 0, jnp.where(i == ng - 1, 2, 1))

    return pl.pallas_call(
        _attn_b_kernel,
        out_shape=jax.ShapeDtypeStruct((bsz, s, 4 * LANES), BF16),
        grid=(ng, bsz),
        in_specs=[pl.BlockSpec((1, tq, 4 * LANES), lambda i, b: (b, i, qb)),
                  kv_spec(kb, 0), kv_spec(kb, 1), kv_spec(kb, 2),
                  kv_spec(vb, 0), kv_spec(vb, 1), kv_spec(vb, 2),
                  pl.BlockSpec((1, lc, 4 * LANES), lambda i, b: (b, 0, kb)),
                  pl.BlockSpec((1, lc, 4 * LANES), lambda i, b: (b, 0, vb)),
                  pl.BlockSpec((1,) + table.shape[1:], lambda i, b: (variant(i), 0, 0, 0))],
        out_specs=pl.BlockSpec((1, tq, 4 * LANES), lambda i, b: (b, i, 0)),
        compiler_params=_params("arbitrary", "arbitrary"),
        name="attn_b",
    )(zq, zq, zq, zq, zq, zq, zq, zc, zc, table)


def _na_table(rpb, rows):
    ng = rows // NA_QROWS
    nq, nk = NA_QROWS * GRID_W, NA_KROWS * GRID_W
    qc = np.arange(GRID_W)
    kc = np.arange(GRID_W)
    ws = np.clip(qc - NA_KW // 2, 0, GRID_W - NA_KW)
    valid_c = (kc[None, :] >= ws[:, None]) & (kc[None, :] < ws[:, None] + NA_KW)
    dc = np.clip(kc[None, :] - qc[:, None], -(NA_KW - 1), NA_KW - 1) + (NA_KW - 1)
    c_sel = (dc[..., None] == np.arange(2 * NA_KW - 1)) & valid_c[..., None]
    r_sels, valids = [], []
    for i in (0, 1, ng - 1):
        start = int(np.clip(NA_QROWS * i - NA_KH // 2, 0, rows - NA_KROWS))
        r = NA_QROWS * i + np.arange(NA_QROWS)
        rs = np.clip(r - NA_KH // 2, 0, rows - NA_KH)
        krow = start + np.arange(NA_KROWS)
        valid_r = (krow[None, :] >= rs[:, None]) & (krow[None, :] < rs[:, None] + NA_KH)
        dr = krow[None, :] - r[:, None] + (NA_KH - 1)
        r_sels.append((dr[..., None] == np.arange(2 * NA_KH - 1)) & valid_r[..., None])
        valids.append((valid_r[:, None, :, None] & valid_c[None, :, None, :]).reshape(nq, nk))
    r_sel = jnp.asarray(np.stack(r_sels), F32)
    bias = jnp.einsum('wxya,hab,uvb->whxuyv', r_sel, rpb.astype(F32), jnp.asarray(c_sel, F32),
                      precision=lax.Precision.HIGHEST).reshape(3, B_HEADS, nq, nk)
    tab = jnp.where(jnp.asarray(np.stack(valids))[:, None], bias * LOG2E, NEG_INF)
    return tab.reshape(3, B_HEADS // 2, 2 * nq, nk)


def _merge_kernel(oa_ref, ob_ref, x_ref, mod_ref, ga_ref, gb_ref, wo_ref, n2_ref, wr_ref, rb_ref,
                  wsg_ref, wsu_ref, wsd_ref, xres_ref, hpa_ref, hpb_ref, idx_ref, gate_ref, *, d):
    b = pl.program_id(0)
    tm = x_ref.shape[1]

    def mod(k):
        return mod_ref[pl.ds(b, 1), k * d:(k + 1) * d]

    na = _rms(oa_ref[0].astype(F32), ga_ref[...])
    nb = _rms(ob_ref[0].astype(F32), gb_ref[...])
    cat = jnp.concatenate([na, nb], axis=1).astype(BF16)
    y = jnp.dot(cat, wo_ref[...], preferred_element_type=F32)
    x1 = x_ref[0] + mod(2) * y
    h2 = _rms(x1, n2_ref[...]) * (1.0 + mod(4)) + mod(3)

    hb = h2.astype(BF16)
    act = _silu(jnp.dot(hb, wsg_ref[...], preferred_element_type=F32)) * jnp.dot(hb, wsu_ref[...],
                                                                                 preferred_element_type=F32)
    shared = jnp.dot(act.astype(BF16), wsd_ref[...], preferred_element_type=F32)
    xres_ref[0] = x1 + mod(5) * shared
    packed = _pack_bf16_pairs(h2)
    hpa_ref[...] = packed[:, :d // 4]
    hpb_ref[...] = packed[:, d // 4:]

    logits = lax.dot_general(wr_ref[...], h2, _NT, preferred_element_type=F32, precision=lax.Precision.HIGHEST)
    scores = jax.nn.sigmoid(logits)
    sel = scores + rb_ref[...]
    per = N_EXPERTS // N_GROUPS
    g3 = sel.reshape(N_GROUPS, per, tm)
    it3 = lax.broadcasted_iota(I32, (N_GROUPS, per, tm), 1)
    m1 = jnp.max(g3, axis=1, keepdims=True)
    first = jnp.min(jnp.where(g3 == m1, it3, per), axis=1, keepdims=True)
    m2 = jnp.max(jnp.where(it3 == first, -jnp.inf, g3), axis=1, keepdims=True)
    gscore = (m1 + m2).reshape(N_GROUPS, tm)

    itg = lax.broadcasted_iota(I32, (N_GROUPS, tm), 0)
    gsel = jnp.zeros((N_GROUPS, tm), F32)
    cur = gscore
    for _ in range(TOPK_GROUPS):
        mx = jnp.max(cur, axis=0, keepdims=True)
        fi = jnp.min(jnp.where(cur == mx, itg, N_GROUPS), axis=0, keepdims=True)
        pick = itg == fi
        gsel = jnp.where(pick, 1.0, gsel)
        cur = jnp.where(pick, -jnp.inf, cur)
    emask = jnp.broadcast_to(gsel.reshape(N_GROUPS, 1, tm), (N_GROUPS, per, tm)).reshape(N_EXPERTS, tm) > 0.5

    ite = lax.broadcasted_iota(I32, (N_EXPERTS, tm), 0)
    cur = jnp.where(emask, sel, NEG_INF)
    idx_rows, s_rows = [], []
    for _ in range(TOP_K):
        mx = jnp.max(cur, axis=0, keepdims=True)
        fi = jnp.min(jnp.where(cur == mx, ite, N_EXPERTS), axis=0, keepdims=True)
        pick = ite == fi
        idx_rows.append(fi)
        s_rows.append(jnp.sum(jnp.where(pick, scores, 0.0), axis=0, keepdims=True))
        cur = jnp.where(pick, -jnp.inf, cur)
    top_s = jnp.concatenate(s_rows, axis=0)
    idx_ref[...] = jnp.concatenate(idx_rows, axis=0)
    gate_ref[...] = top_s / jnp.sum(top_s, axis=0, keepdims=True) * ROUTED_SCALE


def _merge(o_a, o_b, x, mod, ga, gb, wo, n2, wr_t, rb_col, wsg, wsu, wsd, *, tm):
    bsz, s, d = x.shape
    nt = s // tm
    n = bsz * s
    full = lambda a: pl.BlockSpec(a.shape, lambda b, i: (0,) * a.ndim)
    return pl.pallas_call(
        functools.partial(_merge_kernel, d=d),
        out_shape=(jax.ShapeDtypeStruct((bsz, s, d), F32),
                   jax.ShapeDtypeStruct((n, d // 4), U32),
                   jax.ShapeDtypeStruct((n, d // 4), U32),
                   jax.ShapeDtypeStruct((TOP_K, n), I32),
                   jax.ShapeDtypeStruct((TOP_K, n), F32)),
        grid=(bsz, nt),
        in_specs=[pl.BlockSpec((1, tm, d // 2), lambda b, i: (b, i, 0)),
                  pl.BlockSpec((1, tm, d // 2), lambda b, i: (b, i, 0)),
                  pl.BlockSpec((1, tm, d), lambda b, i: (b, i, 0)),
                  full(mod), full(ga), full(gb), full(wo), full(n2), full(wr_t), full(rb_col),
                  full(wsg), full(wsu), full(wsd)],
        out_specs=(pl.BlockSpec((1, tm, d), lambda b, i: (b, i, 0)),
                   pl.BlockSpec((tm, d // 4), lambda b, i: (b * nt + i, 0)),
                   pl.BlockSpec((tm, d // 4), lambda b, i: (b * nt + i, 0)),
                   pl.BlockSpec((TOP_K, tm), lambda b, i: (0, b * nt + i)),
                   pl.BlockSpec((TOP_K, tm), lambda b, i: (0, b * nt + i))),
        compiler_params=_params("arbitrary", "arbitrary"),
        name="merge",
    )(o_a, o_b, x, mod, ga, gb, wo, n2, wr_t, rb_col, wsg, wsu, wsd)


def _rank_kernel(idx_ref, rank_ref, cnt_ref, carry_ref):
    tm = idx_ref.shape[1]

    @pl.when(pl.program_id(0) == 0)
    def _():
        carry_ref[...] = jnp.zeros_like(carry_ref)

    idx = idx_ref[...]
    ite = lax.broadcasted_iota(I32, (N_EXPERTS, tm), 0)
    before = (lax.broadcasted_iota(I32, (tm, tm), 0) < lax.broadcasted_iota(I32, (tm, tm), 1)).astype(BF16)
    base = carry_ref[...]
    rows = []
    for k in range(TOP_K):
        oh = ite == idx[k:k + 1, :]
        ohf = jnp.where(oh, 1.0, 0.0)
        cum = jnp.dot(ohf.astype(BF16), before, preferred_element_type=F32)
        rows.append(jnp.sum(jnp.where(oh, cum + base, 0.0), axis=0, keepdims=True))
        base = base + jnp.sum(ohf, axis=1, keepdims=True)
    rank_ref[...] = jnp.concatenate(rows, axis=0).astype(I32)
    carry_ref[...] = base
    cnt_ref[...] = base


def _rank(idx, *, tm):
    n = idx.shape[1]
    return pl.pallas_call(
        _rank_kernel,
        out_shape=(jax.ShapeDtypeStruct((TOP_K, n), I32), jax.ShapeDtypeStruct((N_EXPERTS, 1), F32)),
        grid=(n // tm,),
        in_specs=[pl.BlockSpec((TOP_K, tm), lambda i: (0, i))],
        out_specs=(pl.BlockSpec((TOP_K, tm), lambda i: (0, i)), pl.BlockSpec((N_EXPERTS, 1), lambda i: (0, 0))),
        scratch_shapes=[pltpu.VMEM((N_EXPERTS, 1), F32)],
        compiler_params=_params("arbitrary"),
        name="rank",
    )(idx)


def _pos_kernel(idx_ref, rank_ref, pstart_ref, pos_ref):
    tm = idx_ref.shape[1]
    idx = idx_ref[...]
    ite = lax.broadcasted_iota(I32, (N_EXPERTS, tm), 0)
    pstart = pstart_ref[...]
    rows = [jnp.sum(jnp.where(ite == idx[k:k + 1, :], pstart, 0.0), axis=0, keepdims=True) for k in range(TOP_K)]
    pos_ref[...] = jnp.concatenate(rows, axis=0).astype(I32) + rank_ref[...]


def _pos(idx, rank, pstart_col, *, tm):
    n = idx.shape[1]
    return pl.pallas_call(
        _pos_kernel,
        out_shape=jax.ShapeDtypeStruct((TOP_K, n), I32),
        grid=(n // tm,),
        in_specs=[pl.BlockSpec((TOP_K, tm), lambda i: (0, i)),
                  pl.BlockSpec((TOP_K, tm), lambda i: (0, i)),
                  pl.BlockSpec((N_EXPERTS, 1), lambda i: (0, 0))],
        out_specs=pl.BlockSpec((TOP_K, tm), lambda i: (0, i)),
        compiler_params=_params("arbitrary"),
        name="pos",
    )(idx, rank, pstart_col)


RING_SLOTS = 4


def _expert_kernel(ps_ref, cnt_ref, tot_ref, wg_ref, wu_ref, wd_ref, xa_hbm, xb_hbm, ya_hbm, yb_hbm,
                   wg_s, wu_s, wd_s, xa_buf, xb_buf, ya_buf, yb_buf, in_sem, out_sem):
    e = pl.program_id(0)
    r = EXPERT_ROWS
    ahead = RING_SLOTS - 1
    total = tot_ref[0]
    first = ps_ref[e] // r
    cnt = cnt_ref[e]
    nch = (cnt + r - 1) // r

    def rows_of(g):
        return pl.ds(pl.multiple_of(g * r, r), r)

    def fetch(g):
        slot = g % RING_SLOTS
        return (pltpu.make_async_copy(xa_hbm.at[rows_of(g)], xa_buf.at[slot], in_sem.at[0, slot]),
                pltpu.make_async_copy(xb_hbm.at[rows_of(g)], xb_buf.at[slot], in_sem.at[1, slot]))

    def flush(g):
        slot = g % RING_SLOTS
        return (pltpu.make_async_copy(ya_buf.at[slot], ya_hbm.at[rows_of(g)], out_sem.at[0, slot]),
                pltpu.make_async_copy(yb_buf.at[slot], yb_hbm.at[rows_of(g)], out_sem.at[1, slot]))

    def start(copies):
        for cp in copies:
            cp.start()

    def wait(copies):
        for cp in copies:
            cp.wait()

    @pl.when(e == 0)
    def _():
        for g in range(ahead):
            @pl.when(g < total)
            def _():
                start(fetch(g))

    @pl.when(nch > 0)
    def _():
        wg_s[...] = wg_ref[0].astype(BF16)
        wu_s[...] = wu_ref[0].astype(BF16)
        wd_s[...] = wd_ref[0].astype(BF16)

        def chunk(c, carry):
            g = first + c
            slot = g % RING_SLOTS
            wait(fetch(g))

            @pl.when(g + ahead < total)
            def _():
                start(fetch(g + ahead))

            @pl.when(g >= RING_SLOTS)
            def _():
                wait(flush(g - RING_SLOTS))

            packed = jnp.concatenate([xa_buf[slot], xb_buf[slot]], axis=1)
            row = lax.broadcasted_iota(I32, packed.shape, 0)
            packed = jnp.where(row < cnt - c * r, packed, jnp.zeros_like(packed))
            x = _unpack_bf16_pairs(packed).astype(BF16)
            gate = jnp.dot(x, wg_s[...], preferred_element_type=F32)
            up = jnp.dot(x, wu_s[...], preferred_element_type=F32)
            y = jnp.dot((_silu(gate) * up).astype(BF16), wd_s[...], preferred_element_type=F32)
            out = _pack_bf16_pairs(y)
            half = out.shape[1] // 2
            ya_buf[slot] = out[:, :half]
            yb_buf[slot] = out[:, half:]
            start(flush(g))
            return carry

        lax.fori_loop(0, nch, chunk, 0)

    @pl.when(e == pl.num_programs(0) - 1)
    def _():
        for k in range(RING_SLOTS):
            @pl.when(total - 1 - k >= 0)
            def _():
                wait(flush(total - 1 - k))


def _experts(pstart, cnt, total_chunks, xs_a, xs_b, wg, wu, wd):
    rows, hw = xs_a.shape
    r = EXPERT_ROWS
    n_exp, d, f = wg.shape
    hbm = pl.BlockSpec(memory_space=pl.ANY)
    grid_spec = pltpu.PrefetchScalarGridSpec(
        num_scalar_prefetch=3,
        grid=(n_exp,),
        in_specs=[pl.BlockSpec((1, d, f), lambda e, ps, cn, tot: (e, 0, 0)),
                  pl.BlockSpec((1, d, f), lambda e, ps, cn, tot: (e, 0, 0)),
                  pl.BlockSpec((1, f, d), lambda e, ps, cn, tot: (e, 0, 0)),
                  hbm, hbm],
        out_specs=(hbm, hbm),
        scratch_shapes=[pltpu.VMEM((d, f), BF16), pltpu.VMEM((d, f), BF16), pltpu.VMEM((f, d), BF16),
                        pltpu.VMEM((RING_SLOTS, r, hw), U32), pltpu.VMEM((RING_SLOTS, r, hw), U32),
                        pltpu.VMEM((RING_SLOTS, r, hw), U32), pltpu.VMEM((RING_SLOTS, r, hw), U32),
                        pltpu.SemaphoreType.DMA((2, RING_SLOTS)), pltpu.SemaphoreType.DMA((2, RING_SLOTS))],
    )
    return pl.pallas_call(
        _expert_kernel,
        out_shape=(jax.ShapeDtypeStruct((rows, hw), U32), jax.ShapeDtypeStruct((rows, hw), U32)),
        grid_spec=grid_spec,
        compiler_params=_params("arbitrary"),
        name="experts",
    )(pstart, cnt, total_chunks, wg, wu, wd, xs_a, xs_b)


def _sc_scatter_rows(x, idx_flat, n_rows):
    n, w = x.shape
    m = idx_flat.shape[1]
    nwin = n // SC_WINDOW
    mesh = plsc.VectorSubcoreMesh(core_axis_name="core", subcore_axis_name="subcore")

    @functools.partial(pl.kernel, out_type=jax.ShapeDtypeStruct((n_rows, w), x.dtype), mesh=mesh,
                       scratch_types=[])
    def scatter_kernel(x_hbm, i_hbm, o_hbm):
        def body(x_vmem, i_vmem):
            pltpu.sync_copy(x_vmem, o_hbm.at[i_vmem.at[0]])

        pltpu.emit_pipeline(
            body,
            grid=(m // SC_WINDOW,),
            in_specs=[pl.BlockSpec((SC_WINDOW, w), lambda i: (i % nwin, 0)),
                      pl.BlockSpec((1, SC_WINDOW), lambda i: (0, i))],
            out_specs=[],
            core_axis_name=("core", "subcore"),
            dimension_semantics=(pltpu.PARALLEL,),
        )(x_hbm, i_hbm)

    return scatter_kernel(x, idx_flat)


def _sc_gather_rows(src, idx_flat):
    m = idx_flat.shape[1]
    w = src.shape[1]
    mesh = plsc.VectorSubcoreMesh(core_axis_name="core", subcore_axis_name="subcore")

    @functools.partial(pl.kernel, out_type=jax.ShapeDtypeStruct((m, w), src.dtype), mesh=mesh)
    def gather_kernel(x_hbm, i_hbm, o_hbm):
        def body(i_vmem, o_vmem):
            pltpu.sync_copy(x_hbm.at[i_vmem.at[0]], o_vmem)

        pltpu.emit_pipeline(
            body,
            grid=(m // SC_WINDOW,),
            in_specs=[pl.BlockSpec((1, SC_WINDOW), lambda i: (0, i))],
            out_specs=[pl.BlockSpec((SC_WINDOW, w), lambda i: (i, 0))],
            core_axis_name=("core", "subcore"),
            dimension_semantics=(pltpu.PARALLEL,),
        )(i_hbm, o_hbm)

    return gather_kernel(src, idx_flat)


def _finish_kernel(gate_ref, xres_ref, mod_ref, ya_ref, yb_ref, o_ref, *, d, tiles_per_batch):
    tm = gate_ref.shape[1]
    b = pl.program_id(0) // tiles_per_batch
    gates = gate_ref[...].T
    acc = jnp.zeros((tm, d), F32)
    for k in range(TOP_K):
        packed = jnp.concatenate([ya_ref[k], yb_ref[k]], axis=1)
        acc = acc + _unpack_bf16_pairs(packed) * gates[:, k:k + 1]
    o_ref[...] = xres_ref[...] + mod_ref[pl.ds(b, 1), 5 * d:6 * d] * acc


def _finish(gates, xres, mod, yg_a, yg_b, *, tm, tiles_per_batch):
    n, d = xres.shape
    return pl.pallas_call(
        functools.partial(_finish_kernel, d=d, tiles_per_batch=tiles_per_batch),
        out_shape=jax.ShapeDtypeStruct((n, d), F32),
        grid=(n // tm,),
        in_specs=[pl.BlockSpec((TOP_K, tm), lambda i: (0, i)),
                  pl.BlockSpec((tm, d), lambda i: (i, 0)),
                  pl.BlockSpec(mod.shape, lambda i: (0, 0)),
                  pl.BlockSpec((TOP_K, tm, d // 4), lambda i: (0, i, 0)),
                  pl.BlockSpec((TOP_K, tm, d // 4), lambda i: (0, i, 0))],
        out_specs=pl.BlockSpec((tm, d), lambda i: (i, 0)),
        compiler_params=_params("arbitrary"),
        name="finish",
    )(gates, xres, mod, yg_a, yg_b)


_QA_HEAD_ORDER = (0, 4, 1, 5, 2, 6, 3, 7)


def _reorder_qa_heads(a, axis):
    return jnp.concatenate([lax.slice_in_dim(a, h * HEAD_DIM, (h + 1) * HEAD_DIM, axis=axis)
                            for h in _QA_HEAD_ORDER], axis=axis)


def _rope_swap(a):
    quarter = HEAD_DIM // 4
    lead = a.shape[:-1]
    return jnp.flip(a.reshape(lead + (-1, 2, quarter)), axis=-2).reshape(a.shape)


def _rope_tables(s):
    quarter = HEAD_DIM // 4
    t = jnp.arange(s)
    row = (t // GRID_W).astype(F32)
    col = (t % GRID_W).astype(F32)
    freqs = ROPE_BASE ** (-jnp.arange(quarter, dtype=F32) / quarter)
    ar = row[:, None] * freqs[None, :]
    ac = col[:, None] * freqs[None, :]
    cos = jnp.concatenate([jnp.cos(ar), jnp.cos(ar), jnp.cos(ac), jnp.cos(ac)], axis=1)
    sin = jnp.concatenate([-jnp.sin(ar), jnp.sin(ar), -jnp.sin(ac), jnp.sin(ac)], axis=1)
    return jnp.tile(cos, (1, 2)), jnp.tile(sin, (1, 2))


def kernel(x, c, ctx, c_ctx, w_ada, b_ada, norm1, norm2, w_in, q_norm_a, k_norm_a, q_norm_b, k_norm_b, sink_a,
           rpb_b, out_norm_a, out_norm_b, w_out, w_router, router_bias, we_gate, we_up, we_down, ws_gate, ws_up,
           ws_down):
    assert w_ada.shape[0] == 1, "single-layer block"
    bsz, s, d = x.shape
    lc = ctx.shape[1]
    n = bsz * s
    rows = s // GRID_W
    assert s % (NA_QROWS * GRID_W) == 0 and rows >= NA_KROWS and bsz <= 4 and d == 1024

    c8 = jnp.concatenate([c, c_ctx[None, :], jnp.zeros((8 - bsz - 1, d), F32)], axis=0)
    mod = _ada(c8, w_ada[0], b_ada[0][None, :])

    w = w_in[0]
    aw, akw, bw = A_Q_HEADS * HEAD_DIM, A_KV_HEADS * HEAD_DIM, B_HEADS * HEAD_DIM
    cuts = np.cumsum([0, aw, akw, akw, bw, bw, bw])
    qa_w, ka_w, va_w, qb_w, kb_w, vb_w = [w[:, cuts[i]:cuts[i + 1]] for i in range(6)]
    qa_w = _reorder_qa_heads(qa_w, 1)
    w_ext = jnp.concatenate([qa_w, qb_w, kb_w, vb_w, ka_w, va_w, _rope_swap(qa_w), _rope_swap(ka_w)],
                            axis=1).astype(BF16)
    scale = HEAD_DIM ** -0.5 * LOG2E
    pair = lambda g: jnp.tile(g, 2)
    gains = jnp.stack([pair(q_norm_a[0]) * scale, pair(q_norm_b[0]) * scale, pair(k_norm_b[0]), pair(k_norm_a[0]),
                       pair(_rope_swap(q_norm_a[0])) * scale, pair(_rope_swap(k_norm_a[0])),
                       jnp.zeros((LANES,), F32), jnp.zeros((LANES,), F32)])
    bd = jnp.asarray(np.kron(np.eye(2), np.full((HEAD_DIM, HEAD_DIM), 1.0 / HEAD_DIM)), BF16)
    cos_t, sin_t = _rope_tables(s)
    n1 = norm1[0][None, :]
    zq = _inproj(x, mod, n1, w_ext, gains, cos_t, sin_t, bd, mod_row=None, tm=512)
    zc = _inproj(ctx, mod, n1, w_ext, gains, jnp.ones((lc, LANES), F32), jnp.zeros((lc, LANES), F32), bd,
                 mod_row=bsz, tm=lc)

    sink = sink_a[0].astype(F32)
    sink_rep = jnp.concatenate([jnp.broadcast_to(sink[h] * LOG2E, (A_WINDOW, LANES)) for h in _QA_HEAD_ORDER])
    o_a = _attn_a(zq, zc, sink_rep, _attn_a_mask(A_WINDOW))
    o_b = _attn_b(zq, zc, _na_table(rpb_b[0], rows))

    ga = _reorder_qa_heads(out_norm_a[0], 0)[None, :]
    gb = out_norm_b[0][None, :]
    wo = jnp.concatenate([_reorder_qa_heads(w_out[0][:aw], 0), w_out[0][aw:]], axis=0).astype(BF16)
    xres, hp_a, hp_b, idx, gates = _merge(o_a, o_b, x, mod, ga, gb, wo, norm2[0][None, :], w_router[0].T,
                                  router_bias[0][:, None], ws_gate[0].astype(BF16), ws_up[0].astype(BF16),
                                  ws_down[0].astype(BF16), tm=512)

    rank, counts = _rank(idx, tm=512)
    t = EXPERT_ROWS
    cnt = counts[:, 0].astype(I32)
    padded = (cnt + t - 1) // t * t
    pends = jnp.cumsum(padded)
    pstart = pends - padded
    n_rows = n * TOP_K + N_EXPERTS * t
    pos = _pos(idx, rank, pstart.astype(F32)[:, None], tm=512)

    pos_flat = pos.reshape(1, TOP_K * n)
    xs_a = _sc_scatter_rows(hp_a, pos_flat, n_rows)
    xs_b = _sc_scatter_rows(hp_b, pos_flat, n_rows)
    ys_a, ys_b = _experts(pstart, cnt, pends[-1:] // t, xs_a, xs_b, we_gate[0], we_up[0], we_down[0])
    yg_a = _sc_gather_rows(ys_a, pos_flat).reshape(TOP_K, n, d // 4)
    yg_b = _sc_gather_rows(ys_b, pos_flat).reshape(TOP_K, n, d // 4)
    out = _finish(gates, xres.reshape(n, d), mod, yg_a, yg_b, tm=256, tiles_per_batch=s // 256)
    return out.reshape(bsz, s, d)
```

```python
import functools

import numpy as np
import jax
import jax.numpy as jnp
from jax import lax
from jax.experimental import pallas as pl
from jax.experimental.pallas import tpu as pltpu
from jax.experimental.pallas import tpu_sc as plsc

F32 = jnp.float32
BF16 = jnp.bfloat16
I32 = jnp.int32
U32 = jnp.uint32

LANES = 128
HEAD_DIM = 64
HEAD_PAIR = 2 * HEAD_DIM
GRID_W = 64
A_Q_HEADS = 8
A_KV_HEADS = 2
A_WINDOW = 128
B_HEADS = 8
NA_KH = 8
NA_KW = 16
NA_QROWS = 4
NA_KROWS = NA_QROWS + NA_KH
ROPE_BASE = 10000.0
N_EXPERTS = 256
TOP_K = 8
N_GROUPS = 8
TOPK_GROUPS = 4
ROUTED_SCALE = 2.5
LOG2E = 1.4426950408889634
EPS = 1e-6
NEG_INF = -1e30
EXPERT_ROWS = 256
SC_WINDOW = 128
VMEM_LIMIT = 56 * 1024 * 1024

_NT = (((1,), (1,)), ((), ()))


def _params(*sem):
    return pltpu.CompilerParams(dimension_semantics=sem, vmem_limit_bytes=VMEM_LIMIT)


def _silu(v):
    return v * jax.nn.sigmoid(v)


def _rms(v, gain):
    return v * lax.rsqrt(jnp.mean(v * v, axis=-1, keepdims=True) + EPS) * gain


def _pack_bf16_pairs(v):
    n = v.shape[1] // 2
    lo = lax.bitcast_convert_type(v[:, :n].astype(BF16).astype(F32), U32) >> 16
    hi = lax.bitcast_convert_type(v[:, n:].astype(BF16).astype(F32), U32) & jnp.uint32(0xFFFF0000)
    return hi | lo


def _split_bf16(v):
    hi = lax.bitcast_convert_type(lax.bitcast_convert_type(v, U32) & jnp.uint32(0xFFFF0000), F32)
    return hi.astype(BF16), (v - hi).astype(BF16)


def _unpack_bf16_pairs(w):
    lo = lax.bitcast_convert_type(w << 16, F32)
    hi = lax.bitcast_convert_type(w & jnp.uint32(0xFFFF0000), F32)
    return jnp.concatenate([lo, hi], axis=1)


def _ada_kernel(c_ref, w_ref, b_ref, o_ref):
    a = _silu(c_ref[...])
    o_ref[...] = jnp.dot(a, w_ref[...], preferred_element_type=F32,
                         precision=lax.Precision.HIGHEST) + b_ref[...]


def _ada(c8, w, b):
    d, n = w.shape
    bn = n // 4
    return pl.pallas_call(
        _ada_kernel,
        out_shape=jax.ShapeDtypeStruct((8, n), F32),
        grid=(n // bn,),
        in_specs=[pl.BlockSpec((8, d), lambda j: (0, 0)),
                  pl.BlockSpec((d, bn), lambda j: (0, j)),
                  pl.BlockSpec((1, bn), lambda j: (0, j))],
        out_specs=pl.BlockSpec((8, bn), lambda j: (0, j)),
        compiler_params=_params("arbitrary"),
        name="ada",
    )(c8, w, b)


_QA, _QB, _KB, _VB, _KA, _VA, _QA_SW, _KA_SW = 0, 4, 8, 12, 16, 17, 18, 22
_OUT_BLOCKS = 18
_EXT_BLOCKS = 23


def _inproj_kernel(x_ref, mod_ref, n1_ref, w_ref, g_ref, cos_ref, sin_ref, bd_ref, o_ref, *, mod_row, d):
    b = pl.program_id(0) if mod_row is None else mod_row
    xn = _rms(x_ref[0], n1_ref[...])
    sh = mod_ref[pl.ds(b, 1), 0:d]
    sc = mod_ref[pl.ds(b, 1), d:2 * d]
    h = (xn * (1.0 + sc) + sh).astype(BF16)
    z = jnp.dot(h, w_ref[...], preferred_element_type=F32)
    bd = bd_ref[...]
    cos = cos_ref[...]
    sin = sin_ref[...]

    def blk(j):
        return z[:, j * LANES:(j + 1) * LANES]

    def head_rinv(zb):
        ms = jnp.dot((zb * zb).astype(BF16), bd, preferred_element_type=F32)
        return lax.rsqrt(ms + EPS)

    def put(j, v):
        o_ref[0, :, j * LANES:(j + 1) * LANES] = v.astype(BF16)

    def roped(j, j_sw, g_row, g_sw_row):
        zb = blk(j)
        r = head_rinv(zb)
        put(j, (zb * r * g_ref[g_row:g_row + 1, :]) * cos + (blk(j_sw) * r * g_ref[g_sw_row:g_sw_row + 1, :]) * sin)

    def normed(j, g_row):
        zb = blk(j)
        put(j, zb * head_rinv(zb) * g_ref[g_row:g_row + 1, :])

    for j in range(4):
        roped(_QA + j, _QA_SW + j, 0, 4)
        normed(_QB + j, 1)
        normed(_KB + j, 2)
        put(_VB + j, blk(_VB + j))
    roped(_KA, _KA_SW, 3, 5)
    put(_VA, blk(_VA))


def _inproj(x, mod, n1, w_ext, gains, cos_t, sin_t, bd, *, mod_row, tm):
    bsz, s, d = x.shape
    kern = functools.partial(_inproj_kernel, mod_row=mod_row, d=d)
    return pl.pallas_call(
        kern,
        out_shape=jax.ShapeDtypeStruct((bsz, s, _OUT_BLOCKS * LANES), BF16),
        grid=(bsz, s // tm),
        in_specs=[pl.BlockSpec((1, tm, d), lambda b, i: (b, i, 0)),
                  pl.BlockSpec(mod.shape, lambda b, i: (0, 0)),
                  pl.BlockSpec((1, d), lambda b, i: (0, 0)),
                  pl.BlockSpec(w_ext.shape, lambda b, i: (0, 0)),
                  pl.BlockSpec(gains.shape, lambda b, i: (0, 0)),
                  pl.BlockSpec((tm, LANES), lambda b, i: (i, 0)),
                  pl.BlockSpec((tm, LANES), lambda b, i: (i, 0)),
                  pl.BlockSpec((LANES, LANES), lambda b, i: (0, 0))],
        out_specs=pl.BlockSpec((1, tm, _OUT_BLOCKS * LANES), lambda b, i: (b, i, 0)),
        compiler_params=_params("arbitrary", "arbitrary"),
        name="inproj",
    )(x, mod, n1, w_ext, gains, cos_t, sin_t, bd)


def _split_pair(qp, lo):
    zero = jnp.zeros_like(qp)
    return jnp.concatenate([jnp.where(lo, qp, zero), jnp.where(lo, zero, qp)], axis=0)


def _softmax_pv(s_parts, v_parts, sink_rep):
    chunks = [s[:, c * LANES:(c + 1) * LANES] for s in s_parts for c in range(s.shape[1] // LANES)]
    m = jnp.max(functools.reduce(jnp.maximum, chunks), axis=-1, keepdims=True)
    m_rep = jnp.broadcast_to(m, (m.shape[0], LANES))
    if sink_rep is not None:
        m_rep = jnp.maximum(m_rep, sink_rep)
    acc = None
    for s, v in zip(s_parts, v_parts):
        p = jnp.concatenate([jnp.exp2(s[:, c * LANES:(c + 1) * LANES] - m_rep)
                             for c in range(s.shape[1] // LANES)], axis=1).astype(BF16)
        v_ext = jnp.concatenate([v, jnp.ones_like(v)], axis=1)
        o = jnp.dot(p, v_ext, preferred_element_type=F32)
        acc = o if acc is None else acc + o
    l_rep = acc[:, LANES:]
    if sink_rep is not None:
        l_rep = l_rep + jnp.exp2(sink_rep - m_rep)
    return acc[:, :LANES] * (1.0 / l_rep)


def _attn_a_kernel(q_ref, kp_ref, kc_ref, kn_ref, vp_ref, vc_ref, vn_ref, kx_ref, vx_ref, sink_ref, mask_ref,
                   o_ref):
    tq = q_ref.shape[1]
    lo = lax.broadcasted_iota(I32, (tq, LANES), 1) < HEAD_DIM
    q = q_ref[0]
    qs = jnp.concatenate([_split_pair(q[:, p * LANES:(p + 1) * LANES], lo) for p in range(4)], axis=0)
    k = jnp.concatenate([kp_ref[0], kc_ref[0], kn_ref[0]], axis=0)
    v = jnp.concatenate([vp_ref[0], vc_ref[0], vn_ref[0]], axis=0)
    s_loc = lax.dot_general(qs, k, _NT, preferred_element_type=F32) + mask_ref[0]
    s_ctx = lax.dot_general(qs, kx_ref[0], _NT, preferred_element_type=F32)
    o = _softmax_pv([s_loc, s_ctx], [v, vx_ref[0]], sink_ref[...])
    for p in range(4):
        o_lo = o[(2 * p) * tq:(2 * p + 1) * tq]
        o_hi = o[(2 * p + 1) * tq:(2 * p + 2) * tq]
        o_ref[0, :, p * LANES:(p + 1) * LANES] = jnp.where(lo, o_lo, o_hi).astype(BF16)


def _attn_a(zq, zc, sink_rep, mask):
    bsz, s, _ = zq.shape
    lc = zc.shape[1]
    tq = A_WINDOW
    nblk = s // tq
    ka, va = _KA, _VA

    def kv_spec(col, shift):
        return pl.BlockSpec((1, tq, LANES), lambda b, j: (b, jnp.clip(j + shift, 0, nblk - 1), col))

    def variant(j):
        return jnp.where(j == 0, 0, jnp.where(j == nblk - 1, 2, 1))

    return pl.pallas_call(
        _attn_a_kernel,
        out_shape=jax.ShapeDtypeStruct((bsz, s, 4 * LANES), BF16),
        grid=(bsz, nblk),
        in_specs=[pl.BlockSpec((1, tq, 4 * LANES), lambda b, j: (b, j, 0)),
                  kv_spec(ka, -1), kv_spec(ka, 0), kv_spec(ka, 1),
                  kv_spec(va, -1), kv_spec(va, 0), kv_spec(va, 1),
                  pl.BlockSpec((1, lc, LANES), lambda b, j: (b, 0, ka)),
                  pl.BlockSpec((1, lc, LANES), lambda b, j: (b, 0, va)),
                  pl.BlockSpec(sink_rep.shape, lambda b, j: (0, 0)),
                  pl.BlockSpec((1,) + mask.shape[1:], lambda b, j: (variant(j), 0, 0))],
        out_specs=pl.BlockSpec((1, tq, 4 * LANES), lambda b, j: (b, j, 0)),
        compiler_params=_params("arbitrary", "arbitrary"),
        name="attn_a",
    )(zq, zq, zq, zq, zq, zq, zq, zc, zc, sink_rep, mask)


def _attn_a_mask(tq):
    qi = np.arange(tq)[:, None]
    kj = np.arange(3 * tq)[None, :]
    ok = (kj >= qi) & (kj <= qi + 2 * tq)
    variants = [ok & (kj >= tq), ok, ok & (kj < 2 * tq)]
    m = np.stack([np.tile(np.where(v, 0.0, NEG_INF).astype(np.float32), (A_Q_HEADS, 1)) for v in variants])
    return jnp.asarray(m)


def _attn_b_kernel(q_ref, k0_ref, k1_ref, k2_ref, v0_ref, v1_ref, v2_ref, kx_ref, vx_ref, tab_ref, o_ref):
    tq = q_ref.shape[1]
    lo = lax.broadcasted_iota(I32, (tq, LANES), 1) < HEAD_DIM
    for p in range(4):
        sl = slice(p * LANES, (p + 1) * LANES)
        qs = _split_pair(q_ref[0, :, sl], lo)
        k = jnp.concatenate([k0_ref[0, :, sl], k1_ref[0, :, sl], k2_ref[0, :, sl]], axis=0)
        v = jnp.concatenate([v0_ref[0, :, sl], v1_ref[0, :, sl], v2_ref[0, :, sl]], axis=0)
        s_loc = lax.dot_general(qs, k, _NT, preferred_element_type=F32) + tab_ref[0, p]
        s_ctx = lax.dot_general(qs, kx_ref[0, :, sl], _NT, preferred_element_type=F32)
        o = _softmax_pv([s_loc, s_ctx], [v, vx_ref[0, :, sl]], None)
        o_ref[0, :, sl] = jnp.where(lo, o[:tq], o[tq:]).astype(BF16)


def _attn_b(zq, zc, table):
    bsz, s, _ = zq.shape
    lc = zc.shape[1]
    tq = NA_QROWS * GRID_W
    ng = s // tq
    qb, kb, vb = _QB // 4, _KB // 4, _VB // 4

    def kv_spec(col, off):
        return pl.BlockSpec((1, tq, 4 * LANES), lambda i, b: (b, jnp.clip(i - 1, 0, ng - 3) + off, col))

    def variant(i):
        return jnp.where(i == 0, 0, jnp.where(i == ng - 1, 2, 1))

    return pl.pallas_call(
        _attn_b_kernel,
        out_shape=jax.ShapeDtypeStruct((bsz, s, 4 * LANES), BF16),
        grid=(ng, bsz),
        in_specs=[pl.BlockSpec((1, tq, 4 * LANES), lambda i, b: (b, i, qb)),
                  kv_spec(kb, 0), kv_spec(kb, 1), kv_spec(kb, 2),
                  kv_spec(vb, 0), kv_spec(vb, 1), kv_spec(vb, 2),
                  pl.BlockSpec((1, lc, 4 * LANES), lambda i, b: (b, 0, kb)),
                  pl.BlockSpec((1, lc, 4 * LANES), lambda i, b: (b, 0, vb)),
                  pl.BlockSpec((1,) + table.shape[1:], lambda i, b: (variant(i), 0, 0, 0))],
        out_specs=pl.BlockSpec((1, tq, 4 * LANES), lambda i, b: (b, i, 0)),
        compiler_params=_params("arbitrary", "arbitrary"),
        name="attn_b",
    )(zq, zq, zq, zq, zq, zq, zq, zc, zc, table)


def _na_table(rpb, rows):
    ng = rows // NA_QROWS
    nq, nk = NA_QROWS * GRID_W, NA_KROWS * GRID_W
    qc = np.arange(GRID_W)
    kc = np.arange(GRID_W)
    ws = np.clip(qc - NA_KW // 2, 0, GRID_W - NA_KW)
    valid_c = (kc[None, :] >= ws[:, None]) & (kc[None, :] < ws[:, None] + NA_KW)
    dc = np.clip(kc[None, :] - qc[:, None], -(NA_KW - 1), NA_KW - 1) + (NA_KW - 1)
    c_sel = (dc[..., None] == np.arange(2 * NA_KW - 1)) & valid_c[..., None]
    tiles = jnp.einsum('hab,uvb->hauv', rpb.astype(F32), jnp.asarray(c_sel, F32), precision=lax.Precision.HIGHEST)
    tiles = jnp.where(jnp.asarray(valid_c)[None, None], tiles * LOG2E, NEG_INF)
    masked = jnp.full((B_HEADS, GRID_W, GRID_W), NEG_INF, F32)
    tabs = []
    for i in (0, 1, ng - 1):
        start = int(np.clip(NA_QROWS * i - NA_KH // 2, 0, rows - NA_KROWS))
        q_rows = []
        for qr in range(NA_QROWS):
            r = NA_QROWS * i + qr
            rs = int(np.clip(r - NA_KH // 2, 0, rows - NA_KH))
            k_tiles = []
            for kr in range(NA_KROWS):
                krow = start + kr
                k_tiles.append(tiles[:, krow - r + (NA_KH - 1)] if rs <= krow < rs + NA_KH else masked)
            q_rows.append(jnp.concatenate(k_tiles, axis=-1))
        tabs.append(jnp.concatenate(q_rows, axis=-2))
    return jnp.stack(tabs).reshape(3, B_HEADS // 2, 2 * nq, nk)


def _merge_kernel(oa_ref, ob_ref, x_ref, mod_ref, ga_ref, gb_ref, wo_ref, n2_ref, wrh_ref, wrl_ref, rb_ref,
                  wsg_ref, wsu_ref, wsd_ref, xres_ref, hpa_ref, hpb_ref, idx_ref, gate_ref, *, d):
    b = pl.program_id(0)
    tm = x_ref.shape[1]

    def mod(k):
        return mod_ref[pl.ds(b, 1), k * d:(k + 1) * d]

    na = _rms(oa_ref[0].astype(F32), ga_ref[...])
    nb = _rms(ob_ref[0].astype(F32), gb_ref[...])
    cat = jnp.concatenate([na, nb], axis=1).astype(BF16)
    y = jnp.dot(cat, wo_ref[...], preferred_element_type=F32)
    x1 = x_ref[0] + mod(2) * y
    h2 = _rms(x1, n2_ref[...]) * (1.0 + mod(4)) + mod(3)

    hb = h2.astype(BF16)
    act = _silu(jnp.dot(hb, wsg_ref[...], preferred_element_type=F32)) * jnp.dot(hb, wsu_ref[...],
                                                                                 preferred_element_type=F32)
    shared = jnp.dot(act.astype(BF16), wsd_ref[...], preferred_element_type=F32)
    xres_ref[0] = x1 + mod(5) * shared
    packed = _pack_bf16_pairs(h2)
    hpa_ref[...] = packed[:, :d // 4]
    hpb_ref[...] = packed[:, d // 4:]

    h_hi, h_lo = _split_bf16(h2)
    logits = (lax.dot_general(wrh_ref[...], h_hi, _NT, preferred_element_type=F32)
              + lax.dot_general(wrh_ref[...], h_lo, _NT, preferred_element_type=F32)
              + lax.dot_general(wrl_ref[...], h_hi, _NT, preferred_element_type=F32))
    scores = jax.nn.sigmoid(logits)
    sel = scores + rb_ref[...]
    per = N_EXPERTS // N_GROUPS
    g3 = sel.reshape(N_GROUPS, per, tm)
    it3 = lax.broadcasted_iota(I32, (N_GROUPS, per, tm), 1)
    m1 = jnp.max(g3, axis=1, keepdims=True)
    first = jnp.min(jnp.where(g3 == m1, it3, per), axis=1, keepdims=True)
    m2 = jnp.max(jnp.where(it3 == first, -jnp.inf, g3), axis=1, keepdims=True)
    gscore = (m1 + m2).reshape(N_GROUPS, tm)

    itg = lax.broadcasted_iota(I32, (N_GROUPS, tm), 0)
    gsel = jnp.zeros((N_GROUPS, tm), F32)
    cur = gscore
    for _ in range(TOPK_GROUPS):
        mx = jnp.max(cur, axis=0, keepdims=True)
        fi = jnp.min(jnp.where(cur == mx, itg, N_GROUPS), axis=0, keepdims=True)
        pick = itg == fi
        gsel = jnp.where(pick, 1.0, gsel)
        cur = jnp.where(pick, -jnp.inf, cur)
    emask = jnp.broadcast_to(gsel.reshape(N_GROUPS, 1, tm), (N_GROUPS, per, tm)).reshape(N_EXPERTS, tm) > 0.5

    ite = lax.broadcasted_iota(I32, (N_EXPERTS, tm), 0)
    cur = jnp.where(emask, sel, NEG_INF)
    idx_rows, s_rows = [], []
    for _ in range(TOP_K):
        mx = jnp.max(cur, axis=0, keepdims=True)
        fi = jnp.min(jnp.where(cur == mx, ite, N_EXPERTS), axis=0, keepdims=True)
        pick = ite == fi
        idx_rows.append(fi)
        s_rows.append(jnp.sum(jnp.where(pick, scores, 0.0), axis=0, keepdims=True))
        cur = jnp.where(pick, -jnp.inf, cur)
    top_s = jnp.concatenate(s_rows, axis=0)
    idx_ref[...] = jnp.concatenate(idx_rows, axis=0)
    gate_ref[...] = top_s / jnp.sum(top_s, axis=0, keepdims=True) * ROUTED_SCALE


def _merge(o_a, o_b, x, mod, ga, gb, wo, n2, wr_hi, wr_lo, rb_col, wsg, wsu, wsd, *, tm):
    bsz, s, d = x.shape
    nt = s // tm
    n = bsz * s
    full = lambda a: pl.BlockSpec(a.shape, lambda b, i: (0,) * a.ndim)
    return pl.pallas_call(
        functools.partial(_merge_kernel, d=d),
        out_shape=(jax.ShapeDtypeStruct((bsz, s, d), F32),
                   jax.ShapeDtypeStruct((n, d // 4), U32),
                   jax.ShapeDtypeStruct((n, d // 4), U32),
                   jax.ShapeDtypeStruct((TOP_K, n), I32),
                   jax.ShapeDtypeStruct((TOP_K, n), F32)),
        grid=(bsz, nt),
        in_specs=[pl.BlockSpec((1, tm, d // 2), lambda b, i: (b, i, 0)),
                  pl.BlockSpec((1, tm, d // 2), lambda b, i: (b, i, 0)),
                  pl.BlockSpec((1, tm, d), lambda b, i: (b, i, 0)),
                  full(mod), full(ga), full(gb), full(wo), full(n2), full(wr_hi), full(wr_lo), full(rb_col),
                  full(wsg), full(wsu), full(wsd)],
        out_specs=(pl.BlockSpec((1, tm, d), lambda b, i: (b, i, 0)),
                   pl.BlockSpec((tm, d // 4), lambda b, i: (b * nt + i, 0)),
                   pl.BlockSpec((tm, d // 4), lambda b, i: (b * nt + i, 0)),
                   pl.BlockSpec((TOP_K, tm), lambda b, i: (0, b * nt + i)),
                   pl.BlockSpec((TOP_K, tm), lambda b, i: (0, b * nt + i))),
        compiler_params=_params("arbitrary", "arbitrary"),
        name="merge",
    )(o_a, o_b, x, mod, ga, gb, wo, n2, wr_hi, wr_lo, rb_col, wsg, wsu, wsd)


def _rank_kernel(idx_ref, rank_ref, cnt_ref, carry_ref):
    tm = idx_ref.shape[1]

    @pl.when(pl.program_id(0) == 0)
    def _():
        carry_ref[...] = jnp.zeros_like(carry_ref)

    idx = idx_ref[...]
    ite = lax.broadcasted_iota(I32, (N_EXPERTS, tm), 0)
    before = (lax.broadcasted_iota(I32, (tm, tm), 0) < lax.broadcasted_iota(I32, (tm, tm), 1)).astype(BF16)
    base = carry_ref[...]
    rows = []
    for k in range(TOP_K):
        oh = ite == idx[k:k + 1, :]
        ohf = jnp.where(oh, 1.0, 0.0)
        cum = jnp.dot(ohf.astype(BF16), before, preferred_element_type=F32)
        rows.append(jnp.sum(jnp.where(oh, cum + base, 0.0), axis=0, keepdims=True))
        base = base + jnp.sum(ohf, axis=1, keepdims=True)
    rank_ref[...] = jnp.concatenate(rows, axis=0).astype(I32)
    carry_ref[...] = base
    cnt_ref[...] = base


def _rank(idx, *, tm):
    n = idx.shape[1]
    return pl.pallas_call(
        _rank_kernel,
        out_shape=(jax.ShapeDtypeStruct((TOP_K, n), I32), jax.ShapeDtypeStruct((N_EXPERTS, 1), F32)),
        grid=(n // tm,),
        in_specs=[pl.BlockSpec((TOP_K, tm), lambda i: (0, i))],
        out_specs=(pl.BlockSpec((TOP_K, tm), lambda i: (0, i)), pl.BlockSpec((N_EXPERTS, 1), lambda i: (0, 0))),
        scratch_shapes=[pltpu.VMEM((N_EXPERTS, 1), F32)],
        compiler_params=_params("arbitrary"),
        name="rank",
    )(idx)


def _pos_kernel(idx_ref, rank_ref, pstart_ref, pos_ref):
    tm = idx_ref.shape[1]
    idx = idx_ref[...]
    ite = lax.broadcasted_iota(I32, (N_EXPERTS, tm), 0)
    pstart = pstart_ref[...]
    rows = [jnp.sum(jnp.where(ite == idx[k:k + 1, :], pstart, 0.0), axis=0, keepdims=True) for k in range(TOP_K)]
    pos_ref[...] = jnp.concatenate(rows, axis=0).astype(I32) + rank_ref[...]


def _pos(idx, rank, pstart_col, *, tm):
    n = idx.shape[1]
    return pl.pallas_call(
        _pos_kernel,
        out_shape=jax.ShapeDtypeStruct((TOP_K, n), I32),
        grid=(n // tm,),
        in_specs=[pl.BlockSpec((TOP_K, tm), lambda i: (0, i)),
                  pl.BlockSpec((TOP_K, tm), lambda i: (0, i)),
                  pl.BlockSpec((N_EXPERTS, 1), lambda i: (0, 0))],
        out_specs=pl.BlockSpec((TOP_K, tm), lambda i: (0, i)),
        compiler_params=_params("arbitrary"),
        name="pos",
    )(idx, rank, pstart_col)


RING_SLOTS = 4


def _expert_kernel(ps_ref, cnt_ref, tot_ref, wg_ref, wu_ref, wd_ref, xa_hbm, xb_hbm, ya_hbm, yb_hbm,
                   wg_s, wu_s, wd_s, xa_buf, xb_buf, ya_buf, yb_buf, in_sem, out_sem):
    e = pl.program_id(0)
    r = EXPERT_ROWS
    ahead = RING_SLOTS - 1
    total = tot_ref[0]
    first = ps_ref[e] // r
    cnt = cnt_ref[e]
    nch = (cnt + r - 1) // r

    def rows_of(g):
        return pl.ds(pl.multiple_of(g * r, r), r)

    def fetch(g):
        slot = g % RING_SLOTS
        return (pltpu.make_async_copy(xa_hbm.at[rows_of(g)], xa_buf.at[slot], in_sem.at[0, slot]),
                pltpu.make_async_copy(xb_hbm.at[rows_of(g)], xb_buf.at[slot], in_sem.at[1, slot]))

    def flush(g):
        slot = g % RING_SLOTS
        return (pltpu.make_async_copy(ya_buf.at[slot], ya_hbm.at[rows_of(g)], out_sem.at[0, slot]),
                pltpu.make_async_copy(yb_buf.at[slot], yb_hbm.at[rows_of(g)], out_sem.at[1, slot]))

    def start(copies):
        for cp in copies:
            cp.start()

    def wait(copies):
        for cp in copies:
            cp.wait()

    @pl.when(e == 0)
    def _():
        for g in range(ahead):
            @pl.when(g < total)
            def _():
                start(fetch(g))

    @pl.when(nch > 0)
    def _():
        wg_s[...] = wg_ref[0].astype(BF16)
        wu_s[...] = wu_ref[0].astype(BF16)
        wd_s[...] = wd_ref[0].astype(BF16)

        def chunk(c, carry):
            g = first + c
            slot = g % RING_SLOTS
            wait(fetch(g))

            @pl.when(g + ahead < total)
            def _():
                start(fetch(g + ahead))

            @pl.when(g >= RING_SLOTS)
            def _():
                wait(flush(g - RING_SLOTS))

            packed = jnp.concatenate([xa_buf[slot], xb_buf[slot]], axis=1)
            row = lax.broadcasted_iota(I32, packed.shape, 0)
            packed = jnp.where(row < cnt - c * r, packed, jnp.zeros_like(packed))
            x = _unpack_bf16_pairs(packed).astype(BF16)
            gate = jnp.dot(x, wg_s[...], preferred_element_type=F32)
            up = jnp.dot(x, wu_s[...], preferred_element_type=F32)
            y = jnp.dot((_silu(gate) * up).astype(BF16), wd_s[...], preferred_element_type=F32)
            out = _pack_bf16_pairs(y)
            half = out.shape[1] // 2
            ya_buf[slot] = out[:, :half]
            yb_buf[slot] = out[:, half:]
            start(flush(g))
            return carry

        lax.fori_loop(0, nch, chunk, 0)

    @pl.when(e == pl.num_programs(0) - 1)
    def _():
        for k in range(RING_SLOTS):
            @pl.when(total - 1 - k >= 0)
            def _():
                wait(flush(total - 1 - k))


def _experts(pstart, cnt, total_chunks, xs_a, xs_b, wg, wu, wd):
    rows, hw = xs_a.shape
    r = EXPERT_ROWS
    n_exp, d, f = wg.shape
    hbm = pl.BlockSpec(memory_space=pl.ANY)
    grid_spec = pltpu.PrefetchScalarGridSpec(
        num_scalar_prefetch=3,
        grid=(n_exp,),
        in_specs=[pl.BlockSpec((1, d, f), lambda e, ps, cn, tot: (e, 0, 0)),
                  pl.BlockSpec((1, d, f), lambda e, ps, cn, tot: (e, 0, 0)),
                  pl.BlockSpec((1, f, d), lambda e, ps, cn, tot: (e, 0, 0)),
                  hbm, hbm],
        out_specs=(hbm, hbm),
        scratch_shapes=[pltpu.VMEM((d, f), BF16), pltpu.VMEM((d, f), BF16), pltpu.VMEM((f, d), BF16),
                        pltpu.VMEM((RING_SLOTS, r, hw), U32), pltpu.VMEM((RING_SLOTS, r, hw), U32),
                        pltpu.VMEM((RING_SLOTS, r, hw), U32), pltpu.VMEM((RING_SLOTS, r, hw), U32),
                        pltpu.SemaphoreType.DMA((2, RING_SLOTS)), pltpu.SemaphoreType.DMA((2, RING_SLOTS))],
    )
    return pl.pallas_call(
        _expert_kernel,
        out_shape=(jax.ShapeDtypeStruct((rows, hw), U32), jax.ShapeDtypeStruct((rows, hw), U32)),
        grid_spec=grid_spec,
        compiler_params=_params("arbitrary"),
        name="experts",
    )(pstart, cnt, total_chunks, wg, wu, wd, xs_a, xs_b)


def _sc_scatter_rows(x, idx_flat, n_rows):
    n, w = x.shape
    m = idx_flat.shape[1]
    nwin = n // SC_WINDOW
    mesh = plsc.VectorSubcoreMesh(core_axis_name="core", subcore_axis_name="subcore")

    @functools.partial(pl.kernel, out_type=jax.ShapeDtypeStruct((n_rows, w), x.dtype), mesh=mesh,
                       scratch_types=[])
    def scatter_kernel(x_hbm, i_hbm, o_hbm):
        def body(x_vmem, i_vmem):
            pltpu.sync_copy(x_vmem, o_hbm.at[i_vmem.at[0]])

        pltpu.emit_pipeline(
            body,
            grid=(m // SC_WINDOW,),
            in_specs=[pl.BlockSpec((SC_WINDOW, w), lambda i: (i % nwin, 0)),
                      pl.BlockSpec((1, SC_WINDOW), lambda i: (0, i))],
            out_specs=[],
            core_axis_name=("core", "subcore"),
            dimension_semantics=(pltpu.PARALLEL,),
        )(x_hbm, i_hbm)

    return scatter_kernel(x, idx_flat)


def _sc_gather_rows(src, idx_flat):
    m = idx_flat.shape[1]
    w = src.shape[1]
    mesh = plsc.VectorSubcoreMesh(core_axis_name="core", subcore_axis_name="subcore")

    @functools.partial(pl.kernel, out_type=jax.ShapeDtypeStruct((m, w), src.dtype), mesh=mesh)
    def gather_kernel(x_hbm, i_hbm, o_hbm):
        def body(i_vmem, o_vmem):
            pltpu.sync_copy(x_hbm.at[i_vmem.at[0]], o_vmem)

        pltpu.emit_pipeline(
            body,
            grid=(m // SC_WINDOW,),
            in_specs=[pl.BlockSpec((1, SC_WINDOW), lambda i: (0, i))],
            out_specs=[pl.BlockSpec((SC_WINDOW, w), lambda i: (i, 0))],
            core_axis_name=("core", "subcore"),
            dimension_semantics=(pltpu.PARALLEL,),
        )(i_hbm, o_hbm)

    return gather_kernel(src, idx_flat)


def _finish_kernel(gate_ref, xres_ref, mod_ref, ya_ref, yb_ref, o_ref, *, d, tiles_per_batch):
    tm = gate_ref.shape[1]
    b = pl.program_id(0) // tiles_per_batch
    gates = gate_ref[...].T
    acc = jnp.zeros((tm, d), F32)
    for k in range(TOP_K):
        packed = jnp.concatenate([ya_ref[k], yb_ref[k]], axis=1)
        acc = acc + _unpack_bf16_pairs(packed) * gates[:, k:k + 1]
    o_ref[...] = xres_ref[...] + mod_ref[pl.ds(b, 1), 5 * d:6 * d] * acc


def _finish(gates, xres, mod, yg_a, yg_b, *, tm, tiles_per_batch):
    n, d = xres.shape
    return pl.pallas_call(
        functools.partial(_finish_kernel, d=d, tiles_per_batch=tiles_per_batch),
        out_shape=jax.ShapeDtypeStruct((n, d), F32),
        grid=(n // tm,),
        in_specs=[pl.BlockSpec((TOP_K, tm), lambda i: (0, i)),
                  pl.BlockSpec((tm, d), lambda i: (i, 0)),
                  pl.BlockSpec(mod.shape, lambda i: (0, 0)),
                  pl.BlockSpec((TOP_K, tm, d // 4), lambda i: (0, i, 0)),
                  pl.BlockSpec((TOP_K, tm, d // 4), lambda i: (0, i, 0))],
        out_specs=pl.BlockSpec((tm, d), lambda i: (i, 0)),
        compiler_params=_params("arbitrary"),
        name="finish",
    )(gates, xres, mod, yg_a, yg_b)


_QA_HEAD_ORDER = (0, 4, 1, 5, 2, 6, 3, 7)


def _reorder_qa_heads(a, axis):
    return jnp.concatenate([lax.slice_in_dim(a, h * HEAD_DIM, (h + 1) * HEAD_DIM, axis=axis)
                            for h in _QA_HEAD_ORDER], axis=axis)


def _rope_swap(a):
    quarter = HEAD_DIM // 4
    lead = a.shape[:-1]
    return jnp.flip(a.reshape(lead + (-1, 2, quarter)), axis=-2).reshape(a.shape)


def _rope_tables(s):
    quarter = HEAD_DIM // 4
    t = jnp.arange(s)
    row = (t // GRID_W).astype(F32)
    col = (t % GRID_W).astype(F32)
    freqs = ROPE_BASE ** (-jnp.arange(quarter, dtype=F32) / quarter)
    ar = row[:, None] * freqs[None, :]
    ac = col[:, None] * freqs[None, :]
    cos = jnp.concatenate([jnp.cos(ar), jnp.cos(ar), jnp.cos(ac), jnp.cos(ac)], axis=1)
    sin = jnp.concatenate([-jnp.sin(ar), jnp.sin(ar), -jnp.sin(ac), jnp.sin(ac)], axis=1)
    return jnp.tile(cos, (1, 2)), jnp.tile(sin, (1, 2))


def kernel(x, c, ctx, c_ctx, w_ada, b_ada, norm1, norm2, w_in, q_norm_a, k_norm_a, q_norm_b, k_norm_b, sink_a,
           rpb_b, out_norm_a, out_norm_b, w_out, w_router, router_bias, we_gate, we_up, we_down, ws_gate, ws_up,
           ws_down):
    assert w_ada.shape[0] == 1, "single-layer block"
    bsz, s, d = x.shape
    lc = ctx.shape[1]
    n = bsz * s
    rows = s // GRID_W
    assert s % (NA_QROWS * GRID_W) == 0 and rows >= NA_KROWS and bsz <= 4 and d == 1024

    c8 = jnp.concatenate([c, c_ctx[None, :], jnp.zeros((8 - bsz - 1, d), F32)], axis=0)
    mod = _ada(c8, w_ada[0], b_ada[0][None, :])

    w = w_in[0]
    aw, akw, bw = A_Q_HEADS * HEAD_DIM, A_KV_HEADS * HEAD_DIM, B_HEADS * HEAD_DIM
    cuts = np.cumsum([0, aw, akw, akw, bw, bw, bw])
    qa_w, ka_w, va_w, qb_w, kb_w, vb_w = [w[:, cuts[i]:cuts[i + 1]] for i in range(6)]
    qa_w = _reorder_qa_heads(qa_w, 1)
    w_ext = jnp.concatenate([qa_w, qb_w, kb_w, vb_w, ka_w, va_w, _rope_swap(qa_w), _rope_swap(ka_w)],
                            axis=1).astype(BF16)
    scale = HEAD_DIM ** -0.5 * LOG2E
    pair = lambda g: jnp.tile(g, 2)
    gains = jnp.stack([pair(q_norm_a[0]) * scale, pair(q_norm_b[0]) * scale, pair(k_norm_b[0]), pair(k_norm_a[0]),
                       pair(_rope_swap(q_norm_a[0])) * scale, pair(_rope_swap(k_norm_a[0])),
                       jnp.zeros((LANES,), F32), jnp.zeros((LANES,), F32)])
    bd = jnp.asarray(np.kron(np.eye(2), np.full((HEAD_DIM, HEAD_DIM), 1.0 / HEAD_DIM)), BF16)
    cos_t, sin_t = _rope_tables(s)
    n1 = norm1[0][None, :]
    zq = _inproj(x, mod, n1, w_ext, gains, cos_t, sin_t, bd, mod_row=None, tm=512)
    zc = _inproj(ctx, mod, n1, w_ext, gains, jnp.ones((lc, LANES), F32), jnp.zeros((lc, LANES), F32), bd,
                 mod_row=bsz, tm=lc)

    sink = sink_a[0].astype(F32)
    sink_rep = jnp.concatenate([jnp.broadcast_to(sink[h] * LOG2E, (A_WINDOW, LANES)) for h in _QA_HEAD_ORDER])
    o_a = _attn_a(zq, zc, sink_rep, _attn_a_mask(A_WINDOW))
    o_b = _attn_b(zq, zc, _na_table(rpb_b[0], rows))

    ga = _reorder_qa_heads(out_norm_a[0], 0)[None, :]
    gb = out_norm_b[0][None, :]
    wo = jnp.concatenate([_reorder_qa_heads(w_out[0][:aw], 0), w_out[0][aw:]], axis=0).astype(BF16)
    wr_hi, wr_lo = _split_bf16(w_router[0].T)
    xres, hp_a, hp_b, idx, gates = _merge(o_a, o_b, x, mod, ga, gb, wo, norm2[0][None, :], wr_hi, wr_lo,
                                  router_bias[0][:, None], ws_gate[0].astype(BF16), ws_up[0].astype(BF16),
                                  ws_down[0].astype(BF16), tm=512)

    rank, counts = _rank(idx, tm=512)
    t = EXPERT_ROWS
    cnt = counts[:, 0].astype(I32)
    padded = (cnt + t - 1) // t * t
    pends = jnp.cumsum(padded)
    pstart = pends - padded
    n_rows = n * TOP_K + N_EXPERTS * t
    pos = _pos(idx, rank, pstart.astype(F32)[:, None], tm=512)

    pos_flat = pos.reshape(1, TOP_K * n)
    xs_a = _sc_scatter_rows(hp_a, pos_flat, n_rows)
    xs_b = _sc_scatter_rows(hp_b, pos_flat, n_rows)
    ys_a, ys_b = _experts(pstart, cnt, pends[-1:] // t, xs_a, xs_b, we_gate[0], we_up[0], we_down[0])
    yg_a = _sc_gather_rows(ys_a, pos_flat).reshape(TOP_K, n, d // 4)
    yg_b = _sc_gather_rows(ys_b, pos_flat).reshape(TOP_K, n, d // 4)
    out = _finish(gates, xres.reshape(n, d), mod, yg_a, yg_b, tm=256, tiles_per_batch=s // 256)
    return out.reshape(bsz, s, d)
```

```python
import functools

import numpy as np
import jax
import jax.numpy as jnp
from jax import lax
from jax.experimental import pallas as pl
from jax.experimental.pallas import tpu as pltpu
from jax.experimental.pallas import tpu_sc as plsc

F32 = jnp.float32
BF16 = jnp.bfloat16
I32 = jnp.int32
U32 = jnp.uint32

LANES = 128
HEAD_DIM = 64
HEAD_PAIR = 2 * HEAD_DIM
GRID_W = 64
A_Q_HEADS = 8
A_KV_HEADS = 2
A_WINDOW = 128
B_HEADS = 8
NA_KH = 8
NA_KW = 16
NA_QROWS = 4
NA_KROWS = NA_QROWS + NA_KH
ROPE_BASE = 10000.0
N_EXPERTS = 256
TOP_K = 8
N_GROUPS = 8
TOPK_GROUPS = 4
ROUTED_SCALE = 2.5
LOG2E = 1.4426950408889634
EPS = 1e-6
NEG_INF = -1e30
EXPERT_ROWS = 256
SC_WINDOW = 128
VMEM_LIMIT = 56 * 1024 * 1024

_NT = (((1,), (1,)), ((), ()))


def _params(*sem):
    return pltpu.CompilerParams(dimension_semantics=sem, vmem_limit_bytes=VMEM_LIMIT)


def _silu(v):
    return v * jax.nn.sigmoid(v)


def _rms(v, gain):
    return v * lax.rsqrt(jnp.mean(v * v, axis=-1, keepdims=True) + EPS) * gain


def _pack_bf16_pairs(v):
    n = v.shape[1] // 2
    lo = lax.bitcast_convert_type(v[:, :n].astype(BF16).astype(F32), U32) >> 16
    hi = lax.bitcast_convert_type(v[:, n:].astype(BF16).astype(F32), U32) & jnp.uint32(0xFFFF0000)
    return hi | lo


def _split_bf16(v):
    hi = lax.bitcast_convert_type(lax.bitcast_convert_type(v, U32) & jnp.uint32(0xFFFF0000), F32)
    return hi.astype(BF16), (v - hi).astype(BF16)


def _unpack_bf16_pairs(w):
    lo = lax.bitcast_convert_type(w << 16, F32)
    hi = lax.bitcast_convert_type(w & jnp.uint32(0xFFFF0000), F32)
    return jnp.concatenate([lo, hi], axis=1)


def _ada_kernel(c_ref, w_ref, b_ref, o_ref):
    a = _silu(c_ref[...])
    o_ref[...] = jnp.dot(a, w_ref[...], preferred_element_type=F32,
                         precision=lax.Precision.HIGHEST) + b_ref[...]


def _ada(c8, w, b):
    d, n = w.shape
    bn = n // 4
    return pl.pallas_call(
        _ada_kernel,
        out_shape=jax.ShapeDtypeStruct((8, n), F32),
        grid=(n // bn,),
        in_specs=[pl.BlockSpec((8, d), lambda j: (0, 0)),
                  pl.BlockSpec((d, bn), lambda j: (0, j)),
                  pl.BlockSpec((1, bn), lambda j: (0, j))],
        out_specs=pl.BlockSpec((8, bn), lambda j: (0, j)),
        compiler_params=_params("arbitrary"),
        name="ada",
    )(c8, w, b)


_QA, _QB, _KB, _VB, _KA, _VA, _QA_SW, _KA_SW = 0, 4, 8, 12, 16, 17, 18, 22
_OUT_BLOCKS = 18
_EXT_BLOCKS = 23


def _inproj_kernel(x_ref, mod_ref, n1_ref, w_ref, g_ref, cos_ref, sin_ref, bd_ref, o_ref, *, mod_row, d):
    b = pl.program_id(0) if mod_row is None else mod_row
    xn = _rms(x_ref[0], n1_ref[...])
    sh = mod_ref[pl.ds(b, 1), 0:d]
    sc = mod_ref[pl.ds(b, 1), d:2 * d]
    h = (xn * (1.0 + sc) + sh).astype(BF16)
    z = jnp.dot(h, w_ref[...], preferred_element_type=F32)
    bd = bd_ref[...]
    cos = cos_ref[...]
    sin = sin_ref[...]

    def blk(j):
        return z[:, j * LANES:(j + 1) * LANES]

    def head_rinv(zb):
        ms = jnp.dot((zb * zb).astype(BF16), bd, preferred_element_type=F32)
        return lax.rsqrt(ms + EPS)

    def put(j, v):
        o_ref[0, :, j * LANES:(j + 1) * LANES] = v.astype(BF16)

    def roped(j, j_sw, g_row, g_sw_row):
        zb = blk(j)
        r = head_rinv(zb)
        put(j, (zb * r * g_ref[g_row:g_row + 1, :]) * cos + (blk(j_sw) * r * g_ref[g_sw_row:g_sw_row + 1, :]) * sin)

    def normed(j, g_row):
        zb = blk(j)
        put(j, zb * head_rinv(zb) * g_ref[g_row:g_row + 1, :])

    for j in range(4):
        roped(_QA + j, _QA_SW + j, 0, 4)
        normed(_QB + j, 1)
        normed(_KB + j, 2)
        put(_VB + j, blk(_VB + j))
    roped(_KA, _KA_SW, 3, 5)
    put(_VA, blk(_VA))


def _inproj(x, mod, n1, w_ext, gains, cos_t, sin_t, bd, *, mod_row, tm):
    bsz, s, d = x.shape
    kern = functools.partial(_inproj_kernel, mod_row=mod_row, d=d)
    return pl.pallas_call(
        kern,
        out_shape=jax.ShapeDtypeStruct((bsz, s, _OUT_BLOCKS * LANES), BF16),
        grid=(bsz, s // tm),
        in_specs=[pl.BlockSpec((1, tm, d), lambda b, i: (b, i, 0)),
                  pl.BlockSpec(mod.shape, lambda b, i: (0, 0)),
                  pl.BlockSpec((1, d), lambda b, i: (0, 0)),
                  pl.BlockSpec(w_ext.shape, lambda b, i: (0, 0)),
                  pl.BlockSpec(gains.shape, lambda b, i: (0, 0)),
                  pl.BlockSpec((tm, LANES), lambda b, i: (i, 0)),
                  pl.BlockSpec((tm, LANES), lambda b, i: (i, 0)),
                  pl.BlockSpec((LANES, LANES), lambda b, i: (0, 0))],
        out_specs=pl.BlockSpec((1, tm, _OUT_BLOCKS * LANES), lambda b, i: (b, i, 0)),
        compiler_params=_params("arbitrary", "arbitrary"),
        name="inproj",
    )(x, mod, n1, w_ext, gains, cos_t, sin_t, bd)


def _split_pair(qp, lo):
    zero = jnp.zeros_like(qp)
    return jnp.concatenate([jnp.where(lo, qp, zero), jnp.where(lo, zero, qp)], axis=0)


def _softmax_pv(s_parts, v_parts, sink_rep):
    chunks = [s[:, c * LANES:(c + 1) * LANES] for s in s_parts for c in range(s.shape[1] // LANES)]
    m = jnp.max(functools.reduce(jnp.maximum, chunks), axis=-1, keepdims=True)
    m_rep = jnp.broadcast_to(m, (m.shape[0], LANES))
    if sink_rep is not None:
        m_rep = jnp.maximum(m_rep, sink_rep)
    acc = None
    for s, v in zip(s_parts, v_parts):
        p = jnp.concatenate([jnp.exp2(s[:, c * LANES:(c + 1) * LANES] - m_rep)
                             for c in range(s.shape[1] // LANES)], axis=1).astype(BF16)
        v_ext = jnp.concatenate([v, jnp.ones_like(v)], axis=1)
        o = jnp.dot(p, v_ext, preferred_element_type=F32)
        acc = o if acc is None else acc + o
    l_rep = acc[:, LANES:]
    if sink_rep is not None:
        l_rep = l_rep + jnp.exp2(sink_rep - m_rep)
    return acc[:, :LANES] * (1.0 / l_rep)


def _attn_a_kernel(q_ref, kp_ref, kc_ref, kn_ref, vp_ref, vc_ref, vn_ref, kx_ref, vx_ref, sink_ref, mask_ref,
                   o_ref):
    tq = q_ref.shape[1]
    lo = lax.broadcasted_iota(I32, (tq, LANES), 1) < HEAD_DIM
    q = q_ref[0]
    qs = jnp.concatenate([_split_pair(q[:, p * LANES:(p + 1) * LANES], lo) for p in range(4)], axis=0)
    k = jnp.concatenate([kp_ref[0], kc_ref[0], kn_ref[0]], axis=0)
    v = jnp.concatenate([vp_ref[0], vc_ref[0], vn_ref[0]], axis=0)
    s_loc = lax.dot_general(qs, k, _NT, preferred_element_type=F32) + mask_ref[0]
    s_ctx = lax.dot_general(qs, kx_ref[0], _NT, preferred_element_type=F32)
    o = _softmax_pv([s_loc, s_ctx], [v, vx_ref[0]], sink_ref[...])
    for p in range(4):
        o_lo = o[(2 * p) * tq:(2 * p + 1) * tq]
        o_hi = o[(2 * p + 1) * tq:(2 * p + 2) * tq]
        o_ref[0, :, p * LANES:(p + 1) * LANES] = jnp.where(lo, o_lo, o_hi).astype(BF16)


def _attn_a(zq, zc, sink_rep, mask):
    bsz, s, _ = zq.shape
    lc = zc.shape[1]
    tq = A_WINDOW
    nblk = s // tq
    ka, va = _KA, _VA

    def kv_spec(col, shift):
        return pl.BlockSpec((1, tq, LANES), lambda b, j: (b, jnp.clip(j + shift, 0, nblk - 1), col))

    def variant(j):
        return jnp.where(j == 0, 0, jnp.where(j == nblk - 1, 2, 1))

    return pl.pallas_call(
        _attn_a_kernel,
        out_shape=jax.ShapeDtypeStruct((bsz, s, 4 * LANES), BF16),
        grid=(bsz, nblk),
        in_specs=[pl.BlockSpec((1, tq, 4 * LANES), lambda b, j: (b, j, 0)),
                  kv_spec(ka, -1), kv_spec(ka, 0), kv_spec(ka, 1),
                  kv_spec(va, -1), kv_spec(va, 0), kv_spec(va, 1),
                  pl.BlockSpec((1, lc, LANES), lambda b, j: (b, 0, ka)),
                  pl.BlockSpec((1, lc, LANES), lambda b, j: (b, 0, va)),
                  pl.BlockSpec(sink_rep.shape, lambda b, j: (0, 0)),
                  pl.BlockSpec((1,) + mask.shape[1:], lambda b, j: (variant(j), 0, 0))],
        out_specs=pl.BlockSpec((1, tq, 4 * LANES), lambda b, j: (b, j, 0)),
        compiler_params=_params("arbitrary", "arbitrary"),
        name="attn_a",
    )(zq, zq, zq, zq, zq, zq, zq, zc, zc, sink_rep, mask)


def _attn_a_mask(tq):
    qi = np.arange(tq)[:, None]
    kj = np.arange(3 * tq)[None, :]
    ok = (kj >= qi) & (kj <= qi + 2 * tq)
    variants = [ok & (kj >= tq), ok, ok & (kj < 2 * tq)]
    m = np.stack([np.tile(np.where(v, 0.0, NEG_INF).astype(np.float32), (A_Q_HEADS, 1)) for v in variants])
    return jnp.asarray(m)


def _attn_b_kernel(q_ref, k0_ref, k1_ref, k2_ref, v0_ref, v1_ref, v2_ref, kx_ref, vx_ref, tab_ref, o_ref):
    tq = q_ref.shape[1]
    lo = lax.broadcasted_iota(I32, (tq, LANES), 1) < HEAD_DIM
    for p in range(4):
        sl = slice(p * LANES, (p + 1) * LANES)
        qs = _split_pair(q_ref[0, :, sl], lo)
        k = jnp.concatenate([k0_ref[0, :, sl], k1_ref[0, :, sl], k2_ref[0, :, sl]], axis=0)
        v = jnp.concatenate([v0_ref[0, :, sl], v1_ref[0, :, sl], v2_ref[0, :, sl]], axis=0)
        s_loc = lax.dot_general(qs, k, _NT, preferred_element_type=F32) + tab_ref[0, p]
        s_ctx = lax.dot_general(qs, kx_ref[0, :, sl], _NT, preferred_element_type=F32)
        o = _softmax_pv([s_loc, s_ctx], [v, vx_ref[0, :, sl]], None)
        o_ref[0, :, sl] = jnp.where(lo, o[:tq], o[tq:]).astype(BF16)


def _attn_b(zq, zc, table):
    bsz, s, _ = zq.shape
    lc = zc.shape[1]
    tq = NA_QROWS * GRID_W
    ng = s // tq
    qb, kb, vb = _QB // 4, _KB // 4, _VB // 4

    def kv_spec(col, off):
        return pl.BlockSpec((1, tq, 4 * LANES), lambda i, b: (b, jnp.clip(i - 1, 0, ng - 3) + off, col))

    def variant(i):
        return jnp.where(i == 0, 0, jnp.where(i == ng - 1, 2, 1))

    return pl.pallas_call(
        _attn_b_kernel,
        out_shape=jax.ShapeDtypeStruct((bsz, s, 4 * LANES), BF16),
        grid=(ng, bsz),
        in_specs=[pl.BlockSpec((1, tq, 4 * LANES), lambda i, b: (b, i, qb)),
                  kv_spec(kb, 0), kv_spec(kb, 1), kv_spec(kb, 2),
                  kv_spec(vb, 0), kv_spec(vb, 1), kv_spec(vb, 2),
                  pl.BlockSpec((1, lc, 4 * LANES), lambda i, b: (b, 0, kb)),
                  pl.BlockSpec((1, lc, 4 * LANES), lambda i, b: (b, 0, vb)),
                  pl.BlockSpec((1,) + table.shape[1:], lambda i, b: (variant(i), 0, 0, 0))],
        out_specs=pl.BlockSpec((1, tq, 4 * LANES), lambda i, b: (b, i, 0)),
        compiler_params=_params("arbitrary", "arbitrary"),
        name="attn_b",
    )(zq, zq, zq, zq, zq, zq, zq, zc, zc, table)


def _na_table(rpb, rows):
    ng = rows // NA_QROWS
    nq, nk = NA_QROWS * GRID_W, NA_KROWS * GRID_W
    qc = np.arange(GRID_W)
    kc = np.arange(GRID_W)
    ws = np.clip(qc - NA_KW // 2, 0, GRID_W - NA_KW)
    valid_c = (kc[None, :] >= ws[:, None]) & (kc[None, :] < ws[:, None] + NA_KW)
    dc = np.clip(kc[None, :] - qc[:, None], -(NA_KW - 1), NA_KW - 1) + (NA_KW - 1)
    c_sel = (dc[..., None] == np.arange(2 * NA_KW - 1)) & valid_c[..., None]
    tiles = jnp.einsum('hab,uvb->hauv', rpb.astype(F32), jnp.asarray(c_sel, F32), precision=lax.Precision.HIGHEST)
    tiles = jnp.where(jnp.asarray(valid_c)[None, None], tiles * LOG2E, NEG_INF)
    masked = jnp.full((B_HEADS, GRID_W, GRID_W), NEG_INF, F32)
    tabs = []
    for i in (0, 1, ng - 1):
        start = int(np.clip(NA_QROWS * i - NA_KH // 2, 0, rows - NA_KROWS))
        q_rows = []
        for qr in range(NA_QROWS):
            r = NA_QROWS * i + qr
            rs = int(np.clip(r - NA_KH // 2, 0, rows - NA_KH))
            k_tiles = []
            for kr in range(NA_KROWS):
                krow = start + kr
                k_tiles.append(tiles[:, krow - r + (NA_KH - 1)] if rs <= krow < rs + NA_KH else masked)
            q_rows.append(jnp.concatenate(k_tiles, axis=-1))
        tabs.append(jnp.concatenate(q_rows, axis=-2))
    return jnp.stack(tabs).reshape(3, B_HEADS // 2, 2 * nq, nk)


def _merge_kernel(oa_ref, ob_ref, x_ref, mod_ref, ga_ref, gb_ref, wo_ref, n2_ref, wrh_ref, wrl_ref, rb_ref,
                  wsg_ref, wsu_ref, wsd_ref, xres_ref, hpa_ref, hpb_ref, idx_ref, gate_ref, *, d, b0):
    b = pl.program_id(0) + b0
    tm = x_ref.shape[1]

    def mod(k):
        return mod_ref[pl.ds(b, 1), k * d:(k + 1) * d]

    na = _rms(oa_ref[0].astype(F32), ga_ref[...])
    nb = _rms(ob_ref[0].astype(F32), gb_ref[...])
    cat = jnp.concatenate([na, nb], axis=1).astype(BF16)
    y = jnp.dot(cat, wo_ref[...], preferred_element_type=F32)
    x1 = x_ref[0] + mod(2) * y
    h2 = _rms(x1, n2_ref[...]) * (1.0 + mod(4)) + mod(3)

    hb = h2.astype(BF16)
    act = _silu(jnp.dot(hb, wsg_ref[...], preferred_element_type=F32)) * jnp.dot(hb, wsu_ref[...],
                                                                                 preferred_element_type=F32)
    shared = jnp.dot(act.astype(BF16), wsd_ref[...], preferred_element_type=F32)
    xres_ref[0] = x1 + mod(5) * shared
    packed = _pack_bf16_pairs(h2)
    hpa_ref[...] = packed[:, :d // 4]
    hpb_ref[...] = packed[:, d // 4:]

    h_hi, h_lo = _split_bf16(h2)
    logits = (lax.dot_general(wrh_ref[...], h_hi, _NT, preferred_element_type=F32)
              + lax.dot_general(wrh_ref[...], h_lo, _NT, preferred_element_type=F32)
              + lax.dot_general(wrl_ref[...], h_hi, _NT, preferred_element_type=F32))
    scores = jax.nn.sigmoid(logits)
    sel = scores + rb_ref[...]
    per = N_EXPERTS // N_GROUPS
    g3 = sel.reshape(N_GROUPS, per, tm)
    it3 = lax.broadcasted_iota(I32, (N_GROUPS, per, tm), 1)
    m1 = jnp.max(g3, axis=1, keepdims=True)
    first = jnp.min(jnp.where(g3 == m1, it3, per), axis=1, keepdims=True)
    m2 = jnp.max(jnp.where(it3 == first, -jnp.inf, g3), axis=1, keepdims=True)
    gscore = (m1 + m2).reshape(N_GROUPS, tm)

    itg = lax.broadcasted_iota(I32, (N_GROUPS, tm), 0)
    gsel = jnp.zeros((N_GROUPS, tm), F32)
    cur = gscore
    for _ in range(TOPK_GROUPS):
        mx = jnp.max(cur, axis=0, keepdims=True)
        fi = jnp.min(jnp.where(cur == mx, itg, N_GROUPS), axis=0, keepdims=True)
        pick = itg == fi
        gsel = jnp.where(pick, 1.0, gsel)
        cur = jnp.where(pick, -jnp.inf, cur)
    emask = jnp.broadcast_to(gsel.reshape(N_GROUPS, 1, tm), (N_GROUPS, per, tm)).reshape(N_EXPERTS, tm) > 0.5

    ite = lax.broadcasted_iota(I32, (N_EXPERTS, tm), 0)
    cur = jnp.where(emask, sel, NEG_INF)
    idx_rows, s_rows = [], []
    for _ in range(TOP_K):
        mx = jnp.max(cur, axis=0, keepdims=True)
        fi = jnp.min(jnp.where(cur == mx, ite, N_EXPERTS), axis=0, keepdims=True)
        pick = ite == fi
        idx_rows.append(fi)
        s_rows.append(jnp.sum(jnp.where(pick, scores, 0.0), axis=0, keepdims=True))
        cur = jnp.where(pick, -jnp.inf, cur)
    top_s = jnp.concatenate(s_rows, axis=0)
    idx_ref[...] = jnp.concatenate(idx_rows, axis=0)
    gate_ref[...] = top_s / jnp.sum(top_s, axis=0, keepdims=True) * ROUTED_SCALE


def _merge(o_a, o_b, x, mod, ga, gb, wo, n2, wr_hi, wr_lo, rb_col, wsg, wsu, wsd, *, tm, b0, nb):
    _, s, d = x.shape
    nt = s // tm
    n = nb * s
    full = lambda a: pl.BlockSpec(a.shape, lambda b, i: (0,) * a.ndim)
    return pl.pallas_call(
        functools.partial(_merge_kernel, d=d, b0=b0),
        out_shape=(jax.ShapeDtypeStruct((nb, s, d), F32),
                   jax.ShapeDtypeStruct((n, d // 4), U32),
                   jax.ShapeDtypeStruct((n, d // 4), U32),
                   jax.ShapeDtypeStruct((TOP_K, n), I32),
                   jax.ShapeDtypeStruct((TOP_K, n), F32)),
        grid=(nb, nt),
        in_specs=[pl.BlockSpec((1, tm, d // 2), lambda b, i: (b + b0, i, 0)),
                  pl.BlockSpec((1, tm, d // 2), lambda b, i: (b + b0, i, 0)),
                  pl.BlockSpec((1, tm, d), lambda b, i: (b + b0, i, 0)),
                  full(mod), full(ga), full(gb), full(wo), full(n2), full(wr_hi), full(wr_lo), full(rb_col),
                  full(wsg), full(wsu), full(wsd)],
        out_specs=(pl.BlockSpec((1, tm, d), lambda b, i: (b, i, 0)),
                   pl.BlockSpec((tm, d // 4), lambda b, i: (b * nt + i, 0)),
                   pl.BlockSpec((tm, d // 4), lambda b, i: (b * nt + i, 0)),
                   pl.BlockSpec((TOP_K, tm), lambda b, i: (0, b * nt + i)),
                   pl.BlockSpec((TOP_K, tm), lambda b, i: (0, b * nt + i))),
        compiler_params=_params("arbitrary", "arbitrary"),
        name="merge",
    )(o_a, o_b, x, mod, ga, gb, wo, n2, wr_hi, wr_lo, rb_col, wsg, wsu, wsd)


def _rank_kernel(idx_ref, rank_ref, cnt_ref, carry_ref):
    tm = idx_ref.shape[1]

    @pl.when(pl.program_id(0) == 0)
    def _():
        carry_ref[...] = jnp.zeros_like(carry_ref)

    idx = idx_ref[...]
    ite = lax.broadcasted_iota(I32, (N_EXPERTS, tm), 0)
    before = (lax.broadcasted_iota(I32, (tm, tm), 0) < lax.broadcasted_iota(I32, (tm, tm), 1)).astype(BF16)
    base = carry_ref[...]
    rows = []
    for k in range(TOP_K):
        oh = ite == idx[k:k + 1, :]
        ohf = jnp.where(oh, 1.0, 0.0)
        cum = jnp.dot(ohf.astype(BF16), before, preferred_element_type=F32)
        rows.append(jnp.sum(jnp.where(oh, cum + base, 0.0), axis=0, keepdims=True))
        base = base + jnp.sum(ohf, axis=1, keepdims=True)
    rank_ref[...] = jnp.concatenate(rows, axis=0).astype(I32)
    carry_ref[...] = base
    cnt_ref[...] = base


def _rank(idx, *, tm):
    n = idx.shape[1]
    return pl.pallas_call(
        _rank_kernel,
        out_shape=(jax.ShapeDtypeStruct((TOP_K, n), I32), jax.ShapeDtypeStruct((N_EXPERTS, 1), F32)),
        grid=(n // tm,),
        in_specs=[pl.BlockSpec((TOP_K, tm), lambda i: (0, i))],
        out_specs=(pl.BlockSpec((TOP_K, tm), lambda i: (0, i)), pl.BlockSpec((N_EXPERTS, 1), lambda i: (0, 0))),
        scratch_shapes=[pltpu.VMEM((N_EXPERTS, 1), F32)],
        compiler_params=_params("arbitrary"),
        name="rank",
    )(idx)


def _pos_kernel(idx_ref, rank_ref, pstart_ref, pos_ref):
    tm = idx_ref.shape[1]
    idx = idx_ref[...]
    ite = lax.broadcasted_iota(I32, (N_EXPERTS, tm), 0)
    pstart = pstart_ref[...]
    rows = [jnp.sum(jnp.where(ite == idx[k:k + 1, :], pstart, 0.0), axis=0, keepdims=True) for k in range(TOP_K)]
    pos_ref[...] = jnp.concatenate(rows, axis=0).astype(I32) + rank_ref[...]


def _pos(idx, rank, pstart_col, *, tm):
    n = idx.shape[1]
    return pl.pallas_call(
        _pos_kernel,
        out_shape=jax.ShapeDtypeStruct((TOP_K, n), I32),
        grid=(n // tm,),
        in_specs=[pl.BlockSpec((TOP_K, tm), lambda i: (0, i)),
                  pl.BlockSpec((TOP_K, tm), lambda i: (0, i)),
                  pl.BlockSpec((N_EXPERTS, 1), lambda i: (0, 0))],
        out_specs=pl.BlockSpec((TOP_K, tm), lambda i: (0, i)),
        compiler_params=_params("arbitrary"),
        name="pos",
    )(idx, rank, pstart_col)


RING_SLOTS = 4


def _expert_kernel(ps_ref, cnt_ref, tot_ref, wg_ref, wu_ref, wd_ref, xa_hbm, xb_hbm, ya_hbm, yb_hbm,
                   wg_s, wu_s, wd_s, xa_buf, xb_buf, ya_buf, yb_buf, in_sem, out_sem):
    e = pl.program_id(0)
    r = EXPERT_ROWS
    ahead = RING_SLOTS - 1
    total = tot_ref[0]
    first = ps_ref[e] // r
    cnt = cnt_ref[e]
    nch = (cnt + r - 1) // r

    def rows_of(g):
        return pl.ds(pl.multiple_of(g * r, r), r)

    def fetch(g):
        slot = g % RING_SLOTS
        return (pltpu.make_async_copy(xa_hbm.at[rows_of(g)], xa_buf.at[slot], in_sem.at[0, slot]),
                pltpu.make_async_copy(xb_hbm.at[rows_of(g)], xb_buf.at[slot], in_sem.at[1, slot]))

    def flush(g):
        slot = g % RING_SLOTS
        return (pltpu.make_async_copy(ya_buf.at[slot], ya_hbm.at[rows_of(g)], out_sem.at[0, slot]),
                pltpu.make_async_copy(yb_buf.at[slot], yb_hbm.at[rows_of(g)], out_sem.at[1, slot]))

    def start(copies):
        for cp in copies:
            cp.start()

    def wait(copies):
        for cp in copies:
            cp.wait()

    @pl.when(e == 0)
    def _():
        for g in range(ahead):
            @pl.when(g < total)
            def _():
                start(fetch(g))

    @pl.when(nch > 0)
    def _():
        wg_s[...] = wg_ref[0].astype(BF16)
        wu_s[...] = wu_ref[0].astype(BF16)
        wd_s[...] = wd_ref[0].astype(BF16)

        def chunk(c, carry):
            g = first + c
            slot = g % RING_SLOTS
            wait(fetch(g))

            @pl.when(g + ahead < total)
            def _():
                start(fetch(g + ahead))

            @pl.when(g >= RING_SLOTS)
            def _():
                wait(flush(g - RING_SLOTS))

            packed = jnp.concatenate([xa_buf[slot], xb_buf[slot]], axis=1)
            row = lax.broadcasted_iota(I32, packed.shape, 0)
            packed = jnp.where(row < cnt - c * r, packed, jnp.zeros_like(packed))
            x = _unpack_bf16_pairs(packed).astype(BF16)
            gate = jnp.dot(x, wg_s[...], preferred_element_type=F32)
            up = jnp.dot(x, wu_s[...], preferred_element_type=F32)
            y = jnp.dot((_silu(gate) * up).astype(BF16), wd_s[...], preferred_element_type=F32)
            out = _pack_bf16_pairs(y)
            half = out.shape[1] // 2
            ya_buf[slot] = out[:, :half]
            yb_buf[slot] = out[:, half:]
            start(flush(g))
            return carry

        lax.fori_loop(0, nch, chunk, 0)

    @pl.when(e == pl.num_programs(0) - 1)
    def _():
        for k in range(RING_SLOTS):
            @pl.when(total - 1 - k >= 0)
            def _():
                wait(flush(total - 1 - k))


def _experts(pstart, cnt, total_chunks, xs_a, xs_b, wg, wu, wd):
    rows, hw = xs_a.shape
    r = EXPERT_ROWS
    n_exp, d, f = wg.shape
    hbm = pl.BlockSpec(memory_space=pl.ANY)
    grid_spec = pltpu.PrefetchScalarGridSpec(
        num_scalar_prefetch=3,
        grid=(n_exp,),
        in_specs=[pl.BlockSpec((1, d, f), lambda e, ps, cn, tot: (e, 0, 0)),
                  pl.BlockSpec((1, d, f), lambda e, ps, cn, tot: (e, 0, 0)),
                  pl.BlockSpec((1, f, d), lambda e, ps, cn, tot: (e, 0, 0)),
                  hbm, hbm],
        out_specs=(hbm, hbm),
        scratch_shapes=[pltpu.VMEM((d, f), BF16), pltpu.VMEM((d, f), BF16), pltpu.VMEM((f, d), BF16),
                        pltpu.VMEM((RING_SLOTS, r, hw), U32), pltpu.VMEM((RING_SLOTS, r, hw), U32),
                        pltpu.VMEM((RING_SLOTS, r, hw), U32), pltpu.VMEM((RING_SLOTS, r, hw), U32),
                        pltpu.SemaphoreType.DMA((2, RING_SLOTS)), pltpu.SemaphoreType.DMA((2, RING_SLOTS))],
    )
    return pl.pallas_call(
        _expert_kernel,
        out_shape=(jax.ShapeDtypeStruct((rows, hw), U32), jax.ShapeDtypeStruct((rows, hw), U32)),
        grid_spec=grid_spec,
        compiler_params=_params("arbitrary"),
        name="experts",
    )(pstart, cnt, total_chunks, wg, wu, wd, xs_a, xs_b)


def _sc_scatter_rows(x, idx_flat, n_rows):
    n, w = x.shape
    m = idx_flat.shape[1]
    nwin = n // SC_WINDOW
    mesh = plsc.VectorSubcoreMesh(core_axis_name="core", subcore_axis_name="subcore")

    @functools.partial(pl.kernel, out_type=jax.ShapeDtypeStruct((n_rows, w), x.dtype), mesh=mesh,
                       scratch_types=[])
    def scatter_kernel(x_hbm, i_hbm, o_hbm):
        def body(x_vmem, i_vmem):
            pltpu.sync_copy(x_vmem, o_hbm.at[i_vmem.at[0]])

        pltpu.emit_pipeline(
            body,
            grid=(m // SC_WINDOW,),
            in_specs=[pl.BlockSpec((SC_WINDOW, w), lambda i: (i % nwin, 0)),
                      pl.BlockSpec((1, SC_WINDOW), lambda i: (0, i))],
            out_specs=[],
            core_axis_name=("core", "subcore"),
            dimension_semantics=(pltpu.PARALLEL,),
        )(x_hbm, i_hbm)

    return scatter_kernel(x, idx_flat)


def _sc_gather_rows(src, idx_flat):
    m = idx_flat.shape[1]
    w = src.shape[1]
    mesh = plsc.VectorSubcoreMesh(core_axis_name="core", subcore_axis_name="subcore")

    @functools.partial(pl.kernel, out_type=jax.ShapeDtypeStruct((m, w), src.dtype), mesh=mesh)
    def gather_kernel(x_hbm, i_hbm, o_hbm):
        def body(i_vmem, o_vmem):
            pltpu.sync_copy(x_hbm.at[i_vmem.at[0]], o_vmem)

        pltpu.emit_pipeline(
            body,
            grid=(m // SC_WINDOW,),
            in_specs=[pl.BlockSpec((1, SC_WINDOW), lambda i: (0, i))],
            out_specs=[pl.BlockSpec((SC_WINDOW, w), lambda i: (i, 0))],
            core_axis_name=("core", "subcore"),
            dimension_semantics=(pltpu.PARALLEL,),
        )(i_hbm, o_hbm)

    return gather_kernel(src, idx_flat)


def _finish_kernel(gate_ref, xres_ref, mod_ref, ya_ref, yb_ref, *rest, d, tiles_per_batch, tile0):
    o_ref = rest[-1]
    tm = gate_ref.shape[1]
    b = (pl.program_id(0) + tile0) // tiles_per_batch
    gates = gate_ref[...].T
    acc = jnp.zeros((tm, d), F32)
    for k in range(TOP_K):
        packed = jnp.concatenate([ya_ref[k], yb_ref[k]], axis=1)
        acc = acc + _unpack_bf16_pairs(packed) * gates[:, k:k + 1]
    o_ref[...] = xres_ref[...] + mod_ref[pl.ds(b, 1), 5 * d:6 * d] * acc


def _finish(gates, xres, mod, yg_a, yg_b, out_prev, *, n_total, row0, tm, tiles_per_batch):
    n, d = xres.shape
    tile0 = row0 // tm
    in_specs = [pl.BlockSpec((TOP_K, tm), lambda i: (0, i)),
                pl.BlockSpec((tm, d), lambda i: (i, 0)),
                pl.BlockSpec(mod.shape, lambda i: (0, 0)),
                pl.BlockSpec((TOP_K, tm, d // 4), lambda i: (0, i, 0)),
                pl.BlockSpec((TOP_K, tm, d // 4), lambda i: (0, i, 0))]
    args = [gates, xres, mod, yg_a, yg_b]
    aliases = {}
    if out_prev is not None:
        in_specs.append(pl.BlockSpec(memory_space=pl.ANY))
        args.append(out_prev)
        aliases = {5: 0}
    return pl.pallas_call(
        functools.partial(_finish_kernel, d=d, tiles_per_batch=tiles_per_batch, tile0=tile0),
        out_shape=jax.ShapeDtypeStruct((n_total, d), F32),
        grid=(n // tm,),
        in_specs=in_specs,
        out_specs=pl.BlockSpec((tm, d), lambda i: (i + tile0, 0)),
        input_output_aliases=aliases,
        compiler_params=_params("arbitrary"),
        name="finish",
    )(*args)


_QA_HEAD_ORDER = (0, 4, 1, 5, 2, 6, 3, 7)


def _reorder_qa_heads(a, axis):
    return jnp.concatenate([lax.slice_in_dim(a, h * HEAD_DIM, (h + 1) * HEAD_DIM, axis=axis)
                            for h in _QA_HEAD_ORDER], axis=axis)


def _rope_swap(a):
    quarter = HEAD_DIM // 4
    lead = a.shape[:-1]
    return jnp.flip(a.reshape(lead + (-1, 2, quarter)), axis=-2).reshape(a.shape)


def _rope_tables(s):
    quarter = HEAD_DIM // 4
    t = jnp.arange(s)
    row = (t // GRID_W).astype(F32)
    col = (t % GRID_W).astype(F32)
    freqs = ROPE_BASE ** (-jnp.arange(quarter, dtype=F32) / quarter)
    ar = row[:, None] * freqs[None, :]
    ac = col[:, None] * freqs[None, :]
    cos = jnp.concatenate([jnp.cos(ar), jnp.cos(ar), jnp.cos(ac), jnp.cos(ac)], axis=1)
    sin = jnp.concatenate([-jnp.sin(ar), jnp.sin(ar), -jnp.sin(ac), jnp.sin(ac)], axis=1)
    return jnp.tile(cos, (1, 2)), jnp.tile(sin, (1, 2))


def kernel(x, c, ctx, c_ctx, w_ada, b_ada, norm1, norm2, w_in, q_norm_a, k_norm_a, q_norm_b, k_norm_b, sink_a,
           rpb_b, out_norm_a, out_norm_b, w_out, w_router, router_bias, we_gate, we_up, we_down, ws_gate, ws_up,
           ws_down):
    assert w_ada.shape[0] == 1, "single-layer block"
    bsz, s, d = x.shape
    lc = ctx.shape[1]
    n = bsz * s
    rows = s // GRID_W
    assert s % (NA_QROWS * GRID_W) == 0 and rows >= NA_KROWS and bsz <= 4 and d == 1024

    c8 = jnp.concatenate([c, c_ctx[None, :], jnp.zeros((8 - bsz - 1, d), F32)], axis=0)
    mod = _ada(c8, w_ada[0], b_ada[0][None, :])

    w = w_in[0]
    aw, akw, bw = A_Q_HEADS * HEAD_DIM, A_KV_HEADS * HEAD_DIM, B_HEADS * HEAD_DIM
    cuts = np.cumsum([0, aw, akw, akw, bw, bw, bw])
    qa_w, ka_w, va_w, qb_w, kb_w, vb_w = [w[:, cuts[i]:cuts[i + 1]] for i in range(6)]
    qa_w = _reorder_qa_heads(qa_w, 1)
    w_ext = jnp.concatenate([qa_w, qb_w, kb_w, vb_w, ka_w, va_w, _rope_swap(qa_w), _rope_swap(ka_w)],
                            axis=1).astype(BF16)
    scale = HEAD_DIM ** -0.5 * LOG2E
    pair = lambda g: jnp.tile(g, 2)
    gains = jnp.stack([pair(q_norm_a[0]) * scale, pair(q_norm_b[0]) * scale, pair(k_norm_b[0]), pair(k_norm_a[0]),
                       pair(_rope_swap(q_norm_a[0])) * scale, pair(_rope_swap(k_norm_a[0])),
                       jnp.zeros((LANES,), F32), jnp.zeros((LANES,), F32)])
    bd = jnp.asarray(np.kron(np.eye(2), np.full((HEAD_DIM, HEAD_DIM), 1.0 / HEAD_DIM)), BF16)
    cos_t, sin_t = _rope_tables(s)
    n1 = norm1[0][None, :]
    zq = _inproj(x, mod, n1, w_ext, gains, cos_t, sin_t, bd, mod_row=None, tm=512)
    zc = _inproj(ctx, mod, n1, w_ext, gains, jnp.ones((lc, LANES), F32), jnp.zeros((lc, LANES), F32), bd,
                 mod_row=bsz, tm=lc)

    sink = sink_a[0].astype(F32)
    sink_rep = jnp.concatenate([jnp.broadcast_to(sink[h] * LOG2E, (A_WINDOW, LANES)) for h in _QA_HEAD_ORDER])
    o_a = _attn_a(zq, zc, sink_rep, _attn_a_mask(A_WINDOW))
    o_b = _attn_b(zq, zc, _na_table(rpb_b[0], rows))

    ga = _reorder_qa_heads(out_norm_a[0], 0)[None, :]
    gb = out_norm_b[0][None, :]
    wo = jnp.concatenate([_reorder_qa_heads(w_out[0][:aw], 0), w_out[0][aw:]], axis=0).astype(BF16)
    wr_hi, wr_lo = _split_bf16(w_router[0].T)
    n2 = norm2[0][None, :]
    rb_col = router_bias[0][:, None]
    wsg, wsu, wsd = ws_gate[0].astype(BF16), ws_up[0].astype(BF16), ws_down[0].astype(BF16)

    t = EXPERT_ROWS
    groups = 2 if bsz % 2 == 0 else 1
    nb = bsz // groups
    ng = nb * s
    out = None
    for h in range(groups):
        xres, hp_a, hp_b, idx, gates = _merge(o_a, o_b, x, mod, ga, gb, wo, n2, wr_hi, wr_lo, rb_col, wsg, wsu, wsd,
                                              tm=512, b0=h * nb, nb=nb)
        rank, counts = _rank(idx, tm=512)
        cnt = counts[:, 0].astype(I32)
        padded = (cnt + t - 1) // t * t
        pends = jnp.cumsum(padded)
        pstart = pends - padded
        n_rows = ng * TOP_K + N_EXPERTS * t
        pos_flat = _pos(idx, rank, pstart.astype(F32)[:, None], tm=512).reshape(1, TOP_K * ng)
        xs_a = _sc_scatter_rows(hp_a, pos_flat, n_rows)
        xs_b = _sc_scatter_rows(hp_b, pos_flat, n_rows)
        ys_a, ys_b = _experts(pstart, cnt, pends[-1:] // t, xs_a, xs_b, we_gate[0], we_up[0], we_down[0])
        yg_a = _sc_gather_rows(ys_a, pos_flat).reshape(TOP_K, ng, d // 4)
        yg_b = _sc_gather_rows(ys_b, pos_flat).reshape(TOP_K, ng, d // 4)
        out = _finish(gates, xres.reshape(ng, d), mod, yg_a, yg_b, out, n_total=n, row0=h * ng, tm=256,
                      tiles_per_batch=s // 256)
    return out.reshape(bsz, s, d)
```

```python
import functools

import numpy as np
import jax
import jax.numpy as jnp
from jax import lax
from jax.experimental import pallas as pl
from jax.experimental.pallas import tpu as pltpu
from jax.experimental.pallas import tpu_sc as plsc

F32 = jnp.float32
BF16 = jnp.bfloat16
I32 = jnp.int32
U32 = jnp.uint32

LANES = 128
HEAD_DIM = 64
HEAD_PAIR = 2 * HEAD_DIM
GRID_W = 64
A_Q_HEADS = 8
A_KV_HEADS = 2
A_WINDOW = 128
B_HEADS = 8
NA_KH = 8
NA_KW = 16
NA_QROWS = 4
NA_KROWS = NA_QROWS + NA_KH
ROPE_BASE = 10000.0
N_EXPERTS = 256
TOP_K = 8
N_GROUPS = 8
TOPK_GROUPS = 4
ROUTED_SCALE = 2.5
LOG2E = 1.4426950408889634
EPS = 1e-6
NEG_INF = -1e30
EXPERT_ROWS = 256
SC_WINDOW = 128
VMEM_LIMIT = 56 * 1024 * 1024

_NT = (((1,), (1,)), ((), ()))


def _params(*sem):
    return pltpu.CompilerParams(dimension_semantics=sem, vmem_limit_bytes=VMEM_LIMIT)


def _silu(v):
    return v * jax.nn.sigmoid(v)


def _rms(v, gain):
    return v * lax.rsqrt(jnp.mean(v * v, axis=-1, keepdims=True) + EPS) * gain


def _pack_bf16_pairs(v):
    n = v.shape[1] // 2
    lo = lax.bitcast_convert_type(v[:, :n].astype(BF16).astype(F32), U32) >> 16
    hi = lax.bitcast_convert_type(v[:, n:].astype(BF16).astype(F32), U32) & jnp.uint32(0xFFFF0000)
    return hi | lo


def _split_bf16(v):
    hi = lax.bitcast_convert_type(lax.bitcast_convert_type(v, U32) & jnp.uint32(0xFFFF0000), F32)
    return hi.astype(BF16), (v - hi).astype(BF16)


def _unpack_bf16_pairs(w):
    lo = lax.bitcast_convert_type(w << 16, F32)
    hi = lax.bitcast_convert_type(w & jnp.uint32(0xFFFF0000), F32)
    return jnp.concatenate([lo, hi], axis=1)


def _ada_kernel(c_ref, w_ref, b_ref, o_ref):
    a = _silu(c_ref[...])
    o_ref[...] = jnp.dot(a, w_ref[...], preferred_element_type=F32,
                         precision=lax.Precision.HIGHEST) + b_ref[...]


def _ada(c8, w, b):
    d, n = w.shape
    bn = n // 4
    return pl.pallas_call(
        _ada_kernel,
        out_shape=jax.ShapeDtypeStruct((8, n), F32),
        grid=(n // bn,),
        in_specs=[pl.BlockSpec((8, d), lambda j: (0, 0)),
                  pl.BlockSpec((d, bn), lambda j: (0, j)),
                  pl.BlockSpec((1, bn), lambda j: (0, j))],
        out_specs=pl.BlockSpec((8, bn), lambda j: (0, j)),
        compiler_params=_params("arbitrary"),
        name="ada",
    )(c8, w, b)


_QA, _QB, _KB, _VB, _KA, _VA, _QA_SW, _KA_SW = 0, 4, 8, 12, 16, 17, 18, 22
_OUT_BLOCKS = 18
_EXT_BLOCKS = 23


def _inproj_kernel(x_ref, mod_ref, n1_ref, w_ref, g_ref, cos_ref, sin_ref, bd_ref, o_ref, *, mod_row, d):
    b = pl.program_id(0) if mod_row is None else mod_row
    xn = _rms(x_ref[0], n1_ref[...])
    sh = mod_ref[pl.ds(b, 1), 0:d]
    sc = mod_ref[pl.ds(b, 1), d:2 * d]
    h = (xn * (1.0 + sc) + sh).astype(BF16)
    z = jnp.dot(h, w_ref[...], preferred_element_type=F32)
    bd = bd_ref[...]
    cos = cos_ref[...]
    sin = sin_ref[...]

    def blk(j):
        return z[:, j * LANES:(j + 1) * LANES]

    def head_rinv(zb):
        ms = jnp.dot((zb * zb).astype(BF16), bd, preferred_element_type=F32)
        return lax.rsqrt(ms + EPS)

    def put(j, v):
        o_ref[0, :, j * LANES:(j + 1) * LANES] = v.astype(BF16)

    def roped(j, j_sw, g_row, g_sw_row):
        zb = blk(j)
        r = head_rinv(zb)
        put(j, (zb * r * g_ref[g_row:g_row + 1, :]) * cos + (blk(j_sw) * r * g_ref[g_sw_row:g_sw_row + 1, :]) * sin)

    def normed(j, g_row):
        zb = blk(j)
        put(j, zb * head_rinv(zb) * g_ref[g_row:g_row + 1, :])

    for j in range(4):
        roped(_QA + j, _QA_SW + j, 0, 4)
        normed(_QB + j, 1)
        normed(_KB + j, 2)
        put(_VB + j, blk(_VB + j))
    roped(_KA, _KA_SW, 3, 5)
    put(_VA, blk(_VA))


def _inproj(x, mod, n1, w_ext, gains, cos_t, sin_t, bd, *, mod_row, tm):
    bsz, s, d = x.shape
    kern = functools.partial(_inproj_kernel, mod_row=mod_row, d=d)
    return pl.pallas_call(
        kern,
        out_shape=jax.ShapeDtypeStruct((bsz, s, _OUT_BLOCKS * LANES), BF16),
        grid=(bsz, s // tm),
        in_specs=[pl.BlockSpec((1, tm, d), lambda b, i: (b, i, 0)),
                  pl.BlockSpec(mod.shape, lambda b, i: (0, 0)),
                  pl.BlockSpec((1, d), lambda b, i: (0, 0)),
                  pl.BlockSpec(w_ext.shape, lambda b, i: (0, 0)),
                  pl.BlockSpec(gains.shape, lambda b, i: (0, 0)),
                  pl.BlockSpec((tm, LANES), lambda b, i: (i, 0)),
                  pl.BlockSpec((tm, LANES), lambda b, i: (i, 0)),
                  pl.BlockSpec((LANES, LANES), lambda b, i: (0, 0))],
        out_specs=pl.BlockSpec((1, tm, _OUT_BLOCKS * LANES), lambda b, i: (b, i, 0)),
        compiler_params=_params("arbitrary", "arbitrary"),
        name="inproj",
    )(x, mod, n1, w_ext, gains, cos_t, sin_t, bd)


def _split_pair(qp, lo):
    zero = jnp.zeros_like(qp)
    return jnp.concatenate([jnp.where(lo, qp, zero), jnp.where(lo, zero, qp)], axis=0)


def _softmax_pv(s_parts, v_parts, sink_rep):
    chunks = [s[:, c * LANES:(c + 1) * LANES] for s in s_parts for c in range(s.shape[1] // LANES)]
    m = jnp.max(functools.reduce(jnp.maximum, chunks), axis=-1, keepdims=True)
    m_rep = jnp.broadcast_to(m, (m.shape[0], LANES))
    if sink_rep is not None:
        m_rep = jnp.maximum(m_rep, sink_rep)
    acc = None
    for s, v in zip(s_parts, v_parts):
        p = jnp.concatenate([jnp.exp2(s[:, c * LANES:(c + 1) * LANES] - m_rep)
                             for c in range(s.shape[1] // LANES)], axis=1).astype(BF16)
        v_ext = jnp.concatenate([v, jnp.ones_like(v)], axis=1)
        o = jnp.dot(p, v_ext, preferred_element_type=F32)
        acc = o if acc is None else acc + o
    l_rep = acc[:, LANES:]
    if sink_rep is not None:
        l_rep = l_rep + jnp.exp2(sink_rep - m_rep)
    return acc[:, :LANES] * (1.0 / l_rep)


def _attn_a_kernel(q_ref, kp_ref, kc_ref, kn_ref, vp_ref, vc_ref, vn_ref, kx_ref, vx_ref, sink_ref, mask_ref,
                   o_ref):
    tq = q_ref.shape[1]
    lo = lax.broadcasted_iota(I32, (tq, LANES), 1) < HEAD_DIM
    q = q_ref[0]
    qs = jnp.concatenate([_split_pair(q[:, p * LANES:(p + 1) * LANES], lo) for p in range(4)], axis=0)
    k = jnp.concatenate([kp_ref[0], kc_ref[0], kn_ref[0]], axis=0)
    v = jnp.concatenate([vp_ref[0], vc_ref[0], vn_ref[0]], axis=0)
    s_loc = lax.dot_general(qs, k, _NT, preferred_element_type=F32) + mask_ref[0]
    s_ctx = lax.dot_general(qs, kx_ref[0], _NT, preferred_element_type=F32)
    o = _softmax_pv([s_loc, s_ctx], [v, vx_ref[0]], sink_ref[...])
    for p in range(4):
        o_lo = o[(2 * p) * tq:(2 * p + 1) * tq]
        o_hi = o[(2 * p + 1) * tq:(2 * p + 2) * tq]
        o_ref[0, :, p * LANES:(p + 1) * LANES] = jnp.where(lo, o_lo, o_hi).astype(BF16)


def _attn_a(zq, zc, sink_rep, mask):
    bsz, s, _ = zq.shape
    lc = zc.shape[1]
    tq = A_WINDOW
    nblk = s // tq
    ka, va = _KA, _VA

    def kv_spec(col, shift):
        return pl.BlockSpec((1, tq, LANES), lambda b, j: (b, jnp.clip(j + shift, 0, nblk - 1), col))

    def variant(j):
        return jnp.where(j == 0, 0, jnp.where(j == nblk - 1, 2, 1))

    return pl.pallas_call(
        _attn_a_kernel,
        out_shape=jax.ShapeDtypeStruct((bsz, s, 4 * LANES), BF16),
        grid=(bsz, nblk),
        in_specs=[pl.BlockSpec((1, tq, 4 * LANES), lambda b, j: (b, j, 0)),
                  kv_spec(ka, -1), kv_spec(ka, 0), kv_spec(ka, 1),
                  kv_spec(va, -1), kv_spec(va, 0), kv_spec(va, 1),
                  pl.BlockSpec((1, lc, LANES), lambda b, j: (b, 0, ka)),
                  pl.BlockSpec((1, lc, LANES), lambda b, j: (b, 0, va)),
                  pl.BlockSpec(sink_rep.shape, lambda b, j: (0, 0)),
                  pl.BlockSpec((1,) + mask.shape[1:], lambda b, j: (variant(j), 0, 0))],
        out_specs=pl.BlockSpec((1, tq, 4 * LANES), lambda b, j: (b, j, 0)),
        compiler_params=_params("arbitrary", "arbitrary"),
        name="attn_a",
    )(zq, zq, zq, zq, zq, zq, zq, zc, zc, sink_rep, mask)


def _attn_a_mask(tq):
    qi = np.arange(tq)[:, None]
    kj = np.arange(3 * tq)[None, :]
    ok = (kj >= qi) & (kj <= qi + 2 * tq)
    variants = [ok & (kj >= tq), ok, ok & (kj < 2 * tq)]
    m = np.stack([np.tile(np.where(v, 0.0, NEG_INF).astype(np.float32), (A_Q_HEADS, 1)) for v in variants])
    return jnp.asarray(m)


def _attn_b_kernel(q_ref, k0_ref, k1_ref, k2_ref, v0_ref, v1_ref, v2_ref, kx_ref, vx_ref, tab_ref, o_ref):
    tq = q_ref.shape[1]
    lo = lax.broadcasted_iota(I32, (tq, LANES), 1) < HEAD_DIM
    for p in range(4):
        sl = slice(p * LANES, (p + 1) * LANES)
        qs = _split_pair(q_ref[0, :, sl], lo)
        k = jnp.concatenate([k0_ref[0, :, sl], k1_ref[0, :, sl], k2_ref[0, :, sl]], axis=0)
        v = jnp.concatenate([v0_ref[0, :, sl], v1_ref[0, :, sl], v2_ref[0, :, sl]], axis=0)
        s_loc = lax.dot_general(qs, k, _NT, preferred_element_type=F32) + tab_ref[0, p]
        s_ctx = lax.dot_general(qs, kx_ref[0, :, sl], _NT, preferred_element_type=F32)
        o = _softmax_pv([s_loc, s_ctx], [v, vx_ref[0, :, sl]], None)
        o_ref[0, :, sl] = jnp.where(lo, o[:tq], o[tq:]).astype(BF16)


def _attn_b(zq, zc, table):
    bsz, s, _ = zq.shape
    lc = zc.shape[1]
    tq = NA_QROWS * GRID_W
    ng = s // tq
    qb, kb, vb = _QB // 4, _KB // 4, _VB // 4

    def kv_spec(col, off):
        return pl.BlockSpec((1, tq, 4 * LANES), lambda i, b: (b, jnp.clip(i - 1, 0, ng - 3) + off, col))

    def variant(i):
        return jnp.where(i == 0, 0, jnp.where(i == ng - 1, 2, 1))

    return pl.pallas_call(
        _attn_b_kernel,
        out_shape=jax.ShapeDtypeStruct((bsz, s, 4 * LANES), BF16),
        grid=(ng, bsz),
        in_specs=[pl.BlockSpec((1, tq, 4 * LANES), lambda i, b: (b, i, qb)),
                  kv_spec(kb, 0), kv_spec(kb, 1), kv_spec(kb, 2),
                  kv_spec(vb, 0), kv_spec(vb, 1), kv_spec(vb, 2),
                  pl.BlockSpec((1, lc, 4 * LANES), lambda i, b: (b, 0, kb)),
                  pl.BlockSpec((1, lc, 4 * LANES), lambda i, b: (b, 0, vb)),
                  pl.BlockSpec((1,) + table.shape[1:], lambda i, b: (variant(i), 0, 0, 0))],
        out_specs=pl.BlockSpec((1, tq, 4 * LANES), lambda i, b: (b, i, 0)),
        compiler_params=_params("arbitrary", "arbitrary"),
        name="attn_b",
    )(zq, zq, zq, zq, zq, zq, zq, zc, zc, table)


def _na_table(rpb, rows):
    ng = rows // NA_QROWS
    nq, nk = NA_QROWS * GRID_W, NA_KROWS * GRID_W
    qc = np.arange(GRID_W)
    kc = np.arange(GRID_W)
    ws = np.clip(qc - NA_KW // 2, 0, GRID_W - NA_KW)
    valid_c = (kc[None, :] >= ws[:, None]) & (kc[None, :] < ws[:, None] + NA_KW)
    dc = np.clip(kc[None, :] - qc[:, None], -(NA_KW - 1), NA_KW - 1) + (NA_KW - 1)
    c_sel = (dc[..., None] == np.arange(2 * NA_KW - 1)) & valid_c[..., None]
    tiles = jnp.einsum('hab,uvb->hauv', rpb.astype(F32), jnp.asarray(c_sel, F32), precision=lax.Precision.HIGHEST)
    tiles = jnp.where(jnp.asarray(valid_c)[None, None], tiles * LOG2E, NEG_INF)
    masked = jnp.full((B_HEADS, GRID_W, GRID_W), NEG_INF, F32)
    tabs = []
    for i in (0, 1, ng - 1):
        start = int(np.clip(NA_QROWS * i - NA_KH // 2, 0, rows - NA_KROWS))
        q_rows = []
        for qr in range(NA_QROWS):
            r = NA_QROWS * i + qr
            rs = int(np.clip(r - NA_KH // 2, 0, rows - NA_KH))
            k_tiles = []
            for kr in range(NA_KROWS):
                krow = start + kr
                k_tiles.append(tiles[:, krow - r + (NA_KH - 1)] if rs <= krow < rs + NA_KH else masked)
            q_rows.append(jnp.concatenate(k_tiles, axis=-1))
        tabs.append(jnp.concatenate(q_rows, axis=-2))
    return jnp.stack(tabs).reshape(3, B_HEADS // 2, 2 * nq, nk)


def _merge_kernel(oa_ref, ob_ref, x_ref, mod_ref, ga_ref, gb_ref, wo_ref, n2_ref, wrh_ref, wrl_ref, rb_ref,
                  wsg_ref, wsu_ref, wsd_ref, xres_ref, hpa_ref, hpb_ref, idx_ref, gate_ref, *, d):
    b = pl.program_id(0)
    tm = x_ref.shape[1]

    def mod(k):
        return mod_ref[pl.ds(b, 1), k * d:(k + 1) * d]

    na = _rms(oa_ref[0].astype(F32), ga_ref[...])
    nb = _rms(ob_ref[0].astype(F32), gb_ref[...])
    cat = jnp.concatenate([na, nb], axis=1).astype(BF16)
    y = jnp.dot(cat, wo_ref[...], preferred_element_type=F32)
    x1 = x_ref[0] + mod(2) * y
    h2 = _rms(x1, n2_ref[...]) * (1.0 + mod(4)) + mod(3)

    hb = h2.astype(BF16)
    act = _silu(jnp.dot(hb, wsg_ref[...], preferred_element_type=F32)) * jnp.dot(hb, wsu_ref[...],
                                                                                 preferred_element_type=F32)
    shared = jnp.dot(act.astype(BF16), wsd_ref[...], preferred_element_type=F32)
    xres_ref[0] = x1 + mod(5) * shared
    packed = _pack_bf16_pairs(h2)
    hpa_ref[...] = packed[:, :d // 4]
    hpb_ref[...] = packed[:, d // 4:]

    h_hi, h_lo = _split_bf16(h2)
    logits = (lax.dot_general(wrh_ref[...], h_hi, _NT, preferred_element_type=F32)
              + lax.dot_general(wrh_ref[...], h_lo, _NT, preferred_element_type=F32)
              + lax.dot_general(wrl_ref[...], h_hi, _NT, preferred_element_type=F32))
    scores = jax.nn.sigmoid(logits)
    sel = scores + rb_ref[...]
    per = N_EXPERTS // N_GROUPS
    g3 = sel.reshape(N_GROUPS, per, tm)
    it3 = lax.broadcasted_iota(I32, (N_GROUPS, per, tm), 1)
    m1 = jnp.max(g3, axis=1, keepdims=True)
    first = jnp.min(jnp.where(g3 == m1, it3, per), axis=1, keepdims=True)
    m2 = jnp.max(jnp.where(it3 == first, -jnp.inf, g3), axis=1, keepdims=True)
    gscore = (m1 + m2).reshape(N_GROUPS, tm)

    itg = lax.broadcasted_iota(I32, (N_GROUPS, tm), 0)
    gsel = jnp.zeros((N_GROUPS, tm), F32)
    cur = gscore
    for _ in range(TOPK_GROUPS):
        mx = jnp.max(cur, axis=0, keepdims=True)
        fi = jnp.min(jnp.where(cur == mx, itg, N_GROUPS), axis=0, keepdims=True)
        pick = itg == fi
        gsel = jnp.where(pick, 1.0, gsel)
        cur = jnp.where(pick, -jnp.inf, cur)
    emask = jnp.broadcast_to(gsel.reshape(N_GROUPS, 1, tm), (N_GROUPS, per, tm)).reshape(N_EXPERTS, tm) > 0.5

    ite = lax.broadcasted_iota(I32, (N_EXPERTS, tm), 0)
    cur = jnp.where(emask, sel, NEG_INF)
    idx_rows, s_rows = [], []
    for _ in range(TOP_K):
        mx = jnp.max(cur, axis=0, keepdims=True)
        fi = jnp.min(jnp.where(cur == mx, ite, N_EXPERTS), axis=0, keepdims=True)
        pick = ite == fi
        idx_rows.append(fi)
        s_rows.append(jnp.sum(jnp.where(pick, scores, 0.0), axis=0, keepdims=True))
        cur = jnp.where(pick, -jnp.inf, cur)
    top_s = jnp.concatenate(s_rows, axis=0)
    idx_ref[...] = jnp.concatenate(idx_rows, axis=0)
    gate_ref[...] = top_s / jnp.sum(top_s, axis=0, keepdims=True) * ROUTED_SCALE


def _merge(o_a, o_b, x, mod, ga, gb, wo, n2, wr_hi, wr_lo, rb_col, wsg, wsu, wsd, *, tm):
    bsz, s, d = x.shape
    nt = s // tm
    n = bsz * s
    full = lambda a: pl.BlockSpec(a.shape, lambda b, i: (0,) * a.ndim)
    return pl.pallas_call(
        functools.partial(_merge_kernel, d=d),
        out_shape=(jax.ShapeDtypeStruct((bsz, s, d), F32),
                   jax.ShapeDtypeStruct((n, d // 4), U32),
                   jax.ShapeDtypeStruct((n, d // 4), U32),
                   jax.ShapeDtypeStruct((TOP_K, n), I32),
                   jax.ShapeDtypeStruct((TOP_K, n), F32)),
        grid=(bsz, nt),
        in_specs=[pl.BlockSpec((1, tm, d // 2), lambda b, i: (b, i, 0)),
                  pl.BlockSpec((1, tm, d // 2), lambda b, i: (b, i, 0)),
                  pl.BlockSpec((1, tm, d), lambda b, i: (b, i, 0)),
                  full(mod), full(ga), full(gb), full(wo), full(n2), full(wr_hi), full(wr_lo), full(rb_col),
                  full(wsg), full(wsu), full(wsd)],
        out_specs=(pl.BlockSpec((1, tm, d), lambda b, i: (b, i, 0)),
                   pl.BlockSpec((tm, d // 4), lambda b, i: (b * nt + i, 0)),
                   pl.BlockSpec((tm, d // 4), lambda b, i: (b * nt + i, 0)),
                   pl.BlockSpec((TOP_K, tm), lambda b, i: (0, b * nt + i)),
                   pl.BlockSpec((TOP_K, tm), lambda b, i: (0, b * nt + i))),
        compiler_params=_params("arbitrary", "arbitrary"),
        name="merge",
    )(o_a, o_b, x, mod, ga, gb, wo, n2, wr_hi, wr_lo, rb_col, wsg, wsu, wsd)


def _rank_kernel(idx_ref, rank_ref, cnt_ref, carry_ref):
    tm = idx_ref.shape[1]

    @pl.when(pl.program_id(0) == 0)
    def _():
        carry_ref[...] = jnp.zeros_like(carry_ref)

    idx = idx_ref[...]
    ite = lax.broadcasted_iota(I32, (N_EXPERTS, tm), 0)
    before = (lax.broadcasted_iota(I32, (tm, tm), 0) < lax.broadcasted_iota(I32, (tm, tm), 1)).astype(BF16)
    base = carry_ref[...]
    rows = []
    for k in range(TOP_K):
        oh = ite == idx[k:k + 1, :]
        ohf = jnp.where(oh, 1.0, 0.0)
        cum = jnp.dot(ohf.astype(BF16), before, preferred_element_type=F32)
        rows.append(jnp.sum(jnp.where(oh, cum + base, 0.0), axis=0, keepdims=True))
        base = base + jnp.sum(ohf, axis=1, keepdims=True)
    rank_ref[...] = jnp.concatenate(rows, axis=0).astype(I32)
    carry_ref[...] = base
    cnt_ref[...] = base


def _rank(idx, *, tm):
    n = idx.shape[1]
    return pl.pallas_call(
        _rank_kernel,
        out_shape=(jax.ShapeDtypeStruct((TOP_K, n), I32), jax.ShapeDtypeStruct((N_EXPERTS, 1), F32)),
        grid=(n // tm,),
        in_specs=[pl.BlockSpec((TOP_K, tm), lambda i: (0, i))],
        out_specs=(pl.BlockSpec((TOP_K, tm), lambda i: (0, i)), pl.BlockSpec((N_EXPERTS, 1), lambda i: (0, 0))),
        scratch_shapes=[pltpu.VMEM((N_EXPERTS, 1), F32)],
        compiler_params=_params("arbitrary"),
        name="rank",
    )(idx)


def _pos_kernel(idx_ref, rank_ref, pstart_ref, pos_ref):
    tm = idx_ref.shape[1]
    idx = idx_ref[...]
    ite = lax.broadcasted_iota(I32, (N_EXPERTS, tm), 0)
    pstart = pstart_ref[...]
    rows = [jnp.sum(jnp.where(ite == idx[k:k + 1, :], pstart, 0.0), axis=0, keepdims=True) for k in range(TOP_K)]
    pos_ref[...] = jnp.concatenate(rows, axis=0).astype(I32) + rank_ref[...]


def _pos(idx, rank, pstart_col, *, tm):
    n = idx.shape[1]
    return pl.pallas_call(
        _pos_kernel,
        out_shape=jax.ShapeDtypeStruct((TOP_K, n), I32),
        grid=(n // tm,),
        in_specs=[pl.BlockSpec((TOP_K, tm), lambda i: (0, i)),
                  pl.BlockSpec((TOP_K, tm), lambda i: (0, i)),
                  pl.BlockSpec((N_EXPERTS, 1), lambda i: (0, 0))],
        out_specs=pl.BlockSpec((TOP_K, tm), lambda i: (0, i)),
        compiler_params=_params("arbitrary"),
        name="pos",
    )(idx, rank, pstart_col)


UNIT_CHUNKS = (4, 2, 1)
RING_AHEAD = 4
RING_SLOTS = RING_AHEAD + UNIT_CHUNKS[0]


def _expert_kernel(ps_ref, cnt_ref, tot_ref, wg_ref, wu_ref, wd_ref, xa_hbm, xb_hbm, ya_hbm, yb_hbm,
                   wg_s, wu_s, wd_s, xa_buf, xb_buf, ya_buf, yb_buf, in_sem, out_sem):
    e = pl.program_id(0)
    r = EXPERT_ROWS
    ahead = RING_AHEAD
    total = tot_ref[0]
    first = ps_ref[e] // r
    cnt = cnt_ref[e]
    nch = (cnt + r - 1) // r

    def rows_of(g):
        return pl.ds(pl.multiple_of(g * r, r), r)

    def fetch(g):
        slot = g % RING_SLOTS
        return (pltpu.make_async_copy(xa_hbm.at[rows_of(g)], xa_buf.at[slot], in_sem.at[0, slot]),
                pltpu.make_async_copy(xb_hbm.at[rows_of(g)], xb_buf.at[slot], in_sem.at[1, slot]))

    def flush(g):
        slot = g % RING_SLOTS
        return (pltpu.make_async_copy(ya_buf.at[slot], ya_hbm.at[rows_of(g)], out_sem.at[0, slot]),
                pltpu.make_async_copy(yb_buf.at[slot], yb_hbm.at[rows_of(g)], out_sem.at[1, slot]))

    def start(copies):
        for cp in copies:
            cp.start()

    def wait(copies):
        for cp in copies:
            cp.wait()

    @pl.when(e == 0)
    def _():
        for g in range(ahead):
            @pl.when(g < total)
            def _():
                start(fetch(g))

    @pl.when(nch > 0)
    def _():
        wg_s[...] = wg_ref[0].astype(BF16)
        wu_s[...] = wu_ref[0].astype(BF16)
        wd_s[...] = wd_ref[0].astype(BF16)

        def unit(c0, u):
            for j in range(u):
                g = first + c0 + j
                wait(fetch(g))

                @pl.when(g + ahead < total)
                def _():
                    start(fetch(g + ahead))

                @pl.when(g >= RING_SLOTS)
                def _():
                    wait(flush(g - RING_SLOTS))

            slots = [(first + c0 + j) % RING_SLOTS for j in range(u)]
            packed = jnp.concatenate([jnp.concatenate([xa_buf[sl], xb_buf[sl]], axis=1) for sl in slots], axis=0)
            row = lax.broadcasted_iota(I32, packed.shape, 0)
            packed = jnp.where(row < cnt - c0 * r, packed, jnp.zeros_like(packed))
            x = _unpack_bf16_pairs(packed).astype(BF16)
            gate = jnp.dot(x, wg_s[...], preferred_element_type=F32)
            up = jnp.dot(x, wu_s[...], preferred_element_type=F32)
            y = jnp.dot((_silu(gate) * up).astype(BF16), wd_s[...], preferred_element_type=F32)
            out = _pack_bf16_pairs(y)
            half = out.shape[1] // 2
            for j, sl in enumerate(slots):
                ya_buf[sl] = out[j * r:(j + 1) * r, :half]
                yb_buf[sl] = out[j * r:(j + 1) * r, half:]
                start(flush(first + c0 + j))

        big = UNIT_CHUNKS[0]

        def big_unit(i, carry):
            unit(i * big, big)
            return carry

        lax.fori_loop(0, nch // big, big_unit, 0)
        done = nch // big * big
        for u in UNIT_CHUNKS[1:]:
            @pl.when((nch % (2 * u)) >= u)
            def _():
                unit(done, u)

            done = done + jnp.where((nch % (2 * u)) >= u, u, 0)

    @pl.when(e == pl.num_programs(0) - 1)
    def _():
        for k in range(RING_SLOTS):
            @pl.when(total - 1 - k >= 0)
            def _():
                wait(flush(total - 1 - k))


def _experts(pstart, cnt, total_chunks, xs_a, xs_b, wg, wu, wd):
    rows, hw = xs_a.shape
    r = EXPERT_ROWS
    n_exp, d, f = wg.shape
    hbm = pl.BlockSpec(memory_space=pl.ANY)
    grid_spec = pltpu.PrefetchScalarGridSpec(
        num_scalar_prefetch=3,
        grid=(n_exp,),
        in_specs=[pl.BlockSpec((1, d, f), lambda e, ps, cn, tot: (e, 0, 0)),
                  pl.BlockSpec((1, d, f), lambda e, ps, cn, tot: (e, 0, 0)),
                  pl.BlockSpec((1, f, d), lambda e, ps, cn, tot: (e, 0, 0)),
                  hbm, hbm],
        out_specs=(hbm, hbm),
        scratch_shapes=[pltpu.VMEM((d, f), BF16), pltpu.VMEM((d, f), BF16), pltpu.VMEM((f, d), BF16),
                        pltpu.VMEM((RING_SLOTS, r, hw), U32), pltpu.VMEM((RING_SLOTS, r, hw), U32),
                        pltpu.VMEM((RING_SLOTS, r, hw), U32), pltpu.VMEM((RING_SLOTS, r, hw), U32),
                        pltpu.SemaphoreType.DMA((2, RING_SLOTS)), pltpu.SemaphoreType.DMA((2, RING_SLOTS))],
    )
    return pl.pallas_call(
        _expert_kernel,
        out_shape=(jax.ShapeDtypeStruct((rows, hw), U32), jax.ShapeDtypeStruct((rows, hw), U32)),
        grid_spec=grid_spec,
        compiler_params=_params("arbitrary"),
        name="experts",
    )(pstart, cnt, total_chunks, wg, wu, wd, xs_a, xs_b)


def _sc_scatter_rows(x, idx_flat, n_rows):
    n, w = x.shape
    m = idx_flat.shape[1]
    nwin = n // SC_WINDOW
    mesh = plsc.VectorSubcoreMesh(core_axis_name="core", subcore_axis_name="subcore")

    @functools.partial(pl.kernel, out_type=jax.ShapeDtypeStruct((n_rows, w), x.dtype), mesh=mesh,
                       scratch_types=[])
    def scatter_kernel(x_hbm, i_hbm, o_hbm):
        def body(x_vmem, i_vmem):
            pltpu.sync_copy(x_vmem, o_hbm.at[i_vmem.at[0]])

        pltpu.emit_pipeline(
            body,
            grid=(m // SC_WINDOW,),
            in_specs=[pl.BlockSpec((SC_WINDOW, w), lambda i: (i % nwin, 0)),
                      pl.BlockSpec((1, SC_WINDOW), lambda i: (0, i))],
            out_specs=[],
            core_axis_name=("core", "subcore"),
            dimension_semantics=(pltpu.PARALLEL,),
        )(x_hbm, i_hbm)

    return scatter_kernel(x, idx_flat)


def _sc_gather_rows(src, idx_flat):
    m = idx_flat.shape[1]
    w = src.shape[1]
    mesh = plsc.VectorSubcoreMesh(core_axis_name="core", subcore_axis_name="subcore")

    @functools.partial(pl.kernel, out_type=jax.ShapeDtypeStruct((m, w), src.dtype), mesh=mesh)
    def gather_kernel(x_hbm, i_hbm, o_hbm):
        def body(i_vmem, o_vmem):
            pltpu.sync_copy(x_hbm.at[i_vmem.at[0]], o_vmem)

        pltpu.emit_pipeline(
            body,
            grid=(m // SC_WINDOW,),
            in_specs=[pl.BlockSpec((1, SC_WINDOW), lambda i: (0, i))],
            out_specs=[pl.BlockSpec((SC_WINDOW, w), lambda i: (i, 0))],
            core_axis_name=("core", "subcore"),
            dimension_semantics=(pltpu.PARALLEL,),
        )(i_hbm, o_hbm)

    return gather_kernel(src, idx_flat)


def _finish_kernel(gate_ref, xres_ref, mod_ref, ya_ref, yb_ref, o_ref, *, d, tiles_per_batch):
    tm = gate_ref.shape[1]
    b = pl.program_id(0) // tiles_per_batch
    gates = gate_ref[...].T
    acc = jnp.zeros((tm, d), F32)
    for k in range(TOP_K):
        packed = jnp.concatenate([ya_ref[k], yb_ref[k]], axis=1)
        acc = acc + _unpack_bf16_pairs(packed) * gates[:, k:k + 1]
    o_ref[...] = xres_ref[...] + mod_ref[pl.ds(b, 1), 5 * d:6 * d] * acc


def _finish(gates, xres, mod, yg_a, yg_b, *, tm, tiles_per_batch):
    n, d = xres.shape
    return pl.pallas_call(
        functools.partial(_finish_kernel, d=d, tiles_per_batch=tiles_per_batch),
        out_shape=jax.ShapeDtypeStruct((n, d), F32),
        grid=(n // tm,),
        in_specs=[pl.BlockSpec((TOP_K, tm), lambda i: (0, i)),
                  pl.BlockSpec((tm, d), lambda i: (i, 0)),
                  pl.BlockSpec(mod.shape, lambda i: (0, 0)),
                  pl.BlockSpec((TOP_K, tm, d // 4), lambda i: (0, i, 0)),
                  pl.BlockSpec((TOP_K, tm, d // 4), lambda i: (0, i, 0))],
        out_specs=pl.BlockSpec((tm, d), lambda i: (i, 0)),
        compiler_params=_params("arbitrary"),
        name="finish",
    )(gates, xres, mod, yg_a, yg_b)


_QA_HEAD_ORDER = (0, 4, 1, 5, 2, 6, 3, 7)


def _reorder_qa_heads(a, axis):
    return jnp.concatenate([lax.slice_in_dim(a, h * HEAD_DIM, (h + 1) * HEAD_DIM, axis=axis)
                            for h in _QA_HEAD_ORDER], axis=axis)


def _rope_swap(a):
    quarter = HEAD_DIM // 4
    lead = a.shape[:-1]
    return jnp.flip(a.reshape(lead + (-1, 2, quarter)), axis=-2).reshape(a.shape)


def _rope_tables(s):
    quarter = HEAD_DIM // 4
    t = jnp.arange(s)
    row = (t // GRID_W).astype(F32)
    col = (t % GRID_W).astype(F32)
    freqs = ROPE_BASE ** (-jnp.arange(quarter, dtype=F32) / quarter)
    ar = row[:, None] * freqs[None, :]
    ac = col[:, None] * freqs[None, :]
    cos = jnp.concatenate([jnp.cos(ar), jnp.cos(ar), jnp.cos(ac), jnp.cos(ac)], axis=1)
    sin = jnp.concatenate([-jnp.sin(ar), jnp.sin(ar), -jnp.sin(ac), jnp.sin(ac)], axis=1)
    return jnp.tile(cos, (1, 2)), jnp.tile(sin, (1, 2))


def kernel(x, c, ctx, c_ctx, w_ada, b_ada, norm1, norm2, w_in, q_norm_a, k_norm_a, q_norm_b, k_norm_b, sink_a,
           rpb_b, out_norm_a, out_norm_b, w_out, w_router, router_bias, we_gate, we_up, we_down, ws_gate, ws_up,
           ws_down):
    assert w_ada.shape[0] == 1, "single-layer block"
    bsz, s, d = x.shape
    lc = ctx.shape[1]
    n = bsz * s
    rows = s // GRID_W
    assert s % (NA_QROWS * GRID_W) == 0 and rows >= NA_KROWS and bsz <= 4 and d == 1024

    c8 = jnp.concatenate([c, c_ctx[None, :], jnp.zeros((8 - bsz - 1, d), F32)], axis=0)
    mod = _ada(c8, w_ada[0], b_ada[0][None, :])

    w = w_in[0]
    aw, akw, bw = A_Q_HEADS * HEAD_DIM, A_KV_HEADS * HEAD_DIM, B_HEADS * HEAD_DIM
    cuts = np.cumsum([0, aw, akw, akw, bw, bw, bw])
    qa_w, ka_w, va_w, qb_w, kb_w, vb_w = [w[:, cuts[i]:cuts[i + 1]] for i in range(6)]
    qa_w = _reorder_qa_heads(qa_w, 1)
    w_ext = jnp.concatenate([qa_w, qb_w, kb_w, vb_w, ka_w, va_w, _rope_swap(qa_w), _rope_swap(ka_w)],
                            axis=1).astype(BF16)
    scale = HEAD_DIM ** -0.5 * LOG2E
    pair = lambda g: jnp.tile(g, 2)
    gains = jnp.stack([pair(q_norm_a[0]) * scale, pair(q_norm_b[0]) * scale, pair(k_norm_b[0]), pair(k_norm_a[0]),
                       pair(_rope_swap(q_norm_a[0])) * scale, pair(_rope_swap(k_norm_a[0])),
                       jnp.zeros((LANES,), F32), jnp.zeros((LANES,), F32)])
    bd = jnp.asarray(np.kron(np.eye(2), np.full((HEAD_DIM, HEAD_DIM), 1.0 / HEAD_DIM)), BF16)
    cos_t, sin_t = _rope_tables(s)
    n1 = norm1[0][None, :]
    zq = _inproj(x, mod, n1, w_ext, gains, cos_t, sin_t, bd, mod_row=None, tm=512)
    zc = _inproj(ctx, mod, n1, w_ext, gains, jnp.ones((lc, LANES), F32), jnp.zeros((lc, LANES), F32), bd,
                 mod_row=bsz, tm=lc)

    sink = sink_a[0].astype(F32)
    sink_rep = jnp.concatenate([jnp.broadcast_to(sink[h] * LOG2E, (A_WINDOW, LANES)) for h in _QA_HEAD_ORDER])
    o_a = _attn_a(zq, zc, sink_rep, _attn_a_mask(A_WINDOW))
    o_b = _attn_b(zq, zc, _na_table(rpb_b[0], rows))

    ga = _reorder_qa_heads(out_norm_a[0], 0)[None, :]
    gb = out_norm_b[0][None, :]
    wo = jnp.concatenate([_reorder_qa_heads(w_out[0][:aw], 0), w_out[0][aw:]], axis=0).astype(BF16)
    wr_hi, wr_lo = _split_bf16(w_router[0].T)
    xres, hp_a, hp_b, idx, gates = _merge(o_a, o_b, x, mod, ga, gb, wo, norm2[0][None, :], wr_hi, wr_lo,
                                  router_bias[0][:, None], ws_gate[0].astype(BF16), ws_up[0].astype(BF16),
                                  ws_down[0].astype(BF16), tm=512)

    rank, counts = _rank(idx, tm=512)
    t = EXPERT_ROWS
    cnt = counts[:, 0].astype(I32)
    padded = (cnt + t - 1) // t * t
    pends = jnp.cumsum(padded)
    pstart = pends - padded
    n_rows = n * TOP_K + N_EXPERTS * t
    pos = _pos(idx, rank, pstart.astype(F32)[:, None], tm=512)

    pos_flat = pos.reshape(1, TOP_K * n)
    xs_a = _sc_scatter_rows(hp_a, pos_flat, n_rows)
    xs_b = _sc_scatter_rows(hp_b, pos_flat, n_rows)
    ys_a, ys_b = _experts(pstart, cnt, pends[-1:] // t, xs_a, xs_b, we_gate[0], we_up[0], we_down[0])
    yg_a = _sc_gather_rows(ys_a, pos_flat).reshape(TOP_K, n, d // 4)
    yg_b = _sc_gather_rows(ys_b, pos_flat).reshape(TOP_K, n, d // 4)
    out = _finish(gates, xres.reshape(n, d), mod, yg_a, yg_b, tm=256, tiles_per_batch=s // 256)
    return out.reshape(bsz, s, d)
```

```python
import functools

import numpy as np
import jax
import jax.numpy as jnp
from jax import lax
from jax.experimental import pallas as pl
from jax.experimental.pallas import tpu as pltpu
from jax.experimental.pallas import tpu_sc as plsc

F32 = jnp.float32
BF16 = jnp.bfloat16
I32 = jnp.int32
U32 = jnp.uint32

LANES = 128
HEAD_DIM = 64
HEAD_PAIR = 2 * HEAD_DIM
GRID_W = 64
A_Q_HEADS = 8
A_KV_HEADS = 2
A_WINDOW = 128
B_HEADS = 8
NA_KH = 8
NA_KW = 16
NA_QROWS = 4
NA_KROWS = NA_QROWS + NA_KH
ROPE_BASE = 10000.0
N_EXPERTS = 256
TOP_K = 8
N_GROUPS = 8
TOPK_GROUPS = 4
ROUTED_SCALE = 2.5
LOG2E = 1.4426950408889634
EPS = 1e-6
NEG_INF = -1e30
EXPERT_ROWS = 256
SC_WINDOW = 128
VMEM_LIMIT = 56 * 1024 * 1024

_NT = (((1,), (1,)), ((), ()))


def _params(*sem):
    return pltpu.CompilerParams(dimension_semantics=sem, vmem_limit_bytes=VMEM_LIMIT)


def _silu(v):
    return v * jax.nn.sigmoid(v)


def _rms(v, gain):
    return v * lax.rsqrt(jnp.mean(v * v, axis=-1, keepdims=True) + EPS) * gain


def _pack_bf16_pairs(v):
    n = v.shape[1] // 2
    lo = lax.bitcast_convert_type(v[:, :n].astype(BF16).astype(F32), U32) >> 16
    hi = lax.bitcast_convert_type(v[:, n:].astype(BF16).astype(F32), U32) & jnp.uint32(0xFFFF0000)
    return hi | lo


def _split_bf16(v):
    hi = lax.bitcast_convert_type(lax.bitcast_convert_type(v, U32) & jnp.uint32(0xFFFF0000), F32)
    return hi.astype(BF16), (v - hi).astype(BF16)


def _unpack_bf16_pairs(w):
    lo = lax.bitcast_convert_type(w << 16, F32)
    hi = lax.bitcast_convert_type(w & jnp.uint32(0xFFFF0000), F32)
    return jnp.concatenate([lo, hi], axis=1)


def _ada_kernel(c_ref, w_ref, b_ref, o_ref):
    a = _silu(c_ref[...])
    o_ref[...] = jnp.dot(a, w_ref[...], preferred_element_type=F32,
                         precision=lax.Precision.HIGHEST) + b_ref[...]


def _ada(c8, w, b):
    d, n = w.shape
    bn = n // 4
    return pl.pallas_call(
        _ada_kernel,
        out_shape=jax.ShapeDtypeStruct((8, n), F32),
        grid=(n // bn,),
        in_specs=[pl.BlockSpec((8, d), lambda j: (0, 0)),
                  pl.BlockSpec((d, bn), lambda j: (0, j)),
                  pl.BlockSpec((1, bn), lambda j: (0, j))],
        out_specs=pl.BlockSpec((8, bn), lambda j: (0, j)),
        compiler_params=_params("arbitrary"),
        name="ada",
    )(c8, w, b)


_QA, _QB, _KB, _VB, _KA, _VA, _QA_SW, _KA_SW = 0, 4, 8, 12, 16, 17, 18, 22
_OUT_BLOCKS = 18
_EXT_BLOCKS = 23


def _inproj_kernel(x_ref, mod_ref, n1_ref, w_ref, g_ref, cos_ref, sin_ref, bd_ref, o_ref, *, mod_row, d):
    b = pl.program_id(0) if mod_row is None else mod_row
    xn = _rms(x_ref[0], n1_ref[...])
    sh = mod_ref[pl.ds(b, 1), 0:d]
    sc = mod_ref[pl.ds(b, 1), d:2 * d]
    h = (xn * (1.0 + sc) + sh).astype(BF16)
    z = jnp.dot(h, w_ref[...], preferred_element_type=F32)
    bd = bd_ref[...]
    cos = cos_ref[...]
    sin = sin_ref[...]

    def blk(j):
        return z[:, j * LANES:(j + 1) * LANES]

    def head_rinv(zb):
        ms = jnp.dot((zb * zb).astype(BF16), bd, preferred_element_type=F32)
        return lax.rsqrt(ms + EPS)

    def put(j, v):
        o_ref[0, :, j * LANES:(j + 1) * LANES] = v.astype(BF16)

    def roped(j, j_sw, g_row, g_sw_row):
        zb = blk(j)
        r = head_rinv(zb)
        put(j, (zb * r * g_ref[g_row:g_row + 1, :]) * cos + (blk(j_sw) * r * g_ref[g_sw_row:g_sw_row + 1, :]) * sin)

    def normed(j, g_row):
        zb = blk(j)
        put(j, zb * head_rinv(zb) * g_ref[g_row:g_row + 1, :])

    for j in range(4):
        roped(_QA + j, _QA_SW + j, 0, 4)
        normed(_QB + j, 1)
        normed(_KB + j, 2)
        put(_VB + j, blk(_VB + j))
    roped(_KA, _KA_SW, 3, 5)
    put(_VA, blk(_VA))


def _inproj(x, mod, n1, w_ext, gains, cos_t, sin_t, bd, *, mod_row, tm):
    bsz, s, d = x.shape
    kern = functools.partial(_inproj_kernel, mod_row=mod_row, d=d)
    return pl.pallas_call(
        kern,
        out_shape=jax.ShapeDtypeStruct((bsz, s, _OUT_BLOCKS * LANES), BF16),
        grid=(bsz, s // tm),
        in_specs=[pl.BlockSpec((1, tm, d), lambda b, i: (b, i, 0)),
                  pl.BlockSpec(mod.shape, lambda b, i: (0, 0)),
                  pl.BlockSpec((1, d), lambda b, i: (0, 0)),
                  pl.BlockSpec(w_ext.shape, lambda b, i: (0, 0)),
                  pl.BlockSpec(gains.shape, lambda b, i: (0, 0)),
                  pl.BlockSpec((tm, LANES), lambda b, i: (i, 0)),
                  pl.BlockSpec((tm, LANES), lambda b, i: (i, 0)),
                  pl.BlockSpec((LANES, LANES), lambda b, i: (0, 0))],
        out_specs=pl.BlockSpec((1, tm, _OUT_BLOCKS * LANES), lambda b, i: (b, i, 0)),
        compiler_params=_params("arbitrary", "arbitrary"),
        name="inproj",
    )(x, mod, n1, w_ext, gains, cos_t, sin_t, bd)


def _split_pair(qp, lo):
    zero = jnp.zeros_like(qp)
    return jnp.concatenate([jnp.where(lo, qp, zero), jnp.where(lo, zero, qp)], axis=0)


def _softmax_pv(s_parts, v_parts, sink_rep):
    chunks = [s[:, c * LANES:(c + 1) * LANES] for s in s_parts for c in range(s.shape[1] // LANES)]
    m = jnp.max(functools.reduce(jnp.maximum, chunks), axis=-1, keepdims=True)
    m_rep = jnp.broadcast_to(m, (m.shape[0], LANES))
    if sink_rep is not None:
        m_rep = jnp.maximum(m_rep, sink_rep)
    acc = None
    for s, v in zip(s_parts, v_parts):
        p = jnp.concatenate([jnp.exp2(s[:, c * LANES:(c + 1) * LANES] - m_rep)
                             for c in range(s.shape[1] // LANES)], axis=1).astype(BF16)
        v_ext = jnp.concatenate([v, jnp.ones_like(v)], axis=1)
        o = jnp.dot(p, v_ext, preferred_element_type=F32)
        acc = o if acc is None else acc + o
    l_rep = acc[:, LANES:]
    if sink_rep is not None:
        l_rep = l_rep + jnp.exp2(sink_rep - m_rep)
    return acc[:, :LANES] * (1.0 / l_rep)


def _attn_a_kernel(q_ref, kp_ref, kc_ref, kn_ref, vp_ref, vc_ref, vn_ref, kx_ref, vx_ref, sink_ref, mask_ref,
                   o_ref):
    tq = q_ref.shape[1]
    lo = lax.broadcasted_iota(I32, (tq, LANES), 1) < HEAD_DIM
    q = q_ref[0]
    qs = jnp.concatenate([_split_pair(q[:, p * LANES:(p + 1) * LANES], lo) for p in range(4)], axis=0)
    k = jnp.concatenate([kp_ref[0], kc_ref[0], kn_ref[0]], axis=0)
    v = jnp.concatenate([vp_ref[0], vc_ref[0], vn_ref[0]], axis=0)
    s_loc = lax.dot_general(qs, k, _NT, preferred_element_type=F32) + mask_ref[0]
    s_ctx = lax.dot_general(qs, kx_ref[0], _NT, preferred_element_type=F32)
    o = _softmax_pv([s_loc, s_ctx], [v, vx_ref[0]], sink_ref[...])
    for p in range(4):
        o_lo = o[(2 * p) * tq:(2 * p + 1) * tq]
        o_hi = o[(2 * p + 1) * tq:(2 * p + 2) * tq]
        o_ref[0, :, p * LANES:(p + 1) * LANES] = jnp.where(lo, o_lo, o_hi).astype(BF16)


def _attn_a(zq, zc, sink_rep, mask):
    bsz, s, _ = zq.shape
    lc = zc.shape[1]
    tq = A_WINDOW
    nblk = s // tq
    ka, va = _KA, _VA

    def kv_spec(col, shift):
        return pl.BlockSpec((1, tq, LANES), lambda b, j: (b, jnp.clip(j + shift, 0, nblk - 1), col))

    def variant(j):
        return jnp.where(j == 0, 0, jnp.where(j == nblk - 1, 2, 1))

    return pl.pallas_call(
        _attn_a_kernel,
        out_shape=jax.ShapeDtypeStruct((bsz, s, 4 * LANES), BF16),
        grid=(bsz, nblk),
        in_specs=[pl.BlockSpec((1, tq, 4 * LANES), lambda b, j: (b, j, 0)),
                  kv_spec(ka, -1), kv_spec(ka, 0), kv_spec(ka, 1),
                  kv_spec(va, -1), kv_spec(va, 0), kv_spec(va, 1),
                  pl.BlockSpec((1, lc, LANES), lambda b, j: (b, 0, ka)),
                  pl.BlockSpec((1, lc, LANES), lambda b, j: (b, 0, va)),
                  pl.BlockSpec(sink_rep.shape, lambda b, j: (0, 0)),
                  pl.BlockSpec((1,) + mask.shape[1:], lambda b, j: (variant(j), 0, 0))],
        out_specs=pl.BlockSpec((1, tq, 4 * LANES), lambda b, j: (b, j, 0)),
        compiler_params=_params("arbitrary", "arbitrary"),
        name="attn_a",
    )(zq, zq, zq, zq, zq, zq, zq, zc, zc, sink_rep, mask)


def _attn_a_mask(tq):
    qi = np.arange(tq)[:, None]
    kj = np.arange(3 * tq)[None, :]
    ok = (kj >= qi) & (kj <= qi + 2 * tq)
    variants = [ok & (kj >= tq), ok, ok & (kj < 2 * tq)]
    m = np.stack([np.tile(np.where(v, 0.0, NEG_INF).astype(np.float32), (A_Q_HEADS, 1)) for v in variants])
    return jnp.asarray(m)


def _attn_b_kernel(q_ref, k0_ref, k1_ref, k2_ref, v0_ref, v1_ref, v2_ref, kx_ref, vx_ref, tab_ref, o_ref):
    tq = q_ref.shape[1]
    lo = lax.broadcasted_iota(I32, (tq, LANES), 1) < HEAD_DIM
    for p in range(4):
        sl = slice(p * LANES, (p + 1) * LANES)
        qs = _split_pair(q_ref[0, :, sl], lo)
        k = jnp.concatenate([k0_ref[0, :, sl], k1_ref[0, :, sl], k2_ref[0, :, sl]], axis=0)
        v = jnp.concatenate([v0_ref[0, :, sl], v1_ref[0, :, sl], v2_ref[0, :, sl]], axis=0)
        s_loc = lax.dot_general(qs, k, _NT, preferred_element_type=F32) + tab_ref[0, p]
        s_ctx = lax.dot_general(qs, kx_ref[0, :, sl], _NT, preferred_element_type=F32)
        o = _softmax_pv([s_loc, s_ctx], [v, vx_ref[0, :, sl]], None)
        o_ref[0, :, sl] = jnp.where(lo, o[:tq], o[tq:]).astype(BF16)


def _attn_b(zq, zc, table):
    bsz, s, _ = zq.shape
    lc = zc.shape[1]
    tq = NA_QROWS * GRID_W
    ng = s // tq
    qb, kb, vb = _QB // 4, _KB // 4, _VB // 4

    def kv_spec(col, off):
        return pl.BlockSpec((1, tq, 4 * LANES), lambda i, b: (b, jnp.clip(i - 1, 0, ng - 3) + off, col))

    def variant(i):
        return jnp.where(i == 0, 0, jnp.where(i == ng - 1, 2, 1))

    return pl.pallas_call(
        _attn_b_kernel,
        out_shape=jax.ShapeDtypeStruct((bsz, s, 4 * LANES), BF16),
        grid=(ng, bsz),
        in_specs=[pl.BlockSpec((1, tq, 4 * LANES), lambda i, b: (b, i, qb)),
                  kv_spec(kb, 0), kv_spec(kb, 1), kv_spec(kb, 2),
                  kv_spec(vb, 0), kv_spec(vb, 1), kv_spec(vb, 2),
                  pl.BlockSpec((1, lc, 4 * LANES), lambda i, b: (b, 0, kb)),
                  pl.BlockSpec((1, lc, 4 * LANES), lambda i, b: (b, 0, vb)),
                  pl.BlockSpec((1,) + table.shape[1:], lambda i, b: (variant(i), 0, 0, 0))],
        out_specs=pl.BlockSpec((1, tq, 4 * LANES), lambda i, b: (b, i, 0)),
        compiler_params=_params("arbitrary", "arbitrary"),
        name="attn_b",
    )(zq, zq, zq, zq, zq, zq, zq, zc, zc, table)


def _na_table(rpb, rows):
    ng = rows // NA_QROWS
    nq, nk = NA_QROWS * GRID_W, NA_KROWS * GRID_W
    qc = np.arange(GRID_W)
    kc = np.arange(GRID_W)
    ws = np.clip(qc - NA_KW // 2, 0, GRID_W - NA_KW)
    valid_c = (kc[None, :] >= ws[:, None]) & (kc[None, :] < ws[:, None] + NA_KW)
    dc = np.clip(kc[None, :] - qc[:, None], -(NA_KW - 1), NA_KW - 1) + (NA_KW - 1)
    c_sel = (dc[..., None] == np.arange(2 * NA_KW - 1)) & valid_c[..., None]
    tiles = jnp.einsum('hab,uvb->hauv', rpb.astype(F32), jnp.asarray(c_sel, F32), precision=lax.Precision.HIGHEST)
    tiles = jnp.where(jnp.asarray(valid_c)[None, None], tiles * LOG2E, NEG_INF)
    masked = jnp.full((B_HEADS, GRID_W, GRID_W), NEG_INF, F32)
    tabs = []
    for i in (0, 1, ng - 1):
        start = int(np.clip(NA_QROWS * i - NA_KH // 2, 0, rows - NA_KROWS))
        q_rows = []
        for qr in range(NA_QROWS):
            r = NA_QROWS * i + qr
            rs = int(np.clip(r - NA_KH // 2, 0, rows - NA_KH))
            k_tiles = []
            for kr in range(NA_KROWS):
                krow = start + kr
                k_tiles.append(tiles[:, krow - r + (NA_KH - 1)] if rs <= krow < rs + NA_KH else masked)
            q_rows.append(jnp.concatenate(k_tiles, axis=-1))
        tabs.append(jnp.concatenate(q_rows, axis=-2))
    return jnp.stack(tabs).reshape(3, B_HEADS // 2, 2 * nq, nk)


def _merge_kernel(oa_ref, ob_ref, x_ref, mod_ref, ga_ref, gb_ref, wo_ref, n2_ref, wrh_ref, wrl_ref, rb_ref,
                  wsg_ref, wsu_ref, wsd_ref, xres_ref, hpa_ref, hpb_ref, idx_ref, gate_ref, *, d):
    b = pl.program_id(0)
    tm = x_ref.shape[1]

    def mod(k):
        return mod_ref[pl.ds(b, 1), k * d:(k + 1) * d]

    na = _rms(oa_ref[0].astype(F32), ga_ref[...])
    nb = _rms(ob_ref[0].astype(F32), gb_ref[...])
    cat = jnp.concatenate([na, nb], axis=1).astype(BF16)
    y = jnp.dot(cat, wo_ref[...], preferred_element_type=F32)
    x1 = x_ref[0] + mod(2) * y
    h2 = _rms(x1, n2_ref[...]) * (1.0 + mod(4)) + mod(3)

    hb = h2.astype(BF16)
    act = _silu(jnp.dot(hb, wsg_ref[...], preferred_element_type=F32)) * jnp.dot(hb, wsu_ref[...],
                                                                                 preferred_element_type=F32)
    shared = jnp.dot(act.astype(BF16), wsd_ref[...], preferred_element_type=F32)
    xres_ref[0] = x1 + mod(5) * shared
    packed = _pack_bf16_pairs(h2)
    hpa_ref[...] = packed[:, :d // 4]
    hpb_ref[...] = packed[:, d // 4:]

    h_hi, h_lo = _split_bf16(h2)
    logits = (lax.dot_general(wrh_ref[...], h_hi, _NT, preferred_element_type=F32)
              + lax.dot_general(wrh_ref[...], h_lo, _NT, preferred_element_type=F32)
              + lax.dot_general(wrl_ref[...], h_hi, _NT, preferred_element_type=F32))
    scores = jax.nn.sigmoid(logits)
    sel = scores + rb_ref[...]
    per = N_EXPERTS // N_GROUPS
    g3 = sel.reshape(N_GROUPS, per, tm)
    it3 = lax.broadcasted_iota(I32, (N_GROUPS, per, tm), 1)
    m1 = jnp.max(g3, axis=1, keepdims=True)
    first = jnp.min(jnp.where(g3 == m1, it3, per), axis=1, keepdims=True)
    m2 = jnp.max(jnp.where(it3 == first, -jnp.inf, g3), axis=1, keepdims=True)
    gscore = (m1 + m2).reshape(N_GROUPS, tm)

    itg = lax.broadcasted_iota(I32, (N_GROUPS, tm), 0)
    gsel = jnp.zeros((N_GROUPS, tm), F32)
    cur = gscore
    for _ in range(TOPK_GROUPS):
        mx = jnp.max(cur, axis=0, keepdims=True)
        fi = jnp.min(jnp.where(cur == mx, itg, N_GROUPS), axis=0, keepdims=True)
        pick = itg == fi
        gsel = jnp.where(pick, 1.0, gsel)
        cur = jnp.where(pick, -jnp.inf, cur)
    emask = jnp.broadcast_to(gsel.reshape(N_GROUPS, 1, tm), (N_GROUPS, per, tm)).reshape(N_EXPERTS, tm) > 0.5

    ite = lax.broadcasted_iota(I32, (N_EXPERTS, tm), 0)
    cur = jnp.where(emask, sel, NEG_INF)
    idx_rows, s_rows = [], []
    for _ in range(TOP_K):
        mx = jnp.max(cur, axis=0, keepdims=True)
        fi = jnp.min(jnp.where(cur == mx, ite, N_EXPERTS), axis=0, keepdims=True)
        pick = ite == fi
        idx_rows.append(fi)
        s_rows.append(jnp.sum(jnp.where(pick, scores, 0.0), axis=0, keepdims=True))
        cur = jnp.where(pick, -jnp.inf, cur)
    top_s = jnp.concatenate(s_rows, axis=0)
    idx_ref[...] = jnp.concatenate(idx_rows, axis=0)
    gate_ref[...] = top_s / jnp.sum(top_s, axis=0, keepdims=True) * ROUTED_SCALE


def _merge(o_a, o_b, x, mod, ga, gb, wo, n2, wr_hi, wr_lo, rb_col, wsg, wsu, wsd, *, tm):
    bsz, s, d = x.shape
    nt = s // tm
    n = bsz * s
    full = lambda a: pl.BlockSpec(a.shape, lambda b, i: (0,) * a.ndim)
    return pl.pallas_call(
        functools.partial(_merge_kernel, d=d),
        out_shape=(jax.ShapeDtypeStruct((bsz, s, d), F32),
                   jax.ShapeDtypeStruct((n, d // 4), U32),
                   jax.ShapeDtypeStruct((n, d // 4), U32),
                   jax.ShapeDtypeStruct((TOP_K, n), I32),
                   jax.ShapeDtypeStruct((TOP_K, n), F32)),
        grid=(bsz, nt),
        in_specs=[pl.BlockSpec((1, tm, d // 2), lambda b, i: (b, i, 0)),
                  pl.BlockSpec((1, tm, d // 2), lambda b, i: (b, i, 0)),
                  pl.BlockSpec((1, tm, d), lambda b, i: (b, i, 0)),
                  full(mod), full(ga), full(gb), full(wo), full(n2), full(wr_hi), full(wr_lo), full(rb_col),
                  full(wsg), full(wsu), full(wsd)],
        out_specs=(pl.BlockSpec((1, tm, d), lambda b, i: (b, i, 0)),
                   pl.BlockSpec((tm, d // 4), lambda b, i: (b * nt + i, 0)),
                   pl.BlockSpec((tm, d // 4), lambda b, i: (b * nt + i, 0)),
                   pl.BlockSpec((TOP_K, tm), lambda b, i: (0, b * nt + i)),
                   pl.BlockSpec((TOP_K, tm), lambda b, i: (0, b * nt + i))),
        compiler_params=_params("arbitrary", "arbitrary"),
        name="merge",
    )(o_a, o_b, x, mod, ga, gb, wo, n2, wr_hi, wr_lo, rb_col, wsg, wsu, wsd)


def _rank_kernel(idx_ref, rank_ref, cnt_ref, carry_ref):
    tm = idx_ref.shape[1]

    @pl.when(pl.program_id(0) == 0)
    def _():
        carry_ref[...] = jnp.zeros_like(carry_ref)

    idx = idx_ref[...]
    ite = lax.broadcasted_iota(I32, (N_EXPERTS, tm), 0)
    before = (lax.broadcasted_iota(I32, (tm, tm), 0) < lax.broadcasted_iota(I32, (tm, tm), 1)).astype(BF16)
    base = carry_ref[...]
    rows = []
    for k in range(TOP_K):
        oh = ite == idx[k:k + 1, :]
        ohf = jnp.where(oh, 1.0, 0.0)
        cum = jnp.dot(ohf.astype(BF16), before, preferred_element_type=F32)
        rows.append(jnp.sum(jnp.where(oh, cum + base, 0.0), axis=0, keepdims=True))
        base = base + jnp.sum(ohf, axis=1, keepdims=True)
    rank_ref[...] = jnp.concatenate(rows, axis=0).astype(I32)
    carry_ref[...] = base
    cnt_ref[...] = base


def _rank(idx, *, tm):
    n = idx.shape[1]
    return pl.pallas_call(
        _rank_kernel,
        out_shape=(jax.ShapeDtypeStruct((TOP_K, n), I32), jax.ShapeDtypeStruct((N_EXPERTS, 1), F32)),
        grid=(n // tm,),
        in_specs=[pl.BlockSpec((TOP_K, tm), lambda i: (0, i))],
        out_specs=(pl.BlockSpec((TOP_K, tm), lambda i: (0, i)), pl.BlockSpec((N_EXPERTS, 1), lambda i: (0, 0))),
        scratch_shapes=[pltpu.VMEM((N_EXPERTS, 1), F32)],
        compiler_params=_params("arbitrary"),
        name="rank",
    )(idx)


def _pos_kernel(idx_ref, rank_ref, pstart_ref, pos_ref):
    tm = idx_ref.shape[1]
    idx = idx_ref[...]
    ite = lax.broadcasted_iota(I32, (N_EXPERTS, tm), 0)
    pstart = pstart_ref[...]
    rows = [jnp.sum(jnp.where(ite == idx[k:k + 1, :], pstart, 0.0), axis=0, keepdims=True) for k in range(TOP_K)]
    pos_ref[...] = jnp.concatenate(rows, axis=0).astype(I32) + rank_ref[...]


def _pos(idx, rank, pstart_col, *, tm):
    n = idx.shape[1]
    return pl.pallas_call(
        _pos_kernel,
        out_shape=jax.ShapeDtypeStruct((TOP_K, n), I32),
        grid=(n // tm,),
        in_specs=[pl.BlockSpec((TOP_K, tm), lambda i: (0, i)),
                  pl.BlockSpec((TOP_K, tm), lambda i: (0, i)),
                  pl.BlockSpec((N_EXPERTS, 1), lambda i: (0, 0))],
        out_specs=pl.BlockSpec((TOP_K, tm), lambda i: (0, i)),
        compiler_params=_params("arbitrary"),
        name="pos",
    )(idx, rank, pstart_col)


UNIT_CHUNKS = (4, 2, 1)
RING_AHEAD = 4
RING_SLOTS = RING_AHEAD + UNIT_CHUNKS[0]


def _expert_kernel(ps_ref, cnt_ref, tot_ref, wg_ref, wu_ref, wd_ref, xa_hbm, xb_hbm, ya_hbm, yb_hbm,
                   wg_s, wu_s, wd_s, xa_buf, xb_buf, ya_buf, yb_buf, in_sem, out_sem):
    e = pl.program_id(0)
    r = EXPERT_ROWS
    ahead = RING_AHEAD
    total = tot_ref[0]
    first = ps_ref[e] // r
    cnt = cnt_ref[e]
    nch = (cnt + r - 1) // r

    def rows_of(g):
        return pl.ds(pl.multiple_of(g * r, r), r)

    def fetch(g):
        slot = g % RING_SLOTS
        return (pltpu.make_async_copy(xa_hbm.at[rows_of(g)], xa_buf.at[slot], in_sem.at[0, slot]),
                pltpu.make_async_copy(xb_hbm.at[rows_of(g)], xb_buf.at[slot], in_sem.at[1, slot]))

    def flush(g):
        slot = g % RING_SLOTS
        return (pltpu.make_async_copy(ya_buf.at[slot], ya_hbm.at[rows_of(g)], out_sem.at[0, slot]),
                pltpu.make_async_copy(yb_buf.at[slot], yb_hbm.at[rows_of(g)], out_sem.at[1, slot]))

    def start(copies):
        for cp in copies:
            cp.start()

    def wait(copies):
        for cp in copies:
            cp.wait()

    @pl.when(e == 0)
    def _():
        for g in range(ahead):
            @pl.when(g < total)
            def _():
                start(fetch(g))

    @pl.when(nch > 0)
    def _():
        wg_s[...] = wg_ref[0].astype(BF16)
        wu_s[...] = wu_ref[0].astype(BF16)
        wd_s[...] = wd_ref[0].astype(BF16)

        def unit(c0, u):
            for j in range(u):
                g = first + c0 + j
                wait(fetch(g))

                @pl.when(g + ahead < total)
                def _():
                    start(fetch(g + ahead))

                @pl.when(g >= RING_SLOTS)
                def _():
                    wait(flush(g - RING_SLOTS))

            slots = [(first + c0 + j) % RING_SLOTS for j in range(u)]
            packed = jnp.concatenate([jnp.concatenate([xa_buf[sl], xb_buf[sl]], axis=1) for sl in slots], axis=0)
            row = lax.broadcasted_iota(I32, packed.shape, 0)
            packed = jnp.where(row < cnt - c0 * r, packed, jnp.zeros_like(packed))
            x = _unpack_bf16_pairs(packed).astype(BF16)
            gate = jnp.dot(x, wg_s[...], preferred_element_type=F32)
            up = jnp.dot(x, wu_s[...], preferred_element_type=F32)
            y = jnp.dot((_silu(gate) * up).astype(BF16), wd_s[...], preferred_element_type=F32)
            out = _pack_bf16_pairs(y)
            half = out.shape[1] // 2
            for j, sl in enumerate(slots):
                ya_buf[sl] = out[j * r:(j + 1) * r, :half]
                yb_buf[sl] = out[j * r:(j + 1) * r, half:]
                start(flush(first + c0 + j))

        big = UNIT_CHUNKS[0]

        def big_unit(i, carry):
            unit(i * big, big)
            return carry

        lax.fori_loop(0, nch // big, big_unit, 0)
        done = nch // big * big
        for u in UNIT_CHUNKS[1:]:
            @pl.when((nch % (2 * u)) >= u)
            def _():
                unit(done, u)

            done = done + jnp.where((nch % (2 * u)) >= u, u, 0)

    @pl.when(e == pl.num_programs(0) - 1)
    def _():
        for k in range(RING_SLOTS):
            @pl.when(total - 1 - k >= 0)
            def _():
                wait(flush(total - 1 - k))


def _experts(pstart, cnt, total_chunks, xs_a, xs_b, wg, wu, wd):
    rows, hw = xs_a.shape
    r = EXPERT_ROWS
    n_exp, d, f = wg.shape
    hbm = pl.BlockSpec(memory_space=pl.ANY)
    grid_spec = pltpu.PrefetchScalarGridSpec(
        num_scalar_prefetch=3,
        grid=(n_exp,),
        in_specs=[pl.BlockSpec((1, d, f), lambda e, ps, cn, tot: (e, 0, 0)),
                  pl.BlockSpec((1, d, f), lambda e, ps, cn, tot: (e, 0, 0)),
                  pl.BlockSpec((1, f, d), lambda e, ps, cn, tot: (e, 0, 0)),
                  hbm, hbm],
        out_specs=(hbm, hbm),
        scratch_shapes=[pltpu.VMEM((d, f), BF16), pltpu.VMEM((d, f), BF16), pltpu.VMEM((f, d), BF16),
                        pltpu.VMEM((RING_SLOTS, r, hw), U32), pltpu.VMEM((RING_SLOTS, r, hw), U32),
                        pltpu.VMEM((RING_SLOTS, r, hw), U32), pltpu.VMEM((RING_SLOTS, r, hw), U32),
                        pltpu.SemaphoreType.DMA((2, RING_SLOTS)), pltpu.SemaphoreType.DMA((2, RING_SLOTS))],
    )
    return pl.pallas_call(
        _expert_kernel,
        out_shape=(jax.ShapeDtypeStruct((rows, hw), U32), jax.ShapeDtypeStruct((rows, hw), U32)),
        grid_spec=grid_spec,
        compiler_params=_params("arbitrary"),
        name="experts",
    )(pstart, cnt, total_chunks, wg, wu, wd, xs_a, xs_b)


def _sc_scatter_rows(x, idx_flat, n_rows):
    n, w = x.shape
    m = idx_flat.shape[1]
    nwin = n // SC_WINDOW
    reps = m // n
    mesh = plsc.VectorSubcoreMesh(core_axis_name="core", subcore_axis_name="subcore")

    @functools.partial(pl.kernel, out_type=jax.ShapeDtypeStruct((n_rows, w), x.dtype), mesh=mesh,
                       scratch_types=[])
    def scatter_kernel(x_hbm, i_hbm, o_hbm):
        def body(x_vmem, i_vmem):
            pltpu.sync_copy(x_vmem, o_hbm.at[i_vmem.at[0]])

        pltpu.emit_pipeline(
            body,
            grid=(nwin, reps),
            in_specs=[pl.BlockSpec((SC_WINDOW, w), lambda i, k: (i, 0)),
                      pl.BlockSpec((1, SC_WINDOW), lambda i, k: (0, k * nwin + i))],
            out_specs=[],
            core_axis_name=("core", "subcore"),
            dimension_semantics=(pltpu.PARALLEL, pltpu.ARBITRARY),
        )(x_hbm, i_hbm)

    return scatter_kernel(x, idx_flat)


def _sc_gather_rows(src, idx_flat):
    m = idx_flat.shape[1]
    w = src.shape[1]
    mesh = plsc.VectorSubcoreMesh(core_axis_name="core", subcore_axis_name="subcore")

    @functools.partial(pl.kernel, out_type=jax.ShapeDtypeStruct((m, w), src.dtype), mesh=mesh)
    def gather_kernel(x_hbm, i_hbm, o_hbm):
        def body(i_vmem, o_vmem):
            pltpu.sync_copy(x_hbm.at[i_vmem.at[0]], o_vmem)

        pltpu.emit_pipeline(
            body,
            grid=(m // SC_WINDOW,),
            in_specs=[pl.BlockSpec((1, SC_WINDOW), lambda i: (0, i))],
            out_specs=[pl.BlockSpec((SC_WINDOW, w), lambda i: (i, 0))],
            core_axis_name=("core", "subcore"),
            dimension_semantics=(pltpu.PARALLEL,),
        )(i_hbm, o_hbm)

    return gather_kernel(src, idx_flat)


def _finish_kernel(gate_ref, xres_ref, mod_ref, ya_ref, yb_ref, o_ref, *, d, tiles_per_batch):
    tm = gate_ref.shape[1]
    b = pl.program_id(0) // tiles_per_batch
    gates = gate_ref[...].T
    acc = jnp.zeros((tm, d), F32)
    for k in range(TOP_K):
        packed = jnp.concatenate([ya_ref[k], yb_ref[k]], axis=1)
        acc = acc + _unpack_bf16_pairs(packed) * gates[:, k:k + 1]
    o_ref[...] = xres_ref[...] + mod_ref[pl.ds(b, 1), 5 * d:6 * d] * acc


def _finish(gates, xres, mod, yg_a, yg_b, *, tm, tiles_per_batch):
    n, d = xres.shape
    return pl.pallas_call(
        functools.partial(_finish_kernel, d=d, tiles_per_batch=tiles_per_batch),
        out_shape=jax.ShapeDtypeStruct((n, d), F32),
        grid=(n // tm,),
        in_specs=[pl.BlockSpec((TOP_K, tm), lambda i: (0, i)),
                  pl.BlockSpec((tm, d), lambda i: (i, 0)),
                  pl.BlockSpec(mod.shape, lambda i: (0, 0)),
                  pl.BlockSpec((TOP_K, tm, d // 4), lambda i: (0, i, 0)),
                  pl.BlockSpec((TOP_K, tm, d // 4), lambda i: (0, i, 0))],
        out_specs=pl.BlockSpec((tm, d), lambda i: (i, 0)),
        compiler_params=_params("arbitrary"),
        name="finish",
    )(gates, xres, mod, yg_a, yg_b)


_QA_HEAD_ORDER = (0, 4, 1, 5, 2, 6, 3, 7)


def _reorder_qa_heads(a, axis):
    return jnp.concatenate([lax.slice_in_dim(a, h * HEAD_DIM, (h + 1) * HEAD_DIM, axis=axis)
                            for h in _QA_HEAD_ORDER], axis=axis)


def _rope_swap(a):
    quarter = HEAD_DIM // 4
    lead = a.shape[:-1]
    return jnp.flip(a.reshape(lead + (-1, 2, quarter)), axis=-2).reshape(a.shape)


def _rope_tables(s):
    quarter = HEAD_DIM // 4
    t = jnp.arange(s)
    row = (t // GRID_W).astype(F32)
    col = (t % GRID_W).astype(F32)
    freqs = ROPE_BASE ** (-jnp.arange(quarter, dtype=F32) / quarter)
    ar = row[:, None] * freqs[None, :]
    ac = col[:, None] * freqs[None, :]
    cos = jnp.concatenate([jnp.cos(ar), jnp.cos(ar), jnp.cos(ac), jnp.cos(ac)], axis=1)
    sin = jnp.concatenate([-jnp.sin(ar), jnp.sin(ar), -jnp.sin(ac), jnp.sin(ac)], axis=1)
    return jnp.tile(cos, (1, 2)), jnp.tile(sin, (1, 2))


def kernel(x, c, ctx, c_ctx, w_ada, b_ada, norm1, norm2, w_in, q_norm_a, k_norm_a, q_norm_b, k_norm_b, sink_a,
           rpb_b, out_norm_a, out_norm_b, w_out, w_router, router_bias, we_gate, we_up, we_down, ws_gate, ws_up,
           ws_down):
    assert w_ada.shape[0] == 1, "single-layer block"
    bsz, s, d = x.shape
    lc = ctx.shape[1]
    n = bsz * s
    rows = s // GRID_W
    assert s % (NA_QROWS * GRID_W) == 0 and rows >= NA_KROWS and bsz <= 4 and d == 1024

    c8 = jnp.concatenate([c, c_ctx[None, :], jnp.zeros((8 - bsz - 1, d), F32)], axis=0)
    mod = _ada(c8, w_ada[0], b_ada[0][None, :])

    w = w_in[0]
    aw, akw, bw = A_Q_HEADS * HEAD_DIM, A_KV_HEADS * HEAD_DIM, B_HEADS * HEAD_DIM
    cuts = np.cumsum([0, aw, akw, akw, bw, bw, bw])
    qa_w, ka_w, va_w, qb_w, kb_w, vb_w = [w[:, cuts[i]:cuts[i + 1]] for i in range(6)]
    qa_w = _reorder_qa_heads(qa_w, 1)
    w_ext = jnp.concatenate([qa_w, qb_w, kb_w, vb_w, ka_w, va_w, _rope_swap(qa_w), _rope_swap(ka_w)],
                            axis=1).astype(BF16)
    scale = HEAD_DIM ** -0.5 * LOG2E
    pair = lambda g: jnp.tile(g, 2)
    gains = jnp.stack([pair(q_norm_a[0]) * scale, pair(q_norm_b[0]) * scale, pair(k_norm_b[0]), pair(k_norm_a[0]),
                       pair(_rope_swap(q_norm_a[0])) * scale, pair(_rope_swap(k_norm_a[0])),
                       jnp.zeros((LANES,), F32), jnp.zeros((LANES,), F32)])
    bd = jnp.asarray(np.kron(np.eye(2), np.full((HEAD_DIM, HEAD_DIM), 1.0 / HEAD_DIM)), BF16)
    cos_t, sin_t = _rope_tables(s)
    n1 = norm1[0][None, :]
    zq = _inproj(x, mod, n1, w_ext, gains, cos_t, sin_t, bd, mod_row=None, tm=512)
    zc = _inproj(ctx, mod, n1, w_ext, gains, jnp.ones((lc, LANES), F32), jnp.zeros((lc, LANES), F32), bd,
                 mod_row=bsz, tm=lc)

    sink = sink_a[0].astype(F32)
    sink_rep = jnp.concatenate([jnp.broadcast_to(sink[h] * LOG2E, (A_WINDOW, LANES)) for h in _QA_HEAD_ORDER])
    o_a = _attn_a(zq, zc, sink_rep, _attn_a_mask(A_WINDOW))
    o_b = _attn_b(zq, zc, _na_table(rpb_b[0], rows))

    ga = _reorder_qa_heads(out_norm_a[0], 0)[None, :]
    gb = out_norm_b[0][None, :]
    wo = jnp.concatenate([_reorder_qa_heads(w_out[0][:aw], 0), w_out[0][aw:]], axis=0).astype(BF16)
    wr_hi, wr_lo = _split_bf16(w_router[0].T)
    xres, hp_a, hp_b, idx, gates = _merge(o_a, o_b, x, mod, ga, gb, wo, norm2[0][None, :], wr_hi, wr_lo,
                                  router_bias[0][:, None], ws_gate[0].astype(BF16), ws_up[0].astype(BF16),
                                  ws_down[0].astype(BF16), tm=512)

    rank, counts = _rank(idx, tm=512)
    t = EXPERT_ROWS
    cnt = counts[:, 0].astype(I32)
    padded = (cnt + t - 1) // t * t
    pends = jnp.cumsum(padded)
    pstart = pends - padded
    n_rows = n * TOP_K + N_EXPERTS * t
    pos = _pos(idx, rank, pstart.astype(F32)[:, None], tm=512)

    pos_flat = pos.reshape(1, TOP_K * n)
    xs_a = _sc_scatter_rows(hp_a, pos_flat, n_rows)
    xs_b = _sc_scatter_rows(hp_b, pos_flat, n_rows)
    ys_a, ys_b = _experts(pstart, cnt, pends[-1:] // t, xs_a, xs_b, we_gate[0], we_up[0], we_down[0])
    yg_a = _sc_gather_rows(ys_a, pos_flat).reshape(TOP_K, n, d // 4)
    yg_b = _sc_gather_rows(ys_b, pos_flat).reshape(TOP_K, n, d // 4)
    out = _finish(gates, xres.reshape(n, d), mod, yg_a, yg_b, tm=256, tiles_per_batch=s // 256)
    return out.reshape(bsz, s, d)
```

```python
import functools

import numpy as np
import jax
import jax.numpy as jnp
from jax import lax
from jax.experimental import pallas as pl
from jax.experimental.pallas import tpu as pltpu
from jax.experimental.pallas import tpu_sc as plsc

F32 = jnp.float32
BF16 = jnp.bfloat16
I32 = jnp.int32
U32 = jnp.uint32

LANES = 128
HEAD_DIM = 64
HEAD_PAIR = 2 * HEAD_DIM
GRID_W = 64
A_Q_HEADS = 8
A_KV_HEADS = 2
A_WINDOW = 128
B_HEADS = 8
NA_KH = 8
NA_KW = 16
NA_QROWS = 4
NA_KROWS = NA_QROWS + NA_KH
ROPE_BASE = 10000.0
N_EXPERTS = 256
TOP_K = 8
N_GROUPS = 8
TOPK_GROUPS = 4
ROUTED_SCALE = 2.5
LOG2E = 1.4426950408889634
EPS = 1e-6
NEG_INF = -1e30
EXPERT_ROWS = 256
SC_WINDOW = 128
VMEM_LIMIT = 56 * 1024 * 1024

_NT = (((1,), (1,)), ((), ()))


def _params(*sem):
    return pltpu.CompilerParams(dimension_semantics=sem, vmem_limit_bytes=VMEM_LIMIT)


def _silu(v):
    return v * jax.nn.sigmoid(v)


def _rms(v, gain):
    return v * lax.rsqrt(jnp.mean(v * v, axis=-1, keepdims=True) + EPS) * gain


def _pack_bf16_pairs(v):
    n = v.shape[1] // 2
    lo = lax.bitcast_convert_type(v[:, :n].astype(BF16).astype(F32), U32) >> 16
    hi = lax.bitcast_convert_type(v[:, n:].astype(BF16).astype(F32), U32) & jnp.uint32(0xFFFF0000)
    return hi | lo


def _split_bf16(v):
    hi = lax.bitcast_convert_type(lax.bitcast_convert_type(v, U32) & jnp.uint32(0xFFFF0000), F32)
    return hi.astype(BF16), (v - hi).astype(BF16)


def _unpack_bf16_pairs(w):
    lo = lax.bitcast_convert_type(w << 16, F32)
    hi = lax.bitcast_convert_type(w & jnp.uint32(0xFFFF0000), F32)
    return jnp.concatenate([lo, hi], axis=1)


def _ada_kernel(c_ref, w_ref, b_ref, o_ref):
    a = _silu(c_ref[...])
    o_ref[...] = jnp.dot(a, w_ref[...], preferred_element_type=F32,
                         precision=lax.Precision.HIGHEST) + b_ref[...]


def _ada(c8, w, b):
    d, n = w.shape
    bn = n // 4
    return pl.pallas_call(
        _ada_kernel,
        out_shape=jax.ShapeDtypeStruct((8, n), F32),
        grid=(n // bn,),
        in_specs=[pl.BlockSpec((8, d), lambda j: (0, 0)),
                  pl.BlockSpec((d, bn), lambda j: (0, j)),
                  pl.BlockSpec((1, bn), lambda j: (0, j))],
        out_specs=pl.BlockSpec((8, bn), lambda j: (0, j)),
        compiler_params=_params("arbitrary"),
        name="ada",
    )(c8, w, b)


_QA, _QB, _KB, _VB, _KA, _VA, _QA_SW, _KA_SW = 0, 4, 8, 12, 16, 17, 18, 22
_OUT_BLOCKS = 18
_EXT_BLOCKS = 23


def _inproj_kernel(x_ref, mod_ref, n1_ref, w_ref, g_ref, cos_ref, sin_ref, bd_ref, o_ref, *, mod_row, d):
    b = pl.program_id(0) if mod_row is None else mod_row
    xn = _rms(x_ref[0], n1_ref[...])
    sh = mod_ref[pl.ds(b, 1), 0:d]
    sc = mod_ref[pl.ds(b, 1), d:2 * d]
    h = (xn * (1.0 + sc) + sh).astype(BF16)
    z = jnp.dot(h, w_ref[...], preferred_element_type=F32)
    bd = bd_ref[...]
    cos = cos_ref[...]
    sin = sin_ref[...]

    def blk(j):
        return z[:, j * LANES:(j + 1) * LANES]

    def head_rinv(zb):
        ms = jnp.dot((zb * zb).astype(BF16), bd, preferred_element_type=F32)
        return lax.rsqrt(ms + EPS)

    def put(j, v):
        o_ref[0, :, j * LANES:(j + 1) * LANES] = v.astype(BF16)

    def roped(j, j_sw, g_row, g_sw_row):
        zb = blk(j)
        r = head_rinv(zb)
        put(j, (zb * r * g_ref[g_row:g_row + 1, :]) * cos + (blk(j_sw) * r * g_ref[g_sw_row:g_sw_row + 1, :]) * sin)

    def normed(j, g_row):
        zb = blk(j)
        put(j, zb * head_rinv(zb) * g_ref[g_row:g_row + 1, :])

    for j in range(4):
        roped(_QA + j, _QA_SW + j, 0, 4)
        normed(_QB + j, 1)
        normed(_KB + j, 2)
        put(_VB + j, blk(_VB + j))
    roped(_KA, _KA_SW, 3, 5)
    put(_VA, blk(_VA))


def _inproj(x, mod, n1, w_ext, gains, cos_t, sin_t, bd, *, mod_row, tm):
    bsz, s, d = x.shape
    kern = functools.partial(_inproj_kernel, mod_row=mod_row, d=d)
    return pl.pallas_call(
        kern,
        out_shape=jax.ShapeDtypeStruct((bsz, s, _OUT_BLOCKS * LANES), BF16),
        grid=(bsz, s // tm),
        in_specs=[pl.BlockSpec((1, tm, d), lambda b, i: (b, i, 0)),
                  pl.BlockSpec(mod.shape, lambda b, i: (0, 0)),
                  pl.BlockSpec((1, d), lambda b, i: (0, 0)),
                  pl.BlockSpec(w_ext.shape, lambda b, i: (0, 0)),
                  pl.BlockSpec(gains.shape, lambda b, i: (0, 0)),
                  pl.BlockSpec((tm, LANES), lambda b, i: (i, 0)),
                  pl.BlockSpec((tm, LANES), lambda b, i: (i, 0)),
                  pl.BlockSpec(bd.shape, lambda b, i: (0, 0))],
        out_specs=pl.BlockSpec((1, tm, _OUT_BLOCKS * LANES), lambda b, i: (b, i, 0)),
        compiler_params=_params("arbitrary", "arbitrary"),
        name="inproj",
    )(x, mod, n1, w_ext, gains, cos_t, sin_t, bd)


def _split_pair(qp, lo):
    zero = jnp.zeros_like(qp)
    return jnp.concatenate([jnp.where(lo, qp, zero), jnp.where(lo, zero, qp)], axis=0)


def _softmax_pv(s_parts, v_parts, sink_rep):
    chunks = [s[:, c * LANES:(c + 1) * LANES] for s in s_parts for c in range(s.shape[1] // LANES)]
    m = jnp.max(functools.reduce(jnp.maximum, chunks), axis=-1, keepdims=True)
    m_rep = jnp.broadcast_to(m, (m.shape[0], LANES))
    if sink_rep is not None:
        m_rep = jnp.maximum(m_rep, sink_rep)
    acc = None
    for s, v in zip(s_parts, v_parts):
        p = jnp.concatenate([jnp.exp2(s[:, c * LANES:(c + 1) * LANES] - m_rep)
                             for c in range(s.shape[1] // LANES)], axis=1).astype(BF16)
        v_ext = jnp.concatenate([v, jnp.ones_like(v)], axis=1)
        o = jnp.dot(p, v_ext, preferred_element_type=F32)
        acc = o if acc is None else acc + o
    l_rep = acc[:, LANES:]
    if sink_rep is not None:
        l_rep = l_rep + jnp.exp2(sink_rep - m_rep)
    return acc[:, :LANES] * (1.0 / l_rep)


A_BLOCKS_PER_STEP = 4


def _attn_a_kernel(q_ref, *refs):
    nq = A_BLOCKS_PER_STEP
    k_refs, v_refs = refs[:nq + 2], refs[nq + 2:2 * nq + 4]
    kx_ref, vx_ref, sink_ref = refs[2 * nq + 4:2 * nq + 7]
    mask_refs = refs[2 * nq + 7:3 * nq + 7]
    o_ref = refs[-1]
    tq = A_WINDOW
    lo = lax.broadcasted_iota(I32, (tq, LANES), 1) < HEAD_DIM
    for h in range(nq):
        q = q_ref[0, h * tq:(h + 1) * tq]
        qs = jnp.concatenate([_split_pair(q[:, p * LANES:(p + 1) * LANES], lo) for p in range(4)], axis=0)
        k = jnp.concatenate([r[0] for r in k_refs[h:h + 3]], axis=0)
        v = jnp.concatenate([r[0] for r in v_refs[h:h + 3]], axis=0)
        s_loc = lax.dot_general(qs, k, _NT, preferred_element_type=F32) + mask_refs[h][0]
        s_ctx = lax.dot_general(qs, kx_ref[0], _NT, preferred_element_type=F32)
        o = _softmax_pv([s_loc, s_ctx], [v, vx_ref[0]], sink_ref[...])
        for p in range(4):
            o_lo = o[(2 * p) * tq:(2 * p + 1) * tq]
            o_hi = o[(2 * p + 1) * tq:(2 * p + 2) * tq]
            o_ref[0, h * tq:(h + 1) * tq, p * LANES:(p + 1) * LANES] = jnp.where(lo, o_lo, o_hi).astype(BF16)


def _attn_a(zq, zc, sink_rep, mask):
    bsz, s, _ = zq.shape
    lc = zc.shape[1]
    tq = A_WINDOW
    nq = A_BLOCKS_PER_STEP
    nblk = s // tq
    ka, va = _KA, _VA

    def kv_spec(col, shift):
        return pl.BlockSpec((1, tq, LANES), lambda b, j: (b, jnp.clip(nq * j + shift, 0, nblk - 1), col))

    def mask_spec(h):
        def variant(j):
            blk = nq * j + h
            return jnp.where(blk == 0, 0, jnp.where(blk == nblk - 1, 2, 1))
        return pl.BlockSpec((1,) + mask.shape[1:], lambda b, j: (variant(j), 0, 0))

    shifts = range(-1, nq + 1)
    return pl.pallas_call(
        _attn_a_kernel,
        out_shape=jax.ShapeDtypeStruct((bsz, s, 4 * LANES), BF16),
        grid=(bsz, nblk // nq),
        in_specs=([pl.BlockSpec((1, nq * tq, 4 * LANES), lambda b, j: (b, j, 0))]
                  + [kv_spec(ka, sh) for sh in shifts] + [kv_spec(va, sh) for sh in shifts]
                  + [pl.BlockSpec((1, lc, LANES), lambda b, j: (b, 0, ka)),
                     pl.BlockSpec((1, lc, LANES), lambda b, j: (b, 0, va)),
                     pl.BlockSpec(sink_rep.shape, lambda b, j: (0, 0))]
                  + [mask_spec(h) for h in range(nq)]),
        out_specs=pl.BlockSpec((1, nq * tq, 4 * LANES), lambda b, j: (b, j, 0)),
        compiler_params=_params("arbitrary", "arbitrary"),
        name="attn_a",
    )(*([zq] * (2 * nq + 5) + [zc, zc, sink_rep] + [mask] * nq))


def _attn_a_mask(tq):
    qi = np.arange(tq)[:, None]
    kj = np.arange(3 * tq)[None, :]
    ok = (kj >= qi) & (kj <= qi + 2 * tq)
    variants = [ok & (kj >= tq), ok, ok & (kj < 2 * tq)]
    m = np.stack([np.tile(np.where(v, 0.0, NEG_INF).astype(np.float32), (A_Q_HEADS, 1)) for v in variants])
    return jnp.asarray(m)


def _attn_b_kernel(q_ref, k0_ref, k1_ref, k2_ref, v0_ref, v1_ref, v2_ref, kx_ref, vx_ref, tab_ref, o_ref):
    tq = q_ref.shape[1]
    lo = lax.broadcasted_iota(I32, (tq, LANES), 1) < HEAD_DIM
    for p in range(4):
        sl = slice(p * LANES, (p + 1) * LANES)
        qs = _split_pair(q_ref[0, :, sl], lo)
        k = jnp.concatenate([k0_ref[0, :, sl], k1_ref[0, :, sl], k2_ref[0, :, sl]], axis=0)
        v = jnp.concatenate([v0_ref[0, :, sl], v1_ref[0, :, sl], v2_ref[0, :, sl]], axis=0)
        s_loc = lax.dot_general(qs, k, _NT, preferred_element_type=F32) + tab_ref[0, p]
        s_ctx = lax.dot_general(qs, kx_ref[0, :, sl], _NT, preferred_element_type=F32)
        o = _softmax_pv([s_loc, s_ctx], [v, vx_ref[0, :, sl]], None)
        o_ref[0, :, sl] = jnp.where(lo, o[:tq], o[tq:]).astype(BF16)


def _attn_b(zq, zc, table):
    bsz, s, _ = zq.shape
    lc = zc.shape[1]
    tq = NA_QROWS * GRID_W
    ng = s // tq
    qb, kb, vb = _QB // 4, _KB // 4, _VB // 4

    def kv_spec(col, off):
        return pl.BlockSpec((1, tq, 4 * LANES), lambda i, b: (b, jnp.clip(i - 1, 0, ng - 3) + off, col))

    def variant(i):
        return jnp.where(i == 0, 0, jnp.where(i == ng - 1, 2, 1))

    return pl.pallas_call(
        _attn_b_kernel,
        out_shape=jax.ShapeDtypeStruct((bsz, s, 4 * LANES), BF16),
        grid=(ng, bsz),
        in_specs=[pl.BlockSpec((1, tq, 4 * LANES), lambda i, b: (b, i, qb)),
                  kv_spec(kb, 0), kv_spec(kb, 1), kv_spec(kb, 2),
                  kv_spec(vb, 0), kv_spec(vb, 1), kv_spec(vb, 2),
                  pl.BlockSpec((1, lc, 4 * LANES), lambda i, b: (b, 0, kb)),
                  pl.BlockSpec((1, lc, 4 * LANES), lambda i, b: (b, 0, vb)),
                  pl.BlockSpec((1,) + table.shape[1:], lambda i, b: (variant(i), 0, 0, 0))],
        out_specs=pl.BlockSpec((1, tq, 4 * LANES), lambda i, b: (b, i, 0)),
        compiler_params=_params("arbitrary", "arbitrary"),
        name="attn_b",
    )(zq, zq, zq, zq, zq, zq, zq, zc, zc, table)


def _na_table(rpb, rows):
    ng = rows // NA_QROWS
    nq, nk = NA_QROWS * GRID_W, NA_KROWS * GRID_W
    qc = np.arange(GRID_W)
    kc = np.arange(GRID_W)
    ws = np.clip(qc - NA_KW // 2, 0, GRID_W - NA_KW)
    valid_c = (kc[None, :] >= ws[:, None]) & (kc[None, :] < ws[:, None] + NA_KW)
    dc = np.clip(kc[None, :] - qc[:, None], -(NA_KW - 1), NA_KW - 1) + (NA_KW - 1)
    c_sel = (dc[..., None] == np.arange(2 * NA_KW - 1)) & valid_c[..., None]
    tiles = jnp.einsum('hab,uvb->hauv', rpb.astype(F32), jnp.asarray(c_sel, F32), precision=lax.Precision.HIGHEST)
    tiles = jnp.where(jnp.asarray(valid_c)[None, None], tiles * LOG2E, NEG_INF)
    masked = jnp.full((B_HEADS, GRID_W, GRID_W), NEG_INF, F32)
    tabs = []
    for i in (0, 1, ng - 1):
        start = int(np.clip(NA_QROWS * i - NA_KH // 2, 0, rows - NA_KROWS))
        q_rows = []
        for qr in range(NA_QROWS):
            r = NA_QROWS * i + qr
            rs = int(np.clip(r - NA_KH // 2, 0, rows - NA_KH))
            k_tiles = []
            for kr in range(NA_KROWS):
                krow = start + kr
                k_tiles.append(tiles[:, krow - r + (NA_KH - 1)] if rs <= krow < rs + NA_KH else masked)
            q_rows.append(jnp.concatenate(k_tiles, axis=-1))
        tabs.append(jnp.concatenate(q_rows, axis=-2))
    return jnp.stack(tabs).reshape(3, B_HEADS // 2, 2 * nq, nk)


def _merge_kernel(oa_ref, ob_ref, x_ref, mod_ref, ga_ref, gb_ref, wo_ref, n2_ref, wrh_ref, wrl_ref, rb_ref,
                  wsg_ref, wsu_ref, wsd_ref, xres_ref, hpa_ref, hpb_ref, idx_ref, gate_ref, *, d):
    b = pl.program_id(0)
    tm = x_ref.shape[1]

    def mod(k):
        return mod_ref[pl.ds(b, 1), k * d:(k + 1) * d]

    na = _rms(oa_ref[0].astype(F32), ga_ref[...])
    nb = _rms(ob_ref[0].astype(F32), gb_ref[...])
    cat = jnp.concatenate([na, nb], axis=1).astype(BF16)
    y = jnp.dot(cat, wo_ref[...], preferred_element_type=F32)
    x1 = x_ref[0] + mod(2) * y
    h2 = _rms(x1, n2_ref[...]) * (1.0 + mod(4)) + mod(3)

    hb = h2.astype(BF16)
    act = _silu(jnp.dot(hb, wsg_ref[...], preferred_element_type=F32)) * jnp.dot(hb, wsu_ref[...],
                                                                                 preferred_element_type=F32)
    shared = jnp.dot(act.astype(BF16), wsd_ref[...], preferred_element_type=F32)
    xres_ref[0] = x1 + mod(5) * shared
    packed = _pack_bf16_pairs(h2)
    hpa_ref[...] = packed[:, :d // 4]
    hpb_ref[...] = packed[:, d // 4:]

    h_hi, h_lo = _split_bf16(h2)
    logits = (lax.dot_general(wrh_ref[...], h_hi, _NT, preferred_element_type=F32)
              + lax.dot_general(wrh_ref[...], h_lo, _NT, preferred_element_type=F32)
              + lax.dot_general(wrl_ref[...], h_hi, _NT, preferred_element_type=F32))
    scores = jax.nn.sigmoid(logits)
    sel = scores + rb_ref[...]
    per = N_EXPERTS // N_GROUPS
    g3 = sel.reshape(N_GROUPS, per, tm)
    it3 = lax.broadcasted_iota(I32, (N_GROUPS, per, tm), 1)
    m1 = jnp.max(g3, axis=1, keepdims=True)
    first = jnp.min(jnp.where(g3 == m1, it3, per), axis=1, keepdims=True)
    m2 = jnp.max(jnp.where(it3 == first, -jnp.inf, g3), axis=1, keepdims=True)
    gscore = (m1 + m2).reshape(N_GROUPS, tm)

    itg = lax.broadcasted_iota(I32, (N_GROUPS, tm), 0)
    gsel = jnp.zeros((N_GROUPS, tm), F32)
    cur = gscore
    for _ in range(TOPK_GROUPS):
        mx = jnp.max(cur, axis=0, keepdims=True)
        fi = jnp.min(jnp.where(cur == mx, itg, N_GROUPS), axis=0, keepdims=True)
        pick = itg == fi
        gsel = jnp.where(pick, 1.0, gsel)
        cur = jnp.where(pick, -jnp.inf, cur)
    emask = jnp.broadcast_to(gsel.reshape(N_GROUPS, 1, tm), (N_GROUPS, per, tm)).reshape(N_EXPERTS, tm) > 0.5

    ite = lax.broadcasted_iota(I32, (N_EXPERTS, tm), 0)
    cur = jnp.where(emask, sel, NEG_INF)
    idx_rows, s_rows = [], []
    for _ in range(TOP_K):
        mx = jnp.max(cur, axis=0, keepdims=True)
        fi = jnp.min(jnp.where(cur == mx, ite, N_EXPERTS), axis=0, keepdims=True)
        pick = ite == fi
        idx_rows.append(fi)
        s_rows.append(jnp.sum(jnp.where(pick, scores, 0.0), axis=0, keepdims=True))
        cur = jnp.where(pick, -jnp.inf, cur)
    top_s = jnp.concatenate(s_rows, axis=0)
    idx_ref[...] = jnp.concatenate(idx_rows, axis=0)
    gate_ref[...] = top_s / jnp.sum(top_s, axis=0, keepdims=True) * ROUTED_SCALE


def _merge(o_a, o_b, x, mod, ga, gb, wo, n2, wr_hi, wr_lo, rb_col, wsg, wsu, wsd, *, tm):
    bsz, s, d = x.shape
    nt = s // tm
    n = bsz * s
    full = lambda a: pl.BlockSpec(a.shape, lambda b, i: (0,) * a.ndim)
    return pl.pallas_call(
        functools.partial(_merge_kernel, d=d),
        out_shape=(jax.ShapeDtypeStruct((bsz, s, d), F32),
                   jax.ShapeDtypeStruct((n, d // 4), U32),
                   jax.ShapeDtypeStruct((n, d // 4), U32),
                   jax.ShapeDtypeStruct((TOP_K, n), I32),
                   jax.ShapeDtypeStruct((TOP_K, n), F32)),
        grid=(bsz, nt),
        in_specs=[pl.BlockSpec((1, tm, d // 2), lambda b, i: (b, i, 0)),
                  pl.BlockSpec((1, tm, d // 2), lambda b, i: (b, i, 0)),
                  pl.BlockSpec((1, tm, d), lambda b, i: (b, i, 0)),
                  full(mod), full(ga), full(gb), full(wo), full(n2), full(wr_hi), full(wr_lo), full(rb_col),
                  full(wsg), full(wsu), full(wsd)],
        out_specs=(pl.BlockSpec((1, tm, d), lambda b, i: (b, i, 0)),
                   pl.BlockSpec((tm, d // 4), lambda b, i: (b * nt + i, 0)),
                   pl.BlockSpec((tm, d // 4), lambda b, i: (b * nt + i, 0)),
                   pl.BlockSpec((TOP_K, tm), lambda b, i: (0, b * nt + i)),
                   pl.BlockSpec((TOP_K, tm), lambda b, i: (0, b * nt + i))),
        compiler_params=_params("arbitrary", "arbitrary"),
        name="merge",
    )(o_a, o_b, x, mod, ga, gb, wo, n2, wr_hi, wr_lo, rb_col, wsg, wsu, wsd)


def _rank_kernel(idx_ref, rank_ref, cnt_ref, carry_ref):
    tm = idx_ref.shape[1]

    @pl.when(pl.program_id(0) == 0)
    def _():
        carry_ref[...] = jnp.zeros_like(carry_ref)

    idx = idx_ref[...]
    ite = lax.broadcasted_iota(I32, (N_EXPERTS, tm), 0)
    before = (lax.broadcasted_iota(I32, (tm, tm), 0) < lax.broadcasted_iota(I32, (tm, tm), 1)).astype(BF16)
    base = carry_ref[...]
    rows = []
    for k in range(TOP_K):
        oh = ite == idx[k:k + 1, :]
        ohf = jnp.where(oh, 1.0, 0.0)
        cum = jnp.dot(ohf.astype(BF16), before, preferred_element_type=F32)
        rows.append(jnp.sum(jnp.where(oh, cum + base, 0.0), axis=0, keepdims=True))
        base = base + jnp.sum(ohf, axis=1, keepdims=True)
    rank_ref[...] = jnp.concatenate(rows, axis=0).astype(I32)
    carry_ref[...] = base
    cnt_ref[...] = base


def _rank(idx, *, tm):
    n = idx.shape[1]
    return pl.pallas_call(
        _rank_kernel,
        out_shape=(jax.ShapeDtypeStruct((TOP_K, n), I32), jax.ShapeDtypeStruct((N_EXPERTS, 1), F32)),
        grid=(n // tm,),
        in_specs=[pl.BlockSpec((TOP_K, tm), lambda i: (0, i))],
        out_specs=(pl.BlockSpec((TOP_K, tm), lambda i: (0, i)), pl.BlockSpec((N_EXPERTS, 1), lambda i: (0, 0))),
        scratch_shapes=[pltpu.VMEM((N_EXPERTS, 1), F32)],
        compiler_params=_params("arbitrary"),
        name="rank",
    )(idx)


def _pos_kernel(idx_ref, rank_ref, pstart_ref, pos_ref):
    tm = idx_ref.shape[1]
    idx = idx_ref[...]
    ite = lax.broadcasted_iota(I32, (N_EXPERTS, tm), 0)
    pstart = pstart_ref[...]
    rows = [jnp.sum(jnp.where(ite == idx[k:k + 1, :], pstart, 0.0), axis=0, keepdims=True) for k in range(TOP_K)]
    pos_ref[...] = jnp.concatenate(rows, axis=0).astype(I32) + rank_ref[...]


def _pos(idx, rank, pstart_col, *, tm):
    n = idx.shape[1]
    return pl.pallas_call(
        _pos_kernel,
        out_shape=jax.ShapeDtypeStruct((TOP_K, n), I32),
        grid=(n // tm,),
        in_specs=[pl.BlockSpec((TOP_K, tm), lambda i: (0, i)),
                  pl.BlockSpec((TOP_K, tm), lambda i: (0, i)),
                  pl.BlockSpec((N_EXPERTS, 1), lambda i: (0, 0))],
        out_specs=pl.BlockSpec((TOP_K, tm), lambda i: (0, i)),
        compiler_params=_params("arbitrary"),
        name="pos",
    )(idx, rank, pstart_col)


UNIT_CHUNKS = (4, 2, 1)
RING_AHEAD = 4
RING_SLOTS = RING_AHEAD + UNIT_CHUNKS[0]


def _expert_kernel(ps_ref, cnt_ref, tot_ref, wg_ref, wu_ref, wd_ref, xa_hbm, xb_hbm, ya_hbm, yb_hbm,
                   wg_s, wu_s, wd_s, xa_buf, xb_buf, ya_buf, yb_buf, in_sem, out_sem):
    e = pl.program_id(0)
    r = EXPERT_ROWS
    ahead = RING_AHEAD
    total = tot_ref[0]
    first = ps_ref[e] // r
    cnt = cnt_ref[e]
    nch = (cnt + r - 1) // r

    def rows_of(g):
        return pl.ds(pl.multiple_of(g * r, r), r)

    def fetch(g):
        slot = g % RING_SLOTS
        return (pltpu.make_async_copy(xa_hbm.at[rows_of(g)], xa_buf.at[slot], in_sem.at[0, slot]),
                pltpu.make_async_copy(xb_hbm.at[rows_of(g)], xb_buf.at[slot], in_sem.at[1, slot]))

    def flush(g):
        slot = g % RING_SLOTS
        return (pltpu.make_async_copy(ya_buf.at[slot], ya_hbm.at[rows_of(g)], out_sem.at[0, slot]),
                pltpu.make_async_copy(yb_buf.at[slot], yb_hbm.at[rows_of(g)], out_sem.at[1, slot]))

    def start(copies):
        for cp in copies:
            cp.start()

    def wait(copies):
        for cp in copies:
            cp.wait()

    @pl.when(e == 0)
    def _():
        for g in range(ahead):
            @pl.when(g < total)
            def _():
                start(fetch(g))

    @pl.when(nch > 0)
    def _():
        wg_s[...] = wg_ref[0].astype(BF16)
        wu_s[...] = wu_ref[0].astype(BF16)
        wd_s[...] = wd_ref[0].astype(BF16)

        def unit(c0, u):
            for j in range(u):
                g = first + c0 + j
                wait(fetch(g))

                @pl.when(g + ahead < total)
                def _():
                    start(fetch(g + ahead))

                @pl.when(g >= RING_SLOTS)
                def _():
                    wait(flush(g - RING_SLOTS))

            slots = [(first + c0 + j) % RING_SLOTS for j in range(u)]
            packed = jnp.concatenate([jnp.concatenate([xa_buf[sl], xb_buf[sl]], axis=1) for sl in slots], axis=0)
            row = lax.broadcasted_iota(I32, packed.shape, 0)
            packed = jnp.where(row < cnt - c0 * r, packed, jnp.zeros_like(packed))
            x = _unpack_bf16_pairs(packed).astype(BF16)
            gate = jnp.dot(x, wg_s[...], preferred_element_type=F32)
            up = jnp.dot(x, wu_s[...], preferred_element_type=F32)
            y = jnp.dot((_silu(gate) * up).astype(BF16), wd_s[...], preferred_element_type=F32)
            out = _pack_bf16_pairs(y)
            half = out.shape[1] // 2
            for j, sl in enumerate(slots):
                ya_buf[sl] = out[j * r:(j + 1) * r, :half]
                yb_buf[sl] = out[j * r:(j + 1) * r, half:]
                start(flush(first + c0 + j))

        big = UNIT_CHUNKS[0]

        def big_unit(i, carry):
            unit(i * big, big)
            return carry

        lax.fori_loop(0, nch // big, big_unit, 0)
        done = nch // big * big
        for u in UNIT_CHUNKS[1:]:
            @pl.when((nch % (2 * u)) >= u)
            def _():
                unit(done, u)

            done = done + jnp.where((nch % (2 * u)) >= u, u, 0)

    @pl.when(e == pl.num_programs(0) - 1)
    def _():
        for k in range(RING_SLOTS):
            @pl.when(total - 1 - k >= 0)
            def _():
                wait(flush(total - 1 - k))


def _experts(pstart, cnt, total_chunks, xs_a, xs_b, wg, wu, wd):
    rows, hw = xs_a.shape
    r = EXPERT_ROWS
    n_exp, d, f = wg.shape
    hbm = pl.BlockSpec(memory_space=pl.ANY)
    grid_spec = pltpu.PrefetchScalarGridSpec(
        num_scalar_prefetch=3,
        grid=(n_exp,),
        in_specs=[pl.BlockSpec((1, d, f), lambda e, ps, cn, tot: (e, 0, 0)),
                  pl.BlockSpec((1, d, f), lambda e, ps, cn, tot: (e, 0, 0)),
                  pl.BlockSpec((1, f, d), lambda e, ps, cn, tot: (e, 0, 0)),
                  hbm, hbm],
        out_specs=(hbm, hbm),
        scratch_shapes=[pltpu.VMEM((d, f), BF16), pltpu.VMEM((d, f), BF16), pltpu.VMEM((f, d), BF16),
                        pltpu.VMEM((RING_SLOTS, r, hw), U32), pltpu.VMEM((RING_SLOTS, r, hw), U32),
                        pltpu.VMEM((RING_SLOTS, r, hw), U32), pltpu.VMEM((RING_SLOTS, r, hw), U32),
                        pltpu.SemaphoreType.DMA((2, RING_SLOTS)), pltpu.SemaphoreType.DMA((2, RING_SLOTS))],
    )
    return pl.pallas_call(
        _expert_kernel,
        out_shape=(jax.ShapeDtypeStruct((rows, hw), U32), jax.ShapeDtypeStruct((rows, hw), U32)),
        grid_spec=grid_spec,
        compiler_params=_params("arbitrary"),
        name="experts",
    )(pstart, cnt, total_chunks, wg, wu, wd, xs_a, xs_b)


def _sc_scatter_rows(x, idx_flat, n_rows):
    n, w = x.shape
    m = idx_flat.shape[1]
    nwin = n // SC_WINDOW
    reps = m // n
    mesh = plsc.VectorSubcoreMesh(core_axis_name="core", subcore_axis_name="subcore")

    @functools.partial(pl.kernel, out_type=jax.ShapeDtypeStruct((n_rows, w), x.dtype), mesh=mesh,
                       scratch_types=[])
    def scatter_kernel(x_hbm, i_hbm, o_hbm):
        def body(x_vmem, i_vmem):
            pltpu.sync_copy(x_vmem, o_hbm.at[i_vmem.at[0]])

        pltpu.emit_pipeline(
            body,
            grid=(nwin, reps),
            in_specs=[pl.BlockSpec((SC_WINDOW, w), lambda i, k: (i, 0)),
                      pl.BlockSpec((1, SC_WINDOW), lambda i, k: (0, k * nwin + i))],
            out_specs=[],
            core_axis_name=("core", "subcore"),
            dimension_semantics=(pltpu.PARALLEL, pltpu.ARBITRARY),
        )(x_hbm, i_hbm)

    return scatter_kernel(x, idx_flat)


def _sc_gather_rows(src, idx_flat):
    m = idx_flat.shape[1]
    w = src.shape[1]
    mesh = plsc.VectorSubcoreMesh(core_axis_name="core", subcore_axis_name="subcore")

    @functools.partial(pl.kernel, out_type=jax.ShapeDtypeStruct((m, w), src.dtype), mesh=mesh)
    def gather_kernel(x_hbm, i_hbm, o_hbm):
        def body(i_vmem, o_vmem):
            pltpu.sync_copy(x_hbm.at[i_vmem.at[0]], o_vmem)

        pltpu.emit_pipeline(
            body,
            grid=(m // SC_WINDOW,),
            in_specs=[pl.BlockSpec((1, SC_WINDOW), lambda i: (0, i))],
            out_specs=[pl.BlockSpec((SC_WINDOW, w), lambda i: (i, 0))],
            core_axis_name=("core", "subcore"),
            dimension_semantics=(pltpu.PARALLEL,),
        )(i_hbm, o_hbm)

    return gather_kernel(src, idx_flat)


def _finish_kernel(gate_ref, xres_ref, mod_ref, ya_ref, yb_ref, o_ref, *, d, tiles_per_batch):
    tm = gate_ref.shape[1]
    b = pl.program_id(0) // tiles_per_batch
    gates = gate_ref[...].T
    acc = jnp.zeros((tm, d), F32)
    for k in range(TOP_K):
        packed = jnp.concatenate([ya_ref[k], yb_ref[k]], axis=1)
        acc = acc + _unpack_bf16_pairs(packed) * gates[:, k:k + 1]
    o_ref[...] = xres_ref[...] + mod_ref[pl.ds(b, 1), 5 * d:6 * d] * acc


def _finish(gates, xres, mod, yg_a, yg_b, *, tm, tiles_per_batch):
    n, d = xres.shape
    return pl.pallas_call(
        functools.partial(_finish_kernel, d=d, tiles_per_batch=tiles_per_batch),
        out_shape=jax.ShapeDtypeStruct((n, d), F32),
        grid=(n // tm,),
        in_specs=[pl.BlockSpec((TOP_K, tm), lambda i: (0, i)),
                  pl.BlockSpec((tm, d), lambda i: (i, 0)),
                  pl.BlockSpec(mod.shape, lambda i: (0, 0)),
                  pl.BlockSpec((TOP_K, tm, d // 4), lambda i: (0, i, 0)),
                  pl.BlockSpec((TOP_K, tm, d // 4), lambda i: (0, i, 0))],
        out_specs=pl.BlockSpec((tm, d), lambda i: (i, 0)),
        compiler_params=_params("arbitrary"),
        name="finish",
    )(gates, xres, mod, yg_a, yg_b)


_QA_HEAD_ORDER = (0, 4, 1, 5, 2, 6, 3, 7)


def _reorder_qa_heads(a, axis):
    return jnp.concatenate([lax.slice_in_dim(a, h * HEAD_DIM, (h + 1) * HEAD_DIM, axis=axis)
                            for h in _QA_HEAD_ORDER], axis=axis)


def _rope_swap(a):
    quarter = HEAD_DIM // 4
    lead = a.shape[:-1]
    return jnp.flip(a.reshape(lead + (-1, 2, quarter)), axis=-2).reshape(a.shape)


def _rope_tables(s):
    quarter = HEAD_DIM // 4
    t = jnp.arange(s)
    row = (t // GRID_W).astype(F32)
    col = (t % GRID_W).astype(F32)
    freqs = ROPE_BASE ** (-jnp.arange(quarter, dtype=F32) / quarter)
    ar = row[:, None] * freqs[None, :]
    ac = col[:, None] * freqs[None, :]
    cos = jnp.concatenate([jnp.cos(ar), jnp.cos(ar), jnp.cos(ac), jnp.cos(ac)], axis=1)
    sin = jnp.concatenate([-jnp.sin(ar), jnp.sin(ar), -jnp.sin(ac), jnp.sin(ac)], axis=1)
    return jnp.tile(cos, (1, 2)), jnp.tile(sin, (1, 2))


def kernel(x, c, ctx, c_ctx, w_ada, b_ada, norm1, norm2, w_in, q_norm_a, k_norm_a, q_norm_b, k_norm_b, sink_a,
           rpb_b, out_norm_a, out_norm_b, w_out, w_router, router_bias, we_gate, we_up, we_down, ws_gate, ws_up,
           ws_down):
    assert w_ada.shape[0] == 1, "single-layer block"
    bsz, s, d = x.shape
    lc = ctx.shape[1]
    n = bsz * s
    rows = s // GRID_W
    assert s % (NA_QROWS * GRID_W) == 0 and rows >= NA_KROWS and bsz <= 4 and d == 1024

    c8 = jnp.concatenate([c, c_ctx[None, :], jnp.zeros((8 - bsz - 1, d), F32)], axis=0)
    mod = _ada(c8, w_ada[0], b_ada[0][None, :])

    w = w_in[0]
    aw, akw, bw = A_Q_HEADS * HEAD_DIM, A_KV_HEADS * HEAD_DIM, B_HEADS * HEAD_DIM
    cuts = np.cumsum([0, aw, akw, akw, bw, bw, bw])
    qa_w, ka_w, va_w, qb_w, kb_w, vb_w = [w[:, cuts[i]:cuts[i + 1]] for i in range(6)]
    qa_w = _reorder_qa_heads(qa_w, 1)
    w_ext = jnp.concatenate([qa_w, qb_w, kb_w, vb_w, ka_w, va_w, _rope_swap(qa_w), _rope_swap(ka_w)],
                            axis=1).astype(BF16)
    scale = HEAD_DIM ** -0.5 * LOG2E
    pair = lambda g: jnp.tile(g, 2)
    gains = jnp.stack([pair(q_norm_a[0]) * scale, pair(q_norm_b[0]) * scale, pair(k_norm_b[0]), pair(k_norm_a[0]),
                       pair(_rope_swap(q_norm_a[0])) * scale, pair(_rope_swap(k_norm_a[0])),
                       jnp.zeros((LANES,), F32), jnp.zeros((LANES,), F32)])
    bd = jnp.asarray(np.kron(np.eye(2), np.full((HEAD_DIM, HEAD_DIM), 1.0 / HEAD_DIM)), BF16)
    cos_t, sin_t = _rope_tables(s)
    n1 = norm1[0][None, :]
    zq = _inproj(x, mod, n1, w_ext, gains, cos_t, sin_t, bd, mod_row=None, tm=512)
    zc = _inproj(ctx, mod, n1, w_ext, gains, jnp.ones((lc, LANES), F32), jnp.zeros((lc, LANES), F32), bd,
                 mod_row=bsz, tm=lc)

    sink = sink_a[0].astype(F32)
    sink_rep = jnp.concatenate([jnp.broadcast_to(sink[h] * LOG2E, (A_WINDOW, LANES)) for h in _QA_HEAD_ORDER])
    o_a = _attn_a(zq, zc, sink_rep, _attn_a_mask(A_WINDOW))
    o_b = _attn_b(zq, zc, _na_table(rpb_b[0], rows))

    ga = _reorder_qa_heads(out_norm_a[0], 0)[None, :]
    gb = out_norm_b[0][None, :]
    wo = jnp.concatenate([_reorder_qa_heads(w_out[0][:aw], 0), w_out[0][aw:]], axis=0).astype(BF16)
    wr_hi, wr_lo = _split_bf16(w_router[0].T)
    xres, hp_a, hp_b, idx, gates = _merge(o_a, o_b, x, mod, ga, gb, wo, norm2[0][None, :], wr_hi, wr_lo,
                                  router_bias[0][:, None], ws_gate[0].astype(BF16), ws_up[0].astype(BF16),
                                  ws_down[0].astype(BF16), tm=512)

    rank, counts = _rank(idx, tm=256)
    t = EXPERT_ROWS
    cnt = counts[:, 0].astype(I32)
    padded = (cnt + t - 1) // t * t
    pends = jnp.cumsum(padded)
    pstart = pends - padded
    n_rows = n * TOP_K + N_EXPERTS * t
    pos = _pos(idx, rank, pstart.astype(F32)[:, None], tm=512)

    pos_flat = pos.reshape(1, TOP_K * n)
    xs_a = _sc_scatter_rows(hp_a, pos_flat, n_rows)
    xs_b = _sc_scatter_rows(hp_b, pos_flat, n_rows)
    ys_a, ys_b = _experts(pstart, cnt, pends[-1:] // t, xs_a, xs_b, we_gate[0], we_up[0], we_down[0])
    yg_a = _sc_gather_rows(ys_a, pos_flat).reshape(TOP_K, n, d // 4)
    yg_b = _sc_gather_rows(ys_b, pos_flat).reshape(TOP_K, n, d // 4)
    out = _finish(gates, xres.reshape(n, d), mod, yg_a, yg_b, tm=256, tiles_per_batch=s // 256)
    return out.reshape(bsz, s, d)
```

```python
import functools

import numpy as np
import jax
import jax.numpy as jnp
from jax import lax
from jax.experimental import pallas as pl
from jax.experimental.pallas import tpu as pltpu
from jax.experimental.pallas import tpu_sc as plsc

F32 = jnp.float32
BF16 = jnp.bfloat16
I32 = jnp.int32
U32 = jnp.uint32

LANES = 128
HEAD_DIM = 64
HEAD_PAIR = 2 * HEAD_DIM
GRID_W = 64
A_Q_HEADS = 8
A_KV_HEADS = 2
A_WINDOW = 128
B_HEADS = 8
NA_KH = 8
NA_KW = 16
NA_QROWS = 4
NA_KROWS = NA_QROWS + NA_KH
ROPE_BASE = 10000.0
N_EXPERTS = 256
TOP_K = 8
N_GROUPS = 8
TOPK_GROUPS = 4
ROUTED_SCALE = 2.5
LOG2E = 1.4426950408889634
EPS = 1e-6
NEG_INF = -1e30
EXPERT_ROWS = 256
SC_WINDOW = 128
VMEM_LIMIT = 56 * 1024 * 1024

_NT = (((1,), (1,)), ((), ()))


def _params(*sem):
    return pltpu.CompilerParams(dimension_semantics=sem, vmem_limit_bytes=VMEM_LIMIT)


def _silu(v):
    return v * jax.nn.sigmoid(v)


def _rms(v, gain):
    return v * lax.rsqrt(jnp.mean(v * v, axis=-1, keepdims=True) + EPS) * gain


def _pack_bf16_pairs(v):
    n = v.shape[1] // 2
    lo = lax.bitcast_convert_type(v[:, :n].astype(BF16).astype(F32), U32) >> 16
    hi = lax.bitcast_convert_type(v[:, n:].astype(BF16).astype(F32), U32) & jnp.uint32(0xFFFF0000)
    return hi | lo


def _split_bf16(v):
    hi = lax.bitcast_convert_type(lax.bitcast_convert_type(v, U32) & jnp.uint32(0xFFFF0000), F32)
    return hi.astype(BF16), (v - hi).astype(BF16)


def _unpack_bf16_pairs(w):
    lo = lax.bitcast_convert_type(w << 16, F32)
    hi = lax.bitcast_convert_type(w & jnp.uint32(0xFFFF0000), F32)
    return jnp.concatenate([lo, hi], axis=1)


def _ada_kernel(c_ref, w_ref, b_ref, o_ref):
    a = _silu(c_ref[...])
    o_ref[...] = jnp.dot(a, w_ref[...], preferred_element_type=F32,
                         precision=lax.Precision.HIGHEST) + b_ref[...]


def _ada(c8, w, b):
    d, n = w.shape
    bn = n // 4
    return pl.pallas_call(
        _ada_kernel,
        out_shape=jax.ShapeDtypeStruct((8, n), F32),
        grid=(n // bn,),
        in_specs=[pl.BlockSpec((8, d), lambda j: (0, 0)),
                  pl.BlockSpec((d, bn), lambda j: (0, j)),
                  pl.BlockSpec((1, bn), lambda j: (0, j))],
        out_specs=pl.BlockSpec((8, bn), lambda j: (0, j)),
        compiler_params=_params("arbitrary"),
        name="ada",
    )(c8, w, b)


_QA, _QB, _KB, _VB, _KA, _VA, _QA_SW, _KA_SW = 0, 4, 8, 12, 16, 17, 18, 22
_OUT_BLOCKS = 18
_EXT_BLOCKS = 23


def _inproj_kernel(x_ref, mod_ref, n1_ref, w_ref, g_ref, cos_ref, sin_ref, bd_ref, o_ref, *, mod_row, d):
    b = pl.program_id(0) if mod_row is None else mod_row
    xn = _rms(x_ref[0], n1_ref[...])
    sh = mod_ref[pl.ds(b, 1), 0:d]
    sc = mod_ref[pl.ds(b, 1), d:2 * d]
    h = (xn * (1.0 + sc) + sh).astype(BF16)
    z = jnp.dot(h, w_ref[...], preferred_element_type=F32)
    bd = bd_ref[...]
    cos = cos_ref[...]
    sin = sin_ref[...]

    def blk(j):
        return z[:, j * LANES:(j + 1) * LANES]

    def head_rinv(zb):
        ms = jnp.dot((zb * zb).astype(BF16), bd, preferred_element_type=F32)
        return lax.rsqrt(ms + EPS)

    def put(j, v):
        o_ref[0, :, j * LANES:(j + 1) * LANES] = v.astype(BF16)

    def roped(j, j_sw, g_row, g_sw_row):
        zb = blk(j)
        r = head_rinv(zb)
        put(j, (zb * r * g_ref[g_row:g_row + 1, :]) * cos + (blk(j_sw) * r * g_ref[g_sw_row:g_sw_row + 1, :]) * sin)

    def normed(j, g_row):
        zb = blk(j)
        put(j, zb * head_rinv(zb) * g_ref[g_row:g_row + 1, :])

    for j in range(4):
        roped(_QA + j, _QA_SW + j, 0, 4)
        normed(_QB + j, 1)
        normed(_KB + j, 2)
        put(_VB + j, blk(_VB + j))
    roped(_KA, _KA_SW, 3, 5)
    put(_VA, blk(_VA))


def _inproj(x, mod, n1, w_ext, gains, cos_t, sin_t, bd, *, mod_row, tm):
    bsz, s, d = x.shape
    kern = functools.partial(_inproj_kernel, mod_row=mod_row, d=d)
    return pl.pallas_call(
        kern,
        out_shape=jax.ShapeDtypeStruct((bsz, s, _OUT_BLOCKS * LANES), BF16),
        grid=(bsz, s // tm),
        in_specs=[pl.BlockSpec((1, tm, d), lambda b, i: (b, i, 0)),
                  pl.BlockSpec(mod.shape, lambda b, i: (0, 0)),
                  pl.BlockSpec((1, d), lambda b, i: (0, 0)),
                  pl.BlockSpec(w_ext.shape, lambda b, i: (0, 0)),
                  pl.BlockSpec(gains.shape, lambda b, i: (0, 0)),
                  pl.BlockSpec((tm, LANES), lambda b, i: (i, 0)),
                  pl.BlockSpec((tm, LANES), lambda b, i: (i, 0)),
                  pl.BlockSpec(bd.shape, lambda b, i: (0, 0))],
        out_specs=pl.BlockSpec((1, tm, _OUT_BLOCKS * LANES), lambda b, i: (b, i, 0)),
        compiler_params=_params("arbitrary", "arbitrary"),
        name="inproj",
    )(x, mod, n1, w_ext, gains, cos_t, sin_t, bd)


def _split_pair(qp, lo):
    zero = jnp.zeros_like(qp)
    return jnp.concatenate([jnp.where(lo, qp, zero), jnp.where(lo, zero, qp)], axis=0)


def _softmax_pv(s_parts, v_parts, sink_rep):
    chunks = [s[:, c * LANES:(c + 1) * LANES] for s in s_parts for c in range(s.shape[1] // LANES)]
    m = jnp.max(functools.reduce(jnp.maximum, chunks), axis=-1, keepdims=True)
    m_rep = jnp.broadcast_to(m, (m.shape[0], LANES))
    if sink_rep is not None:
        m_rep = jnp.maximum(m_rep, sink_rep)
    acc = None
    for s, v in zip(s_parts, v_parts):
        p = jnp.concatenate([jnp.exp2(s[:, c * LANES:(c + 1) * LANES] - m_rep)
                             for c in range(s.shape[1] // LANES)], axis=1).astype(BF16)
        v_ext = jnp.concatenate([v, jnp.ones_like(v)], axis=1)
        o = jnp.dot(p, v_ext, preferred_element_type=F32)
        acc = o if acc is None else acc + o
    l_rep = acc[:, LANES:]
    if sink_rep is not None:
        l_rep = l_rep + jnp.exp2(sink_rep - m_rep)
    return acc[:, :LANES] * (1.0 / l_rep)


A_BLOCKS_PER_STEP = 4


def _attn_a_kernel(q_ref, *refs):
    nq = A_BLOCKS_PER_STEP
    k_refs, v_refs = refs[:nq + 2], refs[nq + 2:2 * nq + 4]
    kx_ref, vx_ref, sink_ref = refs[2 * nq + 4:2 * nq + 7]
    mask_refs = refs[2 * nq + 7:3 * nq + 7]
    o_ref = refs[-1]
    tq = A_WINDOW
    lo = lax.broadcasted_iota(I32, (tq, LANES), 1) < HEAD_DIM
    for h in range(nq):
        q = q_ref[0, h * tq:(h + 1) * tq]
        qs = jnp.concatenate([_split_pair(q[:, p * LANES:(p + 1) * LANES], lo) for p in range(4)], axis=0)
        k = jnp.concatenate([r[0] for r in k_refs[h:h + 3]], axis=0)
        v = jnp.concatenate([r[0] for r in v_refs[h:h + 3]], axis=0)
        s_loc = lax.dot_general(qs, k, _NT, preferred_element_type=F32) + mask_refs[h][0]
        s_ctx = lax.dot_general(qs, kx_ref[0], _NT, preferred_element_type=F32)
        o = _softmax_pv([s_loc, s_ctx], [v, vx_ref[0]], sink_ref[...])
        for p in range(4):
            o_lo = o[(2 * p) * tq:(2 * p + 1) * tq]
            o_hi = o[(2 * p + 1) * tq:(2 * p + 2) * tq]
            o_ref[0, h * tq:(h + 1) * tq, p * LANES:(p + 1) * LANES] = jnp.where(lo, o_lo, o_hi).astype(BF16)


def _attn_a(zq, zc, sink_rep, mask):
    bsz, s, _ = zq.shape
    lc = zc.shape[1]
    tq = A_WINDOW
    nq = A_BLOCKS_PER_STEP
    nblk = s // tq
    ka, va = _KA, _VA

    def kv_spec(col, shift):
        return pl.BlockSpec((1, tq, LANES), lambda b, j: (b, jnp.clip(nq * j + shift, 0, nblk - 1), col))

    def mask_spec(h):
        def variant(j):
            blk = nq * j + h
            return jnp.where(blk == 0, 0, jnp.where(blk == nblk - 1, 2, 1))
        return pl.BlockSpec((1,) + mask.shape[1:], lambda b, j: (variant(j), 0, 0))

    shifts = range(-1, nq + 1)
    return pl.pallas_call(
        _attn_a_kernel,
        out_shape=jax.ShapeDtypeStruct((bsz, s, 4 * LANES), BF16),
        grid=(bsz, nblk // nq),
        in_specs=([pl.BlockSpec((1, nq * tq, 4 * LANES), lambda b, j: (b, j, 0))]
                  + [kv_spec(ka, sh) for sh in shifts] + [kv_spec(va, sh) for sh in shifts]
                  + [pl.BlockSpec((1, lc, LANES), lambda b, j: (b, 0, ka)),
                     pl.BlockSpec((1, lc, LANES), lambda b, j: (b, 0, va)),
                     pl.BlockSpec(sink_rep.shape, lambda b, j: (0, 0))]
                  + [mask_spec(h) for h in range(nq)]),
        out_specs=pl.BlockSpec((1, nq * tq, 4 * LANES), lambda b, j: (b, j, 0)),
        compiler_params=_params("arbitrary", "arbitrary"),
        name="attn_a",
    )(*([zq] * (2 * nq + 5) + [zc, zc, sink_rep] + [mask] * nq))


def _attn_a_mask(tq):
    qi = np.arange(tq)[:, None]
    kj = np.arange(3 * tq)[None, :]
    ok = (kj >= qi) & (kj <= qi + 2 * tq)
    variants = [ok & (kj >= tq), ok, ok & (kj < 2 * tq)]
    m = np.stack([np.tile(np.where(v, 0.0, NEG_INF).astype(np.float32), (A_Q_HEADS, 1)) for v in variants])
    return jnp.asarray(m)


NA_GROUPS_PER_STEP = 2


def _attn_b_kernel(q_ref, *refs):
    ngs = NA_GROUPS_PER_STEP
    k_refs, v_refs = refs[:3 * ngs], refs[3 * ngs:6 * ngs]
    kx_ref, vx_ref = refs[6 * ngs:6 * ngs + 2]
    tab_refs = refs[6 * ngs + 2:7 * ngs + 2]
    o_ref = refs[-1]
    tq = NA_QROWS * GRID_W
    lo = lax.broadcasted_iota(I32, (tq, LANES), 1) < HEAD_DIM
    for g in range(ngs):
        rows_g = slice(g * tq, (g + 1) * tq)
        for p in range(4):
            sl = slice(p * LANES, (p + 1) * LANES)
            qs = _split_pair(q_ref[0, rows_g, sl], lo)
            k = jnp.concatenate([r[0, :, sl] for r in k_refs[3 * g:3 * g + 3]], axis=0)
            v = jnp.concatenate([r[0, :, sl] for r in v_refs[3 * g:3 * g + 3]], axis=0)
            s_loc = lax.dot_general(qs, k, _NT, preferred_element_type=F32) + tab_refs[g][0, p]
            s_ctx = lax.dot_general(qs, kx_ref[0, :, sl], _NT, preferred_element_type=F32)
            o = _softmax_pv([s_loc, s_ctx], [v, vx_ref[0, :, sl]], None)
            o_ref[0, rows_g, sl] = jnp.where(lo, o[:tq], o[tq:]).astype(BF16)


def _attn_b(zq, zc, table):
    bsz, s, _ = zq.shape
    lc = zc.shape[1]
    tq = NA_QROWS * GRID_W
    ngs = NA_GROUPS_PER_STEP
    ng = s // tq
    qb, kb, vb = _QB // 4, _KB // 4, _VB // 4

    def kv_spec(col, g, off):
        return pl.BlockSpec((1, tq, 4 * LANES),
                            lambda i, b: (b, jnp.clip(ngs * i + g - 1, 0, ng - 3) + off, col))

    def tab_spec(g):
        def variant(i):
            grp = ngs * i + g
            return jnp.where(grp == 0, 0, jnp.where(grp == ng - 1, 2, 1))
        return pl.BlockSpec((1,) + table.shape[1:], lambda i, b: (variant(i), 0, 0, 0))

    kv_slots = [(g, off) for g in range(ngs) for off in range(3)]
    return pl.pallas_call(
        _attn_b_kernel,
        out_shape=jax.ShapeDtypeStruct((bsz, s, 4 * LANES), BF16),
        grid=(ng // ngs, bsz),
        in_specs=([pl.BlockSpec((1, ngs * tq, 4 * LANES), lambda i, b: (b, i, qb))]
                  + [kv_spec(kb, g, off) for g, off in kv_slots] + [kv_spec(vb, g, off) for g, off in kv_slots]
                  + [pl.BlockSpec((1, lc, 4 * LANES), lambda i, b: (b, 0, kb)),
                     pl.BlockSpec((1, lc, 4 * LANES), lambda i, b: (b, 0, vb))]
                  + [tab_spec(g) for g in range(ngs)]),
        out_specs=pl.BlockSpec((1, ngs * tq, 4 * LANES), lambda i, b: (b, i, 0)),
        compiler_params=_params("arbitrary", "arbitrary"),
        name="attn_b",
    )(*([zq] * (6 * ngs + 1) + [zc, zc] + [table] * ngs))


def _na_table(rpb, rows):
    ng = rows // NA_QROWS
    nq, nk = NA_QROWS * GRID_W, NA_KROWS * GRID_W
    qc = np.arange(GRID_W)
    kc = np.arange(GRID_W)
    ws = np.clip(qc - NA_KW // 2, 0, GRID_W - NA_KW)
    valid_c = (kc[None, :] >= ws[:, None]) & (kc[None, :] < ws[:, None] + NA_KW)
    dc = np.clip(kc[None, :] - qc[:, None], -(NA_KW - 1), NA_KW - 1) + (NA_KW - 1)
    c_sel = (dc[..., None] == np.arange(2 * NA_KW - 1)) & valid_c[..., None]
    tiles = jnp.einsum('hab,uvb->huav', rpb.astype(F32), jnp.asarray(c_sel, F32), precision=lax.Precision.HIGHEST)
    tiles = jnp.where(jnp.asarray(valid_c)[None, :, None, :], tiles * LOG2E, NEG_INF)
    tiles = tiles.reshape(B_HEADS, GRID_W, (2 * NA_KH - 1) * GRID_W)

    def masked(n_key_rows):
        return jnp.full((B_HEADS, GRID_W, n_key_rows * GRID_W), NEG_INF, F32)

    tabs = []
    for i in (0, 1, ng - 1):
        start = int(np.clip(NA_QROWS * i - NA_KH // 2, 0, rows - NA_KROWS))
        q_rows = []
        for qr in range(NA_QROWS):
            r = NA_QROWS * i + qr
            rs = int(np.clip(r - NA_KH // 2, 0, rows - NA_KH))
            y0 = rs - start
            d0 = rs - r + (NA_KH - 1)
            q_rows.append(jnp.concatenate([masked(y0), tiles[:, :, d0 * GRID_W:(d0 + NA_KH) * GRID_W],
                                           masked(NA_KROWS - NA_KH - y0)], axis=-1))
        tabs.append(jnp.concatenate(q_rows, axis=-2))
    return jnp.stack(tabs).reshape(3, B_HEADS // 2, 2 * nq, nk)


def _merge_kernel(oa_ref, ob_ref, x_ref, mod_ref, ga_ref, gb_ref, wo_ref, n2_ref, wrh_ref, wrl_ref, rb_ref,
                  wsg_ref, wsu_ref, wsd_ref, xres_ref, hpa_ref, hpb_ref, idx_ref, gate_ref, *, d):
    b = pl.program_id(0)
    tm = x_ref.shape[1]

    def mod(k):
        return mod_ref[pl.ds(b, 1), k * d:(k + 1) * d]

    na = _rms(oa_ref[0].astype(F32), ga_ref[...])
    nb = _rms(ob_ref[0].astype(F32), gb_ref[...])
    cat = jnp.concatenate([na, nb], axis=1).astype(BF16)
    y = jnp.dot(cat, wo_ref[...], preferred_element_type=F32)
    x1 = x_ref[0] + mod(2) * y
    h2 = _rms(x1, n2_ref[...]) * (1.0 + mod(4)) + mod(3)

    hb = h2.astype(BF16)
    act = _silu(jnp.dot(hb, wsg_ref[...], preferred_element_type=F32)) * jnp.dot(hb, wsu_ref[...],
                                                                                 preferred_element_type=F32)
    shared = jnp.dot(act.astype(BF16), wsd_ref[...], preferred_element_type=F32)
    xres_ref[0] = x1 + mod(5) * shared
    packed = _pack_bf16_pairs(h2)
    hpa_ref[...] = packed[:, :d // 4]
    hpb_ref[...] = packed[:, d // 4:]

    h_hi, h_lo = _split_bf16(h2)
    logits = (lax.dot_general(wrh_ref[...], h_hi, _NT, preferred_element_type=F32)
              + lax.dot_general(wrh_ref[...], h_lo, _NT, preferred_element_type=F32)
              + lax.dot_general(wrl_ref[...], h_hi, _NT, preferred_element_type=F32))
    scores = jax.nn.sigmoid(logits)
    sel = scores + rb_ref[...]
    per = N_EXPERTS // N_GROUPS
    g3 = sel.reshape(N_GROUPS, per, tm)
    it3 = lax.broadcasted_iota(I32, (N_GROUPS, per, tm), 1)
    m1 = jnp.max(g3, axis=1, keepdims=True)
    first = jnp.min(jnp.where(g3 == m1, it3, per), axis=1, keepdims=True)
    m2 = jnp.max(jnp.where(it3 == first, -jnp.inf, g3), axis=1, keepdims=True)
    gscore = (m1 + m2).reshape(N_GROUPS, tm)

    itg = lax.broadcasted_iota(I32, (N_GROUPS, tm), 0)
    gsel = jnp.zeros((N_GROUPS, tm), F32)
    cur = gscore
    for _ in range(TOPK_GROUPS):
        mx = jnp.max(cur, axis=0, keepdims=True)
        fi = jnp.min(jnp.where(cur == mx, itg, N_GROUPS), axis=0, keepdims=True)
        pick = itg == fi
        gsel = jnp.where(pick, 1.0, gsel)
        cur = jnp.where(pick, -jnp.inf, cur)
    emask = jnp.broadcast_to(gsel.reshape(N_GROUPS, 1, tm), (N_GROUPS, per, tm)).reshape(N_EXPERTS, tm) > 0.5

    ite = lax.broadcasted_iota(I32, (N_EXPERTS, tm), 0)
    cur = jnp.where(emask, sel, NEG_INF)
    idx_rows, s_rows = [], []
    for _ in range(TOP_K):
        mx = jnp.max(cur, axis=0, keepdims=True)
        fi = jnp.min(jnp.where(cur == mx, ite, N_EXPERTS), axis=0, keepdims=True)
        pick = ite == fi
        idx_rows.append(fi)
        s_rows.append(jnp.sum(jnp.where(pick, scores, 0.0), axis=0, keepdims=True))
        cur = jnp.where(pick, -jnp.inf, cur)
    top_s = jnp.concatenate(s_rows, axis=0)
    idx_ref[...] = jnp.concatenate(idx_rows, axis=0)
    gate_ref[...] = top_s / jnp.sum(top_s, axis=0, keepdims=True) * ROUTED_SCALE


def _merge(o_a, o_b, x, mod, ga, gb, wo, n2, wr_hi, wr_lo, rb_col, wsg, wsu, wsd, *, tm):
    bsz, s, d = x.shape
    nt = s // tm
    n = bsz * s
    full = lambda a: pl.BlockSpec(a.shape, lambda b, i: (0,) * a.ndim)
    return pl.pallas_call(
        functools.partial(_merge_kernel, d=d),
        out_shape=(jax.ShapeDtypeStruct((bsz, s, d), F32),
                   jax.ShapeDtypeStruct((n, d // 4), U32),
                   jax.ShapeDtypeStruct((n, d // 4), U32),
                   jax.ShapeDtypeStruct((TOP_K, n), I32),
                   jax.ShapeDtypeStruct((TOP_K, n), F32)),
        grid=(bsz, nt),
        in_specs=[pl.BlockSpec((1, tm, d // 2), lambda b, i: (b, i, 0)),
                  pl.BlockSpec((1, tm, d // 2), lambda b, i: (b, i, 0)),
                  pl.BlockSpec((1, tm, d), lambda b, i: (b, i, 0)),
                  full(mod), full(ga), full(gb), full(wo), full(n2), full(wr_hi), full(wr_lo), full(rb_col),
                  full(wsg), full(wsu), full(wsd)],
        out_specs=(pl.BlockSpec((1, tm, d), lambda b, i: (b, i, 0)),
                   pl.BlockSpec((tm, d // 4), lambda b, i: (b * nt + i, 0)),
                   pl.BlockSpec((tm, d // 4), lambda b, i: (b * nt + i, 0)),
                   pl.BlockSpec((TOP_K, tm), lambda b, i: (0, b * nt + i)),
                   pl.BlockSpec((TOP_K, tm), lambda b, i: (0, b * nt + i))),
        compiler_params=_params("arbitrary", "arbitrary"),
        name="merge",
    )(o_a, o_b, x, mod, ga, gb, wo, n2, wr_hi, wr_lo, rb_col, wsg, wsu, wsd)


def _rank_kernel(idx_ref, rank_ref, cnt_ref, carry_ref):
    tm = idx_ref.shape[1]

    @pl.when(pl.program_id(0) == 0)
    def _():
        carry_ref[...] = jnp.zeros_like(carry_ref)

    idx = idx_ref[...]
    ite = lax.broadcasted_iota(I32, (N_EXPERTS, tm), 0)
    before = (lax.broadcasted_iota(I32, (tm, tm), 0) < lax.broadcasted_iota(I32, (tm, tm), 1)).astype(BF16)
    base = carry_ref[...]
    rows = []
    for k in range(TOP_K):
        oh = ite == idx[k:k + 1, :]
        ohf = jnp.where(oh, 1.0, 0.0)
        cum = jnp.dot(ohf.astype(BF16), before, preferred_element_type=F32)
        rows.append(jnp.sum(jnp.where(oh, cum + base, 0.0), axis=0, keepdims=True))
        base = base + jnp.sum(ohf, axis=1, keepdims=True)
    rank_ref[...] = jnp.concatenate(rows, axis=0).astype(I32)
    carry_ref[...] = base
    cnt_ref[...] = base


def _rank(idx, *, tm):
    n = idx.shape[1]
    return pl.pallas_call(
        _rank_kernel,
        out_shape=(jax.ShapeDtypeStruct((TOP_K, n), I32), jax.ShapeDtypeStruct((N_EXPERTS, 1), F32)),
        grid=(n // tm,),
        in_specs=[pl.BlockSpec((TOP_K, tm), lambda i: (0, i))],
        out_specs=(pl.BlockSpec((TOP_K, tm), lambda i: (0, i)), pl.BlockSpec((N_EXPERTS, 1), lambda i: (0, 0))),
        scratch_shapes=[pltpu.VMEM((N_EXPERTS, 1), F32)],
        compiler_params=_params("arbitrary"),
        name="rank",
    )(idx)


def _pos_kernel(idx_ref, rank_ref, pstart_ref, pos_ref):
    tm = idx_ref.shape[1]
    idx = idx_ref[...]
    ite = lax.broadcasted_iota(I32, (N_EXPERTS, tm), 0)
    pstart = pstart_ref[...]
    rows = [jnp.sum(jnp.where(ite == idx[k:k + 1, :], pstart, 0.0), axis=0, keepdims=True) for k in range(TOP_K)]
    pos_ref[...] = jnp.concatenate(rows, axis=0).astype(I32) + rank_ref[...]


def _pos(idx, rank, pstart_col, *, tm):
    n = idx.shape[1]
    return pl.pallas_call(
        _pos_kernel,
        out_shape=jax.ShapeDtypeStruct((TOP_K, n), I32),
        grid=(n // tm,),
        in_specs=[pl.BlockSpec((TOP_K, tm), lambda i: (0, i)),
                  pl.BlockSpec((TOP_K, tm), lambda i: (0, i)),
                  pl.BlockSpec((N_EXPERTS, 1), lambda i: (0, 0))],
        out_specs=pl.BlockSpec((TOP_K, tm), lambda i: (0, i)),
        compiler_params=_params("arbitrary"),
        name="pos",
    )(idx, rank, pstart_col)


UNIT_CHUNKS = (4, 2, 1)
RING_AHEAD = 4
RING_SLOTS = RING_AHEAD + UNIT_CHUNKS[0]


def _expert_kernel(ps_ref, cnt_ref, tot_ref, wg_ref, wu_ref, wd_ref, xa_hbm, xb_hbm, ya_hbm, yb_hbm,
                   wg_s, wu_s, wd_s, xa_buf, xb_buf, ya_buf, yb_buf, in_sem, out_sem):
    e = pl.program_id(0)
    r = EXPERT_ROWS
    ahead = RING_AHEAD
    total = tot_ref[0]
    first = ps_ref[e] // r
    cnt = cnt_ref[e]
    nch = (cnt + r - 1) // r

    def rows_of(g):
        return pl.ds(pl.multiple_of(g * r, r), r)

    def fetch(g):
        slot = g % RING_SLOTS
        return (pltpu.make_async_copy(xa_hbm.at[rows_of(g)], xa_buf.at[slot], in_sem.at[0, slot]),
                pltpu.make_async_copy(xb_hbm.at[rows_of(g)], xb_buf.at[slot], in_sem.at[1, slot]))

    def flush(g):
        slot = g % RING_SLOTS
        return (pltpu.make_async_copy(ya_buf.at[slot], ya_hbm.at[rows_of(g)], out_sem.at[0, slot]),
                pltpu.make_async_copy(yb_buf.at[slot], yb_hbm.at[rows_of(g)], out_sem.at[1, slot]))

    def start(copies):
        for cp in copies:
            cp.start()

    def wait(copies):
        for cp in copies:
            cp.wait()

    @pl.when(e == 0)
    def _():
        for g in range(ahead):
            @pl.when(g < total)
            def _():
                start(fetch(g))

    @pl.when(nch > 0)
    def _():
        wg_s[...] = wg_ref[0].astype(BF16)
        wu_s[...] = wu_ref[0].astype(BF16)
        wd_s[...] = wd_ref[0].astype(BF16)

        def unit(c0, u):
            for j in range(u):
                g = first + c0 + j
                wait(fetch(g))

                @pl.when(g + ahead < total)
                def _():
                    start(fetch(g + ahead))

                @pl.when(g >= RING_SLOTS)
                def _():
                    wait(flush(g - RING_SLOTS))

            slots = [(first + c0 + j) % RING_SLOTS for j in range(u)]
            packed = jnp.concatenate([jnp.concatenate([xa_buf[sl], xb_buf[sl]], axis=1) for sl in slots], axis=0)
            row = lax.broadcasted_iota(I32, packed.shape, 0)
            packed = jnp.where(row < cnt - c0 * r, packed, jnp.zeros_like(packed))
            x = _unpack_bf16_pairs(packed).astype(BF16)
            gate = jnp.dot(x, wg_s[...], preferred_element_type=F32)
            up = jnp.dot(x, wu_s[...], preferred_element_type=F32)
            y = jnp.dot((_silu(gate) * up).astype(BF16), wd_s[...], preferred_element_type=F32)
            out = _pack_bf16_pairs(y)
            half = out.shape[1] // 2
            for j, sl in enumerate(slots):
                ya_buf[sl] = out[j * r:(j + 1) * r, :half]
                yb_buf[sl] = out[j * r:(j + 1) * r, half:]
                start(flush(first + c0 + j))

        big = UNIT_CHUNKS[0]

        def big_unit(i, carry):
            unit(i * big, big)
            return carry

        lax.fori_loop(0, nch // big, big_unit, 0)
        done = nch // big * big
        for u in UNIT_CHUNKS[1:]:
            @pl.when((nch % (2 * u)) >= u)
            def _():
                unit(done, u)

            done = done + jnp.where((nch % (2 * u)) >= u, u, 0)

    @pl.when(e == pl.num_programs(0) - 1)
    def _():
        for k in range(RING_SLOTS):
            @pl.when(total - 1 - k >= 0)
            def _():
                wait(flush(total - 1 - k))


def _experts(pstart, cnt, total_chunks, xs_a, xs_b, wg, wu, wd):
    rows, hw = xs_a.shape
    r = EXPERT_ROWS
    n_exp, d, f = wg.shape
    hbm = pl.BlockSpec(memory_space=pl.ANY)
    grid_spec = pltpu.PrefetchScalarGridSpec(
        num_scalar_prefetch=3,
        grid=(n_exp,),
        in_specs=[pl.BlockSpec((1, d, f), lambda e, ps, cn, tot: (e, 0, 0)),
                  pl.BlockSpec((1, d, f), lambda e, ps, cn, tot: (e, 0, 0)),
                  pl.BlockSpec((1, f, d), lambda e, ps, cn, tot: (e, 0, 0)),
                  hbm, hbm],
        out_specs=(hbm, hbm),
        scratch_shapes=[pltpu.VMEM((d, f), BF16), pltpu.VMEM((d, f), BF16), pltpu.VMEM((f, d), BF16),
                        pltpu.VMEM((RING_SLOTS, r, hw), U32), pltpu.VMEM((RING_SLOTS, r, hw), U32),
                        pltpu.VMEM((RING_SLOTS, r, hw), U32), pltpu.VMEM((RING_SLOTS, r, hw), U32),
                        pltpu.SemaphoreType.DMA((2, RING_SLOTS)), pltpu.SemaphoreType.DMA((2, RING_SLOTS))],
    )
    return pl.pallas_call(
        _expert_kernel,
        out_shape=(jax.ShapeDtypeStruct((rows, hw), U32), jax.ShapeDtypeStruct((rows, hw), U32)),
        grid_spec=grid_spec,
        compiler_params=_params("arbitrary"),
        name="experts",
    )(pstart, cnt, total_chunks, wg, wu, wd, xs_a, xs_b)


def _sc_scatter_rows(x, idx_flat, n_rows):
    n, w = x.shape
    m = idx_flat.shape[1]
    nwin = n // SC_WINDOW
    reps = m // n
    mesh = plsc.VectorSubcoreMesh(core_axis_name="core", subcore_axis_name="subcore")

    @functools.partial(pl.kernel, out_type=jax.ShapeDtypeStruct((n_rows, w), x.dtype), mesh=mesh,
                       scratch_types=[])
    def scatter_kernel(x_hbm, i_hbm, o_hbm):
        def body(x_vmem, i_vmem):
            pltpu.sync_copy(x_vmem, o_hbm.at[i_vmem.at[0]])

        pltpu.emit_pipeline(
            body,
            grid=(nwin, reps),
            in_specs=[pl.BlockSpec((SC_WINDOW, w), lambda i, k: (i, 0)),
                      pl.BlockSpec((1, SC_WINDOW), lambda i, k: (0, k * nwin + i))],
            out_specs=[],
            core_axis_name=("core", "subcore"),
            dimension_semantics=(pltpu.PARALLEL, pltpu.ARBITRARY),
        )(x_hbm, i_hbm)

    return scatter_kernel(x, idx_flat)


def _sc_gather_rows(src, idx_flat):
    m = idx_flat.shape[1]
    w = src.shape[1]
    mesh = plsc.VectorSubcoreMesh(core_axis_name="core", subcore_axis_name="subcore")

    @functools.partial(pl.kernel, out_type=jax.ShapeDtypeStruct((m, w), src.dtype), mesh=mesh)
    def gather_kernel(x_hbm, i_hbm, o_hbm):
        def body(i_vmem, o_vmem):
            pltpu.sync_copy(x_hbm.at[i_vmem.at[0]], o_vmem)

        pltpu.emit_pipeline(
            body,
            grid=(m // SC_WINDOW,),
            in_specs=[pl.BlockSpec((1, SC_WINDOW), lambda i: (0, i))],
            out_specs=[pl.BlockSpec((SC_WINDOW, w), lambda i: (i, 0))],
            core_axis_name=("core", "subcore"),
            dimension_semantics=(pltpu.PARALLEL,),
        )(i_hbm, o_hbm)

    return gather_kernel(src, idx_flat)


def _finish_kernel(gate_ref, xres_ref, mod_ref, ya_ref, yb_ref, o_ref, *, d, tiles_per_batch):
    tm = gate_ref.shape[1]
    b = pl.program_id(0) // tiles_per_batch
    gates = gate_ref[...].T
    acc = jnp.zeros((tm, d), F32)
    for k in range(TOP_K):
        packed = jnp.concatenate([ya_ref[k], yb_ref[k]], axis=1)
        acc = acc + _unpack_bf16_pairs(packed) * gates[:, k:k + 1]
    o_ref[...] = xres_ref[...] + mod_ref[pl.ds(b, 1), 5 * d:6 * d] * acc


def _finish(gates, xres, mod, yg_a, yg_b, *, tm, tiles_per_batch):
    n, d = xres.shape
    return pl.pallas_call(
        functools.partial(_finish_kernel, d=d, tiles_per_batch=tiles_per_batch),
        out_shape=jax.ShapeDtypeStruct((n, d), F32),
        grid=(n // tm,),
        in_specs=[pl.BlockSpec((TOP_K, tm), lambda i: (0, i)),
                  pl.BlockSpec((tm, d), lambda i: (i, 0)),
                  pl.BlockSpec(mod.shape, lambda i: (0, 0)),
                  pl.BlockSpec((TOP_K, tm, d // 4), lambda i: (0, i, 0)),
                  pl.BlockSpec((TOP_K, tm, d // 4), lambda i: (0, i, 0))],
        out_specs=pl.BlockSpec((tm, d), lambda i: (i, 0)),
        compiler_params=_params("arbitrary"),
        name="finish",
    )(gates, xres, mod, yg_a, yg_b)


_QA_HEAD_ORDER = (0, 4, 1, 5, 2, 6, 3, 7)


def _reorder_qa_heads(a, axis):
    return jnp.concatenate([lax.slice_in_dim(a, h * HEAD_DIM, (h + 1) * HEAD_DIM, axis=axis)
                            for h in _QA_HEAD_ORDER], axis=axis)


def _rope_swap(a):
    quarter = HEAD_DIM // 4
    lead = a.shape[:-1]
    return jnp.flip(a.reshape(lead + (-1, 2, quarter)), axis=-2).reshape(a.shape)


def _rope_tables(s):
    quarter = HEAD_DIM // 4
    t = jnp.arange(s)
    row = (t // GRID_W).astype(F32)
    col = (t % GRID_W).astype(F32)
    freqs = ROPE_BASE ** (-jnp.arange(quarter, dtype=F32) / quarter)
    ar = row[:, None] * freqs[None, :]
    ac = col[:, None] * freqs[None, :]
    cos = jnp.concatenate([jnp.cos(ar), jnp.cos(ar), jnp.cos(ac), jnp.cos(ac)], axis=1)
    sin = jnp.concatenate([-jnp.sin(ar), jnp.sin(ar), -jnp.sin(ac), jnp.sin(ac)], axis=1)
    return jnp.tile(cos, (1, 2)), jnp.tile(sin, (1, 2))


def kernel(x, c, ctx, c_ctx, w_ada, b_ada, norm1, norm2, w_in, q_norm_a, k_norm_a, q_norm_b, k_norm_b, sink_a,
           rpb_b, out_norm_a, out_norm_b, w_out, w_router, router_bias, we_gate, we_up, we_down, ws_gate, ws_up,
           ws_down):
    assert w_ada.shape[0] == 1, "single-layer block"
    bsz, s, d = x.shape
    lc = ctx.shape[1]
    n = bsz * s
    rows = s // GRID_W
    assert s % (NA_QROWS * GRID_W) == 0 and rows >= NA_KROWS and bsz <= 4 and d == 1024

    c8 = jnp.concatenate([c, c_ctx[None, :], jnp.zeros((8 - bsz - 1, d), F32)], axis=0)
    mod = _ada(c8, w_ada[0], b_ada[0][None, :])

    w = w_in[0]
    aw, akw, bw = A_Q_HEADS * HEAD_DIM, A_KV_HEADS * HEAD_DIM, B_HEADS * HEAD_DIM
    cuts = np.cumsum([0, aw, akw, akw, bw, bw, bw])
    qa_w, ka_w, va_w, qb_w, kb_w, vb_w = [w[:, cuts[i]:cuts[i + 1]] for i in range(6)]
    qa_w = _reorder_qa_heads(qa_w, 1)
    w_ext = jnp.concatenate([qa_w, qb_w, kb_w, vb_w, ka_w, va_w, _rope_swap(qa_w), _rope_swap(ka_w)],
                            axis=1).astype(BF16)
    scale = HEAD_DIM ** -0.5 * LOG2E
    pair = lambda g: jnp.tile(g, 2)
    gains = jnp.stack([pair(q_norm_a[0]) * scale, pair(q_norm_b[0]) * scale, pair(k_norm_b[0]), pair(k_norm_a[0]),
                       pair(_rope_swap(q_norm_a[0])) * scale, pair(_rope_swap(k_norm_a[0])),
                       jnp.zeros((LANES,), F32), jnp.zeros((LANES,), F32)])
    bd = jnp.asarray(np.kron(np.eye(2), np.full((HEAD_DIM, HEAD_DIM), 1.0 / HEAD_DIM)), BF16)
    cos_t, sin_t = _rope_tables(s)
    n1 = norm1[0][None, :]
    zq = _inproj(x, mod, n1, w_ext, gains, cos_t, sin_t, bd, mod_row=None, tm=512)
    zc = _inproj(ctx, mod, n1, w_ext, gains, jnp.ones((lc, LANES), F32), jnp.zeros((lc, LANES), F32), bd,
                 mod_row=bsz, tm=lc)

    sink = sink_a[0].astype(F32)
    sink_rep = jnp.concatenate([jnp.broadcast_to(sink[h] * LOG2E, (A_WINDOW, LANES)) for h in _QA_HEAD_ORDER])
    o_a = _attn_a(zq, zc, sink_rep, _attn_a_mask(A_WINDOW))
    o_b = _attn_b(zq, zc, _na_table(rpb_b[0], rows))

    ga = _reorder_qa_heads(out_norm_a[0], 0)[None, :]
    gb = out_norm_b[0][None, :]
    wo = jnp.concatenate([_reorder_qa_heads(w_out[0][:aw], 0), w_out[0][aw:]], axis=0).astype(BF16)
    wr_hi, wr_lo = _split_bf16(w_router[0].T)
    xres, hp_a, hp_b, idx, gates = _merge(o_a, o_b, x, mod, ga, gb, wo, norm2[0][None, :], wr_hi, wr_lo,
                                  router_bias[0][:, None], ws_gate[0].astype(BF16), ws_up[0].astype(BF16),
                                  ws_down[0].astype(BF16), tm=512)

    rank, counts = _rank(idx, tm=512)
    t = EXPERT_ROWS
    cnt = counts[:, 0].astype(I32)
    padded = (cnt + t - 1) // t * t
    pends = jnp.cumsum(padded)
    pstart = pends - padded
    n_rows = n * TOP_K + N_EXPERTS * t
    pos = _pos(idx, rank, pstart.astype(F32)[:, None], tm=512)

    pos_flat = pos.reshape(1, TOP_K * n)
    xs_a = _sc_scatter_rows(hp_a, pos_flat, n_rows)
    xs_b = _sc_scatter_rows(hp_b, pos_flat, n_rows)
    ys_a, ys_b = _experts(pstart, cnt, pends[-1:] // t, xs_a, xs_b, we_gate[0], we_up[0], we_down[0])
    yg_a = _sc_gather_rows(ys_a, pos_flat).reshape(TOP_K, n, d // 4)
    yg_b = _sc_gather_rows(ys_b, pos_flat).reshape(TOP_K, n, d // 4)
    out = _finish(gates, xres.reshape(n, d), mod, yg_a, yg_b, tm=256, tiles_per_batch=s // 256)
    return out.reshape(bsz, s, d)
```

```python
import functools

import numpy as np
import jax
import jax.numpy as jnp
from jax import lax
from jax.experimental import pallas as pl
from jax.experimental.pallas import tpu as pltpu
from jax.experimental.pallas import tpu_sc as plsc

F32 = jnp.float32
BF16 = jnp.bfloat16
I32 = jnp.int32
U32 = jnp.uint32

LANES = 128
HEAD_DIM = 64
HEAD_PAIR = 2 * HEAD_DIM
GRID_W = 64
A_Q_HEADS = 8
A_KV_HEADS = 2
A_WINDOW = 128
B_HEADS = 8
NA_KH = 8
NA_KW = 16
NA_QROWS = 4
NA_KROWS = NA_QROWS + NA_KH
ROPE_BASE = 10000.0
N_EXPERTS = 256
TOP_K = 8
N_GROUPS = 8
TOPK_GROUPS = 4
ROUTED_SCALE = 2.5
LOG2E = 1.4426950408889634
EPS = 1e-6
NEG_INF = -1e30
EXPERT_ROWS = 256
SC_WINDOW = 128
VMEM_LIMIT = 56 * 1024 * 1024

_NT = (((1,), (1,)), ((), ()))


def _params(*sem):
    return pltpu.CompilerParams(dimension_semantics=sem, vmem_limit_bytes=VMEM_LIMIT)


def _silu(v):
    return v * jax.nn.sigmoid(v)


def _rms(v, gain):
    return v * lax.rsqrt(jnp.mean(v * v, axis=-1, keepdims=True) + EPS) * gain


def _pack_bf16_pairs(v):
    n = v.shape[1] // 2
    lo = lax.bitcast_convert_type(v[:, :n].astype(BF16).astype(F32), U32) >> 16
    hi = lax.bitcast_convert_type(v[:, n:].astype(BF16).astype(F32), U32) & jnp.uint32(0xFFFF0000)
    return hi | lo


def _split_bf16(v):
    hi = lax.bitcast_convert_type(lax.bitcast_convert_type(v, U32) & jnp.uint32(0xFFFF0000), F32)
    return hi.astype(BF16), (v - hi).astype(BF16)


def _unpack_bf16_pairs(w):
    lo = lax.bitcast_convert_type(w << 16, F32)
    hi = lax.bitcast_convert_type(w & jnp.uint32(0xFFFF0000), F32)
    return jnp.concatenate([lo, hi], axis=1)


def _ada_kernel(c_ref, w_ref, b_ref, o_ref):
    a = _silu(c_ref[...])
    o_ref[...] = jnp.dot(a, w_ref[...], preferred_element_type=F32,
                         precision=lax.Precision.HIGHEST) + b_ref[...]


def _ada(c8, w, b):
    d, n = w.shape
    bn = n // 4
    return pl.pallas_call(
        _ada_kernel,
        out_shape=jax.ShapeDtypeStruct((8, n), F32),
        grid=(n // bn,),
        in_specs=[pl.BlockSpec((8, d), lambda j: (0, 0)),
                  pl.BlockSpec((d, bn), lambda j: (0, j)),
                  pl.BlockSpec((1, bn), lambda j: (0, j))],
        out_specs=pl.BlockSpec((8, bn), lambda j: (0, j)),
        compiler_params=_params("arbitrary"),
        name="ada",
    )(c8, w, b)


_QA, _QB, _KB, _VB, _KA, _VA = 0, 4, 8, 12, 16, 17
_OUT_BLOCKS = 18


def _inproj_kernel(x_ref, mod_ref, n1_ref, w_ref, g_ref, cos_ref, sin_ref, bd_ref, o_ref, *, mod_row, d):
    b = pl.program_id(0) if mod_row is None else mod_row
    xn = _rms(x_ref[0], n1_ref[...])
    sh = mod_ref[pl.ds(b, 1), 0:d]
    sc = mod_ref[pl.ds(b, 1), d:2 * d]
    h = (xn * (1.0 + sc) + sh).astype(BF16)
    z = jnp.dot(h, w_ref[...], preferred_element_type=F32)
    bd = bd_ref[...]
    cos = cos_ref[...]
    sin = sin_ref[...]

    def blk(j):
        return z[:, j * LANES:(j + 1) * LANES]

    def head_rinv(zb):
        ms = jnp.dot((zb * zb).astype(BF16), bd, preferred_element_type=F32)
        return lax.rsqrt(ms + EPS)

    def put(j, v):
        o_ref[0, :, j * LANES:(j + 1) * LANES] = v.astype(BF16)

    quarter = HEAD_DIM // 4
    first_half = (lax.broadcasted_iota(I32, (1, LANES), 1) % (2 * quarter)) < quarter

    def roped(j, g_row):
        zb = blk(j)
        zn = zb * head_rinv(zb) * g_ref[g_row:g_row + 1, :]
        partner = jnp.where(first_half, pltpu.roll(zn, LANES - quarter, 1), pltpu.roll(zn, quarter, 1))
        put(j, zn * cos + partner * sin)

    def normed(j, g_row):
        zb = blk(j)
        put(j, zb * head_rinv(zb) * g_ref[g_row:g_row + 1, :])

    for j in range(4):
        roped(_QA + j, 0)
        normed(_QB + j, 1)
        normed(_KB + j, 2)
        put(_VB + j, blk(_VB + j))
    roped(_KA, 3)
    put(_VA, blk(_VA))


def _inproj(x, mod, n1, w_ext, gains, cos_t, sin_t, bd, *, mod_row, tm):
    bsz, s, d = x.shape
    kern = functools.partial(_inproj_kernel, mod_row=mod_row, d=d)
    return pl.pallas_call(
        kern,
        out_shape=jax.ShapeDtypeStruct((bsz, s, _OUT_BLOCKS * LANES), BF16),
        grid=(bsz, s // tm),
        in_specs=[pl.BlockSpec((1, tm, d), lambda b, i: (b, i, 0)),
                  pl.BlockSpec(mod.shape, lambda b, i: (0, 0)),
                  pl.BlockSpec((1, d), lambda b, i: (0, 0)),
                  pl.BlockSpec(w_ext.shape, lambda b, i: (0, 0)),
                  pl.BlockSpec(gains.shape, lambda b, i: (0, 0)),
                  pl.BlockSpec((tm, LANES), lambda b, i: (i, 0)),
                  pl.BlockSpec((tm, LANES), lambda b, i: (i, 0)),
                  pl.BlockSpec(bd.shape, lambda b, i: (0, 0))],
        out_specs=pl.BlockSpec((1, tm, _OUT_BLOCKS * LANES), lambda b, i: (b, i, 0)),
        compiler_params=_params("arbitrary", "arbitrary"),
        name="inproj",
    )(x, mod, n1, w_ext, gains, cos_t, sin_t, bd)


def _split_pair(qp, lo):
    zero = jnp.zeros_like(qp)
    return jnp.concatenate([jnp.where(lo, qp, zero), jnp.where(lo, zero, qp)], axis=0)


def _softmax_pv(s_parts, v_parts, sink_rep):
    chunks = [s[:, c * LANES:(c + 1) * LANES] for s in s_parts for c in range(s.shape[1] // LANES)]
    m = jnp.max(functools.reduce(jnp.maximum, chunks), axis=-1, keepdims=True)
    m_rep = jnp.broadcast_to(m, (m.shape[0], LANES))
    if sink_rep is not None:
        m_rep = jnp.maximum(m_rep, sink_rep)
    acc = None
    for s, v in zip(s_parts, v_parts):
        p = jnp.concatenate([jnp.exp2(s[:, c * LANES:(c + 1) * LANES] - m_rep)
                             for c in range(s.shape[1] // LANES)], axis=1).astype(BF16)
        v_ext = jnp.concatenate([v, jnp.ones_like(v)], axis=1)
        o = jnp.dot(p, v_ext, preferred_element_type=F32)
        acc = o if acc is None else acc + o
    l_rep = acc[:, LANES:]
    if sink_rep is not None:
        l_rep = l_rep + jnp.exp2(sink_rep - m_rep)
    return acc[:, :LANES] * (1.0 / l_rep)


A_BLOCKS_PER_STEP = 4


def _attn_a_kernel(q_ref, *refs):
    nq = A_BLOCKS_PER_STEP
    k_refs, v_refs = refs[:nq + 2], refs[nq + 2:2 * nq + 4]
    kx_ref, vx_ref, sink_ref = refs[2 * nq + 4:2 * nq + 7]
    mask_refs = refs[2 * nq + 7:3 * nq + 7]
    o_ref = refs[-1]
    tq = A_WINDOW
    lo = lax.broadcasted_iota(I32, (tq, LANES), 1) < HEAD_DIM
    for h in range(nq):
        q = q_ref[0, h * tq:(h + 1) * tq]
        qs = jnp.concatenate([_split_pair(q[:, p * LANES:(p + 1) * LANES], lo) for p in range(4)], axis=0)
        k = jnp.concatenate([r[0] for r in k_refs[h:h + 3]], axis=0)
        v = jnp.concatenate([r[0] for r in v_refs[h:h + 3]], axis=0)
        s_loc = lax.dot_general(qs, k, _NT, preferred_element_type=F32) + mask_refs[h][0]
        s_ctx = lax.dot_general(qs, kx_ref[0], _NT, preferred_element_type=F32)
        o = _softmax_pv([s_loc, s_ctx], [v, vx_ref[0]], sink_ref[...])
        for p in range(4):
            o_lo = o[(2 * p) * tq:(2 * p + 1) * tq]
            o_hi = o[(2 * p + 1) * tq:(2 * p + 2) * tq]
            o_ref[0, h * tq:(h + 1) * tq, p * LANES:(p + 1) * LANES] = jnp.where(lo, o_lo, o_hi).astype(BF16)


def _attn_a(zq, zc, sink_rep, mask):
    bsz, s, _ = zq.shape
    lc = zc.shape[1]
    tq = A_WINDOW
    nq = A_BLOCKS_PER_STEP
    nblk = s // tq
    ka, va = _KA, _VA

    def kv_spec(col, shift):
        return pl.BlockSpec((1, tq, LANES), lambda b, j: (b, jnp.clip(nq * j + shift, 0, nblk - 1), col))

    def mask_spec(h):
        def variant(j):
            blk = nq * j + h
            return jnp.where(blk == 0, 0, jnp.where(blk == nblk - 1, 2, 1))
        return pl.BlockSpec((1,) + mask.shape[1:], lambda b, j: (variant(j), 0, 0))

    shifts = range(-1, nq + 1)
    return pl.pallas_call(
        _attn_a_kernel,
        out_shape=jax.ShapeDtypeStruct((bsz, s, 4 * LANES), BF16),
        grid=(bsz, nblk // nq),
        in_specs=([pl.BlockSpec((1, nq * tq, 4 * LANES), lambda b, j: (b, j, 0))]
                  + [kv_spec(ka, sh) for sh in shifts] + [kv_spec(va, sh) for sh in shifts]
                  + [pl.BlockSpec((1, lc, LANES), lambda b, j: (b, 0, ka)),
                     pl.BlockSpec((1, lc, LANES), lambda b, j: (b, 0, va)),
                     pl.BlockSpec(sink_rep.shape, lambda b, j: (0, 0))]
                  + [mask_spec(h) for h in range(nq)]),
        out_specs=pl.BlockSpec((1, nq * tq, 4 * LANES), lambda b, j: (b, j, 0)),
        compiler_params=_params("arbitrary", "arbitrary"),
        name="attn_a",
    )(*([zq] * (2 * nq + 5) + [zc, zc, sink_rep] + [mask] * nq))


def _attn_a_mask(tq):
    qi = np.arange(tq)[:, None]
    kj = np.arange(3 * tq)[None, :]
    ok = (kj >= qi) & (kj <= qi + 2 * tq)
    variants = [ok & (kj >= tq), ok, ok & (kj < 2 * tq)]
    m = np.stack([np.tile(np.where(v, 0.0, NEG_INF).astype(np.float32), (A_Q_HEADS, 1)) for v in variants])
    return jnp.asarray(m)


NA_GROUPS_PER_STEP = 2


def _attn_b_kernel(q_ref, *refs):
    ngs = NA_GROUPS_PER_STEP
    k_refs, v_refs = refs[:3 * ngs], refs[3 * ngs:6 * ngs]
    kx_ref, vx_ref = refs[6 * ngs:6 * ngs + 2]
    tab_refs = refs[6 * ngs + 2:7 * ngs + 2]
    o_ref = refs[-1]
    tq = NA_QROWS * GRID_W
    lo = lax.broadcasted_iota(I32, (tq, LANES), 1) < HEAD_DIM
    for g in range(ngs):
        rows_g = slice(g * tq, (g + 1) * tq)
        for p in range(4):
            sl = slice(p * LANES, (p + 1) * LANES)
            qs = _split_pair(q_ref[0, rows_g, sl], lo)
            k = jnp.concatenate([r[0, :, sl] for r in k_refs[3 * g:3 * g + 3]], axis=0)
            v = jnp.concatenate([r[0, :, sl] for r in v_refs[3 * g:3 * g + 3]], axis=0)
            s_loc = lax.dot_general(qs, k, _NT, preferred_element_type=F32) + tab_refs[g][0, p]
            s_ctx = lax.dot_general(qs, kx_ref[0, :, sl], _NT, preferred_element_type=F32)
            o = _softmax_pv([s_loc, s_ctx], [v, vx_ref[0, :, sl]], None)
            o_ref[0, rows_g, sl] = jnp.where(lo, o[:tq], o[tq:]).astype(BF16)


def _attn_b(zq, zc, table):
    bsz, s, _ = zq.shape
    lc = zc.shape[1]
    tq = NA_QROWS * GRID_W
    ngs = NA_GROUPS_PER_STEP
    ng = s // tq
    qb, kb, vb = _QB // 4, _KB // 4, _VB // 4

    def kv_spec(col, g, off):
        return pl.BlockSpec((1, tq, 4 * LANES),
                            lambda i, b: (b, jnp.clip(ngs * i + g - 1, 0, ng - 3) + off, col))

    def tab_spec(g):
        def variant(i):
            grp = ngs * i + g
            return jnp.where(grp == 0, 0, jnp.where(grp == ng - 1, 2, 1))
        return pl.BlockSpec((1,) + table.shape[1:], lambda i, b: (variant(i), 0, 0, 0))

    kv_slots = [(g, off) for g in range(ngs) for off in range(3)]
    return pl.pallas_call(
        _attn_b_kernel,
        out_shape=jax.ShapeDtypeStruct((bsz, s, 4 * LANES), BF16),
        grid=(ng // ngs, bsz),
        in_specs=([pl.BlockSpec((1, ngs * tq, 4 * LANES), lambda i, b: (b, i, qb))]
                  + [kv_spec(kb, g, off) for g, off in kv_slots] + [kv_spec(vb, g, off) for g, off in kv_slots]
                  + [pl.BlockSpec((1, lc, 4 * LANES), lambda i, b: (b, 0, kb)),
                     pl.BlockSpec((1, lc, 4 * LANES), lambda i, b: (b, 0, vb))]
                  + [tab_spec(g) for g in range(ngs)]),
        out_specs=pl.BlockSpec((1, ngs * tq, 4 * LANES), lambda i, b: (b, i, 0)),
        compiler_params=_params("arbitrary", "arbitrary"),
        name="attn_b",
    )(*([zq] * (6 * ngs + 1) + [zc, zc] + [table] * ngs))


def _na_table(rpb, rows):
    ng = rows // NA_QROWS
    nq, nk = NA_QROWS * GRID_W, NA_KROWS * GRID_W
    qc = np.arange(GRID_W)
    kc = np.arange(GRID_W)
    ws = np.clip(qc - NA_KW // 2, 0, GRID_W - NA_KW)
    valid_c = (kc[None, :] >= ws[:, None]) & (kc[None, :] < ws[:, None] + NA_KW)
    dc = np.clip(kc[None, :] - qc[:, None], -(NA_KW - 1), NA_KW - 1) + (NA_KW - 1)
    c_sel = (dc[..., None] == np.arange(2 * NA_KW - 1)) & valid_c[..., None]
    tiles = jnp.einsum('hab,uvb->huav', rpb.astype(F32), jnp.asarray(c_sel, F32), precision=lax.Precision.HIGHEST)
    tiles = jnp.where(jnp.asarray(valid_c)[None, :, None, :], tiles * LOG2E, NEG_INF)
    tiles = tiles.reshape(B_HEADS, GRID_W, (2 * NA_KH - 1) * GRID_W)

    def masked(n_key_rows):
        return jnp.full((B_HEADS, GRID_W, n_key_rows * GRID_W), NEG_INF, F32)

    tabs = []
    for i in (0, 1, ng - 1):
        start = int(np.clip(NA_QROWS * i - NA_KH // 2, 0, rows - NA_KROWS))
        q_rows = []
        for qr in range(NA_QROWS):
            r = NA_QROWS * i + qr
            rs = int(np.clip(r - NA_KH // 2, 0, rows - NA_KH))
            y0 = rs - start
            d0 = rs - r + (NA_KH - 1)
            q_rows.append(jnp.concatenate([masked(y0), tiles[:, :, d0 * GRID_W:(d0 + NA_KH) * GRID_W],
                                           masked(NA_KROWS - NA_KH - y0)], axis=-1))
        tabs.append(jnp.concatenate(q_rows, axis=-2))
    return jnp.stack(tabs).reshape(3, B_HEADS // 2, 2 * nq, nk)


def _merge_kernel(oa_ref, ob_ref, x_ref, mod_ref, ga_ref, gb_ref, wo_ref, n2_ref, wrh_ref, wrl_ref, rb_ref,
                  wsg_ref, wsu_ref, wsd_ref, xres_ref, hpa_ref, hpb_ref, idx_ref, gate_ref, *, d):
    b = pl.program_id(0)
    tm = x_ref.shape[1]

    def mod(k):
        return mod_ref[pl.ds(b, 1), k * d:(k + 1) * d]

    na = _rms(oa_ref[0].astype(F32), ga_ref[...])
    nb = _rms(ob_ref[0].astype(F32), gb_ref[...])
    cat = jnp.concatenate([na, nb], axis=1).astype(BF16)
    y = jnp.dot(cat, wo_ref[...], preferred_element_type=F32)
    x1 = x_ref[0] + mod(2) * y
    h2 = _rms(x1, n2_ref[...]) * (1.0 + mod(4)) + mod(3)

    hb = h2.astype(BF16)
    act = _silu(jnp.dot(hb, wsg_ref[...], preferred_element_type=F32)) * jnp.dot(hb, wsu_ref[...],
                                                                                 preferred_element_type=F32)
    shared = jnp.dot(act.astype(BF16), wsd_ref[...], preferred_element_type=F32)
    xres_ref[0] = x1 + mod(5) * shared
    packed = _pack_bf16_pairs(h2)
    hpa_ref[...] = packed[:, :d // 4]
    hpb_ref[...] = packed[:, d // 4:]

    h_hi, h_lo = _split_bf16(h2)
    logits = (lax.dot_general(wrh_ref[...], h_hi, _NT, preferred_element_type=F32)
              + lax.dot_general(wrh_ref[...], h_lo, _NT, preferred_element_type=F32)
              + lax.dot_general(wrl_ref[...], h_hi, _NT, preferred_element_type=F32))
    scores = jax.nn.sigmoid(logits)
    sel = scores + rb_ref[...]
    per = N_EXPERTS // N_GROUPS
    g3 = sel.reshape(N_GROUPS, per, tm)
    it3 = lax.broadcasted_iota(I32, (N_GROUPS, per, tm), 1)
    m1 = jnp.max(g3, axis=1, keepdims=True)
    first = jnp.min(jnp.where(g3 == m1, it3, per), axis=1, keepdims=True)
    m2 = jnp.max(jnp.where(it3 == first, -jnp.inf, g3), axis=1, keepdims=True)
    gscore = (m1 + m2).reshape(N_GROUPS, tm)

    itg = lax.broadcasted_iota(I32, (N_GROUPS, tm), 0)
    gsel = jnp.zeros((N_GROUPS, tm), F32)
    cur = gscore
    for _ in range(TOPK_GROUPS):
        mx = jnp.max(cur, axis=0, keepdims=True)
        fi = jnp.min(jnp.where(cur == mx, itg, N_GROUPS), axis=0, keepdims=True)
        pick = itg == fi
        gsel = jnp.where(pick, 1.0, gsel)
        cur = jnp.where(pick, -jnp.inf, cur)
    emask = jnp.broadcast_to(gsel.reshape(N_GROUPS, 1, tm), (N_GROUPS, per, tm)).reshape(N_EXPERTS, tm) > 0.5

    ite = lax.broadcasted_iota(I32, (N_EXPERTS, tm), 0)
    cur = jnp.where(emask, sel, NEG_INF)
    idx_rows, s_rows = [], []
    for _ in range(TOP_K):
        mx = jnp.max(cur, axis=0, keepdims=True)
        fi = jnp.min(jnp.where(cur == mx, ite, N_EXPERTS), axis=0, keepdims=True)
        pick = ite == fi
        idx_rows.append(fi)
        s_rows.append(jnp.sum(jnp.where(pick, scores, 0.0), axis=0, keepdims=True))
        cur = jnp.where(pick, -jnp.inf, cur)
    top_s = jnp.concatenate(s_rows, axis=0)
    idx_ref[...] = jnp.concatenate(idx_rows, axis=0)
    gate_ref[...] = top_s / jnp.sum(top_s, axis=0, keepdims=True) * ROUTED_SCALE


def _merge(o_a, o_b, x, mod, ga, gb, wo, n2, wr_hi, wr_lo, rb_col, wsg, wsu, wsd, *, tm):
    bsz, s, d = x.shape
    nt = s // tm
    n = bsz * s
    full = lambda a: pl.BlockSpec(a.shape, lambda b, i: (0,) * a.ndim)
    return pl.pallas_call(
        functools.partial(_merge_kernel, d=d),
        out_shape=(jax.ShapeDtypeStruct((bsz, s, d), F32),
                   jax.ShapeDtypeStruct((n, d // 4), U32),
                   jax.ShapeDtypeStruct((n, d // 4), U32),
                   jax.ShapeDtypeStruct((TOP_K, n), I32),
                   jax.ShapeDtypeStruct((TOP_K, n), F32)),
        grid=(bsz, nt),
        in_specs=[pl.BlockSpec((1, tm, d // 2), lambda b, i: (b, i, 0)),
                  pl.BlockSpec((1, tm, d // 2), lambda b, i: (b, i, 0)),
                  pl.BlockSpec((1, tm, d), lambda b, i: (b, i, 0)),
                  full(mod), full(ga), full(gb), full(wo), full(n2), full(wr_hi), full(wr_lo), full(rb_col),
                  full(wsg), full(wsu), full(wsd)],
        out_specs=(pl.BlockSpec((1, tm, d), lambda b, i: (b, i, 0)),
                   pl.BlockSpec((tm, d // 4), lambda b, i: (b * nt + i, 0)),
                   pl.BlockSpec((tm, d // 4), lambda b, i: (b * nt + i, 0)),
                   pl.BlockSpec((TOP_K, tm), lambda b, i: (0, b * nt + i)),
                   pl.BlockSpec((TOP_K, tm), lambda b, i: (0, b * nt + i))),
        compiler_params=_params("arbitrary", "arbitrary"),
        name="merge",
    )(o_a, o_b, x, mod, ga, gb, wo, n2, wr_hi, wr_lo, rb_col, wsg, wsu, wsd)


def _rank_kernel(idx_ref, rank_ref, cnt_ref, carry_ref):
    tm = idx_ref.shape[1]

    @pl.when(pl.program_id(0) == 0)
    def _():
        carry_ref[...] = jnp.zeros_like(carry_ref)

    idx = idx_ref[...]
    ite = lax.broadcasted_iota(I32, (N_EXPERTS, tm), 0)
    before = (lax.broadcasted_iota(I32, (tm, tm), 0) < lax.broadcasted_iota(I32, (tm, tm), 1)).astype(BF16)
    hits = [ite == idx[k:k + 1, :] for k in range(TOP_K)]
    routed = jnp.where(functools.reduce(jnp.logical_or, hits), 1.0, 0.0)
    base = carry_ref[...]
    ahead = jnp.dot(routed.astype(BF16), before, preferred_element_type=F32) + base
    rows = [jnp.sum(jnp.where(hit, ahead, 0.0), axis=0, keepdims=True) for hit in hits]
    rank_ref[...] = jnp.concatenate(rows, axis=0).astype(I32)
    total = base + jnp.sum(routed, axis=1, keepdims=True)
    carry_ref[...] = total
    cnt_ref[...] = total


def _rank(idx, *, tm):
    n = idx.shape[1]
    return pl.pallas_call(
        _rank_kernel,
        out_shape=(jax.ShapeDtypeStruct((TOP_K, n), I32), jax.ShapeDtypeStruct((N_EXPERTS, 1), F32)),
        grid=(n // tm,),
        in_specs=[pl.BlockSpec((TOP_K, tm), lambda i: (0, i))],
        out_specs=(pl.BlockSpec((TOP_K, tm), lambda i: (0, i)), pl.BlockSpec((N_EXPERTS, 1), lambda i: (0, 0))),
        scratch_shapes=[pltpu.VMEM((N_EXPERTS, 1), F32)],
        compiler_params=_params("arbitrary"),
        name="rank",
    )(idx)


def _pos_kernel(idx_ref, rank_ref, pstart_ref, pos_ref):
    tm = idx_ref.shape[1]
    idx = idx_ref[...]
    ite = lax.broadcasted_iota(I32, (N_EXPERTS, tm), 0)
    pstart = pstart_ref[...]
    rows = [jnp.sum(jnp.where(ite == idx[k:k + 1, :], pstart, 0.0), axis=0, keepdims=True) for k in range(TOP_K)]
    pos_ref[...] = jnp.concatenate(rows, axis=0).astype(I32) + rank_ref[...]


def _pos(idx, rank, pstart_col, *, tm):
    n = idx.shape[1]
    return pl.pallas_call(
        _pos_kernel,
        out_shape=jax.ShapeDtypeStruct((TOP_K, n), I32),
        grid=(n // tm,),
        in_specs=[pl.BlockSpec((TOP_K, tm), lambda i: (0, i)),
                  pl.BlockSpec((TOP_K, tm), lambda i: (0, i)),
                  pl.BlockSpec((N_EXPERTS, 1), lambda i: (0, 0))],
        out_specs=pl.BlockSpec((TOP_K, tm), lambda i: (0, i)),
        compiler_params=_params("arbitrary"),
        name="pos",
    )(idx, rank, pstart_col)


UNIT_CHUNKS = (4, 2, 1)
RING_AHEAD = 4
RING_SLOTS = RING_AHEAD + UNIT_CHUNKS[0]


def _expert_kernel(ps_ref, cnt_ref, tot_ref, wg_ref, wu_ref, wd_ref, xa_hbm, xb_hbm, ya_hbm, yb_hbm,
                   wg_s, wu_s, wd_s, xa_buf, xb_buf, ya_buf, yb_buf, in_sem, out_sem):
    e = pl.program_id(0)
    r = EXPERT_ROWS
    ahead = RING_AHEAD
    total = tot_ref[0]
    first = ps_ref[e] // r
    cnt = cnt_ref[e]
    nch = (cnt + r - 1) // r

    def rows_of(g):
        return pl.ds(pl.multiple_of(g * r, r), r)

    def fetch(g):
        slot = g % RING_SLOTS
        return (pltpu.make_async_copy(xa_hbm.at[rows_of(g)], xa_buf.at[slot], in_sem.at[0, slot]),
                pltpu.make_async_copy(xb_hbm.at[rows_of(g)], xb_buf.at[slot], in_sem.at[1, slot]))

    def flush(g):
        slot = g % RING_SLOTS
        return (pltpu.make_async_copy(ya_buf.at[slot], ya_hbm.at[rows_of(g)], out_sem.at[0, slot]),
                pltpu.make_async_copy(yb_buf.at[slot], yb_hbm.at[rows_of(g)], out_sem.at[1, slot]))

    def start(copies):
        for cp in copies:
            cp.start()

    def wait(copies):
        for cp in copies:
            cp.wait()

    @pl.when(e == 0)
    def _():
        for g in range(ahead):
            @pl.when(g < total)
            def _():
                start(fetch(g))

    @pl.when(nch > 0)
    def _():
        wg_s[...] = wg_ref[0].astype(BF16)
        wu_s[...] = wu_ref[0].astype(BF16)
        wd_s[...] = wd_ref[0].astype(BF16)

        def unit(c0, u):
            for j in range(u):
                g = first + c0 + j
                wait(fetch(g))

                @pl.when(g + ahead < total)
                def _():
                    start(fetch(g + ahead))

                @pl.when(g >= RING_SLOTS)
                def _():
                    wait(flush(g - RING_SLOTS))

            slots = [(first + c0 + j) % RING_SLOTS for j in range(u)]
            packed = jnp.concatenate([jnp.concatenate([xa_buf[sl], xb_buf[sl]], axis=1) for sl in slots], axis=0)
            row = lax.broadcasted_iota(I32, packed.shape, 0)
            packed = jnp.where(row < cnt - c0 * r, packed, jnp.zeros_like(packed))
            x = _unpack_bf16_pairs(packed).astype(BF16)
            gate = jnp.dot(x, wg_s[...], preferred_element_type=F32)
            up = jnp.dot(x, wu_s[...], preferred_element_type=F32)
            y = jnp.dot((_silu(gate) * up).astype(BF16), wd_s[...], preferred_element_type=F32)
            out = _pack_bf16_pairs(y)
            half = out.shape[1] // 2
            for j, sl in enumerate(slots):
                ya_buf[sl] = out[j * r:(j + 1) * r, :half]
                yb_buf[sl] = out[j * r:(j + 1) * r, half:]
                start(flush(first + c0 + j))

        big = UNIT_CHUNKS[0]

        def big_unit(i, carry):
            unit(i * big, big)
            return carry

        lax.fori_loop(0, nch // big, big_unit, 0)
        done = nch // big * big
        for u in UNIT_CHUNKS[1:]:
            @pl.when((nch % (2 * u)) >= u)
            def _():
                unit(done, u)

            done = done + jnp.where((nch % (2 * u)) >= u, u, 0)

    @pl.when(e == pl.num_programs(0) - 1)
    def _():
        for k in range(RING_SLOTS):
            @pl.when(total - 1 - k >= 0)
            def _():
                wait(flush(total - 1 - k))


def _experts(pstart, cnt, total_chunks, xs_a, xs_b, wg, wu, wd):
    rows, hw = xs_a.shape
    r = EXPERT_ROWS
    n_exp, d, f = wg.shape
    hbm = pl.BlockSpec(memory_space=pl.ANY)
    grid_spec = pltpu.PrefetchScalarGridSpec(
        num_scalar_prefetch=3,
        grid=(n_exp,),
        in_specs=[pl.BlockSpec((1, d, f), lambda e, ps, cn, tot: (e, 0, 0)),
                  pl.BlockSpec((1, d, f), lambda e, ps, cn, tot: (e, 0, 0)),
                  pl.BlockSpec((1, f, d), lambda e, ps, cn, tot: (e, 0, 0)),
                  hbm, hbm],
        out_specs=(hbm, hbm),
        scratch_shapes=[pltpu.VMEM((d, f), BF16), pltpu.VMEM((d, f), BF16), pltpu.VMEM((f, d), BF16),
                        pltpu.VMEM((RING_SLOTS, r, hw), U32), pltpu.VMEM((RING_SLOTS, r, hw), U32),
                        pltpu.VMEM((RING_SLOTS, r, hw), U32), pltpu.VMEM((RING_SLOTS, r, hw), U32),
                        pltpu.SemaphoreType.DMA((2, RING_SLOTS)), pltpu.SemaphoreType.DMA((2, RING_SLOTS))],
    )
    return pl.pallas_call(
        _expert_kernel,
        out_shape=(jax.ShapeDtypeStruct((rows, hw), U32), jax.ShapeDtypeStruct((rows, hw), U32)),
        grid_spec=grid_spec,
        compiler_params=_params("arbitrary"),
        name="experts",
    )(pstart, cnt, total_chunks, wg, wu, wd, xs_a, xs_b)


def _sc_scatter_rows(x, idx_flat, n_rows):
    n, w = x.shape
    m = idx_flat.shape[1]
    nwin = n // SC_WINDOW
    reps = m // n
    mesh = plsc.VectorSubcoreMesh(core_axis_name="core", subcore_axis_name="subcore")

    @functools.partial(pl.kernel, out_type=jax.ShapeDtypeStruct((n_rows, w), x.dtype), mesh=mesh,
                       scratch_types=[])
    def scatter_kernel(x_hbm, i_hbm, o_hbm):
        def body(x_vmem, i_vmem):
            pltpu.sync_copy(x_vmem, o_hbm.at[i_vmem.at[0]])

        pltpu.emit_pipeline(
            body,
            grid=(nwin, reps),
            in_specs=[pl.BlockSpec((SC_WINDOW, w), lambda i, k: (i, 0)),
                      pl.BlockSpec((1, SC_WINDOW), lambda i, k: (0, k * nwin + i))],
            out_specs=[],
            core_axis_name=("core", "subcore"),
            dimension_semantics=(pltpu.PARALLEL, pltpu.ARBITRARY),
        )(x_hbm, i_hbm)

    return scatter_kernel(x, idx_flat)


def _sc_gather_rows(src, idx_flat):
    m = idx_flat.shape[1]
    w = src.shape[1]
    mesh = plsc.VectorSubcoreMesh(core_axis_name="core", subcore_axis_name="subcore")

    @functools.partial(pl.kernel, out_type=jax.ShapeDtypeStruct((m, w), src.dtype), mesh=mesh)
    def gather_kernel(x_hbm, i_hbm, o_hbm):
        def body(i_vmem, o_vmem):
            pltpu.sync_copy(x_hbm.at[i_vmem.at[0]], o_vmem)

        pltpu.emit_pipeline(
            body,
            grid=(m // SC_WINDOW,),
            in_specs=[pl.BlockSpec((1, SC_WINDOW), lambda i: (0, i))],
            out_specs=[pl.BlockSpec((SC_WINDOW, w), lambda i: (i, 0))],
            core_axis_name=("core", "subcore"),
            dimension_semantics=(pltpu.PARALLEL,),
        )(i_hbm, o_hbm)

    return gather_kernel(src, idx_flat)


def _finish_kernel(gate_ref, xres_ref, mod_ref, ya_ref, yb_ref, o_ref, *, d, tiles_per_batch):
    tm = gate_ref.shape[1]
    b = pl.program_id(0) // tiles_per_batch
    gates = gate_ref[...].T
    acc = jnp.zeros((tm, d), F32)
    for k in range(TOP_K):
        packed = jnp.concatenate([ya_ref[k], yb_ref[k]], axis=1)
        acc = acc + _unpack_bf16_pairs(packed) * gates[:, k:k + 1]
    o_ref[...] = xres_ref[...] + mod_ref[pl.ds(b, 1), 5 * d:6 * d] * acc


def _finish(gates, xres, mod, yg_a, yg_b, *, tm, tiles_per_batch):
    n, d = xres.shape
    return pl.pallas_call(
        functools.partial(_finish_kernel, d=d, tiles_per_batch=tiles_per_batch),
        out_shape=jax.ShapeDtypeStruct((n, d), F32),
        grid=(n // tm,),
        in_specs=[pl.BlockSpec((TOP_K, tm), lambda i: (0, i)),
                  pl.BlockSpec((tm, d), lambda i: (i, 0)),
                  pl.BlockSpec(mod.shape, lambda i: (0, 0)),
                  pl.BlockSpec((TOP_K, tm, d // 4), lambda i: (0, i, 0)),
                  pl.BlockSpec((TOP_K, tm, d // 4), lambda i: (0, i, 0))],
        out_specs=pl.BlockSpec((tm, d), lambda i: (i, 0)),
        compiler_params=_params("arbitrary"),
        name="finish",
    )(gates, xres, mod, yg_a, yg_b)


_QA_HEAD_ORDER = (0, 4, 1, 5, 2, 6, 3, 7)


def _reorder_qa_heads(a, axis):
    return jnp.concatenate([lax.slice_in_dim(a, h * HEAD_DIM, (h + 1) * HEAD_DIM, axis=axis)
                            for h in _QA_HEAD_ORDER], axis=axis)


def _rope_tables(s):
    quarter = HEAD_DIM // 4
    t = jnp.arange(s)
    row = (t // GRID_W).astype(F32)
    col = (t % GRID_W).astype(F32)
    freqs = ROPE_BASE ** (-jnp.arange(quarter, dtype=F32) / quarter)
    ar = row[:, None] * freqs[None, :]
    ac = col[:, None] * freqs[None, :]
    cos = jnp.concatenate([jnp.cos(ar), jnp.cos(ar), jnp.cos(ac), jnp.cos(ac)], axis=1)
    sin = jnp.concatenate([-jnp.sin(ar), jnp.sin(ar), -jnp.sin(ac), jnp.sin(ac)], axis=1)
    return jnp.tile(cos, (1, 2)), jnp.tile(sin, (1, 2))


def kernel(x, c, ctx, c_ctx, w_ada, b_ada, norm1, norm2, w_in, q_norm_a, k_norm_a, q_norm_b, k_norm_b, sink_a,
           rpb_b, out_norm_a, out_norm_b, w_out, w_router, router_bias, we_gate, we_up, we_down, ws_gate, ws_up,
           ws_down):
    assert w_ada.shape[0] == 1, "single-layer block"
    bsz, s, d = x.shape
    lc = ctx.shape[1]
    n = bsz * s
    rows = s // GRID_W
    assert s % (NA_QROWS * GRID_W) == 0 and rows >= NA_KROWS and bsz <= 4 and d == 1024

    c8 = jnp.concatenate([c, c_ctx[None, :], jnp.zeros((8 - bsz - 1, d), F32)], axis=0)
    mod = _ada(c8, w_ada[0], b_ada[0][None, :])

    w = w_in[0]
    aw, akw, bw = A_Q_HEADS * HEAD_DIM, A_KV_HEADS * HEAD_DIM, B_HEADS * HEAD_DIM
    cuts = np.cumsum([0, aw, akw, akw, bw, bw, bw])
    qa_w, ka_w, va_w, qb_w, kb_w, vb_w = [w[:, cuts[i]:cuts[i + 1]] for i in range(6)]
    qa_w = _reorder_qa_heads(qa_w, 1)
    w_ext = jnp.concatenate([qa_w, qb_w, kb_w, vb_w, ka_w, va_w], axis=1).astype(BF16)
    scale = HEAD_DIM ** -0.5 * LOG2E
    pair = lambda g: jnp.tile(g, 2)
    gains = jnp.stack([pair(q_norm_a[0]) * scale, pair(q_norm_b[0]) * scale, pair(k_norm_b[0]), pair(k_norm_a[0])]
                      + [jnp.zeros((LANES,), F32)] * 4)
    bd = jnp.asarray(np.kron(np.eye(2), np.full((HEAD_DIM, HEAD_DIM), 1.0 / HEAD_DIM)), BF16)
    cos_t, sin_t = _rope_tables(s)
    n1 = norm1[0][None, :]
    zq = _inproj(x, mod, n1, w_ext, gains, cos_t, sin_t, bd, mod_row=None, tm=512)
    zc = _inproj(ctx, mod, n1, w_ext, gains, jnp.ones((lc, LANES), F32), jnp.zeros((lc, LANES), F32), bd,
                 mod_row=bsz, tm=lc)

    sink = sink_a[0].astype(F32)
    sink_rep = jnp.concatenate([jnp.broadcast_to(sink[h] * LOG2E, (A_WINDOW, LANES)) for h in _QA_HEAD_ORDER])
    o_a = _attn_a(zq, zc, sink_rep, _attn_a_mask(A_WINDOW))
    o_b = _attn_b(zq, zc, _na_table(rpb_b[0], rows))

    ga = _reorder_qa_heads(out_norm_a[0], 0)[None, :]
    gb = out_norm_b[0][None, :]
    wo = jnp.concatenate([_reorder_qa_heads(w_out[0][:aw], 0), w_out[0][aw:]], axis=0).astype(BF16)
    wr_hi, wr_lo = _split_bf16(w_router[0].T)
    xres, hp_a, hp_b, idx, gates = _merge(o_a, o_b, x, mod, ga, gb, wo, norm2[0][None, :], wr_hi, wr_lo,
                                  router_bias[0][:, None], ws_gate[0].astype(BF16), ws_up[0].astype(BF16),
                                  ws_down[0].astype(BF16), tm=512)

    rank, counts = _rank(idx, tm=512)
    t = EXPERT_ROWS
    cnt = counts[:, 0].astype(I32)
    padded = (cnt + t - 1) // t * t
    pends = jnp.cumsum(padded)
    pstart = pends - padded
    n_rows = n * TOP_K + N_EXPERTS * t
    pos = _pos(idx, rank, pstart.astype(F32)[:, None], tm=512)

    pos_flat = pos.reshape(1, TOP_K * n)
    xs_a = _sc_scatter_rows(hp_a, pos_flat, n_rows)
    xs_b = _sc_scatter_rows(hp_b, pos_flat, n_rows)
    ys_a, ys_b = _experts(pstart, cnt, pends[-1:] // t, xs_a, xs_b, we_gate[0], we_up[0], we_down[0])
    yg_a = _sc_gather_rows(ys_a, pos_flat).reshape(TOP_K, n, d // 4)
    yg_b = _sc_gather_rows(ys_b, pos_flat).reshape(TOP_K, n, d // 4)
    out = _finish(gates, xres.reshape(n, d), mod, yg_a, yg_b, tm=256, tiles_per_batch=s // 256)
    return out.reshape(bsz, s, d)
```

```python
import functools

import numpy as np
import jax
import jax.numpy as jnp
from jax import lax
from jax.experimental import pallas as pl
from jax.experimental.pallas import tpu as pltpu
from jax.experimental.pallas import tpu_sc as plsc

F32 = jnp.float32
BF16 = jnp.bfloat16
I32 = jnp.int32
U32 = jnp.uint32

LANES = 128
HEAD_DIM = 64
HEAD_PAIR = 2 * HEAD_DIM
GRID_W = 64
A_Q_HEADS = 8
A_KV_HEADS = 2
A_WINDOW = 128
B_HEADS = 8
NA_KH = 8
NA_KW = 16
NA_QROWS = 4
NA_KROWS = NA_QROWS + NA_KH
ROPE_BASE = 10000.0
N_EXPERTS = 256
TOP_K = 8
N_GROUPS = 8
TOPK_GROUPS = 4
ROUTED_SCALE = 2.5
LOG2E = 1.4426950408889634
EPS = 1e-6
NEG_INF = -1e30
EXPERT_ROWS = 256
SC_WINDOW = 128
VMEM_LIMIT = 56 * 1024 * 1024

_NT = (((1,), (1,)), ((), ()))


def _params(*sem):
    return pltpu.CompilerParams(dimension_semantics=sem, vmem_limit_bytes=VMEM_LIMIT)


def _silu(v):
    return v * jax.nn.sigmoid(v)


def _rms(v, gain):
    return v * lax.rsqrt(jnp.mean(v * v, axis=-1, keepdims=True) + EPS) * gain


def _pack_bf16_pairs(v):
    n = v.shape[1] // 2
    lo = lax.bitcast_convert_type(v[:, :n].astype(BF16).astype(F32), U32) >> 16
    hi = lax.bitcast_convert_type(v[:, n:].astype(BF16).astype(F32), U32) & jnp.uint32(0xFFFF0000)
    return hi | lo


def _split_bf16(v):
    hi = lax.bitcast_convert_type(lax.bitcast_convert_type(v, U32) & jnp.uint32(0xFFFF0000), F32)
    return hi.astype(BF16), (v - hi).astype(BF16)


def _unpack_bf16_pairs(w):
    lo = lax.bitcast_convert_type(w << 16, F32)
    hi = lax.bitcast_convert_type(w & jnp.uint32(0xFFFF0000), F32)
    return jnp.concatenate([lo, hi], axis=1)


def _ada_kernel(c_ref, w_ref, b_ref, o_ref):
    a = _silu(c_ref[...])
    o_ref[...] = jnp.dot(a, w_ref[...], preferred_element_type=F32,
                         precision=lax.Precision.HIGHEST) + b_ref[...]


def _ada(c8, w, b):
    d, n = w.shape
    bn = n // 4
    return pl.pallas_call(
        _ada_kernel,
        out_shape=jax.ShapeDtypeStruct((8, n), F32),
        grid=(n // bn,),
        in_specs=[pl.BlockSpec((8, d), lambda j: (0, 0)),
                  pl.BlockSpec((d, bn), lambda j: (0, j)),
                  pl.BlockSpec((1, bn), lambda j: (0, j))],
        out_specs=pl.BlockSpec((8, bn), lambda j: (0, j)),
        compiler_params=_params("arbitrary"),
        name="ada",
    )(c8, w, b)


_QA, _QB, _KB, _VB, _KA, _VA = 0, 4, 8, 12, 16, 17
_OUT_BLOCKS = 18


def _inproj_kernel(x_ref, mod_ref, n1_ref, w_ref, g_ref, cos_ref, sin_ref, bd_ref, o_ref, *, mod_row, d):
    b = pl.program_id(0) if mod_row is None else mod_row
    xn = _rms(x_ref[0], n1_ref[...])
    sh = mod_ref[pl.ds(b, 1), 0:d]
    sc = mod_ref[pl.ds(b, 1), d:2 * d]
    h = (xn * (1.0 + sc) + sh).astype(BF16)
    z = jnp.dot(h, w_ref[...], preferred_element_type=F32)
    bd = bd_ref[...]
    cos = cos_ref[...]
    sin = sin_ref[...]

    def blk(j):
        return z[:, j * LANES:(j + 1) * LANES]

    def head_rinv(zb):
        ms = jnp.dot((zb * zb).astype(BF16), bd, preferred_element_type=F32)
        return lax.rsqrt(ms + EPS)

    def put(j, v):
        o_ref[0, :, j * LANES:(j + 1) * LANES] = v.astype(BF16)

    quarter = HEAD_DIM // 4
    first_half = (lax.broadcasted_iota(I32, (1, LANES), 1) % (2 * quarter)) < quarter

    def roped(j, g_row):
        zb = blk(j)
        zn = zb * head_rinv(zb) * g_ref[g_row:g_row + 1, :]
        partner = jnp.where(first_half, pltpu.roll(zn, LANES - quarter, 1), pltpu.roll(zn, quarter, 1))
        put(j, zn * cos + partner * sin)

    def normed(j, g_row):
        zb = blk(j)
        put(j, zb * head_rinv(zb) * g_ref[g_row:g_row + 1, :])

    for j in range(4):
        roped(_QA + j, 0)
        normed(_QB + j, 1)
        normed(_KB + j, 2)
        put(_VB + j, blk(_VB + j))
    roped(_KA, 3)
    put(_VA, blk(_VA))


def _inproj(x, mod, n1, w_ext, gains, cos_t, sin_t, bd, *, mod_row, tm):
    bsz, s, d = x.shape
    kern = functools.partial(_inproj_kernel, mod_row=mod_row, d=d)
    return pl.pallas_call(
        kern,
        out_shape=jax.ShapeDtypeStruct((bsz, s, _OUT_BLOCKS * LANES), BF16),
        grid=(bsz, s // tm),
        in_specs=[pl.BlockSpec((1, tm, d), lambda b, i: (b, i, 0)),
                  pl.BlockSpec(mod.shape, lambda b, i: (0, 0)),
                  pl.BlockSpec((1, d), lambda b, i: (0, 0)),
                  pl.BlockSpec(w_ext.shape, lambda b, i: (0, 0)),
                  pl.BlockSpec(gains.shape, lambda b, i: (0, 0)),
                  pl.BlockSpec((tm, LANES), lambda b, i: (i, 0)),
                  pl.BlockSpec((tm, LANES), lambda b, i: (i, 0)),
                  pl.BlockSpec(bd.shape, lambda b, i: (0, 0))],
        out_specs=pl.BlockSpec((1, tm, _OUT_BLOCKS * LANES), lambda b, i: (b, i, 0)),
        compiler_params=_params("arbitrary", "arbitrary"),
        name="inproj",
    )(x, mod, n1, w_ext, gains, cos_t, sin_t, bd)


def _split_pair(qp, lo):
    zero = jnp.zeros_like(qp)
    return jnp.concatenate([jnp.where(lo, qp, zero), jnp.where(lo, zero, qp)], axis=0)


def _softmax_pv(s_parts, v_parts, sink_rep):
    chunks = [s[:, c * LANES:(c + 1) * LANES] for s in s_parts for c in range(s.shape[1] // LANES)]
    m = jnp.max(functools.reduce(jnp.maximum, chunks), axis=-1, keepdims=True)
    m_rep = jnp.broadcast_to(m, (m.shape[0], LANES))
    if sink_rep is not None:
        m_rep = jnp.maximum(m_rep, sink_rep)
    acc = None
    for s, v in zip(s_parts, v_parts):
        p = jnp.concatenate([jnp.exp2(s[:, c * LANES:(c + 1) * LANES] - m_rep)
                             for c in range(s.shape[1] // LANES)], axis=1).astype(BF16)
        v_ext = jnp.concatenate([v, jnp.ones_like(v)], axis=1)
        o = jnp.dot(p, v_ext, preferred_element_type=F32)
        acc = o if acc is None else acc + o
    l_rep = acc[:, LANES:]
    if sink_rep is not None:
        l_rep = l_rep + jnp.exp2(sink_rep - m_rep)
    return acc[:, :LANES] * (1.0 / l_rep)


A_BLOCKS_PER_STEP = 4


def _attn_a_kernel(q_ref, *refs):
    nq = A_BLOCKS_PER_STEP
    k_refs, v_refs = refs[:nq + 2], refs[nq + 2:2 * nq + 4]
    kx_ref, vx_ref, sink_ref = refs[2 * nq + 4:2 * nq + 7]
    mask_refs = refs[2 * nq + 7:3 * nq + 7]
    o_ref = refs[-1]
    tq = A_WINDOW
    lo = lax.broadcasted_iota(I32, (tq, LANES), 1) < HEAD_DIM
    for h in range(nq):
        q = q_ref[0, h * tq:(h + 1) * tq]
        qs = jnp.concatenate([_split_pair(q[:, p * LANES:(p + 1) * LANES], lo) for p in range(4)], axis=0)
        k = jnp.concatenate([r[0] for r in k_refs[h:h + 3]], axis=0)
        v = jnp.concatenate([r[0] for r in v_refs[h:h + 3]], axis=0)
        s_loc = lax.dot_general(qs, k, _NT, preferred_element_type=F32) + mask_refs[h][0]
        s_ctx = lax.dot_general(qs, kx_ref[0], _NT, preferred_element_type=F32)
        o = _softmax_pv([s_loc, s_ctx], [v, vx_ref[0]], sink_ref[...])
        for p in range(4):
            o_lo = o[(2 * p) * tq:(2 * p + 1) * tq]
            o_hi = o[(2 * p + 1) * tq:(2 * p + 2) * tq]
            o_ref[0, h * tq:(h + 1) * tq, p * LANES:(p + 1) * LANES] = jnp.where(lo, o_lo, o_hi).astype(BF16)


def _attn_a(zq, zc, sink_rep, mask):
    bsz, s, _ = zq.shape
    lc = zc.shape[1]
    tq = A_WINDOW
    nq = A_BLOCKS_PER_STEP
    nblk = s // tq
    ka, va = _KA, _VA

    def kv_spec(col, shift):
        return pl.BlockSpec((1, tq, LANES), lambda b, j: (b, jnp.clip(nq * j + shift, 0, nblk - 1), col))

    def mask_spec(h):
        def variant(j):
            blk = nq * j + h
            return jnp.where(blk == 0, 0, jnp.where(blk == nblk - 1, 2, 1))
        return pl.BlockSpec((1,) + mask.shape[1:], lambda b, j: (variant(j), 0, 0))

    shifts = range(-1, nq + 1)
    return pl.pallas_call(
        _attn_a_kernel,
        out_shape=jax.ShapeDtypeStruct((bsz, s, 4 * LANES), BF16),
        grid=(bsz, nblk // nq),
        in_specs=([pl.BlockSpec((1, nq * tq, 4 * LANES), lambda b, j: (b, j, 0))]
                  + [kv_spec(ka, sh) for sh in shifts] + [kv_spec(va, sh) for sh in shifts]
                  + [pl.BlockSpec((1, lc, LANES), lambda b, j: (b, 0, ka)),
                     pl.BlockSpec((1, lc, LANES), lambda b, j: (b, 0, va)),
                     pl.BlockSpec(sink_rep.shape, lambda b, j: (0, 0))]
                  + [mask_spec(h) for h in range(nq)]),
        out_specs=pl.BlockSpec((1, nq * tq, 4 * LANES), lambda b, j: (b, j, 0)),
        compiler_params=_params("arbitrary", "arbitrary"),
        name="attn_a",
    )(*([zq] * (2 * nq + 5) + [zc, zc, sink_rep] + [mask] * nq))


def _attn_a_mask(tq):
    qi = np.arange(tq)[:, None]
    kj = np.arange(3 * tq)[None, :]
    ok = (kj >= qi) & (kj <= qi + 2 * tq)
    variants = [ok & (kj >= tq), ok, ok & (kj < 2 * tq)]
    m = np.stack([np.tile(np.where(v, 0.0, NEG_INF).astype(np.float32), (A_Q_HEADS, 1)) for v in variants])
    return jnp.asarray(m)


NA_GROUPS_PER_STEP = 4


def _attn_b_kernel(q_ref, *refs):
    ngs = NA_GROUPS_PER_STEP
    k_refs, v_refs = refs[:3 * ngs], refs[3 * ngs:6 * ngs]
    kx_ref, vx_ref = refs[6 * ngs:6 * ngs + 2]
    tab_refs = refs[6 * ngs + 2:7 * ngs + 2]
    o_ref = refs[-1]
    tq = NA_QROWS * GRID_W
    lo = lax.broadcasted_iota(I32, (tq, LANES), 1) < HEAD_DIM
    for g in range(ngs):
        rows_g = slice(g * tq, (g + 1) * tq)
        for p in range(4):
            sl = slice(p * LANES, (p + 1) * LANES)
            qs = _split_pair(q_ref[0, rows_g, sl], lo)
            k = jnp.concatenate([r[0, :, sl] for r in k_refs[3 * g:3 * g + 3]], axis=0)
            v = jnp.concatenate([r[0, :, sl] for r in v_refs[3 * g:3 * g + 3]], axis=0)
            s_loc = lax.dot_general(qs, k, _NT, preferred_element_type=F32) + tab_refs[g][0, p]
            s_ctx = lax.dot_general(qs, kx_ref[0, :, sl], _NT, preferred_element_type=F32)
            o = _softmax_pv([s_loc, s_ctx], [v, vx_ref[0, :, sl]], None)
            o_ref[0, rows_g, sl] = jnp.where(lo, o[:tq], o[tq:]).astype(BF16)


def _attn_b(zq, zc, table):
    bsz, s, _ = zq.shape
    lc = zc.shape[1]
    tq = NA_QROWS * GRID_W
    ngs = NA_GROUPS_PER_STEP
    ng = s // tq
    qb, kb, vb = _QB // 4, _KB // 4, _VB // 4

    def kv_spec(col, g, off):
        return pl.BlockSpec((1, tq, 4 * LANES),
                            lambda i, b: (b, jnp.clip(ngs * i + g - 1, 0, ng - 3) + off, col))

    def tab_spec(g):
        def variant(i):
            grp = ngs * i + g
            return jnp.where(grp == 0, 0, jnp.where(grp == ng - 1, 2, 1))
        return pl.BlockSpec((1,) + table.shape[1:], lambda i, b: (variant(i), 0, 0, 0),
                            pipeline_mode=pl.Buffered(1))

    kv_slots = [(g, off) for g in range(ngs) for off in range(3)]
    return pl.pallas_call(
        _attn_b_kernel,
        out_shape=jax.ShapeDtypeStruct((bsz, s, 4 * LANES), BF16),
        grid=(ng // ngs, bsz),
        in_specs=([pl.BlockSpec((1, ngs * tq, 4 * LANES), lambda i, b: (b, i, qb))]
                  + [kv_spec(kb, g, off) for g, off in kv_slots] + [kv_spec(vb, g, off) for g, off in kv_slots]
                  + [pl.BlockSpec((1, lc, 4 * LANES), lambda i, b: (b, 0, kb)),
                     pl.BlockSpec((1, lc, 4 * LANES), lambda i, b: (b, 0, vb))]
                  + [tab_spec(g) for g in range(ngs)]),
        out_specs=pl.BlockSpec((1, ngs * tq, 4 * LANES), lambda i, b: (b, i, 0)),
        compiler_params=_params("arbitrary", "arbitrary"),
        name="attn_b",
    )(*([zq] * (6 * ngs + 1) + [zc, zc] + [table] * ngs))


def _na_table(rpb, rows):
    ng = rows // NA_QROWS
    nq, nk = NA_QROWS * GRID_W, NA_KROWS * GRID_W
    qc = np.arange(GRID_W)
    kc = np.arange(GRID_W)
    ws = np.clip(qc - NA_KW // 2, 0, GRID_W - NA_KW)
    valid_c = (kc[None, :] >= ws[:, None]) & (kc[None, :] < ws[:, None] + NA_KW)
    dc = np.clip(kc[None, :] - qc[:, None], -(NA_KW - 1), NA_KW - 1) + (NA_KW - 1)
    c_sel = (dc[..., None] == np.arange(2 * NA_KW - 1)) & valid_c[..., None]
    tiles = jnp.einsum('hab,uvb->huav', rpb.astype(F32), jnp.asarray(c_sel, F32), precision=lax.Precision.HIGHEST)
    tiles = jnp.where(jnp.asarray(valid_c)[None, :, None, :], tiles * LOG2E, NEG_INF)
    tiles = tiles.reshape(B_HEADS, GRID_W, (2 * NA_KH - 1) * GRID_W)

    def masked(n_key_rows):
        return jnp.full((B_HEADS, GRID_W, n_key_rows * GRID_W), NEG_INF, F32)

    tabs = []
    for i in (0, 1, ng - 1):
        start = int(np.clip(NA_QROWS * i - NA_KH // 2, 0, rows - NA_KROWS))
        q_rows = []
        for qr in range(NA_QROWS):
            r = NA_QROWS * i + qr
            rs = int(np.clip(r - NA_KH // 2, 0, rows - NA_KH))
            y0 = rs - start
            d0 = rs - r + (NA_KH - 1)
            q_rows.append(jnp.concatenate([masked(y0), tiles[:, :, d0 * GRID_W:(d0 + NA_KH) * GRID_W],
                                           masked(NA_KROWS - NA_KH - y0)], axis=-1))
        tabs.append(jnp.concatenate(q_rows, axis=-2))
    return jnp.stack(tabs).reshape(3, B_HEADS // 2, 2 * nq, nk)


def _merge_kernel(oa_ref, ob_ref, x_ref, mod_ref, ga_ref, gb_ref, wo_ref, n2_ref, wrh_ref, wrl_ref, rb_ref,
                  wsg_ref, wsu_ref, wsd_ref, xres_ref, hpa_ref, hpb_ref, idx_ref, gate_ref, *, d):
    b = pl.program_id(0)
    tm = x_ref.shape[1]

    def mod(k):
        return mod_ref[pl.ds(b, 1), k * d:(k + 1) * d]

    na = _rms(oa_ref[0].astype(F32), ga_ref[...])
    nb = _rms(ob_ref[0].astype(F32), gb_ref[...])
    cat = jnp.concatenate([na, nb], axis=1).astype(BF16)
    y = jnp.dot(cat, wo_ref[...], preferred_element_type=F32)
    x1 = x_ref[0] + mod(2) * y
    h2 = _rms(x1, n2_ref[...]) * (1.0 + mod(4)) + mod(3)

    hb = h2.astype(BF16)
    act = _silu(jnp.dot(hb, wsg_ref[...], preferred_element_type=F32)) * jnp.dot(hb, wsu_ref[...],
                                                                                 preferred_element_type=F32)
    shared = jnp.dot(act.astype(BF16), wsd_ref[...], preferred_element_type=F32)
    xres_ref[0] = x1 + mod(5) * shared
    packed = _pack_bf16_pairs(h2)
    hpa_ref[...] = packed[:, :d // 4]
    hpb_ref[...] = packed[:, d // 4:]

    h_hi, h_lo = _split_bf16(h2)
    logits = (lax.dot_general(wrh_ref[...], h_hi, _NT, preferred_element_type=F32)
              + lax.dot_general(wrh_ref[...], h_lo, _NT, preferred_element_type=F32)
              + lax.dot_general(wrl_ref[...], h_hi, _NT, preferred_element_type=F32))
    scores = jax.nn.sigmoid(logits)
    sel = scores + rb_ref[...]
    per = N_EXPERTS // N_GROUPS
    g3 = sel.reshape(N_GROUPS, per, tm)
    it3 = lax.broadcasted_iota(I32, (N_GROUPS, per, tm), 1)
    m1 = jnp.max(g3, axis=1, keepdims=True)
    first = jnp.min(jnp.where(g3 == m1, it3, per), axis=1, keepdims=True)
    m2 = jnp.max(jnp.where(it3 == first, -jnp.inf, g3), axis=1, keepdims=True)
    gscore = (m1 + m2).reshape(N_GROUPS, tm)

    itg = lax.broadcasted_iota(I32, (N_GROUPS, tm), 0)
    gsel = jnp.zeros((N_GROUPS, tm), F32)
    cur = gscore
    for _ in range(TOPK_GROUPS):
        mx = jnp.max(cur, axis=0, keepdims=True)
        fi = jnp.min(jnp.where(cur == mx, itg, N_GROUPS), axis=0, keepdims=True)
        pick = itg == fi
        gsel = jnp.where(pick, 1.0, gsel)
        cur = jnp.where(pick, -jnp.inf, cur)
    emask = jnp.broadcast_to(gsel.reshape(N_GROUPS, 1, tm), (N_GROUPS, per, tm)).reshape(N_EXPERTS, tm) > 0.5

    ite = lax.broadcasted_iota(I32, (N_EXPERTS, tm), 0)
    cur = jnp.where(emask, sel, NEG_INF)
    idx_rows, s_rows = [], []
    for _ in range(TOP_K):
        mx = jnp.max(cur, axis=0, keepdims=True)
        fi = jnp.min(jnp.where(cur == mx, ite, N_EXPERTS), axis=0, keepdims=True)
        pick = ite == fi
        idx_rows.append(fi)
        s_rows.append(jnp.sum(jnp.where(pick, scores, 0.0), axis=0, keepdims=True))
        cur = jnp.where(pick, -jnp.inf, cur)
    top_s = jnp.concatenate(s_rows, axis=0)
    idx_ref[...] = jnp.concatenate(idx_rows, axis=0)
    gate_ref[...] = top_s / jnp.sum(top_s, axis=0, keepdims=True) * ROUTED_SCALE


def _merge(o_a, o_b, x, mod, ga, gb, wo, n2, wr_hi, wr_lo, rb_col, wsg, wsu, wsd, *, tm):
    bsz, s, d = x.shape
    nt = s // tm
    n = bsz * s
    full = lambda a: pl.BlockSpec(a.shape, lambda b, i: (0,) * a.ndim)
    return pl.pallas_call(
        functools.partial(_merge_kernel, d=d),
        out_shape=(jax.ShapeDtypeStruct((bsz, s, d), F32),
                   jax.ShapeDtypeStruct((n, d // 4), U32),
                   jax.ShapeDtypeStruct((n, d // 4), U32),
                   jax.ShapeDtypeStruct((TOP_K, n), I32),
                   jax.ShapeDtypeStruct((TOP_K, n), F32)),
        grid=(bsz, nt),
        in_specs=[pl.BlockSpec((1, tm, d // 2), lambda b, i: (b, i, 0)),
                  pl.BlockSpec((1, tm, d // 2), lambda b, i: (b, i, 0)),
                  pl.BlockSpec((1, tm, d), lambda b, i: (b, i, 0)),
                  full(mod), full(ga), full(gb), full(wo), full(n2), full(wr_hi), full(wr_lo), full(rb_col),
                  full(wsg), full(wsu), full(wsd)],
        out_specs=(pl.BlockSpec((1, tm, d), lambda b, i: (b, i, 0)),
                   pl.BlockSpec((tm, d // 4), lambda b, i: (b * nt + i, 0)),
                   pl.BlockSpec((tm, d // 4), lambda b, i: (b * nt + i, 0)),
                   pl.BlockSpec((TOP_K, tm), lambda b, i: (0, b * nt + i)),
                   pl.BlockSpec((TOP_K, tm), lambda b, i: (0, b * nt + i))),
        compiler_params=_params("arbitrary", "arbitrary"),
        name="merge",
    )(o_a, o_b, x, mod, ga, gb, wo, n2, wr_hi, wr_lo, rb_col, wsg, wsu, wsd)


def _rank_kernel(idx_ref, rank_ref, cnt_ref, carry_ref):
    tm = idx_ref.shape[1]

    @pl.when(pl.program_id(0) == 0)
    def _():
        carry_ref[...] = jnp.zeros_like(carry_ref)

    idx = idx_ref[...]
    ite = lax.broadcasted_iota(I32, (N_EXPERTS, tm), 0)
    before = (lax.broadcasted_iota(I32, (tm, tm), 0) < lax.broadcasted_iota(I32, (tm, tm), 1)).astype(BF16)
    hits = [ite == idx[k:k + 1, :] for k in range(TOP_K)]
    routed = jnp.where(functools.reduce(jnp.logical_or, hits), 1.0, 0.0)
    base = carry_ref[...]
    ahead = jnp.dot(routed.astype(BF16), before, preferred_element_type=F32) + base
    rows = [jnp.sum(jnp.where(hit, ahead, 0.0), axis=0, keepdims=True) for hit in hits]
    rank_ref[...] = jnp.concatenate(rows, axis=0).astype(I32)
    total = base + jnp.sum(routed, axis=1, keepdims=True)
    carry_ref[...] = total
    cnt_ref[...] = total


def _rank(idx, *, tm):
    n = idx.shape[1]
    return pl.pallas_call(
        _rank_kernel,
        out_shape=(jax.ShapeDtypeStruct((TOP_K, n), I32), jax.ShapeDtypeStruct((N_EXPERTS, 1), F32)),
        grid=(n // tm,),
        in_specs=[pl.BlockSpec((TOP_K, tm), lambda i: (0, i))],
        out_specs=(pl.BlockSpec((TOP_K, tm), lambda i: (0, i)), pl.BlockSpec((N_EXPERTS, 1), lambda i: (0, 0))),
        scratch_shapes=[pltpu.VMEM((N_EXPERTS, 1), F32)],
        compiler_params=_params("arbitrary"),
        name="rank",
    )(idx)


def _pos_kernel(idx_ref, rank_ref, pstart_ref, pos_ref):
    tm = idx_ref.shape[1]
    idx = idx_ref[...]
    ite = lax.broadcasted_iota(I32, (N_EXPERTS, tm), 0)
    pstart = pstart_ref[...]
    rows = [jnp.sum(jnp.where(ite == idx[k:k + 1, :], pstart, 0.0), axis=0, keepdims=True) for k in range(TOP_K)]
    pos_ref[...] = jnp.concatenate(rows, axis=0).astype(I32) + rank_ref[...]


def _pos(idx, rank, pstart_col, *, tm):
    n = idx.shape[1]
    return pl.pallas_call(
        _pos_kernel,
        out_shape=jax.ShapeDtypeStruct((TOP_K, n), I32),
        grid=(n // tm,),
        in_specs=[pl.BlockSpec((TOP_K, tm), lambda i: (0, i)),
                  pl.BlockSpec((TOP_K, tm), lambda i: (0, i)),
                  pl.BlockSpec((N_EXPERTS, 1), lambda i: (0, 0))],
        out_specs=pl.BlockSpec((TOP_K, tm), lambda i: (0, i)),
        compiler_params=_params("arbitrary"),
        name="pos",
    )(idx, rank, pstart_col)


UNIT_CHUNKS = (4, 2, 1)
RING_AHEAD = 4
RING_SLOTS = RING_AHEAD + UNIT_CHUNKS[0]


def _expert_kernel(ps_ref, cnt_ref, tot_ref, wg_ref, wu_ref, wd_ref, xa_hbm, xb_hbm, ya_hbm, yb_hbm,
                   xa_buf, xb_buf, ya_buf, yb_buf, in_sem, out_sem):
    e = pl.program_id(0)
    r = EXPERT_ROWS
    ahead = RING_AHEAD
    total = tot_ref[0]
    first = ps_ref[e] // r
    cnt = cnt_ref[e]
    nch = (cnt + r - 1) // r

    def rows_of(g):
        return pl.ds(pl.multiple_of(g * r, r), r)

    def fetch(g):
        slot = g % RING_SLOTS
        return (pltpu.make_async_copy(xa_hbm.at[rows_of(g)], xa_buf.at[slot], in_sem.at[0, slot]),
                pltpu.make_async_copy(xb_hbm.at[rows_of(g)], xb_buf.at[slot], in_sem.at[1, slot]))

    def flush(g):
        slot = g % RING_SLOTS
        return (pltpu.make_async_copy(ya_buf.at[slot], ya_hbm.at[rows_of(g)], out_sem.at[0, slot]),
                pltpu.make_async_copy(yb_buf.at[slot], yb_hbm.at[rows_of(g)], out_sem.at[1, slot]))

    def start(copies):
        for cp in copies:
            cp.start()

    def wait(copies):
        for cp in copies:
            cp.wait()

    @pl.when(e == 0)
    def _():
        for g in range(ahead):
            @pl.when(g < total)
            def _():
                start(fetch(g))

    @pl.when(nch > 0)
    def _():
        def unit(c0, u):
            for j in range(u):
                g = first + c0 + j
                wait(fetch(g))

                @pl.when(g + ahead < total)
                def _():
                    start(fetch(g + ahead))

                @pl.when(g >= RING_SLOTS)
                def _():
                    wait(flush(g - RING_SLOTS))

            slots = [(first + c0 + j) % RING_SLOTS for j in range(u)]
            packed = jnp.concatenate([jnp.concatenate([xa_buf[sl], xb_buf[sl]], axis=1) for sl in slots], axis=0)
            row = lax.broadcasted_iota(I32, packed.shape, 0)
            packed = jnp.where(row < cnt - c0 * r, packed, jnp.zeros_like(packed))
            x = _unpack_bf16_pairs(packed).astype(BF16)
            gate = jnp.dot(x, wg_ref[0].astype(BF16), preferred_element_type=F32)
            up = jnp.dot(x, wu_ref[0].astype(BF16), preferred_element_type=F32)
            y = jnp.dot((_silu(gate) * up).astype(BF16), wd_ref[0].astype(BF16), preferred_element_type=F32)
            out = _pack_bf16_pairs(y)
            half = out.shape[1] // 2
            for j, sl in enumerate(slots):
                ya_buf[sl] = out[j * r:(j + 1) * r, :half]
                yb_buf[sl] = out[j * r:(j + 1) * r, half:]
                start(flush(first + c0 + j))

        big = UNIT_CHUNKS[0]

        def big_unit(i, carry):
            unit(i * big, big)
            return carry

        lax.fori_loop(0, nch // big, big_unit, 0)
        done = nch // big * big
        for u in UNIT_CHUNKS[1:]:
            @pl.when((nch % (2 * u)) >= u)
            def _():
                unit(done, u)

            done = done + jnp.where((nch % (2 * u)) >= u, u, 0)

    @pl.when(e == pl.num_programs(0) - 1)
    def _():
        for k in range(RING_SLOTS):
            @pl.when(total - 1 - k >= 0)
            def _():
                wait(flush(total - 1 - k))


def _experts(pstart, cnt, total_chunks, xs_a, xs_b, wg, wu, wd):
    rows, hw = xs_a.shape
    r = EXPERT_ROWS
    n_exp, d, f = wg.shape
    hbm = pl.BlockSpec(memory_space=pl.ANY)
    grid_spec = pltpu.PrefetchScalarGridSpec(
        num_scalar_prefetch=3,
        grid=(n_exp,),
        in_specs=[pl.BlockSpec((1, d, f), lambda e, ps, cn, tot: (e, 0, 0)),
                  pl.BlockSpec((1, d, f), lambda e, ps, cn, tot: (e, 0, 0)),
                  pl.BlockSpec((1, f, d), lambda e, ps, cn, tot: (e, 0, 0)),
                  hbm, hbm],
        out_specs=(hbm, hbm),
        scratch_shapes=[pltpu.VMEM((RING_SLOTS, r, hw), U32), pltpu.VMEM((RING_SLOTS, r, hw), U32),
                        pltpu.VMEM((RING_SLOTS, r, hw), U32), pltpu.VMEM((RING_SLOTS, r, hw), U32),
                        pltpu.SemaphoreType.DMA((2, RING_SLOTS)), pltpu.SemaphoreType.DMA((2, RING_SLOTS))],
    )
    return pl.pallas_call(
        _expert_kernel,
        out_shape=(jax.ShapeDtypeStruct((rows, hw), U32), jax.ShapeDtypeStruct((rows, hw), U32)),
        grid_spec=grid_spec,
        compiler_params=_params("arbitrary"),
        name="experts",
    )(pstart, cnt, total_chunks, wg, wu, wd, xs_a, xs_b)


def _sc_scatter_rows(x, idx_flat, n_rows):
    n, w = x.shape
    m = idx_flat.shape[1]
    nwin = n // SC_WINDOW
    reps = m // n
    mesh = plsc.VectorSubcoreMesh(core_axis_name="core", subcore_axis_name="subcore")

    @functools.partial(pl.kernel, out_type=jax.ShapeDtypeStruct((n_rows, w), x.dtype), mesh=mesh,
                       scratch_types=[])
    def scatter_kernel(x_hbm, i_hbm, o_hbm):
        def body(x_vmem, i_vmem):
            pltpu.sync_copy(x_vmem, o_hbm.at[i_vmem.at[0]])

        pltpu.emit_pipeline(
            body,
            grid=(nwin, reps),
            in_specs=[pl.BlockSpec((SC_WINDOW, w), lambda i, k: (i, 0)),
                      pl.BlockSpec((1, SC_WINDOW), lambda i, k: (0, k * nwin + i))],
            out_specs=[],
            core_axis_name=("core", "subcore"),
            dimension_semantics=(pltpu.PARALLEL, pltpu.ARBITRARY),
        )(x_hbm, i_hbm)

    return scatter_kernel(x, idx_flat)


def _sc_gather_rows(src, idx_flat):
    m = idx_flat.shape[1]
    w = src.shape[1]
    mesh = plsc.VectorSubcoreMesh(core_axis_name="core", subcore_axis_name="subcore")

    @functools.partial(pl.kernel, out_type=jax.ShapeDtypeStruct((m, w), src.dtype), mesh=mesh)
    def gather_kernel(x_hbm, i_hbm, o_hbm):
        def body(i_vmem, o_vmem):
            pltpu.sync_copy(x_hbm.at[i_vmem.at[0]], o_vmem)

        pltpu.emit_pipeline(
            body,
            grid=(m // SC_WINDOW,),
            in_specs=[pl.BlockSpec((1, SC_WINDOW), lambda i: (0, i))],
            out_specs=[pl.BlockSpec((SC_WINDOW, w), lambda i: (i, 0))],
            core_axis_name=("core", "subcore"),
            dimension_semantics=(pltpu.PARALLEL,),
        )(i_hbm, o_hbm)

    return gather_kernel(src, idx_flat)


def _finish_kernel(gate_ref, xres_ref, mod_ref, ya_ref, yb_ref, o_ref, *, d, tiles_per_batch):
    tm = gate_ref.shape[1]
    b = pl.program_id(0) // tiles_per_batch
    gates = gate_ref[...].T
    acc = jnp.zeros((tm, d), F32)
    for k in range(TOP_K):
        packed = jnp.concatenate([ya_ref[k], yb_ref[k]], axis=1)
        acc = acc + _unpack_bf16_pairs(packed) * gates[:, k:k + 1]
    o_ref[...] = xres_ref[...] + mod_ref[pl.ds(b, 1), 5 * d:6 * d] * acc


def _finish(gates, xres, mod, yg_a, yg_b, *, tm, tiles_per_batch):
    n, d = xres.shape
    return pl.pallas_call(
        functools.partial(_finish_kernel, d=d, tiles_per_batch=tiles_per_batch),
        out_shape=jax.ShapeDtypeStruct((n, d), F32),
        grid=(n // tm,),
        in_specs=[pl.BlockSpec((TOP_K, tm), lambda i: (0, i)),
                  pl.BlockSpec((tm, d), lambda i: (i, 0)),
                  pl.BlockSpec(mod.shape, lambda i: (0, 0)),
                  pl.BlockSpec((TOP_K, tm, d // 4), lambda i: (0, i, 0)),
                  pl.BlockSpec((TOP_K, tm, d // 4), lambda i: (0, i, 0))],
        out_specs=pl.BlockSpec((tm, d), lambda i: (i, 0)),
        compiler_params=_params("arbitrary"),
        name="finish",
    )(gates, xres, mod, yg_a, yg_b)


_QA_HEAD_ORDER = (0, 4, 1, 5, 2, 6, 3, 7)


def _reorder_qa_heads(a, axis):
    return jnp.concatenate([lax.slice_in_dim(a, h * HEAD_DIM, (h + 1) * HEAD_DIM, axis=axis)
                            for h in _QA_HEAD_ORDER], axis=axis)


def _rope_tables(s):
    quarter = HEAD_DIM // 4
    t = jnp.arange(s)
    row = (t // GRID_W).astype(F32)
    col = (t % GRID_W).astype(F32)
    freqs = ROPE_BASE ** (-jnp.arange(quarter, dtype=F32) / quarter)
    ar = row[:, None] * freqs[None, :]
    ac = col[:, None] * freqs[None, :]
    cos = jnp.concatenate([jnp.cos(ar), jnp.cos(ar), jnp.cos(ac), jnp.cos(ac)], axis=1)
    sin = jnp.concatenate([-jnp.sin(ar), jnp.sin(ar), -jnp.sin(ac), jnp.sin(ac)], axis=1)
    return jnp.tile(cos, (1, 2)), jnp.tile(sin, (1, 2))


def kernel(x, c, ctx, c_ctx, w_ada, b_ada, norm1, norm2, w_in, q_norm_a, k_norm_a, q_norm_b, k_norm_b, sink_a,
           rpb_b, out_norm_a, out_norm_b, w_out, w_router, router_bias, we_gate, we_up, we_down, ws_gate, ws_up,
           ws_down):
    assert w_ada.shape[0] == 1, "single-layer block"
    bsz, s, d = x.shape
    lc = ctx.shape[1]
    n = bsz * s
    rows = s // GRID_W
    assert s % (NA_QROWS * GRID_W) == 0 and rows >= NA_KROWS and bsz <= 4 and d == 1024

    c8 = jnp.concatenate([c, c_ctx[None, :], jnp.zeros((8 - bsz - 1, d), F32)], axis=0)
    mod = _ada(c8, w_ada[0], b_ada[0][None, :])

    w = w_in[0]
    aw, akw, bw = A_Q_HEADS * HEAD_DIM, A_KV_HEADS * HEAD_DIM, B_HEADS * HEAD_DIM
    cuts = np.cumsum([0, aw, akw, akw, bw, bw, bw])
    qa_w, ka_w, va_w, qb_w, kb_w, vb_w = [w[:, cuts[i]:cuts[i + 1]] for i in range(6)]
    qa_w = _reorder_qa_heads(qa_w, 1)
    w_ext = jnp.concatenate([qa_w, qb_w, kb_w, vb_w, ka_w, va_w], axis=1).astype(BF16)
    scale = HEAD_DIM ** -0.5 * LOG2E
    pair = lambda g: jnp.tile(g, 2)
    gains = jnp.stack([pair(q_norm_a[0]) * scale, pair(q_norm_b[0]) * scale, pair(k_norm_b[0]), pair(k_norm_a[0])]
                      + [jnp.zeros((LANES,), F32)] * 4)
    bd = jnp.asarray(np.kron(np.eye(2), np.full((HEAD_DIM, HEAD_DIM), 1.0 / HEAD_DIM)), BF16)
    cos_t, sin_t = _rope_tables(s)
    n1 = norm1[0][None, :]
    zq = _inproj(x, mod, n1, w_ext, gains, cos_t, sin_t, bd, mod_row=None, tm=512)
    zc = _inproj(ctx, mod, n1, w_ext, gains, jnp.ones((lc, LANES), F32), jnp.zeros((lc, LANES), F32), bd,
                 mod_row=bsz, tm=lc)

    sink = sink_a[0].astype(F32)
    sink_rep = jnp.concatenate([jnp.broadcast_to(sink[h] * LOG2E, (A_WINDOW, LANES)) for h in _QA_HEAD_ORDER])
    o_a = _attn_a(zq, zc, sink_rep, _attn_a_mask(A_WINDOW))
    o_b = _attn_b(zq, zc, _na_table(rpb_b[0], rows))

    ga = _reorder_qa_heads(out_norm_a[0], 0)[None, :]
    gb = out_norm_b[0][None, :]
    wo = jnp.concatenate([_reorder_qa_heads(w_out[0][:aw], 0), w_out[0][aw:]], axis=0).astype(BF16)
    wr_hi, wr_lo = _split_bf16(w_router[0].T)
    xres, hp_a, hp_b, idx, gates = _merge(o_a, o_b, x, mod, ga, gb, wo, norm2[0][None, :], wr_hi, wr_lo,
                                  router_bias[0][:, None], ws_gate[0].astype(BF16), ws_up[0].astype(BF16),
                                  ws_down[0].astype(BF16), tm=512)

    rank, counts = _rank(idx, tm=512)
    t = EXPERT_ROWS
    cnt = counts[:, 0].astype(I32)
    padded = (cnt + t - 1) // t * t
    pends = jnp.cumsum(padded)
    pstart = pends - padded
    n_rows = n * TOP_K + N_EXPERTS * t
    pos = _pos(idx, rank, pstart.astype(F32)[:, None], tm=512)

    pos_flat = pos.reshape(1, TOP_K * n)
    xs_a = _sc_scatter_rows(hp_a, pos_flat, n_rows)
    xs_b = _sc_scatter_rows(hp_b, pos_flat, n_rows)
    ys_a, ys_b = _experts(pstart, cnt, pends[-1:] // t, xs_a, xs_b, we_gate[0], we_up[0], we_down[0])
    yg_a = _sc_gather_rows(ys_a, pos_flat).reshape(TOP_K, n, d // 4)
    yg_b = _sc_gather_rows(ys_b, pos_flat).reshape(TOP_K, n, d // 4)
    out = _finish(gates, xres.reshape(n, d), mod, yg_a, yg_b, tm=256, tiles_per_batch=s // 256)
    return out.reshape(bsz, s, d)
```

```python
import functools

import numpy as np
import jax
import jax.numpy as jnp
from jax import lax
from jax.experimental import pallas as pl
from jax.experimental.pallas import tpu as pltpu
from jax.experimental.pallas import tpu_sc as plsc

F32 = jnp.float32
BF16 = jnp.bfloat16
I32 = jnp.int32
U32 = jnp.uint32

LANES = 128
HEAD_DIM = 64
HEAD_PAIR = 2 * HEAD_DIM
GRID_W = 64
A_Q_HEADS = 8
A_KV_HEADS = 2
A_WINDOW = 128
B_HEADS = 8
NA_KH = 8
NA_KW = 16
NA_QROWS = 4
NA_KROWS = NA_QROWS + NA_KH
ROPE_BASE = 10000.0
N_EXPERTS = 256
TOP_K = 8
N_GROUPS = 8
TOPK_GROUPS = 4
ROUTED_SCALE = 2.5
LOG2E = 1.4426950408889634
EPS = 1e-6
NEG_INF = -1e30
EXPERT_ROWS = 256
SC_WINDOW = 128
VMEM_LIMIT = 56 * 1024 * 1024

_NT = (((1,), (1,)), ((), ()))


def _params(*sem):
    return pltpu.CompilerParams(dimension_semantics=sem, vmem_limit_bytes=VMEM_LIMIT)


def _silu(v):
    return v * jax.nn.sigmoid(v)


def _rms(v, gain):
    return v * lax.rsqrt(jnp.mean(v * v, axis=-1, keepdims=True) + EPS) * gain


def _pack_bf16_pairs(v):
    n = v.shape[1] // 2
    lo = lax.bitcast_convert_type(v[:, :n].astype(BF16).astype(F32), U32) >> 16
    hi = lax.bitcast_convert_type(v[:, n:].astype(BF16).astype(F32), U32) & jnp.uint32(0xFFFF0000)
    return hi | lo


def _split_bf16(v):
    hi = lax.bitcast_convert_type(lax.bitcast_convert_type(v, U32) & jnp.uint32(0xFFFF0000), F32)
    return hi.astype(BF16), (v - hi).astype(BF16)


def _unpack_bf16_pairs(w):
    lo = lax.bitcast_convert_type(w << 16, F32)
    hi = lax.bitcast_convert_type(w & jnp.uint32(0xFFFF0000), F32)
    return jnp.concatenate([lo, hi], axis=1)


def _ada_kernel(c_ref, w_ref, b_ref, o_ref):
    a = _silu(c_ref[...])
    o_ref[...] = jnp.dot(a, w_ref[...], preferred_element_type=F32,
                         precision=lax.Precision.HIGHEST) + b_ref[...]


def _ada(c8, w, b):
    d, n = w.shape
    bn = n // 4
    return pl.pallas_call(
        _ada_kernel,
        out_shape=jax.ShapeDtypeStruct((8, n), F32),
        grid=(n // bn,),
        in_specs=[pl.BlockSpec((8, d), lambda j: (0, 0)),
                  pl.BlockSpec((d, bn), lambda j: (0, j)),
                  pl.BlockSpec((1, bn), lambda j: (0, j))],
        out_specs=pl.BlockSpec((8, bn), lambda j: (0, j)),
        compiler_params=_params("arbitrary"),
        name="ada",
    )(c8, w, b)


_QA, _QB, _KB, _VB, _KA, _VA = 0, 4, 8, 12, 16, 17
_OUT_BLOCKS = 18


def _inproj_kernel(x_ref, mod_ref, n1_ref, w_ref, g_ref, cos_ref, sin_ref, bd_ref, o_ref, *, mod_row, d):
    b = pl.program_id(0) if mod_row is None else mod_row
    xn = _rms(x_ref[0], n1_ref[...])
    sh = mod_ref[pl.ds(b, 1), 0:d]
    sc = mod_ref[pl.ds(b, 1), d:2 * d]
    h = (xn * (1.0 + sc) + sh).astype(BF16)
    z = jnp.dot(h, w_ref[...], preferred_element_type=F32)
    bd = bd_ref[...]
    cos = cos_ref[...]
    sin = sin_ref[...]

    def blk(j):
        return z[:, j * LANES:(j + 1) * LANES]

    def head_rinv(zb):
        ms = jnp.dot((zb * zb).astype(BF16), bd, preferred_element_type=F32)
        return lax.rsqrt(ms + EPS)

    def put(j, v):
        o_ref[0, :, j * LANES:(j + 1) * LANES] = v.astype(BF16)

    quarter = HEAD_DIM // 4
    first_half = (lax.broadcasted_iota(I32, (1, LANES), 1) % (2 * quarter)) < quarter

    def roped(j, g_row):
        zb = blk(j)
        zn = zb * head_rinv(zb) * g_ref[g_row:g_row + 1, :]
        partner = jnp.where(first_half, pltpu.roll(zn, LANES - quarter, 1), pltpu.roll(zn, quarter, 1))
        put(j, zn * cos + partner * sin)

    def normed(j, g_row):
        zb = blk(j)
        put(j, zb * head_rinv(zb) * g_ref[g_row:g_row + 1, :])

    for j in range(4):
        roped(_QA + j, 0)
        normed(_QB + j, 1)
        normed(_KB + j, 2)
        put(_VB + j, blk(_VB + j))
    roped(_KA, 3)
    put(_VA, blk(_VA))


def _inproj(x, mod, n1, w_ext, gains, cos_t, sin_t, bd, *, mod_row, tm):
    bsz, s, d = x.shape
    kern = functools.partial(_inproj_kernel, mod_row=mod_row, d=d)
    return pl.pallas_call(
        kern,
        out_shape=jax.ShapeDtypeStruct((bsz, s, _OUT_BLOCKS * LANES), BF16),
        grid=(bsz, s // tm),
        in_specs=[pl.BlockSpec((1, tm, d), lambda b, i: (b, i, 0)),
                  pl.BlockSpec(mod.shape, lambda b, i: (0, 0)),
                  pl.BlockSpec((1, d), lambda b, i: (0, 0)),
                  pl.BlockSpec(w_ext.shape, lambda b, i: (0, 0)),
                  pl.BlockSpec(gains.shape, lambda b, i: (0, 0)),
                  pl.BlockSpec((tm, LANES), lambda b, i: (i, 0)),
                  pl.BlockSpec((tm, LANES), lambda b, i: (i, 0)),
                  pl.BlockSpec(bd.shape, lambda b, i: (0, 0))],
        out_specs=pl.BlockSpec((1, tm, _OUT_BLOCKS * LANES), lambda b, i: (b, i, 0)),
        compiler_params=_params("arbitrary", "arbitrary"),
        name="inproj",
    )(x, mod, n1, w_ext, gains, cos_t, sin_t, bd)


def _split_pair(qp, lo):
    zero = jnp.zeros_like(qp)
    return jnp.concatenate([jnp.where(lo, qp, zero), jnp.where(lo, zero, qp)], axis=0)


def _softmax_pv(s_parts, v_parts, sink_rep):
    chunks = [s[:, c * LANES:(c + 1) * LANES] for s in s_parts for c in range(s.shape[1] // LANES)]
    m = jnp.max(functools.reduce(jnp.maximum, chunks), axis=-1, keepdims=True)
    m_rep = jnp.broadcast_to(m, (m.shape[0], LANES))
    if sink_rep is not None:
        m_rep = jnp.maximum(m_rep, sink_rep)
    acc = None
    for s, v in zip(s_parts, v_parts):
        p = jnp.concatenate([jnp.exp2(s[:, c * LANES:(c + 1) * LANES] - m_rep)
                             for c in range(s.shape[1] // LANES)], axis=1).astype(BF16)
        v_ext = jnp.concatenate([v, jnp.ones_like(v)], axis=1)
        o = jnp.dot(p, v_ext, preferred_element_type=F32)
        acc = o if acc is None else acc + o
    l_rep = acc[:, LANES:]
    if sink_rep is not None:
        l_rep = l_rep + jnp.exp2(sink_rep - m_rep)
    return acc[:, :LANES] * (1.0 / l_rep)


A_BLOCKS_PER_STEP = 4


def _attn_a_kernel(q_ref, *refs):
    nq = A_BLOCKS_PER_STEP
    k_refs, v_refs = refs[:nq + 2], refs[nq + 2:2 * nq + 4]
    kx_ref, vx_ref, sink_ref = refs[2 * nq + 4:2 * nq + 7]
    mask_refs = refs[2 * nq + 7:3 * nq + 7]
    o_ref = refs[-1]
    tq = A_WINDOW
    lo = lax.broadcasted_iota(I32, (tq, LANES), 1) < HEAD_DIM
    for h in range(nq):
        q = q_ref[0, h * tq:(h + 1) * tq]
        qs = jnp.concatenate([_split_pair(q[:, p * LANES:(p + 1) * LANES], lo) for p in range(4)], axis=0)
        k = jnp.concatenate([r[0] for r in k_refs[h:h + 3]], axis=0)
        v = jnp.concatenate([r[0] for r in v_refs[h:h + 3]], axis=0)
        s_loc = lax.dot_general(qs, k, _NT, preferred_element_type=F32) + mask_refs[h][0]
        s_ctx = lax.dot_general(qs, kx_ref[0], _NT, preferred_element_type=F32)
        o = _softmax_pv([s_loc, s_ctx], [v, vx_ref[0]], sink_ref[...])
        for p in range(4):
            o_lo = o[(2 * p) * tq:(2 * p + 1) * tq]
            o_hi = o[(2 * p + 1) * tq:(2 * p + 2) * tq]
            o_ref[0, h * tq:(h + 1) * tq, p * LANES:(p + 1) * LANES] = jnp.where(lo, o_lo, o_hi).astype(BF16)


def _attn_a(zq, zc, sink_rep, mask):
    bsz, s, _ = zq.shape
    lc = zc.shape[1]
    tq = A_WINDOW
    nq = A_BLOCKS_PER_STEP
    nblk = s // tq
    ka, va = _KA, _VA

    def kv_spec(col, shift):
        return pl.BlockSpec((1, tq, LANES), lambda b, j: (b, jnp.clip(nq * j + shift, 0, nblk - 1), col))

    def mask_spec(h):
        def variant(j):
            blk = nq * j + h
            return jnp.where(blk == 0, 0, jnp.where(blk == nblk - 1, 2, 1))
        return pl.BlockSpec((1,) + mask.shape[1:], lambda b, j: (variant(j), 0, 0))

    shifts = range(-1, nq + 1)
    return pl.pallas_call(
        _attn_a_kernel,
        out_shape=jax.ShapeDtypeStruct((bsz, s, 4 * LANES), BF16),
        grid=(bsz, nblk // nq),
        in_specs=([pl.BlockSpec((1, nq * tq, 4 * LANES), lambda b, j: (b, j, 0))]
                  + [kv_spec(ka, sh) for sh in shifts] + [kv_spec(va, sh) for sh in shifts]
                  + [pl.BlockSpec((1, lc, LANES), lambda b, j: (b, 0, ka)),
                     pl.BlockSpec((1, lc, LANES), lambda b, j: (b, 0, va)),
                     pl.BlockSpec(sink_rep.shape, lambda b, j: (0, 0))]
                  + [mask_spec(h) for h in range(nq)]),
        out_specs=pl.BlockSpec((1, nq * tq, 4 * LANES), lambda b, j: (b, j, 0)),
        compiler_params=_params("arbitrary", "arbitrary"),
        name="attn_a",
    )(*([zq] * (2 * nq + 5) + [zc, zc, sink_rep] + [mask] * nq))


def _attn_a_mask(tq):
    qi = np.arange(tq)[:, None]
    kj = np.arange(3 * tq)[None, :]
    ok = (kj >= qi) & (kj <= qi + 2 * tq)
    variants = [ok & (kj >= tq), ok, ok & (kj < 2 * tq)]
    m = np.stack([np.tile(np.where(v, 0.0, NEG_INF).astype(np.float32), (A_Q_HEADS, 1)) for v in variants])
    return jnp.asarray(m)


NA_GROUPS_PER_STEP = 4


def _attn_b_kernel(q_ref, *refs):
    ngs = NA_GROUPS_PER_STEP
    k_refs, v_refs = refs[:3 * ngs], refs[3 * ngs:6 * ngs]
    kx_ref, vx_ref = refs[6 * ngs:6 * ngs + 2]
    tab_refs = refs[6 * ngs + 2:7 * ngs + 2]
    o_ref = refs[-1]
    tq = NA_QROWS * GRID_W
    lo = lax.broadcasted_iota(I32, (tq, LANES), 1) < HEAD_DIM
    for g in range(ngs):
        rows_g = slice(g * tq, (g + 1) * tq)
        for p in range(4):
            sl = slice(p * LANES, (p + 1) * LANES)
            qs = _split_pair(q_ref[0, rows_g, sl], lo)
            k = jnp.concatenate([r[0, :, sl] for r in k_refs[3 * g:3 * g + 3]], axis=0)
            v = jnp.concatenate([r[0, :, sl] for r in v_refs[3 * g:3 * g + 3]], axis=0)
            s_loc = lax.dot_general(qs, k, _NT, preferred_element_type=F32) + tab_refs[g][0, p]
            s_ctx = lax.dot_general(qs, kx_ref[0, :, sl], _NT, preferred_element_type=F32)
            o = _softmax_pv([s_loc, s_ctx], [v, vx_ref[0, :, sl]], None)
            o_ref[0, rows_g, sl] = jnp.where(lo, o[:tq], o[tq:]).astype(BF16)


def _attn_b(zq, zc, table):
    bsz, s, _ = zq.shape
    lc = zc.shape[1]
    tq = NA_QROWS * GRID_W
    ngs = NA_GROUPS_PER_STEP
    ng = s // tq
    qb, kb, vb = _QB // 4, _KB // 4, _VB // 4

    def kv_spec(col, g, off):
        return pl.BlockSpec((1, tq, 4 * LANES),
                            lambda i, b: (b, jnp.clip(ngs * i + g - 1, 0, ng - 3) + off, col))

    def tab_spec(g):
        def variant(i):
            grp = ngs * i + g
            return jnp.where(grp == 0, 0, jnp.where(grp == ng - 1, 2, 1))
        return pl.BlockSpec((1,) + table.shape[1:], lambda i, b: (variant(i), 0, 0, 0),
                            pipeline_mode=pl.Buffered(1))

    kv_slots = [(g, off) for g in range(ngs) for off in range(3)]
    return pl.pallas_call(
        _attn_b_kernel,
        out_shape=jax.ShapeDtypeStruct((bsz, s, 4 * LANES), BF16),
        grid=(ng // ngs, bsz),
        in_specs=([pl.BlockSpec((1, ngs * tq, 4 * LANES), lambda i, b: (b, i, qb))]
                  + [kv_spec(kb, g, off) for g, off in kv_slots] + [kv_spec(vb, g, off) for g, off in kv_slots]
                  + [pl.BlockSpec((1, lc, 4 * LANES), lambda i, b: (b, 0, kb)),
                     pl.BlockSpec((1, lc, 4 * LANES), lambda i, b: (b, 0, vb))]
                  + [tab_spec(g) for g in range(ngs)]),
        out_specs=pl.BlockSpec((1, ngs * tq, 4 * LANES), lambda i, b: (b, i, 0)),
        compiler_params=_params("arbitrary", "arbitrary"),
        name="attn_b",
    )(*([zq] * (6 * ngs + 1) + [zc, zc] + [table] * ngs))


def _na_table(rpb, rows):
    ng = rows // NA_QROWS
    nq, nk = NA_QROWS * GRID_W, NA_KROWS * GRID_W
    qc = np.arange(GRID_W)
    kc = np.arange(GRID_W)
    ws = np.clip(qc - NA_KW // 2, 0, GRID_W - NA_KW)
    valid_c = (kc[None, :] >= ws[:, None]) & (kc[None, :] < ws[:, None] + NA_KW)
    dc = np.clip(kc[None, :] - qc[:, None], -(NA_KW - 1), NA_KW - 1) + (NA_KW - 1)
    c_sel = (dc[..., None] == np.arange(2 * NA_KW - 1)) & valid_c[..., None]
    tiles = jnp.einsum('hab,uvb->huav', rpb.astype(F32), jnp.asarray(c_sel, F32), precision=lax.Precision.HIGHEST)
    tiles = jnp.where(jnp.asarray(valid_c)[None, :, None, :], tiles * LOG2E, NEG_INF)
    tiles = tiles.reshape(B_HEADS, GRID_W, (2 * NA_KH - 1) * GRID_W)

    def masked(n_key_rows):
        return jnp.full((B_HEADS, GRID_W, n_key_rows * GRID_W), NEG_INF, F32)

    tabs = []
    for i in (0, 1, ng - 1):
        start = int(np.clip(NA_QROWS * i - NA_KH // 2, 0, rows - NA_KROWS))
        q_rows = []
        for qr in range(NA_QROWS):
            r = NA_QROWS * i + qr
            rs = int(np.clip(r - NA_KH // 2, 0, rows - NA_KH))
            y0 = rs - start
            d0 = rs - r + (NA_KH - 1)
            q_rows.append(jnp.concatenate([masked(y0), tiles[:, :, d0 * GRID_W:(d0 + NA_KH) * GRID_W],
                                           masked(NA_KROWS - NA_KH - y0)], axis=-1))
        tabs.append(jnp.concatenate(q_rows, axis=-2))
    return jnp.stack(tabs).reshape(3, B_HEADS // 2, 2 * nq, nk)


def _merge_kernel(oa_ref, ob_ref, x_ref, mod_ref, ga_ref, gb_ref, wo_ref, n2_ref, wrh_ref, wrl_ref, rb_ref,
                  wsg_ref, wsu_ref, wsd_ref, xres_ref, hpa_ref, hpb_ref, idx_ref, gate_ref, *, d):
    b = pl.program_id(0)
    tm = x_ref.shape[1]

    def mod(k):
        return mod_ref[pl.ds(b, 1), k * d:(k + 1) * d]

    na = _rms(oa_ref[0].astype(F32), ga_ref[...])
    nb = _rms(ob_ref[0].astype(F32), gb_ref[...])
    cat = jnp.concatenate([na, nb], axis=1).astype(BF16)
    y = jnp.dot(cat, wo_ref[...], preferred_element_type=F32)
    x1 = x_ref[0] + mod(2) * y
    h2 = _rms(x1, n2_ref[...]) * (1.0 + mod(4)) + mod(3)

    hb = h2.astype(BF16)
    act = _silu(jnp.dot(hb, wsg_ref[...], preferred_element_type=F32)) * jnp.dot(hb, wsu_ref[...],
                                                                                 preferred_element_type=F32)
    shared = jnp.dot(act.astype(BF16), wsd_ref[...], preferred_element_type=F32)
    xres_ref[0] = x1 + mod(5) * shared
    packed = _pack_bf16_pairs(h2)
    hpa_ref[...] = packed[:, :d // 4]
    hpb_ref[...] = packed[:, d // 4:]

    h_hi, h_lo = _split_bf16(h2)
    logits = (lax.dot_general(wrh_ref[...], h_hi, _NT, preferred_element_type=F32)
              + lax.dot_general(wrh_ref[...], h_lo, _NT, preferred_element_type=F32)
              + lax.dot_general(wrl_ref[...], h_hi, _NT, preferred_element_type=F32))
    scores = jax.nn.sigmoid(logits)
    sel = scores + rb_ref[...]
    per = N_EXPERTS // N_GROUPS
    g3 = sel.reshape(N_GROUPS, per, tm)
    it3 = lax.broadcasted_iota(I32, (N_GROUPS, per, tm), 1)
    m1 = jnp.max(g3, axis=1, keepdims=True)
    first = jnp.min(jnp.where(g3 == m1, it3, per), axis=1, keepdims=True)
    m2 = jnp.max(jnp.where(it3 == first, -jnp.inf, g3), axis=1, keepdims=True)
    gscore = (m1 + m2).reshape(N_GROUPS, tm)

    itg = lax.broadcasted_iota(I32, (N_GROUPS, tm), 0)
    gsel = jnp.zeros((N_GROUPS, tm), F32)
    cur = gscore
    for _ in range(TOPK_GROUPS):
        mx = jnp.max(cur, axis=0, keepdims=True)
        fi = jnp.min(jnp.where(cur == mx, itg, N_GROUPS), axis=0, keepdims=True)
        pick = itg == fi
        gsel = jnp.where(pick, 1.0, gsel)
        cur = jnp.where(pick, -jnp.inf, cur)
    emask = jnp.broadcast_to(gsel.reshape(N_GROUPS, 1, tm), (N_GROUPS, per, tm)).reshape(N_EXPERTS, tm) > 0.5

    ite = lax.broadcasted_iota(I32, (N_EXPERTS, tm), 0)
    cur = jnp.where(emask, sel, NEG_INF)
    idx_rows, s_rows = [], []
    for _ in range(TOP_K):
        mx = jnp.max(cur, axis=0, keepdims=True)
        fi = jnp.min(jnp.where(cur == mx, ite, N_EXPERTS), axis=0, keepdims=True)
        pick = ite == fi
        idx_rows.append(fi)
        s_rows.append(jnp.sum(jnp.where(pick, scores, 0.0), axis=0, keepdims=True))
        cur = jnp.where(pick, -jnp.inf, cur)
    top_s = jnp.concatenate(s_rows, axis=0)
    idx_ref[...] = jnp.concatenate(idx_rows, axis=0)
    gate_ref[...] = top_s / jnp.sum(top_s, axis=0, keepdims=True) * ROUTED_SCALE


def _merge(o_a, o_b, x, mod, ga, gb, wo, n2, wr_hi, wr_lo, rb_col, wsg, wsu, wsd, *, tm):
    bsz, s, d = x.shape
    nt = s // tm
    n = bsz * s
    full = lambda a: pl.BlockSpec(a.shape, lambda b, i: (0,) * a.ndim)
    return pl.pallas_call(
        functools.partial(_merge_kernel, d=d),
        out_shape=(jax.ShapeDtypeStruct((bsz, s, d), F32),
                   jax.ShapeDtypeStruct((n, d // 4), U32),
                   jax.ShapeDtypeStruct((n, d // 4), U32),
                   jax.ShapeDtypeStruct((TOP_K, n), I32),
                   jax.ShapeDtypeStruct((TOP_K, n), F32)),
        grid=(bsz, nt),
        in_specs=[pl.BlockSpec((1, tm, d // 2), lambda b, i: (b, i, 0)),
                  pl.BlockSpec((1, tm, d // 2), lambda b, i: (b, i, 0)),
                  pl.BlockSpec((1, tm, d), lambda b, i: (b, i, 0)),
                  full(mod), full(ga), full(gb), full(wo), full(n2), full(wr_hi), full(wr_lo), full(rb_col),
                  full(wsg), full(wsu), full(wsd)],
        out_specs=(pl.BlockSpec((1, tm, d), lambda b, i: (b, i, 0)),
                   pl.BlockSpec((tm, d // 4), lambda b, i: (b * nt + i, 0)),
                   pl.BlockSpec((tm, d // 4), lambda b, i: (b * nt + i, 0)),
                   pl.BlockSpec((TOP_K, tm), lambda b, i: (0, b * nt + i)),
                   pl.BlockSpec((TOP_K, tm), lambda b, i: (0, b * nt + i))),
        compiler_params=_params("arbitrary", "arbitrary"),
        name="merge",
    )(o_a, o_b, x, mod, ga, gb, wo, n2, wr_hi, wr_lo, rb_col, wsg, wsu, wsd)


def _rank_kernel(idx_ref, rank_ref, cnt_ref, carry_ref):
    tm = idx_ref.shape[1]

    @pl.when(pl.program_id(0) == 0)
    def _():
        carry_ref[...] = jnp.zeros_like(carry_ref)

    idx = idx_ref[...]
    ite = lax.broadcasted_iota(I32, (N_EXPERTS, tm), 0)
    before = (lax.broadcasted_iota(I32, (tm, tm), 0) < lax.broadcasted_iota(I32, (tm, tm), 1)).astype(BF16)
    hits = [ite == idx[k:k + 1, :] for k in range(TOP_K)]
    routed = jnp.where(functools.reduce(jnp.logical_or, hits), 1.0, 0.0)
    base = carry_ref[...]
    ahead = jnp.dot(routed.astype(BF16), before, preferred_element_type=F32) + base
    rows = [jnp.sum(jnp.where(hit, ahead, 0.0), axis=0, keepdims=True) for hit in hits]
    rank_ref[...] = jnp.concatenate(rows, axis=0).astype(I32)
    total = base + jnp.sum(routed, axis=1, keepdims=True)
    carry_ref[...] = total
    cnt_ref[...] = total


def _rank(idx, *, tm):
    n = idx.shape[1]
    return pl.pallas_call(
        _rank_kernel,
        out_shape=(jax.ShapeDtypeStruct((TOP_K, n), I32), jax.ShapeDtypeStruct((N_EXPERTS, 1), F32)),
        grid=(n // tm,),
        in_specs=[pl.BlockSpec((TOP_K, tm), lambda i: (0, i))],
        out_specs=(pl.BlockSpec((TOP_K, tm), lambda i: (0, i)), pl.BlockSpec((N_EXPERTS, 1), lambda i: (0, 0))),
        scratch_shapes=[pltpu.VMEM((N_EXPERTS, 1), F32)],
        compiler_params=_params("arbitrary"),
        name="rank",
    )(idx)


def _pos_kernel(idx_ref, rank_ref, pstart_ref, pos_ref):
    tm = idx_ref.shape[1]
    idx = idx_ref[...]
    ite = lax.broadcasted_iota(I32, (N_EXPERTS, tm), 0)
    pstart = pstart_ref[...]
    rows = [jnp.sum(jnp.where(ite == idx[k:k + 1, :], pstart, 0.0), axis=0, keepdims=True) for k in range(TOP_K)]
    pos_ref[...] = jnp.concatenate(rows, axis=0).astype(I32) + rank_ref[...]


def _pos(idx, rank, pstart_col, *, tm):
    n = idx.shape[1]
    return pl.pallas_call(
        _pos_kernel,
        out_shape=jax.ShapeDtypeStruct((TOP_K, n), I32),
        grid=(n // tm,),
        in_specs=[pl.BlockSpec((TOP_K, tm), lambda i: (0, i)),
                  pl.BlockSpec((TOP_K, tm), lambda i: (0, i)),
                  pl.BlockSpec((N_EXPERTS, 1), lambda i: (0, 0))],
        out_specs=pl.BlockSpec((TOP_K, tm), lambda i: (0, i)),
        compiler_params=_params("arbitrary"),
        name="pos",
    )(idx, rank, pstart_col)


UNIT_CHUNKS = (4, 2, 1)
RING_AHEAD = 4
RING_SLOTS = RING_AHEAD + UNIT_CHUNKS[0]


def _expert_kernel(ps_ref, cnt_ref, tot_ref, wg_ref, wu_ref, wd_ref, xa_hbm, xb_hbm, ya_hbm, yb_hbm,
                   xa_buf, xb_buf, ya_buf, yb_buf, in_sem, out_sem):
    e = pl.program_id(0)
    r = EXPERT_ROWS
    ahead = RING_AHEAD
    total = tot_ref[0]
    first = ps_ref[e] // r
    cnt = cnt_ref[e]
    nch = (cnt + r - 1) // r

    def rows_of(g):
        return pl.ds(pl.multiple_of(g * r, r), r)

    def fetch(g):
        slot = g % RING_SLOTS
        return (pltpu.make_async_copy(xa_hbm.at[rows_of(g)], xa_buf.at[slot], in_sem.at[0, slot]),
                pltpu.make_async_copy(xb_hbm.at[rows_of(g)], xb_buf.at[slot], in_sem.at[1, slot]))

    def flush(g):
        slot = g % RING_SLOTS
        return (pltpu.make_async_copy(ya_buf.at[slot], ya_hbm.at[rows_of(g)], out_sem.at[0, slot]),
                pltpu.make_async_copy(yb_buf.at[slot], yb_hbm.at[rows_of(g)], out_sem.at[1, slot]))

    def start(copies):
        for cp in copies:
            cp.start()

    def wait(copies):
        for cp in copies:
            cp.wait()

    @pl.when(e == 0)
    def _():
        for g in range(ahead):
            @pl.when(g < total)
            def _():
                start(fetch(g))

    @pl.when(nch > 0)
    def _():
        def unit(c0, u):
            for j in range(u):
                g = first + c0 + j
                wait(fetch(g))

                @pl.when(g + ahead < total)
                def _():
                    start(fetch(g + ahead))

                @pl.when(g >= RING_SLOTS)
                def _():
                    wait(flush(g - RING_SLOTS))

            slots = [(first + c0 + j) % RING_SLOTS for j in range(u)]
            packed = jnp.concatenate([jnp.concatenate([xa_buf[sl], xb_buf[sl]], axis=1) for sl in slots], axis=0)
            row = lax.broadcasted_iota(I32, packed.shape, 0)
            packed = jnp.where(row < cnt - c0 * r, packed, jnp.zeros_like(packed))
            x = _unpack_bf16_pairs(packed).astype(BF16)
            gate = jnp.dot(x, wg_ref[0].astype(BF16), preferred_element_type=F32)
            up = jnp.dot(x, wu_ref[0].astype(BF16), preferred_element_type=F32)
            y = jnp.dot((_silu(gate) * up).astype(BF16), wd_ref[0].astype(BF16), preferred_element_type=F32)
            out = _pack_bf16_pairs(y)
            half = out.shape[1] // 2
            for j, sl in enumerate(slots):
                ya_buf[sl] = out[j * r:(j + 1) * r, :half]
                yb_buf[sl] = out[j * r:(j + 1) * r, half:]
                start(flush(first + c0 + j))

        big = UNIT_CHUNKS[0]

        def big_unit(i, carry):
            unit(i * big, big)
            return carry

        lax.fori_loop(0, nch // big, big_unit, 0)
        done = nch // big * big
        for u in UNIT_CHUNKS[1:]:
            @pl.when((nch % (2 * u)) >= u)
            def _():
                unit(done, u)

            done = done + jnp.where((nch % (2 * u)) >= u, u, 0)

    @pl.when(e == pl.num_programs(0) - 1)
    def _():
        for k in range(RING_SLOTS):
            @pl.when(total - 1 - k >= 0)
            def _():
                wait(flush(total - 1 - k))


def _experts(pstart, cnt, total_chunks, xs_a, xs_b, wg, wu, wd):
    rows, hw = xs_a.shape
    r = EXPERT_ROWS
    n_exp, d, f = wg.shape
    hbm = pl.BlockSpec(memory_space=pl.ANY)
    grid_spec = pltpu.PrefetchScalarGridSpec(
        num_scalar_prefetch=3,
        grid=(n_exp,),
        in_specs=[pl.BlockSpec((1, d, f), lambda e, ps, cn, tot: (e, 0, 0)),
                  pl.BlockSpec((1, d, f), lambda e, ps, cn, tot: (e, 0, 0)),
                  pl.BlockSpec((1, f, d), lambda e, ps, cn, tot: (e, 0, 0)),
                  hbm, hbm],
        out_specs=(hbm, hbm),
        scratch_shapes=[pltpu.VMEM((RING_SLOTS, r, hw), U32), pltpu.VMEM((RING_SLOTS, r, hw), U32),
                        pltpu.VMEM((RING_SLOTS, r, hw), U32), pltpu.VMEM((RING_SLOTS, r, hw), U32),
                        pltpu.SemaphoreType.DMA((2, RING_SLOTS)), pltpu.SemaphoreType.DMA((2, RING_SLOTS))],
    )
    return pl.pallas_call(
        _expert_kernel,
        out_shape=(jax.ShapeDtypeStruct((rows, hw), U32), jax.ShapeDtypeStruct((rows, hw), U32)),
        grid_spec=grid_spec,
        compiler_params=_params("arbitrary"),
        name="experts",
    )(pstart, cnt, total_chunks, wg, wu, wd, xs_a, xs_b)


def _sc_scatter_rows(x, idx_flat, n_rows):
    n, w = x.shape
    m = idx_flat.shape[1]
    nwin = n // SC_WINDOW
    reps = m // n
    mesh = plsc.VectorSubcoreMesh(core_axis_name="core", subcore_axis_name="subcore")

    @functools.partial(pl.kernel, out_type=jax.ShapeDtypeStruct((n_rows, w), x.dtype), mesh=mesh,
                       scratch_types=[])
    def scatter_kernel(x_hbm, i_hbm, o_hbm):
        def body(x_vmem, i_vmem):
            pltpu.sync_copy(x_vmem, o_hbm.at[i_vmem.at[0]])

        pltpu.emit_pipeline(
            body,
            grid=(nwin, reps),
            in_specs=[pl.BlockSpec((SC_WINDOW, w), lambda i, k: (i, 0)),
                      pl.BlockSpec((1, SC_WINDOW), lambda i, k: (0, k * nwin + i))],
            out_specs=[],
            core_axis_name=("core", "subcore"),
            dimension_semantics=(pltpu.PARALLEL, pltpu.ARBITRARY),
        )(x_hbm, i_hbm)

    return scatter_kernel(x, idx_flat)


def _sc_gather_rows(src, idx_flat):
    m = idx_flat.shape[1]
    w = src.shape[1]
    mesh = plsc.VectorSubcoreMesh(core_axis_name="core", subcore_axis_name="subcore")

    @functools.partial(pl.kernel, out_type=jax.ShapeDtypeStruct((m, w), src.dtype), mesh=mesh)
    def gather_kernel(x_hbm, i_hbm, o_hbm):
        def body(i_vmem, o_vmem):
            pltpu.sync_copy(x_hbm.at[i_vmem.at[0]], o_vmem)

        pltpu.emit_pipeline(
            body,
            grid=(m // SC_WINDOW,),
            in_specs=[pl.BlockSpec((1, SC_WINDOW), lambda i: (0, i))],
            out_specs=[pl.BlockSpec((SC_WINDOW, w), lambda i: (i, 0))],
            core_axis_name=("core", "subcore"),
            dimension_semantics=(pltpu.PARALLEL,),
        )(i_hbm, o_hbm)

    return gather_kernel(src, idx_flat)


def _finish_kernel(gate_ref, xres_ref, mod_ref, ya_ref, yb_ref, o_ref, *, d, tiles_per_batch):
    tm = gate_ref.shape[1]
    b = pl.program_id(0) // tiles_per_batch
    gates = gate_ref[...].T
    acc = jnp.zeros((tm, d), F32)
    for k in range(TOP_K):
        packed = jnp.concatenate([ya_ref[k], yb_ref[k]], axis=1)
        acc = acc + _unpack_bf16_pairs(packed) * gates[:, k:k + 1]
    o_ref[...] = xres_ref[...] + mod_ref[pl.ds(b, 1), 5 * d:6 * d] * acc


def _finish(gates, xres, mod, yg_a, yg_b, *, tm, tiles_per_batch):
    n, d = xres.shape
    return pl.pallas_call(
        functools.partial(_finish_kernel, d=d, tiles_per_batch=tiles_per_batch),
        out_shape=jax.ShapeDtypeStruct((n, d), F32),
        grid=(n // tm,),
        in_specs=[pl.BlockSpec((TOP_K, tm), lambda i: (0, i)),
                  pl.BlockSpec((tm, d), lambda i: (i, 0)),
                  pl.BlockSpec(mod.shape, lambda i: (0, 0)),
                  pl.BlockSpec((TOP_K, tm, d // 4), lambda i: (0, i, 0)),
                  pl.BlockSpec((TOP_K, tm, d // 4), lambda i: (0, i, 0))],
        out_specs=pl.BlockSpec((tm, d), lambda i: (i, 0)),
        compiler_params=_params("arbitrary"),
        name="finish",
    )(gates, xres, mod, yg_a, yg_b)


_QA_HEAD_ORDER = (0, 4, 1, 5, 2, 6, 3, 7)


def _reorder_qa_heads(a, axis):
    return jnp.concatenate([lax.slice_in_dim(a, h * HEAD_DIM, (h + 1) * HEAD_DIM, axis=axis)
                            for h in _QA_HEAD_ORDER], axis=axis)


def _rope_tables(s):
    quarter = HEAD_DIM // 4
    t = jnp.arange(s)
    row = (t // GRID_W).astype(F32)
    col = (t % GRID_W).astype(F32)
    freqs = ROPE_BASE ** (-jnp.arange(quarter, dtype=F32) / quarter)
    ar = row[:, None] * freqs[None, :]
    ac = col[:, None] * freqs[None, :]
    cos = jnp.concatenate([jnp.cos(ar), jnp.cos(ar), jnp.cos(ac), jnp.cos(ac)], axis=1)
    sin = jnp.concatenate([-jnp.sin(ar), jnp.sin(ar), -jnp.sin(ac), jnp.sin(ac)], axis=1)
    return jnp.tile(cos, (1, 2)), jnp.tile(sin, (1, 2))


def kernel(x, c, ctx, c_ctx, w_ada, b_ada, norm1, norm2, w_in, q_norm_a, k_norm_a, q_norm_b, k_norm_b, sink_a,
           rpb_b, out_norm_a, out_norm_b, w_out, w_router, router_bias, we_gate, we_up, we_down, ws_gate, ws_up,
           ws_down):
    assert w_ada.shape[0] == 1, "single-layer block"
    bsz, s, d = x.shape
    lc = ctx.shape[1]
    n = bsz * s
    rows = s // GRID_W
    assert s % (NA_QROWS * GRID_W) == 0 and rows >= NA_KROWS and bsz <= 4 and d == 1024

    c8 = jnp.concatenate([c, c_ctx[None, :], jnp.zeros((8 - bsz - 1, d), F32)], axis=0)
    mod = _ada(c8, w_ada[0], b_ada[0][None, :])

    w = w_in[0]
    aw, akw, bw = A_Q_HEADS * HEAD_DIM, A_KV_HEADS * HEAD_DIM, B_HEADS * HEAD_DIM
    cuts = np.cumsum([0, aw, akw, akw, bw, bw, bw])
    qa_w, ka_w, va_w, qb_w, kb_w, vb_w = [w[:, cuts[i]:cuts[i + 1]] for i in range(6)]
    qa_w = _reorder_qa_heads(qa_w, 1)
    w_ext = jnp.concatenate([qa_w, qb_w, kb_w, vb_w, ka_w, va_w], axis=1).astype(BF16)
    scale = HEAD_DIM ** -0.5 * LOG2E
    pair = lambda g: jnp.tile(g, 2)
    gains = jnp.stack([pair(q_norm_a[0]) * scale, pair(q_norm_b[0]) * scale, pair(k_norm_b[0]), pair(k_norm_a[0])]
                      + [jnp.zeros((LANES,), F32)] * 4)
    bd = jnp.asarray(np.kron(np.eye(2), np.full((HEAD_DIM, HEAD_DIM), 1.0 / HEAD_DIM)), BF16)
    cos_t, sin_t = _rope_tables(s)
    n1 = norm1[0][None, :]
    zq = _inproj(x, mod, n1, w_ext, gains, cos_t, sin_t, bd, mod_row=None, tm=512)
    zc = _inproj(ctx, mod, n1, w_ext, gains, jnp.ones((lc, LANES), F32), jnp.zeros((lc, LANES), F32), bd,
                 mod_row=bsz, tm=lc)

    sink = sink_a[0].astype(F32)
    sink_rep = jnp.concatenate([jnp.broadcast_to(sink[h] * LOG2E, (A_WINDOW, LANES)) for h in _QA_HEAD_ORDER])
    o_a = _attn_a(zq, zc, sink_rep, _attn_a_mask(A_WINDOW))
    o_b = _attn_b(zq, zc, _na_table(rpb_b[0], rows))

    ga = _reorder_qa_heads(out_norm_a[0], 0)[None, :]
    gb = out_norm_b[0][None, :]
    wo = jnp.concatenate([_reorder_qa_heads(w_out[0][:aw], 0), w_out[0][aw:]], axis=0).astype(BF16)
    wr_hi, wr_lo = _split_bf16(w_router[0].T)
    xres, hp_a, hp_b, idx, gates = _merge(o_a, o_b, x, mod, ga, gb, wo, norm2[0][None, :], wr_hi, wr_lo,
                                  router_bias[0][:, None], ws_gate[0].astype(BF16), ws_up[0].astype(BF16),
                                  ws_down[0].astype(BF16), tm=512)

    rank, counts = _rank(idx, tm=512)
    t = EXPERT_ROWS
    cnt = counts[:, 0].astype(I32)
    padded = (cnt + t - 1) // t * t
    pends = jnp.cumsum(padded)
    pstart = pends - padded
    n_rows = n * TOP_K + N_EXPERTS * t
    pos = _pos(idx, rank, pstart.astype(F32)[:, None], tm=min(2048, n))

    pos_flat = pos.reshape(1, TOP_K * n)
    xs_a = _sc_scatter_rows(hp_a, pos_flat, n_rows)
    xs_b = _sc_scatter_rows(hp_b, pos_flat, n_rows)
    ys_a, ys_b = _experts(pstart, cnt, pends[-1:] // t, xs_a, xs_b, we_gate[0], we_up[0], we_down[0])
    yg_a = _sc_gather_rows(ys_a, pos_flat).reshape(TOP_K, n, d // 4)
    yg_b = _sc_gather_rows(ys_b, pos_flat).reshape(TOP_K, n, d // 4)
    out = _finish(gates, xres.reshape(n, d), mod, yg_a, yg_b, tm=512, tiles_per_batch=s // 512)
    return out.reshape(bsz, s, d)
```

```python
import functools

import numpy as np
import jax
import jax.numpy as jnp
from jax import lax
from jax.experimental import pallas as pl
from jax.experimental.pallas import tpu as pltpu
from jax.experimental.pallas import tpu_sc as plsc

F32 = jnp.float32
BF16 = jnp.bfloat16
I32 = jnp.int32
U32 = jnp.uint32

LANES = 128
HEAD_DIM = 64
HEAD_PAIR = 2 * HEAD_DIM
GRID_W = 64
A_Q_HEADS = 8
A_KV_HEADS = 2
A_WINDOW = 128
B_HEADS = 8
NA_KH = 8
NA_KW = 16
NA_QROWS = 4
NA_KROWS = NA_QROWS + NA_KH
ROPE_BASE = 10000.0
N_EXPERTS = 256
TOP_K = 8
N_GROUPS = 8
TOPK_GROUPS = 4
ROUTED_SCALE = 2.5
LOG2E = 1.4426950408889634
EPS = 1e-6
NEG_INF = -1e30
EXPERT_ROWS = 128
SC_WINDOW = 128
VMEM_LIMIT = 56 * 1024 * 1024

_NT = (((1,), (1,)), ((), ()))


def _params(*sem):
    return pltpu.CompilerParams(dimension_semantics=sem, vmem_limit_bytes=VMEM_LIMIT)


def _silu(v):
    return v * jax.nn.sigmoid(v)


def _rms(v, gain):
    return v * lax.rsqrt(jnp.mean(v * v, axis=-1, keepdims=True) + EPS) * gain


def _pack_bf16_pairs(v):
    n = v.shape[1] // 2
    lo = lax.bitcast_convert_type(v[:, :n].astype(BF16).astype(F32), U32) >> 16
    hi = lax.bitcast_convert_type(v[:, n:].astype(BF16).astype(F32), U32) & jnp.uint32(0xFFFF0000)
    return hi | lo


def _split_bf16(v):
    hi = lax.bitcast_convert_type(lax.bitcast_convert_type(v, U32) & jnp.uint32(0xFFFF0000), F32)
    return hi.astype(BF16), (v - hi).astype(BF16)


def _unpack_bf16_pairs(w):
    lo = lax.bitcast_convert_type(w << 16, F32)
    hi = lax.bitcast_convert_type(w & jnp.uint32(0xFFFF0000), F32)
    return jnp.concatenate([lo, hi], axis=1)


def _ada_kernel(c_ref, w_ref, b_ref, o_ref):
    a = _silu(c_ref[...])
    o_ref[...] = jnp.dot(a, w_ref[...], preferred_element_type=F32,
                         precision=lax.Precision.HIGHEST) + b_ref[...]


def _ada(c8, w, b):
    d, n = w.shape
    bn = n // 4
    return pl.pallas_call(
        _ada_kernel,
        out_shape=jax.ShapeDtypeStruct((8, n), F32),
        grid=(n // bn,),
        in_specs=[pl.BlockSpec((8, d), lambda j: (0, 0)),
                  pl.BlockSpec((d, bn), lambda j: (0, j)),
                  pl.BlockSpec((1, bn), lambda j: (0, j))],
        out_specs=pl.BlockSpec((8, bn), lambda j: (0, j)),
        compiler_params=_params("arbitrary"),
        name="ada",
    )(c8, w, b)


_QA, _QB, _KB, _VB, _KA, _VA = 0, 4, 8, 12, 16, 17
_OUT_BLOCKS = 18


def _inproj_kernel(x_ref, mod_ref, n1_ref, w_ref, g_ref, cos_ref, sin_ref, bd_ref, o_ref, *, mod_row, d):
    b = pl.program_id(0) if mod_row is None else mod_row
    xn = _rms(x_ref[0], n1_ref[...])
    sh = mod_ref[pl.ds(b, 1), 0:d]
    sc = mod_ref[pl.ds(b, 1), d:2 * d]
    h = (xn * (1.0 + sc) + sh).astype(BF16)
    z = jnp.dot(h, w_ref[...], preferred_element_type=F32)
    bd = bd_ref[...]
    cos = cos_ref[...]
    sin = sin_ref[...]

    def blk(j):
        return z[:, j * LANES:(j + 1) * LANES]

    def head_rinv(zb):
        ms = jnp.dot((zb * zb).astype(BF16), bd, preferred_element_type=F32)
        return lax.rsqrt(ms + EPS)

    def put(j, v):
        o_ref[0, :, j * LANES:(j + 1) * LANES] = v.astype(BF16)

    quarter = HEAD_DIM // 4
    first_half = (lax.broadcasted_iota(I32, (1, LANES), 1) % (2 * quarter)) < quarter

    def roped(j, g_row):
        zb = blk(j)
        zn = zb * head_rinv(zb) * g_ref[g_row:g_row + 1, :]
        partner = jnp.where(first_half, pltpu.roll(zn, LANES - quarter, 1), pltpu.roll(zn, quarter, 1))
        put(j, zn * cos + partner * sin)

    def normed(j, g_row):
        zb = blk(j)
        put(j, zb * head_rinv(zb) * g_ref[g_row:g_row + 1, :])

    for j in range(4):
        roped(_QA + j, 0)
        normed(_QB + j, 1)
        normed(_KB + j, 2)
        put(_VB + j, blk(_VB + j))
    roped(_KA, 3)
    put(_VA, blk(_VA))


def _inproj(x, mod, n1, w_ext, gains, cos_t, sin_t, bd, *, mod_row, tm):
    bsz, s, d = x.shape
    kern = functools.partial(_inproj_kernel, mod_row=mod_row, d=d)
    return pl.pallas_call(
        kern,
        out_shape=jax.ShapeDtypeStruct((bsz, s, _OUT_BLOCKS * LANES), BF16),
        grid=(bsz, s // tm),
        in_specs=[pl.BlockSpec((1, tm, d), lambda b, i: (b, i, 0)),
                  pl.BlockSpec(mod.shape, lambda b, i: (0, 0)),
                  pl.BlockSpec((1, d), lambda b, i: (0, 0)),
                  pl.BlockSpec(w_ext.shape, lambda b, i: (0, 0)),
                  pl.BlockSpec(gains.shape, lambda b, i: (0, 0)),
                  pl.BlockSpec((tm, LANES), lambda b, i: (i, 0)),
                  pl.BlockSpec((tm, LANES), lambda b, i: (i, 0)),
                  pl.BlockSpec(bd.shape, lambda b, i: (0, 0))],
        out_specs=pl.BlockSpec((1, tm, _OUT_BLOCKS * LANES), lambda b, i: (b, i, 0)),
        compiler_params=_params("arbitrary", "arbitrary"),
        name="inproj",
    )(x, mod, n1, w_ext, gains, cos_t, sin_t, bd)


def _split_pair(qp, lo):
    zero = jnp.zeros_like(qp)
    return jnp.concatenate([jnp.where(lo, qp, zero), jnp.where(lo, zero, qp)], axis=0)


def _softmax_pv(s_parts, v_parts, sink_rep):
    chunks = [s[:, c * LANES:(c + 1) * LANES] for s in s_parts for c in range(s.shape[1] // LANES)]
    m = jnp.max(functools.reduce(jnp.maximum, chunks), axis=-1, keepdims=True)
    m_rep = jnp.broadcast_to(m, (m.shape[0], LANES))
    if sink_rep is not None:
        m_rep = jnp.maximum(m_rep, sink_rep)
    acc = None
    for s, v in zip(s_parts, v_parts):
        p = jnp.concatenate([jnp.exp2(s[:, c * LANES:(c + 1) * LANES] - m_rep)
                             for c in range(s.shape[1] // LANES)], axis=1).astype(BF16)
        v_ext = jnp.concatenate([v, jnp.ones_like(v)], axis=1)
        o = jnp.dot(p, v_ext, preferred_element_type=F32)
        acc = o if acc is None else acc + o
    l_rep = acc[:, LANES:]
    if sink_rep is not None:
        l_rep = l_rep + jnp.exp2(sink_rep - m_rep)
    return acc[:, :LANES] * (1.0 / l_rep)


A_BLOCKS_PER_STEP = 4


def _attn_a_kernel(q_ref, *refs):
    nq = A_BLOCKS_PER_STEP
    k_refs, v_refs = refs[:nq + 2], refs[nq + 2:2 * nq + 4]
    kx_ref, vx_ref, sink_ref = refs[2 * nq + 4:2 * nq + 7]
    mask_refs = refs[2 * nq + 7:3 * nq + 7]
    o_ref = refs[-1]
    tq = A_WINDOW
    lo = lax.broadcasted_iota(I32, (tq, LANES), 1) < HEAD_DIM
    for h in range(nq):
        q = q_ref[0, h * tq:(h + 1) * tq]
        qs = jnp.concatenate([_split_pair(q[:, p * LANES:(p + 1) * LANES], lo) for p in range(4)], axis=0)
        k = jnp.concatenate([r[0] for r in k_refs[h:h + 3]], axis=0)
        v = jnp.concatenate([r[0] for r in v_refs[h:h + 3]], axis=0)
        s_loc = lax.dot_general(qs, k, _NT, preferred_element_type=F32) + mask_refs[h][0]
        s_ctx = lax.dot_general(qs, kx_ref[0], _NT, preferred_element_type=F32)
        o = _softmax_pv([s_loc, s_ctx], [v, vx_ref[0]], sink_ref[...])
        for p in range(4):
            o_lo = o[(2 * p) * tq:(2 * p + 1) * tq]
            o_hi = o[(2 * p + 1) * tq:(2 * p + 2) * tq]
            o_ref[0, h * tq:(h + 1) * tq, p * LANES:(p + 1) * LANES] = jnp.where(lo, o_lo, o_hi).astype(BF16)


def _attn_a(zq, zc, sink_rep, mask):
    bsz, s, _ = zq.shape
    lc = zc.shape[1]
    tq = A_WINDOW
    nq = A_BLOCKS_PER_STEP
    nblk = s // tq
    ka, va = _KA, _VA

    def kv_spec(col, shift):
        return pl.BlockSpec((1, tq, LANES), lambda b, j: (b, jnp.clip(nq * j + shift, 0, nblk - 1), col))

    def mask_spec(h):
        def variant(j):
            blk = nq * j + h
            return jnp.where(blk == 0, 0, jnp.where(blk == nblk - 1, 2, 1))
        return pl.BlockSpec((1,) + mask.shape[1:], lambda b, j: (variant(j), 0, 0))

    shifts = range(-1, nq + 1)
    return pl.pallas_call(
        _attn_a_kernel,
        out_shape=jax.ShapeDtypeStruct((bsz, s, 4 * LANES), BF16),
        grid=(bsz, nblk // nq),
        in_specs=([pl.BlockSpec((1, nq * tq, 4 * LANES), lambda b, j: (b, j, 0))]
                  + [kv_spec(ka, sh) for sh in shifts] + [kv_spec(va, sh) for sh in shifts]
                  + [pl.BlockSpec((1, lc, LANES), lambda b, j: (b, 0, ka)),
                     pl.BlockSpec((1, lc, LANES), lambda b, j: (b, 0, va)),
                     pl.BlockSpec(sink_rep.shape, lambda b, j: (0, 0))]
                  + [mask_spec(h) for h in range(nq)]),
        out_specs=pl.BlockSpec((1, nq * tq, 4 * LANES), lambda b, j: (b, j, 0)),
        compiler_params=_params("arbitrary", "arbitrary"),
        name="attn_a",
    )(*([zq] * (2 * nq + 5) + [zc, zc, sink_rep] + [mask] * nq))


def _attn_a_mask(tq):
    qi = np.arange(tq)[:, None]
    kj = np.arange(3 * tq)[None, :]
    ok = (kj >= qi) & (kj <= qi + 2 * tq)
    variants = [ok & (kj >= tq), ok, ok & (kj < 2 * tq)]
    m = np.stack([np.tile(np.where(v, 0.0, NEG_INF).astype(np.float32), (A_Q_HEADS, 1)) for v in variants])
    return jnp.asarray(m)


NA_GROUPS_PER_STEP = 4


def _attn_b_kernel(q_ref, *refs):
    ngs = NA_GROUPS_PER_STEP
    k_refs, v_refs = refs[:3 * ngs], refs[3 * ngs:6 * ngs]
    kx_ref, vx_ref = refs[6 * ngs:6 * ngs + 2]
    tab_refs = refs[6 * ngs + 2:7 * ngs + 2]
    o_ref = refs[-1]
    tq = NA_QROWS * GRID_W
    lo = lax.broadcasted_iota(I32, (tq, LANES), 1) < HEAD_DIM
    for g in range(ngs):
        rows_g = slice(g * tq, (g + 1) * tq)
        for p in range(4):
            sl = slice(p * LANES, (p + 1) * LANES)
            qs = _split_pair(q_ref[0, rows_g, sl], lo)
            k = jnp.concatenate([r[0, :, sl] for r in k_refs[3 * g:3 * g + 3]], axis=0)
            v = jnp.concatenate([r[0, :, sl] for r in v_refs[3 * g:3 * g + 3]], axis=0)
            s_loc = lax.dot_general(qs, k, _NT, preferred_element_type=F32) + tab_refs[g][0, p]
            s_ctx = lax.dot_general(qs, kx_ref[0, :, sl], _NT, preferred_element_type=F32)
            o = _softmax_pv([s_loc, s_ctx], [v, vx_ref[0, :, sl]], None)
            o_ref[0, rows_g, sl] = jnp.where(lo, o[:tq], o[tq:]).astype(BF16)


def _attn_b(zq, zc, table):
    bsz, s, _ = zq.shape
    lc = zc.shape[1]
    tq = NA_QROWS * GRID_W
    ngs = NA_GROUPS_PER_STEP
    ng = s // tq
    qb, kb, vb = _QB // 4, _KB // 4, _VB // 4

    def kv_spec(col, g, off):
        return pl.BlockSpec((1, tq, 4 * LANES),
                            lambda i, b: (b, jnp.clip(ngs * i + g - 1, 0, ng - 3) + off, col))

    def tab_spec(g):
        def variant(i):
            grp = ngs * i + g
            return jnp.where(grp == 0, 0, jnp.where(grp == ng - 1, 2, 1))
        return pl.BlockSpec((1,) + table.shape[1:], lambda i, b: (variant(i), 0, 0, 0),
                            pipeline_mode=pl.Buffered(1))

    kv_slots = [(g, off) for g in range(ngs) for off in range(3)]
    return pl.pallas_call(
        _attn_b_kernel,
        out_shape=jax.ShapeDtypeStruct((bsz, s, 4 * LANES), BF16),
        grid=(ng // ngs, bsz),
        in_specs=([pl.BlockSpec((1, ngs * tq, 4 * LANES), lambda i, b: (b, i, qb))]
                  + [kv_spec(kb, g, off) for g, off in kv_slots] + [kv_spec(vb, g, off) for g, off in kv_slots]
                  + [pl.BlockSpec((1, lc, 4 * LANES), lambda i, b: (b, 0, kb)),
                     pl.BlockSpec((1, lc, 4 * LANES), lambda i, b: (b, 0, vb))]
                  + [tab_spec(g) for g in range(ngs)]),
        out_specs=pl.BlockSpec((1, ngs * tq, 4 * LANES), lambda i, b: (b, i, 0)),
        compiler_params=_params("arbitrary", "arbitrary"),
        name="attn_b",
    )(*([zq] * (6 * ngs + 1) + [zc, zc] + [table] * ngs))


def _na_table(rpb, rows):
    ng = rows // NA_QROWS
    nq, nk = NA_QROWS * GRID_W, NA_KROWS * GRID_W
    qc = np.arange(GRID_W)
    kc = np.arange(GRID_W)
    ws = np.clip(qc - NA_KW // 2, 0, GRID_W - NA_KW)
    valid_c = (kc[None, :] >= ws[:, None]) & (kc[None, :] < ws[:, None] + NA_KW)
    dc = np.clip(kc[None, :] - qc[:, None], -(NA_KW - 1), NA_KW - 1) + (NA_KW - 1)
    c_sel = (dc[..., None] == np.arange(2 * NA_KW - 1)) & valid_c[..., None]
    tiles = jnp.einsum('hab,uvb->huav', rpb.astype(F32), jnp.asarray(c_sel, F32), precision=lax.Precision.HIGHEST)
    tiles = jnp.where(jnp.asarray(valid_c)[None, :, None, :], tiles * LOG2E, NEG_INF)
    tiles = tiles.reshape(B_HEADS, GRID_W, (2 * NA_KH - 1) * GRID_W)

    def masked(n_key_rows):
        return jnp.full((B_HEADS, GRID_W, n_key_rows * GRID_W), NEG_INF, F32)

    tabs = []
    for i in (0, 1, ng - 1):
        start = int(np.clip(NA_QROWS * i - NA_KH // 2, 0, rows - NA_KROWS))
        q_rows = []
        for qr in range(NA_QROWS):
            r = NA_QROWS * i + qr
            rs = int(np.clip(r - NA_KH // 2, 0, rows - NA_KH))
            y0 = rs - start
            d0 = rs - r + (NA_KH - 1)
            q_rows.append(jnp.concatenate([masked(y0), tiles[:, :, d0 * GRID_W:(d0 + NA_KH) * GRID_W],
                                           masked(NA_KROWS - NA_KH - y0)], axis=-1))
        tabs.append(jnp.concatenate(q_rows, axis=-2))
    return jnp.stack(tabs).reshape(3, B_HEADS // 2, 2 * nq, nk)


def _merge_kernel(oa_ref, ob_ref, x_ref, mod_ref, ga_ref, gb_ref, wo_ref, n2_ref, wrh_ref, wrl_ref, rb_ref,
                  wsg_ref, wsu_ref, wsd_ref, xres_ref, hpa_ref, hpb_ref, idx_ref, gate_ref, *, d):
    b = pl.program_id(0)
    tm = x_ref.shape[1]

    def mod(k):
        return mod_ref[pl.ds(b, 1), k * d:(k + 1) * d]

    na = _rms(oa_ref[0].astype(F32), ga_ref[...])
    nb = _rms(ob_ref[0].astype(F32), gb_ref[...])
    cat = jnp.concatenate([na, nb], axis=1).astype(BF16)
    y = jnp.dot(cat, wo_ref[...], preferred_element_type=F32)
    x1 = x_ref[0] + mod(2) * y
    h2 = _rms(x1, n2_ref[...]) * (1.0 + mod(4)) + mod(3)

    hb = h2.astype(BF16)
    act = _silu(jnp.dot(hb, wsg_ref[...], preferred_element_type=F32)) * jnp.dot(hb, wsu_ref[...],
                                                                                 preferred_element_type=F32)
    shared = jnp.dot(act.astype(BF16), wsd_ref[...], preferred_element_type=F32)
    xres_ref[0] = x1 + mod(5) * shared
    packed = _pack_bf16_pairs(h2)
    hpa_ref[...] = packed[:, :d // 4]
    hpb_ref[...] = packed[:, d // 4:]

    h_hi, h_lo = _split_bf16(h2)
    logits = (lax.dot_general(wrh_ref[...], h_hi, _NT, preferred_element_type=F32)
              + lax.dot_general(wrh_ref[...], h_lo, _NT, preferred_element_type=F32)
              + lax.dot_general(wrl_ref[...], h_hi, _NT, preferred_element_type=F32))
    scores = jax.nn.sigmoid(logits)
    sel = scores + rb_ref[...]
    per = N_EXPERTS // N_GROUPS
    g3 = sel.reshape(N_GROUPS, per, tm)
    it3 = lax.broadcasted_iota(I32, (N_GROUPS, per, tm), 1)
    m1 = jnp.max(g3, axis=1, keepdims=True)
    first = jnp.min(jnp.where(g3 == m1, it3, per), axis=1, keepdims=True)
    m2 = jnp.max(jnp.where(it3 == first, -jnp.inf, g3), axis=1, keepdims=True)
    gscore = (m1 + m2).reshape(N_GROUPS, tm)

    itg = lax.broadcasted_iota(I32, (N_GROUPS, tm), 0)
    gsel = jnp.zeros((N_GROUPS, tm), F32)
    cur = gscore
    for _ in range(TOPK_GROUPS):
        mx = jnp.max(cur, axis=0, keepdims=True)
        fi = jnp.min(jnp.where(cur == mx, itg, N_GROUPS), axis=0, keepdims=True)
        pick = itg == fi
        gsel = jnp.where(pick, 1.0, gsel)
        cur = jnp.where(pick, -jnp.inf, cur)
    emask = jnp.broadcast_to(gsel.reshape(N_GROUPS, 1, tm), (N_GROUPS, per, tm)).reshape(N_EXPERTS, tm) > 0.5

    ite = lax.broadcasted_iota(I32, (N_EXPERTS, tm), 0)
    cur = jnp.where(emask, sel, NEG_INF)
    idx_rows, s_rows = [], []
    for _ in range(TOP_K):
        mx = jnp.max(cur, axis=0, keepdims=True)
        fi = jnp.min(jnp.where(cur == mx, ite, N_EXPERTS), axis=0, keepdims=True)
        pick = ite == fi
        idx_rows.append(fi)
        s_rows.append(jnp.sum(jnp.where(pick, scores, 0.0), axis=0, keepdims=True))
        cur = jnp.where(pick, -jnp.inf, cur)
    top_s = jnp.concatenate(s_rows, axis=0)
    idx_ref[...] = jnp.concatenate(idx_rows, axis=0)
    gate_ref[...] = top_s / jnp.sum(top_s, axis=0, keepdims=True) * ROUTED_SCALE


def _merge(o_a, o_b, x, mod, ga, gb, wo, n2, wr_hi, wr_lo, rb_col, wsg, wsu, wsd, *, tm):
    bsz, s, d = x.shape
    nt = s // tm
    n = bsz * s
    full = lambda a: pl.BlockSpec(a.shape, lambda b, i: (0,) * a.ndim)
    return pl.pallas_call(
        functools.partial(_merge_kernel, d=d),
        out_shape=(jax.ShapeDtypeStruct((bsz, s, d), F32),
                   jax.ShapeDtypeStruct((n, d // 4), U32),
                   jax.ShapeDtypeStruct((n, d // 4), U32),
                   jax.ShapeDtypeStruct((TOP_K, n), I32),
                   jax.ShapeDtypeStruct((TOP_K, n), F32)),
        grid=(bsz, nt),
        in_specs=[pl.BlockSpec((1, tm, d // 2), lambda b, i: (b, i, 0)),
                  pl.BlockSpec((1, tm, d // 2), lambda b, i: (b, i, 0)),
                  pl.BlockSpec((1, tm, d), lambda b, i: (b, i, 0)),
                  full(mod), full(ga), full(gb), full(wo), full(n2), full(wr_hi), full(wr_lo), full(rb_col),
                  full(wsg), full(wsu), full(wsd)],
        out_specs=(pl.BlockSpec((1, tm, d), lambda b, i: (b, i, 0)),
                   pl.BlockSpec((tm, d // 4), lambda b, i: (b * nt + i, 0)),
                   pl.BlockSpec((tm, d // 4), lambda b, i: (b * nt + i, 0)),
                   pl.BlockSpec((TOP_K, tm), lambda b, i: (0, b * nt + i)),
                   pl.BlockSpec((TOP_K, tm), lambda b, i: (0, b * nt + i))),
        compiler_params=_params("arbitrary", "arbitrary"),
        name="merge",
    )(o_a, o_b, x, mod, ga, gb, wo, n2, wr_hi, wr_lo, rb_col, wsg, wsu, wsd)


def _rank_kernel(idx_ref, rank_ref, cnt_ref, carry_ref):
    tm = idx_ref.shape[1]

    @pl.when(pl.program_id(0) == 0)
    def _():
        carry_ref[...] = jnp.zeros_like(carry_ref)

    idx = idx_ref[...]
    ite = lax.broadcasted_iota(I32, (N_EXPERTS, tm), 0)
    before = (lax.broadcasted_iota(I32, (tm, tm), 0) < lax.broadcasted_iota(I32, (tm, tm), 1)).astype(BF16)
    hits = [ite == idx[k:k + 1, :] for k in range(TOP_K)]
    routed = jnp.where(functools.reduce(jnp.logical_or, hits), 1.0, 0.0)
    base = carry_ref[...]
    ahead = jnp.dot(routed.astype(BF16), before, preferred_element_type=F32) + base
    rows = [jnp.sum(jnp.where(hit, ahead, 0.0), axis=0, keepdims=True) for hit in hits]
    rank_ref[...] = jnp.concatenate(rows, axis=0).astype(I32)
    total = base + jnp.sum(routed, axis=1, keepdims=True)
    carry_ref[...] = total
    cnt_ref[...] = total


def _rank(idx, *, tm):
    n = idx.shape[1]
    return pl.pallas_call(
        _rank_kernel,
        out_shape=(jax.ShapeDtypeStruct((TOP_K, n), I32), jax.ShapeDtypeStruct((N_EXPERTS, 1), F32)),
        grid=(n // tm,),
        in_specs=[pl.BlockSpec((TOP_K, tm), lambda i: (0, i))],
        out_specs=(pl.BlockSpec((TOP_K, tm), lambda i: (0, i)), pl.BlockSpec((N_EXPERTS, 1), lambda i: (0, 0))),
        scratch_shapes=[pltpu.VMEM((N_EXPERTS, 1), F32)],
        compiler_params=_params("arbitrary"),
        name="rank",
    )(idx)


def _pos_kernel(idx_ref, rank_ref, pstart_ref, pos_ref):
    tm = idx_ref.shape[1]
    idx = idx_ref[...]
    ite = lax.broadcasted_iota(I32, (N_EXPERTS, tm), 0)
    pstart = pstart_ref[...]
    rows = [jnp.sum(jnp.where(ite == idx[k:k + 1, :], pstart, 0.0), axis=0, keepdims=True) for k in range(TOP_K)]
    pos_ref[...] = jnp.concatenate(rows, axis=0).astype(I32) + rank_ref[...]


def _pos(idx, rank, pstart_col, *, tm):
    n = idx.shape[1]
    return pl.pallas_call(
        _pos_kernel,
        out_shape=jax.ShapeDtypeStruct((TOP_K, n), I32),
        grid=(n // tm,),
        in_specs=[pl.BlockSpec((TOP_K, tm), lambda i: (0, i)),
                  pl.BlockSpec((TOP_K, tm), lambda i: (0, i)),
                  pl.BlockSpec((N_EXPERTS, 1), lambda i: (0, 0))],
        out_specs=pl.BlockSpec((TOP_K, tm), lambda i: (0, i)),
        compiler_params=_params("arbitrary"),
        name="pos",
    )(idx, rank, pstart_col)


UNIT_CHUNKS = (8, 4, 2, 1)
RING_AHEAD = 8
RING_SLOTS = RING_AHEAD + UNIT_CHUNKS[0]


def _expert_kernel(ps_ref, cnt_ref, tot_ref, wg_ref, wu_ref, wd_ref, xa_hbm, xb_hbm, ya_hbm, yb_hbm,
                   xa_buf, xb_buf, ya_buf, yb_buf, in_sem, out_sem):
    e = pl.program_id(0)
    r = EXPERT_ROWS
    ahead = RING_AHEAD
    total = tot_ref[0]
    first = ps_ref[e] // r
    cnt = cnt_ref[e]
    nch = (cnt + r - 1) // r

    def rows_of(g):
        return pl.ds(pl.multiple_of(g * r, r), r)

    def fetch(g):
        slot = g % RING_SLOTS
        return (pltpu.make_async_copy(xa_hbm.at[rows_of(g)], xa_buf.at[slot], in_sem.at[0, slot]),
                pltpu.make_async_copy(xb_hbm.at[rows_of(g)], xb_buf.at[slot], in_sem.at[1, slot]))

    def flush(g):
        slot = g % RING_SLOTS
        return (pltpu.make_async_copy(ya_buf.at[slot], ya_hbm.at[rows_of(g)], out_sem.at[0, slot]),
                pltpu.make_async_copy(yb_buf.at[slot], yb_hbm.at[rows_of(g)], out_sem.at[1, slot]))

    def start(copies):
        for cp in copies:
            cp.start()

    def wait(copies):
        for cp in copies:
            cp.wait()

    @pl.when(e == 0)
    def _():
        for g in range(ahead):
            @pl.when(g < total)
            def _():
                start(fetch(g))

    @pl.when(nch > 0)
    def _():
        def unit(c0, u):
            for j in range(u):
                g = first + c0 + j
                wait(fetch(g))

                @pl.when(g + ahead < total)
                def _():
                    start(fetch(g + ahead))

                @pl.when(g >= RING_SLOTS)
                def _():
                    wait(flush(g - RING_SLOTS))

            slots = [(first + c0 + j) % RING_SLOTS for j in range(u)]
            packed = jnp.concatenate([jnp.concatenate([xa_buf[sl], xb_buf[sl]], axis=1) for sl in slots], axis=0)
            row = lax.broadcasted_iota(I32, packed.shape, 0)
            packed = jnp.where(row < cnt - c0 * r, packed, jnp.zeros_like(packed))
            x = _unpack_bf16_pairs(packed).astype(BF16)
            gate = jnp.dot(x, wg_ref[0].astype(BF16), preferred_element_type=F32)
            up = jnp.dot(x, wu_ref[0].astype(BF16), preferred_element_type=F32)
            y = jnp.dot((_silu(gate) * up).astype(BF16), wd_ref[0].astype(BF16), preferred_element_type=F32)
            out = _pack_bf16_pairs(y)
            half = out.shape[1] // 2
            for j, sl in enumerate(slots):
                ya_buf[sl] = out[j * r:(j + 1) * r, :half]
                yb_buf[sl] = out[j * r:(j + 1) * r, half:]
                start(flush(first + c0 + j))

        big = UNIT_CHUNKS[0]

        def big_unit(i, carry):
            unit(i * big, big)
            return carry

        lax.fori_loop(0, nch // big, big_unit, 0)
        done = nch // big * big
        for u in UNIT_CHUNKS[1:]:
            @pl.when((nch % (2 * u)) >= u)
            def _():
                unit(done, u)

            done = done + jnp.where((nch % (2 * u)) >= u, u, 0)

    @pl.when(e == pl.num_programs(0) - 1)
    def _():
        for k in range(RING_SLOTS):
            @pl.when(total - 1 - k >= 0)
            def _():
                wait(flush(total - 1 - k))


def _experts(pstart, cnt, total_chunks, xs_a, xs_b, wg, wu, wd):
    rows, hw = xs_a.shape
    r = EXPERT_ROWS
    n_exp, d, f = wg.shape
    hbm = pl.BlockSpec(memory_space=pl.ANY)
    grid_spec = pltpu.PrefetchScalarGridSpec(
        num_scalar_prefetch=3,
        grid=(n_exp,),
        in_specs=[pl.BlockSpec((1, d, f), lambda e, ps, cn, tot: (e, 0, 0)),
                  pl.BlockSpec((1, d, f), lambda e, ps, cn, tot: (e, 0, 0)),
                  pl.BlockSpec((1, f, d), lambda e, ps, cn, tot: (e, 0, 0)),
                  hbm, hbm],
        out_specs=(hbm, hbm),
        scratch_shapes=[pltpu.VMEM((RING_SLOTS, r, hw), U32), pltpu.VMEM((RING_SLOTS, r, hw), U32),
                        pltpu.VMEM((RING_SLOTS, r, hw), U32), pltpu.VMEM((RING_SLOTS, r, hw), U32),
                        pltpu.SemaphoreType.DMA((2, RING_SLOTS)), pltpu.SemaphoreType.DMA((2, RING_SLOTS))],
    )
    return pl.pallas_call(
        _expert_kernel,
        out_shape=(jax.ShapeDtypeStruct((rows, hw), U32), jax.ShapeDtypeStruct((rows, hw), U32)),
        grid_spec=grid_spec,
        compiler_params=_params("arbitrary"),
        name="experts",
    )(pstart, cnt, total_chunks, wg, wu, wd, xs_a, xs_b)


def _sc_scatter_rows(x, idx_flat, n_rows):
    n, w = x.shape
    m = idx_flat.shape[1]
    nwin = n // SC_WINDOW
    reps = m // n
    mesh = plsc.VectorSubcoreMesh(core_axis_name="core", subcore_axis_name="subcore")

    @functools.partial(pl.kernel, out_type=jax.ShapeDtypeStruct((n_rows, w), x.dtype), mesh=mesh,
                       scratch_types=[])
    def scatter_kernel(x_hbm, i_hbm, o_hbm):
        def body(x_vmem, i_vmem):
            pltpu.sync_copy(x_vmem, o_hbm.at[i_vmem.at[0]])

        pltpu.emit_pipeline(
            body,
            grid=(nwin, reps),
            in_specs=[pl.BlockSpec((SC_WINDOW, w), lambda i, k: (i, 0)),
                      pl.BlockSpec((1, SC_WINDOW), lambda i, k: (0, k * nwin + i))],
            out_specs=[],
            core_axis_name=("core", "subcore"),
            dimension_semantics=(pltpu.PARALLEL, pltpu.ARBITRARY),
        )(x_hbm, i_hbm)

    return scatter_kernel(x, idx_flat)


def _sc_gather_rows(src, idx_flat):
    m = idx_flat.shape[1]
    w = src.shape[1]
    mesh = plsc.VectorSubcoreMesh(core_axis_name="core", subcore_axis_name="subcore")

    @functools.partial(pl.kernel, out_type=jax.ShapeDtypeStruct((m, w), src.dtype), mesh=mesh)
    def gather_kernel(x_hbm, i_hbm, o_hbm):
        def body(i_vmem, o_vmem):
            pltpu.sync_copy(x_hbm.at[i_vmem.at[0]], o_vmem)

        pltpu.emit_pipeline(
            body,
            grid=(m // SC_WINDOW,),
            in_specs=[pl.BlockSpec((1, SC_WINDOW), lambda i: (0, i))],
            out_specs=[pl.BlockSpec((SC_WINDOW, w), lambda i: (i, 0))],
            core_axis_name=("core", "subcore"),
            dimension_semantics=(pltpu.PARALLEL,),
        )(i_hbm, o_hbm)

    return gather_kernel(src, idx_flat)


def _finish_kernel(gate_ref, xres_ref, mod_ref, ya_ref, yb_ref, o_ref, *, d, tiles_per_batch):
    tm = gate_ref.shape[1]
    b = pl.program_id(0) // tiles_per_batch
    gates = gate_ref[...].T
    acc = jnp.zeros((tm, d), F32)
    for k in range(TOP_K):
        packed = jnp.concatenate([ya_ref[k], yb_ref[k]], axis=1)
        acc = acc + _unpack_bf16_pairs(packed) * gates[:, k:k + 1]
    o_ref[...] = xres_ref[...] + mod_ref[pl.ds(b, 1), 5 * d:6 * d] * acc


def _finish(gates, xres, mod, yg_a, yg_b, *, tm, tiles_per_batch):
    n, d = xres.shape
    return pl.pallas_call(
        functools.partial(_finish_kernel, d=d, tiles_per_batch=tiles_per_batch),
        out_shape=jax.ShapeDtypeStruct((n, d), F32),
        grid=(n // tm,),
        in_specs=[pl.BlockSpec((TOP_K, tm), lambda i: (0, i)),
                  pl.BlockSpec((tm, d), lambda i: (i, 0)),
                  pl.BlockSpec(mod.shape, lambda i: (0, 0)),
                  pl.BlockSpec((TOP_K, tm, d // 4), lambda i: (0, i, 0)),
                  pl.BlockSpec((TOP_K, tm, d // 4), lambda i: (0, i, 0))],
        out_specs=pl.BlockSpec((tm, d), lambda i: (i, 0)),
        compiler_params=_params("arbitrary"),
        name="finish",
    )(gates, xres, mod, yg_a, yg_b)


_QA_HEAD_ORDER = (0, 4, 1, 5, 2, 6, 3, 7)


def _reorder_qa_heads(a, axis):
    return jnp.concatenate([lax.slice_in_dim(a, h * HEAD_DIM, (h + 1) * HEAD_DIM, axis=axis)
                            for h in _QA_HEAD_ORDER], axis=axis)


def _rope_tables(s):
    quarter = HEAD_DIM // 4
    t = jnp.arange(s)
    row = (t // GRID_W).astype(F32)
    col = (t % GRID_W).astype(F32)
    freqs = ROPE_BASE ** (-jnp.arange(quarter, dtype=F32) / quarter)
    ar = row[:, None] * freqs[None, :]
    ac = col[:, None] * freqs[None, :]
    cos = jnp.concatenate([jnp.cos(ar), jnp.cos(ar), jnp.cos(ac), jnp.cos(ac)], axis=1)
    sin = jnp.concatenate([-jnp.sin(ar), jnp.sin(ar), -jnp.sin(ac), jnp.sin(ac)], axis=1)
    return jnp.tile(cos, (1, 2)), jnp.tile(sin, (1, 2))


def kernel(x, c, ctx, c_ctx, w_ada, b_ada, norm1, norm2, w_in, q_norm_a, k_norm_a, q_norm_b, k_norm_b, sink_a,
           rpb_b, out_norm_a, out_norm_b, w_out, w_router, router_bias, we_gate, we_up, we_down, ws_gate, ws_up,
           ws_down):
    assert w_ada.shape[0] == 1, "single-layer block"
    bsz, s, d = x.shape
    lc = ctx.shape[1]
    n = bsz * s
    rows = s // GRID_W
    assert s % (NA_QROWS * GRID_W) == 0 and rows >= NA_KROWS and bsz <= 4 and d == 1024

    c8 = jnp.concatenate([c, c_ctx[None, :], jnp.zeros((8 - bsz - 1, d), F32)], axis=0)
    mod = _ada(c8, w_ada[0], b_ada[0][None, :])

    w = w_in[0]
    aw, akw, bw = A_Q_HEADS * HEAD_DIM, A_KV_HEADS * HEAD_DIM, B_HEADS * HEAD_DIM
    cuts = np.cumsum([0, aw, akw, akw, bw, bw, bw])
    qa_w, ka_w, va_w, qb_w, kb_w, vb_w = [w[:, cuts[i]:cuts[i + 1]] for i in range(6)]
    qa_w = _reorder_qa_heads(qa_w, 1)
    w_ext = jnp.concatenate([qa_w, qb_w, kb_w, vb_w, ka_w, va_w], axis=1).astype(BF16)
    scale = HEAD_DIM ** -0.5 * LOG2E
    pair = lambda g: jnp.tile(g, 2)
    gains = jnp.stack([pair(q_norm_a[0]) * scale, pair(q_norm_b[0]) * scale, pair(k_norm_b[0]), pair(k_norm_a[0])]
                      + [jnp.zeros((LANES,), F32)] * 4)
    bd = jnp.asarray(np.kron(np.eye(2), np.full((HEAD_DIM, HEAD_DIM), 1.0 / HEAD_DIM)), BF16)
    cos_t, sin_t = _rope_tables(s)
    n1 = norm1[0][None, :]
    zq = _inproj(x, mod, n1, w_ext, gains, cos_t, sin_t, bd, mod_row=None, tm=512)
    zc = _inproj(ctx, mod, n1, w_ext, gains, jnp.ones((lc, LANES), F32), jnp.zeros((lc, LANES), F32), bd,
                 mod_row=bsz, tm=lc)

    sink = sink_a[0].astype(F32)
    sink_rep = jnp.concatenate([jnp.broadcast_to(sink[h] * LOG2E, (A_WINDOW, LANES)) for h in _QA_HEAD_ORDER])
    o_a = _attn_a(zq, zc, sink_rep, _attn_a_mask(A_WINDOW))
    o_b = _attn_b(zq, zc, _na_table(rpb_b[0], rows))

    ga = _reorder_qa_heads(out_norm_a[0], 0)[None, :]
    gb = out_norm_b[0][None, :]
    wo = jnp.concatenate([_reorder_qa_heads(w_out[0][:aw], 0), w_out[0][aw:]], axis=0).astype(BF16)
    wr_hi, wr_lo = _split_bf16(w_router[0].T)
    xres, hp_a, hp_b, idx, gates = _merge(o_a, o_b, x, mod, ga, gb, wo, norm2[0][None, :], wr_hi, wr_lo,
                                  router_bias[0][:, None], ws_gate[0].astype(BF16), ws_up[0].astype(BF16),
                                  ws_down[0].astype(BF16), tm=512)

    rank, counts = _rank(idx, tm=512)
    t = EXPERT_ROWS
    cnt = counts[:, 0].astype(I32)
    padded = (cnt + t - 1) // t * t
    pends = jnp.cumsum(padded)
    pstart = pends - padded
    n_rows = n * TOP_K + N_EXPERTS * t
    pos = _pos(idx, rank, pstart.astype(F32)[:, None], tm=min(2048, n))

    pos_flat = pos.reshape(1, TOP_K * n)
    xs_a = _sc_scatter_rows(hp_a, pos_flat, n_rows)
    xs_b = _sc_scatter_rows(hp_b, pos_flat, n_rows)
    ys_a, ys_b = _experts(pstart, cnt, pends[-1:] // t, xs_a, xs_b, we_gate[0], we_up[0], we_down[0])
    yg_a = _sc_gather_rows(ys_a, pos_flat).reshape(TOP_K, n, d // 4)
    yg_b = _sc_gather_rows(ys_b, pos_flat).reshape(TOP_K, n, d // 4)
    out = _finish(gates, xres.reshape(n, d), mod, yg_a, yg_b, tm=512, tiles_per_batch=s // 512)
    return out.reshape(bsz, s, d)
```

```python
import functools

import numpy as np
import jax
import jax.numpy as jnp
from jax import lax
from jax.experimental import pallas as pl
from jax.experimental.pallas import tpu as pltpu
from jax.experimental.pallas import tpu_sc as plsc

F32 = jnp.float32
BF16 = jnp.bfloat16
I32 = jnp.int32
U32 = jnp.uint32

LANES = 128
HEAD_DIM = 64
HEAD_PAIR = 2 * HEAD_DIM
GRID_W = 64
A_Q_HEADS = 8
A_KV_HEADS = 2
A_WINDOW = 128
B_HEADS = 8
NA_KH = 8
NA_KW = 16
NA_QROWS = 4
NA_KROWS = NA_QROWS + NA_KH
ROPE_BASE = 10000.0
N_EXPERTS = 256
TOP_K = 8
N_GROUPS = 8
TOPK_GROUPS = 4
ROUTED_SCALE = 2.5
LOG2E = 1.4426950408889634
EPS = 1e-6
NEG_INF = -1e30
EXPERT_ROWS = 256
SC_WINDOW = 128
VMEM_LIMIT = 56 * 1024 * 1024

_NT = (((1,), (1,)), ((), ()))


def _params(*sem):
    return pltpu.CompilerParams(dimension_semantics=sem, vmem_limit_bytes=VMEM_LIMIT)


def _silu(v):
    return v * jax.nn.sigmoid(v)


def _rms(v, gain):
    return v * lax.rsqrt(jnp.mean(v * v, axis=-1, keepdims=True) + EPS) * gain


def _pack_bf16_pairs(v):
    n = v.shape[1] // 2
    lo = lax.bitcast_convert_type(v[:, :n].astype(BF16).astype(F32), U32) >> 16
    hi = lax.bitcast_convert_type(v[:, n:].astype(BF16).astype(F32), U32) & jnp.uint32(0xFFFF0000)
    return hi | lo


def _split_bf16(v):
    hi = lax.bitcast_convert_type(lax.bitcast_convert_type(v, U32) & jnp.uint32(0xFFFF0000), F32)
    return hi.astype(BF16), (v - hi).astype(BF16)


def _unpack_bf16_pairs(w):
    lo = lax.bitcast_convert_type(w << 16, F32)
    hi = lax.bitcast_convert_type(w & jnp.uint32(0xFFFF0000), F32)
    return jnp.concatenate([lo, hi], axis=1)


def _ada_kernel(c_ref, w_ref, b_ref, o_ref):
    a = _silu(c_ref[...])
    o_ref[...] = jnp.dot(a, w_ref[...], preferred_element_type=F32,
                         precision=lax.Precision.HIGHEST) + b_ref[...]


def _ada(c8, w, b):
    d, n = w.shape
    bn = n // 4
    return pl.pallas_call(
        _ada_kernel,
        out_shape=jax.ShapeDtypeStruct((8, n), F32),
        grid=(n // bn,),
        in_specs=[pl.BlockSpec((8, d), lambda j: (0, 0)),
                  pl.BlockSpec((d, bn), lambda j: (0, j)),
                  pl.BlockSpec((1, bn), lambda j: (0, j))],
        out_specs=pl.BlockSpec((8, bn), lambda j: (0, j)),
        compiler_params=_params("arbitrary"),
        name="ada",
    )(c8, w, b)


_QA, _QB, _KB, _VB, _KA, _VA = 0, 4, 8, 12, 16, 17
_OUT_BLOCKS = 18


def _inproj_kernel(x_ref, mod_ref, n1_ref, w_ref, g_ref, cos_ref, sin_ref, bd_ref, o_ref, *, mod_row, d):
    b = pl.program_id(0) if mod_row is None else mod_row
    xn = _rms(x_ref[0], n1_ref[...])
    sh = mod_ref[pl.ds(b, 1), 0:d]
    sc = mod_ref[pl.ds(b, 1), d:2 * d]
    h = (xn * (1.0 + sc) + sh).astype(BF16)
    z = jnp.dot(h, w_ref[...], preferred_element_type=F32)
    bd = bd_ref[...]
    cos = cos_ref[...]
    sin = sin_ref[...]

    def blk(j):
        return z[:, j * LANES:(j + 1) * LANES]

    def head_rinv(zb):
        ms = jnp.dot((zb * zb).astype(BF16), bd, preferred_element_type=F32)
        return lax.rsqrt(ms + EPS)

    def put(j, v):
        o_ref[0, :, j * LANES:(j + 1) * LANES] = v.astype(BF16)

    quarter = HEAD_DIM // 4
    first_half = (lax.broadcasted_iota(I32, (1, LANES), 1) % (2 * quarter)) < quarter

    def roped(j, g_row):
        zb = blk(j)
        zn = zb * head_rinv(zb) * g_ref[g_row:g_row + 1, :]
        partner = jnp.where(first_half, pltpu.roll(zn, LANES - quarter, 1), pltpu.roll(zn, quarter, 1))
        put(j, zn * cos + partner * sin)

    def normed(j, g_row):
        zb = blk(j)
        put(j, zb * head_rinv(zb) * g_ref[g_row:g_row + 1, :])

    for j in range(4):
        roped(_QA + j, 0)
        normed(_QB + j, 1)
        normed(_KB + j, 2)
        put(_VB + j, blk(_VB + j))
    roped(_KA, 3)
    put(_VA, blk(_VA))


def _inproj(x, mod, n1, w_ext, gains, cos_t, sin_t, bd, *, mod_row, tm):
    bsz, s, d = x.shape
    kern = functools.partial(_inproj_kernel, mod_row=mod_row, d=d)
    return pl.pallas_call(
        kern,
        out_shape=jax.ShapeDtypeStruct((bsz, s, _OUT_BLOCKS * LANES), BF16),
        grid=(bsz, s // tm),
        in_specs=[pl.BlockSpec((1, tm, d), lambda b, i: (b, i, 0)),
                  pl.BlockSpec(mod.shape, lambda b, i: (0, 0)),
                  pl.BlockSpec((1, d), lambda b, i: (0, 0)),
                  pl.BlockSpec(w_ext.shape, lambda b, i: (0, 0)),
                  pl.BlockSpec(gains.shape, lambda b, i: (0, 0)),
                  pl.BlockSpec((tm, LANES), lambda b, i: (i, 0)),
                  pl.BlockSpec((tm, LANES), lambda b, i: (i, 0)),
                  pl.BlockSpec(bd.shape, lambda b, i: (0, 0))],
        out_specs=pl.BlockSpec((1, tm, _OUT_BLOCKS * LANES), lambda b, i: (b, i, 0)),
        compiler_params=_params("arbitrary", "arbitrary"),
        name="inproj",
    )(x, mod, n1, w_ext, gains, cos_t, sin_t, bd)


def _split_pair(qp, lo):
    zero = jnp.zeros_like(qp)
    return jnp.concatenate([jnp.where(lo, qp, zero), jnp.where(lo, zero, qp)], axis=0)


def _softmax_pv(s_parts, v_parts, sink_rep):
    chunks = [s[:, c * LANES:(c + 1) * LANES] for s in s_parts for c in range(s.shape[1] // LANES)]
    m = jnp.max(functools.reduce(jnp.maximum, chunks), axis=-1, keepdims=True)
    m_rep = jnp.broadcast_to(m, (m.shape[0], LANES))
    if sink_rep is not None:
        m_rep = jnp.maximum(m_rep, sink_rep)
    acc = None
    for s, v in zip(s_parts, v_parts):
        p = jnp.concatenate([jnp.exp2(s[:, c * LANES:(c + 1) * LANES] - m_rep)
                             for c in range(s.shape[1] // LANES)], axis=1).astype(BF16)
        v_ext = jnp.concatenate([v, jnp.ones_like(v)], axis=1)
        o = jnp.dot(p, v_ext, preferred_element_type=F32)
        acc = o if acc is None else acc + o
    l_rep = acc[:, LANES:]
    if sink_rep is not None:
        l_rep = l_rep + jnp.exp2(sink_rep - m_rep)
    return acc[:, :LANES] * (1.0 / l_rep)


A_BLOCKS_PER_STEP = 4


def _attn_a_kernel(q_ref, *refs):
    nq = A_BLOCKS_PER_STEP
    k_refs, v_refs = refs[:nq + 2], refs[nq + 2:2 * nq + 4]
    kx_ref, vx_ref, sink_ref = refs[2 * nq + 4:2 * nq + 7]
    mask_refs = refs[2 * nq + 7:3 * nq + 7]
    o_ref = refs[-1]
    tq = A_WINDOW
    lo = lax.broadcasted_iota(I32, (tq, LANES), 1) < HEAD_DIM
    for h in range(nq):
        q = q_ref[0, h * tq:(h + 1) * tq]
        qs = jnp.concatenate([_split_pair(q[:, p * LANES:(p + 1) * LANES], lo) for p in range(4)], axis=0)
        k = jnp.concatenate([r[0] for r in k_refs[h:h + 3]], axis=0)
        v = jnp.concatenate([r[0] for r in v_refs[h:h + 3]], axis=0)
        s_loc = lax.dot_general(qs, k, _NT, preferred_element_type=F32) + mask_refs[h][0]
        s_ctx = lax.dot_general(qs, kx_ref[0], _NT, preferred_element_type=F32)
        o = _softmax_pv([s_loc, s_ctx], [v, vx_ref[0]], sink_ref[...])
        for p in range(4):
            o_lo = o[(2 * p) * tq:(2 * p + 1) * tq]
            o_hi = o[(2 * p + 1) * tq:(2 * p + 2) * tq]
            o_ref[0, h * tq:(h + 1) * tq, p * LANES:(p + 1) * LANES] = jnp.where(lo, o_lo, o_hi).astype(BF16)


def _attn_a(zq, zc, sink_rep, mask):
    bsz, s, _ = zq.shape
    lc = zc.shape[1]
    tq = A_WINDOW
    nq = A_BLOCKS_PER_STEP
    nblk = s // tq
    ka, va = _KA, _VA

    def kv_spec(col, shift):
        return pl.BlockSpec((1, tq, LANES), lambda b, j: (b, jnp.clip(nq * j + shift, 0, nblk - 1), col))

    def mask_spec(h):
        def variant(j):
            blk = nq * j + h
            return jnp.where(blk == 0, 0, jnp.where(blk == nblk - 1, 2, 1))
        return pl.BlockSpec((1,) + mask.shape[1:], lambda b, j: (variant(j), 0, 0))

    shifts = range(-1, nq + 1)
    return pl.pallas_call(
        _attn_a_kernel,
        out_shape=jax.ShapeDtypeStruct((bsz, s, 4 * LANES), BF16),
        grid=(bsz, nblk // nq),
        in_specs=([pl.BlockSpec((1, nq * tq, 4 * LANES), lambda b, j: (b, j, 0))]
                  + [kv_spec(ka, sh) for sh in shifts] + [kv_spec(va, sh) for sh in shifts]
                  + [pl.BlockSpec((1, lc, LANES), lambda b, j: (b, 0, ka)),
                     pl.BlockSpec((1, lc, LANES), lambda b, j: (b, 0, va)),
                     pl.BlockSpec(sink_rep.shape, lambda b, j: (0, 0))]
                  + [mask_spec(h) for h in range(nq)]),
        out_specs=pl.BlockSpec((1, nq * tq, 4 * LANES), lambda b, j: (b, j, 0)),
        compiler_params=_params("arbitrary", "arbitrary"),
        name="attn_a",
    )(*([zq] * (2 * nq + 5) + [zc, zc, sink_rep] + [mask] * nq))


def _attn_a_mask(tq):
    qi = np.arange(tq)[:, None]
    kj = np.arange(3 * tq)[None, :]
    ok = (kj >= qi) & (kj <= qi + 2 * tq)
    variants = [ok & (kj >= tq), ok, ok & (kj < 2 * tq)]
    m = np.stack([np.tile(np.where(v, 0.0, NEG_INF).astype(np.float32), (A_Q_HEADS, 1)) for v in variants])
    return jnp.asarray(m)


NA_GROUPS_PER_STEP = 4


def _attn_b_kernel(q_ref, *refs):
    ngs = NA_GROUPS_PER_STEP
    k_refs, v_refs = refs[:3 * ngs], refs[3 * ngs:6 * ngs]
    kx_ref, vx_ref = refs[6 * ngs:6 * ngs + 2]
    tab_refs = refs[6 * ngs + 2:7 * ngs + 2]
    o_ref = refs[-1]
    tq = NA_QROWS * GRID_W
    lo = lax.broadcasted_iota(I32, (tq, LANES), 1) < HEAD_DIM
    for g in range(ngs):
        rows_g = slice(g * tq, (g + 1) * tq)
        for p in range(4):
            sl = slice(p * LANES, (p + 1) * LANES)
            qs = _split_pair(q_ref[0, rows_g, sl], lo)
            k = jnp.concatenate([r[0, :, sl] for r in k_refs[3 * g:3 * g + 3]], axis=0)
            v = jnp.concatenate([r[0, :, sl] for r in v_refs[3 * g:3 * g + 3]], axis=0)
            s_loc = lax.dot_general(qs, k, _NT, preferred_element_type=F32) + tab_refs[g][0, p]
            s_ctx = lax.dot_general(qs, kx_ref[0, :, sl], _NT, preferred_element_type=F32)
            o = _softmax_pv([s_loc, s_ctx], [v, vx_ref[0, :, sl]], None)
            o_ref[0, rows_g, sl] = jnp.where(lo, o[:tq], o[tq:]).astype(BF16)


def _attn_b(zq, zc, table):
    bsz, s, _ = zq.shape
    lc = zc.shape[1]
    tq = NA_QROWS * GRID_W
    ngs = NA_GROUPS_PER_STEP
    ng = s // tq
    qb, kb, vb = _QB // 4, _KB // 4, _VB // 4

    def kv_spec(col, g, off):
        return pl.BlockSpec((1, tq, 4 * LANES),
                            lambda i, b: (b, jnp.clip(ngs * i + g - 1, 0, ng - 3) + off, col))

    def tab_spec(g):
        def variant(i):
            grp = ngs * i + g
            return jnp.where(grp == 0, 0, jnp.where(grp == ng - 1, 2, 1))
        return pl.BlockSpec((1,) + table.shape[1:], lambda i, b: (variant(i), 0, 0, 0),
                            pipeline_mode=pl.Buffered(1))

    kv_slots = [(g, off) for g in range(ngs) for off in range(3)]
    return pl.pallas_call(
        _attn_b_kernel,
        out_shape=jax.ShapeDtypeStruct((bsz, s, 4 * LANES), BF16),
        grid=(ng // ngs, bsz),
        in_specs=([pl.BlockSpec((1, ngs * tq, 4 * LANES), lambda i, b: (b, i, qb))]
                  + [kv_spec(kb, g, off) for g, off in kv_slots] + [kv_spec(vb, g, off) for g, off in kv_slots]
                  + [pl.BlockSpec((1, lc, 4 * LANES), lambda i, b: (b, 0, kb)),
                     pl.BlockSpec((1, lc, 4 * LANES), lambda i, b: (b, 0, vb))]
                  + [tab_spec(g) for g in range(ngs)]),
        out_specs=pl.BlockSpec((1, ngs * tq, 4 * LANES), lambda i, b: (b, i, 0)),
        compiler_params=_params("arbitrary", "arbitrary"),
        name="attn_b",
    )(*([zq] * (6 * ngs + 1) + [zc, zc] + [table] * ngs))


def _na_table(rpb, rows):
    ng = rows // NA_QROWS
    nq, nk = NA_QROWS * GRID_W, NA_KROWS * GRID_W
    qc = np.arange(GRID_W)
    kc = np.arange(GRID_W)
    ws = np.clip(qc - NA_KW // 2, 0, GRID_W - NA_KW)
    valid_c = (kc[None, :] >= ws[:, None]) & (kc[None, :] < ws[:, None] + NA_KW)
    dc = np.clip(kc[None, :] - qc[:, None], -(NA_KW - 1), NA_KW - 1) + (NA_KW - 1)
    c_sel = (dc[..., None] == np.arange(2 * NA_KW - 1)) & valid_c[..., None]
    tiles = jnp.einsum('hab,uvb->huav', rpb.astype(F32), jnp.asarray(c_sel, F32), precision=lax.Precision.HIGHEST)
    tiles = jnp.where(jnp.asarray(valid_c)[None, :, None, :], tiles * LOG2E, NEG_INF)
    tiles = tiles.reshape(B_HEADS, GRID_W, (2 * NA_KH - 1) * GRID_W)

    def masked(n_key_rows):
        return jnp.full((B_HEADS, GRID_W, n_key_rows * GRID_W), NEG_INF, F32)

    tabs = []
    for i in (0, 1, ng - 1):
        start = int(np.clip(NA_QROWS * i - NA_KH // 2, 0, rows - NA_KROWS))
        q_rows = []
        for qr in range(NA_QROWS):
            r = NA_QROWS * i + qr
            rs = int(np.clip(r - NA_KH // 2, 0, rows - NA_KH))
            y0 = rs - start
            d0 = rs - r + (NA_KH - 1)
            q_rows.append(jnp.concatenate([masked(y0), tiles[:, :, d0 * GRID_W:(d0 + NA_KH) * GRID_W],
                                           masked(NA_KROWS - NA_KH - y0)], axis=-1))
        tabs.append(jnp.concatenate(q_rows, axis=-2))
    return jnp.stack(tabs).reshape(3, B_HEADS // 2, 2 * nq, nk)


def _merge_kernel(oa_ref, ob_ref, x_ref, mod_ref, ga_ref, gb_ref, wo_ref, n2_ref, wrh_ref, wrl_ref, rb_ref,
                  wsg_ref, wsu_ref, wsd_ref, xres_ref, hpa_ref, hpb_ref, idx_ref, gate_ref, *, d):
    b = pl.program_id(0)
    tm = x_ref.shape[1]

    def mod(k):
        return mod_ref[pl.ds(b, 1), k * d:(k + 1) * d]

    na = _rms(oa_ref[0].astype(F32), ga_ref[...])
    nb = _rms(ob_ref[0].astype(F32), gb_ref[...])
    cat = jnp.concatenate([na, nb], axis=1).astype(BF16)
    y = jnp.dot(cat, wo_ref[...], preferred_element_type=F32)
    x1 = x_ref[0] + mod(2) * y
    h2 = _rms(x1, n2_ref[...]) * (1.0 + mod(4)) + mod(3)

    hb = h2.astype(BF16)
    act = _silu(jnp.dot(hb, wsg_ref[...], preferred_element_type=F32)) * jnp.dot(hb, wsu_ref[...],
                                                                                 preferred_element_type=F32)
    shared = jnp.dot(act.astype(BF16), wsd_ref[...], preferred_element_type=F32)
    xres_ref[0] = x1 + mod(5) * shared
    packed = _pack_bf16_pairs(h2)
    hpa_ref[...] = packed[:, :d // 4]
    hpb_ref[...] = packed[:, d // 4:]

    h_hi, h_lo = _split_bf16(h2)
    logits = (lax.dot_general(wrh_ref[...], h_hi, _NT, preferred_element_type=F32)
              + lax.dot_general(wrh_ref[...], h_lo, _NT, preferred_element_type=F32)
              + lax.dot_general(wrl_ref[...], h_hi, _NT, preferred_element_type=F32))
    scores = jax.nn.sigmoid(logits)
    sel = scores + rb_ref[...]
    per = N_EXPERTS // N_GROUPS
    g3 = sel.reshape(N_GROUPS, per, tm)
    it3 = lax.broadcasted_iota(I32, (N_GROUPS, per, tm), 1)
    m1 = jnp.max(g3, axis=1, keepdims=True)
    first = jnp.min(jnp.where(g3 == m1, it3, per), axis=1, keepdims=True)
    m2 = jnp.max(jnp.where(it3 == first, -jnp.inf, g3), axis=1, keepdims=True)
    gscore = (m1 + m2).reshape(N_GROUPS, tm)

    itg = lax.broadcasted_iota(I32, (N_GROUPS, tm), 0)
    gsel = jnp.zeros((N_GROUPS, tm), F32)
    cur = gscore
    for _ in range(TOPK_GROUPS):
        mx = jnp.max(cur, axis=0, keepdims=True)
        fi = jnp.min(jnp.where(cur == mx, itg, N_GROUPS), axis=0, keepdims=True)
        pick = itg == fi
        gsel = jnp.where(pick, 1.0, gsel)
        cur = jnp.where(pick, -jnp.inf, cur)
    emask = jnp.broadcast_to(gsel.reshape(N_GROUPS, 1, tm), (N_GROUPS, per, tm)).reshape(N_EXPERTS, tm) > 0.5

    ite = lax.broadcasted_iota(I32, (N_EXPERTS, tm), 0)
    cur = jnp.where(emask, sel, NEG_INF)
    idx_rows, s_rows = [], []
    for _ in range(TOP_K):
        mx = jnp.max(cur, axis=0, keepdims=True)
        fi = jnp.min(jnp.where(cur == mx, ite, N_EXPERTS), axis=0, keepdims=True)
        pick = ite == fi
        idx_rows.append(fi)
        s_rows.append(jnp.sum(jnp.where(pick, scores, 0.0), axis=0, keepdims=True))
        cur = jnp.where(pick, -jnp.inf, cur)
    top_s = jnp.concatenate(s_rows, axis=0)
    idx_ref[...] = jnp.concatenate(idx_rows, axis=0)
    gate_ref[...] = top_s / jnp.sum(top_s, axis=0, keepdims=True) * ROUTED_SCALE


def _merge(o_a, o_b, x, mod, ga, gb, wo, n2, wr_hi, wr_lo, rb_col, wsg, wsu, wsd, *, tm):
    bsz, s, d = x.shape
    nt = s // tm
    n = bsz * s
    full = lambda a: pl.BlockSpec(a.shape, lambda b, i: (0,) * a.ndim)
    return pl.pallas_call(
        functools.partial(_merge_kernel, d=d),
        out_shape=(jax.ShapeDtypeStruct((bsz, s, d), F32),
                   jax.ShapeDtypeStruct((n, d // 4), U32),
                   jax.ShapeDtypeStruct((n, d // 4), U32),
                   jax.ShapeDtypeStruct((TOP_K, n), I32),
                   jax.ShapeDtypeStruct((TOP_K, n), F32)),
        grid=(bsz, nt),
        in_specs=[pl.BlockSpec((1, tm, d // 2), lambda b, i: (b, i, 0)),
                  pl.BlockSpec((1, tm, d // 2), lambda b, i: (b, i, 0)),
                  pl.BlockSpec((1, tm, d), lambda b, i: (b, i, 0)),
                  full(mod), full(ga), full(gb), full(wo), full(n2), full(wr_hi), full(wr_lo), full(rb_col),
                  full(wsg), full(wsu), full(wsd)],
        out_specs=(pl.BlockSpec((1, tm, d), lambda b, i: (b, i, 0)),
                   pl.BlockSpec((tm, d // 4), lambda b, i: (b * nt + i, 0)),
                   pl.BlockSpec((tm, d // 4), lambda b, i: (b * nt + i, 0)),
                   pl.BlockSpec((TOP_K, tm), lambda b, i: (0, b * nt + i)),
                   pl.BlockSpec((TOP_K, tm), lambda b, i: (0, b * nt + i))),
        compiler_params=_params("arbitrary", "arbitrary"),
        name="merge",
    )(o_a, o_b, x, mod, ga, gb, wo, n2, wr_hi, wr_lo, rb_col, wsg, wsu, wsd)


def _rank_kernel(idx_ref, rank_ref, cnt_ref, carry_ref):
    tm = idx_ref.shape[1]

    @pl.when(pl.program_id(0) == 0)
    def _():
        carry_ref[...] = jnp.zeros_like(carry_ref)

    idx = idx_ref[...]
    ite = lax.broadcasted_iota(I32, (N_EXPERTS, tm), 0)
    before = (lax.broadcasted_iota(I32, (tm, tm), 0) < lax.broadcasted_iota(I32, (tm, tm), 1)).astype(BF16)
    hits = [ite == idx[k:k + 1, :] for k in range(TOP_K)]
    routed = jnp.where(functools.reduce(jnp.logical_or, hits), 1.0, 0.0)
    base = carry_ref[...]
    ahead = jnp.dot(routed.astype(BF16), before, preferred_element_type=F32) + base
    rows = [jnp.sum(jnp.where(hit, ahead, 0.0), axis=0, keepdims=True) for hit in hits]
    rank_ref[...] = jnp.concatenate(rows, axis=0).astype(I32)
    total = base + jnp.sum(routed, axis=1, keepdims=True)
    carry_ref[...] = total
    cnt_ref[...] = total


def _rank(idx, *, tm):
    n = idx.shape[1]
    return pl.pallas_call(
        _rank_kernel,
        out_shape=(jax.ShapeDtypeStruct((TOP_K, n), I32), jax.ShapeDtypeStruct((N_EXPERTS, 1), F32)),
        grid=(n // tm,),
        in_specs=[pl.BlockSpec((TOP_K, tm), lambda i: (0, i))],
        out_specs=(pl.BlockSpec((TOP_K, tm), lambda i: (0, i)), pl.BlockSpec((N_EXPERTS, 1), lambda i: (0, 0))),
        scratch_shapes=[pltpu.VMEM((N_EXPERTS, 1), F32)],
        compiler_params=_params("arbitrary"),
        name="rank",
    )(idx)


def _pos_kernel(idx_ref, rank_ref, pstart_ref, pos_ref):
    tm = idx_ref.shape[1]
    idx = idx_ref[...]
    ite = lax.broadcasted_iota(I32, (N_EXPERTS, tm), 0)
    pstart = pstart_ref[...]
    rows = [jnp.sum(jnp.where(ite == idx[k:k + 1, :], pstart, 0.0), axis=0, keepdims=True) for k in range(TOP_K)]
    pos_ref[...] = jnp.concatenate(rows, axis=0).astype(I32) + rank_ref[...]


def _pos(idx, rank, pstart_col, *, tm):
    n = idx.shape[1]
    return pl.pallas_call(
        _pos_kernel,
        out_shape=jax.ShapeDtypeStruct((TOP_K, n), I32),
        grid=(n // tm,),
        in_specs=[pl.BlockSpec((TOP_K, tm), lambda i: (0, i)),
                  pl.BlockSpec((TOP_K, tm), lambda i: (0, i)),
                  pl.BlockSpec((N_EXPERTS, 1), lambda i: (0, 0))],
        out_specs=pl.BlockSpec((TOP_K, tm), lambda i: (0, i)),
        compiler_params=_params("arbitrary"),
        name="pos",
    )(idx, rank, pstart_col)


UNIT_CHUNKS = (4, 2, 1)
RING_AHEAD = 4
W_SPLIT = 4
RING_SLOTS = RING_AHEAD + UNIT_CHUNKS[0]


def _expert_kernel(ps_ref, cnt_ref, tot_ref, *refs):
    wg_refs, wu_refs, wd_refs = refs[:W_SPLIT], refs[W_SPLIT:2 * W_SPLIT], refs[2 * W_SPLIT:3 * W_SPLIT]
    xa_hbm, xb_hbm, ya_hbm, yb_hbm, xa_buf, xb_buf, ya_buf, yb_buf, in_sem, out_sem = refs[3 * W_SPLIT:]
    e = pl.program_id(0)
    r = EXPERT_ROWS
    ahead = RING_AHEAD
    total = tot_ref[0]
    first = ps_ref[e] // r
    cnt = cnt_ref[e]
    nch = (cnt + r - 1) // r

    def rows_of(g):
        return pl.ds(pl.multiple_of(g * r, r), r)

    def fetch(g):
        slot = g % RING_SLOTS
        return (pltpu.make_async_copy(xa_hbm.at[rows_of(g)], xa_buf.at[slot], in_sem.at[0, slot]),
                pltpu.make_async_copy(xb_hbm.at[rows_of(g)], xb_buf.at[slot], in_sem.at[1, slot]))

    def flush(g):
        slot = g % RING_SLOTS
        return (pltpu.make_async_copy(ya_buf.at[slot], ya_hbm.at[rows_of(g)], out_sem.at[0, slot]),
                pltpu.make_async_copy(yb_buf.at[slot], yb_hbm.at[rows_of(g)], out_sem.at[1, slot]))

    def start(copies):
        for cp in copies:
            cp.start()

    def wait(copies):
        for cp in copies:
            cp.wait()

    @pl.when(e == 0)
    def _():
        for g in range(ahead):
            @pl.when(g < total)
            def _():
                start(fetch(g))

    @pl.when(nch > 0)
    def _():
        def unit(c0, u):
            for j in range(u):
                g = first + c0 + j
                wait(fetch(g))

                @pl.when(g + ahead < total)
                def _():
                    start(fetch(g + ahead))

                @pl.when(g >= RING_SLOTS)
                def _():
                    wait(flush(g - RING_SLOTS))

            slots = [(first + c0 + j) % RING_SLOTS for j in range(u)]
            packed = jnp.concatenate([jnp.concatenate([xa_buf[sl], xb_buf[sl]], axis=1) for sl in slots], axis=0)
            row = lax.broadcasted_iota(I32, packed.shape, 0)
            packed = jnp.where(row < cnt - c0 * r, packed, jnp.zeros_like(packed))
            x = _unpack_bf16_pairs(packed).astype(BF16)
            kw = x.shape[1] // W_SPLIT

            def up_proj(w_refs):
                return sum(jnp.dot(x[:, i * kw:(i + 1) * kw], w_refs[i][0].astype(BF16),
                                   preferred_element_type=F32) for i in range(W_SPLIT))

            act = (_silu(up_proj(wg_refs)) * up_proj(wu_refs)).astype(BF16)
            y = jnp.concatenate([jnp.dot(act, w[0].astype(BF16), preferred_element_type=F32) for w in wd_refs],
                                axis=1)
            out = _pack_bf16_pairs(y)
            half = out.shape[1] // 2
            for j, sl in enumerate(slots):
                ya_buf[sl] = out[j * r:(j + 1) * r, :half]
                yb_buf[sl] = out[j * r:(j + 1) * r, half:]
                start(flush(first + c0 + j))

        big = UNIT_CHUNKS[0]

        def big_unit(i, carry):
            unit(i * big, big)
            return carry

        lax.fori_loop(0, nch // big, big_unit, 0)
        done = nch // big * big
        for u in UNIT_CHUNKS[1:]:
            @pl.when((nch % (2 * u)) >= u)
            def _():
                unit(done, u)

            done = done + jnp.where((nch % (2 * u)) >= u, u, 0)

    @pl.when(e == pl.num_programs(0) - 1)
    def _():
        for k in range(RING_SLOTS):
            @pl.when(total - 1 - k >= 0)
            def _():
                wait(flush(total - 1 - k))


def _experts(pstart, cnt, total_chunks, xs_a, xs_b, wg, wu, wd):
    rows, hw = xs_a.shape
    r = EXPERT_ROWS
    n_exp, d, f = wg.shape
    hbm = pl.BlockSpec(memory_space=pl.ANY)
    grid_spec = pltpu.PrefetchScalarGridSpec(
        num_scalar_prefetch=3,
        grid=(n_exp,),
        in_specs=([pl.BlockSpec((1, d // W_SPLIT, f), lambda e, ps, cn, tot, i=i: (e, i, 0)) for i in range(W_SPLIT)] * 2
                  + [pl.BlockSpec((1, f, d // W_SPLIT), lambda e, ps, cn, tot, i=i: (e, 0, i)) for i in range(W_SPLIT)]
                  + [hbm, hbm]),
        out_specs=(hbm, hbm),
        scratch_shapes=[pltpu.VMEM((RING_SLOTS, r, hw), U32), pltpu.VMEM((RING_SLOTS, r, hw), U32),
                        pltpu.VMEM((RING_SLOTS, r, hw), U32), pltpu.VMEM((RING_SLOTS, r, hw), U32),
                        pltpu.SemaphoreType.DMA((2, RING_SLOTS)), pltpu.SemaphoreType.DMA((2, RING_SLOTS))],
    )
    return pl.pallas_call(
        _expert_kernel,
        out_shape=(jax.ShapeDtypeStruct((rows, hw), U32), jax.ShapeDtypeStruct((rows, hw), U32)),
        grid_spec=grid_spec,
        compiler_params=_params("arbitrary"),
        name="experts",
    )(pstart, cnt, total_chunks, *([wg] * W_SPLIT + [wu] * W_SPLIT + [wd] * W_SPLIT), xs_a, xs_b)


def _sc_scatter_rows(x, idx_flat, n_rows):
    n, w = x.shape
    m = idx_flat.shape[1]
    nwin = n // SC_WINDOW
    reps = m // n
    mesh = plsc.VectorSubcoreMesh(core_axis_name="core", subcore_axis_name="subcore")

    @functools.partial(pl.kernel, out_type=jax.ShapeDtypeStruct((n_rows, w), x.dtype), mesh=mesh,
                       scratch_types=[])
    def scatter_kernel(x_hbm, i_hbm, o_hbm):
        def body(x_vmem, i_vmem):
            pltpu.sync_copy(x_vmem, o_hbm.at[i_vmem.at[0]])

        pltpu.emit_pipeline(
            body,
            grid=(nwin, reps),
            in_specs=[pl.BlockSpec((SC_WINDOW, w), lambda i, k: (i, 0)),
                      pl.BlockSpec((1, SC_WINDOW), lambda i, k: (0, k * nwin + i))],
            out_specs=[],
            core_axis_name=("core", "subcore"),
            dimension_semantics=(pltpu.PARALLEL, pltpu.ARBITRARY),
        )(x_hbm, i_hbm)

    return scatter_kernel(x, idx_flat)


def _sc_gather_rows(src, idx_flat):
    m = idx_flat.shape[1]
    w = src.shape[1]
    mesh = plsc.VectorSubcoreMesh(core_axis_name="core", subcore_axis_name="subcore")

    @functools.partial(pl.kernel, out_type=jax.ShapeDtypeStruct((m, w), src.dtype), mesh=mesh)
    def gather_kernel(x_hbm, i_hbm, o_hbm):
        def body(i_vmem, o_vmem):
            pltpu.sync_copy(x_hbm.at[i_vmem.at[0]], o_vmem)

        pltpu.emit_pipeline(
            body,
            grid=(m // SC_WINDOW,),
            in_specs=[pl.BlockSpec((1, SC_WINDOW), lambda i: (0, i))],
            out_specs=[pl.BlockSpec((SC_WINDOW, w), lambda i: (i, 0))],
            core_axis_name=("core", "subcore"),
            dimension_semantics=(pltpu.PARALLEL,),
        )(i_hbm, o_hbm)

    return gather_kernel(src, idx_flat)


def _finish_kernel(gate_ref, xres_ref, mod_ref, ya_ref, yb_ref, o_ref, *, d, tiles_per_batch):
    tm = gate_ref.shape[1]
    b = pl.program_id(0) // tiles_per_batch
    gates = gate_ref[...].T
    acc = jnp.zeros((tm, d), F32)
    for k in range(TOP_K):
        packed = jnp.concatenate([ya_ref[k], yb_ref[k]], axis=1)
        acc = acc + _unpack_bf16_pairs(packed) * gates[:, k:k + 1]
    o_ref[...] = xres_ref[...] + mod_ref[pl.ds(b, 1), 5 * d:6 * d] * acc


def _finish(gates, xres, mod, yg_a, yg_b, *, tm, tiles_per_batch):
    n, d = xres.shape
    return pl.pallas_call(
        functools.partial(_finish_kernel, d=d, tiles_per_batch=tiles_per_batch),
        out_shape=jax.ShapeDtypeStruct((n, d), F32),
        grid=(n // tm,),
        in_specs=[pl.BlockSpec((TOP_K, tm), lambda i: (0, i)),
                  pl.BlockSpec((tm, d), lambda i: (i, 0)),
                  pl.BlockSpec(mod.shape, lambda i: (0, 0)),
                  pl.BlockSpec((TOP_K, tm, d // 4), lambda i: (0, i, 0)),
                  pl.BlockSpec((TOP_K, tm, d // 4), lambda i: (0, i, 0))],
        out_specs=pl.BlockSpec((tm, d), lambda i: (i, 0)),
        compiler_params=_params("arbitrary"),
        name="finish",
    )(gates, xres, mod, yg_a, yg_b)


_QA_HEAD_ORDER = (0, 4, 1, 5, 2, 6, 3, 7)


def _reorder_qa_heads(a, axis):
    return jnp.concatenate([lax.slice_in_dim(a, h * HEAD_DIM, (h + 1) * HEAD_DIM, axis=axis)
                            for h in _QA_HEAD_ORDER], axis=axis)


def _rope_tables(s):
    quarter = HEAD_DIM // 4
    t = jnp.arange(s)
    row = (t // GRID_W).astype(F32)
    col = (t % GRID_W).astype(F32)
    freqs = ROPE_BASE ** (-jnp.arange(quarter, dtype=F32) / quarter)
    ar = row[:, None] * freqs[None, :]
    ac = col[:, None] * freqs[None, :]
    cos = jnp.concatenate([jnp.cos(ar), jnp.cos(ar), jnp.cos(ac), jnp.cos(ac)], axis=1)
    sin = jnp.concatenate([-jnp.sin(ar), jnp.sin(ar), -jnp.sin(ac), jnp.sin(ac)], axis=1)
    return jnp.tile(cos, (1, 2)), jnp.tile(sin, (1, 2))


def kernel(x, c, ctx, c_ctx, w_ada, b_ada, norm1, norm2, w_in, q_norm_a, k_norm_a, q_norm_b, k_norm_b, sink_a,
           rpb_b, out_norm_a, out_norm_b, w_out, w_router, router_bias, we_gate, we_up, we_down, ws_gate, ws_up,
           ws_down):
    assert w_ada.shape[0] == 1, "single-layer block"
    bsz, s, d = x.shape
    lc = ctx.shape[1]
    n = bsz * s
    rows = s // GRID_W
    assert s % (NA_QROWS * GRID_W) == 0 and rows >= NA_KROWS and bsz <= 4 and d == 1024

    c8 = jnp.concatenate([c, c_ctx[None, :], jnp.zeros((8 - bsz - 1, d), F32)], axis=0)
    mod = _ada(c8, w_ada[0], b_ada[0][None, :])

    w = w_in[0]
    aw, akw, bw = A_Q_HEADS * HEAD_DIM, A_KV_HEADS * HEAD_DIM, B_HEADS * HEAD_DIM
    cuts = np.cumsum([0, aw, akw, akw, bw, bw, bw])
    qa_w, ka_w, va_w, qb_w, kb_w, vb_w = [w[:, cuts[i]:cuts[i + 1]] for i in range(6)]
    qa_w = _reorder_qa_heads(qa_w, 1)
    w_ext = jnp.concatenate([qa_w, qb_w, kb_w, vb_w, ka_w, va_w], axis=1).astype(BF16)
    scale = HEAD_DIM ** -0.5 * LOG2E
    pair = lambda g: jnp.tile(g, 2)
    gains = jnp.stack([pair(q_norm_a[0]) * scale, pair(q_norm_b[0]) * scale, pair(k_norm_b[0]), pair(k_norm_a[0])]
                      + [jnp.zeros((LANES,), F32)] * 4)
    bd = jnp.asarray(np.kron(np.eye(2), np.full((HEAD_DIM, HEAD_DIM), 1.0 / HEAD_DIM)), BF16)
    cos_t, sin_t = _rope_tables(s)
    n1 = norm1[0][None, :]
    zq = _inproj(x, mod, n1, w_ext, gains, cos_t, sin_t, bd, mod_row=None, tm=512)
    zc = _inproj(ctx, mod, n1, w_ext, gains, jnp.ones((lc, LANES), F32), jnp.zeros((lc, LANES), F32), bd,
                 mod_row=bsz, tm=lc)

    sink = sink_a[0].astype(F32)
    sink_rep = jnp.concatenate([jnp.broadcast_to(sink[h] * LOG2E, (A_WINDOW, LANES)) for h in _QA_HEAD_ORDER])
    o_a = _attn_a(zq, zc, sink_rep, _attn_a_mask(A_WINDOW))
    o_b = _attn_b(zq, zc, _na_table(rpb_b[0], rows))

    ga = _reorder_qa_heads(out_norm_a[0], 0)[None, :]
    gb = out_norm_b[0][None, :]
    wo = jnp.concatenate([_reorder_qa_heads(w_out[0][:aw], 0), w_out[0][aw:]], axis=0).astype(BF16)
    wr_hi, wr_lo = _split_bf16(w_router[0].T)
    xres, hp_a, hp_b, idx, gates = _merge(o_a, o_b, x, mod, ga, gb, wo, norm2[0][None, :], wr_hi, wr_lo,
                                  router_bias[0][:, None], ws_gate[0].astype(BF16), ws_up[0].astype(BF16),
                                  ws_down[0].astype(BF16), tm=512)

    rank, counts = _rank(idx, tm=512)
    t = EXPERT_ROWS
    cnt = counts[:, 0].astype(I32)
    padded = (cnt + t - 1) // t * t
    pends = jnp.cumsum(padded)
    pstart = pends - padded
    n_rows = n * TOP_K + N_EXPERTS * t
    pos = _pos(idx, rank, pstart.astype(F32)[:, None], tm=min(2048, n))

    pos_flat = pos.reshape(1, TOP_K * n)
    xs_a = _sc_scatter_rows(hp_a, pos_flat, n_rows)
    xs_b = _sc_scatter_rows(hp_b, pos_flat, n_rows)
    ys_a, ys_b = _experts(pstart, cnt, pends[-1:] // t, xs_a, xs_b, we_gate[0], we_up[0], we_down[0])
    yg_a = _sc_gather_rows(ys_a, pos_flat).reshape(TOP_K, n, d // 4)
    yg_b = _sc_gather_rows(ys_b, pos_flat).reshape(TOP_K, n, d // 4)
    out = _finish(gates, xres.reshape(n, d), mod, yg_a, yg_b, tm=512, tiles_per_batch=s // 512)
    return out.reshape(bsz, s, d)
```

```python
import functools

import numpy as np
import jax
import jax.numpy as jnp
from jax import lax
from jax.experimental import pallas as pl
from jax.experimental.pallas import tpu as pltpu
from jax.experimental.pallas import tpu_sc as plsc

F32 = jnp.float32
BF16 = jnp.bfloat16
I32 = jnp.int32
U32 = jnp.uint32

LANES = 128
HEAD_DIM = 64
HEAD_PAIR = 2 * HEAD_DIM
GRID_W = 64
A_Q_HEADS = 8
A_KV_HEADS = 2
A_WINDOW = 128
B_HEADS = 8
NA_KH = 8
NA_KW = 16
NA_QROWS = 4
NA_KROWS = NA_QROWS + NA_KH
ROPE_BASE = 10000.0
N_EXPERTS = 256
TOP_K = 8
N_GROUPS = 8
TOPK_GROUPS = 4
ROUTED_SCALE = 2.5
LOG2E = 1.4426950408889634
EPS = 1e-6
NEG_INF = -1e30
TOKEN_TILE = 512
POS_TILE = 2048
EXPERT_ROWS = 256
SC_WINDOW = 128
VMEM_LIMIT = 56 * 1024 * 1024

_NT = (((1,), (1,)), ((), ()))


def _params(*sem):
    return pltpu.CompilerParams(dimension_semantics=sem, vmem_limit_bytes=VMEM_LIMIT)


def _silu(v):
    return v * jax.nn.sigmoid(v)


def _rms(v, gain):
    return v * lax.rsqrt(jnp.mean(v * v, axis=-1, keepdims=True) + EPS) * gain


def _pack_bf16_pairs(v):
    n = v.shape[1] // 2
    lo = lax.bitcast_convert_type(v[:, :n].astype(BF16).astype(F32), U32) >> 16
    hi = lax.bitcast_convert_type(v[:, n:].astype(BF16).astype(F32), U32) & jnp.uint32(0xFFFF0000)
    return hi | lo


def _split_bf16(v):
    hi = lax.bitcast_convert_type(lax.bitcast_convert_type(v, U32) & jnp.uint32(0xFFFF0000), F32)
    return hi.astype(BF16), (v - hi).astype(BF16)


def _unpack_bf16_pairs(w):
    lo = lax.bitcast_convert_type(w << 16, F32)
    hi = lax.bitcast_convert_type(w & jnp.uint32(0xFFFF0000), F32)
    return jnp.concatenate([lo, hi], axis=1)


def _ada_kernel(c_ref, w_ref, b_ref, o_ref):
    a = _silu(c_ref[...])
    o_ref[...] = jnp.dot(a, w_ref[...], preferred_element_type=F32,
                         precision=lax.Precision.HIGHEST) + b_ref[...]


def _ada(c8, w, b):
    d, n = w.shape
    bn = n // 4
    return pl.pallas_call(
        _ada_kernel,
        out_shape=jax.ShapeDtypeStruct((8, n), F32),
        grid=(n // bn,),
        in_specs=[pl.BlockSpec((8, d), lambda j: (0, 0)),
                  pl.BlockSpec((d, bn), lambda j: (0, j)),
                  pl.BlockSpec((1, bn), lambda j: (0, j))],
        out_specs=pl.BlockSpec((8, bn), lambda j: (0, j)),
        compiler_params=_params("arbitrary"),
        name="ada",
    )(c8, w, b)


_QA, _QB, _KB, _VB, _KA, _VA = 0, 4, 8, 12, 16, 17
_OUT_BLOCKS = 18


def _inproj_kernel(x_ref, mod_ref, n1_ref, w_ref, g_ref, cos_ref, sin_ref, bd_ref, o_ref, *, mod_row, d):
    b = pl.program_id(0) if mod_row is None else mod_row
    xn = _rms(x_ref[0], n1_ref[...])
    sh = mod_ref[pl.ds(b, 1), 0:d]
    sc = mod_ref[pl.ds(b, 1), d:2 * d]
    h = (xn * (1.0 + sc) + sh).astype(BF16)
    z = jnp.dot(h, w_ref[...], preferred_element_type=F32)
    bd = bd_ref[...]
    cos = cos_ref[...]
    sin = sin_ref[...]

    def blk(j):
        return z[:, j * LANES:(j + 1) * LANES]

    def head_rinv(zb):
        ms = jnp.dot((zb * zb).astype(BF16), bd, preferred_element_type=F32)
        return lax.rsqrt(ms + EPS)

    def put(j, v):
        o_ref[0, :, j * LANES:(j + 1) * LANES] = v.astype(BF16)

    quarter = HEAD_DIM // 4
    first_half = (lax.broadcasted_iota(I32, (1, LANES), 1) % (2 * quarter)) < quarter

    def roped(j, g_row):
        zb = blk(j)
        zn = zb * head_rinv(zb) * g_ref[g_row:g_row + 1, :]
        partner = jnp.where(first_half, pltpu.roll(zn, LANES - quarter, 1), pltpu.roll(zn, quarter, 1))
        put(j, zn * cos + partner * sin)

    def normed(j, g_row):
        zb = blk(j)
        put(j, zb * head_rinv(zb) * g_ref[g_row:g_row + 1, :])

    for j in range(4):
        roped(_QA + j, 0)
        normed(_QB + j, 1)
        normed(_KB + j, 2)
        put(_VB + j, blk(_VB + j))
    roped(_KA, 3)
    put(_VA, blk(_VA))


def _inproj(x, mod, n1, w_ext, gains, cos_t, sin_t, bd, *, mod_row, tm):
    bsz, s, d = x.shape
    kern = functools.partial(_inproj_kernel, mod_row=mod_row, d=d)
    return pl.pallas_call(
        kern,
        out_shape=jax.ShapeDtypeStruct((bsz, s, _OUT_BLOCKS * LANES), BF16),
        grid=(bsz, s // tm),
        in_specs=[pl.BlockSpec((1, tm, d), lambda b, i: (b, i, 0)),
                  pl.BlockSpec(mod.shape, lambda b, i: (0, 0)),
                  pl.BlockSpec((1, d), lambda b, i: (0, 0)),
                  pl.BlockSpec(w_ext.shape, lambda b, i: (0, 0)),
                  pl.BlockSpec(gains.shape, lambda b, i: (0, 0)),
                  pl.BlockSpec((tm, LANES), lambda b, i: (i, 0)),
                  pl.BlockSpec((tm, LANES), lambda b, i: (i, 0)),
                  pl.BlockSpec(bd.shape, lambda b, i: (0, 0))],
        out_specs=pl.BlockSpec((1, tm, _OUT_BLOCKS * LANES), lambda b, i: (b, i, 0)),
        compiler_params=_params("arbitrary", "arbitrary"),
        name="inproj",
    )(x, mod, n1, w_ext, gains, cos_t, sin_t, bd)


def _split_pair(qp, lo):
    zero = jnp.zeros_like(qp)
    return jnp.concatenate([jnp.where(lo, qp, zero), jnp.where(lo, zero, qp)], axis=0)


def _softmax_pv(s_parts, v_parts, sink_rep):
    chunks = [s[:, c * LANES:(c + 1) * LANES] for s in s_parts for c in range(s.shape[1] // LANES)]
    m = jnp.max(functools.reduce(jnp.maximum, chunks), axis=-1, keepdims=True)
    m_rep = jnp.broadcast_to(m, (m.shape[0], LANES))
    if sink_rep is not None:
        m_rep = jnp.maximum(m_rep, sink_rep)
    acc = None
    for s, v in zip(s_parts, v_parts):
        p = jnp.concatenate([jnp.exp2(s[:, c * LANES:(c + 1) * LANES] - m_rep)
                             for c in range(s.shape[1] // LANES)], axis=1).astype(BF16)
        v_ext = jnp.concatenate([v, jnp.ones_like(v)], axis=1)
        o = jnp.dot(p, v_ext, preferred_element_type=F32)
        acc = o if acc is None else acc + o
    l_rep = acc[:, LANES:]
    if sink_rep is not None:
        l_rep = l_rep + jnp.exp2(sink_rep - m_rep)
    return acc[:, :LANES] * (1.0 / l_rep)


A_BLOCKS_PER_STEP = 4


def _attn_a_kernel(q_ref, *refs):
    nq = A_BLOCKS_PER_STEP
    k_refs, v_refs = refs[:nq + 2], refs[nq + 2:2 * nq + 4]
    kx_ref, vx_ref, sink_ref = refs[2 * nq + 4:2 * nq + 7]
    mask_refs = refs[2 * nq + 7:3 * nq + 7]
    o_ref = refs[-1]
    tq = A_WINDOW
    lo = lax.broadcasted_iota(I32, (tq, LANES), 1) < HEAD_DIM
    for h in range(nq):
        q = q_ref[0, h * tq:(h + 1) * tq]
        qs = jnp.concatenate([_split_pair(q[:, p * LANES:(p + 1) * LANES], lo) for p in range(4)], axis=0)
        k = jnp.concatenate([r[0] for r in k_refs[h:h + 3]], axis=0)
        v = jnp.concatenate([r[0] for r in v_refs[h:h + 3]], axis=0)
        s_loc = lax.dot_general(qs, k, _NT, preferred_element_type=F32) + mask_refs[h][0]
        s_ctx = lax.dot_general(qs, kx_ref[0], _NT, preferred_element_type=F32)
        o = _softmax_pv([s_loc, s_ctx], [v, vx_ref[0]], sink_ref[...])
        for p in range(4):
            o_lo = o[(2 * p) * tq:(2 * p + 1) * tq]
            o_hi = o[(2 * p + 1) * tq:(2 * p + 2) * tq]
            o_ref[0, h * tq:(h + 1) * tq, p * LANES:(p + 1) * LANES] = jnp.where(lo, o_lo, o_hi).astype(BF16)


def _attn_a(zq, zc, sink_rep, mask):
    bsz, s, _ = zq.shape
    lc = zc.shape[1]
    tq = A_WINDOW
    nq = A_BLOCKS_PER_STEP
    nblk = s // tq
    ka, va = _KA, _VA

    def kv_spec(col, shift):
        return pl.BlockSpec((1, tq, LANES), lambda b, j: (b, jnp.clip(nq * j + shift, 0, nblk - 1), col))

    def mask_spec(h):
        def variant(j):
            blk = nq * j + h
            return jnp.where(blk == 0, 0, jnp.where(blk == nblk - 1, 2, 1))
        return pl.BlockSpec((1,) + mask.shape[1:], lambda b, j: (variant(j), 0, 0))

    shifts = range(-1, nq + 1)
    return pl.pallas_call(
        _attn_a_kernel,
        out_shape=jax.ShapeDtypeStruct((bsz, s, 4 * LANES), BF16),
        grid=(bsz, nblk // nq),
        in_specs=([pl.BlockSpec((1, nq * tq, 4 * LANES), lambda b, j: (b, j, 0))]
                  + [kv_spec(ka, sh) for sh in shifts] + [kv_spec(va, sh) for sh in shifts]
                  + [pl.BlockSpec((1, lc, LANES), lambda b, j: (b, 0, ka)),
                     pl.BlockSpec((1, lc, LANES), lambda b, j: (b, 0, va)),
                     pl.BlockSpec(sink_rep.shape, lambda b, j: (0, 0))]
                  + [mask_spec(h) for h in range(nq)]),
        out_specs=pl.BlockSpec((1, nq * tq, 4 * LANES), lambda b, j: (b, j, 0)),
        compiler_params=_params("arbitrary", "arbitrary"),
        name="attn_a",
    )(*([zq] * (2 * nq + 5) + [zc, zc, sink_rep] + [mask] * nq))


def _attn_a_mask(tq):
    qi = np.arange(tq)[:, None]
    kj = np.arange(3 * tq)[None, :]
    ok = (kj >= qi) & (kj <= qi + 2 * tq)
    variants = [ok & (kj >= tq), ok, ok & (kj < 2 * tq)]
    m = np.stack([np.tile(np.where(v, 0.0, NEG_INF).astype(np.float32), (A_Q_HEADS, 1)) for v in variants])
    return jnp.asarray(m)


NA_GROUPS_PER_STEP = 4


def _attn_b_kernel(q_ref, *refs):
    ngs = NA_GROUPS_PER_STEP
    k_refs, v_refs = refs[:3 * ngs], refs[3 * ngs:6 * ngs]
    kx_ref, vx_ref = refs[6 * ngs:6 * ngs + 2]
    tab_refs = refs[6 * ngs + 2:7 * ngs + 2]
    o_ref = refs[-1]
    tq = NA_QROWS * GRID_W
    lo = lax.broadcasted_iota(I32, (tq, LANES), 1) < HEAD_DIM
    for g in range(ngs):
        rows_g = slice(g * tq, (g + 1) * tq)
        for p in range(4):
            sl = slice(p * LANES, (p + 1) * LANES)
            qs = _split_pair(q_ref[0, rows_g, sl], lo)
            k = jnp.concatenate([r[0, :, sl] for r in k_refs[3 * g:3 * g + 3]], axis=0)
            v = jnp.concatenate([r[0, :, sl] for r in v_refs[3 * g:3 * g + 3]], axis=0)
            s_loc = lax.dot_general(qs, k, _NT, preferred_element_type=F32) + tab_refs[g][0, p]
            s_ctx = lax.dot_general(qs, kx_ref[0, :, sl], _NT, preferred_element_type=F32)
            o = _softmax_pv([s_loc, s_ctx], [v, vx_ref[0, :, sl]], None)
            o_ref[0, rows_g, sl] = jnp.where(lo, o[:tq], o[tq:]).astype(BF16)


def _attn_b(zq, zc, table):
    bsz, s, _ = zq.shape
    lc = zc.shape[1]
    tq = NA_QROWS * GRID_W
    ngs = NA_GROUPS_PER_STEP
    ng = s // tq
    qb, kb, vb = _QB // 4, _KB // 4, _VB // 4

    def kv_spec(col, g, off):
        return pl.BlockSpec((1, tq, 4 * LANES),
                            lambda i, b: (b, jnp.clip(ngs * i + g - 1, 0, ng - 3) + off, col))

    def tab_spec(g):
        def variant(i):
            grp = ngs * i + g
            return jnp.where(grp == 0, 0, jnp.where(grp == ng - 1, 2, 1))
        return pl.BlockSpec((1,) + table.shape[1:], lambda i, b: (variant(i), 0, 0, 0),
                            pipeline_mode=pl.Buffered(1))

    kv_slots = [(g, off) for g in range(ngs) for off in range(3)]
    return pl.pallas_call(
        _attn_b_kernel,
        out_shape=jax.ShapeDtypeStruct((bsz, s, 4 * LANES), BF16),
        grid=(ng // ngs, bsz),
        in_specs=([pl.BlockSpec((1, ngs * tq, 4 * LANES), lambda i, b: (b, i, qb))]
                  + [kv_spec(kb, g, off) for g, off in kv_slots] + [kv_spec(vb, g, off) for g, off in kv_slots]
                  + [pl.BlockSpec((1, lc, 4 * LANES), lambda i, b: (b, 0, kb)),
                     pl.BlockSpec((1, lc, 4 * LANES), lambda i, b: (b, 0, vb))]
                  + [tab_spec(g) for g in range(ngs)]),
        out_specs=pl.BlockSpec((1, ngs * tq, 4 * LANES), lambda i, b: (b, i, 0)),
        compiler_params=_params("arbitrary", "arbitrary"),
        name="attn_b",
    )(*([zq] * (6 * ngs + 1) + [zc, zc] + [table] * ngs))


def _na_table(rpb, rows):
    ng = rows // NA_QROWS
    nq, nk = NA_QROWS * GRID_W, NA_KROWS * GRID_W
    qc = np.arange(GRID_W)
    kc = np.arange(GRID_W)
    ws = np.clip(qc - NA_KW // 2, 0, GRID_W - NA_KW)
    valid_c = (kc[None, :] >= ws[:, None]) & (kc[None, :] < ws[:, None] + NA_KW)
    dc = np.clip(kc[None, :] - qc[:, None], -(NA_KW - 1), NA_KW - 1) + (NA_KW - 1)
    c_sel = (dc[..., None] == np.arange(2 * NA_KW - 1)) & valid_c[..., None]
    tiles = jnp.einsum('hab,uvb->huav', rpb.astype(F32), jnp.asarray(c_sel, F32), precision=lax.Precision.HIGHEST)
    tiles = jnp.where(jnp.asarray(valid_c)[None, :, None, :], tiles * LOG2E, NEG_INF)
    tiles = tiles.reshape(B_HEADS, GRID_W, (2 * NA_KH - 1) * GRID_W)

    def masked(n_key_rows):
        return jnp.full((B_HEADS, GRID_W, n_key_rows * GRID_W), NEG_INF, F32)

    tabs = []
    for i in (0, 1, ng - 1):
        start = int(np.clip(NA_QROWS * i - NA_KH // 2, 0, rows - NA_KROWS))
        q_rows = []
        for qr in range(NA_QROWS):
            r = NA_QROWS * i + qr
            rs = int(np.clip(r - NA_KH // 2, 0, rows - NA_KH))
            y0 = rs - start
            d0 = rs - r + (NA_KH - 1)
            q_rows.append(jnp.concatenate([masked(y0), tiles[:, :, d0 * GRID_W:(d0 + NA_KH) * GRID_W],
                                           masked(NA_KROWS - NA_KH - y0)], axis=-1))
        tabs.append(jnp.concatenate(q_rows, axis=-2))
    return jnp.stack(tabs).reshape(3, B_HEADS // 2, 2 * nq, nk)


def _merge_kernel(oa_ref, ob_ref, x_ref, mod_ref, ga_ref, gb_ref, wo_ref, n2_ref, wrh_ref, wrl_ref, rb_ref,
                  wsg_ref, wsu_ref, wsd_ref, xres_ref, hpa_ref, hpb_ref, idx_ref, gate_ref, *, d):
    b = pl.program_id(0)
    tm = x_ref.shape[1]

    def mod(k):
        return mod_ref[pl.ds(b, 1), k * d:(k + 1) * d]

    na = _rms(oa_ref[0].astype(F32), ga_ref[...])
    nb = _rms(ob_ref[0].astype(F32), gb_ref[...])
    cat = jnp.concatenate([na, nb], axis=1).astype(BF16)
    y = jnp.dot(cat, wo_ref[...], preferred_element_type=F32)
    x1 = x_ref[0] + mod(2) * y
    h2 = _rms(x1, n2_ref[...]) * (1.0 + mod(4)) + mod(3)

    hb = h2.astype(BF16)
    act = _silu(jnp.dot(hb, wsg_ref[...], preferred_element_type=F32)) * jnp.dot(hb, wsu_ref[...],
                                                                                 preferred_element_type=F32)
    shared = jnp.dot(act.astype(BF16), wsd_ref[...], preferred_element_type=F32)
    xres_ref[0] = x1 + mod(5) * shared
    packed = _pack_bf16_pairs(h2)
    hpa_ref[...] = packed[:, :d // 4]
    hpb_ref[...] = packed[:, d // 4:]

    h_hi, h_lo = _split_bf16(h2)
    logits = (lax.dot_general(wrh_ref[...], h_hi, _NT, preferred_element_type=F32)
              + lax.dot_general(wrh_ref[...], h_lo, _NT, preferred_element_type=F32)
              + lax.dot_general(wrl_ref[...], h_hi, _NT, preferred_element_type=F32))
    scores = jax.nn.sigmoid(logits)
    sel = scores + rb_ref[...]
    per = N_EXPERTS // N_GROUPS
    g3 = sel.reshape(N_GROUPS, per, tm)
    it3 = lax.broadcasted_iota(I32, (N_GROUPS, per, tm), 1)
    m1 = jnp.max(g3, axis=1, keepdims=True)
    first = jnp.min(jnp.where(g3 == m1, it3, per), axis=1, keepdims=True)
    m2 = jnp.max(jnp.where(it3 == first, -jnp.inf, g3), axis=1, keepdims=True)
    gscore = (m1 + m2).reshape(N_GROUPS, tm)

    itg = lax.broadcasted_iota(I32, (N_GROUPS, tm), 0)
    gsel = jnp.zeros((N_GROUPS, tm), F32)
    cur = gscore
    for _ in range(TOPK_GROUPS):
        mx = jnp.max(cur, axis=0, keepdims=True)
        fi = jnp.min(jnp.where(cur == mx, itg, N_GROUPS), axis=0, keepdims=True)
        pick = itg == fi
        gsel = jnp.where(pick, 1.0, gsel)
        cur = jnp.where(pick, -jnp.inf, cur)
    emask = jnp.broadcast_to(gsel.reshape(N_GROUPS, 1, tm), (N_GROUPS, per, tm)).reshape(N_EXPERTS, tm) > 0.5

    ite = lax.broadcasted_iota(I32, (N_EXPERTS, tm), 0)
    cur = jnp.where(emask, sel, NEG_INF)
    idx_rows, s_rows = [], []
    for _ in range(TOP_K):
        mx = jnp.max(cur, axis=0, keepdims=True)
        fi = jnp.min(jnp.where(cur == mx, ite, N_EXPERTS), axis=0, keepdims=True)
        pick = ite == fi
        idx_rows.append(fi)
        s_rows.append(jnp.sum(jnp.where(pick, scores, 0.0), axis=0, keepdims=True))
        cur = jnp.where(pick, -jnp.inf, cur)
    top_s = jnp.concatenate(s_rows, axis=0)
    idx_ref[...] = jnp.concatenate(idx_rows, axis=0)
    gate_ref[...] = top_s / jnp.sum(top_s, axis=0, keepdims=True) * ROUTED_SCALE


def _merge(o_a, o_b, x, mod, ga, gb, wo, n2, wr_hi, wr_lo, rb_col, wsg, wsu, wsd, *, tm):
    bsz, s, d = x.shape
    nt = s // tm
    n = bsz * s
    full = lambda a: pl.BlockSpec(a.shape, lambda b, i: (0,) * a.ndim)
    return pl.pallas_call(
        functools.partial(_merge_kernel, d=d),
        out_shape=(jax.ShapeDtypeStruct((bsz, s, d), F32),
                   jax.ShapeDtypeStruct((n, d // 4), U32),
                   jax.ShapeDtypeStruct((n, d // 4), U32),
                   jax.ShapeDtypeStruct((TOP_K, n), I32),
                   jax.ShapeDtypeStruct((TOP_K, n), F32)),
        grid=(bsz, nt),
        in_specs=[pl.BlockSpec((1, tm, d // 2), lambda b, i: (b, i, 0)),
                  pl.BlockSpec((1, tm, d // 2), lambda b, i: (b, i, 0)),
                  pl.BlockSpec((1, tm, d), lambda b, i: (b, i, 0)),
                  full(mod), full(ga), full(gb), full(wo), full(n2), full(wr_hi), full(wr_lo), full(rb_col),
                  full(wsg), full(wsu), full(wsd)],
        out_specs=(pl.BlockSpec((1, tm, d), lambda b, i: (b, i, 0)),
                   pl.BlockSpec((tm, d // 4), lambda b, i: (b * nt + i, 0)),
                   pl.BlockSpec((tm, d // 4), lambda b, i: (b * nt + i, 0)),
                   pl.BlockSpec((TOP_K, tm), lambda b, i: (0, b * nt + i)),
                   pl.BlockSpec((TOP_K, tm), lambda b, i: (0, b * nt + i))),
        compiler_params=_params("arbitrary", "arbitrary"),
        name="merge",
    )(o_a, o_b, x, mod, ga, gb, wo, n2, wr_hi, wr_lo, rb_col, wsg, wsu, wsd)


def _rank_kernel(idx_ref, rank_ref, cnt_ref, carry_ref):
    tm = idx_ref.shape[1]

    @pl.when(pl.program_id(0) == 0)
    def _():
        carry_ref[...] = jnp.zeros_like(carry_ref)

    idx = idx_ref[...]
    ite = lax.broadcasted_iota(I32, (N_EXPERTS, tm), 0)
    before = (lax.broadcasted_iota(I32, (tm, tm), 0) < lax.broadcasted_iota(I32, (tm, tm), 1)).astype(BF16)
    hits = [ite == idx[k:k + 1, :] for k in range(TOP_K)]
    routed = jnp.where(functools.reduce(jnp.logical_or, hits), 1.0, 0.0)
    base = carry_ref[...]
    ahead = jnp.dot(routed.astype(BF16), before, preferred_element_type=F32) + base
    rows = [jnp.sum(jnp.where(hit, ahead, 0.0), axis=0, keepdims=True) for hit in hits]
    rank_ref[...] = jnp.concatenate(rows, axis=0).astype(I32)
    total = base + jnp.sum(routed, axis=1, keepdims=True)
    carry_ref[...] = total
    cnt_ref[...] = total


def _rank(idx, *, tm):
    n = idx.shape[1]
    return pl.pallas_call(
        _rank_kernel,
        out_shape=(jax.ShapeDtypeStruct((TOP_K, n), I32), jax.ShapeDtypeStruct((N_EXPERTS, 1), F32)),
        grid=(n // tm,),
        in_specs=[pl.BlockSpec((TOP_K, tm), lambda i: (0, i))],
        out_specs=(pl.BlockSpec((TOP_K, tm), lambda i: (0, i)), pl.BlockSpec((N_EXPERTS, 1), lambda i: (0, 0))),
        scratch_shapes=[pltpu.VMEM((N_EXPERTS, 1), F32)],
        compiler_params=_params("arbitrary"),
        name="rank",
    )(idx)


def _pos_kernel(idx_ref, rank_ref, pstart_ref, pos_ref):
    tm = idx_ref.shape[1]
    idx = idx_ref[...]
    ite = lax.broadcasted_iota(I32, (N_EXPERTS, tm), 0)
    pstart = pstart_ref[...]
    rows = [jnp.sum(jnp.where(ite == idx[k:k + 1, :], pstart, 0.0), axis=0, keepdims=True) for k in range(TOP_K)]
    pos_ref[...] = jnp.concatenate(rows, axis=0).astype(I32) + rank_ref[...]


def _pos(idx, rank, pstart_col, *, tm):
    n = idx.shape[1]
    return pl.pallas_call(
        _pos_kernel,
        out_shape=jax.ShapeDtypeStruct((TOP_K, n), I32),
        grid=(n // tm,),
        in_specs=[pl.BlockSpec((TOP_K, tm), lambda i: (0, i)),
                  pl.BlockSpec((TOP_K, tm), lambda i: (0, i)),
                  pl.BlockSpec((N_EXPERTS, 1), lambda i: (0, 0))],
        out_specs=pl.BlockSpec((TOP_K, tm), lambda i: (0, i)),
        compiler_params=_params("arbitrary"),
        name="pos",
    )(idx, rank, pstart_col)


UNIT_CHUNKS = (4, 2, 1)
RING_AHEAD = 4
RING_SLOTS = RING_AHEAD + UNIT_CHUNKS[0]


def _expert_kernel(ps_ref, cnt_ref, tot_ref, wg_ref, wu_ref, wd_ref, xa_hbm, xb_hbm, ya_hbm, yb_hbm,
                   xa_buf, xb_buf, ya_buf, yb_buf, in_sem, out_sem):
    e = pl.program_id(0)
    r = EXPERT_ROWS
    ahead = RING_AHEAD
    total = tot_ref[0]
    first = ps_ref[e] // r
    cnt = cnt_ref[e]
    nch = (cnt + r - 1) // r

    def rows_of(g):
        return pl.ds(pl.multiple_of(g * r, r), r)

    def fetch(g):
        slot = g % RING_SLOTS
        return (pltpu.make_async_copy(xa_hbm.at[rows_of(g)], xa_buf.at[slot], in_sem.at[0, slot]),
                pltpu.make_async_copy(xb_hbm.at[rows_of(g)], xb_buf.at[slot], in_sem.at[1, slot]))

    def flush(g):
        slot = g % RING_SLOTS
        return (pltpu.make_async_copy(ya_buf.at[slot], ya_hbm.at[rows_of(g)], out_sem.at[0, slot]),
                pltpu.make_async_copy(yb_buf.at[slot], yb_hbm.at[rows_of(g)], out_sem.at[1, slot]))

    def start(copies):
        for cp in copies:
            cp.start()

    def wait(copies):
        for cp in copies:
            cp.wait()

    @pl.when(e == 0)
    def _():
        for g in range(ahead):
            @pl.when(g < total)
            def _():
                start(fetch(g))

    @pl.when(nch > 0)
    def _():
        def unit(c0, u):
            for j in range(u):
                g = first + c0 + j
                wait(fetch(g))

                @pl.when(g + ahead < total)
                def _():
                    start(fetch(g + ahead))

                @pl.when(g >= RING_SLOTS)
                def _():
                    wait(flush(g - RING_SLOTS))

            slots = [(first + c0 + j) % RING_SLOTS for j in range(u)]
            packed = jnp.concatenate([jnp.concatenate([xa_buf[sl], xb_buf[sl]], axis=1) for sl in slots], axis=0)
            row = lax.broadcasted_iota(I32, packed.shape, 0)
            packed = jnp.where(row < cnt - c0 * r, packed, jnp.zeros_like(packed))
            x = _unpack_bf16_pairs(packed).astype(BF16)
            gate = jnp.dot(x, wg_ref[0].astype(BF16), preferred_element_type=F32)
            up = jnp.dot(x, wu_ref[0].astype(BF16), preferred_element_type=F32)
            y = jnp.dot((_silu(gate) * up).astype(BF16), wd_ref[0].astype(BF16), preferred_element_type=F32)
            out = _pack_bf16_pairs(y)
            half = out.shape[1] // 2
            for j, sl in enumerate(slots):
                ya_buf[sl] = out[j * r:(j + 1) * r, :half]
                yb_buf[sl] = out[j * r:(j + 1) * r, half:]
                start(flush(first + c0 + j))

        big = UNIT_CHUNKS[0]

        def big_unit(i, carry):
            unit(i * big, big)
            return carry

        lax.fori_loop(0, nch // big, big_unit, 0)
        done = nch // big * big
        for u in UNIT_CHUNKS[1:]:
            @pl.when((nch % (2 * u)) >= u)
            def _():
                unit(done, u)

            done = done + jnp.where((nch % (2 * u)) >= u, u, 0)

    @pl.when(e == pl.num_programs(0) - 1)
    def _():
        for k in range(RING_SLOTS):
            @pl.when(total - 1 - k >= 0)
            def _():
                wait(flush(total - 1 - k))


def _experts(pstart, cnt, total_chunks, xs_a, xs_b, wg, wu, wd):
    rows, hw = xs_a.shape
    r = EXPERT_ROWS
    n_exp, d, f = wg.shape
    hbm = pl.BlockSpec(memory_space=pl.ANY)
    grid_spec = pltpu.PrefetchScalarGridSpec(
        num_scalar_prefetch=3,
        grid=(n_exp,),
        in_specs=[pl.BlockSpec((1, d, f), lambda e, ps, cn, tot: (e, 0, 0)),
                  pl.BlockSpec((1, d, f), lambda e, ps, cn, tot: (e, 0, 0)),
                  pl.BlockSpec((1, f, d), lambda e, ps, cn, tot: (e, 0, 0)),
                  hbm, hbm],
        out_specs=(hbm, hbm),
        scratch_shapes=[pltpu.VMEM((RING_SLOTS, r, hw), U32), pltpu.VMEM((RING_SLOTS, r, hw), U32),
                        pltpu.VMEM((RING_SLOTS, r, hw), U32), pltpu.VMEM((RING_SLOTS, r, hw), U32),
                        pltpu.SemaphoreType.DMA((2, RING_SLOTS)), pltpu.SemaphoreType.DMA((2, RING_SLOTS))],
    )
    return pl.pallas_call(
        _expert_kernel,
        out_shape=(jax.ShapeDtypeStruct((rows, hw), U32), jax.ShapeDtypeStruct((rows, hw), U32)),
        grid_spec=grid_spec,
        compiler_params=_params("arbitrary"),
        name="experts",
    )(pstart, cnt, total_chunks, wg, wu, wd, xs_a, xs_b)


def _sc_scatter_rows(x, idx_flat, n_rows):
    n, w = x.shape
    m = idx_flat.shape[1]
    nwin = n // SC_WINDOW
    reps = m // n
    mesh = plsc.VectorSubcoreMesh(core_axis_name="core", subcore_axis_name="subcore")

    @functools.partial(pl.kernel, out_type=jax.ShapeDtypeStruct((n_rows, w), x.dtype), mesh=mesh,
                       scratch_types=[])
    def scatter_kernel(x_hbm, i_hbm, o_hbm):
        def body(x_vmem, i_vmem):
            pltpu.sync_copy(x_vmem, o_hbm.at[i_vmem.at[0]])

        pltpu.emit_pipeline(
            body,
            grid=(nwin, reps),
            in_specs=[pl.BlockSpec((SC_WINDOW, w), lambda i, k: (i, 0)),
                      pl.BlockSpec((1, SC_WINDOW), lambda i, k: (0, k * nwin + i))],
            out_specs=[],
            core_axis_name=("core", "subcore"),
            dimension_semantics=(pltpu.PARALLEL, pltpu.ARBITRARY),
        )(x_hbm, i_hbm)

    return scatter_kernel(x, idx_flat)


def _sc_gather_rows(src, idx_flat):
    m = idx_flat.shape[1]
    w = src.shape[1]
    mesh = plsc.VectorSubcoreMesh(core_axis_name="core", subcore_axis_name="subcore")

    @functools.partial(pl.kernel, out_type=jax.ShapeDtypeStruct((m, w), src.dtype), mesh=mesh)
    def gather_kernel(x_hbm, i_hbm, o_hbm):
        def body(i_vmem, o_vmem):
            pltpu.sync_copy(x_hbm.at[i_vmem.at[0]], o_vmem)

        pltpu.emit_pipeline(
            body,
            grid=(m // SC_WINDOW,),
            in_specs=[pl.BlockSpec((1, SC_WINDOW), lambda i: (0, i))],
            out_specs=[pl.BlockSpec((SC_WINDOW, w), lambda i: (i, 0))],
            core_axis_name=("core", "subcore"),
            dimension_semantics=(pltpu.PARALLEL,),
        )(i_hbm, o_hbm)

    return gather_kernel(src, idx_flat)


def _finish_kernel(gate_ref, xres_ref, mod_ref, ya_ref, yb_ref, o_ref, *, d, tiles_per_batch):
    tm = gate_ref.shape[1]
    b = pl.program_id(0) // tiles_per_batch
    gates = gate_ref[...].T
    acc = jnp.zeros((tm, d), F32)
    for k in range(TOP_K):
        packed = jnp.concatenate([ya_ref[k], yb_ref[k]], axis=1)
        acc = acc + _unpack_bf16_pairs(packed) * gates[:, k:k + 1]
    o_ref[...] = xres_ref[...] + mod_ref[pl.ds(b, 1), 5 * d:6 * d] * acc


def _finish(gates, xres, mod, yg_a, yg_b, *, tm, tiles_per_batch):
    n, d = xres.shape
    return pl.pallas_call(
        functools.partial(_finish_kernel, d=d, tiles_per_batch=tiles_per_batch),
        out_shape=jax.ShapeDtypeStruct((n, d), F32),
        grid=(n // tm,),
        in_specs=[pl.BlockSpec((TOP_K, tm), lambda i: (0, i)),
                  pl.BlockSpec((tm, d), lambda i: (i, 0)),
                  pl.BlockSpec(mod.shape, lambda i: (0, 0)),
                  pl.BlockSpec((TOP_K, tm, d // 4), lambda i: (0, i, 0)),
                  pl.BlockSpec((TOP_K, tm, d // 4), lambda i: (0, i, 0))],
        out_specs=pl.BlockSpec((tm, d), lambda i: (i, 0)),
        compiler_params=_params("arbitrary"),
        name="finish",
    )(gates, xres, mod, yg_a, yg_b)


_QA_HEAD_ORDER = (0, 4, 1, 5, 2, 6, 3, 7)


def _reorder_qa_heads(a, axis):
    return jnp.concatenate([lax.slice_in_dim(a, h * HEAD_DIM, (h + 1) * HEAD_DIM, axis=axis)
                            for h in _QA_HEAD_ORDER], axis=axis)


def _rope_tables(s):
    quarter = HEAD_DIM // 4
    t = jnp.arange(s)
    row = (t // GRID_W).astype(F32)
    col = (t % GRID_W).astype(F32)
    freqs = ROPE_BASE ** (-jnp.arange(quarter, dtype=F32) / quarter)
    ar = row[:, None] * freqs[None, :]
    ac = col[:, None] * freqs[None, :]
    cos = jnp.concatenate([jnp.cos(ar), jnp.cos(ar), jnp.cos(ac), jnp.cos(ac)], axis=1)
    sin = jnp.concatenate([-jnp.sin(ar), jnp.sin(ar), -jnp.sin(ac), jnp.sin(ac)], axis=1)
    return jnp.tile(cos, (1, 2)), jnp.tile(sin, (1, 2))


def kernel(x, c, ctx, c_ctx, w_ada, b_ada, norm1, norm2, w_in, q_norm_a, k_norm_a, q_norm_b, k_norm_b, sink_a,
           rpb_b, out_norm_a, out_norm_b, w_out, w_router, router_bias, we_gate, we_up, we_down, ws_gate, ws_up,
           ws_down):
    assert w_ada.shape[0] == 1, "single-layer block"
    bsz, s, d = x.shape
    lc = ctx.shape[1]
    n = bsz * s
    rows = s // GRID_W
    assert d == 1024 and bsz + 1 <= 8 and rows >= NA_KROWS and lc % LANES == 0
    assert s % TOKEN_TILE == 0 and n % SC_WINDOW == 0
    assert (s // A_WINDOW) % A_BLOCKS_PER_STEP == 0 and (s // (NA_QROWS * GRID_W)) % NA_GROUPS_PER_STEP == 0

    c8 = jnp.concatenate([c, c_ctx[None, :], jnp.zeros((8 - bsz - 1, d), F32)], axis=0)
    mod = _ada(c8, w_ada[0], b_ada[0][None, :])

    w = w_in[0]
    aw, akw, bw = A_Q_HEADS * HEAD_DIM, A_KV_HEADS * HEAD_DIM, B_HEADS * HEAD_DIM
    cuts = np.cumsum([0, aw, akw, akw, bw, bw, bw])
    qa_w, ka_w, va_w, qb_w, kb_w, vb_w = [w[:, cuts[i]:cuts[i + 1]] for i in range(6)]
    qa_w = _reorder_qa_heads(qa_w, 1)
    w_ext = jnp.concatenate([qa_w, qb_w, kb_w, vb_w, ka_w, va_w], axis=1).astype(BF16)
    scale = HEAD_DIM ** -0.5 * LOG2E
    pair = lambda g: jnp.tile(g, 2)
    gains = jnp.stack([pair(q_norm_a[0]) * scale, pair(q_norm_b[0]) * scale, pair(k_norm_b[0]), pair(k_norm_a[0])]
                      + [jnp.zeros((LANES,), F32)] * 4)
    bd = jnp.asarray(np.kron(np.eye(2), np.full((HEAD_DIM, HEAD_DIM), 1.0 / HEAD_DIM)), BF16)
    cos_t, sin_t = _rope_tables(s)
    n1 = norm1[0][None, :]
    zq = _inproj(x, mod, n1, w_ext, gains, cos_t, sin_t, bd, mod_row=None, tm=TOKEN_TILE)
    zc = _inproj(ctx, mod, n1, w_ext, gains, jnp.ones((lc, LANES), F32), jnp.zeros((lc, LANES), F32), bd,
                 mod_row=bsz, tm=lc)

    sink = sink_a[0].astype(F32)
    sink_rep = jnp.concatenate([jnp.broadcast_to(sink[h] * LOG2E, (A_WINDOW, LANES)) for h in _QA_HEAD_ORDER])
    o_a = _attn_a(zq, zc, sink_rep, _attn_a_mask(A_WINDOW))
    o_b = _attn_b(zq, zc, _na_table(rpb_b[0], rows))

    ga = _reorder_qa_heads(out_norm_a[0], 0)[None, :]
    gb = out_norm_b[0][None, :]
    wo = jnp.concatenate([_reorder_qa_heads(w_out[0][:aw], 0), w_out[0][aw:]], axis=0).astype(BF16)
    wr_hi, wr_lo = _split_bf16(w_router[0].T)
    xres, hp_a, hp_b, idx, gates = _merge(o_a, o_b, x, mod, ga, gb, wo, norm2[0][None, :], wr_hi, wr_lo,
                                  router_bias[0][:, None], ws_gate[0].astype(BF16), ws_up[0].astype(BF16),
                                  ws_down[0].astype(BF16), tm=TOKEN_TILE)

    rank, counts = _rank(idx, tm=TOKEN_TILE)
    t = EXPERT_ROWS
    cnt = counts[:, 0].astype(I32)
    padded = (cnt + t - 1) // t * t
    pends = jnp.cumsum(padded)
    pstart = pends - padded
    n_rows = n * TOP_K + N_EXPERTS * t
    pos = _pos(idx, rank, pstart.astype(F32)[:, None], tm=min(POS_TILE, n))

    pos_flat = pos.reshape(1, TOP_K * n)
    xs_a = _sc_scatter_rows(hp_a, pos_flat, n_rows)
    xs_b = _sc_scatter_rows(hp_b, pos_flat, n_rows)
    ys_a, ys_b = _experts(pstart, cnt, pends[-1:] // t, xs_a, xs_b, we_gate[0], we_up[0], we_down[0])
    yg_a = _sc_gather_rows(ys_a, pos_flat).reshape(TOP_K, n, d // 4)
    yg_b = _sc_gather_rows(ys_b, pos_flat).reshape(TOP_K, n, d // 4)
    out = _finish(gates, xres.reshape(n, d), mod, yg_a, yg_b, tm=TOKEN_TILE, tiles_per_batch=s // TOKEN_TILE)
    return out.reshape(bsz, s, d)
```

```python
import functools

import numpy as np
import jax
import jax.numpy as jnp
from jax import lax
from jax.experimental import pallas as pl
from jax.experimental.pallas import tpu as pltpu
from jax.experimental.pallas import tpu_sc as plsc

F32 = jnp.float32
BF16 = jnp.bfloat16
I32 = jnp.int32
U32 = jnp.uint32

LANES = 128
HEAD_DIM = 64
HEAD_PAIR = 2 * HEAD_DIM
GRID_W = 64
A_Q_HEADS = 8
A_KV_HEADS = 2
A_WINDOW = 128
B_HEADS = 8
NA_KH = 8
NA_KW = 16
NA_QROWS = 4
NA_KROWS = NA_QROWS + NA_KH
ROPE_BASE = 10000.0
N_EXPERTS = 256
TOP_K = 8
N_GROUPS = 8
TOPK_GROUPS = 4
ROUTED_SCALE = 2.5
LOG2E = 1.4426950408889634
EPS = 1e-6
NEG_INF = -1e30
TOKEN_TILE = 512
POS_TILE = 2048
EXPERT_ROWS = 256
SC_WINDOW = 128
VMEM_LIMIT = 56 * 1024 * 1024

_NT = (((1,), (1,)), ((), ()))


def _params(*sem):
    return pltpu.CompilerParams(dimension_semantics=sem, vmem_limit_bytes=VMEM_LIMIT)


def _silu(v):
    return v * jax.nn.sigmoid(v)


def _rms(v, gain):
    return v * lax.rsqrt(jnp.mean(v * v, axis=-1, keepdims=True) + EPS) * gain


def _pack_bf16_pairs(v):
    n = v.shape[1] // 2
    lo = lax.bitcast_convert_type(v[:, :n].astype(BF16).astype(F32), U32) >> 16
    hi = lax.bitcast_convert_type(v[:, n:].astype(BF16).astype(F32), U32) & jnp.uint32(0xFFFF0000)
    return hi | lo


def _split_bf16(v):
    hi = lax.bitcast_convert_type(lax.bitcast_convert_type(v, U32) & jnp.uint32(0xFFFF0000), F32)
    return hi.astype(BF16), (v - hi).astype(BF16)


def _unpack_bf16_pairs(w):
    lo = lax.bitcast_convert_type(w << 16, F32)
    hi = lax.bitcast_convert_type(w & jnp.uint32(0xFFFF0000), F32)
    return jnp.concatenate([lo, hi], axis=1)


def _ada_kernel(c_ref, w_ref, b_ref, o_ref):
    a = _silu(c_ref[...])
    o_ref[...] = jnp.dot(a, w_ref[...], preferred_element_type=F32,
                         precision=lax.Precision.HIGHEST) + b_ref[...]


def _ada(c8, w, b):
    d, n = w.shape
    bn = n // 4
    return pl.pallas_call(
        _ada_kernel,
        out_shape=jax.ShapeDtypeStruct((8, n), F32),
        grid=(n // bn,),
        in_specs=[pl.BlockSpec((8, d), lambda j: (0, 0)),
                  pl.BlockSpec((d, bn), lambda j: (0, j)),
                  pl.BlockSpec((1, bn), lambda j: (0, j))],
        out_specs=pl.BlockSpec((8, bn), lambda j: (0, j)),
        compiler_params=_params("arbitrary"),
        name="ada",
    )(c8, w, b)


_QA, _QB, _KB, _VB, _KA, _VA = 0, 4, 8, 12, 16, 17
_OUT_BLOCKS = 18


def _inproj_kernel(x_ref, mod_ref, n1_ref, w_ref, g_ref, cos_ref, sin_ref, bd_ref, o_ref, *, mod_row, d):
    b = pl.program_id(0) if mod_row is None else mod_row
    xn = _rms(x_ref[0], n1_ref[...])
    sh = mod_ref[pl.ds(b, 1), 0:d]
    sc = mod_ref[pl.ds(b, 1), d:2 * d]
    h = (xn * (1.0 + sc) + sh).astype(BF16)
    z = jnp.dot(h, w_ref[...], preferred_element_type=F32)
    bd = bd_ref[...]
    cos = cos_ref[...]
    sin = sin_ref[...]

    def blk(j):
        return z[:, j * LANES:(j + 1) * LANES]

    def head_rinv(zb):
        ms = jnp.dot((zb * zb).astype(BF16), bd, preferred_element_type=F32)
        return lax.rsqrt(ms + EPS)

    def put(j, v):
        o_ref[0, :, j * LANES:(j + 1) * LANES] = v.astype(BF16)

    quarter = HEAD_DIM // 4
    first_half = (lax.broadcasted_iota(I32, (1, LANES), 1) % (2 * quarter)) < quarter

    def roped(j, g_row):
        zb = blk(j)
        zn = zb * head_rinv(zb) * g_ref[g_row:g_row + 1, :]
        partner = jnp.where(first_half, pltpu.roll(zn, LANES - quarter, 1), pltpu.roll(zn, quarter, 1))
        put(j, zn * cos + partner * sin)

    def normed(j, g_row):
        zb = blk(j)
        put(j, zb * head_rinv(zb) * g_ref[g_row:g_row + 1, :])

    for j in range(4):
        roped(_QA + j, 0)
        normed(_QB + j, 1)
        normed(_KB + j, 2)
        put(_VB + j, blk(_VB + j))
    roped(_KA, 3)
    put(_VA, blk(_VA))


def _inproj(x, mod, n1, w_ext, gains, cos_t, sin_t, bd, *, mod_row, tm):
    bsz, s, d = x.shape
    kern = functools.partial(_inproj_kernel, mod_row=mod_row, d=d)
    return pl.pallas_call(
        kern,
        out_shape=jax.ShapeDtypeStruct((bsz, s, _OUT_BLOCKS * LANES), BF16),
        grid=(bsz, s // tm),
        in_specs=[pl.BlockSpec((1, tm, d), lambda b, i: (b, i, 0)),
                  pl.BlockSpec(mod.shape, lambda b, i: (0, 0)),
                  pl.BlockSpec((1, d), lambda b, i: (0, 0)),
                  pl.BlockSpec(w_ext.shape, lambda b, i: (0, 0)),
                  pl.BlockSpec(gains.shape, lambda b, i: (0, 0)),
                  pl.BlockSpec((tm, LANES), lambda b, i: (i, 0)),
                  pl.BlockSpec((tm, LANES), lambda b, i: (i, 0)),
                  pl.BlockSpec(bd.shape, lambda b, i: (0, 0))],
        out_specs=pl.BlockSpec((1, tm, _OUT_BLOCKS * LANES), lambda b, i: (b, i, 0)),
        compiler_params=_params("arbitrary", "arbitrary"),
        name="inproj",
    )(x, mod, n1, w_ext, gains, cos_t, sin_t, bd)


def _split_pair(qp, lo):
    zero = jnp.zeros_like(qp)
    return jnp.concatenate([jnp.where(lo, qp, zero), jnp.where(lo, zero, qp)], axis=0)


def _softmax_pv(s_parts, v_parts, sink_rep):
    chunks = [s[:, c * LANES:(c + 1) * LANES] for s in s_parts for c in range(s.shape[1] // LANES)]
    m = jnp.max(functools.reduce(jnp.maximum, chunks), axis=-1, keepdims=True)
    m_rep = jnp.broadcast_to(m, (m.shape[0], LANES))
    if sink_rep is not None:
        m_rep = jnp.maximum(m_rep, sink_rep)
    acc = None
    for s, v in zip(s_parts, v_parts):
        p = jnp.concatenate([jnp.exp2(s[:, c * LANES:(c + 1) * LANES] - m_rep)
                             for c in range(s.shape[1] // LANES)], axis=1).astype(BF16)
        v_ext = jnp.concatenate([v, jnp.ones_like(v)], axis=1)
        o = jnp.dot(p, v_ext, preferred_element_type=F32)
        acc = o if acc is None else acc + o
    l_rep = acc[:, LANES:]
    if sink_rep is not None:
        l_rep = l_rep + jnp.exp2(sink_rep - m_rep)
    return acc[:, :LANES] * (1.0 / l_rep)


A_BLOCKS_PER_STEP = 8


def _attn_a_kernel(q_ref, *refs):
    nq = A_BLOCKS_PER_STEP
    k_refs, v_refs = refs[:nq + 2], refs[nq + 2:2 * nq + 4]
    kx_ref, vx_ref, sink_ref = refs[2 * nq + 4:2 * nq + 7]
    mask_refs = refs[2 * nq + 7:3 * nq + 7]
    o_ref = refs[-1]
    tq = A_WINDOW
    lo = lax.broadcasted_iota(I32, (tq, LANES), 1) < HEAD_DIM
    for h in range(nq):
        q = q_ref[0, h * tq:(h + 1) * tq]
        qs = jnp.concatenate([_split_pair(q[:, p * LANES:(p + 1) * LANES], lo) for p in range(4)], axis=0)
        k = jnp.concatenate([r[0] for r in k_refs[h:h + 3]], axis=0)
        v = jnp.concatenate([r[0] for r in v_refs[h:h + 3]], axis=0)
        s_loc = lax.dot_general(qs, k, _NT, preferred_element_type=F32) + mask_refs[h][0]
        s_ctx = lax.dot_general(qs, kx_ref[0], _NT, preferred_element_type=F32)
        o = _softmax_pv([s_loc, s_ctx], [v, vx_ref[0]], sink_ref[...])
        for p in range(4):
            o_lo = o[(2 * p) * tq:(2 * p + 1) * tq]
            o_hi = o[(2 * p + 1) * tq:(2 * p + 2) * tq]
            o_ref[0, h * tq:(h + 1) * tq, p * LANES:(p + 1) * LANES] = jnp.where(lo, o_lo, o_hi).astype(BF16)


def _attn_a(zq, zc, sink_rep, mask):
    bsz, s, _ = zq.shape
    lc = zc.shape[1]
    tq = A_WINDOW
    nq = A_BLOCKS_PER_STEP
    nblk = s // tq
    ka, va = _KA, _VA

    def kv_spec(col, shift):
        return pl.BlockSpec((1, tq, LANES), lambda b, j: (b, jnp.clip(nq * j + shift, 0, nblk - 1), col))

    def mask_spec(h):
        def variant(j):
            blk = nq * j + h
            return jnp.where(blk == 0, 0, jnp.where(blk == nblk - 1, 2, 1))
        return pl.BlockSpec((1,) + mask.shape[1:], lambda b, j: (variant(j), 0, 0))

    shifts = range(-1, nq + 1)
    return pl.pallas_call(
        _attn_a_kernel,
        out_shape=jax.ShapeDtypeStruct((bsz, s, 4 * LANES), BF16),
        grid=(bsz, nblk // nq),
        in_specs=([pl.BlockSpec((1, nq * tq, 4 * LANES), lambda b, j: (b, j, 0))]
                  + [kv_spec(ka, sh) for sh in shifts] + [kv_spec(va, sh) for sh in shifts]
                  + [pl.BlockSpec((1, lc, LANES), lambda b, j: (b, 0, ka)),
                     pl.BlockSpec((1, lc, LANES), lambda b, j: (b, 0, va)),
                     pl.BlockSpec(sink_rep.shape, lambda b, j: (0, 0))]
                  + [mask_spec(h) for h in range(nq)]),
        out_specs=pl.BlockSpec((1, nq * tq, 4 * LANES), lambda b, j: (b, j, 0)),
        compiler_params=_params("arbitrary", "arbitrary"),
        name="attn_a",
    )(*([zq] * (2 * nq + 5) + [zc, zc, sink_rep] + [mask] * nq))


def _attn_a_mask(tq):
    qi = np.arange(tq)[:, None]
    kj = np.arange(3 * tq)[None, :]
    ok = (kj >= qi) & (kj <= qi + 2 * tq)
    variants = [ok & (kj >= tq), ok, ok & (kj < 2 * tq)]
    m = np.stack([np.tile(np.where(v, 0.0, NEG_INF).astype(np.float32), (A_Q_HEADS, 1)) for v in variants])
    return jnp.asarray(m)


NA_GROUPS_PER_STEP = 4


def _attn_b_kernel(q_ref, *refs):
    ngs = NA_GROUPS_PER_STEP
    k_refs, v_refs = refs[:3 * ngs], refs[3 * ngs:6 * ngs]
    kx_ref, vx_ref = refs[6 * ngs:6 * ngs + 2]
    tab_refs = refs[6 * ngs + 2:7 * ngs + 2]
    o_ref = refs[-1]
    tq = NA_QROWS * GRID_W
    lo = lax.broadcasted_iota(I32, (tq, LANES), 1) < HEAD_DIM
    for g in range(ngs):
        rows_g = slice(g * tq, (g + 1) * tq)
        for p in range(4):
            sl = slice(p * LANES, (p + 1) * LANES)
            qs = _split_pair(q_ref[0, rows_g, sl], lo)
            k = jnp.concatenate([r[0, :, sl] for r in k_refs[3 * g:3 * g + 3]], axis=0)
            v = jnp.concatenate([r[0, :, sl] for r in v_refs[3 * g:3 * g + 3]], axis=0)
            s_loc = lax.dot_general(qs, k, _NT, preferred_element_type=F32) + tab_refs[g][0, p]
            s_ctx = lax.dot_general(qs, kx_ref[0, :, sl], _NT, preferred_element_type=F32)
            o = _softmax_pv([s_loc, s_ctx], [v, vx_ref[0, :, sl]], None)
            o_ref[0, rows_g, sl] = jnp.where(lo, o[:tq], o[tq:]).astype(BF16)


def _attn_b(zq, zc, table):
    bsz, s, _ = zq.shape
    lc = zc.shape[1]
    tq = NA_QROWS * GRID_W
    ngs = NA_GROUPS_PER_STEP
    ng = s // tq
    qb, kb, vb = _QB // 4, _KB // 4, _VB // 4

    def kv_spec(col, g, off):
        return pl.BlockSpec((1, tq, 4 * LANES),
                            lambda i, b: (b, jnp.clip(ngs * i + g - 1, 0, ng - 3) + off, col))

    def tab_spec(g):
        def variant(i):
            grp = ngs * i + g
            return jnp.where(grp == 0, 0, jnp.where(grp == ng - 1, 2, 1))
        return pl.BlockSpec((1,) + table.shape[1:], lambda i, b: (variant(i), 0, 0, 0),
                            pipeline_mode=pl.Buffered(1))

    kv_slots = [(g, off) for g in range(ngs) for off in range(3)]
    return pl.pallas_call(
        _attn_b_kernel,
        out_shape=jax.ShapeDtypeStruct((bsz, s, 4 * LANES), BF16),
        grid=(ng // ngs, bsz),
        in_specs=([pl.BlockSpec((1, ngs * tq, 4 * LANES), lambda i, b: (b, i, qb))]
                  + [kv_spec(kb, g, off) for g, off in kv_slots] + [kv_spec(vb, g, off) for g, off in kv_slots]
                  + [pl.BlockSpec((1, lc, 4 * LANES), lambda i, b: (b, 0, kb)),
                     pl.BlockSpec((1, lc, 4 * LANES), lambda i, b: (b, 0, vb))]
                  + [tab_spec(g) for g in range(ngs)]),
        out_specs=pl.BlockSpec((1, ngs * tq, 4 * LANES), lambda i, b: (b, i, 0)),
        compiler_params=_params("arbitrary", "arbitrary"),
        name="attn_b",
    )(*([zq] * (6 * ngs + 1) + [zc, zc] + [table] * ngs))


def _na_table(rpb, rows):
    ng = rows // NA_QROWS
    nq, nk = NA_QROWS * GRID_W, NA_KROWS * GRID_W
    qc = np.arange(GRID_W)
    kc = np.arange(GRID_W)
    ws = np.clip(qc - NA_KW // 2, 0, GRID_W - NA_KW)
    valid_c = (kc[None, :] >= ws[:, None]) & (kc[None, :] < ws[:, None] + NA_KW)
    dc = np.clip(kc[None, :] - qc[:, None], -(NA_KW - 1), NA_KW - 1) + (NA_KW - 1)
    c_sel = (dc[..., None] == np.arange(2 * NA_KW - 1)) & valid_c[..., None]
    tiles = jnp.einsum('hab,uvb->huav', rpb.astype(F32), jnp.asarray(c_sel, F32), precision=lax.Precision.HIGHEST)
    tiles = jnp.where(jnp.asarray(valid_c)[None, :, None, :], tiles * LOG2E, NEG_INF)
    tiles = tiles.reshape(B_HEADS, GRID_W, (2 * NA_KH - 1) * GRID_W)

    def masked(n_key_rows):
        return jnp.full((B_HEADS, GRID_W, n_key_rows * GRID_W), NEG_INF, F32)

    tabs = []
    for i in (0, 1, ng - 1):
        start = int(np.clip(NA_QROWS * i - NA_KH // 2, 0, rows - NA_KROWS))
        q_rows = []
        for qr in range(NA_QROWS):
            r = NA_QROWS * i + qr
            rs = int(np.clip(r - NA_KH // 2, 0, rows - NA_KH))
            y0 = rs - start
            d0 = rs - r + (NA_KH - 1)
            q_rows.append(jnp.concatenate([masked(y0), tiles[:, :, d0 * GRID_W:(d0 + NA_KH) * GRID_W],
                                           masked(NA_KROWS - NA_KH - y0)], axis=-1))
        tabs.append(jnp.concatenate(q_rows, axis=-2))
    return jnp.stack(tabs).reshape(3, B_HEADS // 2, 2 * nq, nk)


def _merge_kernel(oa_ref, ob_ref, x_ref, mod_ref, ga_ref, gb_ref, wo_ref, n2_ref, wrh_ref, wrl_ref, rb_ref,
                  wsg_ref, wsu_ref, wsd_ref, xres_ref, hpa_ref, hpb_ref, idx_ref, gate_ref, *, d):
    b = pl.program_id(0)
    tm = x_ref.shape[1]

    def mod(k):
        return mod_ref[pl.ds(b, 1), k * d:(k + 1) * d]

    na = _rms(oa_ref[0].astype(F32), ga_ref[...])
    nb = _rms(ob_ref[0].astype(F32), gb_ref[...])
    cat = jnp.concatenate([na, nb], axis=1).astype(BF16)
    y = jnp.dot(cat, wo_ref[...], preferred_element_type=F32)
    x1 = x_ref[0] + mod(2) * y
    h2 = _rms(x1, n2_ref[...]) * (1.0 + mod(4)) + mod(3)

    hb = h2.astype(BF16)
    act = _silu(jnp.dot(hb, wsg_ref[...], preferred_element_type=F32)) * jnp.dot(hb, wsu_ref[...],
                                                                                 preferred_element_type=F32)
    shared = jnp.dot(act.astype(BF16), wsd_ref[...], preferred_element_type=F32)
    xres_ref[0] = x1 + mod(5) * shared
    packed = _pack_bf16_pairs(h2)
    hpa_ref[...] = packed[:, :d // 4]
    hpb_ref[...] = packed[:, d // 4:]

    h_hi, h_lo = _split_bf16(h2)
    logits = (lax.dot_general(wrh_ref[...], h_hi, _NT, preferred_element_type=F32)
              + lax.dot_general(wrh_ref[...], h_lo, _NT, preferred_element_type=F32)
              + lax.dot_general(wrl_ref[...], h_hi, _NT, preferred_element_type=F32))
    scores = jax.nn.sigmoid(logits)
    sel = scores + rb_ref[...]
    per = N_EXPERTS // N_GROUPS
    g3 = sel.reshape(N_GROUPS, per, tm)
    it3 = lax.broadcasted_iota(I32, (N_GROUPS, per, tm), 1)
    m1 = jnp.max(g3, axis=1, keepdims=True)
    first = jnp.min(jnp.where(g3 == m1, it3, per), axis=1, keepdims=True)
    m2 = jnp.max(jnp.where(it3 == first, -jnp.inf, g3), axis=1, keepdims=True)
    gscore = (m1 + m2).reshape(N_GROUPS, tm)

    itg = lax.broadcasted_iota(I32, (N_GROUPS, tm), 0)
    gsel = jnp.zeros((N_GROUPS, tm), F32)
    cur = gscore
    for _ in range(TOPK_GROUPS):
        mx = jnp.max(cur, axis=0, keepdims=True)
        fi = jnp.min(jnp.where(cur == mx, itg, N_GROUPS), axis=0, keepdims=True)
        pick = itg == fi
        gsel = jnp.where(pick, 1.0, gsel)
        cur = jnp.where(pick, -jnp.inf, cur)
    emask = jnp.broadcast_to(gsel.reshape(N_GROUPS, 1, tm), (N_GROUPS, per, tm)).reshape(N_EXPERTS, tm) > 0.5

    ite = lax.broadcasted_iota(I32, (N_EXPERTS, tm), 0)
    cur = jnp.where(emask, sel, NEG_INF)
    idx_rows, s_rows = [], []
    for _ in range(TOP_K):
        mx = jnp.max(cur, axis=0, keepdims=True)
        fi = jnp.min(jnp.where(cur == mx, ite, N_EXPERTS), axis=0, keepdims=True)
        pick = ite == fi
        idx_rows.append(fi)
        s_rows.append(jnp.sum(jnp.where(pick, scores, 0.0), axis=0, keepdims=True))
        cur = jnp.where(pick, -jnp.inf, cur)
    top_s = jnp.concatenate(s_rows, axis=0)
    idx_ref[...] = jnp.concatenate(idx_rows, axis=0)
    gate_ref[...] = top_s / jnp.sum(top_s, axis=0, keepdims=True) * ROUTED_SCALE


def _merge(o_a, o_b, x, mod, ga, gb, wo, n2, wr_hi, wr_lo, rb_col, wsg, wsu, wsd, *, tm):
    bsz, s, d = x.shape
    nt = s // tm
    n = bsz * s
    full = lambda a: pl.BlockSpec(a.shape, lambda b, i: (0,) * a.ndim)
    return pl.pallas_call(
        functools.partial(_merge_kernel, d=d),
        out_shape=(jax.ShapeDtypeStruct((bsz, s, d), F32),
                   jax.ShapeDtypeStruct((n, d // 4), U32),
                   jax.ShapeDtypeStruct((n, d // 4), U32),
                   jax.ShapeDtypeStruct((TOP_K, n), I32),
                   jax.ShapeDtypeStruct((TOP_K, n), F32)),
        grid=(bsz, nt),
        in_specs=[pl.BlockSpec((1, tm, d // 2), lambda b, i: (b, i, 0)),
                  pl.BlockSpec((1, tm, d // 2), lambda b, i: (b, i, 0)),
                  pl.BlockSpec((1, tm, d), lambda b, i: (b, i, 0)),
                  full(mod), full(ga), full(gb), full(wo), full(n2), full(wr_hi), full(wr_lo), full(rb_col),
                  full(wsg), full(wsu), full(wsd)],
        out_specs=(pl.BlockSpec((1, tm, d), lambda b, i: (b, i, 0)),
                   pl.BlockSpec((tm, d // 4), lambda b, i: (b * nt + i, 0)),
                   pl.BlockSpec((tm, d // 4), lambda b, i: (b * nt + i, 0)),
                   pl.BlockSpec((TOP_K, tm), lambda b, i: (0, b * nt + i)),
                   pl.BlockSpec((TOP_K, tm), lambda b, i: (0, b * nt + i))),
        compiler_params=_params("arbitrary", "arbitrary"),
        name="merge",
    )(o_a, o_b, x, mod, ga, gb, wo, n2, wr_hi, wr_lo, rb_col, wsg, wsu, wsd)


def _rank_kernel(idx_ref, rank_ref, cnt_ref, carry_ref):
    tm = idx_ref.shape[1]

    @pl.when(pl.program_id(0) == 0)
    def _():
        carry_ref[...] = jnp.zeros_like(carry_ref)

    idx = idx_ref[...]
    ite = lax.broadcasted_iota(I32, (N_EXPERTS, tm), 0)
    before = (lax.broadcasted_iota(I32, (tm, tm), 0) < lax.broadcasted_iota(I32, (tm, tm), 1)).astype(BF16)
    hits = [ite == idx[k:k + 1, :] for k in range(TOP_K)]
    routed = jnp.where(functools.reduce(jnp.logical_or, hits), 1.0, 0.0)
    base = carry_ref[...]
    ahead = jnp.dot(routed.astype(BF16), before, preferred_element_type=F32) + base
    rows = [jnp.sum(jnp.where(hit, ahead, 0.0), axis=0, keepdims=True) for hit in hits]
    rank_ref[...] = jnp.concatenate(rows, axis=0).astype(I32)
    total = base + jnp.sum(routed, axis=1, keepdims=True)
    carry_ref[...] = total
    cnt_ref[...] = total


def _rank(idx, *, tm):
    n = idx.shape[1]
    return pl.pallas_call(
        _rank_kernel,
        out_shape=(jax.ShapeDtypeStruct((TOP_K, n), I32), jax.ShapeDtypeStruct((N_EXPERTS, 1), F32)),
        grid=(n // tm,),
        in_specs=[pl.BlockSpec((TOP_K, tm), lambda i: (0, i))],
        out_specs=(pl.BlockSpec((TOP_K, tm), lambda i: (0, i)), pl.BlockSpec((N_EXPERTS, 1), lambda i: (0, 0))),
        scratch_shapes=[pltpu.VMEM((N_EXPERTS, 1), F32)],
        compiler_params=_params("arbitrary"),
        name="rank",
    )(idx)


def _pos_kernel(idx_ref, rank_ref, pstart_ref, pos_ref):
    tm = idx_ref.shape[1]
    idx = idx_ref[...]
    ite = lax.broadcasted_iota(I32, (N_EXPERTS, tm), 0)
    pstart = pstart_ref[...]
    rows = [jnp.sum(jnp.where(ite == idx[k:k + 1, :], pstart, 0.0), axis=0, keepdims=True) for k in range(TOP_K)]
    pos_ref[...] = jnp.concatenate(rows, axis=0).astype(I32) + rank_ref[...]


def _pos(idx, rank, pstart_col, *, tm):
    n = idx.shape[1]
    return pl.pallas_call(
        _pos_kernel,
        out_shape=jax.ShapeDtypeStruct((TOP_K, n), I32),
        grid=(n // tm,),
        in_specs=[pl.BlockSpec((TOP_K, tm), lambda i: (0, i)),
                  pl.BlockSpec((TOP_K, tm), lambda i: (0, i)),
                  pl.BlockSpec((N_EXPERTS, 1), lambda i: (0, 0))],
        out_specs=pl.BlockSpec((TOP_K, tm), lambda i: (0, i)),
        compiler_params=_params("arbitrary"),
        name="pos",
    )(idx, rank, pstart_col)


UNIT_CHUNKS = (4, 2, 1)
RING_AHEAD = 4
RING_SLOTS = RING_AHEAD + UNIT_CHUNKS[0]


def _expert_kernel(ps_ref, cnt_ref, tot_ref, wg_ref, wu_ref, wd_ref, xa_hbm, xb_hbm, ya_hbm, yb_hbm,
                   xa_buf, xb_buf, ya_buf, yb_buf, in_sem, out_sem):
    e = pl.program_id(0)
    r = EXPERT_ROWS
    ahead = RING_AHEAD
    total = tot_ref[0]
    first = ps_ref[e] // r
    cnt = cnt_ref[e]
    nch = (cnt + r - 1) // r

    def rows_of(g):
        return pl.ds(pl.multiple_of(g * r, r), r)

    def fetch(g):
        slot = g % RING_SLOTS
        return (pltpu.make_async_copy(xa_hbm.at[rows_of(g)], xa_buf.at[slot], in_sem.at[0, slot]),
                pltpu.make_async_copy(xb_hbm.at[rows_of(g)], xb_buf.at[slot], in_sem.at[1, slot]))

    def flush(g):
        slot = g % RING_SLOTS
        return (pltpu.make_async_copy(ya_buf.at[slot], ya_hbm.at[rows_of(g)], out_sem.at[0, slot]),
                pltpu.make_async_copy(yb_buf.at[slot], yb_hbm.at[rows_of(g)], out_sem.at[1, slot]))

    def start(copies):
        for cp in copies:
            cp.start()

    def wait(copies):
        for cp in copies:
            cp.wait()

    @pl.when(e == 0)
    def _():
        for g in range(ahead):
            @pl.when(g < total)
            def _():
                start(fetch(g))

    @pl.when(nch > 0)
    def _():
        def unit(c0, u):
            for j in range(u):
                g = first + c0 + j
                wait(fetch(g))

                @pl.when(g + ahead < total)
                def _():
                    start(fetch(g + ahead))

                @pl.when(g >= RING_SLOTS)
                def _():
                    wait(flush(g - RING_SLOTS))

            slots = [(first + c0 + j) % RING_SLOTS for j in range(u)]
            packed = jnp.concatenate([jnp.concatenate([xa_buf[sl], xb_buf[sl]], axis=1) for sl in slots], axis=0)
            row = lax.broadcasted_iota(I32, packed.shape, 0)
            packed = jnp.where(row < cnt - c0 * r, packed, jnp.zeros_like(packed))
            x = _unpack_bf16_pairs(packed).astype(BF16)
            gate = jnp.dot(x, wg_ref[0].astype(BF16), preferred_element_type=F32)
            up = jnp.dot(x, wu_ref[0].astype(BF16), preferred_element_type=F32)
            y = jnp.dot((_silu(gate) * up).astype(BF16), wd_ref[0].astype(BF16), preferred_element_type=F32)
            out = _pack_bf16_pairs(y)
            half = out.shape[1] // 2
            for j, sl in enumerate(slots):
                ya_buf[sl] = out[j * r:(j + 1) * r, :half]
                yb_buf[sl] = out[j * r:(j + 1) * r, half:]
                start(flush(first + c0 + j))

        big = UNIT_CHUNKS[0]

        def big_unit(i, carry):
            unit(i * big, big)
            return carry

        lax.fori_loop(0, nch // big, big_unit, 0)
        done = nch // big * big
        for u in UNIT_CHUNKS[1:]:
            @pl.when((nch % (2 * u)) >= u)
            def _():
                unit(done, u)

            done = done + jnp.where((nch % (2 * u)) >= u, u, 0)

    @pl.when(e == pl.num_programs(0) - 1)
    def _():
        for k in range(RING_SLOTS):
            @pl.when(total - 1 - k >= 0)
            def _():
                wait(flush(total - 1 - k))


def _experts(pstart, cnt, total_chunks, xs_a, xs_b, wg, wu, wd):
    rows, hw = xs_a.shape
    r = EXPERT_ROWS
    n_exp, d, f = wg.shape
    hbm = pl.BlockSpec(memory_space=pl.ANY)
    grid_spec = pltpu.PrefetchScalarGridSpec(
        num_scalar_prefetch=3,
        grid=(n_exp,),
        in_specs=[pl.BlockSpec((1, d, f), lambda e, ps, cn, tot: (e, 0, 0)),
                  pl.BlockSpec((1, d, f), lambda e, ps, cn, tot: (e, 0, 0)),
                  pl.BlockSpec((1, f, d), lambda e, ps, cn, tot: (e, 0, 0)),
                  hbm, hbm],
        out_specs=(hbm, hbm),
        scratch_shapes=[pltpu.VMEM((RING_SLOTS, r, hw), U32), pltpu.VMEM((RING_SLOTS, r, hw), U32),
                        pltpu.VMEM((RING_SLOTS, r, hw), U32), pltpu.VMEM((RING_SLOTS, r, hw), U32),
                        pltpu.SemaphoreType.DMA((2, RING_SLOTS)), pltpu.SemaphoreType.DMA((2, RING_SLOTS))],
    )
    return pl.pallas_call(
        _expert_kernel,
        out_shape=(jax.ShapeDtypeStruct((rows, hw), U32), jax.ShapeDtypeStruct((rows, hw), U32)),
        grid_spec=grid_spec,
        compiler_params=_params("arbitrary"),
        name="experts",
    )(pstart, cnt, total_chunks, wg, wu, wd, xs_a, xs_b)


def _sc_scatter_rows(xs, idx_flat, n_rows):
    n, w = xs[0].shape
    m = idx_flat.shape[1]
    nwin = n // SC_WINDOW
    reps = m // n
    mesh = plsc.VectorSubcoreMesh(core_axis_name="core", subcore_axis_name="subcore")
    out_type = tuple(jax.ShapeDtypeStruct((n_rows, w), x.dtype) for x in xs)

    @functools.partial(pl.kernel, out_type=out_type, mesh=mesh, scratch_types=[])
    def scatter_kernel(*refs):
        x_hbms, i_hbm, o_hbms = refs[:len(xs)], refs[len(xs)], refs[len(xs) + 1:]
        for x_hbm, o_hbm in zip(x_hbms, o_hbms):
            def body(x_vmem, i_vmem, o_hbm=o_hbm):
                pltpu.sync_copy(x_vmem, o_hbm.at[i_vmem.at[0]])

            pltpu.emit_pipeline(
                body,
                grid=(nwin, reps),
                in_specs=[pl.BlockSpec((SC_WINDOW, w), lambda i, k: (i, 0)),
                          pl.BlockSpec((1, SC_WINDOW), lambda i, k: (0, k * nwin + i))],
                out_specs=[],
                core_axis_name=("core", "subcore"),
                dimension_semantics=(pltpu.PARALLEL, pltpu.ARBITRARY),
            )(x_hbm, i_hbm)

    return scatter_kernel(*xs, idx_flat)


def _sc_gather_rows(srcs, idx_flat):
    m = idx_flat.shape[1]
    w = srcs[0].shape[1]
    mesh = plsc.VectorSubcoreMesh(core_axis_name="core", subcore_axis_name="subcore")
    out_type = tuple(jax.ShapeDtypeStruct((m, w), src.dtype) for src in srcs)

    @functools.partial(pl.kernel, out_type=out_type, mesh=mesh)
    def gather_kernel(*refs):
        x_hbms, i_hbm, o_hbms = refs[:len(srcs)], refs[len(srcs)], refs[len(srcs) + 1:]
        for x_hbm, o_hbm in zip(x_hbms, o_hbms):
            def body(i_vmem, o_vmem, x_hbm=x_hbm):
                pltpu.sync_copy(x_hbm.at[i_vmem.at[0]], o_vmem)

            pltpu.emit_pipeline(
                body,
                grid=(m // SC_WINDOW,),
                in_specs=[pl.BlockSpec((1, SC_WINDOW), lambda i: (0, i))],
                out_specs=[pl.BlockSpec((SC_WINDOW, w), lambda i: (i, 0))],
                core_axis_name=("core", "subcore"),
                dimension_semantics=(pltpu.PARALLEL,),
            )(i_hbm, o_hbm)

    return gather_kernel(*srcs, idx_flat)


def _finish_kernel(gate_ref, xres_ref, mod_ref, ya_ref, yb_ref, o_ref, *, d, tiles_per_batch):
    tm = gate_ref.shape[1]
    b = pl.program_id(0) // tiles_per_batch
    gates = gate_ref[...].T
    acc = jnp.zeros((tm, d), F32)
    for k in range(TOP_K):
        packed = jnp.concatenate([ya_ref[k], yb_ref[k]], axis=1)
        acc = acc + _unpack_bf16_pairs(packed) * gates[:, k:k + 1]
    o_ref[...] = xres_ref[...] + mod_ref[pl.ds(b, 1), 5 * d:6 * d] * acc


def _finish(gates, xres, mod, yg_a, yg_b, *, tm, tiles_per_batch):
    n, d = xres.shape
    return pl.pallas_call(
        functools.partial(_finish_kernel, d=d, tiles_per_batch=tiles_per_batch),
        out_shape=jax.ShapeDtypeStruct((n, d), F32),
        grid=(n // tm,),
        in_specs=[pl.BlockSpec((TOP_K, tm), lambda i: (0, i)),
                  pl.BlockSpec((tm, d), lambda i: (i, 0)),
                  pl.BlockSpec(mod.shape, lambda i: (0, 0)),
                  pl.BlockSpec((TOP_K, tm, d // 4), lambda i: (0, i, 0)),
                  pl.BlockSpec((TOP_K, tm, d // 4), lambda i: (0, i, 0))],
        out_specs=pl.BlockSpec((tm, d), lambda i: (i, 0)),
        compiler_params=_params("arbitrary"),
        name="finish",
    )(gates, xres, mod, yg_a, yg_b)


_QA_HEAD_ORDER = (0, 4, 1, 5, 2, 6, 3, 7)


def _reorder_qa_heads(a, axis):
    return jnp.concatenate([lax.slice_in_dim(a, h * HEAD_DIM, (h + 1) * HEAD_DIM, axis=axis)
                            for h in _QA_HEAD_ORDER], axis=axis)


def _rope_tables(s):
    quarter = HEAD_DIM // 4
    t = jnp.arange(s)
    row = (t // GRID_W).astype(F32)
    col = (t % GRID_W).astype(F32)
    freqs = ROPE_BASE ** (-jnp.arange(quarter, dtype=F32) / quarter)
    ar = row[:, None] * freqs[None, :]
    ac = col[:, None] * freqs[None, :]
    cos = jnp.concatenate([jnp.cos(ar), jnp.cos(ar), jnp.cos(ac), jnp.cos(ac)], axis=1)
    sin = jnp.concatenate([-jnp.sin(ar), jnp.sin(ar), -jnp.sin(ac), jnp.sin(ac)], axis=1)
    return jnp.tile(cos, (1, 2)), jnp.tile(sin, (1, 2))


def kernel(x, c, ctx, c_ctx, w_ada, b_ada, norm1, norm2, w_in, q_norm_a, k_norm_a, q_norm_b, k_norm_b, sink_a,
           rpb_b, out_norm_a, out_norm_b, w_out, w_router, router_bias, we_gate, we_up, we_down, ws_gate, ws_up,
           ws_down):
    assert w_ada.shape[0] == 1, "single-layer block"
    bsz, s, d = x.shape
    lc = ctx.shape[1]
    n = bsz * s
    rows = s // GRID_W
    assert d == 1024 and bsz + 1 <= 8 and rows >= NA_KROWS and lc % LANES == 0
    assert s % TOKEN_TILE == 0 and n % SC_WINDOW == 0
    assert (s // A_WINDOW) % A_BLOCKS_PER_STEP == 0 and (s // (NA_QROWS * GRID_W)) % NA_GROUPS_PER_STEP == 0

    c8 = jnp.concatenate([c, c_ctx[None, :], jnp.zeros((8 - bsz - 1, d), F32)], axis=0)
    mod = _ada(c8, w_ada[0], b_ada[0][None, :])

    w = w_in[0]
    aw, akw, bw = A_Q_HEADS * HEAD_DIM, A_KV_HEADS * HEAD_DIM, B_HEADS * HEAD_DIM
    cuts = np.cumsum([0, aw, akw, akw, bw, bw, bw])
    qa_w, ka_w, va_w, qb_w, kb_w, vb_w = [w[:, cuts[i]:cuts[i + 1]] for i in range(6)]
    qa_w = _reorder_qa_heads(qa_w, 1)
    w_ext = jnp.concatenate([qa_w, qb_w, kb_w, vb_w, ka_w, va_w], axis=1).astype(BF16)
    scale = HEAD_DIM ** -0.5 * LOG2E
    pair = lambda g: jnp.tile(g, 2)
    gains = jnp.stack([pair(q_norm_a[0]) * scale, pair(q_norm_b[0]) * scale, pair(k_norm_b[0]), pair(k_norm_a[0])]
                      + [jnp.zeros((LANES,), F32)] * 4)
    bd = jnp.asarray(np.kron(np.eye(2), np.full((HEAD_DIM, HEAD_DIM), 1.0 / HEAD_DIM)), BF16)
    cos_t, sin_t = _rope_tables(s)
    n1 = norm1[0][None, :]
    zq = _inproj(x, mod, n1, w_ext, gains, cos_t, sin_t, bd, mod_row=None, tm=TOKEN_TILE)
    zc = _inproj(ctx, mod, n1, w_ext, gains, jnp.ones((lc, LANES), F32), jnp.zeros((lc, LANES), F32), bd,
                 mod_row=bsz, tm=lc)

    sink = sink_a[0].astype(F32)
    sink_rep = jnp.concatenate([jnp.broadcast_to(sink[h] * LOG2E, (A_WINDOW, LANES)) for h in _QA_HEAD_ORDER])
    o_a = _attn_a(zq, zc, sink_rep, _attn_a_mask(A_WINDOW))
    o_b = _attn_b(zq, zc, _na_table(rpb_b[0], rows))

    ga = _reorder_qa_heads(out_norm_a[0], 0)[None, :]
    gb = out_norm_b[0][None, :]
    wo = jnp.concatenate([_reorder_qa_heads(w_out[0][:aw], 0), w_out[0][aw:]], axis=0).astype(BF16)
    wr_hi, wr_lo = _split_bf16(w_router[0].T)
    xres, hp_a, hp_b, idx, gates = _merge(o_a, o_b, x, mod, ga, gb, wo, norm2[0][None, :], wr_hi, wr_lo,
                                  router_bias[0][:, None], ws_gate[0].astype(BF16), ws_up[0].astype(BF16),
                                  ws_down[0].astype(BF16), tm=TOKEN_TILE)

    rank, counts = _rank(idx, tm=TOKEN_TILE)
    t = EXPERT_ROWS
    cnt = counts[:, 0].astype(I32)
    padded = (cnt + t - 1) // t * t
    pends = jnp.cumsum(padded)
    pstart = pends - padded
    n_rows = n * TOP_K + N_EXPERTS * t
    pos = _pos(idx, rank, pstart.astype(F32)[:, None], tm=min(POS_TILE, n))

    pos_flat = pos.reshape(1, TOP_K * n)
    xs_a, xs_b = _sc_scatter_rows((hp_a, hp_b), pos_flat, n_rows)
    ys_a, ys_b = _experts(pstart, cnt, pends[-1:] // t, xs_a, xs_b, we_gate[0], we_up[0], we_down[0])
    yg_a, yg_b = [y.reshape(TOP_K, n, d // 4) for y in _sc_gather_rows((ys_a, ys_b), pos_flat)]
    out = _finish(gates, xres.reshape(n, d), mod, yg_a, yg_b, tm=TOKEN_TILE, tiles_per_batch=s // TOKEN_TILE)
    return out.reshape(bsz, s, d)
```

```python
import functools

import numpy as np
import jax
import jax.numpy as jnp
from jax import lax
from jax.experimental import pallas as pl
from jax.experimental.pallas import tpu as pltpu
from jax.experimental.pallas import tpu_sc as plsc

F32 = jnp.float32
BF16 = jnp.bfloat16
I32 = jnp.int32
U32 = jnp.uint32

LANES = 128
HEAD_DIM = 64
HEAD_PAIR = 2 * HEAD_DIM
GRID_W = 64
A_Q_HEADS = 8
A_KV_HEADS = 2
A_WINDOW = 128
B_HEADS = 8
NA_KH = 8
NA_KW = 16
NA_QROWS = 4
NA_KROWS = NA_QROWS + NA_KH
ROPE_BASE = 10000.0
N_EXPERTS = 256
TOP_K = 8
N_GROUPS = 8
TOPK_GROUPS = 4
ROUTED_SCALE = 2.5
LOG2E = 1.4426950408889634
EPS = 1e-6
NEG_INF = -1e30
TOKEN_TILE = 512
POS_TILE = 2048
EXPERT_ROWS = 256
SC_WINDOW = 128
VMEM_LIMIT = 56 * 1024 * 1024

_NT = (((1,), (1,)), ((), ()))


def _params(*sem):
    return pltpu.CompilerParams(dimension_semantics=sem, vmem_limit_bytes=VMEM_LIMIT)


def _silu(v):
    return v * jax.nn.sigmoid(v)


def _rms(v, gain):
    return v * lax.rsqrt(jnp.mean(v * v, axis=-1, keepdims=True) + EPS) * gain


def _pack_bf16_pairs(v):
    n = v.shape[1] // 2
    lo = lax.bitcast_convert_type(v[:, :n].astype(BF16).astype(F32), U32) >> 16
    hi = lax.bitcast_convert_type(v[:, n:].astype(BF16).astype(F32), U32) & jnp.uint32(0xFFFF0000)
    return hi | lo


def _split_bf16(v):
    hi = lax.bitcast_convert_type(lax.bitcast_convert_type(v, U32) & jnp.uint32(0xFFFF0000), F32)
    return hi.astype(BF16), (v - hi).astype(BF16)


def _unpack_bf16_pairs(w):
    lo = lax.bitcast_convert_type(w << 16, F32)
    hi = lax.bitcast_convert_type(w & jnp.uint32(0xFFFF0000), F32)
    return jnp.concatenate([lo, hi], axis=1)


def _ada_kernel(c_ref, w_ref, b_ref, o_ref):
    a = _silu(c_ref[...])
    o_ref[...] = jnp.dot(a, w_ref[...], preferred_element_type=F32,
                         precision=lax.Precision.HIGHEST) + b_ref[...]


def _ada(c8, w, b):
    d, n = w.shape
    bn = n // 4
    return pl.pallas_call(
        _ada_kernel,
        out_shape=jax.ShapeDtypeStruct((8, n), F32),
        grid=(n // bn,),
        in_specs=[pl.BlockSpec((8, d), lambda j: (0, 0)),
                  pl.BlockSpec((d, bn), lambda j: (0, j)),
                  pl.BlockSpec((1, bn), lambda j: (0, j))],
        out_specs=pl.BlockSpec((8, bn), lambda j: (0, j)),
        compiler_params=_params("arbitrary"),
        name="ada",
    )(c8, w, b)


_QA, _QB, _KB, _VB, _KA, _VA = 0, 4, 8, 12, 16, 17
_OUT_BLOCKS = 18


def _inproj_kernel(x_ref, mod_ref, n1_ref, w_ref, g_ref, cos_ref, sin_ref, bd_ref, o_ref, *, mod_row, d):
    b = pl.program_id(0) if mod_row is None else mod_row
    xn = _rms(x_ref[0], n1_ref[...])
    sh = mod_ref[pl.ds(b, 1), 0:d]
    sc = mod_ref[pl.ds(b, 1), d:2 * d]
    h = (xn * (1.0 + sc) + sh).astype(BF16)
    z = jnp.dot(h, w_ref[...], preferred_element_type=F32)
    bd = bd_ref[...]
    cos = cos_ref[...]
    sin = sin_ref[...]

    def blk(j):
        return z[:, j * LANES:(j + 1) * LANES]

    def head_rinv(zb):
        ms = jnp.dot((zb * zb).astype(BF16), bd, preferred_element_type=F32)
        return lax.rsqrt(ms + EPS)

    def put(j, v):
        o_ref[0, :, j * LANES:(j + 1) * LANES] = v.astype(BF16)

    quarter = HEAD_DIM // 4
    first_half = (lax.broadcasted_iota(I32, (1, LANES), 1) % (2 * quarter)) < quarter

    def roped(j, g_row):
        zb = blk(j)
        zn = zb * head_rinv(zb) * g_ref[g_row:g_row + 1, :]
        partner = jnp.where(first_half, pltpu.roll(zn, LANES - quarter, 1), pltpu.roll(zn, quarter, 1))
        put(j, zn * cos + partner * sin)

    def normed(j, g_row):
        zb = blk(j)
        put(j, zb * head_rinv(zb) * g_ref[g_row:g_row + 1, :])

    for j in range(4):
        roped(_QA + j, 0)
        normed(_QB + j, 1)
        normed(_KB + j, 2)
        put(_VB + j, blk(_VB + j))
    roped(_KA, 3)
    put(_VA, blk(_VA))


def _inproj(x, mod, n1, w_ext, gains, cos_t, sin_t, bd, *, mod_row, tm):
    bsz, s, d = x.shape
    kern = functools.partial(_inproj_kernel, mod_row=mod_row, d=d)
    return pl.pallas_call(
        kern,
        out_shape=jax.ShapeDtypeStruct((bsz, s, _OUT_BLOCKS * LANES), BF16),
        grid=(bsz, s // tm),
        in_specs=[pl.BlockSpec((1, tm, d), lambda b, i: (b, i, 0)),
                  pl.BlockSpec(mod.shape, lambda b, i: (0, 0)),
                  pl.BlockSpec((1, d), lambda b, i: (0, 0)),
                  pl.BlockSpec(w_ext.shape, lambda b, i: (0, 0)),
                  pl.BlockSpec(gains.shape, lambda b, i: (0, 0)),
                  pl.BlockSpec((tm, LANES), lambda b, i: (i, 0)),
                  pl.BlockSpec((tm, LANES), lambda b, i: (i, 0)),
                  pl.BlockSpec(bd.shape, lambda b, i: (0, 0))],
        out_specs=pl.BlockSpec((1, tm, _OUT_BLOCKS * LANES), lambda b, i: (b, i, 0)),
        compiler_params=_params("arbitrary", "arbitrary"),
        name="inproj",
    )(x, mod, n1, w_ext, gains, cos_t, sin_t, bd)


def _split_pair(qp, lo):
    zero = jnp.zeros_like(qp)
    return jnp.concatenate([jnp.where(lo, qp, zero), jnp.where(lo, zero, qp)], axis=0)


def _softmax_pv(s_parts, v_parts, sink_rep):
    chunks = [s[:, c * LANES:(c + 1) * LANES] for s in s_parts for c in range(s.shape[1] // LANES)]
    m = jnp.max(functools.reduce(jnp.maximum, chunks), axis=-1, keepdims=True)
    m_rep = jnp.broadcast_to(m, (m.shape[0], LANES))
    if sink_rep is not None:
        m_rep = jnp.maximum(m_rep, sink_rep)
    acc = None
    for s, v in zip(s_parts, v_parts):
        p = jnp.concatenate([jnp.exp2(s[:, c * LANES:(c + 1) * LANES] - m_rep)
                             for c in range(s.shape[1] // LANES)], axis=1).astype(BF16)
        v_ext = jnp.concatenate([v, jnp.ones_like(v)], axis=1)
        o = jnp.dot(p, v_ext, preferred_element_type=F32)
        acc = o if acc is None else acc + o
    l_rep = acc[:, LANES:]
    if sink_rep is not None:
        l_rep = l_rep + jnp.exp2(sink_rep - m_rep)
    return acc[:, :LANES] * (1.0 / l_rep)


A_BLOCKS_PER_STEP = 8


def _attn_a_kernel(q_ref, *refs):
    nq = A_BLOCKS_PER_STEP
    k_refs, v_refs = refs[:nq + 2], refs[nq + 2:2 * nq + 4]
    kx_ref, vx_ref, sink_ref = refs[2 * nq + 4:2 * nq + 7]
    mask_refs = refs[2 * nq + 7:3 * nq + 7]
    o_ref = refs[-1]
    tq = A_WINDOW
    lo = lax.broadcasted_iota(I32, (tq, LANES), 1) < HEAD_DIM
    for h in range(nq):
        q = q_ref[0, h * tq:(h + 1) * tq]
        qs = jnp.concatenate([_split_pair(q[:, p * LANES:(p + 1) * LANES], lo) for p in range(4)], axis=0)
        k = jnp.concatenate([r[0] for r in k_refs[h:h + 3]], axis=0)
        v = jnp.concatenate([r[0] for r in v_refs[h:h + 3]], axis=0)
        s_loc = lax.dot_general(qs, k, _NT, preferred_element_type=F32) + mask_refs[h][0]
        s_ctx = lax.dot_general(qs, kx_ref[0], _NT, preferred_element_type=F32)
        o = _softmax_pv([s_loc, s_ctx], [v, vx_ref[0]], sink_ref[...])
        for p in range(4):
            o_lo = o[(2 * p) * tq:(2 * p + 1) * tq]
            o_hi = o[(2 * p + 1) * tq:(2 * p + 2) * tq]
            o_ref[0, h * tq:(h + 1) * tq, p * LANES:(p + 1) * LANES] = jnp.where(lo, o_lo, o_hi).astype(BF16)


def _attn_a(zq, zc, sink_rep, mask):
    bsz, s, _ = zq.shape
    lc = zc.shape[1]
    tq = A_WINDOW
    nq = A_BLOCKS_PER_STEP
    nblk = s // tq
    ka, va = _KA, _VA

    def kv_spec(col, shift):
        return pl.BlockSpec((1, tq, LANES), lambda b, j: (b, jnp.clip(nq * j + shift, 0, nblk - 1), col))

    def mask_spec(h):
        def variant(j):
            blk = nq * j + h
            return jnp.where(blk == 0, 0, jnp.where(blk == nblk - 1, 2, 1))
        return pl.BlockSpec((1,) + mask.shape[1:], lambda b, j: (variant(j), 0, 0))

    shifts = range(-1, nq + 1)
    return pl.pallas_call(
        _attn_a_kernel,
        out_shape=jax.ShapeDtypeStruct((bsz, s, 4 * LANES), BF16),
        grid=(bsz, nblk // nq),
        in_specs=([pl.BlockSpec((1, nq * tq, 4 * LANES), lambda b, j: (b, j, 0))]
                  + [kv_spec(ka, sh) for sh in shifts] + [kv_spec(va, sh) for sh in shifts]
                  + [pl.BlockSpec((1, lc, LANES), lambda b, j: (b, 0, ka)),
                     pl.BlockSpec((1, lc, LANES), lambda b, j: (b, 0, va)),
                     pl.BlockSpec(sink_rep.shape, lambda b, j: (0, 0))]
                  + [mask_spec(h) for h in range(nq)]),
        out_specs=pl.BlockSpec((1, nq * tq, 4 * LANES), lambda b, j: (b, j, 0)),
        compiler_params=_params("arbitrary", "arbitrary"),
        name="attn_a",
    )(*([zq] * (2 * nq + 5) + [zc, zc, sink_rep] + [mask] * nq))


def _attn_a_mask(tq):
    qi = np.arange(tq)[:, None]
    kj = np.arange(3 * tq)[None, :]
    ok = (kj >= qi) & (kj <= qi + 2 * tq)
    variants = [ok & (kj >= tq), ok, ok & (kj < 2 * tq)]
    m = np.stack([np.tile(np.where(v, 0.0, NEG_INF).astype(np.float32), (A_Q_HEADS, 1)) for v in variants])
    return jnp.asarray(m)


NA_GROUPS_PER_STEP = 4


def _attn_b_kernel(q_ref, *refs):
    ngs = NA_GROUPS_PER_STEP
    nkv = ngs + 2
    k_refs, v_refs = refs[:nkv], refs[nkv:2 * nkv]
    kx_ref, vx_ref = refs[2 * nkv:2 * nkv + 2]
    tab_refs = refs[2 * nkv + 2:2 * nkv + 2 + ngs]
    o_ref = refs[-1]
    tq = NA_QROWS * GRID_W
    lo = lax.broadcasted_iota(I32, (tq, LANES), 1) < HEAD_DIM
    for g in range(ngs):
        rows_g = slice(g * tq, (g + 1) * tq)
        for p in range(4):
            sl = slice(p * LANES, (p + 1) * LANES)
            qs = _split_pair(q_ref[0, rows_g, sl], lo)
            k = jnp.concatenate([r[0, :, sl] for r in k_refs[g:g + 3]], axis=0)
            v = jnp.concatenate([r[0, :, sl] for r in v_refs[g:g + 3]], axis=0)
            s_loc = lax.dot_general(qs, k, _NT, preferred_element_type=F32) + tab_refs[g][0, p]
            s_ctx = lax.dot_general(qs, kx_ref[0, :, sl], _NT, preferred_element_type=F32)
            o = _softmax_pv([s_loc, s_ctx], [v, vx_ref[0, :, sl]], None)
            o_ref[0, rows_g, sl] = jnp.where(lo, o[:tq], o[tq:]).astype(BF16)


def _attn_b(zq, zc, table):
    bsz, s, _ = zq.shape
    lc = zc.shape[1]
    tq = NA_QROWS * GRID_W
    ngs = NA_GROUPS_PER_STEP
    ng = s // tq
    qb, kb, vb = _QB // 4, _KB // 4, _VB // 4

    def kv_spec(col, j):
        return pl.BlockSpec((1, tq, 4 * LANES), lambda i, b: (b, jnp.clip(ngs * i + j - 1, 0, ng - 1), col))

    def tab_spec(g):
        def variant(i):
            grp = ngs * i + g
            return jnp.where(grp == 0, 0, jnp.where(grp == ng - 1, 2, 1))
        return pl.BlockSpec((1,) + table.shape[1:], lambda i, b: (variant(i), 0, 0, 0),
                            pipeline_mode=pl.Buffered(1))

    kv_slots = range(ngs + 2)
    return pl.pallas_call(
        _attn_b_kernel,
        out_shape=jax.ShapeDtypeStruct((bsz, s, 4 * LANES), BF16),
        grid=(ng // ngs, bsz),
        in_specs=([pl.BlockSpec((1, ngs * tq, 4 * LANES), lambda i, b: (b, i, qb))]
                  + [kv_spec(kb, j) for j in kv_slots] + [kv_spec(vb, j) for j in kv_slots]
                  + [pl.BlockSpec((1, lc, 4 * LANES), lambda i, b: (b, 0, kb)),
                     pl.BlockSpec((1, lc, 4 * LANES), lambda i, b: (b, 0, vb))]
                  + [tab_spec(g) for g in range(ngs)]),
        out_specs=pl.BlockSpec((1, ngs * tq, 4 * LANES), lambda i, b: (b, i, 0)),
        compiler_params=_params("arbitrary", "arbitrary"),
        name="attn_b",
    )(*([zq] * (2 * ngs + 5) + [zc, zc] + [table] * ngs))


def _na_table(rpb, rows):
    ng = rows // NA_QROWS
    nq, nk = NA_QROWS * GRID_W, NA_KROWS * GRID_W
    qc = np.arange(GRID_W)
    kc = np.arange(GRID_W)
    ws = np.clip(qc - NA_KW // 2, 0, GRID_W - NA_KW)
    valid_c = (kc[None, :] >= ws[:, None]) & (kc[None, :] < ws[:, None] + NA_KW)
    dc = np.clip(kc[None, :] - qc[:, None], -(NA_KW - 1), NA_KW - 1) + (NA_KW - 1)
    c_sel = (dc[..., None] == np.arange(2 * NA_KW - 1)) & valid_c[..., None]
    tiles = jnp.einsum('hab,uvb->huav', rpb.astype(F32), jnp.asarray(c_sel, F32), precision=lax.Precision.HIGHEST)
    tiles = jnp.where(jnp.asarray(valid_c)[None, :, None, :], tiles * LOG2E, NEG_INF)
    tiles = tiles.reshape(B_HEADS, GRID_W, (2 * NA_KH - 1) * GRID_W)

    def masked(n_key_rows):
        return jnp.full((B_HEADS, GRID_W, n_key_rows * GRID_W), NEG_INF, F32)

    tabs = []
    for i in (0, 1, ng - 1):
        start = NA_QROWS * i - NA_KH // 2
        q_rows = []
        for qr in range(NA_QROWS):
            r = NA_QROWS * i + qr
            rs = int(np.clip(r - NA_KH // 2, 0, rows - NA_KH))
            y0 = rs - start
            d0 = rs - r + (NA_KH - 1)
            q_rows.append(jnp.concatenate([masked(y0), tiles[:, :, d0 * GRID_W:(d0 + NA_KH) * GRID_W],
                                           masked(NA_KROWS - NA_KH - y0)], axis=-1))
        tabs.append(jnp.concatenate(q_rows, axis=-2))
    return jnp.stack(tabs).reshape(3, B_HEADS // 2, 2 * nq, nk)


def _merge_kernel(oa_ref, ob_ref, x_ref, mod_ref, ga_ref, gb_ref, wo_ref, n2_ref, wrh_ref, wrl_ref, rb_ref,
                  wsg_ref, wsu_ref, wsd_ref, xres_ref, hpa_ref, hpb_ref, idx_ref, gate_ref, *, d):
    b = pl.program_id(0)
    tm = x_ref.shape[1]

    def mod(k):
        return mod_ref[pl.ds(b, 1), k * d:(k + 1) * d]

    na = _rms(oa_ref[0].astype(F32), ga_ref[...])
    nb = _rms(ob_ref[0].astype(F32), gb_ref[...])
    cat = jnp.concatenate([na, nb], axis=1).astype(BF16)
    y = jnp.dot(cat, wo_ref[...], preferred_element_type=F32)
    x1 = x_ref[0] + mod(2) * y
    h2 = _rms(x1, n2_ref[...]) * (1.0 + mod(4)) + mod(3)

    hb = h2.astype(BF16)
    act = _silu(jnp.dot(hb, wsg_ref[...], preferred_element_type=F32)) * jnp.dot(hb, wsu_ref[...],
                                                                                 preferred_element_type=F32)
    shared = jnp.dot(act.astype(BF16), wsd_ref[...], preferred_element_type=F32)
    xres_ref[0] = x1 + mod(5) * shared
    packed = _pack_bf16_pairs(h2)
    hpa_ref[...] = packed[:, :d // 4]
    hpb_ref[...] = packed[:, d // 4:]

    h_hi, h_lo = _split_bf16(h2)
    logits = (lax.dot_general(wrh_ref[...], h_hi, _NT, preferred_element_type=F32)
              + lax.dot_general(wrh_ref[...], h_lo, _NT, preferred_element_type=F32)
              + lax.dot_general(wrl_ref[...], h_hi, _NT, preferred_element_type=F32))
    scores = jax.nn.sigmoid(logits)
    sel = scores + rb_ref[...]
    per = N_EXPERTS // N_GROUPS
    g3 = sel.reshape(N_GROUPS, per, tm)
    it3 = lax.broadcasted_iota(I32, (N_GROUPS, per, tm), 1)
    m1 = jnp.max(g3, axis=1, keepdims=True)
    first = jnp.min(jnp.where(g3 == m1, it3, per), axis=1, keepdims=True)
    m2 = jnp.max(jnp.where(it3 == first, -jnp.inf, g3), axis=1, keepdims=True)
    gscore = (m1 + m2).reshape(N_GROUPS, tm)

    itg = lax.broadcasted_iota(I32, (N_GROUPS, tm), 0)
    gsel = jnp.zeros((N_GROUPS, tm), F32)
    cur = gscore
    for _ in range(TOPK_GROUPS):
        mx = jnp.max(cur, axis=0, keepdims=True)
        fi = jnp.min(jnp.where(cur == mx, itg, N_GROUPS), axis=0, keepdims=True)
        pick = itg == fi
        gsel = jnp.where(pick, 1.0, gsel)
        cur = jnp.where(pick, -jnp.inf, cur)
    emask = jnp.broadcast_to(gsel.reshape(N_GROUPS, 1, tm), (N_GROUPS, per, tm)).reshape(N_EXPERTS, tm) > 0.5

    ite = lax.broadcasted_iota(I32, (N_EXPERTS, tm), 0)
    cur = jnp.where(emask, sel, NEG_INF)
    idx_rows, s_rows = [], []
    for _ in range(TOP_K):
        mx = jnp.max(cur, axis=0, keepdims=True)
        fi = jnp.min(jnp.where(cur == mx, ite, N_EXPERTS), axis=0, keepdims=True)
        pick = ite == fi
        idx_rows.append(fi)
        s_rows.append(jnp.sum(jnp.where(pick, scores, 0.0), axis=0, keepdims=True))
        cur = jnp.where(pick, -jnp.inf, cur)
    top_s = jnp.concatenate(s_rows, axis=0)
    idx_ref[...] = jnp.concatenate(idx_rows, axis=0)
    gate_ref[...] = top_s / jnp.sum(top_s, axis=0, keepdims=True) * ROUTED_SCALE


def _merge(o_a, o_b, x, mod, ga, gb, wo, n2, wr_hi, wr_lo, rb_col, wsg, wsu, wsd, *, tm):
    bsz, s, d = x.shape
    nt = s // tm
    n = bsz * s
    full = lambda a: pl.BlockSpec(a.shape, lambda b, i: (0,) * a.ndim)
    return pl.pallas_call(
        functools.partial(_merge_kernel, d=d),
        out_shape=(jax.ShapeDtypeStruct((bsz, s, d), F32),
                   jax.ShapeDtypeStruct((n, d // 4), U32),
                   jax.ShapeDtypeStruct((n, d // 4), U32),
                   jax.ShapeDtypeStruct((TOP_K, n), I32),
                   jax.ShapeDtypeStruct((TOP_K, n), F32)),
        grid=(bsz, nt),
        in_specs=[pl.BlockSpec((1, tm, d // 2), lambda b, i: (b, i, 0)),
                  pl.BlockSpec((1, tm, d // 2), lambda b, i: (b, i, 0)),
                  pl.BlockSpec((1, tm, d), lambda b, i: (b, i, 0)),
                  full(mod), full(ga), full(gb), full(wo), full(n2), full(wr_hi), full(wr_lo), full(rb_col),
                  full(wsg), full(wsu), full(wsd)],
        out_specs=(pl.BlockSpec((1, tm, d), lambda b, i: (b, i, 0)),
                   pl.BlockSpec((tm, d // 4), lambda b, i: (b * nt + i, 0)),
                   pl.BlockSpec((tm, d // 4), lambda b, i: (b * nt + i, 0)),
                   pl.BlockSpec((TOP_K, tm), lambda b, i: (0, b * nt + i)),
                   pl.BlockSpec((TOP_K, tm), lambda b, i: (0, b * nt + i))),
        compiler_params=_params("arbitrary", "arbitrary"),
        name="merge",
    )(o_a, o_b, x, mod, ga, gb, wo, n2, wr_hi, wr_lo, rb_col, wsg, wsu, wsd)


def _rank_kernel(idx_ref, rank_ref, cnt_ref, carry_ref):
    tm = idx_ref.shape[1]

    @pl.when(pl.program_id(0) == 0)
    def _():
        carry_ref[...] = jnp.zeros_like(carry_ref)

    idx = idx_ref[...]
    ite = lax.broadcasted_iota(I32, (N_EXPERTS, tm), 0)
    before = (lax.broadcasted_iota(I32, (tm, tm), 0) < lax.broadcasted_iota(I32, (tm, tm), 1)).astype(BF16)
    hits = [ite == idx[k:k + 1, :] for k in range(TOP_K)]
    routed = jnp.where(functools.reduce(jnp.logical_or, hits), 1.0, 0.0)
    base = carry_ref[...]
    ahead = jnp.dot(routed.astype(BF16), before, preferred_element_type=F32) + base
    rows = [jnp.sum(jnp.where(hit, ahead, 0.0), axis=0, keepdims=True) for hit in hits]
    rank_ref[...] = jnp.concatenate(rows, axis=0).astype(I32)
    total = base + jnp.sum(routed, axis=1, keepdims=True)
    carry_ref[...] = total
    cnt_ref[...] = total


def _rank(idx, *, tm):
    n = idx.shape[1]
    return pl.pallas_call(
        _rank_kernel,
        out_shape=(jax.ShapeDtypeStruct((TOP_K, n), I32), jax.ShapeDtypeStruct((N_EXPERTS, 1), F32)),
        grid=(n // tm,),
        in_specs=[pl.BlockSpec((TOP_K, tm), lambda i: (0, i))],
        out_specs=(pl.BlockSpec((TOP_K, tm), lambda i: (0, i)), pl.BlockSpec((N_EXPERTS, 1), lambda i: (0, 0))),
        scratch_shapes=[pltpu.VMEM((N_EXPERTS, 1), F32)],
        compiler_params=_params("arbitrary"),
        name="rank",
    )(idx)


def _pos_kernel(idx_ref, rank_ref, pstart_ref, pos_ref):
    tm = idx_ref.shape[1]
    idx = idx_ref[...]
    ite = lax.broadcasted_iota(I32, (N_EXPERTS, tm), 0)
    pstart = pstart_ref[...]
    rows = [jnp.sum(jnp.where(ite == idx[k:k + 1, :], pstart, 0.0), axis=0, keepdims=True) for k in range(TOP_K)]
    pos_ref[...] = jnp.concatenate(rows, axis=0).astype(I32) + rank_ref[...]


def _pos(idx, rank, pstart_col, *, tm):
    n = idx.shape[1]
    return pl.pallas_call(
        _pos_kernel,
        out_shape=jax.ShapeDtypeStruct((TOP_K, n), I32),
        grid=(n // tm,),
        in_specs=[pl.BlockSpec((TOP_K, tm), lambda i: (0, i)),
                  pl.BlockSpec((TOP_K, tm), lambda i: (0, i)),
                  pl.BlockSpec((N_EXPERTS, 1), lambda i: (0, 0))],
        out_specs=pl.BlockSpec((TOP_K, tm), lambda i: (0, i)),
        compiler_params=_params("arbitrary"),
        name="pos",
    )(idx, rank, pstart_col)


UNIT_CHUNKS = (4, 2, 1)
RING_AHEAD = 4
RING_SLOTS = RING_AHEAD + UNIT_CHUNKS[0]


def _expert_kernel(ps_ref, cnt_ref, tot_ref, wg_ref, wu_ref, wd_ref, xa_hbm, xb_hbm, ya_hbm, yb_hbm,
                   xa_buf, xb_buf, ya_buf, yb_buf, in_sem, out_sem):
    e = pl.program_id(0)
    r = EXPERT_ROWS
    ahead = RING_AHEAD
    total = tot_ref[0]
    first = ps_ref[e] // r
    cnt = cnt_ref[e]
    nch = (cnt + r - 1) // r

    def rows_of(g):
        return pl.ds(pl.multiple_of(g * r, r), r)

    def fetch(g):
        slot = g % RING_SLOTS
        return (pltpu.make_async_copy(xa_hbm.at[rows_of(g)], xa_buf.at[slot], in_sem.at[0, slot]),
                pltpu.make_async_copy(xb_hbm.at[rows_of(g)], xb_buf.at[slot], in_sem.at[1, slot]))

    def flush(g):
        slot = g % RING_SLOTS
        return (pltpu.make_async_copy(ya_buf.at[slot], ya_hbm.at[rows_of(g)], out_sem.at[0, slot]),
                pltpu.make_async_copy(yb_buf.at[slot], yb_hbm.at[rows_of(g)], out_sem.at[1, slot]))

    def start(copies):
        for cp in copies:
            cp.start()

    def wait(copies):
        for cp in copies:
            cp.wait()

    @pl.when(e == 0)
    def _():
        for g in range(ahead):
            @pl.when(g < total)
            def _():
                start(fetch(g))

    @pl.when(nch > 0)
    def _():
        def unit(c0, u):
            for j in range(u):
                g = first + c0 + j
                wait(fetch(g))

                @pl.when(g + ahead < total)
                def _():
                    start(fetch(g + ahead))

                @pl.when(g >= RING_SLOTS)
                def _():
                    wait(flush(g - RING_SLOTS))

            slots = [(first + c0 + j) % RING_SLOTS for j in range(u)]
            packed = jnp.concatenate([jnp.concatenate([xa_buf[sl], xb_buf[sl]], axis=1) for sl in slots], axis=0)
            row = lax.broadcasted_iota(I32, packed.shape, 0)
            packed = jnp.where(row < cnt - c0 * r, packed, jnp.zeros_like(packed))
            x = _unpack_bf16_pairs(packed).astype(BF16)
            gate = jnp.dot(x, wg_ref[0].astype(BF16), preferred_element_type=F32)
            up = jnp.dot(x, wu_ref[0].astype(BF16), preferred_element_type=F32)
            y = jnp.dot((_silu(gate) * up).astype(BF16), wd_ref[0].astype(BF16), preferred_element_type=F32)
            out = _pack_bf16_pairs(y)
            half = out.shape[1] // 2
            for j, sl in enumerate(slots):
                ya_buf[sl] = out[j * r:(j + 1) * r, :half]
                yb_buf[sl] = out[j * r:(j + 1) * r, half:]
                start(flush(first + c0 + j))

        big = UNIT_CHUNKS[0]

        def big_unit(i, carry):
            unit(i * big, big)
            return carry

        lax.fori_loop(0, nch // big, big_unit, 0)
        done = nch // big * big
        for u in UNIT_CHUNKS[1:]:
            @pl.when((nch % (2 * u)) >= u)
            def _():
                unit(done, u)

            done = done + jnp.where((nch % (2 * u)) >= u, u, 0)

    @pl.when(e == pl.num_programs(0) - 1)
    def _():
        for k in range(RING_SLOTS):
            @pl.when(total - 1 - k >= 0)
            def _():
                wait(flush(total - 1 - k))


def _experts(pstart, cnt, total_chunks, xs_a, xs_b, wg, wu, wd):
    rows, hw = xs_a.shape
    r = EXPERT_ROWS
    n_exp, d, f = wg.shape
    hbm = pl.BlockSpec(memory_space=pl.ANY)
    grid_spec = pltpu.PrefetchScalarGridSpec(
        num_scalar_prefetch=3,
        grid=(n_exp,),
        in_specs=[pl.BlockSpec((1, d, f), lambda e, ps, cn, tot: (e, 0, 0)),
                  pl.BlockSpec((1, d, f), lambda e, ps, cn, tot: (e, 0, 0)),
                  pl.BlockSpec((1, f, d), lambda e, ps, cn, tot: (e, 0, 0)),
                  hbm, hbm],
        out_specs=(hbm, hbm),
        scratch_shapes=[pltpu.VMEM((RING_SLOTS, r, hw), U32), pltpu.VMEM((RING_SLOTS, r, hw), U32),
                        pltpu.VMEM((RING_SLOTS, r, hw), U32), pltpu.VMEM((RING_SLOTS, r, hw), U32),
                        pltpu.SemaphoreType.DMA((2, RING_SLOTS)), pltpu.SemaphoreType.DMA((2, RING_SLOTS))],
    )
    return pl.pallas_call(
        _expert_kernel,
        out_shape=(jax.ShapeDtypeStruct((rows, hw), U32), jax.ShapeDtypeStruct((rows, hw), U32)),
        grid_spec=grid_spec,
        compiler_params=_params("arbitrary"),
        name="experts",
    )(pstart, cnt, total_chunks, wg, wu, wd, xs_a, xs_b)


def _sc_scatter_rows(xs, idx_flat, n_rows):
    n, w = xs[0].shape
    m = idx_flat.shape[1]
    nwin = n // SC_WINDOW
    reps = m // n
    mesh = plsc.VectorSubcoreMesh(core_axis_name="core", subcore_axis_name="subcore")
    out_type = tuple(jax.ShapeDtypeStruct((n_rows, w), x.dtype) for x in xs)

    @functools.partial(pl.kernel, out_type=out_type, mesh=mesh, scratch_types=[])
    def scatter_kernel(*refs):
        x_hbms, i_hbm, o_hbms = refs[:len(xs)], refs[len(xs)], refs[len(xs) + 1:]
        for x_hbm, o_hbm in zip(x_hbms, o_hbms):
            def body(x_vmem, i_vmem, o_hbm=o_hbm):
                pltpu.sync_copy(x_vmem, o_hbm.at[i_vmem.at[0]])

            pltpu.emit_pipeline(
                body,
                grid=(nwin, reps),
                in_specs=[pl.BlockSpec((SC_WINDOW, w), lambda i, k: (i, 0)),
                          pl.BlockSpec((1, SC_WINDOW), lambda i, k: (0, k * nwin + i))],
                out_specs=[],
                core_axis_name=("core", "subcore"),
                dimension_semantics=(pltpu.PARALLEL, pltpu.ARBITRARY),
            )(x_hbm, i_hbm)

    return scatter_kernel(*xs, idx_flat)


def _sc_gather_rows(srcs, idx_flat):
    m = idx_flat.shape[1]
    w = srcs[0].shape[1]
    mesh = plsc.VectorSubcoreMesh(core_axis_name="core", subcore_axis_name="subcore")
    out_type = tuple(jax.ShapeDtypeStruct((m, w), src.dtype) for src in srcs)

    @functools.partial(pl.kernel, out_type=out_type, mesh=mesh)
    def gather_kernel(*refs):
        x_hbms, i_hbm, o_hbms = refs[:len(srcs)], refs[len(srcs)], refs[len(srcs) + 1:]
        for x_hbm, o_hbm in zip(x_hbms, o_hbms):
            def body(i_vmem, o_vmem, x_hbm=x_hbm):
                pltpu.sync_copy(x_hbm.at[i_vmem.at[0]], o_vmem)

            pltpu.emit_pipeline(
                body,
                grid=(m // SC_WINDOW,),
                in_specs=[pl.BlockSpec((1, SC_WINDOW), lambda i: (0, i))],
                out_specs=[pl.BlockSpec((SC_WINDOW, w), lambda i: (i, 0))],
                core_axis_name=("core", "subcore"),
                dimension_semantics=(pltpu.PARALLEL,),
            )(i_hbm, o_hbm)

    return gather_kernel(*srcs, idx_flat)


def _finish_kernel(gate_ref, xres_ref, mod_ref, ya_ref, yb_ref, o_ref, *, d, tiles_per_batch):
    tm = gate_ref.shape[1]
    b = pl.program_id(0) // tiles_per_batch
    gates = gate_ref[...].T
    acc = jnp.zeros((tm, d), F32)
    for k in range(TOP_K):
        packed = jnp.concatenate([ya_ref[k], yb_ref[k]], axis=1)
        acc = acc + _unpack_bf16_pairs(packed) * gates[:, k:k + 1]
    o_ref[...] = xres_ref[...] + mod_ref[pl.ds(b, 1), 5 * d:6 * d] * acc


def _finish(gates, xres, mod, yg_a, yg_b, *, tm, tiles_per_batch):
    n, d = xres.shape
    return pl.pallas_call(
        functools.partial(_finish_kernel, d=d, tiles_per_batch=tiles_per_batch),
        out_shape=jax.ShapeDtypeStruct((n, d), F32),
        grid=(n // tm,),
        in_specs=[pl.BlockSpec((TOP_K, tm), lambda i: (0, i)),
                  pl.BlockSpec((tm, d), lambda i: (i, 0)),
                  pl.BlockSpec(mod.shape, lambda i: (0, 0)),
                  pl.BlockSpec((TOP_K, tm, d // 4), lambda i: (0, i, 0)),
                  pl.BlockSpec((TOP_K, tm, d // 4), lambda i: (0, i, 0))],
        out_specs=pl.BlockSpec((tm, d), lambda i: (i, 0)),
        compiler_params=_params("arbitrary"),
        name="finish",
    )(gates, xres, mod, yg_a, yg_b)


_QA_HEAD_ORDER = (0, 4, 1, 5, 2, 6, 3, 7)


def _reorder_qa_heads(a, axis):
    return jnp.concatenate([lax.slice_in_dim(a, h * HEAD_DIM, (h + 1) * HEAD_DIM, axis=axis)
                            for h in _QA_HEAD_ORDER], axis=axis)


def _rope_tables(s):
    quarter = HEAD_DIM // 4
    t = jnp.arange(s)
    row = (t // GRID_W).astype(F32)
    col = (t % GRID_W).astype(F32)
    freqs = ROPE_BASE ** (-jnp.arange(quarter, dtype=F32) / quarter)
    ar = row[:, None] * freqs[None, :]
    ac = col[:, None] * freqs[None, :]
    cos = jnp.concatenate([jnp.cos(ar), jnp.cos(ar), jnp.cos(ac), jnp.cos(ac)], axis=1)
    sin = jnp.concatenate([-jnp.sin(ar), jnp.sin(ar), -jnp.sin(ac), jnp.sin(ac)], axis=1)
    return jnp.tile(cos, (1, 2)), jnp.tile(sin, (1, 2))


def kernel(x, c, ctx, c_ctx, w_ada, b_ada, norm1, norm2, w_in, q_norm_a, k_norm_a, q_norm_b, k_norm_b, sink_a,
           rpb_b, out_norm_a, out_norm_b, w_out, w_router, router_bias, we_gate, we_up, we_down, ws_gate, ws_up,
           ws_down):
    assert w_ada.shape[0] == 1, "single-layer block"
    bsz, s, d = x.shape
    lc = ctx.shape[1]
    n = bsz * s
    rows = s // GRID_W
    assert d == 1024 and bsz + 1 <= 8 and rows >= NA_KROWS and lc % LANES == 0
    assert s % TOKEN_TILE == 0 and n % SC_WINDOW == 0
    assert (s // A_WINDOW) % A_BLOCKS_PER_STEP == 0 and (s // (NA_QROWS * GRID_W)) % NA_GROUPS_PER_STEP == 0

    c8 = jnp.concatenate([c, c_ctx[None, :], jnp.zeros((8 - bsz - 1, d), F32)], axis=0)
    mod = _ada(c8, w_ada[0], b_ada[0][None, :])

    w = w_in[0]
    aw, akw, bw = A_Q_HEADS * HEAD_DIM, A_KV_HEADS * HEAD_DIM, B_HEADS * HEAD_DIM
    cuts = np.cumsum([0, aw, akw, akw, bw, bw, bw])
    qa_w, ka_w, va_w, qb_w, kb_w, vb_w = [w[:, cuts[i]:cuts[i + 1]] for i in range(6)]
    qa_w = _reorder_qa_heads(qa_w, 1)
    w_ext = jnp.concatenate([qa_w, qb_w, kb_w, vb_w, ka_w, va_w], axis=1).astype(BF16)
    scale = HEAD_DIM ** -0.5 * LOG2E
    pair = lambda g: jnp.tile(g, 2)
    gains = jnp.stack([pair(q_norm_a[0]) * scale, pair(q_norm_b[0]) * scale, pair(k_norm_b[0]), pair(k_norm_a[0])]
                      + [jnp.zeros((LANES,), F32)] * 4)
    bd = jnp.asarray(np.kron(np.eye(2), np.full((HEAD_DIM, HEAD_DIM), 1.0 / HEAD_DIM)), BF16)
    cos_t, sin_t = _rope_tables(s)
    n1 = norm1[0][None, :]
    zq = _inproj(x, mod, n1, w_ext, gains, cos_t, sin_t, bd, mod_row=None, tm=TOKEN_TILE)
    zc = _inproj(ctx, mod, n1, w_ext, gains, jnp.ones((lc, LANES), F32), jnp.zeros((lc, LANES), F32), bd,
                 mod_row=bsz, tm=lc)

    sink = sink_a[0].astype(F32)
    sink_rep = jnp.concatenate([jnp.broadcast_to(sink[h] * LOG2E, (A_WINDOW, LANES)) for h in _QA_HEAD_ORDER])
    o_a = _attn_a(zq, zc, sink_rep, _attn_a_mask(A_WINDOW))
    o_b = _attn_b(zq, zc, _na_table(rpb_b[0], rows))

    ga = _reorder_qa_heads(out_norm_a[0], 0)[None, :]
    gb = out_norm_b[0][None, :]
    wo = jnp.concatenate([_reorder_qa_heads(w_out[0][:aw], 0), w_out[0][aw:]], axis=0).astype(BF16)
    wr_hi, wr_lo = _split_bf16(w_router[0].T)
    xres, hp_a, hp_b, idx, gates = _merge(o_a, o_b, x, mod, ga, gb, wo, norm2[0][None, :], wr_hi, wr_lo,
                                  router_bias[0][:, None], ws_gate[0].astype(BF16), ws_up[0].astype(BF16),
                                  ws_down[0].astype(BF16), tm=TOKEN_TILE)

    rank, counts = _rank(idx, tm=TOKEN_TILE)
    t = EXPERT_ROWS
    cnt = counts[:, 0].astype(I32)
    padded = (cnt + t - 1) // t * t
    pends = jnp.cumsum(padded)
    pstart = pends - padded
    n_rows = n * TOP_K + N_EXPERTS * t
    pos = _pos(idx, rank, pstart.astype(F32)[:, None], tm=min(POS_TILE, n))

    pos_flat = pos.reshape(1, TOP_K * n)
    xs_a, xs_b = _sc_scatter_rows((hp_a, hp_b), pos_flat, n_rows)
    ys_a, ys_b = _experts(pstart, cnt, pends[-1:] // t, xs_a, xs_b, we_gate[0], we_up[0], we_down[0])
    yg_a, yg_b = [y.reshape(TOP_K, n, d // 4) for y in _sc_gather_rows((ys_a, ys_b), pos_flat)]
    out = _finish(gates, xres.reshape(n, d), mod, yg_a, yg_b, tm=TOKEN_TILE, tiles_per_batch=s // TOKEN_TILE)
    return out.reshape(bsz, s, d)
```

```python
import functools

import numpy as np
import jax
import jax.numpy as jnp
from jax import lax
from jax.experimental import pallas as pl
from jax.experimental.pallas import tpu as pltpu
from jax.experimental.pallas import tpu_sc as plsc

F32 = jnp.float32
BF16 = jnp.bfloat16
I32 = jnp.int32
U32 = jnp.uint32

LANES = 128
HEAD_DIM = 64
GRID_W = 64
A_Q_HEADS = 8
A_KV_HEADS = 2
A_WINDOW = 128
B_HEADS = 8
NA_KH = 8
NA_KW = 16
NA_QROWS = 4
NA_KROWS = NA_QROWS + NA_KH
ROPE_BASE = 10000.0
N_EXPERTS = 256
TOP_K = 8
N_GROUPS = 8
TOPK_GROUPS = 4
ROUTED_SCALE = 2.5
LOG2E = 1.4426950408889634
EPS = 1e-6
NEG_INF = -1e30
TOKEN_TILE = 512
POS_TILE = 2048
EXPERT_ROWS = 256
SC_WINDOW = 128
VMEM_LIMIT = 56 * 1024 * 1024

_NT = (((1,), (1,)), ((), ()))


def _params(*sem):
    return pltpu.CompilerParams(dimension_semantics=sem, vmem_limit_bytes=VMEM_LIMIT)


def _silu(v):
    return v * jax.nn.sigmoid(v)


def _rms(v, gain):
    return v * lax.rsqrt(jnp.mean(v * v, axis=-1, keepdims=True) + EPS) * gain


def _pack_bf16_pairs(v):
    n = v.shape[1] // 2
    lo = lax.bitcast_convert_type(v[:, :n].astype(BF16).astype(F32), U32) >> 16
    hi = lax.bitcast_convert_type(v[:, n:].astype(BF16).astype(F32), U32) & jnp.uint32(0xFFFF0000)
    return hi | lo


def _split_bf16(v):
    hi = lax.bitcast_convert_type(lax.bitcast_convert_type(v, U32) & jnp.uint32(0xFFFF0000), F32)
    return hi.astype(BF16), (v - hi).astype(BF16)


def _unpack_bf16_pairs(w):
    lo = lax.bitcast_convert_type(w << 16, F32)
    hi = lax.bitcast_convert_type(w & jnp.uint32(0xFFFF0000), F32)
    return jnp.concatenate([lo, hi], axis=1)


def _ada_kernel(c_ref, w_ref, b_ref, o_ref):
    a = _silu(c_ref[...])
    o_ref[...] = jnp.dot(a, w_ref[...], preferred_element_type=F32,
                         precision=lax.Precision.HIGHEST) + b_ref[...]


def _ada(c8, w, b):
    d, n = w.shape
    bn = n // 4
    return pl.pallas_call(
        _ada_kernel,
        out_shape=jax.ShapeDtypeStruct((8, n), F32),
        grid=(n // bn,),
        in_specs=[pl.BlockSpec((8, d), lambda j: (0, 0)),
                  pl.BlockSpec((d, bn), lambda j: (0, j)),
                  pl.BlockSpec((1, bn), lambda j: (0, j))],
        out_specs=pl.BlockSpec((8, bn), lambda j: (0, j)),
        compiler_params=_params("arbitrary"),
        name="ada",
    )(c8, w, b)


_QA, _QB, _KB, _VB, _KA, _VA = 0, 4, 8, 12, 16, 17
_OUT_BLOCKS = 18


def _inproj_kernel(x_ref, mod_ref, n1_ref, w_ref, g_ref, cos_ref, sin_ref, bd_ref, o_ref, *, mod_row, d):
    b = pl.program_id(0) if mod_row is None else mod_row
    xn = _rms(x_ref[0], n1_ref[...])
    sh = mod_ref[pl.ds(b, 1), 0:d]
    sc = mod_ref[pl.ds(b, 1), d:2 * d]
    h = (xn * (1.0 + sc) + sh).astype(BF16)
    z = jnp.dot(h, w_ref[...], preferred_element_type=F32)
    bd = bd_ref[...]
    cos = cos_ref[...]
    sin = sin_ref[...]

    def blk(j):
        return z[:, j * LANES:(j + 1) * LANES]

    def head_rinv(zb):
        ms = jnp.dot((zb * zb).astype(BF16), bd, preferred_element_type=F32)
        return lax.rsqrt(ms + EPS)

    def put(j, v):
        o_ref[0, :, j * LANES:(j + 1) * LANES] = v.astype(BF16)

    quarter = HEAD_DIM // 4
    first_half = (lax.broadcasted_iota(I32, (1, LANES), 1) % (2 * quarter)) < quarter

    def roped(j, g_row):
        zb = blk(j)
        zn = zb * head_rinv(zb) * g_ref[g_row:g_row + 1, :]
        partner = jnp.where(first_half, pltpu.roll(zn, LANES - quarter, 1), pltpu.roll(zn, quarter, 1))
        put(j, zn * cos + partner * sin)

    def normed(j, g_row):
        zb = blk(j)
        put(j, zb * head_rinv(zb) * g_ref[g_row:g_row + 1, :])

    for j in range(4):
        roped(_QA + j, 0)
        normed(_QB + j, 1)
        normed(_KB + j, 2)
        put(_VB + j, blk(_VB + j))
    roped(_KA, 3)
    put(_VA, blk(_VA))


def _inproj(x, mod, n1, w_ext, gains, cos_t, sin_t, bd, *, mod_row, tm):
    bsz, s, d = x.shape
    kern = functools.partial(_inproj_kernel, mod_row=mod_row, d=d)
    return pl.pallas_call(
        kern,
        out_shape=jax.ShapeDtypeStruct((bsz, s, _OUT_BLOCKS * LANES), BF16),
        grid=(bsz, s // tm),
        in_specs=[pl.BlockSpec((1, tm, d), lambda b, i: (b, i, 0)),
                  pl.BlockSpec(mod.shape, lambda b, i: (0, 0)),
                  pl.BlockSpec((1, d), lambda b, i: (0, 0)),
                  pl.BlockSpec(w_ext.shape, lambda b, i: (0, 0)),
                  pl.BlockSpec(gains.shape, lambda b, i: (0, 0)),
                  pl.BlockSpec((tm, LANES), lambda b, i: (i, 0)),
                  pl.BlockSpec((tm, LANES), lambda b, i: (i, 0)),
                  pl.BlockSpec(bd.shape, lambda b, i: (0, 0))],
        out_specs=pl.BlockSpec((1, tm, _OUT_BLOCKS * LANES), lambda b, i: (b, i, 0)),
        compiler_params=_params("arbitrary", "arbitrary"),
        name="inproj",
    )(x, mod, n1, w_ext, gains, cos_t, sin_t, bd)


def _split_pair(qp, lo):
    zero = jnp.zeros_like(qp)
    return jnp.concatenate([jnp.where(lo, qp, zero), jnp.where(lo, zero, qp)], axis=0)


def _softmax_pv(s_parts, v_parts, sink_rep):
    chunks = [s[:, c * LANES:(c + 1) * LANES] for s in s_parts for c in range(s.shape[1] // LANES)]
    m = jnp.max(functools.reduce(jnp.maximum, chunks), axis=-1, keepdims=True)
    m_rep = jnp.broadcast_to(m, (m.shape[0], LANES))
    if sink_rep is not None:
        m_rep = jnp.maximum(m_rep, sink_rep)
    acc = None
    for s, v in zip(s_parts, v_parts):
        p = jnp.concatenate([jnp.exp2(s[:, c * LANES:(c + 1) * LANES] - m_rep)
                             for c in range(s.shape[1] // LANES)], axis=1).astype(BF16)
        v_ext = jnp.concatenate([v, jnp.ones_like(v)], axis=1)
        o = jnp.dot(p, v_ext, preferred_element_type=F32)
        acc = o if acc is None else acc + o
    l_rep = acc[:, LANES:]
    if sink_rep is not None:
        l_rep = l_rep + jnp.exp2(sink_rep - m_rep)
    return acc[:, :LANES] * (1.0 / l_rep)


A_BLOCKS_PER_STEP = 8


def _attn_a_kernel(q_ref, *refs):
    nq = A_BLOCKS_PER_STEP
    k_refs, v_refs = refs[:nq + 2], refs[nq + 2:2 * nq + 4]
    kx_ref, vx_ref, sink_ref = refs[2 * nq + 4:2 * nq + 7]
    mask_refs = refs[2 * nq + 7:3 * nq + 7]
    o_ref = refs[-1]
    tq = A_WINDOW
    lo = lax.broadcasted_iota(I32, (tq, LANES), 1) < HEAD_DIM
    for h in range(nq):
        q = q_ref[0, h * tq:(h + 1) * tq]
        qs = jnp.concatenate([_split_pair(q[:, p * LANES:(p + 1) * LANES], lo) for p in range(4)], axis=0)
        k = jnp.concatenate([r[0] for r in k_refs[h:h + 3]], axis=0)
        v = jnp.concatenate([r[0] for r in v_refs[h:h + 3]], axis=0)
        s_loc = lax.dot_general(qs, k, _NT, preferred_element_type=F32) + mask_refs[h][0]
        s_ctx = lax.dot_general(qs, kx_ref[0], _NT, preferred_element_type=F32)
        o = _softmax_pv([s_loc, s_ctx], [v, vx_ref[0]], sink_ref[...])
        for p in range(4):
            o_lo = o[(2 * p) * tq:(2 * p + 1) * tq]
            o_hi = o[(2 * p + 1) * tq:(2 * p + 2) * tq]
            o_ref[0, h * tq:(h + 1) * tq, p * LANES:(p + 1) * LANES] = jnp.where(lo, o_lo, o_hi).astype(BF16)


def _attn_a(zq, zc, sink_rep, mask):
    bsz, s, _ = zq.shape
    lc = zc.shape[1]
    tq = A_WINDOW
    nq = A_BLOCKS_PER_STEP
    nblk = s // tq
    ka, va = _KA, _VA

    def kv_spec(col, shift):
        return pl.BlockSpec((1, tq, LANES), lambda b, j: (b, jnp.clip(nq * j + shift, 0, nblk - 1), col))

    def mask_spec(h):
        def variant(j):
            blk = nq * j + h
            return jnp.where(blk == 0, 0, jnp.where(blk == nblk - 1, 2, 1))
        return pl.BlockSpec((1,) + mask.shape[1:], lambda b, j: (variant(j), 0, 0))

    shifts = range(-1, nq + 1)
    return pl.pallas_call(
        _attn_a_kernel,
        out_shape=jax.ShapeDtypeStruct((bsz, s, 4 * LANES), BF16),
        grid=(bsz, nblk // nq),
        in_specs=([pl.BlockSpec((1, nq * tq, 4 * LANES), lambda b, j: (b, j, 0))]
                  + [kv_spec(ka, sh) for sh in shifts] + [kv_spec(va, sh) for sh in shifts]
                  + [pl.BlockSpec((1, lc, LANES), lambda b, j: (b, 0, ka)),
                     pl.BlockSpec((1, lc, LANES), lambda b, j: (b, 0, va)),
                     pl.BlockSpec(sink_rep.shape, lambda b, j: (0, 0))]
                  + [mask_spec(h) for h in range(nq)]),
        out_specs=pl.BlockSpec((1, nq * tq, 4 * LANES), lambda b, j: (b, j, 0)),
        compiler_params=_params("arbitrary", "arbitrary"),
        name="attn_a",
    )(*([zq] * (2 * nq + 5) + [zc, zc, sink_rep] + [mask] * nq))


def _attn_a_mask(tq):
    qi = np.arange(tq)[:, None]
    kj = np.arange(3 * tq)[None, :]
    ok = (kj >= qi) & (kj <= qi + 2 * tq)
    variants = [ok & (kj >= tq), ok, ok & (kj < 2 * tq)]
    m = np.stack([np.tile(np.where(v, 0.0, NEG_INF).astype(np.float32), (A_Q_HEADS, 1)) for v in variants])
    return jnp.asarray(m)


NA_GROUPS_PER_STEP = 4


def _attn_b_kernel(q_ref, *refs):
    ngs = NA_GROUPS_PER_STEP
    nkv = ngs + 2
    k_refs, v_refs = refs[:nkv], refs[nkv:2 * nkv]
    kx_ref, vx_ref = refs[2 * nkv:2 * nkv + 2]
    tab_refs = refs[2 * nkv + 2:2 * nkv + 2 + ngs]
    o_ref = refs[-1]
    tq = NA_QROWS * GRID_W
    lo = lax.broadcasted_iota(I32, (tq, LANES), 1) < HEAD_DIM
    for g in range(ngs):
        rows_g = slice(g * tq, (g + 1) * tq)
        for p in range(4):
            sl = slice(p * LANES, (p + 1) * LANES)
            qs = _split_pair(q_ref[0, rows_g, sl], lo)
            k = jnp.concatenate([r[0, :, sl] for r in k_refs[g:g + 3]], axis=0)
            v = jnp.concatenate([r[0, :, sl] for r in v_refs[g:g + 3]], axis=0)
            s_loc = lax.dot_general(qs, k, _NT, preferred_element_type=F32) + tab_refs[g][0, p]
            s_ctx = lax.dot_general(qs, kx_ref[0, :, sl], _NT, preferred_element_type=F32)
            o = _softmax_pv([s_loc, s_ctx], [v, vx_ref[0, :, sl]], None)
            o_ref[0, rows_g, sl] = jnp.where(lo, o[:tq], o[tq:]).astype(BF16)


def _attn_b(zq, zc, table):
    bsz, s, _ = zq.shape
    lc = zc.shape[1]
    tq = NA_QROWS * GRID_W
    ngs = NA_GROUPS_PER_STEP
    ng = s // tq
    qb, kb, vb = _QB // 4, _KB // 4, _VB // 4

    def kv_spec(col, j):
        return pl.BlockSpec((1, tq, 4 * LANES), lambda i, b: (b, jnp.clip(ngs * i + j - 1, 0, ng - 1), col))

    def tab_spec(g):
        def variant(i):
            grp = ngs * i + g
            return jnp.where(grp == 0, 0, jnp.where(grp == ng - 1, 2, 1))
        return pl.BlockSpec((1,) + table.shape[1:], lambda i, b: (variant(i), 0, 0, 0),
                            pipeline_mode=pl.Buffered(1))

    kv_slots = range(ngs + 2)
    return pl.pallas_call(
        _attn_b_kernel,
        out_shape=jax.ShapeDtypeStruct((bsz, s, 4 * LANES), BF16),
        grid=(ng // ngs, bsz),
        in_specs=([pl.BlockSpec((1, ngs * tq, 4 * LANES), lambda i, b: (b, i, qb))]
                  + [kv_spec(kb, j) for j in kv_slots] + [kv_spec(vb, j) for j in kv_slots]
                  + [pl.BlockSpec((1, lc, 4 * LANES), lambda i, b: (b, 0, kb)),
                     pl.BlockSpec((1, lc, 4 * LANES), lambda i, b: (b, 0, vb))]
                  + [tab_spec(g) for g in range(ngs)]),
        out_specs=pl.BlockSpec((1, ngs * tq, 4 * LANES), lambda i, b: (b, i, 0)),
        compiler_params=_params("arbitrary", "arbitrary"),
        name="attn_b",
    )(*([zq] * (2 * ngs + 5) + [zc, zc] + [table] * ngs))


def _na_table(rpb, rows):
    ng = rows // NA_QROWS
    nq, nk = NA_QROWS * GRID_W, NA_KROWS * GRID_W
    qc = np.arange(GRID_W)
    kc = np.arange(GRID_W)
    ws = np.clip(qc - NA_KW // 2, 0, GRID_W - NA_KW)
    valid_c = (kc[None, :] >= ws[:, None]) & (kc[None, :] < ws[:, None] + NA_KW)
    dc = np.clip(kc[None, :] - qc[:, None], -(NA_KW - 1), NA_KW - 1) + (NA_KW - 1)
    c_sel = (dc[..., None] == np.arange(2 * NA_KW - 1)) & valid_c[..., None]
    tiles = jnp.einsum('hab,uvb->huav', rpb.astype(F32), jnp.asarray(c_sel, F32), precision=lax.Precision.HIGHEST)
    tiles = jnp.where(jnp.asarray(valid_c)[None, :, None, :], tiles * LOG2E, NEG_INF)
    tiles = tiles.reshape(B_HEADS, GRID_W, (2 * NA_KH - 1) * GRID_W)

    def masked(n_key_rows):
        return jnp.full((B_HEADS, GRID_W, n_key_rows * GRID_W), NEG_INF, F32)

    tabs = []
    for i in (0, 1, ng - 1):
        start = NA_QROWS * i - NA_KH // 2
        q_rows = []
        for qr in range(NA_QROWS):
            r = NA_QROWS * i + qr
            rs = int(np.clip(r - NA_KH // 2, 0, rows - NA_KH))
            y0 = rs - start
            d0 = rs - r + (NA_KH - 1)
            q_rows.append(jnp.concatenate([masked(y0), tiles[:, :, d0 * GRID_W:(d0 + NA_KH) * GRID_W],
                                           masked(NA_KROWS - NA_KH - y0)], axis=-1))
        tabs.append(jnp.concatenate(q_rows, axis=-2))
    return jnp.stack(tabs).reshape(3, B_HEADS // 2, 2 * nq, nk)


def _merge_kernel(oa_ref, ob_ref, x_ref, mod_ref, ga_ref, gb_ref, wo_ref, n2_ref, wrh_ref, wrl_ref, rb_ref,
                  wsg_ref, wsu_ref, wsd_ref, xres_ref, hpa_ref, hpb_ref, idx_ref, gate_ref, *, d):
    b = pl.program_id(0)
    tm = x_ref.shape[1]

    def mod(k):
        return mod_ref[pl.ds(b, 1), k * d:(k + 1) * d]

    na = _rms(oa_ref[0].astype(F32), ga_ref[...])
    nb = _rms(ob_ref[0].astype(F32), gb_ref[...])
    cat = jnp.concatenate([na, nb], axis=1).astype(BF16)
    y = jnp.dot(cat, wo_ref[...], preferred_element_type=F32)
    x1 = x_ref[0] + mod(2) * y
    h2 = _rms(x1, n2_ref[...]) * (1.0 + mod(4)) + mod(3)

    hb = h2.astype(BF16)
    act = _silu(jnp.dot(hb, wsg_ref[...], preferred_element_type=F32)) * jnp.dot(hb, wsu_ref[...],
                                                                                 preferred_element_type=F32)
    shared = jnp.dot(act.astype(BF16), wsd_ref[...], preferred_element_type=F32)
    xres_ref[0] = x1 + mod(5) * shared
    packed = _pack_bf16_pairs(h2)
    hpa_ref[...] = packed[:, :d // 4]
    hpb_ref[...] = packed[:, d // 4:]

    h_hi, h_lo = _split_bf16(h2)
    logits = (lax.dot_general(wrh_ref[...], h_hi, _NT, preferred_element_type=F32)
              + lax.dot_general(wrh_ref[...], h_lo, _NT, preferred_element_type=F32)
              + lax.dot_general(wrl_ref[...], h_hi, _NT, preferred_element_type=F32))
    scores = jax.nn.sigmoid(logits)
    sel = scores + rb_ref[...]
    per = N_EXPERTS // N_GROUPS
    g3 = sel.reshape(N_GROUPS, per, tm)
    it3 = lax.broadcasted_iota(I32, (N_GROUPS, per, tm), 1)
    m1 = jnp.max(g3, axis=1, keepdims=True)
    first = jnp.min(jnp.where(g3 == m1, it3, per), axis=1, keepdims=True)
    m2 = jnp.max(jnp.where(it3 == first, -jnp.inf, g3), axis=1, keepdims=True)
    gscore = (m1 + m2).reshape(N_GROUPS, tm)

    itg = lax.broadcasted_iota(I32, (N_GROUPS, tm), 0)
    gsel = jnp.zeros((N_GROUPS, tm), F32)
    cur = gscore
    for _ in range(TOPK_GROUPS):
        mx = jnp.max(cur, axis=0, keepdims=True)
        fi = jnp.min(jnp.where(cur == mx, itg, N_GROUPS), axis=0, keepdims=True)
        pick = itg == fi
        gsel = jnp.where(pick, 1.0, gsel)
        cur = jnp.where(pick, -jnp.inf, cur)
    emask = jnp.broadcast_to(gsel.reshape(N_GROUPS, 1, tm), (N_GROUPS, per, tm)).reshape(N_EXPERTS, tm) > 0.5

    ite = lax.broadcasted_iota(I32, (N_EXPERTS, tm), 0)
    cur = jnp.where(emask, sel, NEG_INF)
    idx_rows, s_rows = [], []
    for _ in range(TOP_K):
        mx = jnp.max(cur, axis=0, keepdims=True)
        fi = jnp.min(jnp.where(cur == mx, ite, N_EXPERTS), axis=0, keepdims=True)
        pick = ite == fi
        idx_rows.append(fi)
        s_rows.append(jnp.sum(jnp.where(pick, scores, 0.0), axis=0, keepdims=True))
        cur = jnp.where(pick, -jnp.inf, cur)
    top_s = jnp.concatenate(s_rows, axis=0)
    idx_ref[...] = jnp.concatenate(idx_rows, axis=0)
    gate_ref[...] = top_s / jnp.sum(top_s, axis=0, keepdims=True) * ROUTED_SCALE


def _merge(o_a, o_b, x, mod, ga, gb, wo, n2, wr_hi, wr_lo, rb_col, wsg, wsu, wsd, *, tm):
    bsz, s, d = x.shape
    nt = s // tm
    n = bsz * s
    full = lambda a: pl.BlockSpec(a.shape, lambda b, i: (0,) * a.ndim)
    return pl.pallas_call(
        functools.partial(_merge_kernel, d=d),
        out_shape=(jax.ShapeDtypeStruct((bsz, s, d), F32),
                   jax.ShapeDtypeStruct((n, d // 4), U32),
                   jax.ShapeDtypeStruct((n, d // 4), U32),
                   jax.ShapeDtypeStruct((TOP_K, n), I32),
                   jax.ShapeDtypeStruct((TOP_K, n), F32)),
        grid=(bsz, nt),
        in_specs=[pl.BlockSpec((1, tm, d // 2), lambda b, i: (b, i, 0)),
                  pl.BlockSpec((1, tm, d // 2), lambda b, i: (b, i, 0)),
                  pl.BlockSpec((1, tm, d), lambda b, i: (b, i, 0)),
                  full(mod), full(ga), full(gb), full(wo), full(n2), full(wr_hi), full(wr_lo), full(rb_col),
                  full(wsg), full(wsu), full(wsd)],
        out_specs=(pl.BlockSpec((1, tm, d), lambda b, i: (b, i, 0)),
                   pl.BlockSpec((tm, d // 4), lambda b, i: (b * nt + i, 0)),
                   pl.BlockSpec((tm, d // 4), lambda b, i: (b * nt + i, 0)),
                   pl.BlockSpec((TOP_K, tm), lambda b, i: (0, b * nt + i)),
                   pl.BlockSpec((TOP_K, tm), lambda b, i: (0, b * nt + i))),
        compiler_params=_params("arbitrary", "arbitrary"),
        name="merge",
    )(o_a, o_b, x, mod, ga, gb, wo, n2, wr_hi, wr_lo, rb_col, wsg, wsu, wsd)


def _rank_kernel(idx_ref, rank_ref, cnt_ref, carry_ref):
    tm = idx_ref.shape[1]

    @pl.when(pl.program_id(0) == 0)
    def _():
        carry_ref[...] = jnp.zeros_like(carry_ref)

    idx = idx_ref[...]
    ite = lax.broadcasted_iota(I32, (N_EXPERTS, tm), 0)
    before = (lax.broadcasted_iota(I32, (tm, tm), 0) < lax.broadcasted_iota(I32, (tm, tm), 1)).astype(BF16)
    hits = [ite == idx[k:k + 1, :] for k in range(TOP_K)]
    routed = jnp.where(functools.reduce(jnp.logical_or, hits), 1.0, 0.0)
    base = carry_ref[...]
    ahead = jnp.dot(routed.astype(BF16), before, preferred_element_type=F32) + base
    rows = [jnp.sum(jnp.where(hit, ahead, 0.0), axis=0, keepdims=True) for hit in hits]
    rank_ref[...] = jnp.concatenate(rows, axis=0).astype(I32)
    total = base + jnp.sum(routed, axis=1, keepdims=True)
    carry_ref[...] = total
    cnt_ref[...] = total


def _rank(idx, *, tm):
    n = idx.shape[1]
    return pl.pallas_call(
        _rank_kernel,
        out_shape=(jax.ShapeDtypeStruct((TOP_K, n), I32), jax.ShapeDtypeStruct((N_EXPERTS, 1), F32)),
        grid=(n // tm,),
        in_specs=[pl.BlockSpec((TOP_K, tm), lambda i: (0, i))],
        out_specs=(pl.BlockSpec((TOP_K, tm), lambda i: (0, i)), pl.BlockSpec((N_EXPERTS, 1), lambda i: (0, 0))),
        scratch_shapes=[pltpu.VMEM((N_EXPERTS, 1), F32)],
        compiler_params=_params("arbitrary"),
        name="rank",
    )(idx)


def _pos_kernel(idx_ref, rank_ref, pstart_ref, pos_ref):
    tm = idx_ref.shape[1]
    idx = idx_ref[...]
    ite = lax.broadcasted_iota(I32, (N_EXPERTS, tm), 0)
    pstart = pstart_ref[...]
    rows = [jnp.sum(jnp.where(ite == idx[k:k + 1, :], pstart, 0.0), axis=0, keepdims=True) for k in range(TOP_K)]
    pos_ref[...] = jnp.concatenate(rows, axis=0).astype(I32) + rank_ref[...]


def _pos(idx, rank, pstart_col, *, tm):
    n = idx.shape[1]
    return pl.pallas_call(
        _pos_kernel,
        out_shape=jax.ShapeDtypeStruct((TOP_K, n), I32),
        grid=(n // tm,),
        in_specs=[pl.BlockSpec((TOP_K, tm), lambda i: (0, i)),
                  pl.BlockSpec((TOP_K, tm), lambda i: (0, i)),
                  pl.BlockSpec((N_EXPERTS, 1), lambda i: (0, 0))],
        out_specs=pl.BlockSpec((TOP_K, tm), lambda i: (0, i)),
        compiler_params=_params("arbitrary"),
        name="pos",
    )(idx, rank, pstart_col)


UNIT_CHUNKS = (4, 2, 1)
RING_AHEAD = 8
RING_SLOTS = RING_AHEAD + UNIT_CHUNKS[0]


def _expert_kernel(ps_ref, cnt_ref, tot_ref, wg_ref, wu_ref, wd_ref, xa_hbm, xb_hbm, ya_hbm, yb_hbm,
                   xa_buf, xb_buf, ya_buf, yb_buf, in_sem, out_sem):
    e = pl.program_id(0)
    r = EXPERT_ROWS
    ahead = RING_AHEAD
    total = tot_ref[0]
    first = ps_ref[e] // r
    cnt = cnt_ref[e]
    nch = (cnt + r - 1) // r

    def rows_of(g):
        return pl.ds(pl.multiple_of(g * r, r), r)

    def fetch(g):
        slot = g % RING_SLOTS
        return (pltpu.make_async_copy(xa_hbm.at[rows_of(g)], xa_buf.at[slot], in_sem.at[0, slot]),
                pltpu.make_async_copy(xb_hbm.at[rows_of(g)], xb_buf.at[slot], in_sem.at[1, slot]))

    def flush(g):
        slot = g % RING_SLOTS
        return (pltpu.make_async_copy(ya_buf.at[slot], ya_hbm.at[rows_of(g)], out_sem.at[0, slot]),
                pltpu.make_async_copy(yb_buf.at[slot], yb_hbm.at[rows_of(g)], out_sem.at[1, slot]))

    def start(copies):
        for cp in copies:
            cp.start()

    def wait(copies):
        for cp in copies:
            cp.wait()

    @pl.when(e == 0)
    def _():
        for g in range(ahead):
            @pl.when(g < total)
            def _():
                start(fetch(g))

    @pl.when(nch > 0)
    def _():
        def unit(c0, u):
            for j in range(u):
                g = first + c0 + j
                wait(fetch(g))

                @pl.when(g + ahead < total)
                def _():
                    start(fetch(g + ahead))

                @pl.when(g >= RING_SLOTS)
                def _():
                    wait(flush(g - RING_SLOTS))

            slots = [(first + c0 + j) % RING_SLOTS for j in range(u)]
            packed = jnp.concatenate([jnp.concatenate([xa_buf[sl], xb_buf[sl]], axis=1) for sl in slots], axis=0)
            row = lax.broadcasted_iota(I32, packed.shape, 0)
            packed = jnp.where(row < cnt - c0 * r, packed, jnp.zeros_like(packed))
            x = _unpack_bf16_pairs(packed).astype(BF16)
            gate = jnp.dot(x, wg_ref[0].astype(BF16), preferred_element_type=F32)
            up = jnp.dot(x, wu_ref[0].astype(BF16), preferred_element_type=F32)
            y = jnp.dot((_silu(gate) * up).astype(BF16), wd_ref[0].astype(BF16), preferred_element_type=F32)
            out = _pack_bf16_pairs(y)
            half = out.shape[1] // 2
            for j, sl in enumerate(slots):
                ya_buf[sl] = out[j * r:(j + 1) * r, :half]
                yb_buf[sl] = out[j * r:(j + 1) * r, half:]
                start(flush(first + c0 + j))

        big = UNIT_CHUNKS[0]

        def big_unit(i, carry):
            unit(i * big, big)
            return carry

        lax.fori_loop(0, nch // big, big_unit, 0)
        done = nch // big * big
        for u in UNIT_CHUNKS[1:]:
            @pl.when((nch % (2 * u)) >= u)
            def _():
                unit(done, u)

            done = done + jnp.where((nch % (2 * u)) >= u, u, 0)

    @pl.when(e == pl.num_programs(0) - 1)
    def _():
        for k in range(RING_SLOTS):
            @pl.when(total - 1 - k >= 0)
            def _():
                wait(flush(total - 1 - k))


def _experts(pstart, cnt, total_chunks, xs_a, xs_b, wg, wu, wd):
    rows, hw = xs_a.shape
    r = EXPERT_ROWS
    n_exp, d, f = wg.shape
    hbm = pl.BlockSpec(memory_space=pl.ANY)
    grid_spec = pltpu.PrefetchScalarGridSpec(
        num_scalar_prefetch=3,
        grid=(n_exp,),
        in_specs=[pl.BlockSpec((1, d, f), lambda e, ps, cn, tot: (e, 0, 0)),
                  pl.BlockSpec((1, d, f), lambda e, ps, cn, tot: (e, 0, 0)),
                  pl.BlockSpec((1, f, d), lambda e, ps, cn, tot: (e, 0, 0)),
                  hbm, hbm],
        out_specs=(hbm, hbm),
        scratch_shapes=[pltpu.VMEM((RING_SLOTS, r, hw), U32), pltpu.VMEM((RING_SLOTS, r, hw), U32),
                        pltpu.VMEM((RING_SLOTS, r, hw), U32), pltpu.VMEM((RING_SLOTS, r, hw), U32),
                        pltpu.SemaphoreType.DMA((2, RING_SLOTS)), pltpu.SemaphoreType.DMA((2, RING_SLOTS))],
    )
    return pl.pallas_call(
        _expert_kernel,
        out_shape=(jax.ShapeDtypeStruct((rows, hw), U32), jax.ShapeDtypeStruct((rows, hw), U32)),
        grid_spec=grid_spec,
        compiler_params=_params("arbitrary"),
        name="experts",
    )(pstart, cnt, total_chunks, wg, wu, wd, xs_a, xs_b)


def _sc_scatter_rows(xs, idx_flat, n_rows):
    n, w = xs[0].shape
    m = idx_flat.shape[1]
    nwin = n // SC_WINDOW
    reps = m // n
    mesh = plsc.VectorSubcoreMesh(core_axis_name="core", subcore_axis_name="subcore")
    out_type = tuple(jax.ShapeDtypeStruct((n_rows, w), x.dtype) for x in xs)

    @functools.partial(pl.kernel, out_type=out_type, mesh=mesh, scratch_types=[])
    def scatter_kernel(*refs):
        x_hbms, i_hbm, o_hbms = refs[:len(xs)], refs[len(xs)], refs[len(xs) + 1:]
        for x_hbm, o_hbm in zip(x_hbms, o_hbms):
            def body(x_vmem, i_vmem, o_hbm=o_hbm):
                pltpu.sync_copy(x_vmem, o_hbm.at[i_vmem.at[0]])

            pltpu.emit_pipeline(
                body,
                grid=(nwin, reps),
                in_specs=[pl.BlockSpec((SC_WINDOW, w), lambda i, k: (i, 0)),
                          pl.BlockSpec((1, SC_WINDOW), lambda i, k: (0, k * nwin + i))],
                out_specs=[],
                core_axis_name=("core", "subcore"),
                dimension_semantics=(pltpu.PARALLEL, pltpu.ARBITRARY),
            )(x_hbm, i_hbm)

    return scatter_kernel(*xs, idx_flat)


def _sc_gather_rows(srcs, idx_flat):
    m = idx_flat.shape[1]
    w = srcs[0].shape[1]
    mesh = plsc.VectorSubcoreMesh(core_axis_name="core", subcore_axis_name="subcore")
    out_type = tuple(jax.ShapeDtypeStruct((m, w), src.dtype) for src in srcs)

    @functools.partial(pl.kernel, out_type=out_type, mesh=mesh)
    def gather_kernel(*refs):
        x_hbms, i_hbm, o_hbms = refs[:len(srcs)], refs[len(srcs)], refs[len(srcs) + 1:]
        for x_hbm, o_hbm in zip(x_hbms, o_hbms):
            def body(i_vmem, o_vmem, x_hbm=x_hbm):
                pltpu.sync_copy(x_hbm.at[i_vmem.at[0]], o_vmem)

            pltpu.emit_pipeline(
                body,
                grid=(m // SC_WINDOW,),
                in_specs=[pl.BlockSpec((1, SC_WINDOW), lambda i: (0, i))],
                out_specs=[pl.BlockSpec((SC_WINDOW, w), lambda i: (i, 0))],
                core_axis_name=("core", "subcore"),
                dimension_semantics=(pltpu.PARALLEL,),
            )(i_hbm, o_hbm)

    return gather_kernel(*srcs, idx_flat)


def _finish_kernel(gate_ref, xres_ref, mod_ref, ya_ref, yb_ref, o_ref, *, d, tiles_per_batch):
    tm = gate_ref.shape[1]
    b = pl.program_id(0) // tiles_per_batch
    gates = gate_ref[...].T
    acc = jnp.zeros((tm, d), F32)
    for k in range(TOP_K):
        packed = jnp.concatenate([ya_ref[k], yb_ref[k]], axis=1)
        acc = acc + _unpack_bf16_pairs(packed) * gates[:, k:k + 1]
    o_ref[...] = xres_ref[...] + mod_ref[pl.ds(b, 1), 5 * d:6 * d] * acc


def _finish(gates, xres, mod, yg_a, yg_b, *, tm, tiles_per_batch):
    n, d = xres.shape
    return pl.pallas_call(
        functools.partial(_finish_kernel, d=d, tiles_per_batch=tiles_per_batch),
        out_shape=jax.ShapeDtypeStruct((n, d), F32),
        grid=(n // tm,),
        in_specs=[pl.BlockSpec((TOP_K, tm), lambda i: (0, i)),
                  pl.BlockSpec((tm, d), lambda i: (i, 0)),
                  pl.BlockSpec(mod.shape, lambda i: (0, 0)),
                  pl.BlockSpec((TOP_K, tm, d // 4), lambda i: (0, i, 0)),
                  pl.BlockSpec((TOP_K, tm, d // 4), lambda i: (0, i, 0))],
        out_specs=pl.BlockSpec((tm, d), lambda i: (i, 0)),
        compiler_params=_params("arbitrary"),
        name="finish",
    )(gates, xres, mod, yg_a, yg_b)


_QA_HEAD_ORDER = (0, 4, 1, 5, 2, 6, 3, 7)


def _reorder_qa_heads(a, axis):
    return jnp.concatenate([lax.slice_in_dim(a, h * HEAD_DIM, (h + 1) * HEAD_DIM, axis=axis)
                            for h in _QA_HEAD_ORDER], axis=axis)


def _rope_tables(s):
    quarter = HEAD_DIM // 4
    t = jnp.arange(s)
    row = (t // GRID_W).astype(F32)
    col = (t % GRID_W).astype(F32)
    freqs = ROPE_BASE ** (-jnp.arange(quarter, dtype=F32) / quarter)
    ar = row[:, None] * freqs[None, :]
    ac = col[:, None] * freqs[None, :]
    cos = jnp.concatenate([jnp.cos(ar), jnp.cos(ar), jnp.cos(ac), jnp.cos(ac)], axis=1)
    sin = jnp.concatenate([-jnp.sin(ar), jnp.sin(ar), -jnp.sin(ac), jnp.sin(ac)], axis=1)
    return jnp.tile(cos, (1, 2)), jnp.tile(sin, (1, 2))


def kernel(x, c, ctx, c_ctx, w_ada, b_ada, norm1, norm2, w_in, q_norm_a, k_norm_a, q_norm_b, k_norm_b, sink_a,
           rpb_b, out_norm_a, out_norm_b, w_out, w_router, router_bias, we_gate, we_up, we_down, ws_gate, ws_up,
           ws_down):
    assert w_ada.shape[0] == 1, "single-layer block"
    bsz, s, d = x.shape
    lc = ctx.shape[1]
    n = bsz * s
    rows = s // GRID_W
    assert d == 1024 and bsz + 1 <= 8 and rows >= NA_KROWS and lc % LANES == 0
    assert s % TOKEN_TILE == 0 and n % SC_WINDOW == 0
    assert (s // A_WINDOW) % A_BLOCKS_PER_STEP == 0 and (s // (NA_QROWS * GRID_W)) % NA_GROUPS_PER_STEP == 0

    c8 = jnp.concatenate([c, c_ctx[None, :], jnp.zeros((8 - bsz - 1, d), F32)], axis=0)
    mod = _ada(c8, w_ada[0], b_ada[0][None, :])

    w = w_in[0]
    aw, akw, bw = A_Q_HEADS * HEAD_DIM, A_KV_HEADS * HEAD_DIM, B_HEADS * HEAD_DIM
    cuts = np.cumsum([0, aw, akw, akw, bw, bw, bw])
    qa_w, ka_w, va_w, qb_w, kb_w, vb_w = [w[:, cuts[i]:cuts[i + 1]] for i in range(6)]
    qa_w = _reorder_qa_heads(qa_w, 1)
    w_ext = jnp.concatenate([qa_w, qb_w, kb_w, vb_w, ka_w, va_w], axis=1).astype(BF16)
    scale = HEAD_DIM ** -0.5 * LOG2E
    pair = lambda g: jnp.tile(g, 2)
    gains = jnp.stack([pair(q_norm_a[0]) * scale, pair(q_norm_b[0]) * scale, pair(k_norm_b[0]), pair(k_norm_a[0])]
                      + [jnp.zeros((LANES,), F32)] * 4)
    bd = jnp.asarray(np.kron(np.eye(2), np.full((HEAD_DIM, HEAD_DIM), 1.0 / HEAD_DIM)), BF16)
    cos_t, sin_t = _rope_tables(s)
    n1 = norm1[0][None, :]
    zq = _inproj(x, mod, n1, w_ext, gains, cos_t, sin_t, bd, mod_row=None, tm=TOKEN_TILE)
    zc = _inproj(ctx, mod, n1, w_ext, gains, jnp.ones((lc, LANES), F32), jnp.zeros((lc, LANES), F32), bd,
                 mod_row=bsz, tm=lc)

    sink = sink_a[0].astype(F32)
    sink_rep = jnp.concatenate([jnp.broadcast_to(sink[h] * LOG2E, (A_WINDOW, LANES)) for h in _QA_HEAD_ORDER])
    o_a = _attn_a(zq, zc, sink_rep, _attn_a_mask(A_WINDOW))
    o_b = _attn_b(zq, zc, _na_table(rpb_b[0], rows))

    ga = _reorder_qa_heads(out_norm_a[0], 0)[None, :]
    gb = out_norm_b[0][None, :]
    wo = jnp.concatenate([_reorder_qa_heads(w_out[0][:aw], 0), w_out[0][aw:]], axis=0).astype(BF16)
    wr_hi, wr_lo = _split_bf16(w_router[0].T)
    xres, hp_a, hp_b, idx, gates = _merge(o_a, o_b, x, mod, ga, gb, wo, norm2[0][None, :], wr_hi, wr_lo,
                                  router_bias[0][:, None], ws_gate[0].astype(BF16), ws_up[0].astype(BF16),
                                  ws_down[0].astype(BF16), tm=TOKEN_TILE)

    rank, counts = _rank(idx, tm=TOKEN_TILE)
    t = EXPERT_ROWS
    cnt = counts[:, 0].astype(I32)
    padded = (cnt + t - 1) // t * t
    pends = jnp.cumsum(padded)
    pstart = pends - padded
    n_rows = n * TOP_K + N_EXPERTS * t
    pos = _pos(idx, rank, pstart.astype(F32)[:, None], tm=min(POS_TILE, n))

    pos_flat = pos.reshape(1, TOP_K * n)
    xs_a, xs_b = _sc_scatter_rows((hp_a, hp_b), pos_flat, n_rows)
    ys_a, ys_b = _experts(pstart, cnt, pends[-1:] // t, xs_a, xs_b, we_gate[0], we_up[0], we_down[0])
    yg_a, yg_b = [y.reshape(TOP_K, n, d // 4) for y in _sc_gather_rows((ys_a, ys_b), pos_flat)]
    out = _finish(gates, xres.reshape(n, d), mod, yg_a, yg_b, tm=TOKEN_TILE, tiles_per_batch=s // TOKEN_TILE)
    return out.reshape(bsz, s, d)
```

```python
import functools

import numpy as np
import jax
import jax.numpy as jnp
from jax import lax
from jax.experimental import pallas as pl
from jax.experimental.pallas import tpu as pltpu
from jax.experimental.pallas import tpu_sc as plsc

F32 = jnp.float32
BF16 = jnp.bfloat16
I32 = jnp.int32
U32 = jnp.uint32

LANES = 128
HEAD_DIM = 64
GRID_W = 64
A_Q_HEADS = 8
A_KV_HEADS = 2
A_WINDOW = 128
B_HEADS = 8
NA_KH = 8
NA_KW = 16
NA_QROWS = 4
NA_KROWS = NA_QROWS + NA_KH
ROPE_BASE = 10000.0
N_EXPERTS = 256
TOP_K = 8
N_GROUPS = 8
TOPK_GROUPS = 4
ROUTED_SCALE = 2.5
LOG2E = 1.4426950408889634
EPS = 1e-6
NEG_INF = -1e30
TOKEN_TILE = 512
POS_TILE = 2048
EXPERT_ROWS = 256
SC_WINDOW = 128
VMEM_LIMIT = 56 * 1024 * 1024

_NT = (((1,), (1,)), ((), ()))


def _params(*sem):
    return pltpu.CompilerParams(dimension_semantics=sem, vmem_limit_bytes=VMEM_LIMIT)


def _silu(v):
    return v * jax.nn.sigmoid(v)


def _rms(v, gain):
    return v * lax.rsqrt(jnp.mean(v * v, axis=-1, keepdims=True) + EPS) * gain


def _pack_bf16_pairs(v):
    n = v.shape[1] // 2
    lo = lax.bitcast_convert_type(v[:, :n].astype(BF16).astype(F32), U32) >> 16
    hi = lax.bitcast_convert_type(v[:, n:].astype(BF16).astype(F32), U32) & jnp.uint32(0xFFFF0000)
    return hi | lo


def _split_bf16(v):
    hi = lax.bitcast_convert_type(lax.bitcast_convert_type(v, U32) & jnp.uint32(0xFFFF0000), F32)
    return hi.astype(BF16), (v - hi).astype(BF16)


def _unpack_bf16_pairs(w):
    lo = lax.bitcast_convert_type(w << 16, F32)
    hi = lax.bitcast_convert_type(w & jnp.uint32(0xFFFF0000), F32)
    return jnp.concatenate([lo, hi], axis=1)


def _ada_kernel(c_ref, w_ref, b_ref, o_ref):
    a = _silu(c_ref[...])
    o_ref[...] = jnp.dot(a, w_ref[...], preferred_element_type=F32,
                         precision=lax.Precision.HIGHEST) + b_ref[...]


def _ada(c8, w, b):
    d, n = w.shape
    bn = n // 4
    return pl.pallas_call(
        _ada_kernel,
        out_shape=jax.ShapeDtypeStruct((8, n), F32),
        grid=(n // bn,),
        in_specs=[pl.BlockSpec((8, d), lambda j: (0, 0)),
                  pl.BlockSpec((d, bn), lambda j: (0, j)),
                  pl.BlockSpec((1, bn), lambda j: (0, j))],
        out_specs=pl.BlockSpec((8, bn), lambda j: (0, j)),
        compiler_params=_params("arbitrary"),
        name="ada",
    )(c8, w, b)


_QA, _QB, _KB, _VB, _KA, _VA = 0, 4, 8, 12, 16, 17
_OUT_BLOCKS = 18


def _inproj_kernel(x_ref, mod_ref, n1_ref, w_ref, g_ref, cos_ref, sin_ref, bd_ref, o_ref, *, mod_row, d):
    b = pl.program_id(0) if mod_row is None else mod_row
    xn = _rms(x_ref[0], n1_ref[...])
    sh = mod_ref[pl.ds(b, 1), 0:d]
    sc = mod_ref[pl.ds(b, 1), d:2 * d]
    h = (xn * (1.0 + sc) + sh).astype(BF16)
    z = jnp.dot(h, w_ref[...], preferred_element_type=F32)
    bd = bd_ref[...]
    cos = cos_ref[...]
    sin = sin_ref[...]

    def blk(j):
        return z[:, j * LANES:(j + 1) * LANES]

    def head_rinv(zb):
        ms = jnp.dot((zb * zb).astype(BF16), bd, preferred_element_type=F32)
        return lax.rsqrt(ms + EPS)

    def put(j, v):
        o_ref[0, :, j * LANES:(j + 1) * LANES] = v.astype(BF16)

    quarter = HEAD_DIM // 4
    first_half = (lax.broadcasted_iota(I32, (1, LANES), 1) % (2 * quarter)) < quarter

    def roped(j, g_row):
        zb = blk(j)
        zn = zb * head_rinv(zb) * g_ref[g_row:g_row + 1, :]
        partner = jnp.where(first_half, pltpu.roll(zn, LANES - quarter, 1), pltpu.roll(zn, quarter, 1))
        put(j, zn * cos + partner * sin)

    def normed(j, g_row):
        zb = blk(j)
        put(j, zb * head_rinv(zb) * g_ref[g_row:g_row + 1, :])

    for j in range(4):
        roped(_QA + j, 0)
        normed(_QB + j, 1)
        normed(_KB + j, 2)
        put(_VB + j, blk(_VB + j))
    roped(_KA, 3)
    put(_VA, blk(_VA))


def _inproj(x, mod, n1, w_ext, gains, cos_t, sin_t, bd, *, mod_row, tm):
    bsz, s, d = x.shape
    kern = functools.partial(_inproj_kernel, mod_row=mod_row, d=d)
    return pl.pallas_call(
        kern,
        out_shape=jax.ShapeDtypeStruct((bsz, s, _OUT_BLOCKS * LANES), BF16),
        grid=(bsz, s // tm),
        in_specs=[pl.BlockSpec((1, tm, d), lambda b, i: (b, i, 0)),
                  pl.BlockSpec(mod.shape, lambda b, i: (0, 0)),
                  pl.BlockSpec((1, d), lambda b, i: (0, 0)),
                  pl.BlockSpec(w_ext.shape, lambda b, i: (0, 0)),
                  pl.BlockSpec(gains.shape, lambda b, i: (0, 0)),
                  pl.BlockSpec((tm, LANES), lambda b, i: (i, 0)),
                  pl.BlockSpec((tm, LANES), lambda b, i: (i, 0)),
                  pl.BlockSpec(bd.shape, lambda b, i: (0, 0))],
        out_specs=pl.BlockSpec((1, tm, _OUT_BLOCKS * LANES), lambda b, i: (b, i, 0)),
        compiler_params=_params("arbitrary", "arbitrary"),
        name="inproj",
    )(x, mod, n1, w_ext, gains, cos_t, sin_t, bd)


def _split_pair(qp, lo):
    zero = jnp.zeros_like(qp)
    return jnp.concatenate([jnp.where(lo, qp, zero), jnp.where(lo, zero, qp)], axis=0)


def _softmax_pv(s_parts, v_parts, sink_rep):
    chunks = [s[:, c * LANES:(c + 1) * LANES] for s in s_parts for c in range(s.shape[1] // LANES)]
    m = jnp.max(functools.reduce(jnp.maximum, chunks), axis=-1, keepdims=True)
    m_rep = jnp.broadcast_to(m, (m.shape[0], LANES))
    if sink_rep is not None:
        m_rep = jnp.maximum(m_rep, sink_rep)
    acc = None
    for s, v in zip(s_parts, v_parts):
        p = jnp.concatenate([jnp.exp2(s[:, c * LANES:(c + 1) * LANES] - m_rep)
                             for c in range(s.shape[1] // LANES)], axis=1).astype(BF16)
        v_ext = jnp.concatenate([v, jnp.ones_like(v)], axis=1)
        o = jnp.dot(p, v_ext, preferred_element_type=F32)
        acc = o if acc is None else acc + o
    l_rep = acc[:, LANES:]
    if sink_rep is not None:
        l_rep = l_rep + jnp.exp2(sink_rep - m_rep)
    return acc[:, :LANES] * (1.0 / l_rep)


A_BLOCKS_PER_STEP = 8


def _attn_a_kernel(q_ref, *refs):
    nq = A_BLOCKS_PER_STEP
    k_refs, v_refs = refs[:nq + 2], refs[nq + 2:2 * nq + 4]
    kx_ref, vx_ref, sink_ref = refs[2 * nq + 4:2 * nq + 7]
    mask_refs = refs[2 * nq + 7:3 * nq + 7]
    o_ref = refs[-1]
    tq = A_WINDOW
    lo = lax.broadcasted_iota(I32, (tq, LANES), 1) < HEAD_DIM
    for h in range(nq):
        q = q_ref[0, h * tq:(h + 1) * tq]
        qs = jnp.concatenate([_split_pair(q[:, p * LANES:(p + 1) * LANES], lo) for p in range(4)], axis=0)
        k = jnp.concatenate([r[0] for r in k_refs[h:h + 3]], axis=0)
        v = jnp.concatenate([r[0] for r in v_refs[h:h + 3]], axis=0)
        s_loc = lax.dot_general(qs, k, _NT, preferred_element_type=F32) + mask_refs[h][0]
        s_ctx = lax.dot_general(qs, kx_ref[0], _NT, preferred_element_type=F32)
        o = _softmax_pv([s_loc, s_ctx], [v, vx_ref[0]], sink_ref[...])
        for p in range(4):
            o_lo = o[(2 * p) * tq:(2 * p + 1) * tq]
            o_hi = o[(2 * p + 1) * tq:(2 * p + 2) * tq]
            o_ref[0, h * tq:(h + 1) * tq, p * LANES:(p + 1) * LANES] = jnp.where(lo, o_lo, o_hi).astype(BF16)


def _attn_a(zq, zc, sink_rep, mask):
    bsz, s, _ = zq.shape
    lc = zc.shape[1]
    tq = A_WINDOW
    nq = A_BLOCKS_PER_STEP
    nblk = s // tq
    ka, va = _KA, _VA

    def kv_spec(col, shift):
        return pl.BlockSpec((1, tq, LANES), lambda b, j: (b, jnp.clip(nq * j + shift, 0, nblk - 1), col))

    def mask_spec(h):
        def variant(j):
            blk = nq * j + h
            return jnp.where(blk == 0, 0, jnp.where(blk == nblk - 1, 2, 1))
        return pl.BlockSpec((1,) + mask.shape[1:], lambda b, j: (variant(j), 0, 0))

    shifts = range(-1, nq + 1)
    return pl.pallas_call(
        _attn_a_kernel,
        out_shape=jax.ShapeDtypeStruct((bsz, s, 4 * LANES), BF16),
        grid=(bsz, nblk // nq),
        in_specs=([pl.BlockSpec((1, nq * tq, 4 * LANES), lambda b, j: (b, j, 0))]
                  + [kv_spec(ka, sh) for sh in shifts] + [kv_spec(va, sh) for sh in shifts]
                  + [pl.BlockSpec((1, lc, LANES), lambda b, j: (b, 0, ka)),
                     pl.BlockSpec((1, lc, LANES), lambda b, j: (b, 0, va)),
                     pl.BlockSpec(sink_rep.shape, lambda b, j: (0, 0))]
                  + [mask_spec(h) for h in range(nq)]),
        out_specs=pl.BlockSpec((1, nq * tq, 4 * LANES), lambda b, j: (b, j, 0)),
        compiler_params=_params("arbitrary", "arbitrary"),
        name="attn_a",
    )(*([zq] * (2 * nq + 5) + [zc, zc, sink_rep] + [mask] * nq))


def _attn_a_mask(tq):
    qi = np.arange(tq)[:, None]
    kj = np.arange(3 * tq)[None, :]
    ok = (kj >= qi) & (kj <= qi + 2 * tq)
    variants = [ok & (kj >= tq), ok, ok & (kj < 2 * tq)]
    m = np.stack([np.tile(np.where(v, 0.0, NEG_INF).astype(np.float32), (A_Q_HEADS, 1)) for v in variants])
    return jnp.asarray(m)


NA_GROUPS_PER_STEP = 4


def _attn_b_kernel(q_ref, *refs):
    ngs = NA_GROUPS_PER_STEP
    nkv = ngs + 2
    k_refs, v_refs = refs[:nkv], refs[nkv:2 * nkv]
    kx_ref, vx_ref = refs[2 * nkv:2 * nkv + 2]
    tab_refs = refs[2 * nkv + 2:2 * nkv + 2 + ngs]
    o_ref = refs[-1]
    tq = NA_QROWS * GRID_W
    lo = lax.broadcasted_iota(I32, (tq, LANES), 1) < HEAD_DIM
    for g in range(ngs):
        rows_g = slice(g * tq, (g + 1) * tq)
        for p in range(4):
            sl = slice(p * LANES, (p + 1) * LANES)
            qs = _split_pair(q_ref[0, rows_g, sl], lo)
            k = jnp.concatenate([r[0, :, sl] for r in k_refs[g:g + 3]], axis=0)
            v = jnp.concatenate([r[0, :, sl] for r in v_refs[g:g + 3]], axis=0)
            s_loc = lax.dot_general(qs, k, _NT, preferred_element_type=F32) + tab_refs[g][0, p]
            s_ctx = lax.dot_general(qs, kx_ref[0, :, sl], _NT, preferred_element_type=F32)
            o = _softmax_pv([s_loc, s_ctx], [v, vx_ref[0, :, sl]], None)
            o_ref[0, rows_g, sl] = jnp.where(lo, o[:tq], o[tq:]).astype(BF16)


def _attn_b(zq, zc, table):
    bsz, s, _ = zq.shape
    lc = zc.shape[1]
    tq = NA_QROWS * GRID_W
    ngs = NA_GROUPS_PER_STEP
    ng = s // tq
    qb, kb, vb = _QB // 4, _KB // 4, _VB // 4

    def kv_spec(col, j):
        return pl.BlockSpec((1, tq, 4 * LANES), lambda i, b: (b, jnp.clip(ngs * i + j - 1, 0, ng - 1), col))

    def tab_spec(g):
        def variant(i):
            grp = ngs * i + g
            return jnp.where(grp == 0, 0, jnp.where(grp == ng - 1, 2, 1))
        return pl.BlockSpec((1,) + table.shape[1:], lambda i, b: (variant(i), 0, 0, 0),
                            pipeline_mode=pl.Buffered(1))

    kv_slots = range(ngs + 2)
    return pl.pallas_call(
        _attn_b_kernel,
        out_shape=jax.ShapeDtypeStruct((bsz, s, 4 * LANES), BF16),
        grid=(ng // ngs, bsz),
        in_specs=([pl.BlockSpec((1, ngs * tq, 4 * LANES), lambda i, b: (b, i, qb))]
                  + [kv_spec(kb, j) for j in kv_slots] + [kv_spec(vb, j) for j in kv_slots]
                  + [pl.BlockSpec((1, lc, 4 * LANES), lambda i, b: (b, 0, kb)),
                     pl.BlockSpec((1, lc, 4 * LANES), lambda i, b: (b, 0, vb))]
                  + [tab_spec(g) for g in range(ngs)]),
        out_specs=pl.BlockSpec((1, ngs * tq, 4 * LANES), lambda i, b: (b, i, 0)),
        compiler_params=_params("arbitrary", "arbitrary"),
        name="attn_b",
    )(*([zq] * (2 * ngs + 5) + [zc, zc] + [table] * ngs))


def _na_table(rpb, rows):
    ng = rows // NA_QROWS
    nq, nk = NA_QROWS * GRID_W, NA_KROWS * GRID_W
    qc = np.arange(GRID_W)
    kc = np.arange(GRID_W)
    ws = np.clip(qc - NA_KW // 2, 0, GRID_W - NA_KW)
    valid_c = (kc[None, :] >= ws[:, None]) & (kc[None, :] < ws[:, None] + NA_KW)
    dc = np.clip(kc[None, :] - qc[:, None], -(NA_KW - 1), NA_KW - 1) + (NA_KW - 1)
    c_sel = (dc[..., None] == np.arange(2 * NA_KW - 1)) & valid_c[..., None]
    tiles = jnp.einsum('hab,uvb->huav', rpb.astype(F32), jnp.asarray(c_sel, F32), precision=lax.Precision.HIGHEST)
    tiles = jnp.where(jnp.asarray(valid_c)[None, :, None, :], tiles * LOG2E, NEG_INF)
    tiles = tiles.reshape(B_HEADS, GRID_W, (2 * NA_KH - 1) * GRID_W)

    def masked(n_key_rows):
        return jnp.full((B_HEADS, GRID_W, n_key_rows * GRID_W), NEG_INF, F32)

    tabs = []
    for i in (0, 1, ng - 1):
        start = NA_QROWS * i - NA_KH // 2
        q_rows = []
        for qr in range(NA_QROWS):
            r = NA_QROWS * i + qr
            rs = int(np.clip(r - NA_KH // 2, 0, rows - NA_KH))
            y0 = rs - start
            d0 = rs - r + (NA_KH - 1)
            q_rows.append(jnp.concatenate([masked(y0), tiles[:, :, d0 * GRID_W:(d0 + NA_KH) * GRID_W],
                                           masked(NA_KROWS - NA_KH - y0)], axis=-1))
        tabs.append(jnp.concatenate(q_rows, axis=-2))
    return jnp.stack(tabs).reshape(3, B_HEADS // 2, 2 * nq, nk)


def _merge_kernel(oa_ref, ob_ref, x_ref, mod_ref, ga_ref, gb_ref, wo_ref, n2_ref, wrh_ref, wrl_ref, rb_ref,
                  wsg_ref, wsu_ref, wsd_ref, xres_ref, hpa_ref, hpb_ref, idx_ref, gate_ref, *, d):
    b = pl.program_id(0)
    tm = x_ref.shape[1]

    def mod(k):
        return mod_ref[pl.ds(b, 1), k * d:(k + 1) * d]

    na = _rms(oa_ref[0].astype(F32), ga_ref[...])
    nb = _rms(ob_ref[0].astype(F32), gb_ref[...])
    cat = jnp.concatenate([na, nb], axis=1).astype(BF16)
    y = jnp.dot(cat, wo_ref[...], preferred_element_type=F32)
    x1 = x_ref[0] + mod(2) * y
    h2 = _rms(x1, n2_ref[...]) * (1.0 + mod(4)) + mod(3)

    hb = h2.astype(BF16)
    act = _silu(jnp.dot(hb, wsg_ref[...], preferred_element_type=F32)) * jnp.dot(hb, wsu_ref[...],
                                                                                 preferred_element_type=F32)
    shared = jnp.dot(act.astype(BF16), wsd_ref[...], preferred_element_type=F32)
    xres_ref[0] = x1 + mod(5) * shared
    packed = _pack_bf16_pairs(h2)
    hpa_ref[...] = packed[:, :d // 4]
    hpb_ref[...] = packed[:, d // 4:]

    h_hi, h_lo = _split_bf16(h2)
    logits = (lax.dot_general(wrh_ref[...], h_hi, _NT, preferred_element_type=F32)
              + lax.dot_general(wrh_ref[...], h_lo, _NT, preferred_element_type=F32)
              + lax.dot_general(wrl_ref[...], h_hi, _NT, preferred_element_type=F32))
    scores = jax.nn.sigmoid(logits)
    sel = scores + rb_ref[...]
    per = N_EXPERTS // N_GROUPS
    g3 = sel.reshape(N_GROUPS, per, tm)
    it3 = lax.broadcasted_iota(I32, (N_GROUPS, per, tm), 1)
    m1 = jnp.max(g3, axis=1, keepdims=True)
    first = jnp.min(jnp.where(g3 == m1, it3, per), axis=1, keepdims=True)
    m2 = jnp.max(jnp.where(it3 == first, -jnp.inf, g3), axis=1, keepdims=True)
    gscore = (m1 + m2).reshape(N_GROUPS, tm)

    itg = lax.broadcasted_iota(I32, (N_GROUPS, tm), 0)
    gsel = jnp.zeros((N_GROUPS, tm), F32)
    cur = gscore
    for _ in range(TOPK_GROUPS):
        mx = jnp.max(cur, axis=0, keepdims=True)
        fi = jnp.min(jnp.where(cur == mx, itg, N_GROUPS), axis=0, keepdims=True)
        pick = itg == fi
        gsel = jnp.where(pick, 1.0, gsel)
        cur = jnp.where(pick, -jnp.inf, cur)
    emask = jnp.broadcast_to(gsel.reshape(N_GROUPS, 1, tm), (N_GROUPS, per, tm)).reshape(N_EXPERTS, tm) > 0.5

    ite = lax.broadcasted_iota(I32, (N_EXPERTS, tm), 0)
    cur = jnp.where(emask, sel, NEG_INF)
    idx_rows, s_rows = [], []
    for _ in range(TOP_K):
        mx = jnp.max(cur, axis=0, keepdims=True)
        fi = jnp.min(jnp.where(cur == mx, ite, N_EXPERTS), axis=0, keepdims=True)
        pick = ite == fi
        idx_rows.append(fi)
        s_rows.append(jnp.sum(jnp.where(pick, scores, 0.0), axis=0, keepdims=True))
        cur = jnp.where(pick, -jnp.inf, cur)
    top_s = jnp.concatenate(s_rows, axis=0)
    idx_ref[...] = jnp.concatenate(idx_rows, axis=0)
    gate_ref[...] = top_s / jnp.sum(top_s, axis=0, keepdims=True) * ROUTED_SCALE


def _merge(o_a, o_b, x, mod, ga, gb, wo, n2, wr_hi, wr_lo, rb_col, wsg, wsu, wsd, *, tm):
    bsz, s, d = x.shape
    nt = s // tm
    n = bsz * s
    full = lambda a: pl.BlockSpec(a.shape, lambda b, i: (0,) * a.ndim)
    return pl.pallas_call(
        functools.partial(_merge_kernel, d=d),
        out_shape=(jax.ShapeDtypeStruct((bsz, s, d), F32),
                   jax.ShapeDtypeStruct((n, d // 4), U32),
                   jax.ShapeDtypeStruct((n, d // 4), U32),
                   jax.ShapeDtypeStruct((TOP_K, n), I32),
                   jax.ShapeDtypeStruct((TOP_K, n), F32)),
        grid=(bsz, nt),
        in_specs=[pl.BlockSpec((1, tm, d // 2), lambda b, i: (b, i, 0)),
                  pl.BlockSpec((1, tm, d // 2), lambda b, i: (b, i, 0)),
                  pl.BlockSpec((1, tm, d), lambda b, i: (b, i, 0)),
                  full(mod), full(ga), full(gb), full(wo), full(n2), full(wr_hi), full(wr_lo), full(rb_col),
                  full(wsg), full(wsu), full(wsd)],
        out_specs=(pl.BlockSpec((1, tm, d), lambda b, i: (b, i, 0)),
                   pl.BlockSpec((tm, d // 4), lambda b, i: (b * nt + i, 0)),
                   pl.BlockSpec((tm, d // 4), lambda b, i: (b * nt + i, 0)),
                   pl.BlockSpec((TOP_K, tm), lambda b, i: (0, b * nt + i)),
                   pl.BlockSpec((TOP_K, tm), lambda b, i: (0, b * nt + i))),
        compiler_params=_params("arbitrary", "arbitrary"),
        name="merge",
    )(o_a, o_b, x, mod, ga, gb, wo, n2, wr_hi, wr_lo, rb_col, wsg, wsu, wsd)


def _rank_kernel(idx_ref, rank_ref, cnt_ref, carry_ref):
    tm = idx_ref.shape[1]

    @pl.when(pl.program_id(0) == 0)
    def _():
        carry_ref[...] = jnp.zeros_like(carry_ref)

    idx = idx_ref[...]
    ite = lax.broadcasted_iota(I32, (N_EXPERTS, tm), 0)
    before = (lax.broadcasted_iota(I32, (tm, tm), 0) < lax.broadcasted_iota(I32, (tm, tm), 1)).astype(BF16)
    hits = [ite == idx[k:k + 1, :] for k in range(TOP_K)]
    routed = jnp.where(functools.reduce(jnp.logical_or, hits), 1.0, 0.0)
    base = carry_ref[...]
    ahead = jnp.dot(routed.astype(BF16), before, preferred_element_type=F32) + base
    rows = [jnp.sum(jnp.where(hit, ahead, 0.0), axis=0, keepdims=True) for hit in hits]
    rank_ref[...] = jnp.concatenate(rows, axis=0).astype(I32)
    total = base + jnp.sum(routed, axis=1, keepdims=True)
    carry_ref[...] = total
    cnt_ref[...] = total


def _rank(idx, *, tm):
    n = idx.shape[1]
    return pl.pallas_call(
        _rank_kernel,
        out_shape=(jax.ShapeDtypeStruct((TOP_K, n), I32), jax.ShapeDtypeStruct((N_EXPERTS, 1), F32)),
        grid=(n // tm,),
        in_specs=[pl.BlockSpec((TOP_K, tm), lambda i: (0, i))],
        out_specs=(pl.BlockSpec((TOP_K, tm), lambda i: (0, i)), pl.BlockSpec((N_EXPERTS, 1), lambda i: (0, 0))),
        scratch_shapes=[pltpu.VMEM((N_EXPERTS, 1), F32)],
        compiler_params=_params("arbitrary"),
        name="rank",
    )(idx)


def _pos_kernel(idx_ref, rank_ref, pstart_ref, pos_ref):
    tm = idx_ref.shape[1]
    idx = idx_ref[...]
    ite = lax.broadcasted_iota(I32, (N_EXPERTS, tm), 0)
    pstart = pstart_ref[...]
    rows = [jnp.sum(jnp.where(ite == idx[k:k + 1, :], pstart, 0.0), axis=0, keepdims=True) for k in range(TOP_K)]
    pos_ref[...] = jnp.concatenate(rows, axis=0).astype(I32) + rank_ref[...]


def _pos(idx, rank, pstart_col, *, tm):
    n = idx.shape[1]
    return pl.pallas_call(
        _pos_kernel,
        out_shape=jax.ShapeDtypeStruct((TOP_K, n), I32),
        grid=(n // tm,),
        in_specs=[pl.BlockSpec((TOP_K, tm), lambda i: (0, i)),
                  pl.BlockSpec((TOP_K, tm), lambda i: (0, i)),
                  pl.BlockSpec((N_EXPERTS, 1), lambda i: (0, 0))],
        out_specs=pl.BlockSpec((TOP_K, tm), lambda i: (0, i)),
        compiler_params=_params("arbitrary"),
        name="pos",
    )(idx, rank, pstart_col)


UNIT_CHUNKS = (4, 2, 1)
RING_AHEAD = 16
RING_SLOTS = RING_AHEAD + UNIT_CHUNKS[0]


def _expert_kernel(ps_ref, cnt_ref, tot_ref, wg_ref, wu_ref, wd_ref, xa_hbm, xb_hbm, ya_hbm, yb_hbm,
                   xa_buf, xb_buf, ya_buf, yb_buf, in_sem, out_sem):
    e = pl.program_id(0)
    r = EXPERT_ROWS
    ahead = RING_AHEAD
    total = tot_ref[0]
    first = ps_ref[e] // r
    cnt = cnt_ref[e]
    nch = (cnt + r - 1) // r

    def rows_of(g):
        return pl.ds(pl.multiple_of(g * r, r), r)

    def fetch(g):
        slot = g % RING_SLOTS
        return (pltpu.make_async_copy(xa_hbm.at[rows_of(g)], xa_buf.at[slot], in_sem.at[0, slot]),
                pltpu.make_async_copy(xb_hbm.at[rows_of(g)], xb_buf.at[slot], in_sem.at[1, slot]))

    def flush(g):
        slot = g % RING_SLOTS
        return (pltpu.make_async_copy(ya_buf.at[slot], ya_hbm.at[rows_of(g)], out_sem.at[0, slot]),
                pltpu.make_async_copy(yb_buf.at[slot], yb_hbm.at[rows_of(g)], out_sem.at[1, slot]))

    def start(copies):
        for cp in copies:
            cp.start()

    def wait(copies):
        for cp in copies:
            cp.wait()

    @pl.when(e == 0)
    def _():
        for g in range(ahead):
            @pl.when(g < total)
            def _():
                start(fetch(g))

    @pl.when(nch > 0)
    def _():
        def unit(c0, u):
            for j in range(u):
                g = first + c0 + j
                wait(fetch(g))

                @pl.when(g + ahead < total)
                def _():
                    start(fetch(g + ahead))

                @pl.when(g >= RING_SLOTS)
                def _():
                    wait(flush(g - RING_SLOTS))

            slots = [(first + c0 + j) % RING_SLOTS for j in range(u)]
            packed = jnp.concatenate([jnp.concatenate([xa_buf[sl], xb_buf[sl]], axis=1) for sl in slots], axis=0)
            row = lax.broadcasted_iota(I32, packed.shape, 0)
            packed = jnp.where(row < cnt - c0 * r, packed, jnp.zeros_like(packed))
            x = _unpack_bf16_pairs(packed).astype(BF16)
            gate = jnp.dot(x, wg_ref[0].astype(BF16), preferred_element_type=F32)
            up = jnp.dot(x, wu_ref[0].astype(BF16), preferred_element_type=F32)
            y = jnp.dot((_silu(gate) * up).astype(BF16), wd_ref[0].astype(BF16), preferred_element_type=F32)
            out = _pack_bf16_pairs(y)
            half = out.shape[1] // 2
            for j, sl in enumerate(slots):
                ya_buf[sl] = out[j * r:(j + 1) * r, :half]
                yb_buf[sl] = out[j * r:(j + 1) * r, half:]
                start(flush(first + c0 + j))

        big = UNIT_CHUNKS[0]

        def big_unit(i, carry):
            unit(i * big, big)
            return carry

        lax.fori_loop(0, nch // big, big_unit, 0)
        done = nch // big * big
        for u in UNIT_CHUNKS[1:]:
            @pl.when((nch % (2 * u)) >= u)
            def _():
                unit(done, u)

            done = done + jnp.where((nch % (2 * u)) >= u, u, 0)

    @pl.when(e == pl.num_programs(0) - 1)
    def _():
        for k in range(RING_SLOTS):
            @pl.when(total - 1 - k >= 0)
            def _():
                wait(flush(total - 1 - k))


def _experts(pstart, cnt, total_chunks, xs_a, xs_b, wg, wu, wd):
    rows, hw = xs_a.shape
    r = EXPERT_ROWS
    n_exp, d, f = wg.shape
    hbm = pl.BlockSpec(memory_space=pl.ANY)
    grid_spec = pltpu.PrefetchScalarGridSpec(
        num_scalar_prefetch=3,
        grid=(n_exp,),
        in_specs=[pl.BlockSpec((1, d, f), lambda e, ps, cn, tot: (e, 0, 0)),
                  pl.BlockSpec((1, d, f), lambda e, ps, cn, tot: (e, 0, 0)),
                  pl.BlockSpec((1, f, d), lambda e, ps, cn, tot: (e, 0, 0)),
                  hbm, hbm],
        out_specs=(hbm, hbm),
        scratch_shapes=[pltpu.VMEM((RING_SLOTS, r, hw), U32), pltpu.VMEM((RING_SLOTS, r, hw), U32),
                        pltpu.VMEM((RING_SLOTS, r, hw), U32), pltpu.VMEM((RING_SLOTS, r, hw), U32),
                        pltpu.SemaphoreType.DMA((2, RING_SLOTS)), pltpu.SemaphoreType.DMA((2, RING_SLOTS))],
    )
    return pl.pallas_call(
        _expert_kernel,
        out_shape=(jax.ShapeDtypeStruct((rows, hw), U32), jax.ShapeDtypeStruct((rows, hw), U32)),
        grid_spec=grid_spec,
        compiler_params=_params("arbitrary"),
        name="experts",
    )(pstart, cnt, total_chunks, wg, wu, wd, xs_a, xs_b)


def _sc_scatter_rows(xs, idx_flat, n_rows):
    n, w = xs[0].shape
    m = idx_flat.shape[1]
    nwin = n // SC_WINDOW
    reps = m // n
    mesh = plsc.VectorSubcoreMesh(core_axis_name="core", subcore_axis_name="subcore")
    out_type = tuple(jax.ShapeDtypeStruct((n_rows, w), x.dtype) for x in xs)

    @functools.partial(pl.kernel, out_type=out_type, mesh=mesh, scratch_types=[])
    def scatter_kernel(*refs):
        x_hbms, i_hbm, o_hbms = refs[:len(xs)], refs[len(xs)], refs[len(xs) + 1:]
        for x_hbm, o_hbm in zip(x_hbms, o_hbms):
            def body(x_vmem, i_vmem, o_hbm=o_hbm):
                pltpu.sync_copy(x_vmem, o_hbm.at[i_vmem.at[0]])

            pltpu.emit_pipeline(
                body,
                grid=(nwin, reps),
                in_specs=[pl.BlockSpec((SC_WINDOW, w), lambda i, k: (i, 0)),
                          pl.BlockSpec((1, SC_WINDOW), lambda i, k: (0, k * nwin + i))],
                out_specs=[],
                core_axis_name=("core", "subcore"),
                dimension_semantics=(pltpu.PARALLEL, pltpu.ARBITRARY),
            )(x_hbm, i_hbm)

    return scatter_kernel(*xs, idx_flat)


def _sc_gather_rows(srcs, idx_flat):
    m = idx_flat.shape[1]
    w = srcs[0].shape[1]
    mesh = plsc.VectorSubcoreMesh(core_axis_name="core", subcore_axis_name="subcore")
    out_type = tuple(jax.ShapeDtypeStruct((m, w), src.dtype) for src in srcs)

    @functools.partial(pl.kernel, out_type=out_type, mesh=mesh)
    def gather_kernel(*refs):
        x_hbms, i_hbm, o_hbms = refs[:len(srcs)], refs[len(srcs)], refs[len(srcs) + 1:]
        for x_hbm, o_hbm in zip(x_hbms, o_hbms):
            def body(i_vmem, o_vmem, x_hbm=x_hbm):
                pltpu.sync_copy(x_hbm.at[i_vmem.at[0]], o_vmem)

            pltpu.emit_pipeline(
                body,
                grid=(m // SC_WINDOW,),
                in_specs=[pl.BlockSpec((1, SC_WINDOW), lambda i: (0, i))],
                out_specs=[pl.BlockSpec((SC_WINDOW, w), lambda i: (i, 0))],
                core_axis_name=("core", "subcore"),
                dimension_semantics=(pltpu.PARALLEL,),
            )(i_hbm, o_hbm)

    return gather_kernel(*srcs, idx_flat)


def _finish_kernel(gate_ref, xres_ref, mod_ref, ya_ref, yb_ref, o_ref, *, d, tiles_per_batch):
    tm = gate_ref.shape[1]
    b = pl.program_id(0) // tiles_per_batch
    gates = gate_ref[...].T
    acc = jnp.zeros((tm, d), F32)
    for k in range(TOP_K):
        packed = jnp.concatenate([ya_ref[k], yb_ref[k]], axis=1)
        acc = acc + _unpack_bf16_pairs(packed) * gates[:, k:k + 1]
    o_ref[...] = xres_ref[...] + mod_ref[pl.ds(b, 1), 5 * d:6 * d] * acc


def _finish(gates, xres, mod, yg_a, yg_b, *, tm, tiles_per_batch):
    n, d = xres.shape
    return pl.pallas_call(
        functools.partial(_finish_kernel, d=d, tiles_per_batch=tiles_per_batch),
        out_shape=jax.ShapeDtypeStruct((n, d), F32),
        grid=(n // tm,),
        in_specs=[pl.BlockSpec((TOP_K, tm), lambda i: (0, i)),
                  pl.BlockSpec((tm, d), lambda i: (i, 0)),
                  pl.BlockSpec(mod.shape, lambda i: (0, 0)),
                  pl.BlockSpec((TOP_K, tm, d // 4), lambda i: (0, i, 0)),
                  pl.BlockSpec((TOP_K, tm, d // 4), lambda i: (0, i, 0))],
        out_specs=pl.BlockSpec((tm, d), lambda i: (i, 0)),
        compiler_params=_params("arbitrary"),
        name="finish",
    )(gates, xres, mod, yg_a, yg_b)


_QA_HEAD_ORDER = (0, 4, 1, 5, 2, 6, 3, 7)


def _reorder_qa_heads(a, axis):
    return jnp.concatenate([lax.slice_in_dim(a, h * HEAD_DIM, (h + 1) * HEAD_DIM, axis=axis)
                            for h in _QA_HEAD_ORDER], axis=axis)


def _rope_tables(s):
    quarter = HEAD_DIM // 4
    t = jnp.arange(s)
    row = (t // GRID_W).astype(F32)
    col = (t % GRID_W).astype(F32)
    freqs = ROPE_BASE ** (-jnp.arange(quarter, dtype=F32) / quarter)
    ar = row[:, None] * freqs[None, :]
    ac = col[:, None] * freqs[None, :]
    cos = jnp.concatenate([jnp.cos(ar), jnp.cos(ar), jnp.cos(ac), jnp.cos(ac)], axis=1)
    sin = jnp.concatenate([-jnp.sin(ar), jnp.sin(ar), -jnp.sin(ac), jnp.sin(ac)], axis=1)
    return jnp.tile(cos, (1, 2)), jnp.tile(sin, (1, 2))


def kernel(x, c, ctx, c_ctx, w_ada, b_ada, norm1, norm2, w_in, q_norm_a, k_norm_a, q_norm_b, k_norm_b, sink_a,
           rpb_b, out_norm_a, out_norm_b, w_out, w_router, router_bias, we_gate, we_up, we_down, ws_gate, ws_up,
           ws_down):
    assert w_ada.shape[0] == 1, "single-layer block"
    bsz, s, d = x.shape
    lc = ctx.shape[1]
    n = bsz * s
    rows = s // GRID_W
    assert d == 1024 and bsz + 1 <= 8 and rows >= NA_KROWS and lc % LANES == 0
    assert s % TOKEN_TILE == 0 and n % SC_WINDOW == 0
    assert (s // A_WINDOW) % A_BLOCKS_PER_STEP == 0 and (s // (NA_QROWS * GRID_W)) % NA_GROUPS_PER_STEP == 0

    c8 = jnp.concatenate([c, c_ctx[None, :], jnp.zeros((8 - bsz - 1, d), F32)], axis=0)
    mod = _ada(c8, w_ada[0], b_ada[0][None, :])

    w = w_in[0]
    aw, akw, bw = A_Q_HEADS * HEAD_DIM, A_KV_HEADS * HEAD_DIM, B_HEADS * HEAD_DIM
    cuts = np.cumsum([0, aw, akw, akw, bw, bw, bw])
    qa_w, ka_w, va_w, qb_w, kb_w, vb_w = [w[:, cuts[i]:cuts[i + 1]] for i in range(6)]
    qa_w = _reorder_qa_heads(qa_w, 1)
    w_ext = jnp.concatenate([qa_w, qb_w, kb_w, vb_w, ka_w, va_w], axis=1).astype(BF16)
    scale = HEAD_DIM ** -0.5 * LOG2E
    pair = lambda g: jnp.tile(g, 2)
    gains = jnp.stack([pair(q_norm_a[0]) * scale, pair(q_norm_b[0]) * scale, pair(k_norm_b[0]), pair(k_norm_a[0])]
                      + [jnp.zeros((LANES,), F32)] * 4)
    bd = jnp.asarray(np.kron(np.eye(2), np.full((HEAD_DIM, HEAD_DIM), 1.0 / HEAD_DIM)), BF16)
    cos_t, sin_t = _rope_tables(s)
    n1 = norm1[0][None, :]
    zq = _inproj(x, mod, n1, w_ext, gains, cos_t, sin_t, bd, mod_row=None, tm=TOKEN_TILE)
    zc = _inproj(ctx, mod, n1, w_ext, gains, jnp.ones((lc, LANES), F32), jnp.zeros((lc, LANES), F32), bd,
                 mod_row=bsz, tm=lc)

    sink = sink_a[0].astype(F32)
    sink_rep = jnp.concatenate([jnp.broadcast_to(sink[h] * LOG2E, (A_WINDOW, LANES)) for h in _QA_HEAD_ORDER])
    o_a = _attn_a(zq, zc, sink_rep, _attn_a_mask(A_WINDOW))
    o_b = _attn_b(zq, zc, _na_table(rpb_b[0], rows))

    ga = _reorder_qa_heads(out_norm_a[0], 0)[None, :]
    gb = out_norm_b[0][None, :]
    wo = jnp.concatenate([_reorder_qa_heads(w_out[0][:aw], 0), w_out[0][aw:]], axis=0).astype(BF16)
    wr_hi, wr_lo = _split_bf16(w_router[0].T)
    xres, hp_a, hp_b, idx, gates = _merge(o_a, o_b, x, mod, ga, gb, wo, norm2[0][None, :], wr_hi, wr_lo,
                                  router_bias[0][:, None], ws_gate[0].astype(BF16), ws_up[0].astype(BF16),
                                  ws_down[0].astype(BF16), tm=TOKEN_TILE)

    rank, counts = _rank(idx, tm=TOKEN_TILE)
    t = EXPERT_ROWS
    cnt = counts[:, 0].astype(I32)
    padded = (cnt + t - 1) // t * t
    pends = jnp.cumsum(padded)
    pstart = pends - padded
    n_rows = n * TOP_K + N_EXPERTS * t
    pos = _pos(idx, rank, pstart.astype(F32)[:, None], tm=min(POS_TILE, n))

    pos_flat = pos.reshape(1, TOP_K * n)
    xs_a, xs_b = _sc_scatter_rows((hp_a, hp_b), pos_flat, n_rows)
    ys_a, ys_b = _experts(pstart, cnt, pends[-1:] // t, xs_a, xs_b, we_gate[0], we_up[0], we_down[0])
    yg_a, yg_b = [y.reshape(TOP_K, n, d // 4) for y in _sc_gather_rows((ys_a, ys_b), pos_flat)]
    out = _finish(gates, xres.reshape(n, d), mod, yg_a, yg_b, tm=TOKEN_TILE, tiles_per_batch=s // TOKEN_TILE)
    return out.reshape(bsz, s, d)
```

```python
import functools

import numpy as np
import jax
import jax.numpy as jnp
from jax import lax
from jax.experimental import pallas as pl
from jax.experimental.pallas import tpu as pltpu
from jax.experimental.pallas import tpu_sc as plsc

F32 = jnp.float32
BF16 = jnp.bfloat16
I32 = jnp.int32
U32 = jnp.uint32

LANES = 128
HEAD_DIM = 64
GRID_W = 64
A_Q_HEADS = 8
A_KV_HEADS = 2
A_WINDOW = 128
B_HEADS = 8
NA_KH = 8
NA_KW = 16
NA_QROWS = 4
NA_KROWS = NA_QROWS + NA_KH
ROPE_BASE = 10000.0
N_EXPERTS = 256
TOP_K = 8
N_GROUPS = 8
TOPK_GROUPS = 4
ROUTED_SCALE = 2.5
LOG2E = 1.4426950408889634
EPS = 1e-6
NEG_INF = -1e30
TOKEN_TILE = 512
POS_TILE = 2048
EXPERT_ROWS = 256
SC_WINDOW = 128
VMEM_LIMIT = 56 * 1024 * 1024

_NT = (((1,), (1,)), ((), ()))


def _params(*sem):
    return pltpu.CompilerParams(dimension_semantics=sem, vmem_limit_bytes=VMEM_LIMIT)


def _silu(v):
    return v * jax.nn.sigmoid(v)


def _rms(v, gain):
    return v * lax.rsqrt(jnp.mean(v * v, axis=-1, keepdims=True) + EPS) * gain


def _pack_bf16_pairs(v):
    n = v.shape[1] // 2
    lo = lax.bitcast_convert_type(v[:, :n].astype(BF16).astype(F32), U32) >> 16
    hi = lax.bitcast_convert_type(v[:, n:].astype(BF16).astype(F32), U32) & jnp.uint32(0xFFFF0000)
    return hi | lo


def _split_bf16(v):
    hi = lax.bitcast_convert_type(lax.bitcast_convert_type(v, U32) & jnp.uint32(0xFFFF0000), F32)
    return hi.astype(BF16), (v - hi).astype(BF16)


def _unpack_bf16_pairs(w):
    lo = lax.bitcast_convert_type(w << 16, F32)
    hi = lax.bitcast_convert_type(w & jnp.uint32(0xFFFF0000), F32)
    return jnp.concatenate([lo, hi], axis=1)


def _ada_kernel(c_ref, w_ref, b_ref, o_ref):
    a = _silu(c_ref[...])
    o_ref[...] = jnp.dot(a, w_ref[...], preferred_element_type=F32,
                         precision=lax.Precision.HIGHEST) + b_ref[...]


def _ada(c8, w, b):
    d, n = w.shape
    bn = n // 4
    return pl.pallas_call(
        _ada_kernel,
        out_shape=jax.ShapeDtypeStruct((8, n), F32),
        grid=(n // bn,),
        in_specs=[pl.BlockSpec((8, d), lambda j: (0, 0)),
                  pl.BlockSpec((d, bn), lambda j: (0, j)),
                  pl.BlockSpec((1, bn), lambda j: (0, j))],
        out_specs=pl.BlockSpec((8, bn), lambda j: (0, j)),
        compiler_params=_params("arbitrary"),
        name="ada",
    )(c8, w, b)


_QA, _QB, _KB, _VB, _KA, _VA = 0, 4, 8, 12, 16, 17
_OUT_BLOCKS = 18


def _inproj_kernel(x_ref, mod_ref, n1_ref, w_ref, g_ref, cos_ref, sin_ref, bd_ref, o_ref, *, mod_row, d):
    b = pl.program_id(0) if mod_row is None else mod_row
    xn = _rms(x_ref[0], n1_ref[...])
    sh = mod_ref[pl.ds(b, 1), 0:d]
    sc = mod_ref[pl.ds(b, 1), d:2 * d]
    h = (xn * (1.0 + sc) + sh).astype(BF16)
    z = jnp.dot(h, w_ref[...], preferred_element_type=F32)
    bd = bd_ref[...]
    cos = cos_ref[...]
    sin = sin_ref[...]

    def blk(j):
        return z[:, j * LANES:(j + 1) * LANES]

    def head_rinv(zb):
        ms = jnp.dot((zb * zb).astype(BF16), bd, preferred_element_type=F32)
        return lax.rsqrt(ms + EPS)

    def put(j, v):
        o_ref[0, :, j * LANES:(j + 1) * LANES] = v.astype(BF16)

    quarter = HEAD_DIM // 4
    first_half = (lax.broadcasted_iota(I32, (1, LANES), 1) % (2 * quarter)) < quarter

    def roped(j, g_row):
        zb = blk(j)
        zn = zb * head_rinv(zb) * g_ref[g_row:g_row + 1, :]
        partner = jnp.where(first_half, pltpu.roll(zn, LANES - quarter, 1), pltpu.roll(zn, quarter, 1))
        put(j, zn * cos + partner * sin)

    def normed(j, g_row):
        zb = blk(j)
        put(j, zb * head_rinv(zb) * g_ref[g_row:g_row + 1, :])

    for j in range(4):
        roped(_QA + j, 0)
        normed(_QB + j, 1)
        normed(_KB + j, 2)
        put(_VB + j, blk(_VB + j))
    roped(_KA, 3)
    put(_VA, blk(_VA))


def _inproj(x, mod, n1, w_ext, gains, cos_t, sin_t, bd, *, mod_row, tm):
    bsz, s, d = x.shape
    kern = functools.partial(_inproj_kernel, mod_row=mod_row, d=d)
    return pl.pallas_call(
        kern,
        out_shape=jax.ShapeDtypeStruct((bsz, s, _OUT_BLOCKS * LANES), BF16),
        grid=(bsz, s // tm),
        in_specs=[pl.BlockSpec((1, tm, d), lambda b, i: (b, i, 0)),
                  pl.BlockSpec(mod.shape, lambda b, i: (0, 0)),
                  pl.BlockSpec((1, d), lambda b, i: (0, 0)),
                  pl.BlockSpec(w_ext.shape, lambda b, i: (0, 0)),
                  pl.BlockSpec(gains.shape, lambda b, i: (0, 0)),
                  pl.BlockSpec((tm, LANES), lambda b, i: (i, 0)),
                  pl.BlockSpec((tm, LANES), lambda b, i: (i, 0)),
                  pl.BlockSpec(bd.shape, lambda b, i: (0, 0))],
        out_specs=pl.BlockSpec((1, tm, _OUT_BLOCKS * LANES), lambda b, i: (b, i, 0)),
        compiler_params=_params("arbitrary", "arbitrary"),
        name="inproj",
    )(x, mod, n1, w_ext, gains, cos_t, sin_t, bd)


def _split_pair(qp, lo):
    zero = jnp.zeros_like(qp)
    return jnp.concatenate([jnp.where(lo, qp, zero), jnp.where(lo, zero, qp)], axis=0)


def _softmax_pv(s_parts, v_parts, sink_rep):
    chunks = [s[:, c * LANES:(c + 1) * LANES] for s in s_parts for c in range(s.shape[1] // LANES)]
    m = jnp.max(functools.reduce(jnp.maximum, chunks), axis=-1, keepdims=True)
    m_rep = jnp.broadcast_to(m, (m.shape[0], LANES))
    if sink_rep is not None:
        m_rep = jnp.maximum(m_rep, sink_rep)
    acc = None
    for s, v in zip(s_parts, v_parts):
        p = jnp.concatenate([jnp.exp2(s[:, c * LANES:(c + 1) * LANES] - m_rep)
                             for c in range(s.shape[1] // LANES)], axis=1).astype(BF16)
        v_ext = jnp.concatenate([v, jnp.ones_like(v)], axis=1)
        o = jnp.dot(p, v_ext, preferred_element_type=F32)
        acc = o if acc is None else acc + o
    l_rep = acc[:, LANES:]
    if sink_rep is not None:
        l_rep = l_rep + jnp.exp2(sink_rep - m_rep)
    return acc[:, :LANES] * (1.0 / l_rep)


A_BLOCKS_PER_STEP = 8


def _attn_a_kernel(q_ref, *refs):
    nq = A_BLOCKS_PER_STEP
    k_refs, v_refs = refs[:nq + 2], refs[nq + 2:2 * nq + 4]
    kx_ref, vx_ref, sink_ref = refs[2 * nq + 4:2 * nq + 7]
    mask_refs = refs[2 * nq + 7:3 * nq + 7]
    o_ref = refs[-1]
    tq = A_WINDOW
    lo = lax.broadcasted_iota(I32, (tq, LANES), 1) < HEAD_DIM
    for h in range(nq):
        q = q_ref[0, h * tq:(h + 1) * tq]
        qs = jnp.concatenate([_split_pair(q[:, p * LANES:(p + 1) * LANES], lo) for p in range(4)], axis=0)
        k = jnp.concatenate([r[0] for r in k_refs[h:h + 3]], axis=0)
        v = jnp.concatenate([r[0] for r in v_refs[h:h + 3]], axis=0)
        s_loc = lax.dot_general(qs, k, _NT, preferred_element_type=F32) + mask_refs[h][0]
        s_ctx = lax.dot_general(qs, kx_ref[0], _NT, preferred_element_type=F32)
        o = _softmax_pv([s_loc, s_ctx], [v, vx_ref[0]], sink_ref[...])
        for p in range(4):
            o_lo = o[(2 * p) * tq:(2 * p + 1) * tq]
            o_hi = o[(2 * p + 1) * tq:(2 * p + 2) * tq]
            o_ref[0, h * tq:(h + 1) * tq, p * LANES:(p + 1) * LANES] = jnp.where(lo, o_lo, o_hi).astype(BF16)


def _attn_a(zq, zc, sink_rep, mask):
    bsz, s, _ = zq.shape
    lc = zc.shape[1]
    tq = A_WINDOW
    nq = A_BLOCKS_PER_STEP
    nblk = s // tq
    ka, va = _KA, _VA

    def kv_spec(col, shift):
        return pl.BlockSpec((1, tq, LANES), lambda b, j: (b, jnp.clip(nq * j + shift, 0, nblk - 1), col))

    def mask_spec(h):
        def variant(j):
            blk = nq * j + h
            return jnp.where(blk == 0, 0, jnp.where(blk == nblk - 1, 2, 1))
        return pl.BlockSpec((1,) + mask.shape[1:], lambda b, j: (variant(j), 0, 0))

    shifts = range(-1, nq + 1)
    return pl.pallas_call(
        _attn_a_kernel,
        out_shape=jax.ShapeDtypeStruct((bsz, s, 4 * LANES), BF16),
        grid=(bsz, nblk // nq),
        in_specs=([pl.BlockSpec((1, nq * tq, 4 * LANES), lambda b, j: (b, j, 0))]
                  + [kv_spec(ka, sh) for sh in shifts] + [kv_spec(va, sh) for sh in shifts]
                  + [pl.BlockSpec((1, lc, LANES), lambda b, j: (b, 0, ka)),
                     pl.BlockSpec((1, lc, LANES), lambda b, j: (b, 0, va)),
                     pl.BlockSpec(sink_rep.shape, lambda b, j: (0, 0))]
                  + [mask_spec(h) for h in range(nq)]),
        out_specs=pl.BlockSpec((1, nq * tq, 4 * LANES), lambda b, j: (b, j, 0)),
        compiler_params=_params("arbitrary", "arbitrary"),
        name="attn_a",
    )(*([zq] * (2 * nq + 5) + [zc, zc, sink_rep] + [mask] * nq))


def _attn_a_mask(tq):
    qi = np.arange(tq)[:, None]
    kj = np.arange(3 * tq)[None, :]
    ok = (kj >= qi) & (kj <= qi + 2 * tq)
    variants = [ok & (kj >= tq), ok, ok & (kj < 2 * tq)]
    m = np.stack([np.tile(np.where(v, 0.0, NEG_INF).astype(np.float32), (A_Q_HEADS, 1)) for v in variants])
    return jnp.asarray(m)


NA_GROUPS_PER_STEP = 4


def _attn_b_kernel(q_ref, *refs):
    ngs = NA_GROUPS_PER_STEP
    nkv = ngs + 2
    k_refs, v_refs = refs[:nkv], refs[nkv:2 * nkv]
    kx_ref, vx_ref = refs[2 * nkv:2 * nkv + 2]
    tab_refs = refs[2 * nkv + 2:2 * nkv + 2 + ngs]
    o_ref = refs[-1]
    tq = NA_QROWS * GRID_W
    lo = lax.broadcasted_iota(I32, (tq, LANES), 1) < HEAD_DIM
    for g in range(ngs):
        rows_g = slice(g * tq, (g + 1) * tq)
        for p in range(4):
            sl = slice(p * LANES, (p + 1) * LANES)
            qs = _split_pair(q_ref[0, rows_g, sl], lo)
            k = jnp.concatenate([r[0, :, sl] for r in k_refs[g:g + 3]], axis=0)
            v = jnp.concatenate([r[0, :, sl] for r in v_refs[g:g + 3]], axis=0)
            s_loc = lax.dot_general(qs, k, _NT, preferred_element_type=F32) + tab_refs[g][0, p]
            s_ctx = lax.dot_general(qs, kx_ref[0, :, sl], _NT, preferred_element_type=F32)
            o = _softmax_pv([s_loc, s_ctx], [v, vx_ref[0, :, sl]], None)
            o_ref[0, rows_g, sl] = jnp.where(lo, o[:tq], o[tq:]).astype(BF16)


def _attn_b(zq, zc, table):
    bsz, s, _ = zq.shape
    lc = zc.shape[1]
    tq = NA_QROWS * GRID_W
    ngs = NA_GROUPS_PER_STEP
    ng = s // tq
    qb, kb, vb = _QB // 4, _KB // 4, _VB // 4

    def kv_spec(col, j):
        return pl.BlockSpec((1, tq, 4 * LANES), lambda i, b: (b, jnp.clip(ngs * i + j - 1, 0, ng - 1), col))

    def tab_spec(g):
        def variant(i):
            grp = ngs * i + g
            return jnp.where(grp == 0, 0, jnp.where(grp == ng - 1, 2, 1))
        return pl.BlockSpec((1,) + table.shape[1:], lambda i, b: (variant(i), 0, 0, 0),
                            pipeline_mode=pl.Buffered(1))

    kv_slots = range(ngs + 2)
    return pl.pallas_call(
        _attn_b_kernel,
        out_shape=jax.ShapeDtypeStruct((bsz, s, 4 * LANES), BF16),
        grid=(ng // ngs, bsz),
        in_specs=([pl.BlockSpec((1, ngs * tq, 4 * LANES), lambda i, b: (b, i, qb))]
                  + [kv_spec(kb, j) for j in kv_slots] + [kv_spec(vb, j) for j in kv_slots]
                  + [pl.BlockSpec((1, lc, 4 * LANES), lambda i, b: (b, 0, kb)),
                     pl.BlockSpec((1, lc, 4 * LANES), lambda i, b: (b, 0, vb))]
                  + [tab_spec(g) for g in range(ngs)]),
        out_specs=pl.BlockSpec((1, ngs * tq, 4 * LANES), lambda i, b: (b, i, 0)),
        compiler_params=_params("arbitrary", "arbitrary"),
        name="attn_b",
    )(*([zq] * (2 * ngs + 5) + [zc, zc] + [table] * ngs))


def _na_table(rpb, rows):
    ng = rows // NA_QROWS
    nq, nk = NA_QROWS * GRID_W, NA_KROWS * GRID_W
    qc = np.arange(GRID_W)
    kc = np.arange(GRID_W)
    ws = np.clip(qc - NA_KW // 2, 0, GRID_W - NA_KW)
    valid_c = (kc[None, :] >= ws[:, None]) & (kc[None, :] < ws[:, None] + NA_KW)
    dc = np.clip(kc[None, :] - qc[:, None], -(NA_KW - 1), NA_KW - 1) + (NA_KW - 1)
    c_sel = (dc[..., None] == np.arange(2 * NA_KW - 1)) & valid_c[..., None]
    tiles = jnp.einsum('hab,uvb->huav', rpb.astype(F32), jnp.asarray(c_sel, F32), precision=lax.Precision.HIGHEST)
    tiles = jnp.where(jnp.asarray(valid_c)[None, :, None, :], tiles * LOG2E, NEG_INF)
    tiles = tiles.reshape(B_HEADS, GRID_W, (2 * NA_KH - 1) * GRID_W)

    def masked(n_key_rows):
        return jnp.full((B_HEADS, GRID_W, n_key_rows * GRID_W), NEG_INF, F32)

    tabs = []
    for i in (0, 1, ng - 1):
        start = NA_QROWS * i - NA_KH // 2
        q_rows = []
        for qr in range(NA_QROWS):
            r = NA_QROWS * i + qr
            rs = int(np.clip(r - NA_KH // 2, 0, rows - NA_KH))
            y0 = rs - start
            d0 = rs - r + (NA_KH - 1)
            q_rows.append(jnp.concatenate([masked(y0), tiles[:, :, d0 * GRID_W:(d0 + NA_KH) * GRID_W],
                                           masked(NA_KROWS - NA_KH - y0)], axis=-1))
        tabs.append(jnp.concatenate(q_rows, axis=-2))
    return jnp.stack(tabs).reshape(3, B_HEADS // 2, 2 * nq, nk)


def _merge_kernel(oa_ref, ob_ref, x_ref, mod_ref, ga_ref, gb_ref, wo_ref, n2_ref, wrh_ref, wrl_ref, rb_ref,
                  wsg_ref, wsu_ref, wsd_ref, xres_ref, hpa_ref, hpb_ref, idx_ref, gate_ref, *, d):
    b = pl.program_id(0)
    tm = x_ref.shape[1]

    def mod(k):
        return mod_ref[pl.ds(b, 1), k * d:(k + 1) * d]

    na = _rms(oa_ref[0].astype(F32), ga_ref[...])
    nb = _rms(ob_ref[0].astype(F32), gb_ref[...])
    cat = jnp.concatenate([na, nb], axis=1).astype(BF16)
    y = jnp.dot(cat, wo_ref[...], preferred_element_type=F32)
    x1 = x_ref[0] + mod(2) * y
    h2 = _rms(x1, n2_ref[...]) * (1.0 + mod(4)) + mod(3)

    hb = h2.astype(BF16)
    act = _silu(jnp.dot(hb, wsg_ref[...], preferred_element_type=F32)) * jnp.dot(hb, wsu_ref[...],
                                                                                 preferred_element_type=F32)
    shared = jnp.dot(act.astype(BF16), wsd_ref[...], preferred_element_type=F32)
    xres_ref[0] = x1 + mod(5) * shared
    packed = _pack_bf16_pairs(h2)
    hpa_ref[...] = packed[:, :d // 4]
    hpb_ref[...] = packed[:, d // 4:]

    h_hi, h_lo = _split_bf16(h2)
    logits = (lax.dot_general(wrh_ref[...], h_hi, _NT, preferred_element_type=F32)
              + lax.dot_general(wrh_ref[...], h_lo, _NT, preferred_element_type=F32)
              + lax.dot_general(wrl_ref[...], h_hi, _NT, preferred_element_type=F32))
    scores = jax.nn.sigmoid(logits)
    sel = scores + rb_ref[...]
    per = N_EXPERTS // N_GROUPS
    g3 = sel.reshape(N_GROUPS, per, tm)
    it3 = lax.broadcasted_iota(I32, (N_GROUPS, per, tm), 1)
    m1 = jnp.max(g3, axis=1, keepdims=True)
    first = jnp.min(jnp.where(g3 == m1, it3, per), axis=1, keepdims=True)
    m2 = jnp.max(jnp.where(it3 == first, -jnp.inf, g3), axis=1, keepdims=True)
    gscore = (m1 + m2).reshape(N_GROUPS, tm)

    itg = lax.broadcasted_iota(I32, (N_GROUPS, tm), 0)
    gsel = jnp.zeros((N_GROUPS, tm), F32)
    cur = gscore
    for _ in range(TOPK_GROUPS):
        mx = jnp.max(cur, axis=0, keepdims=True)
        fi = jnp.min(jnp.where(cur == mx, itg, N_GROUPS), axis=0, keepdims=True)
        pick = itg == fi
        gsel = jnp.where(pick, 1.0, gsel)
        cur = jnp.where(pick, -jnp.inf, cur)
    emask = jnp.broadcast_to(gsel.reshape(N_GROUPS, 1, tm), (N_GROUPS, per, tm)).reshape(N_EXPERTS, tm) > 0.5

    ite = lax.broadcasted_iota(I32, (N_EXPERTS, tm), 0)
    cur = jnp.where(emask, sel, NEG_INF)
    idx_rows, s_rows = [], []
    for _ in range(TOP_K):
        mx = jnp.max(cur, axis=0, keepdims=True)
        fi = jnp.min(jnp.where(cur == mx, ite, N_EXPERTS), axis=0, keepdims=True)
        pick = ite == fi
        idx_rows.append(fi)
        s_rows.append(jnp.sum(jnp.where(pick, scores, 0.0), axis=0, keepdims=True))
        cur = jnp.where(pick, -jnp.inf, cur)
    top_s = jnp.concatenate(s_rows, axis=0)
    idx_ref[...] = jnp.concatenate(idx_rows, axis=0)
    gate_ref[...] = top_s / jnp.sum(top_s, axis=0, keepdims=True) * ROUTED_SCALE


def _merge(o_a, o_b, x, mod, ga, gb, wo, n2, wr_hi, wr_lo, rb_col, wsg, wsu, wsd, *, tm):
    bsz, s, d = x.shape
    nt = s // tm
    n = bsz * s
    full = lambda a: pl.BlockSpec(a.shape, lambda b, i: (0,) * a.ndim)
    return pl.pallas_call(
        functools.partial(_merge_kernel, d=d),
        out_shape=(jax.ShapeDtypeStruct((bsz, s, d), F32),
                   jax.ShapeDtypeStruct((n, d // 4), U32),
                   jax.ShapeDtypeStruct((n, d // 4), U32),
                   jax.ShapeDtypeStruct((TOP_K, n), I32),
                   jax.ShapeDtypeStruct((TOP_K, n), F32)),
        grid=(bsz, nt),
        in_specs=[pl.BlockSpec((1, tm, d // 2), lambda b, i: (b, i, 0)),
                  pl.BlockSpec((1, tm, d // 2), lambda b, i: (b, i, 0)),
                  pl.BlockSpec((1, tm, d), lambda b, i: (b, i, 0)),
                  full(mod), full(ga), full(gb), full(wo), full(n2), full(wr_hi), full(wr_lo), full(rb_col),
                  full(wsg), full(wsu), full(wsd)],
        out_specs=(pl.BlockSpec((1, tm, d), lambda b, i: (b, i, 0)),
                   pl.BlockSpec((tm, d // 4), lambda b, i: (b * nt + i, 0)),
                   pl.BlockSpec((tm, d // 4), lambda b, i: (b * nt + i, 0)),
                   pl.BlockSpec((TOP_K, tm), lambda b, i: (0, b * nt + i)),
                   pl.BlockSpec((TOP_K, tm), lambda b, i: (0, b * nt + i))),
        compiler_params=_params("arbitrary", "arbitrary"),
        name="merge",
    )(o_a, o_b, x, mod, ga, gb, wo, n2, wr_hi, wr_lo, rb_col, wsg, wsu, wsd)


def _rank_kernel(idx_ref, rank_ref, cnt_ref, carry_ref):
    tm = idx_ref.shape[1]

    @pl.when(pl.program_id(0) == 0)
    def _():
        carry_ref[...] = jnp.zeros_like(carry_ref)

    idx = idx_ref[...]
    ite = lax.broadcasted_iota(I32, (N_EXPERTS, tm), 0)
    before = (lax.broadcasted_iota(I32, (tm, tm), 0) < lax.broadcasted_iota(I32, (tm, tm), 1)).astype(BF16)
    hits = [ite == idx[k:k + 1, :] for k in range(TOP_K)]
    routed = jnp.where(functools.reduce(jnp.logical_or, hits), 1.0, 0.0)
    base = carry_ref[...]
    ahead = jnp.dot(routed.astype(BF16), before, preferred_element_type=F32) + base
    rows = [jnp.sum(jnp.where(hit, ahead, 0.0), axis=0, keepdims=True) for hit in hits]
    rank_ref[...] = jnp.concatenate(rows, axis=0).astype(I32)
    total = base + jnp.sum(routed, axis=1, keepdims=True)
    carry_ref[...] = total
    cnt_ref[...] = total


def _rank(idx, *, tm):
    n = idx.shape[1]
    return pl.pallas_call(
        _rank_kernel,
        out_shape=(jax.ShapeDtypeStruct((TOP_K, n), I32), jax.ShapeDtypeStruct((N_EXPERTS, 1), F32)),
        grid=(n // tm,),
        in_specs=[pl.BlockSpec((TOP_K, tm), lambda i: (0, i))],
        out_specs=(pl.BlockSpec((TOP_K, tm), lambda i: (0, i)), pl.BlockSpec((N_EXPERTS, 1), lambda i: (0, 0))),
        scratch_shapes=[pltpu.VMEM((N_EXPERTS, 1), F32)],
        compiler_params=_params("arbitrary"),
        name="rank",
    )(idx)


def _pos_kernel(idx_ref, rank_ref, pstart_ref, pos_ref):
    tm = idx_ref.shape[1]
    idx = idx_ref[...]
    ite = lax.broadcasted_iota(I32, (N_EXPERTS, tm), 0)
    pstart = pstart_ref[...]
    rows = [jnp.sum(jnp.where(ite == idx[k:k + 1, :], pstart, 0.0), axis=0, keepdims=True) for k in range(TOP_K)]
    pos_ref[...] = jnp.concatenate(rows, axis=0).astype(I32) + rank_ref[...]


def _pos(idx, rank, pstart_col, *, tm):
    n = idx.shape[1]
    return pl.pallas_call(
        _pos_kernel,
        out_shape=jax.ShapeDtypeStruct((TOP_K, n), I32),
        grid=(n // tm,),
        in_specs=[pl.BlockSpec((TOP_K, tm), lambda i: (0, i)),
                  pl.BlockSpec((TOP_K, tm), lambda i: (0, i)),
                  pl.BlockSpec((N_EXPERTS, 1), lambda i: (0, 0))],
        out_specs=pl.BlockSpec((TOP_K, tm), lambda i: (0, i)),
        compiler_params=_params("arbitrary"),
        name="pos",
    )(idx, rank, pstart_col)


UNIT_CHUNKS = (4, 2, 1)
RING_AHEAD = 8
W_SPLIT = 4
RING_SLOTS = RING_AHEAD + UNIT_CHUNKS[0]


def _expert_kernel(ps_ref, cnt_ref, tot_ref, *refs):
    wg_refs, wu_refs, wd_refs = refs[:W_SPLIT], refs[W_SPLIT:2 * W_SPLIT], refs[2 * W_SPLIT:3 * W_SPLIT]
    xa_hbm, xb_hbm, ya_hbm, yb_hbm, xa_buf, xb_buf, ya_buf, yb_buf, in_sem, out_sem = refs[3 * W_SPLIT:]
    e = pl.program_id(0)
    r = EXPERT_ROWS
    ahead = RING_AHEAD
    total = tot_ref[0]
    first = ps_ref[e] // r
    cnt = cnt_ref[e]
    nch = (cnt + r - 1) // r

    def rows_of(g):
        return pl.ds(pl.multiple_of(g * r, r), r)

    def fetch(g):
        slot = g % RING_SLOTS
        return (pltpu.make_async_copy(xa_hbm.at[rows_of(g)], xa_buf.at[slot], in_sem.at[0, slot]),
                pltpu.make_async_copy(xb_hbm.at[rows_of(g)], xb_buf.at[slot], in_sem.at[1, slot]))

    def flush(g):
        slot = g % RING_SLOTS
        return (pltpu.make_async_copy(ya_buf.at[slot], ya_hbm.at[rows_of(g)], out_sem.at[0, slot]),
                pltpu.make_async_copy(yb_buf.at[slot], yb_hbm.at[rows_of(g)], out_sem.at[1, slot]))

    def start(copies):
        for cp in copies:
            cp.start()

    def wait(copies):
        for cp in copies:
            cp.wait()

    @pl.when(e == 0)
    def _():
        for g in range(ahead):
            @pl.when(g < total)
            def _():
                start(fetch(g))

    @pl.when(nch > 0)
    def _():
        def unit(c0, u):
            for j in range(u):
                g = first + c0 + j
                wait(fetch(g))

                @pl.when(g + ahead < total)
                def _():
                    start(fetch(g + ahead))

                @pl.when(g >= RING_SLOTS)
                def _():
                    wait(flush(g - RING_SLOTS))

            slots = [(first + c0 + j) % RING_SLOTS for j in range(u)]
            packed = jnp.concatenate([jnp.concatenate([xa_buf[sl], xb_buf[sl]], axis=1) for sl in slots], axis=0)
            row = lax.broadcasted_iota(I32, packed.shape, 0)
            packed = jnp.where(row < cnt - c0 * r, packed, jnp.zeros_like(packed))
            x = _unpack_bf16_pairs(packed).astype(BF16)
            kw = x.shape[1] // W_SPLIT

            def up_proj(w_refs):
                return sum(jnp.dot(x[:, i * kw:(i + 1) * kw], w_refs[i][0].astype(BF16),
                                   preferred_element_type=F32) for i in range(W_SPLIT))

            act = (_silu(up_proj(wg_refs)) * up_proj(wu_refs)).astype(BF16)
            y = jnp.concatenate([jnp.dot(act, w[0].astype(BF16), preferred_element_type=F32) for w in wd_refs],
                                axis=1)
            out = _pack_bf16_pairs(y)
            half = out.shape[1] // 2
            for j, sl in enumerate(slots):
                ya_buf[sl] = out[j * r:(j + 1) * r, :half]
                yb_buf[sl] = out[j * r:(j + 1) * r, half:]
                start(flush(first + c0 + j))

        big = UNIT_CHUNKS[0]

        def big_unit(i, carry):
            unit(i * big, big)
            return carry

        lax.fori_loop(0, nch // big, big_unit, 0)
        done = nch // big * big
        for u in UNIT_CHUNKS[1:]:
            @pl.when((nch % (2 * u)) >= u)
            def _():
                unit(done, u)

            done = done + jnp.where((nch % (2 * u)) >= u, u, 0)

    @pl.when(e == pl.num_programs(0) - 1)
    def _():
        for k in range(RING_SLOTS):
            @pl.when(total - 1 - k >= 0)
            def _():
                wait(flush(total - 1 - k))


def _experts(pstart, cnt, total_chunks, xs_a, xs_b, wg, wu, wd):
    rows, hw = xs_a.shape
    r = EXPERT_ROWS
    n_exp, d, f = wg.shape
    hbm = pl.BlockSpec(memory_space=pl.ANY)
    grid_spec = pltpu.PrefetchScalarGridSpec(
        num_scalar_prefetch=3,
        grid=(n_exp,),
        in_specs=([pl.BlockSpec((1, d // W_SPLIT, f), lambda e, ps, cn, tot, i=i: (e, i, 0)) for i in range(W_SPLIT)] * 2
                  + [pl.BlockSpec((1, f, d // W_SPLIT), lambda e, ps, cn, tot, i=i: (e, 0, i)) for i in range(W_SPLIT)]
                  + [hbm, hbm]),
        out_specs=(hbm, hbm),
        scratch_shapes=[pltpu.VMEM((RING_SLOTS, r, hw), U32), pltpu.VMEM((RING_SLOTS, r, hw), U32),
                        pltpu.VMEM((RING_SLOTS, r, hw), U32), pltpu.VMEM((RING_SLOTS, r, hw), U32),
                        pltpu.SemaphoreType.DMA((2, RING_SLOTS)), pltpu.SemaphoreType.DMA((2, RING_SLOTS))],
    )
    return pl.pallas_call(
        _expert_kernel,
        out_shape=(jax.ShapeDtypeStruct((rows, hw), U32), jax.ShapeDtypeStruct((rows, hw), U32)),
        grid_spec=grid_spec,
        compiler_params=_params("arbitrary"),
        name="experts",
    )(pstart, cnt, total_chunks, *([wg] * W_SPLIT + [wu] * W_SPLIT + [wd] * W_SPLIT), xs_a, xs_b)


def _sc_scatter_rows(xs, idx_flat, n_rows):
    n, w = xs[0].shape
    m = idx_flat.shape[1]
    nwin = n // SC_WINDOW
    reps = m // n
    mesh = plsc.VectorSubcoreMesh(core_axis_name="core", subcore_axis_name="subcore")
    out_type = tuple(jax.ShapeDtypeStruct((n_rows, w), x.dtype) for x in xs)

    @functools.partial(pl.kernel, out_type=out_type, mesh=mesh, scratch_types=[])
    def scatter_kernel(*refs):
        x_hbms, i_hbm, o_hbms = refs[:len(xs)], refs[len(xs)], refs[len(xs) + 1:]
        for x_hbm, o_hbm in zip(x_hbms, o_hbms):
            def body(x_vmem, i_vmem, o_hbm=o_hbm):
                pltpu.sync_copy(x_vmem, o_hbm.at[i_vmem.at[0]])

            pltpu.emit_pipeline(
                body,
                grid=(nwin, reps),
                in_specs=[pl.BlockSpec((SC_WINDOW, w), lambda i, k: (i, 0)),
                          pl.BlockSpec((1, SC_WINDOW), lambda i, k: (0, k * nwin + i))],
                out_specs=[],
                core_axis_name=("core", "subcore"),
                dimension_semantics=(pltpu.PARALLEL, pltpu.ARBITRARY),
            )(x_hbm, i_hbm)

    return scatter_kernel(*xs, idx_flat)


def _sc_gather_rows(srcs, idx_flat):
    m = idx_flat.shape[1]
    w = srcs[0].shape[1]
    mesh = plsc.VectorSubcoreMesh(core_axis_name="core", subcore_axis_name="subcore")
    out_type = tuple(jax.ShapeDtypeStruct((m, w), src.dtype) for src in srcs)

    @functools.partial(pl.kernel, out_type=out_type, mesh=mesh)
    def gather_kernel(*refs):
        x_hbms, i_hbm, o_hbms = refs[:len(srcs)], refs[len(srcs)], refs[len(srcs) + 1:]
        for x_hbm, o_hbm in zip(x_hbms, o_hbms):
            def body(i_vmem, o_vmem, x_hbm=x_hbm):
                pltpu.sync_copy(x_hbm.at[i_vmem.at[0]], o_vmem)

            pltpu.emit_pipeline(
                body,
                grid=(m // SC_WINDOW,),
                in_specs=[pl.BlockSpec((1, SC_WINDOW), lambda i: (0, i))],
                out_specs=[pl.BlockSpec((SC_WINDOW, w), lambda i: (i, 0))],
                core_axis_name=("core", "subcore"),
                dimension_semantics=(pltpu.PARALLEL,),
            )(i_hbm, o_hbm)

    return gather_kernel(*srcs, idx_flat)


def _finish_kernel(gate_ref, xres_ref, mod_ref, ya_ref, yb_ref, o_ref, *, d, tiles_per_batch):
    tm = gate_ref.shape[1]
    b = pl.program_id(0) // tiles_per_batch
    gates = gate_ref[...].T
    acc = jnp.zeros((tm, d), F32)
    for k in range(TOP_K):
        packed = jnp.concatenate([ya_ref[k], yb_ref[k]], axis=1)
        acc = acc + _unpack_bf16_pairs(packed) * gates[:, k:k + 1]
    o_ref[...] = xres_ref[...] + mod_ref[pl.ds(b, 1), 5 * d:6 * d] * acc


def _finish(gates, xres, mod, yg_a, yg_b, *, tm, tiles_per_batch):
    n, d = xres.shape
    return pl.pallas_call(
        functools.partial(_finish_kernel, d=d, tiles_per_batch=tiles_per_batch),
        out_shape=jax.ShapeDtypeStruct((n, d), F32),
        grid=(n // tm,),
        in_specs=[pl.BlockSpec((TOP_K, tm), lambda i: (0, i)),
                  pl.BlockSpec((tm, d), lambda i: (i, 0)),
                  pl.BlockSpec(mod.shape, lambda i: (0, 0)),
                  pl.BlockSpec((TOP_K, tm, d // 4), lambda i: (0, i, 0)),
                  pl.BlockSpec((TOP_K, tm, d // 4), lambda i: (0, i, 0))],
        out_specs=pl.BlockSpec((tm, d), lambda i: (i, 0)),
        compiler_params=_params("arbitrary"),
        name="finish",
    )(gates, xres, mod, yg_a, yg_b)


_QA_HEAD_ORDER = (0, 4, 1, 5, 2, 6, 3, 7)


def _reorder_qa_heads(a, axis):
    return jnp.concatenate([lax.slice_in_dim(a, h * HEAD_DIM, (h + 1) * HEAD_DIM, axis=axis)
                            for h in _QA_HEAD_ORDER], axis=axis)


def _rope_tables(s):
    quarter = HEAD_DIM // 4
    t = jnp.arange(s)
    row = (t // GRID_W).astype(F32)
    col = (t % GRID_W).astype(F32)
    freqs = ROPE_BASE ** (-jnp.arange(quarter, dtype=F32) / quarter)
    ar = row[:, None] * freqs[None, :]
    ac = col[:, None] * freqs[None, :]
    cos = jnp.concatenate([jnp.cos(ar), jnp.cos(ar), jnp.cos(ac), jnp.cos(ac)], axis=1)
    sin = jnp.concatenate([-jnp.sin(ar), jnp.sin(ar), -jnp.sin(ac), jnp.sin(ac)], axis=1)
    return jnp.tile(cos, (1, 2)), jnp.tile(sin, (1, 2))


def kernel(x, c, ctx, c_ctx, w_ada, b_ada, norm1, norm2, w_in, q_norm_a, k_norm_a, q_norm_b, k_norm_b, sink_a,
           rpb_b, out_norm_a, out_norm_b, w_out, w_router, router_bias, we_gate, we_up, we_down, ws_gate, ws_up,
           ws_down):
    assert w_ada.shape[0] == 1, "single-layer block"
    bsz, s, d = x.shape
    lc = ctx.shape[1]
    n = bsz * s
    rows = s // GRID_W
    assert d == 1024 and bsz + 1 <= 8 and rows >= NA_KROWS and lc % LANES == 0
    assert s % TOKEN_TILE == 0 and n % SC_WINDOW == 0
    assert (s // A_WINDOW) % A_BLOCKS_PER_STEP == 0 and (s // (NA_QROWS * GRID_W)) % NA_GROUPS_PER_STEP == 0

    c8 = jnp.concatenate([c, c_ctx[None, :], jnp.zeros((8 - bsz - 1, d), F32)], axis=0)
    mod = _ada(c8, w_ada[0], b_ada[0][None, :])

    w = w_in[0]
    aw, akw, bw = A_Q_HEADS * HEAD_DIM, A_KV_HEADS * HEAD_DIM, B_HEADS * HEAD_DIM
    cuts = np.cumsum([0, aw, akw, akw, bw, bw, bw])
    qa_w, ka_w, va_w, qb_w, kb_w, vb_w = [w[:, cuts[i]:cuts[i + 1]] for i in range(6)]
    qa_w = _reorder_qa_heads(qa_w, 1)
    w_ext = jnp.concatenate([qa_w, qb_w, kb_w, vb_w, ka_w, va_w], axis=1).astype(BF16)
    scale = HEAD_DIM ** -0.5 * LOG2E
    pair = lambda g: jnp.tile(g, 2)
    gains = jnp.stack([pair(q_norm_a[0]) * scale, pair(q_norm_b[0]) * scale, pair(k_norm_b[0]), pair(k_norm_a[0])]
                      + [jnp.zeros((LANES,), F32)] * 4)
    bd = jnp.asarray(np.kron(np.eye(2), np.full((HEAD_DIM, HEAD_DIM), 1.0 / HEAD_DIM)), BF16)
    cos_t, sin_t = _rope_tables(s)
    n1 = norm1[0][None, :]
    zq = _inproj(x, mod, n1, w_ext, gains, cos_t, sin_t, bd, mod_row=None, tm=TOKEN_TILE)
    zc = _inproj(ctx, mod, n1, w_ext, gains, jnp.ones((lc, LANES), F32), jnp.zeros((lc, LANES), F32), bd,
                 mod_row=bsz, tm=lc)

    sink = sink_a[0].astype(F32)
    sink_rep = jnp.concatenate([jnp.broadcast_to(sink[h] * LOG2E, (A_WINDOW, LANES)) for h in _QA_HEAD_ORDER])
    o_a = _attn_a(zq, zc, sink_rep, _attn_a_mask(A_WINDOW))
    o_b = _attn_b(zq, zc, _na_table(rpb_b[0], rows))

    ga = _reorder_qa_heads(out_norm_a[0], 0)[None, :]
    gb = out_norm_b[0][None, :]
    wo = jnp.concatenate([_reorder_qa_heads(w_out[0][:aw], 0), w_out[0][aw:]], axis=0).astype(BF16)
    wr_hi, wr_lo = _split_bf16(w_router[0].T)
    xres, hp_a, hp_b, idx, gates = _merge(o_a, o_b, x, mod, ga, gb, wo, norm2[0][None, :], wr_hi, wr_lo,
                                  router_bias[0][:, None], ws_gate[0].astype(BF16), ws_up[0].astype(BF16),
                                  ws_down[0].astype(BF16), tm=TOKEN_TILE)

    rank, counts = _rank(idx, tm=TOKEN_TILE)
    t = EXPERT_ROWS
    cnt = counts[:, 0].astype(I32)
    padded = (cnt + t - 1) // t * t
    pends = jnp.cumsum(padded)
    pstart = pends - padded
    n_rows = n * TOP_K + N_EXPERTS * t
    pos = _pos(idx, rank, pstart.astype(F32)[:, None], tm=min(POS_TILE, n))

    pos_flat = pos.reshape(1, TOP_K * n)
    xs_a, xs_b = _sc_scatter_rows((hp_a, hp_b), pos_flat, n_rows)
    ys_a, ys_b = _experts(pstart, cnt, pends[-1:] // t, xs_a, xs_b, we_gate[0], we_up[0], we_down[0])
    yg_a, yg_b = [y.reshape(TOP_K, n, d // 4) for y in _sc_gather_rows((ys_a, ys_b), pos_flat)]
    out = _finish(gates, xres.reshape(n, d), mod, yg_a, yg_b, tm=TOKEN_TILE, tiles_per_batch=s // TOKEN_TILE)
    return out.reshape(bsz, s, d)
```

```python
import functools

import numpy as np
import jax
import jax.numpy as jnp
from jax import lax
from jax.experimental import pallas as pl
from jax.experimental.pallas import tpu as pltpu
from jax.experimental.pallas import tpu_sc as plsc

F32 = jnp.float32
BF16 = jnp.bfloat16
I32 = jnp.int32
U32 = jnp.uint32

LANES = 128
HEAD_DIM = 64
GRID_W = 64
A_Q_HEADS = 8
A_KV_HEADS = 2
A_WINDOW = 128
B_HEADS = 8
NA_KH = 8
NA_KW = 16
NA_QROWS = 4
NA_KROWS = NA_QROWS + NA_KH
ROPE_BASE = 10000.0
N_EXPERTS = 256
TOP_K = 8
N_GROUPS = 8
TOPK_GROUPS = 4
ROUTED_SCALE = 2.5
LOG2E = 1.4426950408889634
EPS = 1e-6
NEG_INF = -1e30
TOKEN_TILE = 512
POS_TILE = 2048
EXPERT_ROWS = 256
SC_WINDOW = 128
VMEM_LIMIT = 56 * 1024 * 1024

_NT = (((1,), (1,)), ((), ()))


def _params(*sem):
    return pltpu.CompilerParams(dimension_semantics=sem, vmem_limit_bytes=VMEM_LIMIT)


def _silu(v):
    return v * jax.nn.sigmoid(v)


def _rms(v, gain):
    return v * lax.rsqrt(jnp.mean(v * v, axis=-1, keepdims=True) + EPS) * gain


def _pack_bf16_pairs(v):
    n = v.shape[1] // 2
    lo = lax.bitcast_convert_type(v[:, :n].astype(BF16).astype(F32), U32) >> 16
    hi = lax.bitcast_convert_type(v[:, n:].astype(BF16).astype(F32), U32) & jnp.uint32(0xFFFF0000)
    return hi | lo


def _split_bf16(v):
    hi = lax.bitcast_convert_type(lax.bitcast_convert_type(v, U32) & jnp.uint32(0xFFFF0000), F32)
    return hi.astype(BF16), (v - hi).astype(BF16)


def _unpack_bf16_pairs(w):
    lo = lax.bitcast_convert_type(w << 16, F32)
    hi = lax.bitcast_convert_type(w & jnp.uint32(0xFFFF0000), F32)
    return jnp.concatenate([lo, hi], axis=1)


def _ada_kernel(c_ref, w_ref, b_ref, o_ref):
    a = _silu(c_ref[...])
    o_ref[...] = jnp.dot(a, w_ref[...], preferred_element_type=F32,
                         precision=lax.Precision.HIGHEST) + b_ref[...]


def _ada(c8, w, b):
    d, n = w.shape
    bn = n // 4
    return pl.pallas_call(
        _ada_kernel,
        out_shape=jax.ShapeDtypeStruct((8, n), F32),
        grid=(n // bn,),
        in_specs=[pl.BlockSpec((8, d), lambda j: (0, 0)),
                  pl.BlockSpec((d, bn), lambda j: (0, j)),
                  pl.BlockSpec((1, bn), lambda j: (0, j))],
        out_specs=pl.BlockSpec((8, bn), lambda j: (0, j)),
        compiler_params=_params("arbitrary"),
        name="ada",
    )(c8, w, b)


_QA, _QB, _KB, _VB, _KA, _VA = 0, 4, 8, 12, 16, 17
_OUT_BLOCKS = 18


def _inproj_kernel(x_ref, mod_ref, n1_ref, w_ref, g_ref, cos_ref, sin_ref, bd_ref, o_ref, *, mod_row, d):
    b = pl.program_id(0) if mod_row is None else mod_row
    xn = _rms(x_ref[0], n1_ref[...])
    sh = mod_ref[pl.ds(b, 1), 0:d]
    sc = mod_ref[pl.ds(b, 1), d:2 * d]
    h = (xn * (1.0 + sc) + sh).astype(BF16)
    z = jnp.dot(h, w_ref[...], preferred_element_type=F32)
    bd = bd_ref[...]
    cos = cos_ref[...]
    sin = sin_ref[...]

    def blk(j):
        return z[:, j * LANES:(j + 1) * LANES]

    def head_rinv(zb):
        ms = jnp.dot((zb * zb).astype(BF16), bd, preferred_element_type=F32)
        return lax.rsqrt(ms + EPS)

    def put(j, v):
        o_ref[0, :, j * LANES:(j + 1) * LANES] = v.astype(BF16)

    quarter = HEAD_DIM // 4
    first_half = (lax.broadcasted_iota(I32, (1, LANES), 1) % (2 * quarter)) < quarter

    def roped(j, g_row):
        zb = blk(j)
        zn = zb * head_rinv(zb) * g_ref[g_row:g_row + 1, :]
        partner = jnp.where(first_half, pltpu.roll(zn, LANES - quarter, 1), pltpu.roll(zn, quarter, 1))
        put(j, zn * cos + partner * sin)

    def normed(j, g_row):
        zb = blk(j)
        put(j, zb * head_rinv(zb) * g_ref[g_row:g_row + 1, :])

    for j in range(4):
        roped(_QA + j, 0)
        normed(_QB + j, 1)
        normed(_KB + j, 2)
        put(_VB + j, blk(_VB + j))
    roped(_KA, 3)
    put(_VA, blk(_VA))


def _inproj(x, mod, n1, w_ext, gains, cos_t, sin_t, bd, *, mod_row, tm):
    bsz, s, d = x.shape
    kern = functools.partial(_inproj_kernel, mod_row=mod_row, d=d)
    return pl.pallas_call(
        kern,
        out_shape=jax.ShapeDtypeStruct((bsz, s, _OUT_BLOCKS * LANES), BF16),
        grid=(bsz, s // tm),
        in_specs=[pl.BlockSpec((1, tm, d), lambda b, i: (b, i, 0)),
                  pl.BlockSpec(mod.shape, lambda b, i: (0, 0)),
                  pl.BlockSpec((1, d), lambda b, i: (0, 0)),
                  pl.BlockSpec(w_ext.shape, lambda b, i: (0, 0)),
                  pl.BlockSpec(gains.shape, lambda b, i: (0, 0)),
                  pl.BlockSpec((tm, LANES), lambda b, i: (i, 0)),
                  pl.BlockSpec((tm, LANES), lambda b, i: (i, 0)),
                  pl.BlockSpec(bd.shape, lambda b, i: (0, 0))],
        out_specs=pl.BlockSpec((1, tm, _OUT_BLOCKS * LANES), lambda b, i: (b, i, 0)),
        compiler_params=_params("arbitrary", "arbitrary"),
        name="inproj",
    )(x, mod, n1, w_ext, gains, cos_t, sin_t, bd)


def _split_pair(qp, lo):
    zero = jnp.zeros_like(qp)
    return jnp.concatenate([jnp.where(lo, qp, zero), jnp.where(lo, zero, qp)], axis=0)


def _softmax_pv(s_parts, v_parts, sink_rep):
    chunks = [s[:, c * LANES:(c + 1) * LANES] for s in s_parts for c in range(s.shape[1] // LANES)]
    m = jnp.max(functools.reduce(jnp.maximum, chunks), axis=-1, keepdims=True)
    m_rep = jnp.broadcast_to(m, (m.shape[0], LANES))
    if sink_rep is not None:
        m_rep = jnp.maximum(m_rep, sink_rep)
    acc = None
    for s, v in zip(s_parts, v_parts):
        p = jnp.concatenate([jnp.exp2(s[:, c * LANES:(c + 1) * LANES] - m_rep)
                             for c in range(s.shape[1] // LANES)], axis=1).astype(BF16)
        v_ext = jnp.concatenate([v, jnp.ones_like(v)], axis=1)
        o = jnp.dot(p, v_ext, preferred_element_type=F32)
        acc = o if acc is None else acc + o
    l_rep = acc[:, LANES:]
    if sink_rep is not None:
        l_rep = l_rep + jnp.exp2(sink_rep - m_rep)
    return acc[:, :LANES] * (1.0 / l_rep)


A_BLOCKS_PER_STEP = 8


def _attn_a_kernel(q_ref, *refs):
    nq = A_BLOCKS_PER_STEP
    k_refs, v_refs = refs[:nq + 2], refs[nq + 2:2 * nq + 4]
    kx_ref, vx_ref, sink_ref = refs[2 * nq + 4:2 * nq + 7]
    mask_refs = refs[2 * nq + 7:3 * nq + 7]
    o_ref = refs[-1]
    tq = A_WINDOW
    lo = lax.broadcasted_iota(I32, (tq, LANES), 1) < HEAD_DIM
    for h in range(nq):
        q = q_ref[0, h * tq:(h + 1) * tq]
        qs = jnp.concatenate([_split_pair(q[:, p * LANES:(p + 1) * LANES], lo) for p in range(4)], axis=0)
        k = jnp.concatenate([r[0] for r in k_refs[h:h + 3]], axis=0)
        v = jnp.concatenate([r[0] for r in v_refs[h:h + 3]], axis=0)
        s_loc = lax.dot_general(qs, k, _NT, preferred_element_type=F32) + mask_refs[h][0]
        s_ctx = lax.dot_general(qs, kx_ref[0], _NT, preferred_element_type=F32)
        o = _softmax_pv([s_loc, s_ctx], [v, vx_ref[0]], sink_ref[...])
        for p in range(4):
            o_lo = o[(2 * p) * tq:(2 * p + 1) * tq]
            o_hi = o[(2 * p + 1) * tq:(2 * p + 2) * tq]
            o_ref[0, h * tq:(h + 1) * tq, p * LANES:(p + 1) * LANES] = jnp.where(lo, o_lo, o_hi).astype(BF16)


def _attn_a(zq, zc, sink_rep, mask):
    bsz, s, _ = zq.shape
    lc = zc.shape[1]
    tq = A_WINDOW
    nq = A_BLOCKS_PER_STEP
    nblk = s // tq
    ka, va = _KA, _VA

    def kv_spec(col, shift):
        return pl.BlockSpec((1, tq, LANES), lambda b, j: (b, jnp.clip(nq * j + shift, 0, nblk - 1), col))

    def mask_spec(h):
        def variant(j):
            blk = nq * j + h
            return jnp.where(blk == 0, 0, jnp.where(blk == nblk - 1, 2, 1))
        return pl.BlockSpec((1,) + mask.shape[1:], lambda b, j: (variant(j), 0, 0))

    shifts = range(-1, nq + 1)
    return pl.pallas_call(
        _attn_a_kernel,
        out_shape=jax.ShapeDtypeStruct((bsz, s, 4 * LANES), BF16),
        grid=(bsz, nblk // nq),
        in_specs=([pl.BlockSpec((1, nq * tq, 4 * LANES), lambda b, j: (b, j, 0))]
                  + [kv_spec(ka, sh) for sh in shifts] + [kv_spec(va, sh) for sh in shifts]
                  + [pl.BlockSpec((1, lc, LANES), lambda b, j: (b, 0, ka)),
                     pl.BlockSpec((1, lc, LANES), lambda b, j: (b, 0, va)),
                     pl.BlockSpec(sink_rep.shape, lambda b, j: (0, 0))]
                  + [mask_spec(h) for h in range(nq)]),
        out_specs=pl.BlockSpec((1, nq * tq, 4 * LANES), lambda b, j: (b, j, 0)),
        compiler_params=_params("arbitrary", "arbitrary"),
        name="attn_a",
    )(*([zq] * (2 * nq + 5) + [zc, zc, sink_rep] + [mask] * nq))


def _attn_a_mask(tq):
    qi = np.arange(tq)[:, None]
    kj = np.arange(3 * tq)[None, :]
    ok = (kj >= qi) & (kj <= qi + 2 * tq)
    variants = [ok & (kj >= tq), ok, ok & (kj < 2 * tq)]
    m = np.stack([np.tile(np.where(v, 0.0, NEG_INF).astype(np.float32), (A_Q_HEADS, 1)) for v in variants])
    return jnp.asarray(m)


NA_GROUPS_PER_STEP = 4


def _attn_b_kernel(q_ref, *refs):
    ngs = NA_GROUPS_PER_STEP
    nkv = ngs + 2
    k_refs, v_refs = refs[:nkv], refs[nkv:2 * nkv]
    kx_ref, vx_ref = refs[2 * nkv:2 * nkv + 2]
    tab_refs = refs[2 * nkv + 2:2 * nkv + 2 + ngs]
    o_ref = refs[-1]
    tq = NA_QROWS * GRID_W
    lo = lax.broadcasted_iota(I32, (tq, LANES), 1) < HEAD_DIM
    for g in range(ngs):
        rows_g = slice(g * tq, (g + 1) * tq)
        for p in range(4):
            sl = slice(p * LANES, (p + 1) * LANES)
            qs = _split_pair(q_ref[0, rows_g, sl], lo)
            k = jnp.concatenate([r[0, :, sl] for r in k_refs[g:g + 3]], axis=0)
            v = jnp.concatenate([r[0, :, sl] for r in v_refs[g:g + 3]], axis=0)
            s_loc = lax.dot_general(qs, k, _NT, preferred_element_type=F32) + tab_refs[g][0, p]
            s_ctx = lax.dot_general(qs, kx_ref[0, :, sl], _NT, preferred_element_type=F32)
            o = _softmax_pv([s_loc, s_ctx], [v, vx_ref[0, :, sl]], None)
            o_ref[0, rows_g, sl] = jnp.where(lo, o[:tq], o[tq:]).astype(BF16)


def _attn_b(zq, zc, table):
    bsz, s, _ = zq.shape
    lc = zc.shape[1]
    tq = NA_QROWS * GRID_W
    ngs = NA_GROUPS_PER_STEP
    ng = s // tq
    qb, kb, vb = _QB // 4, _KB // 4, _VB // 4

    def kv_spec(col, j):
        return pl.BlockSpec((1, tq, 4 * LANES), lambda i, b: (b, jnp.clip(ngs * i + j - 1, 0, ng - 1), col))

    def tab_spec(g):
        def variant(i):
            grp = ngs * i + g
            return jnp.where(grp == 0, 0, jnp.where(grp == ng - 1, 2, 1))
        return pl.BlockSpec((1,) + table.shape[1:], lambda i, b: (variant(i), 0, 0, 0),
                            pipeline_mode=pl.Buffered(1))

    kv_slots = range(ngs + 2)
    return pl.pallas_call(
        _attn_b_kernel,
        out_shape=jax.ShapeDtypeStruct((bsz, s, 4 * LANES), BF16),
        grid=(ng // ngs, bsz),
        in_specs=([pl.BlockSpec((1, ngs * tq, 4 * LANES), lambda i, b: (b, i, qb))]
                  + [kv_spec(kb, j) for j in kv_slots] + [kv_spec(vb, j) for j in kv_slots]
                  + [pl.BlockSpec((1, lc, 4 * LANES), lambda i, b: (b, 0, kb)),
                     pl.BlockSpec((1, lc, 4 * LANES), lambda i, b: (b, 0, vb))]
                  + [tab_spec(g) for g in range(ngs)]),
        out_specs=pl.BlockSpec((1, ngs * tq, 4 * LANES), lambda i, b: (b, i, 0)),
        compiler_params=_params("arbitrary", "arbitrary"),
        name="attn_b",
    )(*([zq] * (2 * ngs + 5) + [zc, zc] + [table] * ngs))


def _na_table(rpb, rows):
    ng = rows // NA_QROWS
    nq, nk = NA_QROWS * GRID_W, NA_KROWS * GRID_W
    qc = np.arange(GRID_W)
    kc = np.arange(GRID_W)
    ws = np.clip(qc - NA_KW // 2, 0, GRID_W - NA_KW)
    valid_c = (kc[None, :] >= ws[:, None]) & (kc[None, :] < ws[:, None] + NA_KW)
    dc = np.clip(kc[None, :] - qc[:, None], -(NA_KW - 1), NA_KW - 1) + (NA_KW - 1)
    c_sel = (dc[..., None] == np.arange(2 * NA_KW - 1)) & valid_c[..., None]
    tiles = jnp.einsum('hab,uvb->huav', rpb.astype(F32), jnp.asarray(c_sel, F32), precision=lax.Precision.HIGHEST)
    tiles = jnp.where(jnp.asarray(valid_c)[None, :, None, :], tiles * LOG2E, NEG_INF)
    tiles = tiles.reshape(B_HEADS, GRID_W, (2 * NA_KH - 1) * GRID_W)

    def masked(n_key_rows):
        return jnp.full((B_HEADS, GRID_W, n_key_rows * GRID_W), NEG_INF, F32)

    tabs = []
    for i in (0, 1, ng - 1):
        start = NA_QROWS * i - NA_KH // 2
        q_rows = []
        for qr in range(NA_QROWS):
            r = NA_QROWS * i + qr
            rs = int(np.clip(r - NA_KH // 2, 0, rows - NA_KH))
            y0 = rs - start
            d0 = rs - r + (NA_KH - 1)
            q_rows.append(jnp.concatenate([masked(y0), tiles[:, :, d0 * GRID_W:(d0 + NA_KH) * GRID_W],
                                           masked(NA_KROWS - NA_KH - y0)], axis=-1))
        tabs.append(jnp.concatenate(q_rows, axis=-2))
    return jnp.stack(tabs).reshape(3, B_HEADS // 2, 2 * nq, nk)


def _merge_kernel(oa_ref, ob_ref, x_ref, mod_ref, ga_ref, gb_ref, wo_ref, n2_ref, wrh_ref, wrl_ref, rb_ref,
                  wsg_ref, wsu_ref, wsd_ref, xres_ref, hpa_ref, hpb_ref, idx_ref, gate_ref, *, d):
    b = pl.program_id(0)
    tm = x_ref.shape[1]

    def mod(k):
        return mod_ref[pl.ds(b, 1), k * d:(k + 1) * d]

    na = _rms(oa_ref[0].astype(F32), ga_ref[...])
    nb = _rms(ob_ref[0].astype(F32), gb_ref[...])
    cat = jnp.concatenate([na, nb], axis=1).astype(BF16)
    y = jnp.dot(cat, wo_ref[...], preferred_element_type=F32)
    x1 = x_ref[0] + mod(2) * y
    h2 = _rms(x1, n2_ref[...]) * (1.0 + mod(4)) + mod(3)

    hb = h2.astype(BF16)
    act = _silu(jnp.dot(hb, wsg_ref[...], preferred_element_type=F32)) * jnp.dot(hb, wsu_ref[...],
                                                                                 preferred_element_type=F32)
    shared = jnp.dot(act.astype(BF16), wsd_ref[...], preferred_element_type=F32)
    xres_ref[0] = x1 + mod(5) * shared
    packed = _pack_bf16_pairs(h2)
    hpa_ref[...] = packed[:, :d // 4]
    hpb_ref[...] = packed[:, d // 4:]

    h_hi, h_lo = _split_bf16(h2)
    logits = (lax.dot_general(wrh_ref[...], h_hi, _NT, preferred_element_type=F32)
              + lax.dot_general(wrh_ref[...], h_lo, _NT, preferred_element_type=F32)
              + lax.dot_general(wrl_ref[...], h_hi, _NT, preferred_element_type=F32))
    scores = jax.nn.sigmoid(logits)
    sel = scores + rb_ref[...]
    per = N_EXPERTS // N_GROUPS
    g3 = sel.reshape(N_GROUPS, per, tm)
    it3 = lax.broadcasted_iota(I32, (N_GROUPS, per, tm), 1)
    m1 = jnp.max(g3, axis=1, keepdims=True)
    first = jnp.min(jnp.where(g3 == m1, it3, per), axis=1, keepdims=True)
    m2 = jnp.max(jnp.where(it3 == first, -jnp.inf, g3), axis=1, keepdims=True)
    gscore = (m1 + m2).reshape(N_GROUPS, tm)

    itg = lax.broadcasted_iota(I32, (N_GROUPS, tm), 0)
    gsel = jnp.zeros((N_GROUPS, tm), F32)
    cur = gscore
    for _ in range(TOPK_GROUPS):
        mx = jnp.max(cur, axis=0, keepdims=True)
        fi = jnp.min(jnp.where(cur == mx, itg, N_GROUPS), axis=0, keepdims=True)
        pick = itg == fi
        gsel = jnp.where(pick, 1.0, gsel)
        cur = jnp.where(pick, -jnp.inf, cur)
    emask = jnp.broadcast_to(gsel.reshape(N_GROUPS, 1, tm), (N_GROUPS, per, tm)).reshape(N_EXPERTS, tm) > 0.5

    ite = lax.broadcasted_iota(I32, (N_EXPERTS, tm), 0)
    cur = jnp.where(emask, sel, NEG_INF)
    idx_rows, s_rows = [], []
    for _ in range(TOP_K):
        mx = jnp.max(cur, axis=0, keepdims=True)
        fi = jnp.min(jnp.where(cur == mx, ite, N_EXPERTS), axis=0, keepdims=True)
        pick = ite == fi
        idx_rows.append(fi)
        s_rows.append(jnp.sum(jnp.where(pick, scores, 0.0), axis=0, keepdims=True))
        cur = jnp.where(pick, -jnp.inf, cur)
    top_s = jnp.concatenate(s_rows, axis=0)
    idx_ref[...] = jnp.concatenate(idx_rows, axis=0)
    gate_ref[...] = top_s / jnp.sum(top_s, axis=0, keepdims=True) * ROUTED_SCALE


def _merge(o_a, o_b, x, mod, ga, gb, wo, n2, wr_hi, wr_lo, rb_col, wsg, wsu, wsd, *, tm):
    bsz, s, d = x.shape
    nt = s // tm
    n = bsz * s
    full = lambda a: pl.BlockSpec(a.shape, lambda b, i: (0,) * a.ndim)
    return pl.pallas_call(
        functools.partial(_merge_kernel, d=d),
        out_shape=(jax.ShapeDtypeStruct((bsz, s, d), F32),
                   jax.ShapeDtypeStruct((n, d // 4), U32),
                   jax.ShapeDtypeStruct((n, d // 4), U32),
                   jax.ShapeDtypeStruct((TOP_K, n), I32),
                   jax.ShapeDtypeStruct((TOP_K, n), F32)),
        grid=(bsz, nt),
        in_specs=[pl.BlockSpec((1, tm, d // 2), lambda b, i: (b, i, 0)),
                  pl.BlockSpec((1, tm, d // 2), lambda b, i: (b, i, 0)),
                  pl.BlockSpec((1, tm, d), lambda b, i: (b, i, 0)),
                  full(mod), full(ga), full(gb), full(wo), full(n2), full(wr_hi), full(wr_lo), full(rb_col),
                  full(wsg), full(wsu), full(wsd)],
        out_specs=(pl.BlockSpec((1, tm, d), lambda b, i: (b, i, 0)),
                   pl.BlockSpec((tm, d // 4), lambda b, i: (b * nt + i, 0)),
                   pl.BlockSpec((tm, d // 4), lambda b, i: (b * nt + i, 0)),
                   pl.BlockSpec((TOP_K, tm), lambda b, i: (0, b * nt + i)),
                   pl.BlockSpec((TOP_K, tm), lambda b, i: (0, b * nt + i))),
        compiler_params=_params("arbitrary", "arbitrary"),
        name="merge",
    )(o_a, o_b, x, mod, ga, gb, wo, n2, wr_hi, wr_lo, rb_col, wsg, wsu, wsd)


def _rank_kernel(idx_ref, rank_ref, cnt_ref, carry_ref):
    tm = idx_ref.shape[1]

    @pl.when(pl.program_id(0) == 0)
    def _():
        carry_ref[...] = jnp.zeros_like(carry_ref)

    idx = idx_ref[...]
    ite = lax.broadcasted_iota(I32, (N_EXPERTS, tm), 0)
    before = (lax.broadcasted_iota(I32, (tm, tm), 0) < lax.broadcasted_iota(I32, (tm, tm), 1)).astype(BF16)
    hits = [ite == idx[k:k + 1, :] for k in range(TOP_K)]
    routed = jnp.where(functools.reduce(jnp.logical_or, hits), 1.0, 0.0)
    base = carry_ref[...]
    ahead = jnp.dot(routed.astype(BF16), before, preferred_element_type=F32) + base
    rows = [jnp.sum(jnp.where(hit, ahead, 0.0), axis=0, keepdims=True) for hit in hits]
    rank_ref[...] = jnp.concatenate(rows, axis=0).astype(I32)
    total = base + jnp.sum(routed, axis=1, keepdims=True)
    carry_ref[...] = total
    cnt_ref[...] = total


def _rank(idx, *, tm):
    n = idx.shape[1]
    return pl.pallas_call(
        _rank_kernel,
        out_shape=(jax.ShapeDtypeStruct((TOP_K, n), I32), jax.ShapeDtypeStruct((N_EXPERTS, 1), F32)),
        grid=(n // tm,),
        in_specs=[pl.BlockSpec((TOP_K, tm), lambda i: (0, i))],
        out_specs=(pl.BlockSpec((TOP_K, tm), lambda i: (0, i)), pl.BlockSpec((N_EXPERTS, 1), lambda i: (0, 0))),
        scratch_shapes=[pltpu.VMEM((N_EXPERTS, 1), F32)],
        compiler_params=_params("arbitrary"),
        name="rank",
    )(idx)


def _pos_kernel(idx_ref, rank_ref, pstart_ref, pos_ref):
    tm = idx_ref.shape[1]
    idx = idx_ref[...]
    ite = lax.broadcasted_iota(I32, (N_EXPERTS, tm), 0)
    pstart = pstart_ref[...]
    rows = [jnp.sum(jnp.where(ite == idx[k:k + 1, :], pstart, 0.0), axis=0, keepdims=True) for k in range(TOP_K)]
    pos_ref[...] = jnp.concatenate(rows, axis=0).astype(I32) + rank_ref[...]


def _pos(idx, rank, pstart_col, *, tm):
    n = idx.shape[1]
    return pl.pallas_call(
        _pos_kernel,
        out_shape=jax.ShapeDtypeStruct((TOP_K, n), I32),
        grid=(n // tm,),
        in_specs=[pl.BlockSpec((TOP_K, tm), lambda i: (0, i)),
                  pl.BlockSpec((TOP_K, tm), lambda i: (0, i)),
                  pl.BlockSpec((N_EXPERTS, 1), lambda i: (0, 0))],
        out_specs=pl.BlockSpec((TOP_K, tm), lambda i: (0, i)),
        compiler_params=_params("arbitrary"),
        name="pos",
    )(idx, rank, pstart_col)


UNIT_CHUNKS = (4, 2, 1)
RING_AHEAD = 8
RING_SLOTS = RING_AHEAD + UNIT_CHUNKS[0]


W_SLOTS = 3


def _expert_kernel(ps_ref, cnt_ref, tot_ref, wg_hbm, wu_hbm, wd_hbm, xa_hbm, xb_hbm, ya_hbm, yb_hbm,
                   wg_buf, wu_buf, wd_buf, xa_buf, xb_buf, ya_buf, yb_buf, w_sem, in_sem, out_sem):
    e = pl.program_id(0)
    n_exp = pl.num_programs(0)
    r = EXPERT_ROWS
    ahead = RING_AHEAD
    w_slot = e % W_SLOTS

    def wfetch(x):
        slot = x % W_SLOTS
        return (pltpu.make_async_copy(wg_hbm.at[x], wg_buf.at[slot], w_sem.at[0, slot]),
                pltpu.make_async_copy(wu_hbm.at[x], wu_buf.at[slot], w_sem.at[1, slot]),
                pltpu.make_async_copy(wd_hbm.at[x], wd_buf.at[slot], w_sem.at[2, slot]))
    total = tot_ref[0]
    first = ps_ref[e] // r
    cnt = cnt_ref[e]
    nch = (cnt + r - 1) // r

    def rows_of(g):
        return pl.ds(pl.multiple_of(g * r, r), r)

    def fetch(g):
        slot = g % RING_SLOTS
        return (pltpu.make_async_copy(xa_hbm.at[rows_of(g)], xa_buf.at[slot], in_sem.at[0, slot]),
                pltpu.make_async_copy(xb_hbm.at[rows_of(g)], xb_buf.at[slot], in_sem.at[1, slot]))

    def flush(g):
        slot = g % RING_SLOTS
        return (pltpu.make_async_copy(ya_buf.at[slot], ya_hbm.at[rows_of(g)], out_sem.at[0, slot]),
                pltpu.make_async_copy(yb_buf.at[slot], yb_hbm.at[rows_of(g)], out_sem.at[1, slot]))

    def start(copies):
        for cp in copies:
            cp.start()

    def wait(copies):
        for cp in copies:
            cp.wait()

    @pl.when(e == 0)
    def _():
        for g in range(ahead):
            @pl.when(g < total)
            def _():
                start(fetch(g))
        for x in range(W_SLOTS - 1):
            @pl.when(x < n_exp)
            def _():
                start(wfetch(x))

    wait(wfetch(e))

    @pl.when(e + W_SLOTS - 1 < n_exp)
    def _():
        start(wfetch(e + W_SLOTS - 1))

    @pl.when(nch > 0)
    def _():
        def unit(c0, u):
            for j in range(u):
                g = first + c0 + j
                wait(fetch(g))

                @pl.when(g + ahead < total)
                def _():
                    start(fetch(g + ahead))

                @pl.when(g >= RING_SLOTS)
                def _():
                    wait(flush(g - RING_SLOTS))

            slots = [(first + c0 + j) % RING_SLOTS for j in range(u)]
            packed = jnp.concatenate([jnp.concatenate([xa_buf[sl], xb_buf[sl]], axis=1) for sl in slots], axis=0)
            row = lax.broadcasted_iota(I32, packed.shape, 0)
            packed = jnp.where(row < cnt - c0 * r, packed, jnp.zeros_like(packed))
            x = _unpack_bf16_pairs(packed).astype(BF16)
            gate = jnp.dot(x, wg_buf[w_slot].astype(BF16), preferred_element_type=F32)
            up = jnp.dot(x, wu_buf[w_slot].astype(BF16), preferred_element_type=F32)
            y = jnp.dot((_silu(gate) * up).astype(BF16), wd_buf[w_slot].astype(BF16), preferred_element_type=F32)
            out = _pack_bf16_pairs(y)
            half = out.shape[1] // 2
            for j, sl in enumerate(slots):
                ya_buf[sl] = out[j * r:(j + 1) * r, :half]
                yb_buf[sl] = out[j * r:(j + 1) * r, half:]
                start(flush(first + c0 + j))

        big = UNIT_CHUNKS[0]

        def big_unit(i, carry):
            unit(i * big, big)
            return carry

        lax.fori_loop(0, nch // big, big_unit, 0)
        done = nch // big * big
        for u in UNIT_CHUNKS[1:]:
            @pl.when((nch % (2 * u)) >= u)
            def _():
                unit(done, u)

            done = done + jnp.where((nch % (2 * u)) >= u, u, 0)

    @pl.when(e == pl.num_programs(0) - 1)
    def _():
        for k in range(RING_SLOTS):
            @pl.when(total - 1 - k >= 0)
            def _():
                wait(flush(total - 1 - k))


def _experts(pstart, cnt, total_chunks, xs_a, xs_b, wg, wu, wd):
    rows, hw = xs_a.shape
    r = EXPERT_ROWS
    n_exp, d, f = wg.shape
    hbm = pl.BlockSpec(memory_space=pl.ANY)
    grid_spec = pltpu.PrefetchScalarGridSpec(
        num_scalar_prefetch=3,
        grid=(n_exp,),
        in_specs=[hbm, hbm, hbm, hbm, hbm],
        out_specs=(hbm, hbm),
        scratch_shapes=[pltpu.VMEM((W_SLOTS, d, f), F32), pltpu.VMEM((W_SLOTS, d, f), F32),
                        pltpu.VMEM((W_SLOTS, f, d), F32),
                        pltpu.VMEM((RING_SLOTS, r, hw), U32), pltpu.VMEM((RING_SLOTS, r, hw), U32),
                        pltpu.VMEM((RING_SLOTS, r, hw), U32), pltpu.VMEM((RING_SLOTS, r, hw), U32),
                        pltpu.SemaphoreType.DMA((3, W_SLOTS)),
                        pltpu.SemaphoreType.DMA((2, RING_SLOTS)), pltpu.SemaphoreType.DMA((2, RING_SLOTS))],
    )
    return pl.pallas_call(
        _expert_kernel,
        out_shape=(jax.ShapeDtypeStruct((rows, hw), U32), jax.ShapeDtypeStruct((rows, hw), U32)),
        grid_spec=grid_spec,
        compiler_params=_params("arbitrary"),
        name="experts",
    )(pstart, cnt, total_chunks, wg, wu, wd, xs_a, xs_b)


def _sc_scatter_rows(xs, idx_flat, n_rows):
    n, w = xs[0].shape
    m = idx_flat.shape[1]
    nwin = n // SC_WINDOW
    reps = m // n
    mesh = plsc.VectorSubcoreMesh(core_axis_name="core", subcore_axis_name="subcore")
    out_type = tuple(jax.ShapeDtypeStruct((n_rows, w), x.dtype) for x in xs)

    @functools.partial(pl.kernel, out_type=out_type, mesh=mesh, scratch_types=[])
    def scatter_kernel(*refs):
        x_hbms, i_hbm, o_hbms = refs[:len(xs)], refs[len(xs)], refs[len(xs) + 1:]
        for x_hbm, o_hbm in zip(x_hbms, o_hbms):
            def body(x_vmem, i_vmem, o_hbm=o_hbm):
                pltpu.sync_copy(x_vmem, o_hbm.at[i_vmem.at[0]])

            pltpu.emit_pipeline(
                body,
                grid=(nwin, reps),
                in_specs=[pl.BlockSpec((SC_WINDOW, w), lambda i, k: (i, 0)),
                          pl.BlockSpec((1, SC_WINDOW), lambda i, k: (0, k * nwin + i))],
                out_specs=[],
                core_axis_name=("core", "subcore"),
                dimension_semantics=(pltpu.PARALLEL, pltpu.ARBITRARY),
            )(x_hbm, i_hbm)

    return scatter_kernel(*xs, idx_flat)


def _sc_gather_rows(srcs, idx_flat):
    m = idx_flat.shape[1]
    w = srcs[0].shape[1]
    mesh = plsc.VectorSubcoreMesh(core_axis_name="core", subcore_axis_name="subcore")
    out_type = tuple(jax.ShapeDtypeStruct((m, w), src.dtype) for src in srcs)

    @functools.partial(pl.kernel, out_type=out_type, mesh=mesh)
    def gather_kernel(*refs):
        x_hbms, i_hbm, o_hbms = refs[:len(srcs)], refs[len(srcs)], refs[len(srcs) + 1:]
        for x_hbm, o_hbm in zip(x_hbms, o_hbms):
            def body(i_vmem, o_vmem, x_hbm=x_hbm):
                pltpu.sync_copy(x_hbm.at[i_vmem.at[0]], o_vmem)

            pltpu.emit_pipeline(
                body,
                grid=(m // SC_WINDOW,),
                in_specs=[pl.BlockSpec((1, SC_WINDOW), lambda i: (0, i))],
                out_specs=[pl.BlockSpec((SC_WINDOW, w), lambda i: (i, 0))],
                core_axis_name=("core", "subcore"),
                dimension_semantics=(pltpu.PARALLEL,),
            )(i_hbm, o_hbm)

    return gather_kernel(*srcs, idx_flat)


def _finish_kernel(gate_ref, xres_ref, mod_ref, ya_ref, yb_ref, o_ref, *, d, tiles_per_batch):
    tm = gate_ref.shape[1]
    b = pl.program_id(0) // tiles_per_batch
    gates = gate_ref[...].T
    acc = jnp.zeros((tm, d), F32)
    for k in range(TOP_K):
        packed = jnp.concatenate([ya_ref[k], yb_ref[k]], axis=1)
        acc = acc + _unpack_bf16_pairs(packed) * gates[:, k:k + 1]
    o_ref[...] = xres_ref[...] + mod_ref[pl.ds(b, 1), 5 * d:6 * d] * acc


def _finish(gates, xres, mod, yg_a, yg_b, *, tm, tiles_per_batch):
    n, d = xres.shape
    return pl.pallas_call(
        functools.partial(_finish_kernel, d=d, tiles_per_batch=tiles_per_batch),
        out_shape=jax.ShapeDtypeStruct((n, d), F32),
        grid=(n // tm,),
        in_specs=[pl.BlockSpec((TOP_K, tm), lambda i: (0, i)),
                  pl.BlockSpec((tm, d), lambda i: (i, 0)),
                  pl.BlockSpec(mod.shape, lambda i: (0, 0)),
                  pl.BlockSpec((TOP_K, tm, d // 4), lambda i: (0, i, 0)),
                  pl.BlockSpec((TOP_K, tm, d // 4), lambda i: (0, i, 0))],
        out_specs=pl.BlockSpec((tm, d), lambda i: (i, 0)),
        compiler_params=_params("arbitrary"),
        name="finish",
    )(gates, xres, mod, yg_a, yg_b)


_QA_HEAD_ORDER = (0, 4, 1, 5, 2, 6, 3, 7)


def _reorder_qa_heads(a, axis):
    return jnp.concatenate([lax.slice_in_dim(a, h * HEAD_DIM, (h + 1) * HEAD_DIM, axis=axis)
                            for h in _QA_HEAD_ORDER], axis=axis)


def _rope_tables(s):
    quarter = HEAD_DIM // 4
    t = jnp.arange(s)
    row = (t // GRID_W).astype(F32)
    col = (t % GRID_W).astype(F32)
    freqs = ROPE_BASE ** (-jnp.arange(quarter, dtype=F32) / quarter)
    ar = row[:, None] * freqs[None, :]
    ac = col[:, None] * freqs[None, :]
    cos = jnp.concatenate([jnp.cos(ar), jnp.cos(ar), jnp.cos(ac), jnp.cos(ac)], axis=1)
    sin = jnp.concatenate([-jnp.sin(ar), jnp.sin(ar), -jnp.sin(ac), jnp.sin(ac)], axis=1)
    return jnp.tile(cos, (1, 2)), jnp.tile(sin, (1, 2))


def kernel(x, c, ctx, c_ctx, w_ada, b_ada, norm1, norm2, w_in, q_norm_a, k_norm_a, q_norm_b, k_norm_b, sink_a,
           rpb_b, out_norm_a, out_norm_b, w_out, w_router, router_bias, we_gate, we_up, we_down, ws_gate, ws_up,
           ws_down):
    assert w_ada.shape[0] == 1, "single-layer block"
    bsz, s, d = x.shape
    lc = ctx.shape[1]
    n = bsz * s
    rows = s // GRID_W
    assert d == 1024 and bsz + 1 <= 8 and rows >= NA_KROWS and lc % LANES == 0
    assert s % TOKEN_TILE == 0 and n % SC_WINDOW == 0
    assert (s // A_WINDOW) % A_BLOCKS_PER_STEP == 0 and (s // (NA_QROWS * GRID_W)) % NA_GROUPS_PER_STEP == 0

    c8 = jnp.concatenate([c, c_ctx[None, :], jnp.zeros((8 - bsz - 1, d), F32)], axis=0)
    mod = _ada(c8, w_ada[0], b_ada[0][None, :])

    w = w_in[0]
    aw, akw, bw = A_Q_HEADS * HEAD_DIM, A_KV_HEADS * HEAD_DIM, B_HEADS * HEAD_DIM
    cuts = np.cumsum([0, aw, akw, akw, bw, bw, bw])
    qa_w, ka_w, va_w, qb_w, kb_w, vb_w = [w[:, cuts[i]:cuts[i + 1]] for i in range(6)]
    qa_w = _reorder_qa_heads(qa_w, 1)
    w_ext = jnp.concatenate([qa_w, qb_w, kb_w, vb_w, ka_w, va_w], axis=1).astype(BF16)
    scale = HEAD_DIM ** -0.5 * LOG2E
    pair = lambda g: jnp.tile(g, 2)
    gains = jnp.stack([pair(q_norm_a[0]) * scale, pair(q_norm_b[0]) * scale, pair(k_norm_b[0]), pair(k_norm_a[0])]
                      + [jnp.zeros((LANES,), F32)] * 4)
    bd = jnp.asarray(np.kron(np.eye(2), np.full((HEAD_DIM, HEAD_DIM), 1.0 / HEAD_DIM)), BF16)
    cos_t, sin_t = _rope_tables(s)
    n1 = norm1[0][None, :]
    zq = _inproj(x, mod, n1, w_ext, gains, cos_t, sin_t, bd, mod_row=None, tm=TOKEN_TILE)
    zc = _inproj(ctx, mod, n1, w_ext, gains, jnp.ones((lc, LANES), F32), jnp.zeros((lc, LANES), F32), bd,
                 mod_row=bsz, tm=lc)

    sink = sink_a[0].astype(F32)
    sink_rep = jnp.concatenate([jnp.broadcast_to(sink[h] * LOG2E, (A_WINDOW, LANES)) for h in _QA_HEAD_ORDER])
    o_a = _attn_a(zq, zc, sink_rep, _attn_a_mask(A_WINDOW))
    o_b = _attn_b(zq, zc, _na_table(rpb_b[0], rows))

    ga = _reorder_qa_heads(out_norm_a[0], 0)[None, :]
    gb = out_norm_b[0][None, :]
    wo = jnp.concatenate([_reorder_qa_heads(w_out[0][:aw], 0), w_out[0][aw:]], axis=0).astype(BF16)
    wr_hi, wr_lo = _split_bf16(w_router[0].T)
    xres, hp_a, hp_b, idx, gates = _merge(o_a, o_b, x, mod, ga, gb, wo, norm2[0][None, :], wr_hi, wr_lo,
                                  router_bias[0][:, None], ws_gate[0].astype(BF16), ws_up[0].astype(BF16),
                                  ws_down[0].astype(BF16), tm=TOKEN_TILE)

    rank, counts = _rank(idx, tm=TOKEN_TILE)
    t = EXPERT_ROWS
    cnt = counts[:, 0].astype(I32)
    padded = (cnt + t - 1) // t * t
    pends = jnp.cumsum(padded)
    pstart = pends - padded
    n_rows = n * TOP_K + N_EXPERTS * t
    pos = _pos(idx, rank, pstart.astype(F32)[:, None], tm=min(POS_TILE, n))

    pos_flat = pos.reshape(1, TOP_K * n)
    xs_a, xs_b = _sc_scatter_rows((hp_a, hp_b), pos_flat, n_rows)
    ys_a, ys_b = _experts(pstart, cnt, pends[-1:] // t, xs_a, xs_b, we_gate[0], we_up[0], we_down[0])
    yg_a, yg_b = [y.reshape(TOP_K, n, d // 4) for y in _sc_gather_rows((ys_a, ys_b), pos_flat)]
    out = _finish(gates, xres.reshape(n, d), mod, yg_a, yg_b, tm=TOKEN_TILE, tiles_per_batch=s // TOKEN_TILE)
    return out.reshape(bsz, s, d)
```

```python
import functools

import numpy as np
import jax
import jax.numpy as jnp
from jax import lax
from jax.experimental import pallas as pl
from jax.experimental.pallas import tpu as pltpu
from jax.experimental.pallas import tpu_sc as plsc

F32 = jnp.float32
BF16 = jnp.bfloat16
I32 = jnp.int32
U32 = jnp.uint32

LANES = 128
HEAD_DIM = 64
GRID_W = 64
A_Q_HEADS = 8
A_KV_HEADS = 2
A_WINDOW = 128
B_HEADS = 8
NA_KH = 8
NA_KW = 16
NA_QROWS = 4
NA_KROWS = NA_QROWS + NA_KH
ROPE_BASE = 10000.0
N_EXPERTS = 256
TOP_K = 8
N_GROUPS = 8
TOPK_GROUPS = 4
ROUTED_SCALE = 2.5
LOG2E = 1.4426950408889634
EPS = 1e-6
NEG_INF = -1e30
TOKEN_TILE = 512
POS_TILE = 2048
EXPERT_ROWS = 256
SC_WINDOW = 128
VMEM_LIMIT = 56 * 1024 * 1024

_NT = (((1,), (1,)), ((), ()))


def _params(*sem):
    return pltpu.CompilerParams(dimension_semantics=sem, vmem_limit_bytes=VMEM_LIMIT)


def _silu(v):
    return v * jax.nn.sigmoid(v)


def _rms(v, gain):
    return v * lax.rsqrt(jnp.mean(v * v, axis=-1, keepdims=True) + EPS) * gain


def _pack_bf16_pairs(v):
    n = v.shape[1] // 2
    lo = lax.bitcast_convert_type(v[:, :n].astype(BF16).astype(F32), U32) >> 16
    hi = lax.bitcast_convert_type(v[:, n:].astype(BF16).astype(F32), U32) & jnp.uint32(0xFFFF0000)
    return hi | lo


def _split_bf16(v):
    hi = lax.bitcast_convert_type(lax.bitcast_convert_type(v, U32) & jnp.uint32(0xFFFF0000), F32)
    return hi.astype(BF16), (v - hi).astype(BF16)


def _unpack_bf16_pairs(w):
    lo = lax.bitcast_convert_type(w << 16, F32)
    hi = lax.bitcast_convert_type(w & jnp.uint32(0xFFFF0000), F32)
    return jnp.concatenate([lo, hi], axis=1)


def _ada_kernel(c_ref, w_ref, b_ref, o_ref):
    a = _silu(c_ref[...])
    o_ref[...] = jnp.dot(a, w_ref[...], preferred_element_type=F32,
                         precision=lax.Precision.HIGHEST) + b_ref[...]


def _ada(c8, w, b):
    d, n = w.shape
    bn = n // 4
    return pl.pallas_call(
        _ada_kernel,
        out_shape=jax.ShapeDtypeStruct((8, n), F32),
        grid=(n // bn,),
        in_specs=[pl.BlockSpec((8, d), lambda j: (0, 0)),
                  pl.BlockSpec((d, bn), lambda j: (0, j)),
                  pl.BlockSpec((1, bn), lambda j: (0, j))],
        out_specs=pl.BlockSpec((8, bn), lambda j: (0, j)),
        compiler_params=_params("arbitrary"),
        name="ada",
    )(c8, w, b)


_QA, _QB, _KB, _VB, _KA, _VA = 0, 4, 8, 12, 16, 17
_OUT_BLOCKS = 18


def _inproj_kernel(x_ref, mod_ref, n1_ref, w_ref, g_ref, cos_ref, sin_ref, bd_ref, o_ref, *, mod_row, d):
    b = pl.program_id(0) if mod_row is None else mod_row
    xn = _rms(x_ref[0], n1_ref[...])
    sh = mod_ref[pl.ds(b, 1), 0:d]
    sc = mod_ref[pl.ds(b, 1), d:2 * d]
    h = (xn * (1.0 + sc) + sh).astype(BF16)
    z = jnp.dot(h, w_ref[...], preferred_element_type=F32)
    bd = bd_ref[...]
    cos = cos_ref[...]
    sin = sin_ref[...]

    def blk(j):
        return z[:, j * LANES:(j + 1) * LANES]

    def head_rinv(zb):
        ms = jnp.dot((zb * zb).astype(BF16), bd, preferred_element_type=F32)
        return lax.rsqrt(ms + EPS)

    def put(j, v):
        o_ref[0, :, j * LANES:(j + 1) * LANES] = v.astype(BF16)

    quarter = HEAD_DIM // 4
    first_half = (lax.broadcasted_iota(I32, (1, LANES), 1) % (2 * quarter)) < quarter

    def roped(j, g_row):
        zb = blk(j)
        zn = zb * head_rinv(zb) * g_ref[g_row:g_row + 1, :]
        partner = jnp.where(first_half, pltpu.roll(zn, LANES - quarter, 1), pltpu.roll(zn, quarter, 1))
        put(j, zn * cos + partner * sin)

    def normed(j, g_row):
        zb = blk(j)
        put(j, zb * head_rinv(zb) * g_ref[g_row:g_row + 1, :])

    for j in range(4):
        roped(_QA + j, 0)
        normed(_QB + j, 1)
        normed(_KB + j, 2)
        put(_VB + j, blk(_VB + j))
    roped(_KA, 3)
    put(_VA, blk(_VA))


def _inproj(x, mod, n1, w_ext, gains, cos_t, sin_t, bd, *, mod_row, tm):
    bsz, s, d = x.shape
    kern = functools.partial(_inproj_kernel, mod_row=mod_row, d=d)
    return pl.pallas_call(
        kern,
        out_shape=jax.ShapeDtypeStruct((bsz, s, _OUT_BLOCKS * LANES), BF16),
        grid=(bsz, s // tm),
        in_specs=[pl.BlockSpec((1, tm, d), lambda b, i: (b, i, 0)),
                  pl.BlockSpec(mod.shape, lambda b, i: (0, 0)),
                  pl.BlockSpec((1, d), lambda b, i: (0, 0)),
                  pl.BlockSpec(w_ext.shape, lambda b, i: (0, 0)),
                  pl.BlockSpec(gains.shape, lambda b, i: (0, 0)),
                  pl.BlockSpec((tm, LANES), lambda b, i: (i, 0)),
                  pl.BlockSpec((tm, LANES), lambda b, i: (i, 0)),
                  pl.BlockSpec(bd.shape, lambda b, i: (0, 0))],
        out_specs=pl.BlockSpec((1, tm, _OUT_BLOCKS * LANES), lambda b, i: (b, i, 0)),
        compiler_params=_params("arbitrary", "arbitrary"),
        name="inproj",
    )(x, mod, n1, w_ext, gains, cos_t, sin_t, bd)


def _split_pair(qp, lo):
    zero = jnp.zeros_like(qp)
    return jnp.concatenate([jnp.where(lo, qp, zero), jnp.where(lo, zero, qp)], axis=0)


def _softmax_pv(s_parts, v_parts, sink_rep):
    chunks = [s[:, c * LANES:(c + 1) * LANES] for s in s_parts for c in range(s.shape[1] // LANES)]
    m = jnp.max(functools.reduce(jnp.maximum, chunks), axis=-1, keepdims=True)
    m_rep = jnp.broadcast_to(m, (m.shape[0], LANES))
    if sink_rep is not None:
        m_rep = jnp.maximum(m_rep, sink_rep)
    acc = None
    for s, v in zip(s_parts, v_parts):
        p = jnp.concatenate([jnp.exp2(s[:, c * LANES:(c + 1) * LANES] - m_rep)
                             for c in range(s.shape[1] // LANES)], axis=1).astype(BF16)
        v_ext = jnp.concatenate([v, jnp.ones_like(v)], axis=1)
        o = jnp.dot(p, v_ext, preferred_element_type=F32)
        acc = o if acc is None else acc + o
    l_rep = acc[:, LANES:]
    if sink_rep is not None:
        l_rep = l_rep + jnp.exp2(sink_rep - m_rep)
    return acc[:, :LANES] * (1.0 / l_rep)


A_BLOCKS_PER_STEP = 8


def _attn_a_kernel(q_ref, *refs):
    nq = A_BLOCKS_PER_STEP
    k_refs, v_refs = refs[:nq + 2], refs[nq + 2:2 * nq + 4]
    kx_ref, vx_ref, sink_ref = refs[2 * nq + 4:2 * nq + 7]
    mask_refs = refs[2 * nq + 7:3 * nq + 7]
    o_ref = refs[-1]
    tq = A_WINDOW
    lo = lax.broadcasted_iota(I32, (tq, LANES), 1) < HEAD_DIM
    for h in range(nq):
        q = q_ref[0, h * tq:(h + 1) * tq]
        qs = jnp.concatenate([_split_pair(q[:, p * LANES:(p + 1) * LANES], lo) for p in range(4)], axis=0)
        k = jnp.concatenate([r[0] for r in k_refs[h:h + 3]], axis=0)
        v = jnp.concatenate([r[0] for r in v_refs[h:h + 3]], axis=0)
        s_loc = lax.dot_general(qs, k, _NT, preferred_element_type=F32) + mask_refs[h][0]
        s_ctx = lax.dot_general(qs, kx_ref[0], _NT, preferred_element_type=F32)
        o = _softmax_pv([s_loc, s_ctx], [v, vx_ref[0]], sink_ref[...])
        for p in range(4):
            o_lo = o[(2 * p) * tq:(2 * p + 1) * tq]
            o_hi = o[(2 * p + 1) * tq:(2 * p + 2) * tq]
            o_ref[0, h * tq:(h + 1) * tq, p * LANES:(p + 1) * LANES] = jnp.where(lo, o_lo, o_hi).astype(BF16)


def _attn_a(zq, zc, sink_rep, mask):
    bsz, s, _ = zq.shape
    lc = zc.shape[1]
    tq = A_WINDOW
    nq = A_BLOCKS_PER_STEP
    nblk = s // tq
    ka, va = _KA, _VA

    def kv_spec(col, shift):
        return pl.BlockSpec((1, tq, LANES), lambda b, j: (b, jnp.clip(nq * j + shift, 0, nblk - 1), col))

    def mask_spec(h):
        def variant(j):
            blk = nq * j + h
            return jnp.where(blk == 0, 0, jnp.where(blk == nblk - 1, 2, 1))
        return pl.BlockSpec((1,) + mask.shape[1:], lambda b, j: (variant(j), 0, 0))

    shifts = range(-1, nq + 1)
    return pl.pallas_call(
        _attn_a_kernel,
        out_shape=jax.ShapeDtypeStruct((bsz, s, 4 * LANES), BF16),
        grid=(bsz, nblk // nq),
        in_specs=([pl.BlockSpec((1, nq * tq, 4 * LANES), lambda b, j: (b, j, 0))]
                  + [kv_spec(ka, sh) for sh in shifts] + [kv_spec(va, sh) for sh in shifts]
                  + [pl.BlockSpec((1, lc, LANES), lambda b, j: (b, 0, ka)),
                     pl.BlockSpec((1, lc, LANES), lambda b, j: (b, 0, va)),
                     pl.BlockSpec(sink_rep.shape, lambda b, j: (0, 0))]
                  + [mask_spec(h) for h in range(nq)]),
        out_specs=pl.BlockSpec((1, nq * tq, 4 * LANES), lambda b, j: (b, j, 0)),
        compiler_params=_params("arbitrary", "arbitrary"),
        name="attn_a",
    )(*([zq] * (2 * nq + 5) + [zc, zc, sink_rep] + [mask] * nq))


def _attn_a_mask(tq):
    qi = np.arange(tq)[:, None]
    kj = np.arange(3 * tq)[None, :]
    ok = (kj >= qi) & (kj <= qi + 2 * tq)
    variants = [ok & (kj >= tq), ok, ok & (kj < 2 * tq)]
    m = np.stack([np.tile(np.where(v, 0.0, NEG_INF).astype(np.float32), (A_Q_HEADS, 1)) for v in variants])
    return jnp.asarray(m)


NA_GROUPS_PER_STEP = 4


def _attn_b_kernel(q_ref, *refs):
    ngs = NA_GROUPS_PER_STEP
    nkv = ngs + 2
    k_refs, v_refs = refs[:nkv], refs[nkv:2 * nkv]
    kx_ref, vx_ref = refs[2 * nkv:2 * nkv + 2]
    tab_refs = refs[2 * nkv + 2:2 * nkv + 2 + ngs]
    o_ref = refs[-1]
    tq = NA_QROWS * GRID_W
    lo = lax.broadcasted_iota(I32, (tq, LANES), 1) < HEAD_DIM
    for g in range(ngs):
        rows_g = slice(g * tq, (g + 1) * tq)
        for p in range(4):
            sl = slice(p * LANES, (p + 1) * LANES)
            qs = _split_pair(q_ref[0, rows_g, sl], lo)
            k = jnp.concatenate([r[0, :, sl] for r in k_refs[g:g + 3]], axis=0)
            v = jnp.concatenate([r[0, :, sl] for r in v_refs[g:g + 3]], axis=0)
            s_loc = lax.dot_general(qs, k, _NT, preferred_element_type=F32) + tab_refs[g][0, p]
            s_ctx = lax.dot_general(qs, kx_ref[0, :, sl], _NT, preferred_element_type=F32)
            o = _softmax_pv([s_loc, s_ctx], [v, vx_ref[0, :, sl]], None)
            o_ref[0, rows_g, sl] = jnp.where(lo, o[:tq], o[tq:]).astype(BF16)


def _attn_b(zq, zc, table):
    bsz, s, _ = zq.shape
    lc = zc.shape[1]
    tq = NA_QROWS * GRID_W
    ngs = NA_GROUPS_PER_STEP
    ng = s // tq
    qb, kb, vb = _QB // 4, _KB // 4, _VB // 4

    def kv_spec(col, j):
        return pl.BlockSpec((1, tq, 4 * LANES), lambda i, b: (b, jnp.clip(ngs * i + j - 1, 0, ng - 1), col))

    def tab_spec(g):
        def variant(i):
            grp = ngs * i + g
            return jnp.where(grp == 0, 0, jnp.where(grp == ng - 1, 2, 1))
        return pl.BlockSpec((1,) + table.shape[1:], lambda i, b: (variant(i), 0, 0, 0),
                            pipeline_mode=pl.Buffered(1))

    kv_slots = range(ngs + 2)
    return pl.pallas_call(
        _attn_b_kernel,
        out_shape=jax.ShapeDtypeStruct((bsz, s, 4 * LANES), BF16),
        grid=(ng // ngs, bsz),
        in_specs=([pl.BlockSpec((1, ngs * tq, 4 * LANES), lambda i, b: (b, i, qb))]
                  + [kv_spec(kb, j) for j in kv_slots] + [kv_spec(vb, j) for j in kv_slots]
                  + [pl.BlockSpec((1, lc, 4 * LANES), lambda i, b: (b, 0, kb)),
                     pl.BlockSpec((1, lc, 4 * LANES), lambda i, b: (b, 0, vb))]
                  + [tab_spec(g) for g in range(ngs)]),
        out_specs=pl.BlockSpec((1, ngs * tq, 4 * LANES), lambda i, b: (b, i, 0)),
        compiler_params=_params("arbitrary", "arbitrary"),
        name="attn_b",
    )(*([zq] * (2 * ngs + 5) + [zc, zc] + [table] * ngs))


def _na_table(rpb, rows):
    ng = rows // NA_QROWS
    nq, nk = NA_QROWS * GRID_W, NA_KROWS * GRID_W
    qc = np.arange(GRID_W)
    kc = np.arange(GRID_W)
    ws = np.clip(qc - NA_KW // 2, 0, GRID_W - NA_KW)
    valid_c = (kc[None, :] >= ws[:, None]) & (kc[None, :] < ws[:, None] + NA_KW)
    dc = np.clip(kc[None, :] - qc[:, None], -(NA_KW - 1), NA_KW - 1) + (NA_KW - 1)
    c_sel = (dc[..., None] == np.arange(2 * NA_KW - 1)) & valid_c[..., None]
    tiles = jnp.einsum('hab,uvb->huav', rpb.astype(F32), jnp.asarray(c_sel, F32), precision=lax.Precision.HIGHEST)
    tiles = jnp.where(jnp.asarray(valid_c)[None, :, None, :], tiles * LOG2E, NEG_INF)
    tiles = tiles.reshape(B_HEADS, GRID_W, (2 * NA_KH - 1) * GRID_W)

    def masked(n_key_rows):
        return jnp.full((B_HEADS, GRID_W, n_key_rows * GRID_W), NEG_INF, F32)

    tabs = []
    for i in (0, 1, ng - 1):
        start = NA_QROWS * i - NA_KH // 2
        q_rows = []
        for qr in range(NA_QROWS):
            r = NA_QROWS * i + qr
            rs = int(np.clip(r - NA_KH // 2, 0, rows - NA_KH))
            y0 = rs - start
            d0 = rs - r + (NA_KH - 1)
            q_rows.append(jnp.concatenate([masked(y0), tiles[:, :, d0 * GRID_W:(d0 + NA_KH) * GRID_W],
                                           masked(NA_KROWS - NA_KH - y0)], axis=-1))
        tabs.append(jnp.concatenate(q_rows, axis=-2))
    return jnp.stack(tabs).reshape(3, B_HEADS // 2, 2 * nq, nk)


def _merge_kernel(oa_ref, ob_ref, x_ref, mod_ref, ga_ref, gb_ref, wo_ref, n2_ref, wrh_ref, wrl_ref, rb_ref,
                  wsg_ref, wsu_ref, wsd_ref, xres_ref, hpa_ref, hpb_ref, idx_ref, gate_ref, *, d):
    b = pl.program_id(0)
    tm = x_ref.shape[1]

    def mod(k):
        return mod_ref[pl.ds(b, 1), k * d:(k + 1) * d]

    na = _rms(oa_ref[0].astype(F32), ga_ref[...])
    nb = _rms(ob_ref[0].astype(F32), gb_ref[...])
    cat = jnp.concatenate([na, nb], axis=1).astype(BF16)
    y = jnp.dot(cat, wo_ref[...], preferred_element_type=F32)
    x1 = x_ref[0] + mod(2) * y
    h2 = _rms(x1, n2_ref[...]) * (1.0 + mod(4)) + mod(3)

    hb = h2.astype(BF16)
    act = _silu(jnp.dot(hb, wsg_ref[...], preferred_element_type=F32)) * jnp.dot(hb, wsu_ref[...],
                                                                                 preferred_element_type=F32)
    shared = jnp.dot(act.astype(BF16), wsd_ref[...], preferred_element_type=F32)
    xres_ref[0] = x1 + mod(5) * shared
    packed = _pack_bf16_pairs(h2)
    hpa_ref[...] = packed[:, :d // 4]
    hpb_ref[...] = packed[:, d // 4:]

    h_hi, h_lo = _split_bf16(h2)
    logits = (lax.dot_general(wrh_ref[...], h_hi, _NT, preferred_element_type=F32)
              + lax.dot_general(wrh_ref[...], h_lo, _NT, preferred_element_type=F32)
              + lax.dot_general(wrl_ref[...], h_hi, _NT, preferred_element_type=F32))
    scores = jax.nn.sigmoid(logits)
    sel = scores + rb_ref[...]
    per = N_EXPERTS // N_GROUPS
    g3 = sel.reshape(N_GROUPS, per, tm)
    it3 = lax.broadcasted_iota(I32, (N_GROUPS, per, tm), 1)
    m1 = jnp.max(g3, axis=1, keepdims=True)
    first = jnp.min(jnp.where(g3 == m1, it3, per), axis=1, keepdims=True)
    m2 = jnp.max(jnp.where(it3 == first, -jnp.inf, g3), axis=1, keepdims=True)
    gscore = (m1 + m2).reshape(N_GROUPS, tm)

    itg = lax.broadcasted_iota(I32, (N_GROUPS, tm), 0)
    gsel = jnp.zeros((N_GROUPS, tm), F32)
    cur = gscore
    for _ in range(TOPK_GROUPS):
        mx = jnp.max(cur, axis=0, keepdims=True)
        fi = jnp.min(jnp.where(cur == mx, itg, N_GROUPS), axis=0, keepdims=True)
        pick = itg == fi
        gsel = jnp.where(pick, 1.0, gsel)
        cur = jnp.where(pick, -jnp.inf, cur)
    emask = jnp.broadcast_to(gsel.reshape(N_GROUPS, 1, tm), (N_GROUPS, per, tm)).reshape(N_EXPERTS, tm) > 0.5

    ite = lax.broadcasted_iota(I32, (N_EXPERTS, tm), 0)
    cur = jnp.where(emask, sel, NEG_INF)
    idx_rows, s_rows = [], []
    for _ in range(TOP_K):
        mx = jnp.max(cur, axis=0, keepdims=True)
        fi = jnp.min(jnp.where(cur == mx, ite, N_EXPERTS), axis=0, keepdims=True)
        pick = ite == fi
        idx_rows.append(fi)
        s_rows.append(jnp.sum(jnp.where(pick, scores, 0.0), axis=0, keepdims=True))
        cur = jnp.where(pick, -jnp.inf, cur)
    top_s = jnp.concatenate(s_rows, axis=0)
    idx_ref[...] = jnp.concatenate(idx_rows, axis=0)
    gate_ref[...] = top_s / jnp.sum(top_s, axis=0, keepdims=True) * ROUTED_SCALE


def _merge(o_a, o_b, x, mod, ga, gb, wo, n2, wr_hi, wr_lo, rb_col, wsg, wsu, wsd, *, tm):
    bsz, s, d = x.shape
    nt = s // tm
    n = bsz * s
    full = lambda a: pl.BlockSpec(a.shape, lambda b, i: (0,) * a.ndim)
    return pl.pallas_call(
        functools.partial(_merge_kernel, d=d),
        out_shape=(jax.ShapeDtypeStruct((bsz, s, d), F32),
                   jax.ShapeDtypeStruct((n, d // 4), U32),
                   jax.ShapeDtypeStruct((n, d // 4), U32),
                   jax.ShapeDtypeStruct((TOP_K, n), I32),
                   jax.ShapeDtypeStruct((TOP_K, n), F32)),
        grid=(bsz, nt),
        in_specs=[pl.BlockSpec((1, tm, d // 2), lambda b, i: (b, i, 0)),
                  pl.BlockSpec((1, tm, d // 2), lambda b, i: (b, i, 0)),
                  pl.BlockSpec((1, tm, d), lambda b, i: (b, i, 0)),
                  full(mod), full(ga), full(gb), full(wo), full(n2), full(wr_hi), full(wr_lo), full(rb_col),
                  full(wsg), full(wsu), full(wsd)],
        out_specs=(pl.BlockSpec((1, tm, d), lambda b, i: (b, i, 0)),
                   pl.BlockSpec((tm, d // 4), lambda b, i: (b * nt + i, 0)),
                   pl.BlockSpec((tm, d // 4), lambda b, i: (b * nt + i, 0)),
                   pl.BlockSpec((TOP_K, tm), lambda b, i: (0, b * nt + i)),
                   pl.BlockSpec((TOP_K, tm), lambda b, i: (0, b * nt + i))),
        compiler_params=_params("arbitrary", "arbitrary"),
        name="merge",
    )(o_a, o_b, x, mod, ga, gb, wo, n2, wr_hi, wr_lo, rb_col, wsg, wsu, wsd)


def _rank_kernel(idx_ref, rank_ref, cnt_ref, carry_ref):
    tm = idx_ref.shape[1]

    @pl.when(pl.program_id(0) == 0)
    def _():
        carry_ref[...] = jnp.zeros_like(carry_ref)

    idx = idx_ref[...]
    ite = lax.broadcasted_iota(I32, (N_EXPERTS, tm), 0)
    before = (lax.broadcasted_iota(I32, (tm, tm), 0) < lax.broadcasted_iota(I32, (tm, tm), 1)).astype(BF16)
    hits = [ite == idx[k:k + 1, :] for k in range(TOP_K)]
    routed = jnp.where(functools.reduce(jnp.logical_or, hits), 1.0, 0.0)
    base = carry_ref[...]
    ahead = jnp.dot(routed.astype(BF16), before, preferred_element_type=F32) + base
    rows = [jnp.sum(jnp.where(hit, ahead, 0.0), axis=0, keepdims=True) for hit in hits]
    rank_ref[...] = jnp.concatenate(rows, axis=0).astype(I32)
    total = base + jnp.sum(routed, axis=1, keepdims=True)
    carry_ref[...] = total
    cnt_ref[...] = total


def _rank(idx, *, tm):
    n = idx.shape[1]
    return pl.pallas_call(
        _rank_kernel,
        out_shape=(jax.ShapeDtypeStruct((TOP_K, n), I32), jax.ShapeDtypeStruct((N_EXPERTS, 1), F32)),
        grid=(n // tm,),
        in_specs=[pl.BlockSpec((TOP_K, tm), lambda i: (0, i))],
        out_specs=(pl.BlockSpec((TOP_K, tm), lambda i: (0, i)), pl.BlockSpec((N_EXPERTS, 1), lambda i: (0, 0))),
        scratch_shapes=[pltpu.VMEM((N_EXPERTS, 1), F32)],
        compiler_params=_params("arbitrary"),
        name="rank",
    )(idx)


def _pos_kernel(idx_ref, rank_ref, pstart_ref, pos_ref):
    tm = idx_ref.shape[1]
    idx = idx_ref[...]
    ite = lax.broadcasted_iota(I32, (N_EXPERTS, tm), 0)
    pstart = pstart_ref[...]
    rows = [jnp.sum(jnp.where(ite == idx[k:k + 1, :], pstart, 0.0), axis=0, keepdims=True) for k in range(TOP_K)]
    pos_ref[...] = jnp.concatenate(rows, axis=0).astype(I32) + rank_ref[...]


def _pos(idx, rank, pstart_col, *, tm):
    n = idx.shape[1]
    return pl.pallas_call(
        _pos_kernel,
        out_shape=jax.ShapeDtypeStruct((TOP_K, n), I32),
        grid=(n // tm,),
        in_specs=[pl.BlockSpec((TOP_K, tm), lambda i: (0, i)),
                  pl.BlockSpec((TOP_K, tm), lambda i: (0, i)),
                  pl.BlockSpec((N_EXPERTS, 1), lambda i: (0, 0))],
        out_specs=pl.BlockSpec((TOP_K, tm), lambda i: (0, i)),
        compiler_params=_params("arbitrary"),
        name="pos",
    )(idx, rank, pstart_col)


UNIT_CHUNKS = (4, 2, 1)
RING_AHEAD = 8
RING_SLOTS = RING_AHEAD + UNIT_CHUNKS[0]


W_SLOTS = 4


def _expert_kernel(ps_ref, cnt_ref, tot_ref, wg_hbm, wu_hbm, wd_hbm, xa_hbm, xb_hbm, ya_hbm, yb_hbm,
                   wg_buf, wu_buf, wd_buf, xa_buf, xb_buf, ya_buf, yb_buf, w_sem, in_sem, out_sem):
    e = pl.program_id(0)
    n_exp = pl.num_programs(0)
    r = EXPERT_ROWS
    ahead = RING_AHEAD
    w_slot = e % W_SLOTS

    def wfetch(x):
        slot = x % W_SLOTS
        return (pltpu.make_async_copy(wg_hbm.at[x], wg_buf.at[slot], w_sem.at[0, slot]),
                pltpu.make_async_copy(wu_hbm.at[x], wu_buf.at[slot], w_sem.at[1, slot]),
                pltpu.make_async_copy(wd_hbm.at[x], wd_buf.at[slot], w_sem.at[2, slot]))
    total = tot_ref[0]
    first = ps_ref[e] // r
    cnt = cnt_ref[e]
    nch = (cnt + r - 1) // r

    def rows_of(g):
        return pl.ds(pl.multiple_of(g * r, r), r)

    def fetch(g):
        slot = g % RING_SLOTS
        return (pltpu.make_async_copy(xa_hbm.at[rows_of(g)], xa_buf.at[slot], in_sem.at[0, slot]),
                pltpu.make_async_copy(xb_hbm.at[rows_of(g)], xb_buf.at[slot], in_sem.at[1, slot]))

    def flush(g):
        slot = g % RING_SLOTS
        return (pltpu.make_async_copy(ya_buf.at[slot], ya_hbm.at[rows_of(g)], out_sem.at[0, slot]),
                pltpu.make_async_copy(yb_buf.at[slot], yb_hbm.at[rows_of(g)], out_sem.at[1, slot]))

    def start(copies):
        for cp in copies:
            cp.start()

    def wait(copies):
        for cp in copies:
            cp.wait()

    @pl.when(e == 0)
    def _():
        for g in range(ahead):
            @pl.when(g < total)
            def _():
                start(fetch(g))
        for x in range(W_SLOTS - 1):
            @pl.when(x < n_exp)
            def _():
                start(wfetch(x))

    wait(wfetch(e))

    @pl.when(e + W_SLOTS - 1 < n_exp)
    def _():
        start(wfetch(e + W_SLOTS - 1))

    @pl.when(nch > 0)
    def _():
        def unit(c0, u):
            for j in range(u):
                g = first + c0 + j
                wait(fetch(g))

                @pl.when(g + ahead < total)
                def _():
                    start(fetch(g + ahead))

                @pl.when(g >= RING_SLOTS)
                def _():
                    wait(flush(g - RING_SLOTS))

            slots = [(first + c0 + j) % RING_SLOTS for j in range(u)]
            packed = jnp.concatenate([jnp.concatenate([xa_buf[sl], xb_buf[sl]], axis=1) for sl in slots], axis=0)
            row = lax.broadcasted_iota(I32, packed.shape, 0)
            packed = jnp.where(row < cnt - c0 * r, packed, jnp.zeros_like(packed))
            x = _unpack_bf16_pairs(packed).astype(BF16)
            gate = jnp.dot(x, wg_buf[w_slot].astype(BF16), preferred_element_type=F32)
            up = jnp.dot(x, wu_buf[w_slot].astype(BF16), preferred_element_type=F32)
            y = jnp.dot((_silu(gate) * up).astype(BF16), wd_buf[w_slot].astype(BF16), preferred_element_type=F32)
            out = _pack_bf16_pairs(y)
            half = out.shape[1] // 2
            for j, sl in enumerate(slots):
                ya_buf[sl] = out[j * r:(j + 1) * r, :half]
                yb_buf[sl] = out[j * r:(j + 1) * r, half:]
                start(flush(first + c0 + j))

        big = UNIT_CHUNKS[0]

        def big_unit(i, carry):
            unit(i * big, big)
            return carry

        lax.fori_loop(0, nch // big, big_unit, 0)
        done = nch // big * big
        for u in UNIT_CHUNKS[1:]:
            @pl.when((nch % (2 * u)) >= u)
            def _():
                unit(done, u)

            done = done + jnp.where((nch % (2 * u)) >= u, u, 0)

    @pl.when(e == pl.num_programs(0) - 1)
    def _():
        for k in range(RING_SLOTS):
            @pl.when(total - 1 - k >= 0)
            def _():
                wait(flush(total - 1 - k))


def _experts(pstart, cnt, total_chunks, xs_a, xs_b, wg, wu, wd):
    rows, hw = xs_a.shape
    r = EXPERT_ROWS
    n_exp, d, f = wg.shape
    hbm = pl.BlockSpec(memory_space=pl.ANY)
    grid_spec = pltpu.PrefetchScalarGridSpec(
        num_scalar_prefetch=3,
        grid=(n_exp,),
        in_specs=[hbm, hbm, hbm, hbm, hbm],
        out_specs=(hbm, hbm),
        scratch_shapes=[pltpu.VMEM((W_SLOTS, d, f), F32), pltpu.VMEM((W_SLOTS, d, f), F32),
                        pltpu.VMEM((W_SLOTS, f, d), F32),
                        pltpu.VMEM((RING_SLOTS, r, hw), U32), pltpu.VMEM((RING_SLOTS, r, hw), U32),
                        pltpu.VMEM((RING_SLOTS, r, hw), U32), pltpu.VMEM((RING_SLOTS, r, hw), U32),
                        pltpu.SemaphoreType.DMA((3, W_SLOTS)),
                        pltpu.SemaphoreType.DMA((2, RING_SLOTS)), pltpu.SemaphoreType.DMA((2, RING_SLOTS))],
    )
    return pl.pallas_call(
        _expert_kernel,
        out_shape=(jax.ShapeDtypeStruct((rows, hw), U32), jax.ShapeDtypeStruct((rows, hw), U32)),
        grid_spec=grid_spec,
        compiler_params=_params("arbitrary"),
        name="experts",
    )(pstart, cnt, total_chunks, wg, wu, wd, xs_a, xs_b)


def _sc_scatter_rows(xs, idx_flat, n_rows):
    n, w = xs[0].shape
    m = idx_flat.shape[1]
    nwin = n // SC_WINDOW
    reps = m // n
    mesh = plsc.VectorSubcoreMesh(core_axis_name="core", subcore_axis_name="subcore")
    out_type = tuple(jax.ShapeDtypeStruct((n_rows, w), x.dtype) for x in xs)

    @functools.partial(pl.kernel, out_type=out_type, mesh=mesh, scratch_types=[])
    def scatter_kernel(*refs):
        x_hbms, i_hbm, o_hbms = refs[:len(xs)], refs[len(xs)], refs[len(xs) + 1:]
        for x_hbm, o_hbm in zip(x_hbms, o_hbms):
            def body(x_vmem, i_vmem, o_hbm=o_hbm):
                pltpu.sync_copy(x_vmem, o_hbm.at[i_vmem.at[0]])

            pltpu.emit_pipeline(
                body,
                grid=(nwin, reps),
                in_specs=[pl.BlockSpec((SC_WINDOW, w), lambda i, k: (i, 0)),
                          pl.BlockSpec((1, SC_WINDOW), lambda i, k: (0, k * nwin + i))],
                out_specs=[],
                core_axis_name=("core", "subcore"),
                dimension_semantics=(pltpu.PARALLEL, pltpu.ARBITRARY),
            )(x_hbm, i_hbm)

    return scatter_kernel(*xs, idx_flat)


def _sc_gather_rows(srcs, idx_flat):
    m = idx_flat.shape[1]
    w = srcs[0].shape[1]
    mesh = plsc.VectorSubcoreMesh(core_axis_name="core", subcore_axis_name="subcore")
    out_type = tuple(jax.ShapeDtypeStruct((m, w), src.dtype) for src in srcs)

    @functools.partial(pl.kernel, out_type=out_type, mesh=mesh)
    def gather_kernel(*refs):
        x_hbms, i_hbm, o_hbms = refs[:len(srcs)], refs[len(srcs)], refs[len(srcs) + 1:]
        for x_hbm, o_hbm in zip(x_hbms, o_hbms):
            def body(i_vmem, o_vmem, x_hbm=x_hbm):
                pltpu.sync_copy(x_hbm.at[i_vmem.at[0]], o_vmem)

            pltpu.emit_pipeline(
                body,
                grid=(m // SC_WINDOW,),
                in_specs=[pl.BlockSpec((1, SC_WINDOW), lambda i: (0, i))],
                out_specs=[pl.BlockSpec((SC_WINDOW, w), lambda i: (i, 0))],
                core_axis_name=("core", "subcore"),
                dimension_semantics=(pltpu.PARALLEL,),
            )(i_hbm, o_hbm)

    return gather_kernel(*srcs, idx_flat)


def _finish_kernel(gate_ref, xres_ref, mod_ref, ya_ref, yb_ref, o_ref, *, d, tiles_per_batch):
    tm = gate_ref.shape[1]
    b = pl.program_id(0) // tiles_per_batch
    gates = gate_ref[...].T
    acc = jnp.zeros((tm, d), F32)
    for k in range(TOP_K):
        packed = jnp.concatenate([ya_ref[k], yb_ref[k]], axis=1)
        acc = acc + _unpack_bf16_pairs(packed) * gates[:, k:k + 1]
    o_ref[...] = xres_ref[...] + mod_ref[pl.ds(b, 1), 5 * d:6 * d] * acc


def _finish(gates, xres, mod, yg_a, yg_b, *, tm, tiles_per_batch):
    n, d = xres.shape
    return pl.pallas_call(
        functools.partial(_finish_kernel, d=d, tiles_per_batch=tiles_per_batch),
        out_shape=jax.ShapeDtypeStruct((n, d), F32),
        grid=(n // tm,),
        in_specs=[pl.BlockSpec((TOP_K, tm), lambda i: (0, i)),
                  pl.BlockSpec((tm, d), lambda i: (i, 0)),
                  pl.BlockSpec(mod.shape, lambda i: (0, 0)),
                  pl.BlockSpec((TOP_K, tm, d // 4), lambda i: (0, i, 0)),
                  pl.BlockSpec((TOP_K, tm, d // 4), lambda i: (0, i, 0))],
        out_specs=pl.BlockSpec((tm, d), lambda i: (i, 0)),
        compiler_params=_params("arbitrary"),
        name="finish",
    )(gates, xres, mod, yg_a, yg_b)


_QA_HEAD_ORDER = (0, 4, 1, 5, 2, 6, 3, 7)


def _reorder_qa_heads(a, axis):
    return jnp.concatenate([lax.slice_in_dim(a, h * HEAD_DIM, (h + 1) * HEAD_DIM, axis=axis)
                            for h in _QA_HEAD_ORDER], axis=axis)


def _rope_tables(s):
    quarter = HEAD_DIM // 4
    t = jnp.arange(s)
    row = (t // GRID_W).astype(F32)
    col = (t % GRID_W).astype(F32)
    freqs = ROPE_BASE ** (-jnp.arange(quarter, dtype=F32) / quarter)
    ar = row[:, None] * freqs[None, :]
    ac = col[:, None] * freqs[None, :]
    cos = jnp.concatenate([jnp.cos(ar), jnp.cos(ar), jnp.cos(ac), jnp.cos(ac)], axis=1)
    sin = jnp.concatenate([-jnp.sin(ar), jnp.sin(ar), -jnp.sin(ac), jnp.sin(ac)], axis=1)
    return jnp.tile(cos, (1, 2)), jnp.tile(sin, (1, 2))


def kernel(x, c, ctx, c_ctx, w_ada, b_ada, norm1, norm2, w_in, q_norm_a, k_norm_a, q_norm_b, k_norm_b, sink_a,
           rpb_b, out_norm_a, out_norm_b, w_out, w_router, router_bias, we_gate, we_up, we_down, ws_gate, ws_up,
           ws_down):
    assert w_ada.shape[0] == 1, "single-layer block"
    bsz, s, d = x.shape
    lc = ctx.shape[1]
    n = bsz * s
    rows = s // GRID_W
    assert d == 1024 and bsz + 1 <= 8 and rows >= NA_KROWS and lc % LANES == 0
    assert s % TOKEN_TILE == 0 and n % SC_WINDOW == 0
    assert (s // A_WINDOW) % A_BLOCKS_PER_STEP == 0 and (s // (NA_QROWS * GRID_W)) % NA_GROUPS_PER_STEP == 0

    c8 = jnp.concatenate([c, c_ctx[None, :], jnp.zeros((8 - bsz - 1, d), F32)], axis=0)
    mod = _ada(c8, w_ada[0], b_ada[0][None, :])

    w = w_in[0]
    aw, akw, bw = A_Q_HEADS * HEAD_DIM, A_KV_HEADS * HEAD_DIM, B_HEADS * HEAD_DIM
    cuts = np.cumsum([0, aw, akw, akw, bw, bw, bw])
    qa_w, ka_w, va_w, qb_w, kb_w, vb_w = [w[:, cuts[i]:cuts[i + 1]] for i in range(6)]
    qa_w = _reorder_qa_heads(qa_w, 1)
    w_ext = jnp.concatenate([qa_w, qb_w, kb_w, vb_w, ka_w, va_w], axis=1).astype(BF16)
    scale = HEAD_DIM ** -0.5 * LOG2E
    pair = lambda g: jnp.tile(g, 2)
    gains = jnp.stack([pair(q_norm_a[0]) * scale, pair(q_norm_b[0]) * scale, pair(k_norm_b[0]), pair(k_norm_a[0])]
                      + [jnp.zeros((LANES,), F32)] * 4)
    bd = jnp.asarray(np.kron(np.eye(2), np.full((HEAD_DIM, HEAD_DIM), 1.0 / HEAD_DIM)), BF16)
    cos_t, sin_t = _rope_tables(s)
    n1 = norm1[0][None, :]
    zq = _inproj(x, mod, n1, w_ext, gains, cos_t, sin_t, bd, mod_row=None, tm=TOKEN_TILE)
    zc = _inproj(ctx, mod, n1, w_ext, gains, jnp.ones((lc, LANES), F32), jnp.zeros((lc, LANES), F32), bd,
                 mod_row=bsz, tm=lc)

    sink = sink_a[0].astype(F32)
    sink_rep = jnp.concatenate([jnp.broadcast_to(sink[h] * LOG2E, (A_WINDOW, LANES)) for h in _QA_HEAD_ORDER])
    o_a = _attn_a(zq, zc, sink_rep, _attn_a_mask(A_WINDOW))
    o_b = _attn_b(zq, zc, _na_table(rpb_b[0], rows))

    ga = _reorder_qa_heads(out_norm_a[0], 0)[None, :]
    gb = out_norm_b[0][None, :]
    wo = jnp.concatenate([_reorder_qa_heads(w_out[0][:aw], 0), w_out[0][aw:]], axis=0).astype(BF16)
    wr_hi, wr_lo = _split_bf16(w_router[0].T)
    xres, hp_a, hp_b, idx, gates = _merge(o_a, o_b, x, mod, ga, gb, wo, norm2[0][None, :], wr_hi, wr_lo,
                                  router_bias[0][:, None], ws_gate[0].astype(BF16), ws_up[0].astype(BF16),
                                  ws_down[0].astype(BF16), tm=TOKEN_TILE)

    rank, counts = _rank(idx, tm=TOKEN_TILE)
    t = EXPERT_ROWS
    cnt = counts[:, 0].astype(I32)
    padded = (cnt + t - 1) // t * t
    pends = jnp.cumsum(padded)
    pstart = pends - padded
    n_rows = n * TOP_K + N_EXPERTS * t
    pos = _pos(idx, rank, pstart.astype(F32)[:, None], tm=min(POS_TILE, n))

    pos_flat = pos.reshape(1, TOP_K * n)
    xs_a, xs_b = _sc_scatter_rows((hp_a, hp_b), pos_flat, n_rows)
    ys_a, ys_b = _experts(pstart, cnt, pends[-1:] // t, xs_a, xs_b, we_gate[0], we_up[0], we_down[0])
    yg_a, yg_b = [y.reshape(TOP_K, n, d // 4) for y in _sc_gather_rows((ys_a, ys_b), pos_flat)]
    out = _finish(gates, xres.reshape(n, d), mod, yg_a, yg_b, tm=TOKEN_TILE, tiles_per_batch=s // TOKEN_TILE)
    return out.reshape(bsz, s, d)
```
